```python
import math
import jax, jax.numpy as jnp
from jax import lax
import numpy as np

D_MODEL = 1024
BATCH = 16
SEQ = 2048
DEPTH = 1

E_A = D_MODEL
CONV_K = 31
N_GROUPS_A = 8
E_B = D_MODEL
CHUNK = 128
N_HEADS_B = 8
HEAD_DIM_B = E_B // N_HEADS_B
N_BRANCH = 2
COLS_A_IN = 2 * E_A
COLS_A_GATE = E_A
COLS_B_UV = 2 * E_B
COLS_B_GATE = E_B
COLS_MERGE = N_BRANCH * D_MODEL
D_IN = COLS_A_IN + COLS_A_GATE + COLS_B_UV + COLS_B_GATE + COLS_MERGE
DEEPNORM_ALPHA = (2.0 * DEPTH) ** 0.25
DEEPNORM_BETA = (8.0 * DEPTH) ** -0.25
LN_EPS = 1e-5

kernel_name = "hybrid_conformer_conv_chunked_gmlp_gated_deepnorm"


def _layer_norm(x, g, b):
    xf = x.astype(jnp.float32)
    mu = jnp.mean(xf, axis=-1, keepdims=True)
    var = jnp.mean(jnp.square(xf - mu), axis=-1, keepdims=True)
    y = (xf - mu) * lax.rsqrt(var + LN_EPS)
    return (y * g.astype(jnp.float32) + b.astype(jnp.float32)).astype(x.dtype)


def _group_norm_channels(x, g, b, n_groups):
    shp = x.shape
    xf = x.astype(jnp.float32).reshape(shp[:-1] + (n_groups, shp[-1] // n_groups))
    mu = jnp.mean(xf, axis=-1, keepdims=True)
    var = jnp.mean(jnp.square(xf - mu), axis=-1, keepdims=True)
    y = ((xf - mu) * lax.rsqrt(var + LN_EPS)).reshape(shp)
    return (y * g.astype(jnp.float32) + b.astype(jnp.float32)).astype(x.dtype)


def _conformer_conv_branch(a_in, a_gate, conv_w, conv_b, gn_g, gn_b, w_pa):
    val, gate = jnp.split(a_in, 2, axis=-1)
    h = val * jax.nn.sigmoid(gate)
    h = lax.conv_general_dilated(
        h, conv_w[:, None, :].astype(h.dtype),
        window_strides=(1,), padding=[(CONV_K - 1, 0)],
        dimension_numbers=("NWC", "WIO", "NWC"),
        feature_group_count=E_A) + conv_b
    h = _group_norm_channels(h, gn_g, gn_b, N_GROUPS_A)
    h = jax.nn.silu(h) * jax.nn.silu(a_gate)
    return jnp.einsum("bse,ed->bsd", h, w_pa)


def _chunked_gmlp_branch(b_uv, b_gate, ln_v_g, ln_v_b, w_spatial, b_spatial, w_pb):
    bsz, seq = b_uv.shape[0], b_uv.shape[1]
    z = jax.nn.gelu(b_uv)
    u, v = jnp.split(z, 2, axis=-1)
    v = _layer_norm(v, ln_v_g, ln_v_b)
    v = v.reshape(bsz, seq // CHUNK, CHUNK, N_HEADS_B, HEAD_DIM_B)
    causal = jnp.tril(jnp.ones((CHUNK, CHUNK), dtype=bool))
    ws = jnp.where(causal[None], w_spatial, jnp.zeros((), w_spatial.dtype))
    v_mix = jnp.einsum("hts,bnshd->bnthd", ws, v) + b_spatial.T[None, None, :, :, None]
    s = u * v_mix.reshape(bsz, seq, E_B)
    s = s * jax.nn.silu(b_gate)
    return jnp.einsum("bse,ed->bsd", s, w_pb)


def _fwd_setup_inputs(seed: int = 0) -> dict:
    key = jax.random.key(seed)
    ks = jax.random.split(key, 20)
    f32 = jnp.float32
    nrm = lambda k, shp, sc: jax.random.normal(k, shp, f32) * sc
    return {
        "x": jax.random.normal(ks[0], (BATCH, SEQ, D_MODEL), f32),
        "w_in": nrm(ks[1], (D_MODEL, D_IN), D_MODEL ** -0.5),
        "b_in": nrm(ks[2], (D_IN,), 0.02),
        "conv_w": nrm(ks[3], (CONV_K, E_A), CONV_K ** -0.5),
        "conv_b": nrm(ks[4], (E_A,), 0.02),
        "gn_g": 1.0 + nrm(ks[5], (E_A,), 0.02),
        "gn_b": nrm(ks[6], (E_A,), 0.02),
        "ln_v_g": 1.0 + nrm(ks[7], (E_B,), 0.02),
        "ln_v_b": nrm(ks[8], (E_B,), 0.02),
        "w_spatial": nrm(ks[9], (N_HEADS_B, CHUNK, CHUNK), CHUNK ** -0.5),
        "b_spatial": 1.0 + nrm(ks[10], (N_HEADS_B, CHUNK), 0.1),
        "w_pa": nrm(ks[11], (E_A, D_MODEL), E_A ** -0.5),
        "w_pb": nrm(ks[12], (E_B, D_MODEL), E_B ** -0.5),
        "w_o": nrm(ks[13], (D_MODEL, D_MODEL), DEEPNORM_BETA * D_MODEL ** -0.5),
        "b_o": nrm(ks[14], (D_MODEL,), 0.02),
        "ln_out_g": 1.0 + nrm(ks[15], (D_MODEL,), 0.02),
        "ln_out_b": nrm(ks[16], (D_MODEL,), 0.02),
    }


def _fwd_reference(x, w_in, b_in, conv_w, conv_b, gn_g, gn_b, ln_v_g, ln_v_b,
              w_spatial, b_spatial, w_pa, w_pb, w_o, b_o, ln_out_g, ln_out_b):
    splits = np.cumsum([COLS_A_IN, COLS_A_GATE, COLS_B_UV, COLS_B_GATE]).tolist()
    for _ in range(DEPTH):
        proj = jnp.einsum("bsd,dk->bsk", x, w_in) + b_in
        a_in, a_gate, b_uv, b_gate, merge = jnp.split(proj, splits, axis=-1)
        y_a = _conformer_conv_branch(a_in, a_gate, conv_w, conv_b, gn_g, gn_b, w_pa)
        y_b = _chunked_gmlp_branch(b_uv, b_gate, ln_v_g, ln_v_b, w_spatial, b_spatial, w_pb)
        g_a, g_b = jnp.split(jax.nn.sigmoid(merge), 2, axis=-1)
        mixed = g_a * y_a + g_b * y_b
        sub = jnp.einsum("bsd,de->bse", mixed, w_o) + b_o
        x = _layer_norm(DEEPNORM_ALPHA * x + sub, ln_out_g, ln_out_b)
    return x


import jax as _jax
import jax.numpy as _jnp

TWIN_FORMAT = 'train_step'
FWD_PARAMS = ['x', 'w_in', 'b_in', 'conv_w', 'conv_b', 'gn_g', 'gn_b', 'ln_v_g', 'ln_v_b', 'w_spatial', 'b_spatial', 'w_pa', 'w_pb', 'w_o', 'b_o', 'ln_out_g', 'ln_out_b']
TWIN_WEIGHTS = ['w_in', 'b_in', 'conv_w', 'conv_b', 'gn_g', 'gn_b', 'ln_v_g', 'ln_v_b', 'w_spatial', 'b_spatial', 'w_pa', 'w_pb', 'w_o', 'b_o', 'ln_out_g', 'ln_out_b']
TWIN_DIFF_INPUT = 'x'
TWIN_INPUTS = ['x', 'w_in', 'b_in', 'conv_w', 'conv_b', 'gn_g', 'gn_b', 'ln_v_g', 'ln_v_b', 'w_spatial', 'b_spatial', 'w_pa', 'w_pb', 'w_o', 'b_o', 'ln_out_g', 'ln_out_b', 'loss_target', 'm_w_in', 'm_b_in', 'm_conv_w', 'm_conv_b', 'm_gn_g', 'm_gn_b', 'm_ln_v_g', 'm_ln_v_b', 'm_w_spatial', 'm_b_spatial', 'm_w_pa', 'm_w_pb', 'm_w_o', 'm_b_o', 'm_ln_out_g', 'm_ln_out_b', 'v_w_in', 'v_b_in', 'v_conv_w', 'v_conv_b', 'v_gn_g', 'v_gn_b', 'v_ln_v_g', 'v_ln_v_b', 'v_w_spatial', 'v_b_spatial', 'v_w_pa', 'v_w_pb', 'v_w_o', 'v_b_o', 'v_ln_out_g', 'v_ln_out_b']
TWIN_OUTPUTS = ['loss', 'grad_x', 'grad_w_in', 'grad_b_in', 'grad_conv_w', 'grad_conv_b', 'grad_gn_g', 'grad_gn_b', 'grad_ln_v_g', 'grad_ln_v_b', 'grad_w_spatial', 'grad_b_spatial', 'grad_w_pa', 'grad_w_pb', 'grad_w_o', 'grad_b_o', 'grad_ln_out_g', 'grad_ln_out_b', 'delta_w_in', 'delta_b_in', 'delta_conv_w', 'delta_conv_b', 'delta_gn_g', 'delta_gn_b', 'delta_ln_v_g', 'delta_ln_v_b', 'delta_w_spatial', 'delta_b_spatial', 'delta_w_pa', 'delta_w_pb', 'delta_w_o', 'delta_b_o', 'delta_ln_out_g', 'delta_ln_out_b', 'new_m_w_in', 'new_m_b_in', 'new_m_conv_w', 'new_m_conv_b', 'new_m_gn_g', 'new_m_gn_b', 'new_m_ln_v_g', 'new_m_ln_v_b', 'new_m_w_spatial', 'new_m_b_spatial', 'new_m_w_pa', 'new_m_w_pb', 'new_m_w_o', 'new_m_b_o', 'new_m_ln_out_g', 'new_m_ln_out_b', 'new_v_w_in', 'new_v_b_in', 'new_v_conv_w', 'new_v_conv_b', 'new_v_gn_g', 'new_v_gn_b', 'new_v_ln_v_g', 'new_v_ln_v_b', 'new_v_w_spatial', 'new_v_b_spatial', 'new_v_w_pa', 'new_v_w_pb', 'new_v_w_o', 'new_v_b_o', 'new_v_ln_out_g', 'new_v_ln_out_b']
TWIN_LEAF_KINDS = {'loss': 'loss', 'grad_x': 'grad_x', 'grad_w_in': 'grad_w', 'grad_b_in': 'grad_w', 'grad_conv_w': 'grad_w', 'grad_conv_b': 'grad_w', 'grad_gn_g': 'grad_w', 'grad_gn_b': 'grad_w', 'grad_ln_v_g': 'grad_w', 'grad_ln_v_b': 'grad_w', 'grad_w_spatial': 'grad_w', 'grad_b_spatial': 'grad_w', 'grad_w_pa': 'grad_w', 'grad_w_pb': 'grad_w', 'grad_w_o': 'grad_w', 'grad_b_o': 'grad_w', 'grad_ln_out_g': 'grad_w', 'grad_ln_out_b': 'grad_w', 'delta_w_in': 'delta_w', 'delta_b_in': 'delta_w', 'delta_conv_w': 'delta_w', 'delta_conv_b': 'delta_w', 'delta_gn_g': 'delta_w', 'delta_gn_b': 'delta_w', 'delta_ln_v_g': 'delta_w', 'delta_ln_v_b': 'delta_w', 'delta_w_spatial': 'delta_w', 'delta_b_spatial': 'delta_w', 'delta_w_pa': 'delta_w', 'delta_w_pb': 'delta_w', 'delta_w_o': 'delta_w', 'delta_b_o': 'delta_w', 'delta_ln_out_g': 'delta_w', 'delta_ln_out_b': 'delta_w', 'new_m_w_in': 'new_m', 'new_m_b_in': 'new_m', 'new_m_conv_w': 'new_m', 'new_m_conv_b': 'new_m', 'new_m_gn_g': 'new_m', 'new_m_gn_b': 'new_m', 'new_m_ln_v_g': 'new_m', 'new_m_ln_v_b': 'new_m', 'new_m_w_spatial': 'new_m', 'new_m_b_spatial': 'new_m', 'new_m_w_pa': 'new_m', 'new_m_w_pb': 'new_m', 'new_m_w_o': 'new_m', 'new_m_b_o': 'new_m', 'new_m_ln_out_g': 'new_m', 'new_m_ln_out_b': 'new_m', 'new_v_w_in': 'new_v', 'new_v_b_in': 'new_v', 'new_v_conv_w': 'new_v', 'new_v_conv_b': 'new_v', 'new_v_gn_g': 'new_v', 'new_v_gn_b': 'new_v', 'new_v_ln_v_g': 'new_v', 'new_v_ln_v_b': 'new_v', 'new_v_w_spatial': 'new_v', 'new_v_b_spatial': 'new_v', 'new_v_w_pa': 'new_v', 'new_v_w_pb': 'new_v', 'new_v_w_o': 'new_v', 'new_v_b_o': 'new_v', 'new_v_ln_out_g': 'new_v', 'new_v_ln_out_b': 'new_v'}


def _forward(args):
    return _fwd_reference(*[args[k] for k in FWD_PARAMS])


def _output_shape():
    out = _jax.eval_shape(lambda: _forward(_fwd_setup_inputs(0)))
    return out.shape, out.dtype

N_MICROBATCH = 1
ADAM_LR = 0.001
ADAM_B1 = 0.9
ADAM_B2 = 0.999
ADAM_EPS = 1e-08
ADAM_WD = 0.01
ADAM_STEP = 10
PER_EXAMPLE_BATCH_AXIS = {'x': 0, 'loss_target': 0}
SHARED_INPUTS = []
_WEIGHT_DTYPES = {'w_in': _jnp.float32, 'b_in': _jnp.float32, 'conv_w': _jnp.float32, 'conv_b': _jnp.float32, 'gn_g': _jnp.float32, 'gn_b': _jnp.float32, 'ln_v_g': _jnp.float32, 'ln_v_b': _jnp.float32, 'w_spatial': _jnp.float32, 'b_spatial': _jnp.float32, 'w_pa': _jnp.float32, 'w_pb': _jnp.float32, 'w_o': _jnp.float32, 'b_o': _jnp.float32, 'ln_out_g': _jnp.float32, 'ln_out_b': _jnp.float32}
MOMENT_SCALE = {'w_in': 1.626697e-02, 'b_in': 1.759115e-02, 'conv_w': 1.760647e-02, 'conv_b': 3.790056e-02, 'gn_g': 2.076188e-02, 'gn_b': 1.783221e-02, 'ln_v_g': 1.337434e-02, 'ln_v_b': 1.363159e-02, 'w_spatial': 1.302067e-02, 'b_spatial': 1.876406e-02, 'w_pa': 1.726179e-02, 'w_pb': 2.319670e-02, 'w_o': 4.839993e-02, 'b_o': 3.569497e-01, 'ln_out_g': 3.195327e+01, 'ln_out_b': 7.584872e-01}


def _to_microbatches(a, axis):
    t = _jnp.moveaxis(a, axis, 0)
    t = t.reshape((N_MICROBATCH, t.shape[0] // N_MICROBATCH) + t.shape[1:])
    return _jnp.moveaxis(t, 1, axis + 1)


def setup_inputs(seed: int = 0) -> dict:
    inp = _fwd_setup_inputs(seed)
    key = _jax.random.fold_in(_jax.random.key(seed), 7919)
    shape, _ = _output_shape()
    out = dict(inp)
    out["loss_target"] = _jax.random.normal(_jax.random.fold_in(key, 0), shape, _jnp.float32)
    for i, name in enumerate(TWIN_WEIGHTS):
        w = inp[name].astype(_jnp.float32)
        if MOMENT_SCALE is None:
            s = _jnp.sqrt(_jnp.mean(_jnp.square(w)) + 1e-30)
        else:
            s = MOMENT_SCALE[name]
        km, kv = _jax.random.split(_jax.random.fold_in(key, i + 1))
        out[name] = w
        out["m_" + name] = s * _jax.random.normal(km, w.shape, _jnp.float32)
        out["v_" + name] = (s * s) * _jax.random.uniform(kv, w.shape, _jnp.float32, 0.5, 1.5)
    if N_MICROBATCH > 1:
        for name, axis in PER_EXAMPLE_BATCH_AXIS.items():
            out[name] = _to_microbatches(out[name], axis)
    return {'x': out['x'], 'w_in': out['w_in'], 'b_in': out['b_in'], 'conv_w': out['conv_w'], 'conv_b': out['conv_b'], 'gn_g': out['gn_g'], 'gn_b': out['gn_b'], 'ln_v_g': out['ln_v_g'], 'ln_v_b': out['ln_v_b'], 'w_spatial': out['w_spatial'], 'b_spatial': out['b_spatial'], 'w_pa': out['w_pa'], 'w_pb': out['w_pb'], 'w_o': out['w_o'], 'b_o': out['b_o'], 'ln_out_g': out['ln_out_g'], 'ln_out_b': out['ln_out_b'], 'loss_target': out['loss_target'], 'm_w_in': out['m_w_in'], 'm_b_in': out['m_b_in'], 'm_conv_w': out['m_conv_w'], 'm_conv_b': out['m_conv_b'], 'm_gn_g': out['m_gn_g'], 'm_gn_b': out['m_gn_b'], 'm_ln_v_g': out['m_ln_v_g'], 'm_ln_v_b': out['m_ln_v_b'], 'm_w_spatial': out['m_w_spatial'], 'm_b_spatial': out['m_b_spatial'], 'm_w_pa': out['m_w_pa'], 'm_w_pb': out['m_w_pb'], 'm_w_o': out['m_w_o'], 'm_b_o': out['m_b_o'], 'm_ln_out_g': out['m_ln_out_g'], 'm_ln_out_b': out['m_ln_out_b'], 'v_w_in': out['v_w_in'], 'v_b_in': out['v_b_in'], 'v_conv_w': out['v_conv_w'], 'v_conv_b': out['v_conv_b'], 'v_gn_g': out['v_gn_g'], 'v_gn_b': out['v_gn_b'], 'v_ln_v_g': out['v_ln_v_g'], 'v_ln_v_b': out['v_ln_v_b'], 'v_w_spatial': out['v_w_spatial'], 'v_b_spatial': out['v_b_spatial'], 'v_w_pa': out['v_w_pa'], 'v_w_pb': out['v_w_pb'], 'v_w_o': out['v_w_o'], 'v_b_o': out['v_b_o'], 'v_ln_out_g': out['v_ln_out_g'], 'v_ln_out_b': out['v_ln_out_b']}


def _loss(weights, diff, rest, loss_target):
    with _jax.named_scope("forward"):
        args = {**rest, TWIN_DIFF_INPUT: diff, **{k: w.astype(_WEIGHT_DTYPES[k]) for k, w in weights.items()}}
        y = _forward(args)
    with _jax.named_scope("loss_head"):
        err = _jnp.square(y.astype(_jnp.float32) - loss_target)
        return 0.5 * _jnp.sum(_jnp.mean(err, axis=-1)) if err.ndim else 0.5 * err


def _adamw(w, g, m, v):
    m = ADAM_B1 * m + (1.0 - ADAM_B1) * g
    v = ADAM_B2 * v + (1.0 - ADAM_B2) * _jnp.square(g)
    m_hat = m / (1.0 - ADAM_B1 ** ADAM_STEP)
    v_hat = v / (1.0 - ADAM_B2 ** ADAM_STEP)
    delta = -ADAM_LR * (m_hat / (_jnp.sqrt(v_hat) + ADAM_EPS) + ADAM_WD * w)
    return delta, m, v


def reference(x, w_in, b_in, conv_w, conv_b, gn_g, gn_b, ln_v_g, ln_v_b, w_spatial, b_spatial, w_pa, w_pb, w_o, b_o, ln_out_g, ln_out_b, loss_target, m_w_in, m_b_in, m_conv_w, m_conv_b, m_gn_g, m_gn_b, m_ln_v_g, m_ln_v_b, m_w_spatial, m_b_spatial, m_w_pa, m_w_pb, m_w_o, m_b_o, m_ln_out_g, m_ln_out_b, v_w_in, v_b_in, v_conv_w, v_conv_b, v_gn_g, v_gn_b, v_ln_v_g, v_ln_v_b, v_w_spatial, v_b_spatial, v_w_pa, v_w_pb, v_w_o, v_b_o, v_ln_out_g, v_ln_out_b):
    given = dict(x=x, w_in=w_in, b_in=b_in, conv_w=conv_w, conv_b=conv_b, gn_g=gn_g, gn_b=gn_b, ln_v_g=ln_v_g, ln_v_b=ln_v_b, w_spatial=w_spatial, b_spatial=b_spatial, w_pa=w_pa, w_pb=w_pb, w_o=w_o, b_o=b_o, ln_out_g=ln_out_g, ln_out_b=ln_out_b, loss_target=loss_target, m_w_in=m_w_in, m_b_in=m_b_in, m_conv_w=m_conv_w, m_conv_b=m_conv_b, m_gn_g=m_gn_g, m_gn_b=m_gn_b, m_ln_v_g=m_ln_v_g, m_ln_v_b=m_ln_v_b, m_w_spatial=m_w_spatial, m_b_spatial=m_b_spatial, m_w_pa=m_w_pa, m_w_pb=m_w_pb, m_w_o=m_w_o, m_b_o=m_b_o, m_ln_out_g=m_ln_out_g, m_ln_out_b=m_ln_out_b, v_w_in=v_w_in, v_b_in=v_b_in, v_conv_w=v_conv_w, v_conv_b=v_conv_b, v_gn_g=v_gn_g, v_gn_b=v_gn_b, v_ln_v_g=v_ln_v_g, v_ln_v_b=v_ln_v_b, v_w_spatial=v_w_spatial, v_b_spatial=v_b_spatial, v_w_pa=v_w_pa, v_w_pb=v_w_pb, v_w_o=v_w_o, v_b_o=v_b_o, v_ln_out_g=v_ln_out_g, v_ln_out_b=v_ln_out_b)
    weights = {n: given[n] for n in TWIN_WEIGHTS}
    shared = {n: given[n] for n in SHARED_INPUTS}
    per_example = {n: given[n] for n in ['x']}
    grad_fn = _jax.value_and_grad(_loss, argnums=(0, 1))

    def one_microbatch(ex, loss_target):
        ex = dict(ex)
        diff = ex.pop(TWIN_DIFF_INPUT)
        return grad_fn(weights, diff, {**shared, **ex}, loss_target)

    if N_MICROBATCH == 1:
        loss, (grad_w, grad_x) = one_microbatch(per_example, given["loss_target"])
    else:
        def body(carry, xs):
            loss_sum, grad_sum = carry
            l_k, (gw_k, gx_k) = one_microbatch(xs[0], xs[1])
            with _jax.named_scope("update"):
                return (loss_sum + l_k, _jax.tree.map(_jnp.add, grad_sum, gw_k)), gx_k

        init = (_jnp.zeros((), _jnp.float32), _jax.tree.map(_jnp.zeros_like, weights))
        (loss, grad_w), grad_x = _jax.lax.scan(body, init, (per_example, given["loss_target"]))
    with _jax.named_scope("update"):
        delta_w, new_m, new_v = {}, {}, {}
        for n in TWIN_WEIGHTS:
            delta_w[n], new_m[n], new_v[n] = _adamw(weights[n], grad_w[n], given["m_" + n], given["v_" + n])
    return (loss, grad_x, *[grad_w[n] for n in TWIN_WEIGHTS], *[delta_w[n] for n in TWIN_WEIGHTS],
            *[new_m[n] for n in TWIN_WEIGHTS], *[new_v[n] for n in TWIN_WEIGHTS])
```

```python
import functools

import jax
import jax.numpy as jnp
from jax import lax
from jax.experimental import pallas as pl
from jax.experimental.pallas import tpu as pltpu

F32 = jnp.float32
BF16 = jnp.bfloat16
SDS = jax.ShapeDtypeStruct

N_DEV = 8
LANES = 128
SUBLANES = 8
CONV_K = 31
HALO = 32
LN_EPS = 1e-5
DEEPNORM_ALPHA = 2.0 ** 0.25
ADAM_LR, ADAM_B1, ADAM_B2, ADAM_EPS, ADAM_WD, ADAM_STEP = 0.001, 0.9, 0.999, 1e-08, 0.01, 10
GELU_C = 0.7978845608028654
GELU_A = 0.044715
VMEM_LIMIT = 56 * 1024 * 1024
MESH = pl.DeviceIdType.MESH
ANY = pl.BlockSpec(memory_space=pl.ANY)
VMEM = pl.BlockSpec(memory_space=pltpu.VMEM)


def _params(n_grid=0):
    sem = ("arbitrary",) * n_grid if n_grid else None
    return pltpu.CompilerParams(dimension_semantics=sem, vmem_limit_bytes=VMEM_LIMIT)


def _tile(n, pref, mult):
    t = min(n, pref)
    while n % t or t % mult:
        t -= 1
    return t


def _colsum(v):
    return jnp.sum(v, axis=0, keepdims=True)


def _sigmoid(v):
    return jax.nn.sigmoid(v)


def _silu_and_grad(v):
    s = _sigmoid(v)
    return v * s, s * (1.0 + v * (1.0 - s))


def _gelu_and_grad(v):
    inner = GELU_C * (v + GELU_A * v * v * v)
    th = jnp.tanh(inner)
    val = 0.5 * v * (1.0 + th)
    grad = 0.5 * (1.0 + th) + 0.5 * v * (1.0 - th * th) * GELU_C * (1.0 + 3.0 * GELU_A * v * v)
    return val, grad


def _tril_mask():
    r = lax.broadcasted_iota(jnp.int32, (LANES, LANES), 0)
    c = lax.broadcasted_iota(jnp.int32, (LANES, LANES), 1)
    return c <= r


def _dot(a, b):
    return jnp.dot(a, b, preferred_element_type=F32)


def _dot_tb(a, b):
    return lax.dot_general(a, b, (((1,), (1,)), ((), ())), preferred_element_type=F32)


def _dot_ta(a, b):
    return lax.dot_general(a, b, (((0,), (0,)), ((), ())), preferred_element_type=F32)


def _mesh_pos():
    return lax.axis_index("x"), lax.axis_index("y"), lax.axis_index("c")


def _peers():
    x, y, c = _mesh_pos()
    out = []
    for k in range(1, N_DEV):
        px = 1 - x if k & 4 else x
        py = 1 - y if k & 2 else y
        pc = 1 - c if k & 1 else c
        out.append(((px, py, pc), 4 * px + 2 * py + pc))
    return out


def _all_gather_weights(w_in, w_pa, w_pb, w_o, conv_w):
    d_model, r8 = w_in.shape[0], w_pa.shape[0]
    kc, dc = conv_w.shape

    def body(win_ref, wpa_ref, wpb_ref, wo_ref, cw_ref, oin_ref, op_ref, ocw_ref,
             st_in, st_p, send_sems, recv_sems, local_sems):
        x, y, c = _mesh_pos()
        me = (x, y, c)
        sibling = (x, y, 1 - c)
        chips = [(1 - x, y), (x, 1 - y), (1 - x, 1 - y)]
        st_in[...] = win_ref[...].astype(BF16)
        st_p[0] = wpa_ref[...].astype(BF16)
        st_p[1] = wpb_ref[...].astype(BF16)
        st_p[2] = wo_ref[...].astype(BF16)
        arrays = [(st_in, oin_ref), (st_p, op_ref), (cw_ref, ocw_ref)]

        def slot(o, pos):
            return o.at[4 * pos[0] + 2 * pos[1] + pos[2]]

        def copy(a, k, block, to, src=None):
            o = arrays[a][1]
            return pltpu.make_async_remote_copy(
                src_ref=slot(o, block) if src is None else src, dst_ref=slot(o, block),
                send_sem=send_sems.at[a, k], recv_sem=recv_sems.at[a, k],
                device_id=to, device_id_type=MESH)

        mine, first, passed = [], [], []
        for a, (st, o) in enumerate(arrays):
            cp = pltpu.make_async_copy(st, slot(o, me), local_sems.at[a])
            cp.start()
            mine.append(cp)
            first.append(copy(a, 0, me, sibling, src=st))
            for j, chip in enumerate(chips):
                first.append(copy(a, 1 + j, me, (*chip, c), src=st))
        for cp in first:
            cp.start()
        for j, chip in enumerate(chips):
            for a in range(len(arrays)):
                copy(a, 1 + j, (*chip, c), me).wait_recv()
                cp = copy(a, 4 + j, (*chip, c), sibling)
                cp.start()
                passed.append(cp)
        for a in range(len(arrays)):
            copy(a, 0, sibling, me).wait_recv()
            for j, chip in enumerate(chips):
                copy(a, 4 + j, (*chip, 1 - c), me).wait_recv()
        for cp in first + passed:
            cp.wait_send()
        for cp in mine:
            cp.wait()

    return pl.pallas_call(
        body, name="all_gather_weights",
        out_shape=(SDS((N_DEV, d_model, d_model), BF16), SDS((N_DEV, 3, r8, d_model), BF16),
                   SDS((N_DEV, kc, dc), F32)),
        in_specs=[VMEM] * 5, out_specs=(ANY, ANY, ANY),
        scratch_shapes=[pltpu.VMEM((d_model, d_model), BF16), pltpu.VMEM((3, r8, d_model), BF16),
                        pltpu.SemaphoreType.DMA((3, 7)), pltpu.SemaphoreType.DMA((3, 7)),
                        pltpu.SemaphoreType.DMA((3,))],
        compiler_params=_params(),
    )(w_in, w_pa, w_pb, w_o, conv_w)


def _reduce_scatter_big(gin_bf, gin_f32, gp_bf, gp_f32):
    d_model = gin_bf.shape[1]
    r8 = d_model // N_DEV
    rows = _tile(d_model, 32, 16)
    prow = _tile(r8, 32, 16)

    def body(gin_bf_ref, gin_f32_ref, gpa_bf, gpb_bf, gpo_bf, gpa_f32, gpb_f32, gpo_f32,
             oin_ref, opa_ref, opb_ref, opo_ref, rbuf_in, rbuf_p, own_in, own_p,
             send_sems, recv_sems, local_sems):
        x, y, c = _mesh_pos()
        me = 4 * x + 2 * y + c
        peers = _peers()
        gp_bf_refs = [gpa_bf, gpb_bf, gpo_bf]
        gp_f32_refs = [gpa_f32, gpb_f32, gpo_f32]
        sends = []
        for k, (pos, blk) in enumerate(peers):
            sends.append(pltpu.make_async_remote_copy(
                src_ref=gin_bf_ref.at[blk], dst_ref=rbuf_in.at[k],
                send_sem=send_sems.at[0, k], recv_sem=recv_sems.at[0, k],
                device_id=pos, device_id_type=MESH))
            for a in range(3):
                sends.append(pltpu.make_async_remote_copy(
                    src_ref=gp_bf_refs[a].at[pl.ds(pl.multiple_of(blk * r8, 16), r8), :],
                    dst_ref=rbuf_p.at[k, a],
                    send_sem=send_sems.at[1 + a, k], recv_sem=recv_sems.at[1 + a, k],
                    device_id=pos, device_id_type=MESH))
        for cp in sends:
            cp.start()
        locs = [pltpu.make_async_copy(gin_f32_ref.at[me], own_in, local_sems.at[0])]
        for a in range(3):
            locs.append(pltpu.make_async_copy(
                gp_f32_refs[a].at[pl.ds(pl.multiple_of(me * r8, 8), r8), :], own_p.at[a],
                local_sems.at[1 + a]))
        for cp in locs:
            cp.start()
        for cp in locs:
            cp.wait()
        for cp in sends:
            cp.wait_recv()

        def sum_in(i, carry):
            r = pl.ds(pl.multiple_of(i * rows, rows), rows)
            acc = own_in[r, :]
            for k in range(N_DEV - 1):
                acc = acc + rbuf_in[k, r, :].astype(F32)
            oin_ref[r, :] = acc
            return carry

        lax.fori_loop(0, d_model // rows, sum_in, 0)
        for a, o in enumerate([opa_ref, opb_ref, opo_ref]):
            for i in range(r8 // prow):
                r = pl.ds(i * prow, prow)
                acc = own_p[a, r, :]
                for k in range(N_DEV - 1):
                    acc = acc + rbuf_p[k, a, r, :].astype(F32)
                o[r, :] = acc
        for cp in sends:
            cp.wait_send()

    return pl.pallas_call(
        body, name="reduce_scatter_big",
        out_shape=(SDS((d_model, d_model), F32),) + (SDS((r8, d_model), F32),) * 3,
        in_specs=[ANY] * 8, out_specs=(VMEM,) * 4,
        scratch_shapes=[pltpu.VMEM((N_DEV - 1, d_model, d_model), BF16),
                        pltpu.VMEM((N_DEV - 1, 3, r8, d_model), BF16),
                        pltpu.VMEM((d_model, d_model), F32), pltpu.VMEM((3, r8, d_model), F32),
                        pltpu.SemaphoreType.DMA((4, 7)), pltpu.SemaphoreType.DMA((4, 7)),
                        pltpu.SemaphoreType.DMA((4,))],
        compiler_params=_params(),
    )(gin_bf, gin_f32, *gp_bf, *gp_f32)


def _all_reduce_small(part):
    n_rows = part.shape[0]
    rsl = n_rows // N_DEV

    def body(p_ref, o_ref, rbuf, red, send1, recv1, send2, recv2):
        x, y, c = _mesh_pos()
        me = 4 * x + 2 * y + c
        peers = _peers()

        def rows_of(blk):
            return pl.ds(pl.multiple_of(blk * rsl, SUBLANES), rsl)

        scatter = [pltpu.make_async_remote_copy(
            src_ref=p_ref.at[rows_of(blk), :], dst_ref=rbuf.at[k],
            send_sem=send1.at[k], recv_sem=recv1.at[k], device_id=pos, device_id_type=MESH)
            for k, (pos, blk) in enumerate(peers)]
        for cp in scatter:
            cp.start()
        for cp in scatter:
            cp.wait_recv()
        acc = p_ref[rows_of(me), :]
        for k in range(N_DEV - 1):
            acc = acc + rbuf[k]
        red[...] = acc
        o_ref[rows_of(me), :] = acc
        gather = [pltpu.make_async_remote_copy(
            src_ref=red, dst_ref=o_ref.at[rows_of(me), :],
            send_sem=send2.at[k], recv_sem=recv2.at[k], device_id=pos, device_id_type=MESH)
            for k, (pos, blk) in enumerate(peers)]
        for cp in gather:
            cp.start()
        for k, (pos, blk) in enumerate(peers):
            pltpu.make_async_remote_copy(
                src_ref=red, dst_ref=o_ref.at[rows_of(blk), :],
                send_sem=send2.at[k], recv_sem=recv2.at[k], device_id=pos,
                device_id_type=MESH).wait_recv()
        for cp in scatter + gather:
            cp.wait_send()

    return pl.pallas_call(
        body, name="all_reduce_small", out_shape=SDS(part.shape, F32),
        in_specs=[VMEM], out_specs=VMEM,
        scratch_shapes=[pltpu.VMEM((N_DEV - 1, rsl, LANES), F32), pltpu.VMEM((rsl, LANES), F32)]
        + [pltpu.SemaphoreType.DMA((N_DEV - 1,))] * 4,
        compiler_params=_params(),
    )(part)


def _proj(x2, w_all, b_in3):
    n_tok, d_model = x2.shape
    tm = _tile(n_tok, 1024, 16)

    def body(x_ref, w_ref, b_ref, o_ref, xb_ref):
        xb = x_ref[...].astype(BF16)

        @pl.when(pl.program_id(1) == 0)
        def _():
            xb_ref[...] = xb

        o_ref[...] = _dot(xb, w_ref[...]) + b_ref[...]

    return pl.pallas_call(
        body, name="proj", grid=(n_tok // tm, N_DEV),
        in_specs=[pl.BlockSpec((tm, d_model), lambda i, j: (i, 0)),
                  pl.BlockSpec((None, d_model, d_model), lambda i, j: (j, 0, 0)),
                  pl.BlockSpec((None, 1, d_model), lambda i, j: (j, 0, 0))],
        out_specs=[pl.BlockSpec((tm, d_model), lambda i, j: (i, j)),
                   pl.BlockSpec((tm, d_model), lambda i, j: (i, 0))],
        out_shape=[SDS((n_tok, N_DEV * d_model), F32), SDS((n_tok, d_model), BF16)],
        compiler_params=_params(2),
    )(x2, w_all, b_in3)


def _conv_rows(ta):
    return _tile(ta, 64, SUBLANES)


def _branch_a_fwd(proj, conv_w_full, conv_b, gn_g, gn_b, seq):
    n_tok = proj.shape[0]
    d_model = conv_b.shape[1]
    ta = _tile(seq, 256, HALO)
    per_seq = seq // ta
    rc = _conv_rows(ta)

    def body(av_ref, ag_ref, gt_ref, avh_ref, agh_ref, cw_ref, cb_ref, gg_ref, gb_ref,
             h3_ref, h1_ref, ext):
        keep = jnp.where(pl.program_id(0) % per_seq == 0, 0.0, 1.0)

        def group(g, carry):
            sl = pl.ds(pl.multiple_of(g * LANES, LANES), LANES)
            ext[0:HALO, :] = avh_ref[:, sl] * _sigmoid(agh_ref[:, sl]) * keep
            ext[HALO:HALO + ta, :] = av_ref[:, sl] * _sigmoid(ag_ref[:, sl])
            for r0 in range(0, ta, rc):
                acc = jnp.broadcast_to(cb_ref[:, sl], (rc, LANES))
                for k in range(CONV_K):
                    acc = acc + ext[pl.ds(r0 + HALO - (CONV_K - 1) + k, rc), :] * cw_ref[k:k + 1, sl]
                h1_ref[pl.ds(r0, rc), sl] = acc
            h1 = h1_ref[:, sl]
            mu = jnp.mean(h1, axis=-1, keepdims=True)
            dlt = h1 - mu
            var = jnp.mean(dlt * dlt, axis=-1, keepdims=True)
            h2 = dlt * lax.rsqrt(var + LN_EPS) * gg_ref[:, sl] + gb_ref[:, sl]
            gate = gt_ref[:, sl]
            h3_ref[:, sl] = (h2 * _sigmoid(h2) * gate * _sigmoid(gate)).astype(BF16)
            return carry

        lax.fori_loop(0, d_model // LANES, group, 0)

    blk = lambda j: pl.BlockSpec((ta, d_model), lambda i: (i, j))
    halo = lambda j: pl.BlockSpec((HALO, d_model), lambda i: (jnp.maximum(i * (ta // HALO) - 1, 0), j))
    row = pl.BlockSpec((1, d_model), lambda i: (0, 0))
    return pl.pallas_call(
        body, name="branch_a_fwd", grid=(n_tok // ta,),
        in_specs=[blk(0), blk(1), blk(2), halo(0), halo(1),
                  pl.BlockSpec((HALO, d_model), lambda i: (0, 0)), row, row, row],
        out_specs=[pl.BlockSpec((ta, d_model), lambda i: (i, 0))] * 2,
        out_shape=[SDS((n_tok, d_model), BF16), SDS((n_tok, d_model), F32)],
        scratch_shapes=[pltpu.VMEM((HALO + ta, LANES), F32)],
        compiler_params=_params(1),
    )(proj, proj, proj, proj, proj, conv_w_full, conv_b, gn_g, gn_b)


def _branch_b_fwd(proj, ln_g, ln_b, w_spatial, b_spatial_t, seq):
    n_tok = proj.shape[0]
    d_model = ln_g.shape[1]
    n_heads = d_model // LANES
    tb = _tile(seq, 256, LANES)

    def body(u_ref, v_ref, bg_ref, lg_ref, lb_ref, ws_ref, bs_ref, s_ref, vn_buf):
        v, _ = _gelu_and_grad(v_ref[...])
        mu = jnp.mean(v, axis=-1, keepdims=True)
        dlt = v - mu
        var = jnp.mean(dlt * dlt, axis=-1, keepdims=True)
        vn_buf[...] = (dlt * lax.rsqrt(var + LN_EPS) * lg_ref[...] + lb_ref[...]).astype(BF16)
        tril = _tril_mask()
        for h in range(n_heads):
            cols = slice(h * LANES, (h + 1) * LANES)
            w_h = jnp.where(tril, ws_ref[h], 0.0).astype(BF16)
            bias = bs_ref[:, h:h + 1]
            for ch in range(tb // LANES):
                rows = slice(ch * LANES, (ch + 1) * LANES)
                mix = _dot(w_h, vn_buf[rows, cols]) + bias
                u, _ = _gelu_and_grad(u_ref[rows, cols])
                gate = bg_ref[rows, cols]
                s_ref[rows, cols] = (u * mix * gate * _sigmoid(gate)).astype(BF16)

    blk = lambda j: pl.BlockSpec((tb, d_model), lambda i: (i, j))
    row = pl.BlockSpec((1, d_model), lambda i: (0, 0))
    return pl.pallas_call(
        body, name="branch_b_fwd", grid=(n_tok // tb,),
        in_specs=[blk(3), blk(4), blk(5), row, row,
                  pl.BlockSpec((n_heads, LANES, LANES), lambda i: (0, 0, 0)),
                  pl.BlockSpec((LANES, n_heads), lambda i: (0, 0))],
        out_specs=pl.BlockSpec((tb, d_model), lambda i: (i, 0)),
        out_shape=SDS((n_tok, d_model), BF16),
        scratch_shapes=[pltpu.VMEM((tb, d_model), BF16)],
        compiler_params=_params(1),
    )(proj, proj, proj, ln_g, ln_b, w_spatial, b_spatial_t)


MID_ROWS = 8


def _mid(h3, s, proj, x2, target, w_pa, w_pb, w_o, b_o, lo_g, lo_b):
    n_tok, d_model = x2.shape
    tm = _tile(n_tok, 256, 16)

    def body(h3_ref, s_ref, ma_ref, mb_ref, x_ref, t_ref, wpa_ref, wpb_ref, wo_ref, bo_ref,
             lg_ref, lb_ref, dproj_ref, dh3_ref, ds_ref, dr_ref, drb_ref, mixed_ref, dya_ref,
             dyb_ref, vec_ref):
        @pl.when(pl.program_id(0) == 0)
        def _():
            vec_ref[...] = jnp.zeros_like(vec_ref)

        ya = _dot(h3_ref[...], wpa_ref[...])
        yb = _dot(s_ref[...], wpb_ref[...])
        ga = _sigmoid(ma_ref[...])
        gb = _sigmoid(mb_ref[...])
        mixed = (ga * ya + gb * yb).astype(BF16)
        mixed_ref[...] = mixed
        r = DEEPNORM_ALPHA * x_ref[...] + _dot(mixed, wo_ref[...]) + bo_ref[...]
        mu = jnp.mean(r, axis=-1, keepdims=True)
        dlt = r - mu
        rstd = lax.rsqrt(jnp.mean(dlt * dlt, axis=-1, keepdims=True) + LN_EPS)
        rhat = dlt * rstd
        diff = rhat * lg_ref[...] + lb_ref[...] - t_ref[...]
        dy = diff * (1.0 / d_model)
        vec_ref[0:1, :] += _colsum(dy * rhat)
        vec_ref[1:2, :] += _colsum(dy)
        vec_ref[5:6, :] += _colsum(diff * diff)
        drh = dy * lg_ref[...]
        dr = rstd * (drh - jnp.mean(drh, axis=-1, keepdims=True)
                     - rhat * jnp.mean(drh * rhat, axis=-1, keepdims=True))
        vec_ref[2:3, :] += _colsum(dr)
        dr_ref[...] = dr
        drb = dr.astype(BF16)
        drb_ref[...] = drb
        dmixed = _dot_tb(drb, wo_ref[...])
        dma = dmixed * ya * ga * (1.0 - ga)
        dmb = dmixed * yb * gb * (1.0 - gb)
        vec_ref[3:4, :] += _colsum(dma)
        vec_ref[4:5, :] += _colsum(dmb)
        dproj_ref[:, 0:d_model] = dma.astype(BF16)
        dproj_ref[:, d_model:2 * d_model] = dmb.astype(BF16)
        dya = (dmixed * ga).astype(BF16)
        dyb = (dmixed * gb).astype(BF16)
        dya_ref[...] = dya
        dyb_ref[...] = dyb
        dh3_ref[...] = _dot_tb(dya, wpa_ref[...])
        ds_ref[...] = _dot_tb(dyb, wpb_ref[...])

    tile = pl.BlockSpec((tm, d_model), lambda i: (i, 0))
    full = pl.BlockSpec((d_model, d_model), lambda i: (0, 0))
    row = pl.BlockSpec((1, d_model), lambda i: (0, 0))
    bf = SDS((n_tok, d_model), BF16)
    f32 = SDS((n_tok, d_model), F32)
    return pl.pallas_call(
        body, name="mid", grid=(n_tok // tm,),
        in_specs=[tile, tile, pl.BlockSpec((tm, d_model), lambda i: (i, 6)),
                  pl.BlockSpec((tm, d_model), lambda i: (i, 7)), tile, tile, full, full, full,
                  row, row, row],
        out_specs=[pl.BlockSpec((tm, 2 * d_model), lambda i: (i, 3)), tile, tile, tile, tile, tile,
                   tile, tile, pl.BlockSpec((MID_ROWS, d_model), lambda i: (0, 0))],
        out_shape=[SDS((n_tok, N_DEV * d_model), BF16), f32, f32, f32, bf, bf, bf, bf,
                   SDS((MID_ROWS, d_model), F32)],
        compiler_params=_params(1),
    )(h3, s, proj, proj, x2, target, w_pa, w_pb, w_o, b_o, lo_g, lo_b)


B_ROWS = 8


def _branch_b_bwd(dproj, proj, d_s, ln_g, ln_b, w_spatial, b_spatial_t, seq):
    n_tok = proj.shape[0]
    d_model = ln_g.shape[1]
    n_heads = d_model // LANES
    tb = _tile(seq, 256, LANES)

    def body(dproj_in, u_ref, v_ref, bg_ref, ds_ref, lg_ref, lb_ref, ws_ref, bs_ref,
             dproj_ref, vec_ref, dws_ref, dbs_ref, vn_buf, dv_buf):
        del dproj_in

        @pl.when(pl.program_id(0) == 0)
        def _():
            vec_ref[...] = jnp.zeros_like(vec_ref)
            dws_ref[...] = jnp.zeros_like(dws_ref)
            dbs_ref[...] = jnp.zeros_like(dbs_ref)

        v, dgelu_v = _gelu_and_grad(v_ref[...])
        mu = jnp.mean(v, axis=-1, keepdims=True)
        dlt = v - mu
        rstd = lax.rsqrt(jnp.mean(dlt * dlt, axis=-1, keepdims=True) + LN_EPS)
        vhat = dlt * rstd
        vn_buf[...] = (vhat * lg_ref[...] + lb_ref[...]).astype(BF16)
        tril = _tril_mask()
        for h in range(n_heads):
            cols = slice(h * LANES, (h + 1) * LANES)
            w_h = jnp.where(tril, ws_ref[h], 0.0).astype(BF16)
            bias = bs_ref[:, h:h + 1]
            for ch in range(tb // LANES):
                rows = slice(ch * LANES, (ch + 1) * LANES)
                vn = vn_buf[rows, cols]
                mix = _dot(w_h, vn) + bias
                u, dgelu_u = _gelu_and_grad(u_ref[rows, cols])
                sg, dsilu = _silu_and_grad(bg_ref[rows, cols])
                dsv = ds_ref[rows, cols]
                du = dsv * mix * sg * dgelu_u
                dbg = dsv * u * mix * dsilu
                dmix = dsv * u * sg
                dmix_bf = dmix.astype(BF16)
                dproj_ref[rows, cols] = du.astype(BF16)
                dproj_ref[rows, 2 * d_model + h * LANES:2 * d_model + (h + 1) * LANES] = dbg.astype(BF16)
                vec_ref[0:1, cols] += _colsum(du)
                vec_ref[2:3, cols] += _colsum(dbg)
                dbs_ref[:, h:h + 1] += jnp.sum(dmix, axis=1, keepdims=True)
                dws_ref[h] += jnp.where(tril, _dot_tb(dmix_bf, vn), 0.0)
                dv_buf[rows, cols] = _dot_ta(w_h, dmix_bf)
        dvn = dv_buf[...]
        vec_ref[3:4, :] += _colsum(dvn * vhat)
        vec_ref[4:5, :] += _colsum(dvn)
        dvh = dvn * lg_ref[...]
        dv = rstd * (dvh - jnp.mean(dvh, axis=-1, keepdims=True)
                     - vhat * jnp.mean(dvh * vhat, axis=-1, keepdims=True)) * dgelu_v
        vec_ref[1:2, :] += _colsum(dv)
        dproj_ref[:, d_model:2 * d_model] = dv.astype(BF16)

    blk = lambda j: pl.BlockSpec((tb, d_model), lambda i: (i, j))
    row = pl.BlockSpec((1, d_model), lambda i: (0, 0))
    return pl.pallas_call(
        body, name="branch_b_bwd", grid=(n_tok // tb,),
        in_specs=[ANY, blk(3), blk(4), blk(5), pl.BlockSpec((tb, d_model), lambda i: (i, 0)), row, row,
                  pl.BlockSpec((n_heads, LANES, LANES), lambda i: (0, 0, 0)),
                  pl.BlockSpec((LANES, n_heads), lambda i: (0, 0))],
        out_specs=[pl.BlockSpec((tb, 3 * d_model), lambda i: (i, 1)),
                   pl.BlockSpec((B_ROWS, d_model), lambda i: (0, 0)),
                   pl.BlockSpec((n_heads, LANES, LANES), lambda i: (0, 0, 0)),
                   pl.BlockSpec((LANES, n_heads), lambda i: (0, 0))],
        out_shape=[SDS(dproj.shape, BF16), SDS((B_ROWS, d_model), F32),
                   SDS((n_heads, LANES, LANES), F32), SDS((LANES, n_heads), F32)],
        scratch_shapes=[pltpu.VMEM((tb, d_model), BF16), pltpu.VMEM((tb, d_model), F32)],
        input_output_aliases={0: 0},
        compiler_params=_params(1),
    )(dproj, proj, proj, proj, d_s, ln_g, ln_b, w_spatial, b_spatial_t)


A1_ROWS = 8


def _branch_a_bwd_norm(dproj, proj, h1, d_h3, gn_g, gn_b, seq):
    n_tok = proj.shape[0]
    d_model = gn_g.shape[1]
    ta = _tile(seq, 256, 16)

    def body(dproj_in, gt_ref, h1_ref, dh3_ref, gg_ref, gb_ref, dproj_ref, dh1_ref, vec_ref):
        del dproj_in

        @pl.when(pl.program_id(0) == 0)
        def _():
            vec_ref[...] = jnp.zeros_like(vec_ref)

        def group(g, carry):
            sl = pl.ds(pl.multiple_of(g * LANES, LANES), LANES)
            h1 = h1_ref[:, sl]
            mu = jnp.mean(h1, axis=-1, keepdims=True)
            dlt = h1 - mu
            rstd = lax.rsqrt(jnp.mean(dlt * dlt, axis=-1, keepdims=True) + LN_EPS)
            nrm = dlt * rstd
            sw, dsw = _silu_and_grad(nrm * gg_ref[:, sl] + gb_ref[:, sl])
            sg, dsg = _silu_and_grad(gt_ref[:, sl])
            dh3 = dh3_ref[:, sl]
            dgate = dh3 * sw * dsg
            dproj_ref[:, sl] = dgate.astype(BF16)
            vec_ref[0:1, sl] += _colsum(dgate)
            dh2 = dh3 * sg * dsw
            vec_ref[1:2, sl] += _colsum(dh2 * nrm)
            vec_ref[2:3, sl] += _colsum(dh2)
            dn = dh2 * gg_ref[:, sl]
            dh1 = rstd * (dn - jnp.mean(dn, axis=-1, keepdims=True)
                          - nrm * jnp.mean(dn * nrm, axis=-1, keepdims=True))
            vec_ref[3:4, sl] += _colsum(dh1)
            dh1_ref[:, sl] = dh1
            return carry

        lax.fori_loop(0, d_model // LANES, group, 0)

    tile = pl.BlockSpec((ta, d_model), lambda i: (i, 0))
    row = pl.BlockSpec((1, d_model), lambda i: (0, 0))
    return pl.pallas_call(
        body, name="branch_a_bwd_norm", grid=(n_tok // ta,),
        in_specs=[ANY, pl.BlockSpec((ta, d_model), lambda i: (i, 2)), tile, tile, row, row],
        out_specs=[pl.BlockSpec((ta, d_model), lambda i: (i, 2)), tile,
                   pl.BlockSpec((A1_ROWS, d_model), lambda i: (0, 0))],
        out_shape=[SDS(dproj.shape, BF16), SDS((n_tok, d_model), F32), SDS((A1_ROWS, d_model), F32)],
        input_output_aliases={0: 0},
        compiler_params=_params(1),
    )(dproj, proj, h1, d_h3, gn_g, gn_b)


A2_ROWS = 8


def _branch_a_bwd_conv(dproj, proj, d_h1, conv_w_full, seq):
    n_tok = proj.shape[0]
    d_model = conv_w_full.shape[1]
    n_groups = d_model // LANES
    ta = _tile(seq, 256, HALO)
    per_seq = seq // ta
    rc = _conv_rows(ta)
    last_halo = n_tok // HALO - 1

    def body(dproj_in, av_ref, ag_ref, avh_ref, agh_ref, dh1_ref, dh1h_ref, cw_ref,
             dproj_ref, vec_ref, dcw_ref, ext_h0, ext_d):
        del dproj_in
        i = pl.program_id(0)

        @pl.when(i == 0)
        def _():
            vec_ref[...] = jnp.zeros_like(vec_ref)
            dcw_ref[...] = jnp.zeros_like(dcw_ref)

        keep_past = jnp.where(i % per_seq == 0, 0.0, 1.0)
        keep_next = jnp.where(i % per_seq == per_seq - 1, 0.0, 1.0)

        def group(g, carry):
            sl = pl.ds(pl.multiple_of(g * LANES, LANES), LANES)
            av = av_ref[:, sl]
            sig = _sigmoid(ag_ref[:, sl])
            ext_h0[0:HALO, :] = avh_ref[:, sl] * _sigmoid(agh_ref[:, sl]) * keep_past
            ext_h0[HALO:HALO + ta, :] = av * sig
            ext_d[0:ta, :] = dh1_ref[:, sl]
            ext_d[ta:ta + HALO, :] = dh1h_ref[:, sl] * keep_next
            for r0 in range(0, ta, rc):
                dh1 = ext_d[pl.ds(r0, rc), :]
                acc = jnp.zeros((rc, LANES), F32)
                for k in range(CONV_K):
                    acc = acc + ext_d[pl.ds(r0 + CONV_K - 1 - k, rc), :] * cw_ref[k:k + 1, sl]
                    prod = dh1 * ext_h0[pl.ds(r0 + HALO - (CONV_K - 1) + k, rc), :]
                    dcw_ref[g, k] += jnp.sum(prod.reshape(rc // SUBLANES, SUBLANES, LANES), axis=0)
                rows = pl.ds(r0, rc)
                sig_r = sig[r0:r0 + rc]
                dav = acc * sig_r
                dag = acc * av[r0:r0 + rc] * sig_r * (1.0 - sig_r)
                dproj_ref[rows, sl] = dav.astype(BF16)
                dproj_ref[rows, pl.ds(pl.multiple_of(d_model + g * LANES, LANES), LANES)] = dag.astype(BF16)
                vec_ref[0:1, sl] += _colsum(dav)
                vec_ref[1:2, sl] += _colsum(dag)
            return carry

        lax.fori_loop(0, n_groups, group, 0)

    blk = lambda j: pl.BlockSpec((ta, d_model), lambda i: (i, j))
    halo = lambda j: pl.BlockSpec((HALO, d_model), lambda i: (jnp.maximum(i * (ta // HALO) - 1, 0), j))
    return pl.pallas_call(
        body, name="branch_a_bwd_conv", grid=(n_tok // ta,),
        in_specs=[ANY, blk(0), blk(1), halo(0), halo(1), pl.BlockSpec((ta, d_model), lambda i: (i, 0)),
                  pl.BlockSpec((HALO, d_model), lambda i: (jnp.minimum((i + 1) * (ta // HALO), last_halo), 0)),
                  pl.BlockSpec((HALO, d_model), lambda i: (0, 0))],
        out_specs=[pl.BlockSpec((ta, 2 * d_model), lambda i: (i, 0)),
                   pl.BlockSpec((A2_ROWS, d_model), lambda i: (0, 0)),
                   pl.BlockSpec((n_groups, HALO, SUBLANES, LANES), lambda i: (0, 0, 0, 0))],
        out_shape=[SDS(dproj.shape, BF16), SDS((A2_ROWS, d_model), F32),
                   SDS((n_groups, HALO, SUBLANES, LANES), F32)],
        scratch_shapes=[pltpu.VMEM((HALO + ta, LANES), F32), pltpu.VMEM((ta + HALO, LANES), F32)],
        input_output_aliases={0: 0},
        compiler_params=_params(1),
    )(dproj, proj, proj, proj, proj, d_h1, d_h1, conv_w_full)


def _grad_x(dproj, w_all, dr):
    n_tok, d_model = dr.shape
    tm = _tile(n_tok, 1024, 16)

    def body(dp_ref, w_ref, dr_ref, o_ref):
        part = _dot_tb(dp_ref[...], w_ref[...])

        @pl.when(pl.program_id(1) == 0)
        def _():
            o_ref[...] = DEEPNORM_ALPHA * dr_ref[...] + part

        @pl.when(pl.program_id(1) != 0)
        def _():
            o_ref[...] += part

    return pl.pallas_call(
        body, name="grad_x", grid=(n_tok // tm, N_DEV),
        in_specs=[pl.BlockSpec((tm, d_model), lambda i, j: (i, j)),
                  pl.BlockSpec((None, d_model, d_model), lambda i, j: (j, 0, 0)),
                  pl.BlockSpec((tm, d_model), lambda i, j: (i, 0))],
        out_specs=pl.BlockSpec((tm, d_model), lambda i, j: (i, 0)),
        out_shape=SDS((n_tok, d_model), F32),
        compiler_params=_params(2),
    )(dproj, w_all, dr)


def _weight_grad(lhs, rhs, name):
    n_tok, d_in = lhs.shape
    d_out = d_in
    n_blk = rhs.shape[1] // d_out
    tk = _tile(n_tok, 512, 16)
    n_k = n_tok // tk

    def body(a_ref, g_ref, o_ref, ob_ref):
        part = _dot_ta(a_ref[...], g_ref[...])

        @pl.when(pl.program_id(1) == 0)
        def _():
            o_ref[...] = part

        @pl.when(pl.program_id(1) != 0)
        def _():
            o_ref[...] += part

        @pl.when(pl.program_id(1) == n_k - 1)
        def _():
            ob_ref[...] = o_ref[...].astype(BF16)

    out_blk = pl.BlockSpec((None, d_in, d_out), lambda j, i: (j, 0, 0))
    return pl.pallas_call(
        body, name=name, grid=(n_blk, n_k),
        in_specs=[pl.BlockSpec((tk, d_in), lambda j, i: (i, 0)),
                  pl.BlockSpec((tk, d_out), lambda j, i: (i, j))],
        out_specs=[out_blk, out_blk],
        out_shape=[SDS((n_blk, d_in, d_out), F32), SDS((n_blk, d_in, d_out), BF16)],
        compiler_params=_params(2),
    )(lhs, rhs)


def _adamw_math(w, g, m, v):
    m = ADAM_B1 * m + (1.0 - ADAM_B1) * g
    v = ADAM_B2 * v + (1.0 - ADAM_B2) * (g * g)
    m_hat = m / (1.0 - ADAM_B1 ** ADAM_STEP)
    v_hat = v / (1.0 - ADAM_B2 ** ADAM_STEP)
    delta = -ADAM_LR * (m_hat / (jnp.sqrt(v_hat) + ADAM_EPS) + ADAM_WD * w)
    return delta, m, v


def _adamw_tiled(w, g, m, v):
    n_rows, n_cols = w.shape
    tr = _tile(n_rows, 256, SUBLANES)

    def body(w_ref, g_ref, m_ref, v_ref, d_ref, mo_ref, vo_ref):
        d_ref[...], mo_ref[...], vo_ref[...] = _adamw_math(w_ref[...], g_ref[...], m_ref[...], v_ref[...])

    tile = pl.BlockSpec((tr, n_cols), lambda i: (i, 0))
    return pl.pallas_call(
        body, name="adamw_w_in", grid=(n_rows // tr,), in_specs=[tile] * 4, out_specs=[tile] * 3,
        out_shape=[SDS(w.shape, F32)] * 3, compiler_params=_params(1),
    )(w, g, m, v)


def _adamw_many(groups):
    n = len(groups)

    def body(*refs):
        ins, outs = refs[:4 * n], refs[4 * n:]
        for p in range(n):
            w_ref, g_ref, m_ref, v_ref = ins[4 * p:4 * p + 4]
            d, m, v = _adamw_math(w_ref[...], g_ref[...], m_ref[...], v_ref[...])
            outs[3 * p][...] = d
            outs[3 * p + 1][...] = m
            outs[3 * p + 2][...] = v

    flat = [a for grp in groups for a in grp]
    out_shape = [SDS(grp[0].shape, F32) for grp in groups for _ in range(3)]
    res = pl.pallas_call(
        body, name="adamw_small", in_specs=[VMEM] * (4 * n), out_specs=[VMEM] * (3 * n),
        out_shape=out_shape, compiler_params=_params(),
    )(*flat)
    return [tuple(res[3 * p:3 * p + 3]) for p in range(n)]


def _as_rows(a):
    return a.reshape(-1, LANES)


def kernel(x, w_in, b_in, conv_w, conv_b, gn_g, gn_b, ln_v_g, ln_v_b, w_spatial, b_spatial, w_pa, w_pb, w_o, b_o, ln_out_g, ln_out_b, loss_target, m_w_in, m_b_in, m_conv_w, m_conv_b, m_gn_g, m_gn_b, m_ln_v_g, m_ln_v_b, m_w_spatial, m_b_spatial, m_w_pa, m_w_pb, m_w_o, m_b_o, m_ln_out_g, m_ln_out_b, v_w_in, v_b_in, v_conv_w, v_conv_b, v_gn_g, v_gn_b, v_ln_v_g, v_ln_v_b, v_w_spatial, v_b_spatial, v_w_pa, v_w_pb, v_w_o, v_b_o, v_ln_out_g, v_ln_out_b):
    n_batch, seq, d_model = x.shape
    n_tok = n_batch * seq
    n_heads = d_model // LANES
    dc = conv_w.shape[1]
    me = 4 * lax.axis_index("x") + 2 * lax.axis_index("y") + lax.axis_index("c")
    row = lambda a: a.reshape(1, d_model)

    x2 = x.reshape(n_tok, d_model)
    target2 = loss_target.reshape(n_tok, d_model)
    b_spatial_t = b_spatial.T

    w_all, wp_all, cw_all = _all_gather_weights(w_in, w_pa, w_pb, w_o, conv_w)
    wp_full = [wp_all[:, a].reshape(d_model, d_model) for a in range(3)]
    conv_w_full = jnp.pad(cw_all.transpose(1, 0, 2).reshape(CONV_K, d_model), ((0, HALO - CONV_K), (0, 0)))

    proj, x_bf = _proj(x2, w_all, b_in.reshape(N_DEV, 1, d_model))
    h3, h1 = _branch_a_fwd(proj, conv_w_full, row(conv_b), row(gn_g), row(gn_b), seq)
    s = _branch_b_fwd(proj, row(ln_v_g), row(ln_v_b), w_spatial, b_spatial_t, seq)

    dproj, d_h3, d_s, dr, dr_bf, mixed, d_ya, d_yb, vec_mid = _mid(
        h3, s, proj, x2, target2, *wp_full, row(b_o), row(ln_out_g), row(ln_out_b))
    loss = lax.psum(jnp.sum(vec_mid[5]) * (0.5 / d_model), ("x", "y", "c"))

    dproj, vec_b, d_ws, d_bs_t = _branch_b_bwd(dproj, proj, d_s, row(ln_v_g), row(ln_v_b), w_spatial, b_spatial_t, seq)
    dproj, d_h1, vec_a1 = _branch_a_bwd_norm(dproj, proj, h1, d_h3, row(gn_g), row(gn_b), seq)
    dproj, vec_a2, d_cw8 = _branch_a_bwd_conv(dproj, proj, d_h1, conv_w_full, seq)

    grad_x = _grad_x(dproj, w_all, dr).reshape(x.shape)
    gin_f32, gin_bf = _weight_grad(x_bf, dproj, "grad_w_in")
    gp = [_weight_grad(lhs, rhs, name) for lhs, rhs, name in
          ((h3, d_ya, "grad_w_pa"), (s, d_yb, "grad_w_pb"), (mixed, dr_bf, "grad_w_o"))]
    g_w_in, g_w_pa, g_w_pb, g_w_o = _reduce_scatter_big(
        gin_bf, gin_f32, [g[1][0] for g in gp], [g[0][0] for g in gp])

    d_cw = jnp.sum(d_cw8, axis=2)
    pieces = [
        _as_rows(jnp.concatenate([vec_a2[0:2], vec_a1[0:1], vec_b[0:3], vec_mid[3:5]], axis=0)),
        _as_rows(jnp.concatenate([vec_a1[3:4], vec_a1[1:3], vec_b[3:5], vec_mid[2:3], vec_mid[0:2]], axis=0)),
        _as_rows(d_bs_t.T), _as_rows(d_ws), _as_rows(d_cw),
    ]
    n_rows = sum(p.shape[0] for p in pieces)
    pad_rows = -n_rows % (N_DEV * SUBLANES)
    small = _all_reduce_small(jnp.concatenate(pieces + [jnp.zeros((pad_rows, LANES), F32)], axis=0))
    g_rows = d_model // LANES
    o0 = N_DEV * g_rows
    g_b_in = small[0:o0].reshape(N_DEV * d_model)
    vecs = [small[o0 + a * g_rows:o0 + (a + 1) * g_rows].reshape(d_model) for a in range(8)]
    g_conv_b, g_gn_g, g_gn_b, g_ln_v_g, g_ln_v_b, g_b_o, g_ln_out_g, g_ln_out_b = vecs
    o1 = o0 + 8 * g_rows
    g_b_spatial = small[o1:o1 + n_heads].reshape(n_heads, LANES)
    o2 = o1 + n_heads
    g_w_spatial = small[o2:o2 + n_heads * LANES].reshape(n_heads, LANES, LANES)
    o3 = o2 + n_heads * LANES
    g_cw_full = small[o3:o3 + n_heads * HALO].reshape(n_heads, HALO, LANES).transpose(1, 0, 2).reshape(HALO, d_model)
    g_conv_w = lax.dynamic_slice(g_cw_full, (0, me * dc), (CONV_K, dc))

    d_w_in, nm_w_in, nv_w_in = _adamw_tiled(w_in, g_w_in, m_w_in, v_w_in)
    two_d = lambda a: a.reshape(-1, a.shape[-1]) if a.ndim != 1 else (
        a.reshape(-1, LANES) if a.shape[0] % LANES == 0 else a.reshape(1, -1))
    names = ["b_in", "conv_w", "conv_b", "gn_g", "gn_b", "ln_v_g", "ln_v_b", "w_spatial", "b_spatial",
             "w_pa", "w_pb", "w_o", "b_o", "ln_out_g", "ln_out_b"]
    ws = dict(b_in=b_in, conv_w=conv_w, conv_b=conv_b, gn_g=gn_g, gn_b=gn_b, ln_v_g=ln_v_g, ln_v_b=ln_v_b,
              w_spatial=w_spatial, b_spatial=b_spatial, w_pa=w_pa, w_pb=w_pb, w_o=w_o, b_o=b_o,
              ln_out_g=ln_out_g, ln_out_b=ln_out_b)
    gs = dict(b_in=g_b_in, conv_w=g_conv_w, conv_b=g_conv_b, gn_g=g_gn_g, gn_b=g_gn_b, ln_v_g=g_ln_v_g,
              ln_v_b=g_ln_v_b, w_spatial=g_w_spatial, b_spatial=g_b_spatial, w_pa=g_w_pa, w_pb=g_w_pb,
              w_o=g_w_o, b_o=g_b_o, ln_out_g=g_ln_out_g, ln_out_b=g_ln_out_b)
    ms = dict(b_in=m_b_in, conv_w=m_conv_w, conv_b=m_conv_b, gn_g=m_gn_g, gn_b=m_gn_b, ln_v_g=m_ln_v_g,
              ln_v_b=m_ln_v_b, w_spatial=m_w_spatial, b_spatial=m_b_spatial, w_pa=m_w_pa, w_pb=m_w_pb,
              w_o=m_w_o, b_o=m_b_o, ln_out_g=m_ln_out_g, ln_out_b=m_ln_out_b)
    vs = dict(b_in=v_b_in, conv_w=v_conv_w, conv_b=v_conv_b, gn_g=v_gn_g, gn_b=v_gn_b, ln_v_g=v_ln_v_g,
              ln_v_b=v_ln_v_b, w_spatial=v_w_spatial, b_spatial=v_b_spatial, w_pa=v_w_pa, w_pb=v_w_pb,
              w_o=v_w_o, b_o=v_b_o, ln_out_g=v_ln_out_g, ln_out_b=v_ln_out_b)
    upd = _adamw_many([tuple(two_d(d[n]) for d in (ws, gs, ms, vs)) for n in names])
    delta = {n: u[0].reshape(ws[n].shape) for n, u in zip(names, upd)}
    new_m = {n: u[1].reshape(ws[n].shape) for n, u in zip(names, upd)}
    new_v = {n: u[2].reshape(ws[n].shape) for n, u in zip(names, upd)}
    gs["w_in"], delta["w_in"], new_m["w_in"], new_v["w_in"] = g_w_in, d_w_in, nm_w_in, nv_w_in

    order = ["w_in"] + names
    return (loss, grad_x, *[gs[n] for n in order], *[delta[n] for n in order],
            *[new_m[n] for n in order], *[new_v[n] for n in order])
```

```python
import jax
import jax.numpy as jnp
from jax import lax
from jax.experimental import pallas as pl
from jax.experimental.pallas import tpu as pltpu

F32 = jnp.float32
BF16 = jnp.bfloat16
SDS = jax.ShapeDtypeStruct

N_DEV = 8
LANES = 128
SUBLANES = 8
CONV_K = 31
HALO = 32
LN_EPS = 1e-5
DEEPNORM_ALPHA = 2.0 ** 0.25
ADAM_LR, ADAM_B1, ADAM_B2, ADAM_EPS, ADAM_WD, ADAM_STEP = 0.001, 0.9, 0.999, 1e-08, 0.01, 10
GELU_C = 0.7978845608028654
GELU_A = 0.044715
VMEM_LIMIT = 56 * 1024 * 1024
MESH = pl.DeviceIdType.MESH
ANY = pl.BlockSpec(memory_space=pl.ANY)
VMEM = pl.BlockSpec(memory_space=pltpu.VMEM)
AG_ORDER = (0, 1, 4, 2, 5, 3, 6, 7)
RS_ORDER = (7, 6, 5, 4, 3, 2, 1, 0)


def _params(n_grid=0):
    sem = ("arbitrary",) * n_grid if n_grid else None
    return pltpu.CompilerParams(dimension_semantics=sem, vmem_limit_bytes=VMEM_LIMIT)


def _tile(n, pref, mult):
    t = min(n, pref)
    while n % t or t % mult:
        t -= 1
    return t


def _colsum(v):
    return jnp.sum(v, axis=0, keepdims=True)


def _sigmoid(v):
    return jax.nn.sigmoid(v)


def _silu_and_grad(v):
    s = _sigmoid(v)
    return v * s, s * (1.0 + v * (1.0 - s))


def _gelu_and_grad(v):
    inner = GELU_C * (v + GELU_A * v * v * v)
    th = jnp.tanh(inner)
    val = 0.5 * v * (1.0 + th)
    grad = 0.5 * (1.0 + th) + 0.5 * v * (1.0 - th * th) * GELU_C * (1.0 + 3.0 * GELU_A * v * v)
    return val, grad


def _tril_mask():
    r = lax.broadcasted_iota(jnp.int32, (LANES, LANES), 0)
    c = lax.broadcasted_iota(jnp.int32, (LANES, LANES), 1)
    return c <= r


def _dot(a, b):
    return jnp.dot(a, b, preferred_element_type=F32)


def _dot_tb(a, b):
    return lax.dot_general(a, b, (((1,), (1,)), ((), ())), preferred_element_type=F32)


def _dot_ta(a, b):
    return lax.dot_general(a, b, (((0,), (0,)), ((), ())), preferred_element_type=F32)


def _mesh_pos():
    return lax.axis_index("x"), lax.axis_index("y"), lax.axis_index("c")


def _block_of(pos):
    return 4 * pos[0] + 2 * pos[1] + pos[2]


def _peers():
    x, y, c = _mesh_pos()
    out = []
    for k in range(1, N_DEV):
        pos = (1 - x if k & 4 else x, 1 - y if k & 2 else y, 1 - c if k & 1 else c)
        out.append((pos, _block_of(pos)))
    return out


def _proj_all_gather(x2, w_in, w_pa, w_pb, w_o, conv_w, b_in3, blk_order):
    n_tok, d_model = x2.shape
    r8 = w_pa.shape[0]
    kc, dc = conv_w.shape
    tm = _tile(n_tok, 1024, 16)
    n_t = n_tok // tm

    def body(ord_ref, x_ref, b_ref, win_ref, wpa_ref, wpb_ref, wo_ref, cw_ref,
             proj_ref, xb_ref, wall_ref, wp_ref, cwall_ref,
             wbuf, st_p, send_sems, recv_sems, local_sems, wall_sems):
        s = pl.program_id(0)
        t = pl.program_id(1)
        x, y, c = _mesh_pos()
        me = (x, y, c)
        sibling = (x, y, 1 - c)
        chips = [(1 - x, y), (x, 1 - y), (1 - x, 1 - y)]
        outs = [wbuf, wp_ref, cwall_ref]
        srcs = [None, st_p, cw_ref]
        consumed = [me, sibling, (*chips[0], c), (*chips[1], c), (*chips[0], 1 - c), (*chips[1], 1 - c),
                    (*chips[2], c), (*chips[2], 1 - c)]

        def slot(o, pos):
            return o.at[_block_of(pos)]

        def copy(a, k, block, to, src=None):
            o = outs[a]
            return pltpu.make_async_remote_copy(
                src_ref=slot(o, block) if src is None else src, dst_ref=slot(o, block),
                send_sem=send_sems.at[a, k], recv_sem=recv_sems.at[a, k],
                device_id=to, device_id_type=MESH)

        def first_copies():
            cps = []
            for a in range(3):
                cps.append(copy(a, 0, me, sibling, src=srcs[a]))
                for j, chip in enumerate(chips):
                    cps.append(copy(a, 1 + j, me, (*chip, c), src=srcs[a]))
            return cps

        def passed_copies(j):
            return [copy(a, 4 + j, (*chips[j], c), sibling) for a in range(3)]

        def local_copies():
            return [pltpu.make_async_copy(srcs[a], slot(outs[a], me), local_sems.at[a]) for a in (1, 2)]

        def to_hbm(step):
            return pltpu.make_async_copy(slot(wbuf, consumed[step]), slot(wall_ref, consumed[step]),
                                         wall_sems.at[step])

        @pl.when(jnp.logical_and(s == 0, t == 0))
        def _():
            slot(wbuf, me)[...] = win_ref[...].astype(BF16)
            st_p[0] = wpa_ref[...].astype(BF16)
            st_p[1] = wpb_ref[...].astype(BF16)
            st_p[2] = wo_ref[...].astype(BF16)
            for cp in first_copies() + local_copies():
                cp.start()
            to_hbm(0).start()

        arrivals = {1: ("sibling", 0), 2: ("chip", 0), 3: ("chip", 1), 4: ("passed", 0), 5: ("passed", 1),
                    6: ("chip", 2), 7: ("passed", 2)}
        for step, (kind, j) in arrivals.items():
            @pl.when(jnp.logical_and(s == step, t == 0))
            def _(step=step, kind=kind, j=j):
                if kind == "sibling":
                    copy(0, 0, sibling, me).wait_recv()
                elif kind == "chip":
                    for a in range(3):
                        copy(a, 1 + j, (*chips[j], c), me).wait_recv()
                    for cp in passed_copies(j):
                        cp.start()
                else:
                    copy(0, 4 + j, (*chips[j], 1 - c), me).wait_recv()
                to_hbm(step).start()

        xb = x_ref[...].astype(BF16)
        xb_ref[...] = xb
        proj_ref[...] = _dot(xb, wbuf[ord_ref[s]]) + b_ref[...]

        @pl.when(jnp.logical_and(s == N_DEV - 1, t == n_t - 1))
        def _():
            for a in (1, 2):
                copy(a, 0, sibling, me).wait_recv()
                for j in range(3):
                    copy(a, 4 + j, (*chips[j], 1 - c), me).wait_recv()
            for cp in first_copies() + passed_copies(0) + passed_copies(1) + passed_copies(2):
                cp.wait_send()
            for cp in local_copies() + [to_hbm(step) for step in range(N_DEV)]:
                cp.wait()

    grid_spec = pltpu.PrefetchScalarGridSpec(
        num_scalar_prefetch=1, grid=(N_DEV, n_t),
        in_specs=[pl.BlockSpec((tm, d_model), lambda s, t, o: (t, 0)),
                  pl.BlockSpec((None, 1, d_model), lambda s, t, o: (o[s], 0, 0)),
                  VMEM, VMEM, VMEM, VMEM, VMEM],
        out_specs=[pl.BlockSpec((tm, d_model), lambda s, t, o: (t, o[s])),
                   pl.BlockSpec((tm, d_model), lambda s, t, o: (jnp.where(s == 0, t, n_t), 0)),
                   ANY, ANY, ANY],
        scratch_shapes=[pltpu.VMEM((N_DEV, d_model, d_model), BF16), pltpu.VMEM((3, r8, d_model), BF16),
                        pltpu.SemaphoreType.DMA((3, 7)), pltpu.SemaphoreType.DMA((3, 7)),
                        pltpu.SemaphoreType.DMA((3,)), pltpu.SemaphoreType.DMA((N_DEV,))])
    return pl.pallas_call(
        body, name="proj_all_gather", grid_spec=grid_spec,
        out_shape=[SDS((n_tok, N_DEV * d_model), F32), SDS(((n_t + 1) * tm, d_model), BF16),
                   SDS((N_DEV, d_model, d_model), BF16), SDS((N_DEV, 3, r8, d_model), BF16),
                   SDS((N_DEV, kc, dc), F32)],
        compiler_params=_params(2),
    )(blk_order, x2, b_in3, w_in, w_pa, w_pb, w_o, conv_w)


def _all_reduce_small(part):
    n_rows = part.shape[0]
    rsl = n_rows // N_DEV

    def body(p_ref, o_ref, rbuf, red, send1, recv1, send2, recv2):
        x, y, c = _mesh_pos()
        me = _block_of((x, y, c))
        peers = _peers()

        def rows_of(blk):
            return pl.ds(pl.multiple_of(blk * rsl, SUBLANES), rsl)

        scatter = [pltpu.make_async_remote_copy(
            src_ref=p_ref.at[rows_of(blk), :], dst_ref=rbuf.at[k],
            send_sem=send1.at[k], recv_sem=recv1.at[k], device_id=pos, device_id_type=MESH)
            for k, (pos, blk) in enumerate(peers)]
        for cp in scatter:
            cp.start()
        for cp in scatter:
            cp.wait_recv()
        acc = p_ref[rows_of(me), :]
        for k in range(N_DEV - 1):
            acc = acc + rbuf[k]
        red[...] = acc
        o_ref[rows_of(me), :] = acc
        gather = [pltpu.make_async_remote_copy(
            src_ref=red, dst_ref=o_ref.at[rows_of(me), :],
            send_sem=send2.at[k], recv_sem=recv2.at[k], device_id=pos, device_id_type=MESH)
            for k, (pos, blk) in enumerate(peers)]
        for cp in gather:
            cp.start()
        for k, (pos, blk) in enumerate(peers):
            pltpu.make_async_remote_copy(
                src_ref=red, dst_ref=o_ref.at[rows_of(blk), :],
                send_sem=send2.at[k], recv_sem=recv2.at[k], device_id=pos,
                device_id_type=MESH).wait_recv()
        for cp in scatter + gather:
            cp.wait_send()

    return pl.pallas_call(
        body, name="all_reduce_small", out_shape=SDS(part.shape, F32),
        in_specs=[VMEM], out_specs=VMEM,
        scratch_shapes=[pltpu.VMEM((N_DEV - 1, rsl, LANES), F32), pltpu.VMEM((rsl, LANES), F32)]
        + [pltpu.SemaphoreType.DMA((N_DEV - 1,))] * 4,
        compiler_params=_params(),
    )(part)


def _conv_rows(ta):
    return _tile(ta, 64, SUBLANES)


def _branch_a_fwd(proj, conv_w_full, conv_b, gn_g, gn_b, seq):
    n_tok = proj.shape[0]
    d_model = conv_b.shape[1]
    ta = _tile(seq, 256, HALO)
    per_seq = seq // ta
    rc = _conv_rows(ta)

    def body(av_ref, ag_ref, gt_ref, avh_ref, agh_ref, cw_ref, cb_ref, gg_ref, gb_ref,
             h3_ref, h1_ref, ext):
        keep = jnp.where(pl.program_id(0) % per_seq == 0, 0.0, 1.0)

        def group(g, carry):
            sl = pl.ds(pl.multiple_of(g * LANES, LANES), LANES)
            ext[0:HALO, :] = avh_ref[:, sl] * _sigmoid(agh_ref[:, sl]) * keep
            ext[HALO:HALO + ta, :] = av_ref[:, sl] * _sigmoid(ag_ref[:, sl])
            for r0 in range(0, ta, rc):
                acc = jnp.broadcast_to(cb_ref[:, sl], (rc, LANES))
                for k in range(CONV_K):
                    acc = acc + ext[pl.ds(r0 + HALO - (CONV_K - 1) + k, rc), :] * cw_ref[k:k + 1, sl]
                h1_ref[pl.ds(r0, rc), sl] = acc
            h1 = h1_ref[:, sl]
            mu = jnp.mean(h1, axis=-1, keepdims=True)
            dlt = h1 - mu
            var = jnp.mean(dlt * dlt, axis=-1, keepdims=True)
            h2 = dlt * lax.rsqrt(var + LN_EPS) * gg_ref[:, sl] + gb_ref[:, sl]
            gate = gt_ref[:, sl]
            h3_ref[:, sl] = (h2 * _sigmoid(h2) * gate * _sigmoid(gate)).astype(BF16)
            return carry

        lax.fori_loop(0, d_model // LANES, group, 0)

    blk = lambda j: pl.BlockSpec((ta, d_model), lambda i: (i, j))
    halo = lambda j: pl.BlockSpec((HALO, d_model), lambda i: (jnp.maximum(i * (ta // HALO) - 1, 0), j))
    row = pl.BlockSpec((1, d_model), lambda i: (0, 0))
    return pl.pallas_call(
        body, name="branch_a_fwd", grid=(n_tok // ta,),
        in_specs=[blk(0), blk(1), blk(2), halo(0), halo(1),
                  pl.BlockSpec((HALO, d_model), lambda i: (0, 0)), row, row, row],
        out_specs=[pl.BlockSpec((ta, d_model), lambda i: (i, 0))] * 2,
        out_shape=[SDS((n_tok, d_model), BF16), SDS((n_tok, d_model), F32)],
        scratch_shapes=[pltpu.VMEM((HALO + ta, LANES), F32)],
        compiler_params=_params(1),
    )(proj, proj, proj, proj, proj, conv_w_full, conv_b, gn_g, gn_b)


def _branch_b_fwd(proj, ln_g, ln_b, w_spatial, b_spatial_t, seq):
    n_tok = proj.shape[0]
    d_model = ln_g.shape[1]
    n_heads = d_model // LANES
    tb = _tile(seq, 256, LANES)

    def body(u_ref, v_ref, bg_ref, lg_ref, lb_ref, ws_ref, bs_ref, s_ref, vn_buf):
        v, _ = _gelu_and_grad(v_ref[...])
        mu = jnp.mean(v, axis=-1, keepdims=True)
        dlt = v - mu
        var = jnp.mean(dlt * dlt, axis=-1, keepdims=True)
        vn_buf[...] = (dlt * lax.rsqrt(var + LN_EPS) * lg_ref[...] + lb_ref[...]).astype(BF16)
        tril = _tril_mask()
        for h in range(n_heads):
            cols = slice(h * LANES, (h + 1) * LANES)
            w_h = jnp.where(tril, ws_ref[h], 0.0).astype(BF16)
            bias = bs_ref[:, h:h + 1]
            for ch in range(tb // LANES):
                rows = slice(ch * LANES, (ch + 1) * LANES)
                mix = _dot(w_h, vn_buf[rows, cols]) + bias
                u, _ = _gelu_and_grad(u_ref[rows, cols])
                gate = bg_ref[rows, cols]
                s_ref[rows, cols] = (u * mix * gate * _sigmoid(gate)).astype(BF16)

    blk = lambda j: pl.BlockSpec((tb, d_model), lambda i: (i, j))
    row = pl.BlockSpec((1, d_model), lambda i: (0, 0))
    return pl.pallas_call(
        body, name="branch_b_fwd", grid=(n_tok // tb,),
        in_specs=[blk(3), blk(4), blk(5), row, row,
                  pl.BlockSpec((n_heads, LANES, LANES), lambda i: (0, 0, 0)),
                  pl.BlockSpec((LANES, n_heads), lambda i: (0, 0))],
        out_specs=pl.BlockSpec((tb, d_model), lambda i: (i, 0)),
        out_shape=SDS((n_tok, d_model), BF16),
        scratch_shapes=[pltpu.VMEM((tb, d_model), BF16)],
        compiler_params=_params(1),
    )(proj, proj, proj, ln_g, ln_b, w_spatial, b_spatial_t)


MID_ROWS = 8


def _mid(h3, s, proj, x2, target, w_pa, w_pb, w_o, b_o, lo_g, lo_b):
    n_tok, d_model = x2.shape
    tm = _tile(n_tok, 256, 16)

    def body(h3_ref, s_ref, ma_ref, mb_ref, x_ref, t_ref, wpa_ref, wpb_ref, wo_ref, bo_ref,
             lg_ref, lb_ref, dproj_ref, dh3_ref, ds_ref, dr_ref, drb_ref, mixed_ref, dya_ref,
             dyb_ref, vec_ref):
        @pl.when(pl.program_id(0) == 0)
        def _():
            vec_ref[...] = jnp.zeros_like(vec_ref)

        ya = _dot(h3_ref[...], wpa_ref[...])
        yb = _dot(s_ref[...], wpb_ref[...])
        ga = _sigmoid(ma_ref[...])
        gb = _sigmoid(mb_ref[...])
        mixed = (ga * ya + gb * yb).astype(BF16)
        mixed_ref[...] = mixed
        r = DEEPNORM_ALPHA * x_ref[...] + _dot(mixed, wo_ref[...]) + bo_ref[...]
        mu = jnp.mean(r, axis=-1, keepdims=True)
        dlt = r - mu
        rstd = lax.rsqrt(jnp.mean(dlt * dlt, axis=-1, keepdims=True) + LN_EPS)
        rhat = dlt * rstd
        diff = rhat * lg_ref[...] + lb_ref[...] - t_ref[...]
        dy = diff * (1.0 / d_model)
        vec_ref[0:1, :] += _colsum(dy * rhat)
        vec_ref[1:2, :] += _colsum(dy)
        vec_ref[5:6, :] += _colsum(diff * diff)
        drh = dy * lg_ref[...]
        dr = rstd * (drh - jnp.mean(drh, axis=-1, keepdims=True)
                     - rhat * jnp.mean(drh * rhat, axis=-1, keepdims=True))
        vec_ref[2:3, :] += _colsum(dr)
        dr_ref[...] = dr
        drb = dr.astype(BF16)
        drb_ref[...] = drb
        dmixed = _dot_tb(drb, wo_ref[...])
        dma = dmixed * ya * ga * (1.0 - ga)
        dmb = dmixed * yb * gb * (1.0 - gb)
        vec_ref[3:4, :] += _colsum(dma)
        vec_ref[4:5, :] += _colsum(dmb)
        dproj_ref[:, 0:d_model] = dma.astype(BF16)
        dproj_ref[:, d_model:2 * d_model] = dmb.astype(BF16)
        dya = (dmixed * ga).astype(BF16)
        dyb = (dmixed * gb).astype(BF16)
        dya_ref[...] = dya
        dyb_ref[...] = dyb
        dh3_ref[...] = _dot_tb(dya, wpa_ref[...])
        ds_ref[...] = _dot_tb(dyb, wpb_ref[...])

    tile = pl.BlockSpec((tm, d_model), lambda i: (i, 0))
    full = pl.BlockSpec((d_model, d_model), lambda i: (0, 0))
    row = pl.BlockSpec((1, d_model), lambda i: (0, 0))
    bf = SDS((n_tok, d_model), BF16)
    f32 = SDS((n_tok, d_model), F32)
    return pl.pallas_call(
        body, name="mid", grid=(n_tok // tm,),
        in_specs=[tile, tile, pl.BlockSpec((tm, d_model), lambda i: (i, 6)),
                  pl.BlockSpec((tm, d_model), lambda i: (i, 7)), tile, tile, full, full, full,
                  row, row, row],
        out_specs=[pl.BlockSpec((tm, 2 * d_model), lambda i: (i, 3)), tile, tile, tile, tile, tile,
                   tile, tile, pl.BlockSpec((MID_ROWS, d_model), lambda i: (0, 0))],
        out_shape=[SDS((n_tok, N_DEV * d_model), BF16), f32, f32, f32, bf, bf, bf, bf,
                   SDS((MID_ROWS, d_model), F32)],
        compiler_params=_params(1),
    )(h3, s, proj, proj, x2, target, w_pa, w_pb, w_o, b_o, lo_g, lo_b)


B_ROWS = 8


def _branch_b_bwd(dproj, proj, d_s, ln_g, ln_b, w_spatial, b_spatial_t, seq):
    n_tok = proj.shape[0]
    d_model = ln_g.shape[1]
    n_heads = d_model // LANES
    tb = _tile(seq, 256, LANES)

    def body(dproj_in, u_ref, v_ref, bg_ref, ds_ref, lg_ref, lb_ref, ws_ref, bs_ref,
             dproj_ref, vec_ref, dws_ref, dbs_ref, vn_buf, dv_buf):
        del dproj_in

        @pl.when(pl.program_id(0) == 0)
        def _():
            vec_ref[...] = jnp.zeros_like(vec_ref)
            dws_ref[...] = jnp.zeros_like(dws_ref)
            dbs_ref[...] = jnp.zeros_like(dbs_ref)

        v, dgelu_v = _gelu_and_grad(v_ref[...])
        mu = jnp.mean(v, axis=-1, keepdims=True)
        dlt = v - mu
        rstd = lax.rsqrt(jnp.mean(dlt * dlt, axis=-1, keepdims=True) + LN_EPS)
        vhat = dlt * rstd
        vn_buf[...] = (vhat * lg_ref[...] + lb_ref[...]).astype(BF16)
        tril = _tril_mask()
        for h in range(n_heads):
            cols = slice(h * LANES, (h + 1) * LANES)
            w_h = jnp.where(tril, ws_ref[h], 0.0).astype(BF16)
            bias = bs_ref[:, h:h + 1]
            for ch in range(tb // LANES):
                rows = slice(ch * LANES, (ch + 1) * LANES)
                vn = vn_buf[rows, cols]
                mix = _dot(w_h, vn) + bias
                u, dgelu_u = _gelu_and_grad(u_ref[rows, cols])
                sg, dsilu = _silu_and_grad(bg_ref[rows, cols])
                dsv = ds_ref[rows, cols]
                du = dsv * mix * sg * dgelu_u
                dbg = dsv * u * mix * dsilu
                dmix = dsv * u * sg
                dmix_bf = dmix.astype(BF16)
                dproj_ref[rows, cols] = du.astype(BF16)
                dproj_ref[rows, 2 * d_model + h * LANES:2 * d_model + (h + 1) * LANES] = dbg.astype(BF16)
                vec_ref[0:1, cols] += _colsum(du)
                vec_ref[2:3, cols] += _colsum(dbg)
                dbs_ref[:, h:h + 1] += jnp.sum(dmix, axis=1, keepdims=True)
                dws_ref[h] += jnp.where(tril, _dot_tb(dmix_bf, vn), 0.0)
                dv_buf[rows, cols] = _dot_ta(w_h, dmix_bf)
        dvn = dv_buf[...]
        vec_ref[3:4, :] += _colsum(dvn * vhat)
        vec_ref[4:5, :] += _colsum(dvn)
        dvh = dvn * lg_ref[...]
        dv = rstd * (dvh - jnp.mean(dvh, axis=-1, keepdims=True)
                     - vhat * jnp.mean(dvh * vhat, axis=-1, keepdims=True)) * dgelu_v
        vec_ref[1:2, :] += _colsum(dv)
        dproj_ref[:, d_model:2 * d_model] = dv.astype(BF16)

    blk = lambda j: pl.BlockSpec((tb, d_model), lambda i: (i, j))
    row = pl.BlockSpec((1, d_model), lambda i: (0, 0))
    return pl.pallas_call(
        body, name="branch_b_bwd", grid=(n_tok // tb,),
        in_specs=[ANY, blk(3), blk(4), blk(5), pl.BlockSpec((tb, d_model), lambda i: (i, 0)), row, row,
                  pl.BlockSpec((n_heads, LANES, LANES), lambda i: (0, 0, 0)),
                  pl.BlockSpec((LANES, n_heads), lambda i: (0, 0))],
        out_specs=[pl.BlockSpec((tb, 3 * d_model), lambda i: (i, 1)),
                   pl.BlockSpec((B_ROWS, d_model), lambda i: (0, 0)),
                   pl.BlockSpec((n_heads, LANES, LANES), lambda i: (0, 0, 0)),
                   pl.BlockSpec((LANES, n_heads), lambda i: (0, 0))],
        out_shape=[SDS(dproj.shape, BF16), SDS((B_ROWS, d_model), F32),
                   SDS((n_heads, LANES, LANES), F32), SDS((LANES, n_heads), F32)],
        scratch_shapes=[pltpu.VMEM((tb, d_model), BF16), pltpu.VMEM((tb, d_model), F32)],
        input_output_aliases={0: 0},
        compiler_params=_params(1),
    )(dproj, proj, proj, proj, d_s, ln_g, ln_b, w_spatial, b_spatial_t)


A1_ROWS = 8


def _branch_a_bwd_norm(dproj, proj, h1, d_h3, gn_g, gn_b, seq):
    n_tok = proj.shape[0]
    d_model = gn_g.shape[1]
    ta = _tile(seq, 256, 16)

    def body(dproj_in, gt_ref, h1_ref, dh3_ref, gg_ref, gb_ref, dproj_ref, dh1_ref, vec_ref):
        del dproj_in

        @pl.when(pl.program_id(0) == 0)
        def _():
            vec_ref[...] = jnp.zeros_like(vec_ref)

        def group(g, carry):
            sl = pl.ds(pl.multiple_of(g * LANES, LANES), LANES)
            h1 = h1_ref[:, sl]
            mu = jnp.mean(h1, axis=-1, keepdims=True)
            dlt = h1 - mu
            rstd = lax.rsqrt(jnp.mean(dlt * dlt, axis=-1, keepdims=True) + LN_EPS)
            nrm = dlt * rstd
            sw, dsw = _silu_and_grad(nrm * gg_ref[:, sl] + gb_ref[:, sl])
            sg, dsg = _silu_and_grad(gt_ref[:, sl])
            dh3 = dh3_ref[:, sl]
            dgate = dh3 * sw * dsg
            dproj_ref[:, sl] = dgate.astype(BF16)
            vec_ref[0:1, sl] += _colsum(dgate)
            dh2 = dh3 * sg * dsw
            vec_ref[1:2, sl] += _colsum(dh2 * nrm)
            vec_ref[2:3, sl] += _colsum(dh2)
            dn = dh2 * gg_ref[:, sl]
            dh1 = rstd * (dn - jnp.mean(dn, axis=-1, keepdims=True)
                          - nrm * jnp.mean(dn * nrm, axis=-1, keepdims=True))
            vec_ref[3:4, sl] += _colsum(dh1)
            dh1_ref[:, sl] = dh1
            return carry

        lax.fori_loop(0, d_model // LANES, group, 0)

    tile = pl.BlockSpec((ta, d_model), lambda i: (i, 0))
    row = pl.BlockSpec((1, d_model), lambda i: (0, 0))
    return pl.pallas_call(
        body, name="branch_a_bwd_norm", grid=(n_tok // ta,),
        in_specs=[ANY, pl.BlockSpec((ta, d_model), lambda i: (i, 2)), tile, tile, row, row],
        out_specs=[pl.BlockSpec((ta, d_model), lambda i: (i, 2)), tile,
                   pl.BlockSpec((A1_ROWS, d_model), lambda i: (0, 0))],
        out_shape=[SDS(dproj.shape, BF16), SDS((n_tok, d_model), F32), SDS((A1_ROWS, d_model), F32)],
        input_output_aliases={0: 0},
        compiler_params=_params(1),
    )(dproj, proj, h1, d_h3, gn_g, gn_b)


A2_ROWS = 8


def _branch_a_bwd_conv(dproj, proj, d_h1, conv_w_full, seq):
    n_tok = proj.shape[0]
    d_model = conv_w_full.shape[1]
    n_groups = d_model // LANES
    ta = _tile(seq, 256, HALO)
    per_seq = seq // ta
    rc = _conv_rows(ta)
    last_halo = n_tok // HALO - 1

    def body(dproj_in, av_ref, ag_ref, avh_ref, agh_ref, dh1_ref, dh1h_ref, cw_ref,
             dproj_ref, vec_ref, dcw_ref, ext_h0, ext_d):
        del dproj_in
        i = pl.program_id(0)

        @pl.when(i == 0)
        def _():
            vec_ref[...] = jnp.zeros_like(vec_ref)
            dcw_ref[...] = jnp.zeros_like(dcw_ref)

        keep_past = jnp.where(i % per_seq == 0, 0.0, 1.0)
        keep_next = jnp.where(i % per_seq == per_seq - 1, 0.0, 1.0)

        def group(g, carry):
            sl = pl.ds(pl.multiple_of(g * LANES, LANES), LANES)
            av = av_ref[:, sl]
            sig = _sigmoid(ag_ref[:, sl])
            ext_h0[0:HALO, :] = avh_ref[:, sl] * _sigmoid(agh_ref[:, sl]) * keep_past
            ext_h0[HALO:HALO + ta, :] = av * sig
            ext_d[0:ta, :] = dh1_ref[:, sl]
            ext_d[ta:ta + HALO, :] = dh1h_ref[:, sl] * keep_next
            for r0 in range(0, ta, rc):
                dh1 = ext_d[pl.ds(r0, rc), :]
                acc = jnp.zeros((rc, LANES), F32)
                for k in range(CONV_K):
                    acc = acc + ext_d[pl.ds(r0 + CONV_K - 1 - k, rc), :] * cw_ref[k:k + 1, sl]
                    prod = dh1 * ext_h0[pl.ds(r0 + HALO - (CONV_K - 1) + k, rc), :]
                    dcw_ref[g, k] += jnp.sum(prod.reshape(rc // SUBLANES, SUBLANES, LANES), axis=0)
                rows = pl.ds(r0, rc)
                sig_r = sig[r0:r0 + rc]
                dav = acc * sig_r
                dag = acc * av[r0:r0 + rc] * sig_r * (1.0 - sig_r)
                dproj_ref[rows, sl] = dav.astype(BF16)
                dproj_ref[rows, pl.ds(pl.multiple_of(d_model + g * LANES, LANES), LANES)] = dag.astype(BF16)
                vec_ref[0:1, sl] += _colsum(dav)
                vec_ref[1:2, sl] += _colsum(dag)
            return carry

        lax.fori_loop(0, n_groups, group, 0)

    blk = lambda j: pl.BlockSpec((ta, d_model), lambda i: (i, j))
    halo = lambda j: pl.BlockSpec((HALO, d_model), lambda i: (jnp.maximum(i * (ta // HALO) - 1, 0), j))
    return pl.pallas_call(
        body, name="branch_a_bwd_conv", grid=(n_tok // ta,),
        in_specs=[ANY, blk(0), blk(1), halo(0), halo(1), pl.BlockSpec((ta, d_model), lambda i: (i, 0)),
                  pl.BlockSpec((HALO, d_model), lambda i: (jnp.minimum((i + 1) * (ta // HALO), last_halo), 0)),
                  pl.BlockSpec((HALO, d_model), lambda i: (0, 0))],
        out_specs=[pl.BlockSpec((ta, 2 * d_model), lambda i: (i, 0)),
                   pl.BlockSpec((A2_ROWS, d_model), lambda i: (0, 0)),
                   pl.BlockSpec((n_groups, HALO, SUBLANES, LANES), lambda i: (0, 0, 0, 0))],
        out_shape=[SDS(dproj.shape, BF16), SDS((A2_ROWS, d_model), F32),
                   SDS((n_groups, HALO, SUBLANES, LANES), F32)],
        scratch_shapes=[pltpu.VMEM((HALO + ta, LANES), F32), pltpu.VMEM((ta + HALO, LANES), F32)],
        input_output_aliases={0: 0},
        compiler_params=_params(1),
    )(dproj, proj, proj, proj, proj, d_h1, d_h1, conv_w_full)


def _weight_grad(lhs, rhs, name):
    n_tok, d_model = lhs.shape
    tk = _tile(n_tok, 2048, 16)
    n_k = n_tok // tk

    def body(a_ref, g_ref, o_ref, ob_ref):
        part = _dot_ta(a_ref[...], g_ref[...])

        @pl.when(pl.program_id(0) == 0)
        def _():
            o_ref[...] = part

        @pl.when(pl.program_id(0) != 0)
        def _():
            o_ref[...] += part

        @pl.when(pl.program_id(0) == n_k - 1)
        def _():
            ob_ref[...] = o_ref[...].astype(BF16)

    tile = pl.BlockSpec((tk, d_model), lambda i: (i, 0))
    out = pl.BlockSpec((d_model, d_model), lambda i: (0, 0))
    return pl.pallas_call(
        body, name=name, grid=(n_k,), in_specs=[tile, tile], out_specs=[out, out],
        out_shape=[SDS((d_model, d_model), F32), SDS((d_model, d_model), BF16)],
        compiler_params=_params(1),
    )(lhs, rhs)


def _grad_w_in_reduce_scatter(x_bf, dproj, blk_order):
    n_tok = dproj.shape[0]
    d_model = x_bf.shape[1]
    th = n_tok // 2
    rows = _tile(d_model, 32, 16)

    def body(ord_ref, a_ref, g_ref, o_ref, acc, fb, sbuf, gbuf, rfin,
             send_f, send_s, recv_g, recv_f, out_sem):
        del ord_ref
        s = pl.program_id(0)
        h = pl.program_id(1)
        x, y, c = _mesh_pos()
        sibling = (x, y, 1 - c)
        chips = [(1 - x, y), (x, 1 - y), (1 - x, 1 - y)]

        part = _dot_ta(a_ref[...], g_ref[...])

        @pl.when(h == 0)
        def _():
            acc[...] = part

        @pl.when(h == 1)
        def _():
            acc[...] += part

        def feed(p):
            return pltpu.make_async_remote_copy(src_ref=fb, dst_ref=gbuf.at[p], send_sem=send_f.at[p],
                                                recv_sem=recv_g.at[p], device_id=sibling, device_id_type=MESH)

        def feed_sibling():
            return pltpu.make_async_remote_copy(src_ref=fb, dst_ref=rfin.at[0], send_sem=send_f.at[3],
                                                recv_sem=recv_f.at[0], device_id=sibling, device_id_type=MESH)

        def chip_sum(p):
            return pltpu.make_async_remote_copy(src_ref=sbuf.at[p], dst_ref=rfin.at[1 + p], send_sem=send_s.at[p],
                                                recv_sem=recv_f.at[1 + p], device_id=(*chips[p], c),
                                                device_id_type=MESH)

        steps = {0: ("feed", 2, None), 1: ("gate", 2, None), 2: ("feed", 0, 2), 3: ("gate", 0, None),
                 4: ("feed", 1, 0), 5: ("gate", 1, None)}
        for step, (kind, p, prev) in steps.items():
            @pl.when(jnp.logical_and(s == step, h == 1))
            def _(kind=kind, p=p, prev=prev):
                if kind == "feed":
                    if prev is not None:
                        feed(prev).wait_send()
                    fb[...] = acc[...].astype(BF16)
                    feed(p).start()
                else:
                    feed(p).wait_recv()
                    sbuf[p] = (acc[...] + gbuf[p].astype(F32)).astype(BF16)
                    chip_sum(p).start()

        @pl.when(jnp.logical_and(s == 6, h == 1))
        def _():
            feed(1).wait_send()
            fb[...] = acc[...].astype(BF16)
            feed_sibling().start()

        @pl.when(jnp.logical_and(s == 7, h == 1))
        def _():
            feed_sibling().wait_recv()
            for p in range(3):
                chip_sum(p).wait_recv()

            def add(i, carry):
                r = pl.ds(pl.multiple_of(i * rows, rows), rows)
                tot = acc[r, :]
                for q in range(4):
                    tot = tot + rfin[q, r, :].astype(F32)
                acc[r, :] = tot
                return carry

            lax.fori_loop(0, d_model // rows, add, 0)
            out = pltpu.make_async_copy(acc, o_ref, out_sem)
            out.start()
            feed_sibling().wait_send()
            for p in range(3):
                chip_sum(p).wait_send()
            out.wait()

    grid_spec = pltpu.PrefetchScalarGridSpec(
        num_scalar_prefetch=1, grid=(N_DEV, 2),
        in_specs=[pl.BlockSpec((th, d_model), lambda s, h, o: (h, 0)),
                  pl.BlockSpec((th, d_model), lambda s, h, o: (h, o[s]))],
        out_specs=ANY,
        scratch_shapes=[pltpu.VMEM((d_model, d_model), F32), pltpu.VMEM((d_model, d_model), BF16),
                        pltpu.VMEM((3, d_model, d_model), BF16), pltpu.VMEM((3, d_model, d_model), BF16),
                        pltpu.VMEM((4, d_model, d_model), BF16),
                        pltpu.SemaphoreType.DMA((4,)), pltpu.SemaphoreType.DMA((3,)),
                        pltpu.SemaphoreType.DMA((3,)), pltpu.SemaphoreType.DMA((4,)),
                        pltpu.SemaphoreType.DMA(())])
    return pl.pallas_call(
        body, name="grad_w_in_reduce_scatter", grid_spec=grid_spec,
        out_shape=SDS((d_model, d_model), F32), compiler_params=_params(2),
    )(blk_order, x_bf, dproj)


def _grad_x_reduce_scatter(dproj, w_all, dr, gp_bf, gp_f32):
    n_tok, d_model = dr.shape
    tm = _tile(n_tok, 1024, 16)
    n_t = n_tok // tm
    r8 = d_model // N_DEV
    prow = _tile(r8, 32, 16)

    def body(dp_ref, w_ref, dr_ref, gpa_bf, gpb_bf, gpo_bf, gpa_f32, gpb_f32, gpo_f32,
             o_ref, opa_ref, opb_ref, opo_ref, rbuf, own, send_sems, recv_sems, local_sems):
        i = pl.program_id(0)
        j = pl.program_id(1)
        x, y, c = _mesh_pos()
        me = _block_of((x, y, c))
        peers = _peers()
        gp_bf_refs = [gpa_bf, gpb_bf, gpo_bf]
        gp_f32_refs = [gpa_f32, gpb_f32, gpo_f32]

        def sends():
            return [pltpu.make_async_remote_copy(
                src_ref=gp_bf_refs[a].at[pl.ds(pl.multiple_of(blk * r8, 16), r8), :], dst_ref=rbuf.at[k, a],
                send_sem=send_sems.at[a, k], recv_sem=recv_sems.at[a, k], device_id=pos, device_id_type=MESH)
                for k, (pos, blk) in enumerate(peers) for a in range(3)]

        def locals_():
            return [pltpu.make_async_copy(gp_f32_refs[a].at[pl.ds(pl.multiple_of(me * r8, 8), r8), :],
                                          own.at[a], local_sems.at[a]) for a in range(3)]

        @pl.when(jnp.logical_and(i == 0, j == 0))
        def _():
            for cp in sends() + locals_():
                cp.start()

        part = _dot_tb(dp_ref[...], w_ref[...])

        @pl.when(j == 0)
        def _():
            o_ref[...] = DEEPNORM_ALPHA * dr_ref[...] + part

        @pl.when(j != 0)
        def _():
            o_ref[...] += part

        @pl.when(jnp.logical_and(i == n_t - 1, j == N_DEV - 1))
        def _():
            for cp in locals_():
                cp.wait()
            for cp in sends():
                cp.wait_recv()
            for a, o in enumerate([opa_ref, opb_ref, opo_ref]):
                for q in range(r8 // prow):
                    r = pl.ds(q * prow, prow)
                    tot = own[a, r, :]
                    for k in range(N_DEV - 1):
                        tot = tot + rbuf[k, a, r, :].astype(F32)
                    o[r, :] = tot
            for cp in sends():
                cp.wait_send()

    small = pl.BlockSpec((r8, d_model), lambda i, j: (0, 0))
    return pl.pallas_call(
        body, name="grad_x_reduce_scatter", grid=(n_t, N_DEV),
        in_specs=[pl.BlockSpec((tm, d_model), lambda i, j: (i, j)),
                  pl.BlockSpec((None, d_model, d_model), lambda i, j: (j, 0, 0)),
                  pl.BlockSpec((tm, d_model), lambda i, j: (i, 0))] + [ANY] * 6,
        out_specs=[pl.BlockSpec((tm, d_model), lambda i, j: (i, 0)), small, small, small],
        out_shape=[SDS((n_tok, d_model), F32)] + [SDS((r8, d_model), F32)] * 3,
        scratch_shapes=[pltpu.VMEM((N_DEV - 1, 3, r8, d_model), BF16), pltpu.VMEM((3, r8, d_model), F32),
                        pltpu.SemaphoreType.DMA((3, 7)), pltpu.SemaphoreType.DMA((3, 7)),
                        pltpu.SemaphoreType.DMA((3,))],
        compiler_params=_params(2),
    )(dproj, w_all, dr, *gp_bf, *gp_f32)


def _adamw_math(w, g, m, v):
    m = ADAM_B1 * m + (1.0 - ADAM_B1) * g
    v = ADAM_B2 * v + (1.0 - ADAM_B2) * (g * g)
    m_hat = m / (1.0 - ADAM_B1 ** ADAM_STEP)
    v_hat = v / (1.0 - ADAM_B2 ** ADAM_STEP)
    delta = -ADAM_LR * (m_hat / (jnp.sqrt(v_hat) + ADAM_EPS) + ADAM_WD * w)
    return delta, m, v


def _adamw_tiled(w, g, m, v):
    n_rows, n_cols = w.shape
    tr = _tile(n_rows, 256, SUBLANES)

    def body(w_ref, g_ref, m_ref, v_ref, d_ref, mo_ref, vo_ref):
        d_ref[...], mo_ref[...], vo_ref[...] = _adamw_math(w_ref[...], g_ref[...], m_ref[...], v_ref[...])

    tile = pl.BlockSpec((tr, n_cols), lambda i: (i, 0))
    return pl.pallas_call(
        body, name="adamw_w_in", grid=(n_rows // tr,), in_specs=[tile] * 4, out_specs=[tile] * 3,
        out_shape=[SDS(w.shape, F32)] * 3, compiler_params=_params(1),
    )(w, g, m, v)


def _adamw_many(groups):
    n = len(groups)

    def body(*refs):
        ins, outs = refs[:4 * n], refs[4 * n:]
        for p in range(n):
            w_ref, g_ref, m_ref, v_ref = ins[4 * p:4 * p + 4]
            d, m, v = _adamw_math(w_ref[...], g_ref[...], m_ref[...], v_ref[...])
            outs[3 * p][...] = d
            outs[3 * p + 1][...] = m
            outs[3 * p + 2][...] = v

    flat = [a for grp in groups for a in grp]
    out_shape = [SDS(grp[0].shape, F32) for grp in groups for _ in range(3)]
    res = pl.pallas_call(
        body, name="adamw_small", in_specs=[VMEM] * (4 * n), out_specs=[VMEM] * (3 * n),
        out_shape=out_shape, compiler_params=_params(),
    )(*flat)
    return [tuple(res[3 * p:3 * p + 3]) for p in range(n)]


def _as_rows(a):
    return a.reshape(-1, LANES)


def kernel(x, w_in, b_in, conv_w, conv_b, gn_g, gn_b, ln_v_g, ln_v_b, w_spatial, b_spatial, w_pa, w_pb, w_o, b_o, ln_out_g, ln_out_b, loss_target, m_w_in, m_b_in, m_conv_w, m_conv_b, m_gn_g, m_gn_b, m_ln_v_g, m_ln_v_b, m_w_spatial, m_b_spatial, m_w_pa, m_w_pb, m_w_o, m_b_o, m_ln_out_g, m_ln_out_b, v_w_in, v_b_in, v_conv_w, v_conv_b, v_gn_g, v_gn_b, v_ln_v_g, v_ln_v_b, v_w_spatial, v_b_spatial, v_w_pa, v_w_pb, v_w_o, v_b_o, v_ln_out_g, v_ln_out_b):
    n_batch, seq, d_model = x.shape
    n_tok = n_batch * seq
    n_heads = d_model // LANES
    dc = conv_w.shape[1]
    me = 4 * lax.axis_index("x") + 2 * lax.axis_index("y") + lax.axis_index("c")
    row = lambda a: a.reshape(1, d_model)

    x2 = x.reshape(n_tok, d_model)
    target2 = loss_target.reshape(n_tok, d_model)
    b_spatial_t = b_spatial.T

    ag_blocks = jnp.bitwise_xor(me, jnp.array(AG_ORDER, jnp.int32))
    proj, x_bf, w_all, wp_all, cw_all = _proj_all_gather(
        x2, w_in, w_pa, w_pb, w_o, conv_w, b_in.reshape(N_DEV, 1, d_model), ag_blocks)
    wp_full = [wp_all[:, a].reshape(d_model, d_model) for a in range(3)]
    conv_w_full = jnp.pad(cw_all.transpose(1, 0, 2).reshape(CONV_K, d_model), ((0, HALO - CONV_K), (0, 0)))

    h3, h1 = _branch_a_fwd(proj, conv_w_full, row(conv_b), row(gn_g), row(gn_b), seq)
    s = _branch_b_fwd(proj, row(ln_v_g), row(ln_v_b), w_spatial, b_spatial_t, seq)

    dproj, d_h3, d_s, dr, dr_bf, mixed, d_ya, d_yb, vec_mid = _mid(
        h3, s, proj, x2, target2, *wp_full, row(b_o), row(ln_out_g), row(ln_out_b))
    loss = lax.psum(jnp.sum(vec_mid[5]) * (0.5 / d_model), ("x", "y", "c"))

    dproj, vec_b, d_ws, d_bs_t = _branch_b_bwd(dproj, proj, d_s, row(ln_v_g), row(ln_v_b), w_spatial, b_spatial_t, seq)
    dproj, d_h1, vec_a1 = _branch_a_bwd_norm(dproj, proj, h1, d_h3, row(gn_g), row(gn_b), seq)
    dproj, vec_a2, d_cw8 = _branch_a_bwd_conv(dproj, proj, d_h1, conv_w_full, seq)

    gp = [_weight_grad(lhs, rhs, name) for lhs, rhs, name in
          ((h3, d_ya, "grad_w_pa"), (s, d_yb, "grad_w_pb"), (mixed, dr_bf, "grad_w_o"))]
    rs_blocks = jnp.bitwise_xor(me, jnp.array(RS_ORDER, jnp.int32))
    g_w_in = _grad_w_in_reduce_scatter(x_bf, dproj, rs_blocks)
    grad_x, g_w_pa, g_w_pb, g_w_o = _grad_x_reduce_scatter(
        dproj, w_all, dr, [g[1] for g in gp], [g[0] for g in gp])
    grad_x = grad_x.reshape(x.shape)

    d_cw = jnp.sum(d_cw8, axis=2)
    pieces = [
        _as_rows(jnp.concatenate([vec_a2[0:2], vec_a1[0:1], vec_b[0:3], vec_mid[3:5]], axis=0)),
        _as_rows(jnp.concatenate([vec_a1[3:4], vec_a1[1:3], vec_b[3:5], vec_mid[2:3], vec_mid[0:2]], axis=0)),
        _as_rows(d_bs_t.T), _as_rows(d_ws), _as_rows(d_cw),
    ]
    n_rows = sum(p.shape[0] for p in pieces)
    pad_rows = -n_rows % (N_DEV * SUBLANES)
    small = _all_reduce_small(jnp.concatenate(pieces + [jnp.zeros((pad_rows, LANES), F32)], axis=0))
    g_rows = d_model // LANES
    o0 = N_DEV * g_rows
    g_b_in = small[0:o0].reshape(N_DEV * d_model)
    vecs = [small[o0 + a * g_rows:o0 + (a + 1) * g_rows].reshape(d_model) for a in range(8)]
    g_conv_b, g_gn_g, g_gn_b, g_ln_v_g, g_ln_v_b, g_b_o, g_ln_out_g, g_ln_out_b = vecs
    o1 = o0 + 8 * g_rows
    g_b_spatial = small[o1:o1 + n_heads].reshape(n_heads, LANES)
    o2 = o1 + n_heads
    g_w_spatial = small[o2:o2 + n_heads * LANES].reshape(n_heads, LANES, LANES)
    o3 = o2 + n_heads * LANES
    g_cw_full = small[o3:o3 + n_heads * HALO].reshape(n_heads, HALO, LANES).transpose(1, 0, 2).reshape(HALO, d_model)
    g_conv_w = lax.dynamic_slice(g_cw_full, (0, me * dc), (CONV_K, dc))

    d_w_in, nm_w_in, nv_w_in = _adamw_tiled(w_in, g_w_in, m_w_in, v_w_in)
    two_d = lambda a: a.reshape(-1, a.shape[-1]) if a.ndim != 1 else (
        a.reshape(-1, LANES) if a.shape[0] % LANES == 0 else a.reshape(1, -1))
    names = ["b_in", "conv_w", "conv_b", "gn_g", "gn_b", "ln_v_g", "ln_v_b", "w_spatial", "b_spatial",
             "w_pa", "w_pb", "w_o", "b_o", "ln_out_g", "ln_out_b"]
    ws = dict(b_in=b_in, conv_w=conv_w, conv_b=conv_b, gn_g=gn_g, gn_b=gn_b, ln_v_g=ln_v_g, ln_v_b=ln_v_b,
              w_spatial=w_spatial, b_spatial=b_spatial, w_pa=w_pa, w_pb=w_pb, w_o=w_o, b_o=b_o,
              ln_out_g=ln_out_g, ln_out_b=ln_out_b)
    gs = dict(b_in=g_b_in, conv_w=g_conv_w, conv_b=g_conv_b, gn_g=g_gn_g, gn_b=g_gn_b, ln_v_g=g_ln_v_g,
              ln_v_b=g_ln_v_b, w_spatial=g_w_spatial, b_spatial=g_b_spatial, w_pa=g_w_pa, w_pb=g_w_pb,
              w_o=g_w_o, b_o=g_b_o, ln_out_g=g_ln_out_g, ln_out_b=g_ln_out_b)
    ms = dict(b_in=m_b_in, conv_w=m_conv_w, conv_b=m_conv_b, gn_g=m_gn_g, gn_b=m_gn_b, ln_v_g=m_ln_v_g,
              ln_v_b=m_ln_v_b, w_spatial=m_w_spatial, b_spatial=m_b_spatial, w_pa=m_w_pa, w_pb=m_w_pb,
              w_o=m_w_o, b_o=m_b_o, ln_out_g=m_ln_out_g, ln_out_b=m_ln_out_b)
    vs = dict(b_in=v_b_in, conv_w=v_conv_w, conv_b=v_conv_b, gn_g=v_gn_g, gn_b=v_gn_b, ln_v_g=v_ln_v_g,
              ln_v_b=v_ln_v_b, w_spatial=v_w_spatial, b_spatial=v_b_spatial, w_pa=v_w_pa, w_pb=v_w_pb,
              w_o=v_w_o, b_o=v_b_o, ln_out_g=v_ln_out_g, ln_out_b=v_ln_out_b)
    upd = _adamw_many([tuple(two_d(d[n]) for d in (ws, gs, ms, vs)) for n in names])
    delta = {n: u[0].reshape(ws[n].shape) for n, u in zip(names, upd)}
    new_m = {n: u[1].reshape(ws[n].shape) for n, u in zip(names, upd)}
    new_v = {n: u[2].reshape(ws[n].shape) for n, u in zip(names, upd)}
    gs["w_in"], delta["w_in"], new_m["w_in"], new_v["w_in"] = g_w_in, d_w_in, nm_w_in, nv_w_in

    order = ["w_in"] + names
    return (loss, grad_x, *[gs[n] for n in order], *[delta[n] for n in order],
            *[new_m[n] for n in order], *[new_v[n] for n in order])
```

```python
import jax
import jax.numpy as jnp
from jax import lax
from jax.experimental import pallas as pl
from jax.experimental.pallas import tpu as pltpu

F32 = jnp.float32
BF16 = jnp.bfloat16
SDS = jax.ShapeDtypeStruct

N_DEV = 8
LANES = 128
SUBLANES = 8
CONV_K = 31
HALO = 32
LN_EPS = 1e-5
DEEPNORM_ALPHA = 2.0 ** 0.25
ADAM_LR, ADAM_B1, ADAM_B2, ADAM_EPS, ADAM_WD, ADAM_STEP = 0.001, 0.9, 0.999, 1e-08, 0.01, 10
GELU_C = 0.7978845608028654
GELU_A = 0.044715
VMEM_LIMIT = 56 * 1024 * 1024
MESH = pl.DeviceIdType.MESH
ANY = pl.BlockSpec(memory_space=pl.ANY)
VMEM = pl.BlockSpec(memory_space=pltpu.VMEM)
RS_ORDER = (7, 6, 5, 4, 3, 2, 1, 0)


def _params(n_grid=0):
    sem = ("arbitrary",) * n_grid if n_grid else None
    return pltpu.CompilerParams(dimension_semantics=sem, vmem_limit_bytes=VMEM_LIMIT)


def _tile(n, pref, mult):
    t = min(n, pref)
    while n % t or t % mult:
        t -= 1
    return t


def _colsum(v):
    return jnp.sum(v, axis=0, keepdims=True)


def _sigmoid(v):
    return jax.nn.sigmoid(v)


def _silu_and_grad(v):
    s = _sigmoid(v)
    return v * s, s * (1.0 + v * (1.0 - s))


def _gelu_and_grad(v):
    inner = GELU_C * (v + GELU_A * v * v * v)
    th = jnp.tanh(inner)
    val = 0.5 * v * (1.0 + th)
    grad = 0.5 * (1.0 + th) + 0.5 * v * (1.0 - th * th) * GELU_C * (1.0 + 3.0 * GELU_A * v * v)
    return val, grad


def _tril_mask():
    r = lax.broadcasted_iota(jnp.int32, (LANES, LANES), 0)
    c = lax.broadcasted_iota(jnp.int32, (LANES, LANES), 1)
    return c <= r


def _dot(a, b):
    return jnp.dot(a, b, preferred_element_type=F32)


def _dot_tb(a, b):
    return lax.dot_general(a, b, (((1,), (1,)), ((), ())), preferred_element_type=F32)


def _dot_ta(a, b):
    return lax.dot_general(a, b, (((0,), (0,)), ((), ())), preferred_element_type=F32)


def _mesh_pos():
    return lax.axis_index("x"), lax.axis_index("y"), lax.axis_index("c")


def _block_of(pos):
    return 4 * pos[0] + 2 * pos[1] + pos[2]


def _peers():
    x, y, c = _mesh_pos()
    out = []
    for k in range(1, N_DEV):
        pos = (1 - x if k & 4 else x, 1 - y if k & 2 else y, 1 - c if k & 1 else c)
        out.append((pos, _block_of(pos)))
    return out


def _proj_all_gather(x2, w_in, w_pa, w_pb, w_o, conv_w, b_in3, blk_order):
    n_tok, d_model = x2.shape
    r8 = w_pa.shape[0]
    kc, dc = conv_w.shape
    tm = _tile(n_tok, 1024, LANES)
    n_t = n_tok // tm

    def body(ord_ref, x_ref, b_ref, win_ref, wpa_ref, wpb_ref, wo_ref, cw_ref,
             proj_ref, xt_ref, wall_ref, wp_ref, cwall_ref,
             wbuf, st_p, send_sems, recv_sems, local_sems, wall_sems):
        s = pl.program_id(0)
        t = pl.program_id(1)
        x, y, c = _mesh_pos()
        me = (x, y, c)
        sibling = (x, y, 1 - c)
        n1 = (jnp.bitwise_xor(x, c), jnp.bitwise_xor(y, 1 - c))
        n2 = (jnp.bitwise_xor(x, 1 - c), jnp.bitwise_xor(y, c))
        dg = (1 - x, 1 - y)
        outs = [wbuf, wp_ref, cwall_ref]
        srcs = [None, st_p, cw_ref]
        consumed = [me, sibling, (*n1, c), (*n2, 1 - c), (*n2, c), (*n1, 1 - c), (*dg, c), (*dg, 1 - c)]
        leaves = [(me, sibling), (me, (*n1, c)), (me, (*n2, c)), ((*n1, c), (*n2, c)),
                  ((*n1, c), sibling), ((*n2, c), sibling), ((*dg, c), sibling)]
        lands = [sibling, (*n1, c), (*n2, c), (*dg, c), (*n2, 1 - c), (*n1, 1 - c), (*dg, 1 - c)]

        def slot(o, pos):
            return o.at[_block_of(pos)]

        def copy(a, k, block, to, src=None):
            o = outs[a]
            return pltpu.make_async_remote_copy(
                src_ref=slot(o, block) if src is None else src, dst_ref=slot(o, block),
                send_sem=send_sems.at[a, k], recv_sem=recv_sems.at[a, k],
                device_id=to, device_id_type=MESH)

        def send(a, k):
            block, to = leaves[k]
            return copy(a, k, block, to, src=srcs[a] if k < 3 else None)

        def recv(a, k):
            return copy(a, k, lands[k], me)

        def local_copies():
            return [pltpu.make_async_copy(srcs[a], slot(outs[a], me), local_sems.at[a]) for a in (1, 2)]

        def to_hbm(step):
            return pltpu.make_async_copy(slot(wbuf, consumed[step]), slot(wall_ref, consumed[step]),
                                         wall_sems.at[step])

        def at_step(step):
            return pl.when(jnp.logical_and(s == step, t == 0))

        @at_step(0)
        def _():
            slot(wbuf, me)[...] = win_ref[...].astype(BF16)
            st_p[0] = wpa_ref[...].astype(BF16)
            st_p[1] = wpb_ref[...].astype(BF16)
            st_p[2] = wo_ref[...].astype(BF16)
            for a in range(3):
                send(a, 0).start()
                send(a, 1).start()
            for cp in local_copies():
                cp.start()
            to_hbm(0).start()

        @at_step(1)
        def _():
            recv(0, 0).wait_recv()
            to_hbm(1).start()

        for rnd in range(3):
            @at_step(2 + 2 * rnd)
            def _(rnd=rnd):
                if rnd == 0:
                    for a in range(3):
                        send(a, 2).start()
                recv(0, 1 + rnd).wait_recv()
                if rnd == 0:
                    send(0, 3).start()
                send(0, 4 + rnd).start()
                to_hbm(2 + 2 * rnd).start()

            @at_step(3 + 2 * rnd)
            def _(rnd=rnd):
                for a in (1, 2):
                    recv(a, 1 + rnd).wait_recv()
                    if rnd == 0:
                        send(a, 3).start()
                    send(a, 4 + rnd).start()
                recv(0, 4 + rnd).wait_recv()
                to_hbm(3 + 2 * rnd).start()

        xb = x_ref[...].astype(BF16)
        xt_ref[...] = xb.T
        proj_ref[...] = _dot(xb, wbuf[ord_ref[s]]) + b_ref[...]

        @pl.when(jnp.logical_and(s == N_DEV - 1, t == n_t - 1))
        def _():
            for a in (1, 2):
                for k in (0, 4, 5, 6):
                    recv(a, k).wait_recv()
            for a in range(3):
                for k in range(7):
                    send(a, k).wait_send()
            for cp in local_copies() + [to_hbm(step) for step in range(N_DEV)]:
                cp.wait()

    grid_spec = pltpu.PrefetchScalarGridSpec(
        num_scalar_prefetch=1, grid=(N_DEV, n_t),
        in_specs=[pl.BlockSpec((tm, d_model), lambda s, t, o: (t, 0)),
                  pl.BlockSpec((None, 1, d_model), lambda s, t, o: (o[s], 0, 0)),
                  VMEM, VMEM, VMEM, VMEM, VMEM],
        out_specs=[pl.BlockSpec((tm, d_model), lambda s, t, o: (t, o[s])),
                   pl.BlockSpec((d_model, tm), lambda s, t, o: (0, jnp.where(s == 0, t, n_t))),
                   ANY, ANY, ANY],
        scratch_shapes=[pltpu.VMEM((N_DEV, d_model, d_model), BF16), pltpu.VMEM((3, r8, d_model), BF16),
                        pltpu.SemaphoreType.DMA((3, 7)), pltpu.SemaphoreType.DMA((3, 7)),
                        pltpu.SemaphoreType.DMA((3,)), pltpu.SemaphoreType.DMA((N_DEV,))])
    return pl.pallas_call(
        body, name="proj_all_gather", grid_spec=grid_spec,
        out_shape=[SDS((n_tok, N_DEV * d_model), F32), SDS((d_model, (n_t + 1) * tm), BF16),
                   SDS((N_DEV, d_model, d_model), BF16), SDS((N_DEV, 3, r8, d_model), BF16),
                   SDS((N_DEV, kc, dc), F32)],
        compiler_params=_params(2),
    )(blk_order, x2, b_in3, w_in, w_pa, w_pb, w_o, conv_w)


def _all_reduce_small(part):
    n_rows = part.shape[0]
    rsl = n_rows // N_DEV

    def body(p_ref, o_ref, rbuf, red, send1, recv1, send2, recv2):
        x, y, c = _mesh_pos()
        me = _block_of((x, y, c))
        peers = _peers()

        def rows_of(blk):
            return pl.ds(pl.multiple_of(blk * rsl, SUBLANES), rsl)

        scatter = [pltpu.make_async_remote_copy(
            src_ref=p_ref.at[rows_of(blk), :], dst_ref=rbuf.at[k],
            send_sem=send1.at[k], recv_sem=recv1.at[k], device_id=pos, device_id_type=MESH)
            for k, (pos, blk) in enumerate(peers)]
        for cp in scatter:
            cp.start()
        for cp in scatter:
            cp.wait_recv()
        acc = p_ref[rows_of(me), :]
        for k in range(N_DEV - 1):
            acc = acc + rbuf[k]
        red[...] = acc
        o_ref[rows_of(me), :] = acc
        gather = [pltpu.make_async_remote_copy(
            src_ref=red, dst_ref=o_ref.at[rows_of(me), :],
            send_sem=send2.at[k], recv_sem=recv2.at[k], device_id=pos, device_id_type=MESH)
            for k, (pos, blk) in enumerate(peers)]
        for cp in gather:
            cp.start()
        for k, (pos, blk) in enumerate(peers):
            pltpu.make_async_remote_copy(
                src_ref=red, dst_ref=o_ref.at[rows_of(blk), :],
                send_sem=send2.at[k], recv_sem=recv2.at[k], device_id=pos,
                device_id_type=MESH).wait_recv()
        for cp in scatter + gather:
            cp.wait_send()

    return pl.pallas_call(
        body, name="all_reduce_small", out_shape=SDS(part.shape, F32),
        in_specs=[VMEM], out_specs=VMEM,
        scratch_shapes=[pltpu.VMEM((N_DEV - 1, rsl, LANES), F32), pltpu.VMEM((rsl, LANES), F32)]
        + [pltpu.SemaphoreType.DMA((N_DEV - 1,))] * 4,
        compiler_params=_params(),
    )(part)


def _conv_rows(ta):
    return _tile(ta, 64, SUBLANES)


def _branch_a_fwd(proj, conv_w_full, conv_b, gn_g, gn_b, seq):
    n_tok = proj.shape[0]
    d_model = conv_b.shape[1]
    ta = _tile(seq, 256, HALO)
    per_seq = seq // ta
    rc = _conv_rows(ta)

    def body(av_ref, ag_ref, gt_ref, avh_ref, agh_ref, cw_ref, cb_ref, gg_ref, gb_ref,
             h3_ref, h1_ref, ext):
        keep = jnp.where(pl.program_id(0) % per_seq == 0, 0.0, 1.0)

        def group(g, carry):
            sl = pl.ds(pl.multiple_of(g * LANES, LANES), LANES)
            ext[0:HALO, :] = avh_ref[:, sl] * _sigmoid(agh_ref[:, sl]) * keep
            ext[HALO:HALO + ta, :] = av_ref[:, sl] * _sigmoid(ag_ref[:, sl])
            for r0 in range(0, ta, rc):
                acc = jnp.broadcast_to(cb_ref[:, sl], (rc, LANES))
                for k in range(CONV_K):
                    acc = acc + ext[pl.ds(r0 + HALO - (CONV_K - 1) + k, rc), :] * cw_ref[k:k + 1, sl]
                h1_ref[pl.ds(r0, rc), sl] = acc
            h1 = h1_ref[:, sl]
            mu = jnp.mean(h1, axis=-1, keepdims=True)
            dlt = h1 - mu
            var = jnp.mean(dlt * dlt, axis=-1, keepdims=True)
            h2 = dlt * lax.rsqrt(var + LN_EPS) * gg_ref[:, sl] + gb_ref[:, sl]
            gate = gt_ref[:, sl]
            h3_ref[:, sl] = (h2 * _sigmoid(h2) * gate * _sigmoid(gate)).astype(BF16)
            return carry

        lax.fori_loop(0, d_model // LANES, group, 0)

    blk = lambda j: pl.BlockSpec((ta, d_model), lambda i: (i, j))
    halo = lambda j: pl.BlockSpec((HALO, d_model), lambda i: (jnp.maximum(i * (ta // HALO) - 1, 0), j))
    row = pl.BlockSpec((1, d_model), lambda i: (0, 0))
    return pl.pallas_call(
        body, name="branch_a_fwd", grid=(n_tok // ta,),
        in_specs=[blk(0), blk(1), blk(2), halo(0), halo(1),
                  pl.BlockSpec((HALO, d_model), lambda i: (0, 0)), row, row, row],
        out_specs=[pl.BlockSpec((ta, d_model), lambda i: (i, 0))] * 2,
        out_shape=[SDS((n_tok, d_model), BF16), SDS((n_tok, d_model), F32)],
        scratch_shapes=[pltpu.VMEM((HALO + ta, LANES), F32)],
        compiler_params=_params(1),
    )(proj, proj, proj, proj, proj, conv_w_full, conv_b, gn_g, gn_b)


def _branch_b_fwd(proj, ln_g, ln_b, w_spatial, b_spatial_t, seq):
    n_tok = proj.shape[0]
    d_model = ln_g.shape[1]
    n_heads = d_model // LANES
    tb = _tile(seq, 256, LANES)

    def body(u_ref, v_ref, bg_ref, lg_ref, lb_ref, ws_ref, bs_ref, s_ref, vn_buf):
        v, _ = _gelu_and_grad(v_ref[...])
        mu = jnp.mean(v, axis=-1, keepdims=True)
        dlt = v - mu
        var = jnp.mean(dlt * dlt, axis=-1, keepdims=True)
        vn_buf[...] = (dlt * lax.rsqrt(var + LN_EPS) * lg_ref[...] + lb_ref[...]).astype(BF16)
        tril = _tril_mask()
        for h in range(n_heads):
            cols = slice(h * LANES, (h + 1) * LANES)
            w_h = jnp.where(tril, ws_ref[h], 0.0).astype(BF16)
            bias = bs_ref[:, h:h + 1]
            for ch in range(tb // LANES):
                rows = slice(ch * LANES, (ch + 1) * LANES)
                mix = _dot(w_h, vn_buf[rows, cols]) + bias
                u, _ = _gelu_and_grad(u_ref[rows, cols])
                gate = bg_ref[rows, cols]
                s_ref[rows, cols] = (u * mix * gate * _sigmoid(gate)).astype(BF16)

    blk = lambda j: pl.BlockSpec((tb, d_model), lambda i: (i, j))
    row = pl.BlockSpec((1, d_model), lambda i: (0, 0))
    return pl.pallas_call(
        body, name="branch_b_fwd", grid=(n_tok // tb,),
        in_specs=[blk(3), blk(4), blk(5), row, row,
                  pl.BlockSpec((n_heads, LANES, LANES), lambda i: (0, 0, 0)),
                  pl.BlockSpec((LANES, n_heads), lambda i: (0, 0))],
        out_specs=pl.BlockSpec((tb, d_model), lambda i: (i, 0)),
        out_shape=SDS((n_tok, d_model), BF16),
        scratch_shapes=[pltpu.VMEM((tb, d_model), BF16)],
        compiler_params=_params(1),
    )(proj, proj, proj, ln_g, ln_b, w_spatial, b_spatial_t)


MID_ROWS = 8


def _mid(h3, s, proj, x2, target, w_pa, w_pb, w_o, b_o, lo_g, lo_b):
    n_tok, d_model = x2.shape
    tm = _tile(n_tok, 256, 16)

    def body(h3_ref, s_ref, ma_ref, mb_ref, x_ref, t_ref, wpa_ref, wpb_ref, wo_ref, bo_ref,
             lg_ref, lb_ref, dproj_ref, dh3_ref, ds_ref, dr_ref, drb_ref, mixed_ref, dya_ref,
             dyb_ref, vec_ref):
        @pl.when(pl.program_id(0) == 0)
        def _():
            vec_ref[...] = jnp.zeros_like(vec_ref)

        ya = _dot(h3_ref[...], wpa_ref[...])
        yb = _dot(s_ref[...], wpb_ref[...])
        ga = _sigmoid(ma_ref[...])
        gb = _sigmoid(mb_ref[...])
        mixed = (ga * ya + gb * yb).astype(BF16)
        mixed_ref[...] = mixed
        r = DEEPNORM_ALPHA * x_ref[...] + _dot(mixed, wo_ref[...]) + bo_ref[...]
        mu = jnp.mean(r, axis=-1, keepdims=True)
        dlt = r - mu
        rstd = lax.rsqrt(jnp.mean(dlt * dlt, axis=-1, keepdims=True) + LN_EPS)
        rhat = dlt * rstd
        diff = rhat * lg_ref[...] + lb_ref[...] - t_ref[...]
        dy = diff * (1.0 / d_model)
        vec_ref[0:1, :] += _colsum(dy * rhat)
        vec_ref[1:2, :] += _colsum(dy)
        vec_ref[5:6, :] += _colsum(diff * diff)
        drh = dy * lg_ref[...]
        dr = rstd * (drh - jnp.mean(drh, axis=-1, keepdims=True)
                     - rhat * jnp.mean(drh * rhat, axis=-1, keepdims=True))
        vec_ref[2:3, :] += _colsum(dr)
        dr_ref[...] = dr
        drb = dr.astype(BF16)
        drb_ref[...] = drb
        dmixed = _dot_tb(drb, wo_ref[...])
        dma = dmixed * ya * ga * (1.0 - ga)
        dmb = dmixed * yb * gb * (1.0 - gb)
        vec_ref[3:4, :] += _colsum(dma)
        vec_ref[4:5, :] += _colsum(dmb)
        dproj_ref[:, 0:d_model] = dma.astype(BF16)
        dproj_ref[:, d_model:2 * d_model] = dmb.astype(BF16)
        dya = (dmixed * ga).astype(BF16)
        dyb = (dmixed * gb).astype(BF16)
        dya_ref[...] = dya
        dyb_ref[...] = dyb
        dh3_ref[...] = _dot_tb(dya, wpa_ref[...])
        ds_ref[...] = _dot_tb(dyb, wpb_ref[...])

    tile = pl.BlockSpec((tm, d_model), lambda i: (i, 0))
    full = pl.BlockSpec((d_model, d_model), lambda i: (0, 0))
    row = pl.BlockSpec((1, d_model), lambda i: (0, 0))
    bf = SDS((n_tok, d_model), BF16)
    f32 = SDS((n_tok, d_model), F32)
    return pl.pallas_call(
        body, name="mid", grid=(n_tok // tm,),
        in_specs=[tile, tile, pl.BlockSpec((tm, d_model), lambda i: (i, 6)),
                  pl.BlockSpec((tm, d_model), lambda i: (i, 7)), tile, tile, full, full, full,
                  row, row, row],
        out_specs=[pl.BlockSpec((tm, 2 * d_model), lambda i: (i, 3)), tile, tile, tile, tile, tile,
                   tile, tile, pl.BlockSpec((MID_ROWS, d_model), lambda i: (0, 0))],
        out_shape=[SDS((n_tok, N_DEV * d_model), BF16), f32, f32, f32, bf, bf, bf, bf,
                   SDS((MID_ROWS, d_model), F32)],
        compiler_params=_params(1),
    )(h3, s, proj, proj, x2, target, w_pa, w_pb, w_o, b_o, lo_g, lo_b)


B_ROWS = 8


def _branch_b_bwd(dproj, proj, d_s, ln_g, ln_b, w_spatial, b_spatial_t, seq):
    n_tok = proj.shape[0]
    d_model = ln_g.shape[1]
    n_heads = d_model // LANES
    tb = _tile(seq, 256, LANES)

    def body(dproj_in, u_ref, v_ref, bg_ref, ds_ref, lg_ref, lb_ref, ws_ref, bs_ref,
             dproj_ref, vec_ref, dws_ref, dbs_ref, vn_buf, dv_buf):
        del dproj_in

        @pl.when(pl.program_id(0) == 0)
        def _():
            vec_ref[...] = jnp.zeros_like(vec_ref)
            dws_ref[...] = jnp.zeros_like(dws_ref)
            dbs_ref[...] = jnp.zeros_like(dbs_ref)

        v, dgelu_v = _gelu_and_grad(v_ref[...])
        mu = jnp.mean(v, axis=-1, keepdims=True)
        dlt = v - mu
        rstd = lax.rsqrt(jnp.mean(dlt * dlt, axis=-1, keepdims=True) + LN_EPS)
        vhat = dlt * rstd
        vn_buf[...] = (vhat * lg_ref[...] + lb_ref[...]).astype(BF16)
        tril = _tril_mask()
        for h in range(n_heads):
            cols = slice(h * LANES, (h + 1) * LANES)
            w_h = jnp.where(tril, ws_ref[h], 0.0).astype(BF16)
            bias = bs_ref[:, h:h + 1]
            for ch in range(tb // LANES):
                rows = slice(ch * LANES, (ch + 1) * LANES)
                vn = vn_buf[rows, cols]
                mix = _dot(w_h, vn) + bias
                u, dgelu_u = _gelu_and_grad(u_ref[rows, cols])
                sg, dsilu = _silu_and_grad(bg_ref[rows, cols])
                dsv = ds_ref[rows, cols]
                du = dsv * mix * sg * dgelu_u
                dbg = dsv * u * mix * dsilu
                dmix = dsv * u * sg
                dmix_bf = dmix.astype(BF16)
                dproj_ref[rows, cols] = du.astype(BF16)
                dproj_ref[rows, 2 * d_model + h * LANES:2 * d_model + (h + 1) * LANES] = dbg.astype(BF16)
                vec_ref[0:1, cols] += _colsum(du)
                vec_ref[2:3, cols] += _colsum(dbg)
                dbs_ref[:, h:h + 1] += jnp.sum(dmix, axis=1, keepdims=True)
                dws_ref[h] += jnp.where(tril, _dot_tb(dmix_bf, vn), 0.0)
                dv_buf[rows, cols] = _dot_ta(w_h, dmix_bf)
        dvn = dv_buf[...]
        vec_ref[3:4, :] += _colsum(dvn * vhat)
        vec_ref[4:5, :] += _colsum(dvn)
        dvh = dvn * lg_ref[...]
        dv = rstd * (dvh - jnp.mean(dvh, axis=-1, keepdims=True)
                     - vhat * jnp.mean(dvh * vhat, axis=-1, keepdims=True)) * dgelu_v
        vec_ref[1:2, :] += _colsum(dv)
        dproj_ref[:, d_model:2 * d_model] = dv.astype(BF16)

    blk = lambda j: pl.BlockSpec((tb, d_model), lambda i: (i, j))
    row = pl.BlockSpec((1, d_model), lambda i: (0, 0))
    return pl.pallas_call(
        body, name="branch_b_bwd", grid=(n_tok // tb,),
        in_specs=[ANY, blk(3), blk(4), blk(5), pl.BlockSpec((tb, d_model), lambda i: (i, 0)), row, row,
                  pl.BlockSpec((n_heads, LANES, LANES), lambda i: (0, 0, 0)),
                  pl.BlockSpec((LANES, n_heads), lambda i: (0, 0))],
        out_specs=[pl.BlockSpec((tb, 3 * d_model), lambda i: (i, 1)),
                   pl.BlockSpec((B_ROWS, d_model), lambda i: (0, 0)),
                   pl.BlockSpec((n_heads, LANES, LANES), lambda i: (0, 0, 0)),
                   pl.BlockSpec((LANES, n_heads), lambda i: (0, 0))],
        out_shape=[SDS(dproj.shape, BF16), SDS((B_ROWS, d_model), F32),
                   SDS((n_heads, LANES, LANES), F32), SDS((LANES, n_heads), F32)],
        scratch_shapes=[pltpu.VMEM((tb, d_model), BF16), pltpu.VMEM((tb, d_model), F32)],
        input_output_aliases={0: 0},
        compiler_params=_params(1),
    )(dproj, proj, proj, proj, d_s, ln_g, ln_b, w_spatial, b_spatial_t)


A1_ROWS = 8


def _branch_a_bwd_norm(dproj, proj, h1, d_h3, gn_g, gn_b, seq):
    n_tok = proj.shape[0]
    d_model = gn_g.shape[1]
    ta = _tile(seq, 256, 16)

    def body(dproj_in, gt_ref, h1_ref, dh3_ref, gg_ref, gb_ref, dproj_ref, dh1_ref, vec_ref):
        del dproj_in

        @pl.when(pl.program_id(0) == 0)
        def _():
            vec_ref[...] = jnp.zeros_like(vec_ref)

        def group(g, carry):
            sl = pl.ds(pl.multiple_of(g * LANES, LANES), LANES)
            h1 = h1_ref[:, sl]
            mu = jnp.mean(h1, axis=-1, keepdims=True)
            dlt = h1 - mu
            rstd = lax.rsqrt(jnp.mean(dlt * dlt, axis=-1, keepdims=True) + LN_EPS)
            nrm = dlt * rstd
            sw, dsw = _silu_and_grad(nrm * gg_ref[:, sl] + gb_ref[:, sl])
            sg, dsg = _silu_and_grad(gt_ref[:, sl])
            dh3 = dh3_ref[:, sl]
            dgate = dh3 * sw * dsg
            dproj_ref[:, sl] = dgate.astype(BF16)
            vec_ref[0:1, sl] += _colsum(dgate)
            dh2 = dh3 * sg * dsw
            vec_ref[1:2, sl] += _colsum(dh2 * nrm)
            vec_ref[2:3, sl] += _colsum(dh2)
            dn = dh2 * gg_ref[:, sl]
            dh1 = rstd * (dn - jnp.mean(dn, axis=-1, keepdims=True)
                          - nrm * jnp.mean(dn * nrm, axis=-1, keepdims=True))
            vec_ref[3:4, sl] += _colsum(dh1)
            dh1_ref[:, sl] = dh1
            return carry

        lax.fori_loop(0, d_model // LANES, group, 0)

    tile = pl.BlockSpec((ta, d_model), lambda i: (i, 0))
    row = pl.BlockSpec((1, d_model), lambda i: (0, 0))
    return pl.pallas_call(
        body, name="branch_a_bwd_norm", grid=(n_tok // ta,),
        in_specs=[ANY, pl.BlockSpec((ta, d_model), lambda i: (i, 2)), tile, tile, row, row],
        out_specs=[pl.BlockSpec((ta, d_model), lambda i: (i, 2)), tile,
                   pl.BlockSpec((A1_ROWS, d_model), lambda i: (0, 0))],
        out_shape=[SDS(dproj.shape, BF16), SDS((n_tok, d_model), F32), SDS((A1_ROWS, d_model), F32)],
        input_output_aliases={0: 0},
        compiler_params=_params(1),
    )(dproj, proj, h1, d_h3, gn_g, gn_b)


A2_ROWS = 8


def _branch_a_bwd_conv(dproj, proj, d_h1, conv_w_full, gp_bf, gp_f32, seq):
    n_tok = proj.shape[0]
    d_model = conv_w_full.shape[1]
    n_groups = d_model // LANES
    ta = _tile(seq, 256, HALO)
    n_tiles = n_tok // ta
    per_seq = seq // ta
    rc = _conv_rows(ta)
    last_halo = n_tok // HALO - 1
    r8 = d_model // N_DEV
    prow = _tile(r8, 32, 16)

    def body(dproj_in, av_ref, ag_ref, avh_ref, agh_ref, dh1_ref, dh1h_ref, cw_ref,
             gpa_bf, gpb_bf, gpo_bf, gpa_f32, gpb_f32, gpo_f32,
             dproj_ref, vec_ref, dcw_ref, opa_ref, opb_ref, opo_ref,
             ext_h0, ext_d, rbuf, own, send_sems, recv_sems, local_sems):
        del dproj_in
        i = pl.program_id(0)
        x, y, c = _mesh_pos()
        me = _block_of((x, y, c))
        peers = _peers()
        gp_bf_refs = [gpa_bf, gpb_bf, gpo_bf]
        gp_f32_refs = [gpa_f32, gpb_f32, gpo_f32]

        def sends():
            return [pltpu.make_async_remote_copy(
                src_ref=gp_bf_refs[a].at[pl.ds(pl.multiple_of(blk * r8, 16), r8), :], dst_ref=rbuf.at[k, a],
                send_sem=send_sems.at[a, k], recv_sem=recv_sems.at[a, k], device_id=pos, device_id_type=MESH)
                for k, (pos, blk) in enumerate(peers) for a in range(3)]

        def own_rows():
            return [pltpu.make_async_copy(gp_f32_refs[a].at[pl.ds(pl.multiple_of(me * r8, 8), r8), :],
                                          own.at[a], local_sems.at[a]) for a in range(3)]

        @pl.when(i == 0)
        def _():
            vec_ref[...] = jnp.zeros_like(vec_ref)
            dcw_ref[...] = jnp.zeros_like(dcw_ref)
            for cp in sends() + own_rows():
                cp.start()

        keep_past = jnp.where(i % per_seq == 0, 0.0, 1.0)
        keep_next = jnp.where(i % per_seq == per_seq - 1, 0.0, 1.0)

        def group(g, carry):
            sl = pl.ds(pl.multiple_of(g * LANES, LANES), LANES)
            av = av_ref[:, sl]
            sig = _sigmoid(ag_ref[:, sl])
            ext_h0[0:HALO, :] = avh_ref[:, sl] * _sigmoid(agh_ref[:, sl]) * keep_past
            ext_h0[HALO:HALO + ta, :] = av * sig
            ext_d[0:ta, :] = dh1_ref[:, sl]
            ext_d[ta:ta + HALO, :] = dh1h_ref[:, sl] * keep_next
            for r0 in range(0, ta, rc):
                dh1 = ext_d[pl.ds(r0, rc), :]
                acc = jnp.zeros((rc, LANES), F32)
                for k in range(CONV_K):
                    acc = acc + ext_d[pl.ds(r0 + CONV_K - 1 - k, rc), :] * cw_ref[k:k + 1, sl]
                    prod = dh1 * ext_h0[pl.ds(r0 + HALO - (CONV_K - 1) + k, rc), :]
                    dcw_ref[g, k] += jnp.sum(prod.reshape(rc // SUBLANES, SUBLANES, LANES), axis=0)
                rows = pl.ds(r0, rc)
                sig_r = sig[r0:r0 + rc]
                dav = acc * sig_r
                dag = acc * av[r0:r0 + rc] * sig_r * (1.0 - sig_r)
                dproj_ref[rows, sl] = dav.astype(BF16)
                dproj_ref[rows, pl.ds(pl.multiple_of(d_model + g * LANES, LANES), LANES)] = dag.astype(BF16)
                vec_ref[0:1, sl] += _colsum(dav)
                vec_ref[1:2, sl] += _colsum(dag)
            return carry

        lax.fori_loop(0, n_groups, group, 0)

        @pl.when(i == n_tiles - 1)
        def _():
            for cp in own_rows():
                cp.wait()
            for cp in sends():
                cp.wait_recv()
            for a, o in enumerate([opa_ref, opb_ref, opo_ref]):
                for q in range(r8 // prow):
                    r = pl.ds(q * prow, prow)
                    tot = own[a, r, :]
                    for k in range(N_DEV - 1):
                        tot = tot + rbuf[k, a, r, :].astype(F32)
                    o[r, :] = tot
            for cp in sends():
                cp.wait_send()

    blk = lambda j: pl.BlockSpec((ta, d_model), lambda i: (i, j))
    halo = lambda j: pl.BlockSpec((HALO, d_model), lambda i: (jnp.maximum(i * (ta // HALO) - 1, 0), j))
    shard = pl.BlockSpec((r8, d_model), lambda i: (0, 0))
    return pl.pallas_call(
        body, name="branch_a_bwd_conv", grid=(n_tiles,),
        in_specs=[ANY, blk(0), blk(1), halo(0), halo(1), pl.BlockSpec((ta, d_model), lambda i: (i, 0)),
                  pl.BlockSpec((HALO, d_model), lambda i: (jnp.minimum((i + 1) * (ta // HALO), last_halo), 0)),
                  pl.BlockSpec((HALO, d_model), lambda i: (0, 0))] + [ANY] * 6,
        out_specs=[pl.BlockSpec((ta, 2 * d_model), lambda i: (i, 0)),
                   pl.BlockSpec((A2_ROWS, d_model), lambda i: (0, 0)),
                   pl.BlockSpec((n_groups, HALO, SUBLANES, LANES), lambda i: (0, 0, 0, 0)),
                   shard, shard, shard],
        out_shape=[SDS(dproj.shape, BF16), SDS((A2_ROWS, d_model), F32),
                   SDS((n_groups, HALO, SUBLANES, LANES), F32)] + [SDS((r8, d_model), F32)] * 3,
        scratch_shapes=[pltpu.VMEM((HALO + ta, LANES), F32), pltpu.VMEM((ta + HALO, LANES), F32),
                        pltpu.VMEM((N_DEV - 1, 3, r8, d_model), BF16), pltpu.VMEM((3, r8, d_model), F32),
                        pltpu.SemaphoreType.DMA((3, 7)), pltpu.SemaphoreType.DMA((3, 7)),
                        pltpu.SemaphoreType.DMA((3,))],
        input_output_aliases={0: 0},
        compiler_params=_params(1),
    )(dproj, proj, proj, proj, proj, d_h1, d_h1, conv_w_full, *gp_bf, *gp_f32)


def _weight_grad(lhs, rhs, name):
    n_tok, d_model = lhs.shape
    tk = _tile(n_tok, 2048, 16)
    n_k = n_tok // tk

    def body(a_ref, g_ref, o_ref, ob_ref):
        part = _dot_ta(a_ref[...], g_ref[...])

        @pl.when(pl.program_id(0) == 0)
        def _():
            o_ref[...] = part

        @pl.when(pl.program_id(0) != 0)
        def _():
            o_ref[...] += part

        @pl.when(pl.program_id(0) == n_k - 1)
        def _():
            ob_ref[...] = o_ref[...].astype(BF16)

    tile = pl.BlockSpec((tk, d_model), lambda i: (i, 0))
    out = pl.BlockSpec((d_model, d_model), lambda i: (0, 0))
    return pl.pallas_call(
        body, name=name, grid=(n_k,), in_specs=[tile, tile], out_specs=[out, out],
        out_shape=[SDS((d_model, d_model), F32), SDS((d_model, d_model), BF16)],
        compiler_params=_params(1),
    )(lhs, rhs)


def _grad_w_in_reduce_scatter(xt_bf, dproj, blk_order):
    n_tok = dproj.shape[0]
    d_model = xt_bf.shape[0]
    dh = d_model // 2
    n_units = 2 * N_DEV

    def body(ord_ref, a_ref, g_ref, o_ref, acc, fb, sbuf, gbuf, rfin,
             send_f, send_s, recv_g, recv_f, out_sems):
        del ord_ref
        u = pl.program_id(0)
        s = u // 2
        hf = u % 2
        x, y, c = _mesh_pos()
        sibling = (x, y, 1 - c)
        chips = [(1 - x, y), (x, 1 - y), (1 - x, 1 - y)]
        pair = (s // 2 + 2) % 3

        def feed(p, half):
            return pltpu.make_async_remote_copy(
                src_ref=fb.at[half], dst_ref=gbuf.at[p, half], send_sem=send_f.at[p, half],
                recv_sem=recv_g.at[p, half], device_id=sibling, device_id_type=MESH)

        def feed_sibling(half):
            return pltpu.make_async_remote_copy(
                src_ref=fb.at[half], dst_ref=rfin.at[0, half], send_sem=send_f.at[3, half],
                recv_sem=recv_f.at[0, half], device_id=sibling, device_id_type=MESH)

        def chip_sum(p, half):
            return pltpu.make_async_remote_copy(
                src_ref=sbuf.at[p, half], dst_ref=rfin.at[1 + p, half], send_sem=send_s.at[p, half],
                recv_sem=recv_f.at[1 + p, half], device_id=(*chips[p], c), device_id_type=MESH)

        def out_copy(half):
            return pltpu.make_async_copy(acc.at[half], o_ref.at[:, pl.ds(half * dh, dh)], out_sems.at[half])

        def partial_sum():
            return _dot(a_ref[:, 0:n_tok], g_ref[...])

        feeds = {0: (2, None), 2: (0, 2), 4: (1, 0), 6: (None, 1)}
        for step, (p, prev) in feeds.items():
            for half in range(2):
                @pl.when(u == 2 * step + half)
                def _(p=p, prev=prev, half=half):
                    if prev is not None:
                        feed(prev, half).wait_send()
        for step, p in {1: 2, 3: 0, 5: 1}.items():
            for half in range(2):
                @pl.when(u == 2 * step + half)
                def _(p=p, half=half):
                    feed(p, half).wait_recv()
        for half in range(2):
            @pl.when(u == 14 + half)
            def _(half=half):
                feed_sibling(half).wait_recv()
                for p in range(3):
                    chip_sum(p, half).wait_recv()

        @pl.when(jnp.logical_and(s % 2 == 0, s < 7))
        def _():
            fb[hf] = partial_sum().astype(BF16)

        @pl.when(jnp.logical_and(s % 2 == 1, s < 7))
        def _():
            sbuf[pair, hf] = (partial_sum() + gbuf[pair, hf].astype(F32)).astype(BF16)

        @pl.when(s == 7)
        def _():
            acc[hf] = (partial_sum() + rfin[0, hf].astype(F32) + rfin[1, hf].astype(F32)
                       + rfin[2, hf].astype(F32) + rfin[3, hf].astype(F32))

        for step, (p, prev) in feeds.items():
            for half in range(2):
                @pl.when(u == 2 * step + half)
                def _(p=p, half=half):
                    (feed_sibling(half) if p is None else feed(p, half)).start()
        for step, p in {1: 2, 3: 0, 5: 1}.items():
            for half in range(2):
                @pl.when(u == 2 * step + half)
                def _(p=p, half=half):
                    chip_sum(p, half).start()

        @pl.when(u == 14)
        def _():
            out_copy(0).start()

        @pl.when(u == 15)
        def _():
            out_copy(1).start()
            for half in range(2):
                feed_sibling(half).wait_send()
                for p in range(3):
                    chip_sum(p, half).wait_send()
                out_copy(half).wait()

    grid_spec = pltpu.PrefetchScalarGridSpec(
        num_scalar_prefetch=1, grid=(n_units,),
        in_specs=[VMEM, pl.BlockSpec((n_tok, dh), lambda u, o: (0, 2 * o[u // 2] + u % 2))],
        out_specs=ANY,
        scratch_shapes=[pltpu.VMEM((2, d_model, dh), F32), pltpu.VMEM((2, d_model, dh), BF16),
                        pltpu.VMEM((3, 2, d_model, dh), BF16), pltpu.VMEM((3, 2, d_model, dh), BF16),
                        pltpu.VMEM((4, 2, d_model, dh), BF16),
                        pltpu.SemaphoreType.DMA((4, 2)), pltpu.SemaphoreType.DMA((3, 2)),
                        pltpu.SemaphoreType.DMA((3, 2)), pltpu.SemaphoreType.DMA((4, 2)),
                        pltpu.SemaphoreType.DMA((2,))])
    return pl.pallas_call(
        body, name="grad_w_in_reduce_scatter", grid_spec=grid_spec,
        out_shape=SDS((d_model, d_model), F32), compiler_params=_params(1),
    )(blk_order, xt_bf, dproj)


def _grad_x(dproj, w_all, dr):
    n_tok, d_model = dr.shape
    tm = _tile(n_tok, 512, 16)

    def body(dp_ref, w_ref, dr_ref, o_ref):
        acc = _dot_tb(dp_ref[:, 0:d_model], w_ref[0])
        for j in range(1, N_DEV):
            acc = acc + _dot_tb(dp_ref[:, j * d_model:(j + 1) * d_model], w_ref[j])
        o_ref[...] = acc + DEEPNORM_ALPHA * dr_ref[...]

    return pl.pallas_call(
        body, name="grad_x", grid=(n_tok // tm,),
        in_specs=[pl.BlockSpec((tm, N_DEV * d_model), lambda i: (i, 0)), VMEM,
                  pl.BlockSpec((tm, d_model), lambda i: (i, 0))],
        out_specs=pl.BlockSpec((tm, d_model), lambda i: (i, 0)),
        out_shape=SDS((n_tok, d_model), F32),
        compiler_params=_params(1),
    )(dproj, w_all, dr)


def _adamw_math(w, g, m, v):
    m = ADAM_B1 * m + (1.0 - ADAM_B1) * g
    v = ADAM_B2 * v + (1.0 - ADAM_B2) * (g * g)
    m_hat = m / (1.0 - ADAM_B1 ** ADAM_STEP)
    v_hat = v / (1.0 - ADAM_B2 ** ADAM_STEP)
    delta = -ADAM_LR * (m_hat / (jnp.sqrt(v_hat) + ADAM_EPS) + ADAM_WD * w)
    return delta, m, v


def _adamw_tiled(w, g, m, v):
    n_rows, n_cols = w.shape
    tr = _tile(n_rows, 256, SUBLANES)

    def body(w_ref, g_ref, m_ref, v_ref, d_ref, mo_ref, vo_ref):
        d_ref[...], mo_ref[...], vo_ref[...] = _adamw_math(w_ref[...], g_ref[...], m_ref[...], v_ref[...])

    tile = pl.BlockSpec((tr, n_cols), lambda i: (i, 0))
    return pl.pallas_call(
        body, name="adamw_w_in", grid=(n_rows // tr,), in_specs=[tile] * 4, out_specs=[tile] * 3,
        out_shape=[SDS(w.shape, F32)] * 3, compiler_params=_params(1),
    )(w, g, m, v)


def _adamw_many(groups):
    n = len(groups)

    def body(*refs):
        ins, outs = refs[:4 * n], refs[4 * n:]
        for p in range(n):
            w_ref, g_ref, m_ref, v_ref = ins[4 * p:4 * p + 4]
            d, m, v = _adamw_math(w_ref[...], g_ref[...], m_ref[...], v_ref[...])
            outs[3 * p][...] = d
            outs[3 * p + 1][...] = m
            outs[3 * p + 2][...] = v

    flat = [a for grp in groups for a in grp]
    out_shape = [SDS(grp[0].shape, F32) for grp in groups for _ in range(3)]
    res = pl.pallas_call(
        body, name="adamw_small", in_specs=[VMEM] * (4 * n), out_specs=[VMEM] * (3 * n),
        out_shape=out_shape, compiler_params=_params(),
    )(*flat)
    return [tuple(res[3 * p:3 * p + 3]) for p in range(n)]


def _as_rows(a):
    return a.reshape(-1, LANES)


def kernel(x, w_in, b_in, conv_w, conv_b, gn_g, gn_b, ln_v_g, ln_v_b, w_spatial, b_spatial, w_pa, w_pb, w_o, b_o, ln_out_g, ln_out_b, loss_target, m_w_in, m_b_in, m_conv_w, m_conv_b, m_gn_g, m_gn_b, m_ln_v_g, m_ln_v_b, m_w_spatial, m_b_spatial, m_w_pa, m_w_pb, m_w_o, m_b_o, m_ln_out_g, m_ln_out_b, v_w_in, v_b_in, v_conv_w, v_conv_b, v_gn_g, v_gn_b, v_ln_v_g, v_ln_v_b, v_w_spatial, v_b_spatial, v_w_pa, v_w_pb, v_w_o, v_b_o, v_ln_out_g, v_ln_out_b):
    n_batch, seq, d_model = x.shape
    n_tok = n_batch * seq
    n_heads = d_model // LANES
    dc = conv_w.shape[1]
    me = 4 * lax.axis_index("x") + 2 * lax.axis_index("y") + lax.axis_index("c")
    row = lambda a: a.reshape(1, d_model)

    x2 = x.reshape(n_tok, d_model)
    target2 = loss_target.reshape(n_tok, d_model)
    b_spatial_t = b_spatial.T

    first = jnp.where(lax.axis_index("c") == 1, 4, 2)
    second = 6 - first
    ag_rel = jnp.stack([0 * first, 0 * first + 1, first, second + 1, second, first + 1, 0 * first + 6, 0 * first + 7])
    ag_blocks = jnp.bitwise_xor(me, ag_rel).astype(jnp.int32)
    proj, xt_bf, w_all, wp_all, cw_all = _proj_all_gather(
        x2, w_in, w_pa, w_pb, w_o, conv_w, b_in.reshape(N_DEV, 1, d_model), ag_blocks)
    wp_full = [wp_all[:, a].reshape(d_model, d_model) for a in range(3)]
    conv_w_full = jnp.pad(cw_all.transpose(1, 0, 2).reshape(CONV_K, d_model), ((0, HALO - CONV_K), (0, 0)))

    h3, h1 = _branch_a_fwd(proj, conv_w_full, row(conv_b), row(gn_g), row(gn_b), seq)
    s = _branch_b_fwd(proj, row(ln_v_g), row(ln_v_b), w_spatial, b_spatial_t, seq)

    dproj, d_h3, d_s, dr, dr_bf, mixed, d_ya, d_yb, vec_mid = _mid(
        h3, s, proj, x2, target2, *wp_full, row(b_o), row(ln_out_g), row(ln_out_b))
    loss = lax.psum(jnp.sum(vec_mid[5]) * (0.5 / d_model), ("x", "y", "c"))

    dproj, vec_b, d_ws, d_bs_t = _branch_b_bwd(dproj, proj, d_s, row(ln_v_g), row(ln_v_b), w_spatial, b_spatial_t, seq)
    dproj, d_h1, vec_a1 = _branch_a_bwd_norm(dproj, proj, h1, d_h3, row(gn_g), row(gn_b), seq)
    gp = [_weight_grad(lhs, rhs, name) for lhs, rhs, name in
          ((h3, d_ya, "grad_w_pa"), (s, d_yb, "grad_w_pb"), (mixed, dr_bf, "grad_w_o"))]
    dproj, vec_a2, d_cw8, g_w_pa, g_w_pb, g_w_o = _branch_a_bwd_conv(
        dproj, proj, d_h1, conv_w_full, [g[1] for g in gp], [g[0] for g in gp], seq)

    rs_blocks = jnp.bitwise_xor(me, jnp.array(RS_ORDER, jnp.int32))
    g_w_in = _grad_w_in_reduce_scatter(xt_bf, dproj, rs_blocks)
    grad_x = _grad_x(dproj, w_all, dr).reshape(x.shape)

    d_cw = jnp.sum(d_cw8, axis=2)
    pieces = [
        _as_rows(jnp.concatenate([vec_a2[0:2], vec_a1[0:1], vec_b[0:3], vec_mid[3:5]], axis=0)),
        _as_rows(jnp.concatenate([vec_a1[3:4], vec_a1[1:3], vec_b[3:5], vec_mid[2:3], vec_mid[0:2]], axis=0)),
        _as_rows(d_bs_t.T), _as_rows(d_ws), _as_rows(d_cw),
    ]
    n_rows = sum(p.shape[0] for p in pieces)
    pad_rows = -n_rows % (N_DEV * SUBLANES)
    small = _all_reduce_small(jnp.concatenate(pieces + [jnp.zeros((pad_rows, LANES), F32)], axis=0))
    g_rows = d_model // LANES
    o0 = N_DEV * g_rows
    g_b_in = small[0:o0].reshape(N_DEV * d_model)
    vecs = [small[o0 + a * g_rows:o0 + (a + 1) * g_rows].reshape(d_model) for a in range(8)]
    g_conv_b, g_gn_g, g_gn_b, g_ln_v_g, g_ln_v_b, g_b_o, g_ln_out_g, g_ln_out_b = vecs
    o1 = o0 + 8 * g_rows
    g_b_spatial = small[o1:o1 + n_heads].reshape(n_heads, LANES)
    o2 = o1 + n_heads
    g_w_spatial = small[o2:o2 + n_heads * LANES].reshape(n_heads, LANES, LANES)
    o3 = o2 + n_heads * LANES
    g_cw_full = small[o3:o3 + n_heads * HALO].reshape(n_heads, HALO, LANES).transpose(1, 0, 2).reshape(HALO, d_model)
    g_conv_w = lax.dynamic_slice(g_cw_full, (0, me * dc), (CONV_K, dc))

    d_w_in, nm_w_in, nv_w_in = _adamw_tiled(w_in, g_w_in, m_w_in, v_w_in)
    two_d = lambda a: a.reshape(-1, a.shape[-1]) if a.ndim != 1 else (
        a.reshape(-1, LANES) if a.shape[0] % LANES == 0 else a.reshape(1, -1))
    names = ["b_in", "conv_w", "conv_b", "gn_g", "gn_b", "ln_v_g", "ln_v_b", "w_spatial", "b_spatial",
             "w_pa", "w_pb", "w_o", "b_o", "ln_out_g", "ln_out_b"]
    ws = dict(b_in=b_in, conv_w=conv_w, conv_b=conv_b, gn_g=gn_g, gn_b=gn_b, ln_v_g=ln_v_g, ln_v_b=ln_v_b,
              w_spatial=w_spatial, b_spatial=b_spatial, w_pa=w_pa, w_pb=w_pb, w_o=w_o, b_o=b_o,
              ln_out_g=ln_out_g, ln_out_b=ln_out_b)
    gs = dict(b_in=g_b_in, conv_w=g_conv_w, conv_b=g_conv_b, gn_g=g_gn_g, gn_b=g_gn_b, ln_v_g=g_ln_v_g,
              ln_v_b=g_ln_v_b, w_spatial=g_w_spatial, b_spatial=g_b_spatial, w_pa=g_w_pa, w_pb=g_w_pb,
              w_o=g_w_o, b_o=g_b_o, ln_out_g=g_ln_out_g, ln_out_b=g_ln_out_b)
    ms = dict(b_in=m_b_in, conv_w=m_conv_w, conv_b=m_conv_b, gn_g=m_gn_g, gn_b=m_gn_b, ln_v_g=m_ln_v_g,
              ln_v_b=m_ln_v_b, w_spatial=m_w_spatial, b_spatial=m_b_spatial, w_pa=m_w_pa, w_pb=m_w_pb,
              w_o=m_w_o, b_o=m_b_o, ln_out_g=m_ln_out_g, ln_out_b=m_ln_out_b)
    vs = dict(b_in=v_b_in, conv_w=v_conv_w, conv_b=v_conv_b, gn_g=v_gn_g, gn_b=v_gn_b, ln_v_g=v_ln_v_g,
              ln_v_b=v_ln_v_b, w_spatial=v_w_spatial, b_spatial=v_b_spatial, w_pa=v_w_pa, w_pb=v_w_pb,
              w_o=v_w_o, b_o=v_b_o, ln_out_g=v_ln_out_g, ln_out_b=v_ln_out_b)
    upd = _adamw_many([tuple(two_d(d[n]) for d in (ws, gs, ms, vs)) for n in names])
    delta = {n: u[0].reshape(ws[n].shape) for n, u in zip(names, upd)}
    new_m = {n: u[1].reshape(ws[n].shape) for n, u in zip(names, upd)}
    new_v = {n: u[2].reshape(ws[n].shape) for n, u in zip(names, upd)}
    gs["w_in"], delta["w_in"], new_m["w_in"], new_v["w_in"] = g_w_in, d_w_in, nm_w_in, nv_w_in

    order = ["w_in"] + names
    return (loss, grad_x, *[gs[n] for n in order], *[delta[n] for n in order],
            *[new_m[n] for n in order], *[new_v[n] for n in order])
```

```python
import jax
import jax.numpy as jnp
from jax import lax
from jax.experimental import pallas as pl
from jax.experimental.pallas import tpu as pltpu

F32 = jnp.float32
BF16 = jnp.bfloat16
SDS = jax.ShapeDtypeStruct

N_DEV = 8
LANES = 128
SUBLANES = 8
CONV_K = 31
HALO = 32
LN_EPS = 1e-5
DEEPNORM_ALPHA = 2.0 ** 0.25
ADAM_LR, ADAM_B1, ADAM_B2, ADAM_EPS, ADAM_WD, ADAM_STEP = 0.001, 0.9, 0.999, 1e-08, 0.01, 10
GELU_C = 0.7978845608028654
GELU_A = 0.044715
VMEM_LIMIT = 56 * 1024 * 1024
MESH = pl.DeviceIdType.MESH
ANY = pl.BlockSpec(memory_space=pl.ANY)
VMEM = pl.BlockSpec(memory_space=pltpu.VMEM)


def _params(n_grid=0):
    sem = ("arbitrary",) * n_grid if n_grid else None
    return pltpu.CompilerParams(dimension_semantics=sem, vmem_limit_bytes=VMEM_LIMIT)


def _tile(n, pref, mult):
    t = min(n, pref)
    while n % t or t % mult:
        t -= 1
    return t


def _colsum(v):
    return jnp.sum(v, axis=0, keepdims=True)


def _sigmoid(v):
    return jax.nn.sigmoid(v)


def _silu_and_grad(v):
    s = _sigmoid(v)
    return v * s, s * (1.0 + v * (1.0 - s))


def _gelu_and_grad(v):
    inner = GELU_C * (v + GELU_A * v * v * v)
    th = jnp.tanh(inner)
    val = 0.5 * v * (1.0 + th)
    grad = 0.5 * (1.0 + th) + 0.5 * v * (1.0 - th * th) * GELU_C * (1.0 + 3.0 * GELU_A * v * v)
    return val, grad


def _tril_mask():
    r = lax.broadcasted_iota(jnp.int32, (LANES, LANES), 0)
    c = lax.broadcasted_iota(jnp.int32, (LANES, LANES), 1)
    return c <= r


def _dot(a, b):
    return jnp.dot(a, b, preferred_element_type=F32)


def _dot_tb(a, b):
    return lax.dot_general(a, b, (((1,), (1,)), ((), ())), preferred_element_type=F32)


def _dot_ta(a, b):
    return lax.dot_general(a, b, (((0,), (0,)), ((), ())), preferred_element_type=F32)


def _mesh_pos():
    return lax.axis_index("x"), lax.axis_index("y"), lax.axis_index("c")


def _block_of(pos):
    return 4 * pos[0] + 2 * pos[1] + pos[2]


def _peers():
    x, y, c = _mesh_pos()
    out = []
    for k in range(1, N_DEV):
        pos = (1 - x if k & 4 else x, 1 - y if k & 2 else y, 1 - c if k & 1 else c)
        out.append((pos, _block_of(pos)))
    return out


def _proj_all_gather(x2, w_in, w_pa, w_pb, w_o, conv_w, b_in3, blk_order):
    n_tok, d_model = x2.shape
    r8 = w_pa.shape[0]
    kc, dc = conv_w.shape
    tm = _tile(n_tok, 1024, LANES)
    n_t = n_tok // tm

    def body(ord_ref, x_ref, b_ref, win_ref, wpa_ref, wpb_ref, wo_ref, cw_ref,
             proj_ref, xt_ref, wall_ref, wp_ref, cwall_ref,
             wbuf, st_p, send_sems, recv_sems, local_sems, wall_sems):
        s = pl.program_id(0)
        t = pl.program_id(1)
        x, y, c = _mesh_pos()
        me = (x, y, c)
        sibling = (x, y, 1 - c)
        n1 = (jnp.bitwise_xor(x, c), jnp.bitwise_xor(y, 1 - c))
        n2 = (jnp.bitwise_xor(x, 1 - c), jnp.bitwise_xor(y, c))
        dg = (1 - x, 1 - y)
        outs = [wbuf, wp_ref, cwall_ref]
        srcs = [None, st_p, cw_ref]
        consumed = [me, sibling, (*n1, c), (*n2, 1 - c), (*n2, c), (*n1, 1 - c), (*dg, c), (*dg, 1 - c)]
        leaves = [(me, sibling), (me, (*n1, c)), (me, (*n2, c)), ((*n1, c), (*n2, c)),
                  ((*n1, c), sibling), ((*n2, c), sibling), ((*dg, c), sibling)]
        lands = [sibling, (*n1, c), (*n2, c), (*dg, c), (*n2, 1 - c), (*n1, 1 - c), (*dg, 1 - c)]

        def slot(o, pos):
            return o.at[_block_of(pos)]

        def copy(a, k, block, to, src=None):
            o = outs[a]
            return pltpu.make_async_remote_copy(
                src_ref=slot(o, block) if src is None else src, dst_ref=slot(o, block),
                send_sem=send_sems.at[a, k], recv_sem=recv_sems.at[a, k],
                device_id=to, device_id_type=MESH)

        def send(a, k):
            block, to = leaves[k]
            return copy(a, k, block, to, src=srcs[a] if k < 3 else None)

        def recv(a, k):
            return copy(a, k, lands[k], me)

        def local_copies():
            return [pltpu.make_async_copy(srcs[a], slot(outs[a], me), local_sems.at[a]) for a in (1, 2)]

        def to_hbm(step):
            return pltpu.make_async_copy(slot(wbuf, consumed[step]), slot(wall_ref, consumed[step]),
                                         wall_sems.at[step])

        def at_step(step):
            return pl.when(jnp.logical_and(s == step, t == 0))

        @at_step(0)
        def _():
            slot(wbuf, me)[...] = win_ref[...].astype(BF16)
            st_p[0] = wpa_ref[...].astype(BF16)
            st_p[1] = wpb_ref[...].astype(BF16)
            st_p[2] = wo_ref[...].astype(BF16)
            for a in range(3):
                send(a, 0).start()
                send(a, 1).start()
            for cp in local_copies():
                cp.start()
            to_hbm(0).start()

        @at_step(1)
        def _():
            recv(0, 0).wait_recv()
            to_hbm(1).start()

        for rnd in range(3):
            @at_step(2 + 2 * rnd)
            def _(rnd=rnd):
                if rnd == 0:
                    for a in range(3):
                        send(a, 2).start()
                recv(0, 1 + rnd).wait_recv()
                if rnd == 0:
                    send(0, 3).start()
                send(0, 4 + rnd).start()
                to_hbm(2 + 2 * rnd).start()

            @at_step(3 + 2 * rnd)
            def _(rnd=rnd):
                for a in (1, 2):
                    recv(a, 1 + rnd).wait_recv()
                    if rnd == 0:
                        send(a, 3).start()
                    send(a, 4 + rnd).start()
                recv(0, 4 + rnd).wait_recv()
                to_hbm(3 + 2 * rnd).start()

        xb = x_ref[...].astype(BF16)
        xt_ref[...] = xb.T
        proj_ref[...] = _dot(xb, wbuf[ord_ref[s]]) + b_ref[...]

        @pl.when(jnp.logical_and(s == N_DEV - 1, t == n_t - 1))
        def _():
            for a in (1, 2):
                for k in (0, 4, 5, 6):
                    recv(a, k).wait_recv()
            for a in range(3):
                for k in range(7):
                    send(a, k).wait_send()
            for cp in local_copies() + [to_hbm(step) for step in range(N_DEV)]:
                cp.wait()

    grid_spec = pltpu.PrefetchScalarGridSpec(
        num_scalar_prefetch=1, grid=(N_DEV, n_t),
        in_specs=[pl.BlockSpec((tm, d_model), lambda s, t, o: (t, 0)),
                  pl.BlockSpec((None, 1, d_model), lambda s, t, o: (o[s], 0, 0)),
                  VMEM, VMEM, VMEM, VMEM, VMEM],
        out_specs=[pl.BlockSpec((tm, d_model), lambda s, t, o: (t, o[s])),
                   pl.BlockSpec((d_model, tm), lambda s, t, o: (0, jnp.where(s == 0, t, n_t))),
                   ANY, ANY, ANY],
        scratch_shapes=[pltpu.VMEM((N_DEV, d_model, d_model), BF16), pltpu.VMEM((3, r8, d_model), BF16),
                        pltpu.SemaphoreType.DMA((3, 7)), pltpu.SemaphoreType.DMA((3, 7)),
                        pltpu.SemaphoreType.DMA((3,)), pltpu.SemaphoreType.DMA((N_DEV,))])
    return pl.pallas_call(
        body, name="proj_all_gather", grid_spec=grid_spec,
        out_shape=[SDS((n_tok, N_DEV * d_model), F32), SDS((d_model, (n_t + 1) * tm), BF16),
                   SDS((N_DEV, d_model, d_model), BF16), SDS((N_DEV, 3, r8, d_model), BF16),
                   SDS((N_DEV, kc, dc), F32)],
        compiler_params=_params(2),
    )(blk_order, x2, b_in3, w_in, w_pa, w_pb, w_o, conv_w)


def _all_reduce_small(part):
    n_rows = part.shape[0]
    rsl = n_rows // N_DEV

    def body(p_ref, o_ref, rbuf, red, send1, recv1, send2, recv2):
        x, y, c = _mesh_pos()
        me = _block_of((x, y, c))
        peers = _peers()

        def rows_of(blk):
            return pl.ds(pl.multiple_of(blk * rsl, SUBLANES), rsl)

        scatter = [pltpu.make_async_remote_copy(
            src_ref=p_ref.at[rows_of(blk), :], dst_ref=rbuf.at[k],
            send_sem=send1.at[k], recv_sem=recv1.at[k], device_id=pos, device_id_type=MESH)
            for k, (pos, blk) in enumerate(peers)]
        for cp in scatter:
            cp.start()
        for cp in scatter:
            cp.wait_recv()
        acc = p_ref[rows_of(me), :]
        for k in range(N_DEV - 1):
            acc = acc + rbuf[k]
        red[...] = acc
        o_ref[rows_of(me), :] = acc
        gather = [pltpu.make_async_remote_copy(
            src_ref=red, dst_ref=o_ref.at[rows_of(me), :],
            send_sem=send2.at[k], recv_sem=recv2.at[k], device_id=pos, device_id_type=MESH)
            for k, (pos, blk) in enumerate(peers)]
        for cp in gather:
            cp.start()
        for k, (pos, blk) in enumerate(peers):
            pltpu.make_async_remote_copy(
                src_ref=red, dst_ref=o_ref.at[rows_of(blk), :],
                send_sem=send2.at[k], recv_sem=recv2.at[k], device_id=pos,
                device_id_type=MESH).wait_recv()
        for cp in scatter + gather:
            cp.wait_send()

    return pl.pallas_call(
        body, name="all_reduce_small", out_shape=SDS(part.shape, F32),
        in_specs=[VMEM], out_specs=VMEM,
        scratch_shapes=[pltpu.VMEM((N_DEV - 1, rsl, LANES), F32), pltpu.VMEM((rsl, LANES), F32)]
        + [pltpu.SemaphoreType.DMA((N_DEV - 1,))] * 4,
        compiler_params=_params(),
    )(part)


def _conv_rows(ta):
    return _tile(ta, 64, SUBLANES)


def _branch_a_fwd(proj, conv_w_full, conv_b, gn_g, gn_b, seq):
    n_tok = proj.shape[0]
    d_model = conv_b.shape[1]
    ta = _tile(seq, 256, HALO)
    per_seq = seq // ta
    rc = _conv_rows(ta)

    def body(av_ref, ag_ref, gt_ref, avh_ref, agh_ref, cw_ref, cb_ref, gg_ref, gb_ref,
             h3_ref, h1_ref, ext):
        keep = jnp.where(pl.program_id(0) % per_seq == 0, 0.0, 1.0)

        def group(g, carry):
            sl = pl.ds(pl.multiple_of(g * LANES, LANES), LANES)
            ext[0:HALO, :] = avh_ref[:, sl] * _sigmoid(agh_ref[:, sl]) * keep
            ext[HALO:HALO + ta, :] = av_ref[:, sl] * _sigmoid(ag_ref[:, sl])
            for r0 in range(0, ta, rc):
                acc = jnp.broadcast_to(cb_ref[:, sl], (rc, LANES))
                for k in range(CONV_K):
                    acc = acc + ext[pl.ds(r0 + HALO - (CONV_K - 1) + k, rc), :] * cw_ref[k:k + 1, sl]
                h1_ref[pl.ds(r0, rc), sl] = acc
            h1 = h1_ref[:, sl]
            mu = jnp.mean(h1, axis=-1, keepdims=True)
            dlt = h1 - mu
            var = jnp.mean(dlt * dlt, axis=-1, keepdims=True)
            h2 = dlt * lax.rsqrt(var + LN_EPS) * gg_ref[:, sl] + gb_ref[:, sl]
            gate = gt_ref[:, sl]
            h3_ref[:, sl] = (h2 * _sigmoid(h2) * gate * _sigmoid(gate)).astype(BF16)
            return carry

        lax.fori_loop(0, d_model // LANES, group, 0, unroll=True)

    blk = lambda j: pl.BlockSpec((ta, d_model), lambda i: (i, j))
    halo = lambda j: pl.BlockSpec((HALO, d_model), lambda i: (jnp.maximum(i * (ta // HALO) - 1, 0), j))
    row = pl.BlockSpec((1, d_model), lambda i: (0, 0))
    return pl.pallas_call(
        body, name="branch_a_fwd", grid=(n_tok // ta,),
        in_specs=[blk(0), blk(1), blk(2), halo(0), halo(1),
                  pl.BlockSpec((HALO, d_model), lambda i: (0, 0)), row, row, row],
        out_specs=[pl.BlockSpec((ta, d_model), lambda i: (i, 0))] * 2,
        out_shape=[SDS((n_tok, d_model), BF16), SDS((n_tok, d_model), F32)],
        scratch_shapes=[pltpu.VMEM((HALO + ta, LANES), F32)],
        compiler_params=_params(1),
    )(proj, proj, proj, proj, proj, conv_w_full, conv_b, gn_g, gn_b)


def _branch_b_fwd(proj, ln_g, ln_b, w_spatial, b_spatial_t, seq):
    n_tok = proj.shape[0]
    d_model = ln_g.shape[1]
    n_heads = d_model // LANES
    tb = _tile(seq, 256, LANES)

    def body(u_ref, v_ref, bg_ref, lg_ref, lb_ref, ws_ref, bs_ref, s_ref, vn_buf):
        v, _ = _gelu_and_grad(v_ref[...])
        mu = jnp.mean(v, axis=-1, keepdims=True)
        dlt = v - mu
        var = jnp.mean(dlt * dlt, axis=-1, keepdims=True)
        vn_buf[...] = (dlt * lax.rsqrt(var + LN_EPS) * lg_ref[...] + lb_ref[...]).astype(BF16)
        tril = _tril_mask()
        for h in range(n_heads):
            cols = slice(h * LANES, (h + 1) * LANES)
            w_h = jnp.where(tril, ws_ref[h], 0.0).astype(BF16)
            bias = bs_ref[:, h:h + 1]
            for ch in range(tb // LANES):
                rows = slice(ch * LANES, (ch + 1) * LANES)
                mix = _dot(w_h, vn_buf[rows, cols]) + bias
                u, _ = _gelu_and_grad(u_ref[rows, cols])
                gate = bg_ref[rows, cols]
                s_ref[rows, cols] = (u * mix * gate * _sigmoid(gate)).astype(BF16)

    blk = lambda j: pl.BlockSpec((tb, d_model), lambda i: (i, j))
    row = pl.BlockSpec((1, d_model), lambda i: (0, 0))
    return pl.pallas_call(
        body, name="branch_b_fwd", grid=(n_tok // tb,),
        in_specs=[blk(3), blk(4), blk(5), row, row,
                  pl.BlockSpec((n_heads, LANES, LANES), lambda i: (0, 0, 0)),
                  pl.BlockSpec((LANES, n_heads), lambda i: (0, 0))],
        out_specs=pl.BlockSpec((tb, d_model), lambda i: (i, 0)),
        out_shape=SDS((n_tok, d_model), BF16),
        scratch_shapes=[pltpu.VMEM((tb, d_model), BF16)],
        compiler_params=_params(1),
    )(proj, proj, proj, ln_g, ln_b, w_spatial, b_spatial_t)


MID_ROWS = 8


def _mid(h3, s, proj, x2, target, w_pa, w_pb, w_o, b_o, lo_g, lo_b):
    n_tok, d_model = x2.shape
    tm = _tile(n_tok, 256, 16)

    def body(h3_ref, s_ref, ma_ref, mb_ref, x_ref, t_ref, wpa_ref, wpb_ref, wo_ref, bo_ref,
             lg_ref, lb_ref, dproj_ref, dh3_ref, ds_ref, dr_ref, drb_ref, mixed_ref, dya_ref,
             dyb_ref, vec_ref):
        @pl.when(pl.program_id(0) == 0)
        def _():
            vec_ref[...] = jnp.zeros_like(vec_ref)

        ya = _dot(h3_ref[...], wpa_ref[...])
        yb = _dot(s_ref[...], wpb_ref[...])
        ga = _sigmoid(ma_ref[...])
        gb = _sigmoid(mb_ref[...])
        mixed = (ga * ya + gb * yb).astype(BF16)
        mixed_ref[...] = mixed
        r = DEEPNORM_ALPHA * x_ref[...] + _dot(mixed, wo_ref[...]) + bo_ref[...]
        mu = jnp.mean(r, axis=-1, keepdims=True)
        dlt = r - mu
        rstd = lax.rsqrt(jnp.mean(dlt * dlt, axis=-1, keepdims=True) + LN_EPS)
        rhat = dlt * rstd
        diff = rhat * lg_ref[...] + lb_ref[...] - t_ref[...]
        dy = diff * (1.0 / d_model)
        vec_ref[0:1, :] += _colsum(dy * rhat)
        vec_ref[1:2, :] += _colsum(dy)
        vec_ref[5:6, :] += _colsum(diff * diff)
        drh = dy * lg_ref[...]
        dr = rstd * (drh - jnp.mean(drh, axis=-1, keepdims=True)
                     - rhat * jnp.mean(drh * rhat, axis=-1, keepdims=True))
        vec_ref[2:3, :] += _colsum(dr)
        dr_ref[...] = dr
        drb = dr.astype(BF16)
        drb_ref[...] = drb
        dmixed = _dot_tb(drb, wo_ref[...])
        dma = dmixed * ya * ga * (1.0 - ga)
        dmb = dmixed * yb * gb * (1.0 - gb)
        vec_ref[3:4, :] += _colsum(dma)
        vec_ref[4:5, :] += _colsum(dmb)
        dproj_ref[:, 0:d_model] = dma.astype(BF16)
        dproj_ref[:, d_model:2 * d_model] = dmb.astype(BF16)
        dya = (dmixed * ga).astype(BF16)
        dyb = (dmixed * gb).astype(BF16)
        dya_ref[...] = dya
        dyb_ref[...] = dyb
        dh3_ref[...] = _dot_tb(dya, wpa_ref[...])
        ds_ref[...] = _dot_tb(dyb, wpb_ref[...])

    tile = pl.BlockSpec((tm, d_model), lambda i: (i, 0))
    full = pl.BlockSpec((d_model, d_model), lambda i: (0, 0))
    row = pl.BlockSpec((1, d_model), lambda i: (0, 0))
    bf = SDS((n_tok, d_model), BF16)
    f32 = SDS((n_tok, d_model), F32)
    return pl.pallas_call(
        body, name="mid", grid=(n_tok // tm,),
        in_specs=[tile, tile, pl.BlockSpec((tm, d_model), lambda i: (i, 6)),
                  pl.BlockSpec((tm, d_model), lambda i: (i, 7)), tile, tile, full, full, full,
                  row, row, row],
        out_specs=[pl.BlockSpec((tm, 2 * d_model), lambda i: (i, 3)), tile, tile, tile, tile, tile,
                   tile, tile, pl.BlockSpec((MID_ROWS, d_model), lambda i: (0, 0))],
        out_shape=[SDS((n_tok, N_DEV * d_model), BF16), f32, f32, f32, bf, bf, bf, bf,
                   SDS((MID_ROWS, d_model), F32)],
        compiler_params=_params(1),
    )(h3, s, proj, proj, x2, target, w_pa, w_pb, w_o, b_o, lo_g, lo_b)


B_ROWS = 8


def _branch_b_bwd(dproj, proj, d_s, ln_g, ln_b, w_spatial, b_spatial_t, seq):
    n_tok = proj.shape[0]
    d_model = ln_g.shape[1]
    n_heads = d_model // LANES
    tb = _tile(seq, 256, LANES)

    def body(dproj_in, u_ref, v_ref, bg_ref, ds_ref, lg_ref, lb_ref, ws_ref, bs_ref,
             dproj_ref, vec_ref, dws_ref, dbs_ref, vn_buf, dv_buf):
        del dproj_in

        @pl.when(pl.program_id(0) == 0)
        def _():
            vec_ref[...] = jnp.zeros_like(vec_ref)
            dws_ref[...] = jnp.zeros_like(dws_ref)
            dbs_ref[...] = jnp.zeros_like(dbs_ref)

        v, dgelu_v = _gelu_and_grad(v_ref[...])
        mu = jnp.mean(v, axis=-1, keepdims=True)
        dlt = v - mu
        rstd = lax.rsqrt(jnp.mean(dlt * dlt, axis=-1, keepdims=True) + LN_EPS)
        vhat = dlt * rstd
        vn_buf[...] = (vhat * lg_ref[...] + lb_ref[...]).astype(BF16)
        tril = _tril_mask()
        for h in range(n_heads):
            cols = slice(h * LANES, (h + 1) * LANES)
            w_h = jnp.where(tril, ws_ref[h], 0.0).astype(BF16)
            bias = bs_ref[:, h:h + 1]
            for ch in range(tb // LANES):
                rows = slice(ch * LANES, (ch + 1) * LANES)
                vn = vn_buf[rows, cols]
                mix = _dot(w_h, vn) + bias
                u, dgelu_u = _gelu_and_grad(u_ref[rows, cols])
                sg, dsilu = _silu_and_grad(bg_ref[rows, cols])
                dsv = ds_ref[rows, cols]
                du = dsv * mix * sg * dgelu_u
                dbg = dsv * u * mix * dsilu
                dmix = dsv * u * sg
                dmix_bf = dmix.astype(BF16)
                dproj_ref[rows, cols] = du.astype(BF16)
                dproj_ref[rows, 2 * d_model + h * LANES:2 * d_model + (h + 1) * LANES] = dbg.astype(BF16)
                vec_ref[0:1, cols] += _colsum(du)
                vec_ref[2:3, cols] += _colsum(dbg)
                dbs_ref[:, h:h + 1] += jnp.sum(dmix, axis=1, keepdims=True)
                dws_ref[h] += jnp.where(tril, _dot_tb(dmix_bf, vn), 0.0)
                dv_buf[rows, cols] = _dot_ta(w_h, dmix_bf)
        dvn = dv_buf[...]
        vec_ref[3:4, :] += _colsum(dvn * vhat)
        vec_ref[4:5, :] += _colsum(dvn)
        dvh = dvn * lg_ref[...]
        dv = rstd * (dvh - jnp.mean(dvh, axis=-1, keepdims=True)
                     - vhat * jnp.mean(dvh * vhat, axis=-1, keepdims=True)) * dgelu_v
        vec_ref[1:2, :] += _colsum(dv)
        dproj_ref[:, d_model:2 * d_model] = dv.astype(BF16)

    blk = lambda j: pl.BlockSpec((tb, d_model), lambda i: (i, j))
    row = pl.BlockSpec((1, d_model), lambda i: (0, 0))
    return pl.pallas_call(
        body, name="branch_b_bwd", grid=(n_tok // tb,),
        in_specs=[ANY, blk(3), blk(4), blk(5), pl.BlockSpec((tb, d_model), lambda i: (i, 0)), row, row,
                  pl.BlockSpec((n_heads, LANES, LANES), lambda i: (0, 0, 0)),
                  pl.BlockSpec((LANES, n_heads), lambda i: (0, 0))],
        out_specs=[pl.BlockSpec((tb, 3 * d_model), lambda i: (i, 1)),
                   pl.BlockSpec((B_ROWS, d_model), lambda i: (0, 0)),
                   pl.BlockSpec((n_heads, LANES, LANES), lambda i: (0, 0, 0)),
                   pl.BlockSpec((LANES, n_heads), lambda i: (0, 0))],
        out_shape=[SDS(dproj.shape, BF16), SDS((B_ROWS, d_model), F32),
                   SDS((n_heads, LANES, LANES), F32), SDS((LANES, n_heads), F32)],
        scratch_shapes=[pltpu.VMEM((tb, d_model), BF16), pltpu.VMEM((tb, d_model), F32)],
        input_output_aliases={0: 0},
        compiler_params=_params(1),
    )(dproj, proj, proj, proj, d_s, ln_g, ln_b, w_spatial, b_spatial_t)


A1_ROWS = 8


def _branch_a_bwd_norm(dproj, proj, h1, d_h3, gn_g, gn_b, seq):
    n_tok = proj.shape[0]
    d_model = gn_g.shape[1]
    ta = _tile(seq, 256, 16)

    def body(dproj_in, gt_ref, h1_ref, dh3_ref, gg_ref, gb_ref, dproj_ref, dh1_ref, vec_ref):
        del dproj_in

        @pl.when(pl.program_id(0) == 0)
        def _():
            vec_ref[...] = jnp.zeros_like(vec_ref)

        def group(g, carry):
            sl = pl.ds(pl.multiple_of(g * LANES, LANES), LANES)
            h1 = h1_ref[:, sl]
            mu = jnp.mean(h1, axis=-1, keepdims=True)
            dlt = h1 - mu
            rstd = lax.rsqrt(jnp.mean(dlt * dlt, axis=-1, keepdims=True) + LN_EPS)
            nrm = dlt * rstd
            sw, dsw = _silu_and_grad(nrm * gg_ref[:, sl] + gb_ref[:, sl])
            sg, dsg = _silu_and_grad(gt_ref[:, sl])
            dh3 = dh3_ref[:, sl]
            dgate = dh3 * sw * dsg
            dproj_ref[:, sl] = dgate.astype(BF16)
            vec_ref[0:1, sl] += _colsum(dgate)
            dh2 = dh3 * sg * dsw
            vec_ref[1:2, sl] += _colsum(dh2 * nrm)
            vec_ref[2:3, sl] += _colsum(dh2)
            dn = dh2 * gg_ref[:, sl]
            dh1 = rstd * (dn - jnp.mean(dn, axis=-1, keepdims=True)
                          - nrm * jnp.mean(dn * nrm, axis=-1, keepdims=True))
            vec_ref[3:4, sl] += _colsum(dh1)
            dh1_ref[:, sl] = dh1
            return carry

        lax.fori_loop(0, d_model // LANES, group, 0, unroll=True)

    tile = pl.BlockSpec((ta, d_model), lambda i: (i, 0))
    row = pl.BlockSpec((1, d_model), lambda i: (0, 0))
    return pl.pallas_call(
        body, name="branch_a_bwd_norm", grid=(n_tok // ta,),
        in_specs=[ANY, pl.BlockSpec((ta, d_model), lambda i: (i, 2)), tile, tile, row, row],
        out_specs=[pl.BlockSpec((ta, d_model), lambda i: (i, 2)), tile,
                   pl.BlockSpec((A1_ROWS, d_model), lambda i: (0, 0))],
        out_shape=[SDS(dproj.shape, BF16), SDS((n_tok, d_model), F32), SDS((A1_ROWS, d_model), F32)],
        input_output_aliases={0: 0},
        compiler_params=_params(1),
    )(dproj, proj, h1, d_h3, gn_g, gn_b)


A2_ROWS = 8


def _branch_a_bwd_conv(dproj, proj, d_h1, conv_w_full, gp_bf, gp_f32, seq):
    n_tok = proj.shape[0]
    d_model = conv_w_full.shape[1]
    n_groups = d_model // LANES
    ta = _tile(seq, 256, HALO)
    n_tiles = n_tok // ta
    per_seq = seq // ta
    rc = _conv_rows(ta)
    last_halo = n_tok // HALO - 1
    r8 = d_model // N_DEV
    prow = _tile(r8, 32, 16)

    def body(dproj_in, av_ref, ag_ref, avh_ref, agh_ref, dh1_ref, dh1h_ref, cw_ref,
             gpa_bf, gpb_bf, gpo_bf, gpa_f32, gpb_f32, gpo_f32,
             dproj_ref, vec_ref, dcw_ref, opa_ref, opb_ref, opo_ref,
             ext_h0, ext_d, rbuf, own, send_sems, recv_sems, local_sems):
        del dproj_in
        i = pl.program_id(0)
        x, y, c = _mesh_pos()
        me = _block_of((x, y, c))
        peers = _peers()
        gp_bf_refs = [gpa_bf, gpb_bf, gpo_bf]
        gp_f32_refs = [gpa_f32, gpb_f32, gpo_f32]

        def sends():
            return [pltpu.make_async_remote_copy(
                src_ref=gp_bf_refs[a].at[pl.ds(pl.multiple_of(blk * r8, 16), r8), :], dst_ref=rbuf.at[k, a],
                send_sem=send_sems.at[a, k], recv_sem=recv_sems.at[a, k], device_id=pos, device_id_type=MESH)
                for k, (pos, blk) in enumerate(peers) for a in range(3)]

        def own_rows():
            return [pltpu.make_async_copy(gp_f32_refs[a].at[pl.ds(pl.multiple_of(me * r8, 8), r8), :],
                                          own.at[a], local_sems.at[a]) for a in range(3)]

        @pl.when(i == 0)
        def _():
            vec_ref[...] = jnp.zeros_like(vec_ref)
            dcw_ref[...] = jnp.zeros_like(dcw_ref)
            for cp in sends() + own_rows():
                cp.start()

        keep_past = jnp.where(i % per_seq == 0, 0.0, 1.0)
        keep_next = jnp.where(i % per_seq == per_seq - 1, 0.0, 1.0)

        def group(g, carry):
            sl = pl.ds(pl.multiple_of(g * LANES, LANES), LANES)
            av = av_ref[:, sl]
            sig = _sigmoid(ag_ref[:, sl])
            ext_h0[0:HALO, :] = avh_ref[:, sl] * _sigmoid(agh_ref[:, sl]) * keep_past
            ext_h0[HALO:HALO + ta, :] = av * sig
            ext_d[0:ta, :] = dh1_ref[:, sl]
            ext_d[ta:ta + HALO, :] = dh1h_ref[:, sl] * keep_next
            for r0 in range(0, ta, rc):
                dh1 = ext_d[pl.ds(r0, rc), :]
                acc = jnp.zeros((rc, LANES), F32)
                for k in range(CONV_K):
                    acc = acc + ext_d[pl.ds(r0 + CONV_K - 1 - k, rc), :] * cw_ref[k:k + 1, sl]
                    prod = dh1 * ext_h0[pl.ds(r0 + HALO - (CONV_K - 1) + k, rc), :]
                    dcw_ref[g, k] += jnp.sum(prod.reshape(rc // SUBLANES, SUBLANES, LANES), axis=0)
                rows = pl.ds(r0, rc)
                sig_r = sig[r0:r0 + rc]
                dav = acc * sig_r
                dag = acc * av[r0:r0 + rc] * sig_r * (1.0 - sig_r)
                dproj_ref[rows, sl] = dav.astype(BF16)
                dproj_ref[rows, pl.ds(pl.multiple_of(d_model + g * LANES, LANES), LANES)] = dag.astype(BF16)
                vec_ref[0:1, sl] += _colsum(dav)
                vec_ref[1:2, sl] += _colsum(dag)
            return carry

        lax.fori_loop(0, n_groups, group, 0, unroll=True)

        @pl.when(i == n_tiles - 1)
        def _():
            for cp in own_rows():
                cp.wait()
            for cp in sends():
                cp.wait_recv()
            for a, o in enumerate([opa_ref, opb_ref, opo_ref]):
                for q in range(r8 // prow):
                    r = pl.ds(q * prow, prow)
                    tot = own[a, r, :]
                    for k in range(N_DEV - 1):
                        tot = tot + rbuf[k, a, r, :].astype(F32)
                    o[r, :] = tot
            for cp in sends():
                cp.wait_send()

    blk = lambda j: pl.BlockSpec((ta, d_model), lambda i: (i, j))
    halo = lambda j: pl.BlockSpec((HALO, d_model), lambda i: (jnp.maximum(i * (ta // HALO) - 1, 0), j))
    shard = pl.BlockSpec((r8, d_model), lambda i: (0, 0))
    return pl.pallas_call(
        body, name="branch_a_bwd_conv", grid=(n_tiles,),
        in_specs=[ANY, blk(0), blk(1), halo(0), halo(1), pl.BlockSpec((ta, d_model), lambda i: (i, 0)),
                  pl.BlockSpec((HALO, d_model), lambda i: (jnp.minimum((i + 1) * (ta // HALO), last_halo), 0)),
                  pl.BlockSpec((HALO, d_model), lambda i: (0, 0))] + [ANY] * 6,
        out_specs=[pl.BlockSpec((ta, 2 * d_model), lambda i: (i, 0)),
                   pl.BlockSpec((A2_ROWS, d_model), lambda i: (0, 0)),
                   pl.BlockSpec((n_groups, HALO, SUBLANES, LANES), lambda i: (0, 0, 0, 0)),
                   shard, shard, shard],
        out_shape=[SDS(dproj.shape, BF16), SDS((A2_ROWS, d_model), F32),
                   SDS((n_groups, HALO, SUBLANES, LANES), F32)] + [SDS((r8, d_model), F32)] * 3,
        scratch_shapes=[pltpu.VMEM((HALO + ta, LANES), F32), pltpu.VMEM((ta + HALO, LANES), F32),
                        pltpu.VMEM((N_DEV - 1, 3, r8, d_model), BF16), pltpu.VMEM((3, r8, d_model), F32),
                        pltpu.SemaphoreType.DMA((3, 7)), pltpu.SemaphoreType.DMA((3, 7)),
                        pltpu.SemaphoreType.DMA((3,))],
        input_output_aliases={0: 0},
        compiler_params=_params(1),
    )(dproj, proj, proj, proj, proj, d_h1, d_h1, conv_w_full, *gp_bf, *gp_f32)


def _weight_grad(lhs, rhs, name):
    n_tok, d_model = lhs.shape
    tk = _tile(n_tok, 2048, 16)
    n_k = n_tok // tk

    def body(a_ref, g_ref, o_ref, ob_ref):
        part = _dot_ta(a_ref[...], g_ref[...])

        @pl.when(pl.program_id(0) == 0)
        def _():
            o_ref[...] = part

        @pl.when(pl.program_id(0) != 0)
        def _():
            o_ref[...] += part

        @pl.when(pl.program_id(0) == n_k - 1)
        def _():
            ob_ref[...] = o_ref[...].astype(BF16)

    tile = pl.BlockSpec((tk, d_model), lambda i: (i, 0))
    out = pl.BlockSpec((d_model, d_model), lambda i: (0, 0))
    return pl.pallas_call(
        body, name=name, grid=(n_k,), in_specs=[tile, tile], out_specs=[out, out],
        out_shape=[SDS((d_model, d_model), F32), SDS((d_model, d_model), BF16)],
        compiler_params=_params(1),
    )(lhs, rhs)


def _grad_w_in_reduce_scatter(xt_bf, dproj, blk_order):
    n_tok = dproj.shape[0]
    d_model = xt_bf.shape[0]
    dh = d_model // 2
    n_units = 2 * N_DEV

    def body(ord_ref, a_ref, g_ref, o_ref, acc, fb, sbuf, gbuf, tbuf, rfin,
             send_f, send_s, recv_g, recv_t, recv_f, out_sems):
        del ord_ref
        u = pl.program_id(0)
        s = u // 2
        hf = u % 2
        rnd = s // 2
        x, y, c = _mesh_pos()
        sibling = (x, y, 1 - c)
        n1 = (jnp.bitwise_xor(x, c), jnp.bitwise_xor(y, 1 - c), c)
        n2 = (jnp.bitwise_xor(x, 1 - c), jnp.bitwise_xor(y, c), c)

        def feed(r, half):
            return pltpu.make_async_remote_copy(
                src_ref=fb.at[half], dst_ref=gbuf.at[r, half], send_sem=send_f.at[r, half],
                recv_sem=recv_g.at[r, half], device_id=sibling, device_id_type=MESH)

        def feed_sibling(half):
            return pltpu.make_async_remote_copy(
                src_ref=fb.at[half], dst_ref=rfin.at[0, half], send_sem=send_f.at[3, half],
                recv_sem=recv_f.at[0, half], device_id=sibling, device_id_type=MESH)

        def chip_sum(r, half):
            dst = [tbuf.at[half], rfin.at[2, half], rfin.at[1, half]][r]
            sem = [recv_t.at[half], recv_f.at[2, half], recv_f.at[1, half]][r]
            return pltpu.make_async_remote_copy(
                src_ref=sbuf.at[r, half], dst_ref=dst, send_sem=send_s.at[r, half], recv_sem=sem,
                device_id=[n2, n2, n1][r], device_id_type=MESH)

        def out_copy(half):
            return pltpu.make_async_copy(acc.at[half], o_ref.at[:, pl.ds(half * dh, dh)], out_sems.at[half])

        def partial_sum():
            return _dot(a_ref[:, 0:n_tok], g_ref[...])

        for half in range(2):
            for r in range(3):
                @pl.when(u == 4 * r + 4 + half)
                def _(r=r, half=half):
                    feed(r, half).wait_send()

                @pl.when(u == 4 * r + 2 + half)
                def _(r=r, half=half):
                    feed(r, half).wait_recv()
                    if r == 2:
                        chip_sum(0, half).wait_recv()

            @pl.when(u == 14 + half)
            def _(half=half):
                feed_sibling(half).wait_recv()
                chip_sum(2, half).wait_recv()
                chip_sum(1, half).wait_recv()

        @pl.when(jnp.logical_and(s % 2 == 0, s < 7))
        def _():
            fb[hf] = partial_sum().astype(BF16)

        @pl.when(jnp.logical_or(s == 1, s == 3))
        def _():
            sbuf[rnd, hf] = (partial_sum() + gbuf[rnd, hf].astype(F32)).astype(BF16)

        @pl.when(s == 5)
        def _():
            sbuf[2, hf] = (partial_sum() + gbuf[2, hf].astype(F32) + tbuf[hf].astype(F32)).astype(BF16)

        @pl.when(s == 7)
        def _():
            acc[hf] = (partial_sum() + rfin[0, hf].astype(F32) + rfin[1, hf].astype(F32)
                       + rfin[2, hf].astype(F32))

        for half in range(2):
            for r in range(3):
                @pl.when(u == 4 * r + half)
                def _(r=r, half=half):
                    feed(r, half).start()

                @pl.when(u == 4 * r + 2 + half)
                def _(r=r, half=half):
                    chip_sum(r, half).start()

            @pl.when(u == 12 + half)
            def _(half=half):
                feed_sibling(half).start()

        @pl.when(u == 14)
        def _():
            out_copy(0).start()

        @pl.when(u == 15)
        def _():
            out_copy(1).start()
            for half in range(2):
                feed_sibling(half).wait_send()
                for r in range(3):
                    chip_sum(r, half).wait_send()
                out_copy(half).wait()

    grid_spec = pltpu.PrefetchScalarGridSpec(
        num_scalar_prefetch=1, grid=(n_units,),
        in_specs=[VMEM, pl.BlockSpec((n_tok, dh), lambda u, o: (0, 2 * o[u // 2] + u % 2))],
        out_specs=ANY,
        scratch_shapes=[pltpu.VMEM((2, d_model, dh), F32), pltpu.VMEM((2, d_model, dh), BF16),
                        pltpu.VMEM((3, 2, d_model, dh), BF16), pltpu.VMEM((3, 2, d_model, dh), BF16),
                        pltpu.VMEM((2, d_model, dh), BF16), pltpu.VMEM((3, 2, d_model, dh), BF16),
                        pltpu.SemaphoreType.DMA((4, 2)), pltpu.SemaphoreType.DMA((3, 2)),
                        pltpu.SemaphoreType.DMA((3, 2)), pltpu.SemaphoreType.DMA((2,)),
                        pltpu.SemaphoreType.DMA((3, 2)),
                        pltpu.SemaphoreType.DMA((2,))])
    return pl.pallas_call(
        body, name="grad_w_in_reduce_scatter", grid_spec=grid_spec,
        out_shape=SDS((d_model, d_model), F32), compiler_params=_params(1),
    )(blk_order, xt_bf, dproj)


def _grad_x(dproj, w_all, dr):
    n_tok, d_model = dr.shape
    tm = _tile(n_tok, 512, 16)

    def body(dp_ref, w_ref, dr_ref, o_ref):
        acc = _dot_tb(dp_ref[:, 0:d_model], w_ref[0])
        for j in range(1, N_DEV):
            acc = acc + _dot_tb(dp_ref[:, j * d_model:(j + 1) * d_model], w_ref[j])
        o_ref[...] = acc + DEEPNORM_ALPHA * dr_ref[...]

    return pl.pallas_call(
        body, name="grad_x", grid=(n_tok // tm,),
        in_specs=[pl.BlockSpec((tm, N_DEV * d_model), lambda i: (i, 0)), VMEM,
                  pl.BlockSpec((tm, d_model), lambda i: (i, 0))],
        out_specs=pl.BlockSpec((tm, d_model), lambda i: (i, 0)),
        out_shape=SDS((n_tok, d_model), F32),
        compiler_params=_params(1),
    )(dproj, w_all, dr)


def _adamw_math(w, g, m, v):
    m = ADAM_B1 * m + (1.0 - ADAM_B1) * g
    v = ADAM_B2 * v + (1.0 - ADAM_B2) * (g * g)
    m_hat = m / (1.0 - ADAM_B1 ** ADAM_STEP)
    v_hat = v / (1.0 - ADAM_B2 ** ADAM_STEP)
    delta = -ADAM_LR * (m_hat / (jnp.sqrt(v_hat) + ADAM_EPS) + ADAM_WD * w)
    return delta, m, v


def _adamw_tiled(w, g, m, v):
    n_rows, n_cols = w.shape
    tr = _tile(n_rows, 256, SUBLANES)

    def body(w_ref, g_ref, m_ref, v_ref, d_ref, mo_ref, vo_ref):
        d_ref[...], mo_ref[...], vo_ref[...] = _adamw_math(w_ref[...], g_ref[...], m_ref[...], v_ref[...])

    tile = pl.BlockSpec((tr, n_cols), lambda i: (i, 0))
    return pl.pallas_call(
        body, name="adamw_w_in", grid=(n_rows // tr,), in_specs=[tile] * 4, out_specs=[tile] * 3,
        out_shape=[SDS(w.shape, F32)] * 3, compiler_params=_params(1),
    )(w, g, m, v)


def _adamw_many(groups):
    n = len(groups)

    def body(*refs):
        ins, outs = refs[:4 * n], refs[4 * n:]
        for p in range(n):
            w_ref, g_ref, m_ref, v_ref = ins[4 * p:4 * p + 4]
            d, m, v = _adamw_math(w_ref[...], g_ref[...], m_ref[...], v_ref[...])
            outs[3 * p][...] = d
            outs[3 * p + 1][...] = m
            outs[3 * p + 2][...] = v

    flat = [a for grp in groups for a in grp]
    out_shape = [SDS(grp[0].shape, F32) for grp in groups for _ in range(3)]
    res = pl.pallas_call(
        body, name="adamw_small", in_specs=[VMEM] * (4 * n), out_specs=[VMEM] * (3 * n),
        out_shape=out_shape, compiler_params=_params(),
    )(*flat)
    return [tuple(res[3 * p:3 * p + 3]) for p in range(n)]


def _as_rows(a):
    return a.reshape(-1, LANES)


def kernel(x, w_in, b_in, conv_w, conv_b, gn_g, gn_b, ln_v_g, ln_v_b, w_spatial, b_spatial, w_pa, w_pb, w_o, b_o, ln_out_g, ln_out_b, loss_target, m_w_in, m_b_in, m_conv_w, m_conv_b, m_gn_g, m_gn_b, m_ln_v_g, m_ln_v_b, m_w_spatial, m_b_spatial, m_w_pa, m_w_pb, m_w_o, m_b_o, m_ln_out_g, m_ln_out_b, v_w_in, v_b_in, v_conv_w, v_conv_b, v_gn_g, v_gn_b, v_ln_v_g, v_ln_v_b, v_w_spatial, v_b_spatial, v_w_pa, v_w_pb, v_w_o, v_b_o, v_ln_out_g, v_ln_out_b):
    n_batch, seq, d_model = x.shape
    n_tok = n_batch * seq
    n_heads = d_model // LANES
    dc = conv_w.shape[1]
    me = 4 * lax.axis_index("x") + 2 * lax.axis_index("y") + lax.axis_index("c")
    row = lambda a: a.reshape(1, d_model)

    x2 = x.reshape(n_tok, d_model)
    target2 = loss_target.reshape(n_tok, d_model)
    b_spatial_t = b_spatial.T

    first = jnp.where(lax.axis_index("c") == 1, 4, 2)
    second = 6 - first
    ag_rel = jnp.stack([0 * first, 0 * first + 1, first, second + 1, second, first + 1, 0 * first + 6, 0 * first + 7])
    ag_blocks = jnp.bitwise_xor(me, ag_rel).astype(jnp.int32)
    proj, xt_bf, w_all, wp_all, cw_all = _proj_all_gather(
        x2, w_in, w_pa, w_pb, w_o, conv_w, b_in.reshape(N_DEV, 1, d_model), ag_blocks)
    wp_full = [wp_all[:, a].reshape(d_model, d_model) for a in range(3)]
    conv_w_full = jnp.pad(cw_all.transpose(1, 0, 2).reshape(CONV_K, d_model), ((0, HALO - CONV_K), (0, 0)))

    h3, h1 = _branch_a_fwd(proj, conv_w_full, row(conv_b), row(gn_g), row(gn_b), seq)
    s = _branch_b_fwd(proj, row(ln_v_g), row(ln_v_b), w_spatial, b_spatial_t, seq)

    dproj, d_h3, d_s, dr, dr_bf, mixed, d_ya, d_yb, vec_mid = _mid(
        h3, s, proj, x2, target2, *wp_full, row(b_o), row(ln_out_g), row(ln_out_b))

    dproj, vec_b, d_ws, d_bs_t = _branch_b_bwd(dproj, proj, d_s, row(ln_v_g), row(ln_v_b), w_spatial, b_spatial_t, seq)
    dproj, d_h1, vec_a1 = _branch_a_bwd_norm(dproj, proj, h1, d_h3, row(gn_g), row(gn_b), seq)
    gp = [_weight_grad(lhs, rhs, name) for lhs, rhs, name in
          ((h3, d_ya, "grad_w_pa"), (s, d_yb, "grad_w_pb"), (mixed, dr_bf, "grad_w_o"))]
    dproj, vec_a2, d_cw8, g_w_pa, g_w_pb, g_w_o = _branch_a_bwd_conv(
        dproj, proj, d_h1, conv_w_full, [g[1] for g in gp], [g[0] for g in gp], seq)

    rs_rel = jnp.stack([0 * first + 7, 0 * first + 6, first + 1, second, second + 1, first, 0 * first + 1, 0 * first])
    rs_blocks = jnp.bitwise_xor(me, rs_rel).astype(jnp.int32)
    g_w_in = _grad_w_in_reduce_scatter(xt_bf, dproj, rs_blocks)
    grad_x = _grad_x(dproj, w_all, dr).reshape(x.shape)

    d_cw = jnp.sum(d_cw8, axis=2)
    pieces = [
        _as_rows(jnp.concatenate([vec_a2[0:2], vec_a1[0:1], vec_b[0:3], vec_mid[3:5]], axis=0)),
        _as_rows(jnp.concatenate([vec_a1[3:4], vec_a1[1:3], vec_b[3:5], vec_mid[2:3], vec_mid[0:2]], axis=0)),
        _as_rows(d_bs_t.T), _as_rows(d_ws), _as_rows(d_cw), _as_rows(vec_mid[5:6]),
    ]
    n_rows = sum(p.shape[0] for p in pieces)
    pad_rows = -n_rows % (N_DEV * SUBLANES)
    small = _all_reduce_small(jnp.concatenate(pieces + [jnp.zeros((pad_rows, LANES), F32)], axis=0))
    g_rows = d_model // LANES
    o0 = N_DEV * g_rows
    g_b_in = small[0:o0].reshape(N_DEV * d_model)
    vecs = [small[o0 + a * g_rows:o0 + (a + 1) * g_rows].reshape(d_model) for a in range(8)]
    g_conv_b, g_gn_g, g_gn_b, g_ln_v_g, g_ln_v_b, g_b_o, g_ln_out_g, g_ln_out_b = vecs
    o1 = o0 + 8 * g_rows
    g_b_spatial = small[o1:o1 + n_heads].reshape(n_heads, LANES)
    o2 = o1 + n_heads
    g_w_spatial = small[o2:o2 + n_heads * LANES].reshape(n_heads, LANES, LANES)
    o3 = o2 + n_heads * LANES
    g_cw_full = small[o3:o3 + n_heads * HALO].reshape(n_heads, HALO, LANES).transpose(1, 0, 2).reshape(HALO, d_model)
    g_conv_w = lax.dynamic_slice(g_cw_full, (0, me * dc), (CONV_K, dc))
    o4 = o3 + n_heads * HALO
    loss = jnp.sum(small[o4:o4 + g_rows]) * (0.5 / d_model)

    d_w_in, nm_w_in, nv_w_in = _adamw_tiled(w_in, g_w_in, m_w_in, v_w_in)
    two_d = lambda a: a.reshape(-1, a.shape[-1]) if a.ndim != 1 else (
        a.reshape(-1, LANES) if a.shape[0] % LANES == 0 else a.reshape(1, -1))
    names = ["b_in", "conv_w", "conv_b", "gn_g", "gn_b", "ln_v_g", "ln_v_b", "w_spatial", "b_spatial",
             "w_pa", "w_pb", "w_o", "b_o", "ln_out_g", "ln_out_b"]
    ws = dict(b_in=b_in, conv_w=conv_w, conv_b=conv_b, gn_g=gn_g, gn_b=gn_b, ln_v_g=ln_v_g, ln_v_b=ln_v_b,
              w_spatial=w_spatial, b_spatial=b_spatial, w_pa=w_pa, w_pb=w_pb, w_o=w_o, b_o=b_o,
              ln_out_g=ln_out_g, ln_out_b=ln_out_b)
    gs = dict(b_in=g_b_in, conv_w=g_conv_w, conv_b=g_conv_b, gn_g=g_gn_g, gn_b=g_gn_b, ln_v_g=g_ln_v_g,
              ln_v_b=g_ln_v_b, w_spatial=g_w_spatial, b_spatial=g_b_spatial, w_pa=g_w_pa, w_pb=g_w_pb,
              w_o=g_w_o, b_o=g_b_o, ln_out_g=g_ln_out_g, ln_out_b=g_ln_out_b)
    ms = dict(b_in=m_b_in, conv_w=m_conv_w, conv_b=m_conv_b, gn_g=m_gn_g, gn_b=m_gn_b, ln_v_g=m_ln_v_g,
              ln_v_b=m_ln_v_b, w_spatial=m_w_spatial, b_spatial=m_b_spatial, w_pa=m_w_pa, w_pb=m_w_pb,
              w_o=m_w_o, b_o=m_b_o, ln_out_g=m_ln_out_g, ln_out_b=m_ln_out_b)
    vs = dict(b_in=v_b_in, conv_w=v_conv_w, conv_b=v_conv_b, gn_g=v_gn_g, gn_b=v_gn_b, ln_v_g=v_ln_v_g,
              ln_v_b=v_ln_v_b, w_spatial=v_w_spatial, b_spatial=v_b_spatial, w_pa=v_w_pa, w_pb=v_w_pb,
              w_o=v_w_o, b_o=v_b_o, ln_out_g=v_ln_out_g, ln_out_b=v_ln_out_b)
    upd = _adamw_many([tuple(two_d(d[n]) for d in (ws, gs, ms, vs)) for n in names])
    delta = {n: u[0].reshape(ws[n].shape) for n, u in zip(names, upd)}
    new_m = {n: u[1].reshape(ws[n].shape) for n, u in zip(names, upd)}
    new_v = {n: u[2].reshape(ws[n].shape) for n, u in zip(names, upd)}
    gs["w_in"], delta["w_in"], new_m["w_in"], new_v["w_in"] = g_w_in, d_w_in, nm_w_in, nv_w_in

    order = ["w_in"] + names
    return (loss, grad_x, *[gs[n] for n in order], *[delta[n] for n in order],
            *[new_m[n] for n in order], *[new_v[n] for n in order])
```

```python
import jax
import jax.numpy as jnp
from jax import lax
from jax.experimental import pallas as pl
from jax.experimental.pallas import tpu as pltpu

F32 = jnp.float32
BF16 = jnp.bfloat16
SDS = jax.ShapeDtypeStruct

N_DEV = 8
LANES = 128
SUBLANES = 8
CONV_K = 31
HALO = 32
LN_EPS = 1e-5
DEEPNORM_ALPHA = 2.0 ** 0.25
ADAM_LR, ADAM_B1, ADAM_B2, ADAM_EPS, ADAM_WD, ADAM_STEP = 0.001, 0.9, 0.999, 1e-08, 0.01, 10
GELU_C = 0.7978845608028654
GELU_A = 0.044715
VMEM_LIMIT = 56 * 1024 * 1024
MESH = pl.DeviceIdType.MESH
ANY = pl.BlockSpec(memory_space=pl.ANY)
VMEM = pl.BlockSpec(memory_space=pltpu.VMEM)


def _params(n_grid=0):
    sem = ("arbitrary",) * n_grid if n_grid else None
    return pltpu.CompilerParams(dimension_semantics=sem, vmem_limit_bytes=VMEM_LIMIT)


def _tile(n, pref, mult):
    t = min(n, pref)
    while n % t or t % mult:
        t -= 1
    return t


def _colsum(v):
    return jnp.sum(v, axis=0, keepdims=True)


def _sigmoid(v):
    return jax.nn.sigmoid(v)


def _silu_and_grad(v):
    s = _sigmoid(v)
    return v * s, s * (1.0 + v * (1.0 - s))


def _gelu_and_grad(v):
    inner = GELU_C * (v + GELU_A * v * v * v)
    th = jnp.tanh(inner)
    val = 0.5 * v * (1.0 + th)
    grad = 0.5 * (1.0 + th) + 0.5 * v * (1.0 - th * th) * GELU_C * (1.0 + 3.0 * GELU_A * v * v)
    return val, grad


def _tril_mask():
    r = lax.broadcasted_iota(jnp.int32, (LANES, LANES), 0)
    c = lax.broadcasted_iota(jnp.int32, (LANES, LANES), 1)
    return c <= r


def _dot(a, b):
    return jnp.dot(a, b, preferred_element_type=F32)


def _dot_tb(a, b):
    return lax.dot_general(a, b, (((1,), (1,)), ((), ())), preferred_element_type=F32)


def _dot_ta(a, b):
    return lax.dot_general(a, b, (((0,), (0,)), ((), ())), preferred_element_type=F32)


def _mesh_pos():
    return lax.axis_index("x"), lax.axis_index("y"), lax.axis_index("c")


def _block_of(pos):
    return 4 * pos[0] + 2 * pos[1] + pos[2]


def _peers():
    x, y, c = _mesh_pos()
    out = []
    for k in range(1, N_DEV):
        pos = (1 - x if k & 4 else x, 1 - y if k & 2 else y, 1 - c if k & 1 else c)
        out.append((pos, _block_of(pos)))
    return out


def _proj_all_gather(x2, w_in, w_pa, w_pb, w_o, conv_w, b_in3, blk_order):
    n_tok, d_model = x2.shape
    r8 = w_pa.shape[0]
    kc, dc = conv_w.shape
    tm = _tile(n_tok, 1024, LANES)
    n_t = n_tok // tm

    def body(ord_ref, x_ref, b_ref, win_ref, wpa_ref, wpb_ref, wo_ref, cw_ref,
             proj_ref, xt_ref, wall_ref, wp_ref, cwall_ref,
             wbuf, st_p, send_sems, recv_sems, local_sems, wall_sems):
        s = pl.program_id(0)
        t = pl.program_id(1)
        x, y, c = _mesh_pos()
        me = (x, y, c)
        sibling = (x, y, 1 - c)
        n1 = (jnp.bitwise_xor(x, c), jnp.bitwise_xor(y, 1 - c))
        n2 = (jnp.bitwise_xor(x, 1 - c), jnp.bitwise_xor(y, c))
        dg = (1 - x, 1 - y)
        outs = [wbuf, wp_ref, cwall_ref]
        srcs = [None, st_p, cw_ref]
        consumed = [me, sibling, (*n1, c), (*n2, 1 - c), (*n2, c), (*n1, 1 - c), (*dg, c), (*dg, 1 - c)]
        leaves = [(me, sibling), (me, (*n1, c)), (me, (*n2, c)), ((*n1, c), (*n2, c)),
                  ((*n1, c), sibling), ((*n2, c), sibling), ((*dg, c), sibling)]
        lands = [sibling, (*n1, c), (*n2, c), (*dg, c), (*n2, 1 - c), (*n1, 1 - c), (*dg, 1 - c)]

        def slot(o, pos):
            return o.at[_block_of(pos)]

        def copy(a, k, block, to, src=None):
            o = outs[a]
            return pltpu.make_async_remote_copy(
                src_ref=slot(o, block) if src is None else src, dst_ref=slot(o, block),
                send_sem=send_sems.at[a, k], recv_sem=recv_sems.at[a, k],
                device_id=to, device_id_type=MESH)

        def send(a, k):
            block, to = leaves[k]
            return copy(a, k, block, to, src=srcs[a] if k < 3 else None)

        def recv(a, k):
            return copy(a, k, lands[k], me)

        def local_copies():
            return [pltpu.make_async_copy(srcs[a], slot(outs[a], me), local_sems.at[a]) for a in (1, 2)]

        def to_hbm(step):
            return pltpu.make_async_copy(slot(wbuf, consumed[step]), slot(wall_ref, consumed[step]),
                                         wall_sems.at[step])

        def at_step(step):
            return pl.when(jnp.logical_and(s == step, t == 0))

        @at_step(0)
        def _():
            slot(wbuf, me)[...] = win_ref[...].astype(BF16)
            st_p[0] = wpa_ref[...].astype(BF16)
            st_p[1] = wpb_ref[...].astype(BF16)
            st_p[2] = wo_ref[...].astype(BF16)
            for a in range(3):
                send(a, 0).start()
                send(a, 1).start()
            for cp in local_copies():
                cp.start()
            to_hbm(0).start()

        @at_step(1)
        def _():
            recv(0, 0).wait_recv()
            to_hbm(1).start()

        for rnd in range(3):
            @at_step(2 + 2 * rnd)
            def _(rnd=rnd):
                if rnd == 0:
                    for a in range(3):
                        send(a, 2).start()
                recv(0, 1 + rnd).wait_recv()
                if rnd == 0:
                    send(0, 3).start()
                send(0, 4 + rnd).start()
                to_hbm(2 + 2 * rnd).start()

            @at_step(3 + 2 * rnd)
            def _(rnd=rnd):
                for a in (1, 2):
                    recv(a, 1 + rnd).wait_recv()
                    if rnd == 0:
                        send(a, 3).start()
                    send(a, 4 + rnd).start()
                recv(0, 4 + rnd).wait_recv()
                to_hbm(3 + 2 * rnd).start()

        xb = x_ref[...].astype(BF16)
        xt_ref[...] = xb.T
        proj_ref[...] = _dot(xb, wbuf[ord_ref[s]]) + b_ref[...]

        @pl.when(jnp.logical_and(s == N_DEV - 1, t == n_t - 1))
        def _():
            for a in (1, 2):
                for k in (0, 4, 5, 6):
                    recv(a, k).wait_recv()
            for a in range(3):
                for k in range(7):
                    send(a, k).wait_send()
            for cp in local_copies() + [to_hbm(step) for step in range(N_DEV)]:
                cp.wait()

    grid_spec = pltpu.PrefetchScalarGridSpec(
        num_scalar_prefetch=1, grid=(N_DEV, n_t),
        in_specs=[pl.BlockSpec((tm, d_model), lambda s, t, o: (t, 0)),
                  pl.BlockSpec((None, 1, d_model), lambda s, t, o: (o[s], 0, 0)),
                  VMEM, VMEM, VMEM, VMEM, VMEM],
        out_specs=[pl.BlockSpec((tm, d_model), lambda s, t, o: (t, o[s])),
                   pl.BlockSpec((d_model, tm), lambda s, t, o: (0, jnp.where(s == 0, t, n_t))),
                   ANY, ANY, ANY],
        scratch_shapes=[pltpu.VMEM((N_DEV, d_model, d_model), BF16), pltpu.VMEM((3, r8, d_model), BF16),
                        pltpu.SemaphoreType.DMA((3, 7)), pltpu.SemaphoreType.DMA((3, 7)),
                        pltpu.SemaphoreType.DMA((3,)), pltpu.SemaphoreType.DMA((N_DEV,))])
    return pl.pallas_call(
        body, name="proj_all_gather", grid_spec=grid_spec,
        out_shape=[SDS((n_tok, N_DEV * d_model), F32), SDS((d_model, (n_t + 1) * tm), BF16),
                   SDS((N_DEV, d_model, d_model), BF16), SDS((N_DEV, 3, r8, d_model), BF16),
                   SDS((N_DEV, kc, dc), F32)],
        compiler_params=_params(2),
    )(blk_order, x2, b_in3, w_in, w_pa, w_pb, w_o, conv_w)


def _all_reduce_small(part):
    n_rows = part.shape[0]
    rsl = n_rows // N_DEV

    def body(p_ref, o_ref, rbuf, red, send1, recv1, send2, recv2):
        x, y, c = _mesh_pos()
        me = _block_of((x, y, c))
        peers = _peers()

        def rows_of(blk):
            return pl.ds(pl.multiple_of(blk * rsl, SUBLANES), rsl)

        scatter = [pltpu.make_async_remote_copy(
            src_ref=p_ref.at[rows_of(blk), :], dst_ref=rbuf.at[k],
            send_sem=send1.at[k], recv_sem=recv1.at[k], device_id=pos, device_id_type=MESH)
            for k, (pos, blk) in enumerate(peers)]
        for cp in scatter:
            cp.start()
        for cp in scatter:
            cp.wait_recv()
        acc = p_ref[rows_of(me), :]
        for k in range(N_DEV - 1):
            acc = acc + rbuf[k]
        red[...] = acc
        o_ref[rows_of(me), :] = acc
        gather = [pltpu.make_async_remote_copy(
            src_ref=red, dst_ref=o_ref.at[rows_of(me), :],
            send_sem=send2.at[k], recv_sem=recv2.at[k], device_id=pos, device_id_type=MESH)
            for k, (pos, blk) in enumerate(peers)]
        for cp in gather:
            cp.start()
        for k, (pos, blk) in enumerate(peers):
            pltpu.make_async_remote_copy(
                src_ref=red, dst_ref=o_ref.at[rows_of(blk), :],
                send_sem=send2.at[k], recv_sem=recv2.at[k], device_id=pos,
                device_id_type=MESH).wait_recv()
        for cp in scatter + gather:
            cp.wait_send()

    return pl.pallas_call(
        body, name="all_reduce_small", out_shape=SDS(part.shape, F32),
        in_specs=[VMEM], out_specs=VMEM,
        scratch_shapes=[pltpu.VMEM((N_DEV - 1, rsl, LANES), F32), pltpu.VMEM((rsl, LANES), F32)]
        + [pltpu.SemaphoreType.DMA((N_DEV - 1,))] * 4,
        compiler_params=_params(),
    )(part)


def _conv_rows(ta):
    return _tile(ta, 64, SUBLANES)


def _branch_a_fwd(proj, conv_w_full, conv_b, gn_g, gn_b, seq):
    n_tok = proj.shape[0]
    d_model = conv_b.shape[1]
    ta = _tile(seq, 256, HALO)
    per_seq = seq // ta
    rc = _conv_rows(ta)

    def body(av_ref, ag_ref, gt_ref, avh_ref, agh_ref, cw_ref, cb_ref, gg_ref, gb_ref,
             h3_ref, h1_ref, ext):
        keep = jnp.where(pl.program_id(0) % per_seq == 0, 0.0, 1.0)

        def group(g, carry):
            sl = pl.ds(pl.multiple_of(g * LANES, LANES), LANES)
            ext[0:HALO, :] = avh_ref[:, sl] * _sigmoid(agh_ref[:, sl]) * keep
            ext[HALO:HALO + ta, :] = av_ref[:, sl] * _sigmoid(ag_ref[:, sl])
            for r0 in range(0, ta, rc):
                acc = jnp.broadcast_to(cb_ref[:, sl], (rc, LANES))
                for k in range(CONV_K):
                    acc = acc + ext[pl.ds(r0 + HALO - (CONV_K - 1) + k, rc), :] * cw_ref[k:k + 1, sl]
                h1_ref[pl.ds(r0, rc), sl] = acc
            h1 = h1_ref[:, sl]
            mu = jnp.mean(h1, axis=-1, keepdims=True)
            dlt = h1 - mu
            var = jnp.mean(dlt * dlt, axis=-1, keepdims=True)
            h2 = dlt * lax.rsqrt(var + LN_EPS) * gg_ref[:, sl] + gb_ref[:, sl]
            gate = gt_ref[:, sl]
            h3_ref[:, sl] = (h2 * _sigmoid(h2) * gate * _sigmoid(gate)).astype(BF16)
            return carry

        lax.fori_loop(0, d_model // LANES, group, 0, unroll=True)

    blk = lambda j: pl.BlockSpec((ta, d_model), lambda i: (i, j))
    halo = lambda j: pl.BlockSpec((HALO, d_model), lambda i: (jnp.maximum(i * (ta // HALO) - 1, 0), j))
    row = pl.BlockSpec((1, d_model), lambda i: (0, 0))
    return pl.pallas_call(
        body, name="branch_a_fwd", grid=(n_tok // ta,),
        in_specs=[blk(0), blk(1), blk(2), halo(0), halo(1),
                  pl.BlockSpec((HALO, d_model), lambda i: (0, 0)), row, row, row],
        out_specs=[pl.BlockSpec((ta, d_model), lambda i: (i, 0))] * 2,
        out_shape=[SDS((n_tok, d_model), BF16), SDS((n_tok, d_model), F32)],
        scratch_shapes=[pltpu.VMEM((HALO + ta, LANES), F32)],
        compiler_params=_params(1),
    )(proj, proj, proj, proj, proj, conv_w_full, conv_b, gn_g, gn_b)


def _branch_b_fwd(proj, ln_g, ln_b, w_spatial, b_spatial_t, seq):
    n_tok = proj.shape[0]
    d_model = ln_g.shape[1]
    n_heads = d_model // LANES
    tb = _tile(seq, 256, LANES)

    def body(u_ref, v_ref, bg_ref, lg_ref, lb_ref, ws_ref, bs_ref, s_ref, vn_buf):
        v, _ = _gelu_and_grad(v_ref[...])
        mu = jnp.mean(v, axis=-1, keepdims=True)
        dlt = v - mu
        var = jnp.mean(dlt * dlt, axis=-1, keepdims=True)
        vn_buf[...] = (dlt * lax.rsqrt(var + LN_EPS) * lg_ref[...] + lb_ref[...]).astype(BF16)
        tril = _tril_mask()
        for h in range(n_heads):
            cols = slice(h * LANES, (h + 1) * LANES)
            w_h = jnp.where(tril, ws_ref[h], 0.0).astype(BF16)
            bias = bs_ref[:, h:h + 1]
            for ch in range(tb // LANES):
                rows = slice(ch * LANES, (ch + 1) * LANES)
                mix = _dot(w_h, vn_buf[rows, cols]) + bias
                u, _ = _gelu_and_grad(u_ref[rows, cols])
                gate = bg_ref[rows, cols]
                s_ref[rows, cols] = (u * mix * gate * _sigmoid(gate)).astype(BF16)

    blk = lambda j: pl.BlockSpec((tb, d_model), lambda i: (i, j))
    row = pl.BlockSpec((1, d_model), lambda i: (0, 0))
    return pl.pallas_call(
        body, name="branch_b_fwd", grid=(n_tok // tb,),
        in_specs=[blk(3), blk(4), blk(5), row, row,
                  pl.BlockSpec((n_heads, LANES, LANES), lambda i: (0, 0, 0)),
                  pl.BlockSpec((LANES, n_heads), lambda i: (0, 0))],
        out_specs=pl.BlockSpec((tb, d_model), lambda i: (i, 0)),
        out_shape=SDS((n_tok, d_model), BF16),
        scratch_shapes=[pltpu.VMEM((tb, d_model), BF16)],
        compiler_params=_params(1),
    )(proj, proj, proj, ln_g, ln_b, w_spatial, b_spatial_t)


MID_ROWS = 8


def _mid(h3, s, proj, x2, target, w_pa, w_pb, w_o, b_o, lo_g, lo_b):
    n_tok, d_model = x2.shape
    tm = _tile(n_tok, 256, 16)

    def body(h3_ref, s_ref, ma_ref, mb_ref, x_ref, t_ref, wpa_ref, wpb_ref, wo_ref, bo_ref,
             lg_ref, lb_ref, dproj_ref, dh3_ref, ds_ref, dr_ref, drb_ref, mixed_ref, dya_ref,
             dyb_ref, vec_ref, dp67_ref):
        @pl.when(pl.program_id(0) == 0)
        def _():
            vec_ref[...] = jnp.zeros_like(vec_ref)

        ya = _dot(h3_ref[...], wpa_ref[...])
        yb = _dot(s_ref[...], wpb_ref[...])
        ga = _sigmoid(ma_ref[...])
        gb = _sigmoid(mb_ref[...])
        mixed = (ga * ya + gb * yb).astype(BF16)
        mixed_ref[...] = mixed
        r = DEEPNORM_ALPHA * x_ref[...] + _dot(mixed, wo_ref[...]) + bo_ref[...]
        mu = jnp.mean(r, axis=-1, keepdims=True)
        dlt = r - mu
        rstd = lax.rsqrt(jnp.mean(dlt * dlt, axis=-1, keepdims=True) + LN_EPS)
        rhat = dlt * rstd
        diff = rhat * lg_ref[...] + lb_ref[...] - t_ref[...]
        dy = diff * (1.0 / d_model)
        vec_ref[0:1, :] += _colsum(dy * rhat)
        vec_ref[1:2, :] += _colsum(dy)
        vec_ref[5:6, :] += _colsum(diff * diff)
        drh = dy * lg_ref[...]
        dr = rstd * (drh - jnp.mean(drh, axis=-1, keepdims=True)
                     - rhat * jnp.mean(drh * rhat, axis=-1, keepdims=True))
        vec_ref[2:3, :] += _colsum(dr)
        dr_ref[...] = dr
        drb = dr.astype(BF16)
        drb_ref[...] = drb
        dmixed = _dot_tb(drb, wo_ref[...])
        dma = dmixed * ya * ga * (1.0 - ga)
        dmb = dmixed * yb * gb * (1.0 - gb)
        vec_ref[3:4, :] += _colsum(dma)
        vec_ref[4:5, :] += _colsum(dmb)
        dp67 = jnp.concatenate([dma.astype(BF16), dmb.astype(BF16)], axis=1)
        dproj_ref[...] = dp67
        dp67_ref[...] = dp67
        dya = (dmixed * ga).astype(BF16)
        dyb = (dmixed * gb).astype(BF16)
        dya_ref[...] = dya
        dyb_ref[...] = dyb
        dh3_ref[...] = _dot_tb(dya, wpa_ref[...])
        ds_ref[...] = _dot_tb(dyb, wpb_ref[...])

    tile = pl.BlockSpec((tm, d_model), lambda i: (i, 0))
    full = pl.BlockSpec((d_model, d_model), lambda i: (0, 0))
    row = pl.BlockSpec((1, d_model), lambda i: (0, 0))
    bf = SDS((n_tok, d_model), BF16)
    f32 = SDS((n_tok, d_model), F32)
    return pl.pallas_call(
        body, name="mid", grid=(n_tok // tm,),
        in_specs=[tile, tile, pl.BlockSpec((tm, d_model), lambda i: (i, 6)),
                  pl.BlockSpec((tm, d_model), lambda i: (i, 7)), tile, tile, full, full, full,
                  row, row, row],
        out_specs=[pl.BlockSpec((tm, 2 * d_model), lambda i: (i, 3)), tile, tile, tile, tile, tile,
                   tile, tile, pl.BlockSpec((MID_ROWS, d_model), lambda i: (0, 0)),
                   pl.BlockSpec((tm, 2 * d_model), lambda i: (i, 0))],
        out_shape=[SDS((n_tok, N_DEV * d_model), BF16), f32, f32, f32, bf, bf, bf, bf,
                   SDS((MID_ROWS, d_model), F32), SDS((n_tok, 2 * d_model), BF16)],
        compiler_params=_params(1),
    )(h3, s, proj, proj, x2, target, w_pa, w_pb, w_o, b_o, lo_g, lo_b)


B_ROWS = 8


def _branch_b_bwd(dproj, proj, d_s, ln_g, ln_b, w_spatial, b_spatial_t, dp67, dr, w_all, gp_bf, gp_f32, seq):
    n_tok = proj.shape[0]
    d_model = ln_g.shape[1]
    n_heads = d_model // LANES
    tb = _tile(seq, 256, LANES)
    n_tiles = n_tok // tb
    r8 = d_model // N_DEV
    prow = _tile(r8, 32, 16)
    gx_cols = _tile(d_model, 256, LANES)
    heads_per_chunk = n_heads // (d_model // gx_cols)

    def body(dproj_in, u_ref, v_ref, bg_ref, ds_ref, lg_ref, lb_ref, ws_ref, bs_ref,
             dp67_ref, dr_ref, w345_ref, w67_ref, gpa_bf, gpb_bf, gpo_bf, gpa_f32, gpb_f32, gpo_f32,
             dproj_ref, vec_out, dws_out, dbs_out, gx_ref, opa_ref, opb_ref, opo_ref,
             vn_buf, dv_buf, dp_prev, vec_ref, dws_ref, dbs_ref, rbuf, own, send_sems, recv_sems, local_sems):
        del dproj_in
        i = pl.program_id(0)
        x, y, c = _mesh_pos()
        me = _block_of((x, y, c))
        peers = _peers()
        gp_bf_refs = [gpa_bf, gpb_bf, gpo_bf]
        gp_f32_refs = [gpa_f32, gpb_f32, gpo_f32]

        def sends():
            return [pltpu.make_async_remote_copy(
                src_ref=gp_bf_refs[a].at[pl.ds(pl.multiple_of(blk * r8, 16), r8), :], dst_ref=rbuf.at[k, a],
                send_sem=send_sems.at[a, k], recv_sem=recv_sems.at[a, k], device_id=pos, device_id_type=MESH)
                for k, (pos, blk) in enumerate(peers) for a in range(3)]

        def own_rows():
            return [pltpu.make_async_copy(gp_f32_refs[a].at[pl.ds(pl.multiple_of(me * r8, 8), r8), :],
                                          own.at[a], local_sems.at[a]) for a in range(3)]

        @pl.when(i == 0)
        def _():
            dp_prev[...] = jnp.zeros_like(dp_prev)
            vec_out[...] = jnp.zeros_like(vec_out)
            dws_out[...] = jnp.zeros_like(dws_out)
            dbs_out[...] = jnp.zeros_like(dbs_out)
            for cp in sends() + own_rows():
                cp.start()

        def gx_columns(q):
            cs = slice(q * gx_cols, (q + 1) * gx_cols)
            acc = DEEPNORM_ALPHA * dr_ref[:, cs]
            for j in range(2):
                acc = acc + _dot_tb(dp67_ref[:, j * d_model:(j + 1) * d_model], w67_ref[j, cs, :])
            for j in range(3):
                acc = acc + _dot_tb(dp_prev[:, j * d_model:(j + 1) * d_model], w345_ref[j, cs, :])
            gx_ref[:, cs] = acc

        vec_ref[...] = jnp.zeros_like(vec_ref)
        dws_ref[...] = jnp.zeros_like(dws_ref)
        dbs_ref[...] = jnp.zeros_like(dbs_ref)

        v, dgelu_v = _gelu_and_grad(v_ref[...])
        mu = jnp.mean(v, axis=-1, keepdims=True)
        dlt = v - mu
        rstd = lax.rsqrt(jnp.mean(dlt * dlt, axis=-1, keepdims=True) + LN_EPS)
        vhat = dlt * rstd
        vn_buf[...] = (vhat * lg_ref[...] + lb_ref[...]).astype(BF16)
        tril = _tril_mask()
        for h in range(n_heads):
            cols = slice(h * LANES, (h + 1) * LANES)
            w_h = jnp.where(tril, ws_ref[h], 0.0).astype(BF16)
            bias = bs_ref[:, h:h + 1]
            for ch in range(tb // LANES):
                rows = slice(ch * LANES, (ch + 1) * LANES)
                vn = vn_buf[rows, cols]
                mix = _dot(w_h, vn) + bias
                u, dgelu_u = _gelu_and_grad(u_ref[rows, cols])
                sg, dsilu = _silu_and_grad(bg_ref[rows, cols])
                dsv = ds_ref[rows, cols]
                du = dsv * mix * sg * dgelu_u
                dbg = dsv * u * mix * dsilu
                dmix = dsv * u * sg
                dmix_bf = dmix.astype(BF16)
                dproj_ref[rows, cols] = du.astype(BF16)
                dproj_ref[rows, 2 * d_model + h * LANES:2 * d_model + (h + 1) * LANES] = dbg.astype(BF16)
                vec_ref[0:1, cols] += _colsum(du)
                vec_ref[2:3, cols] += _colsum(dbg)
                dbs_ref[:, h:h + 1] += jnp.sum(dmix, axis=1, keepdims=True)
                dws_ref[h] += jnp.where(tril, _dot_tb(dmix_bf, vn), 0.0)
                dv_buf[rows, cols] = _dot_ta(w_h, dmix_bf)
            if (h + 1) % heads_per_chunk == 0:
                gx_columns((h + 1) // heads_per_chunk - 1)
        dvn = dv_buf[...]
        vec_ref[3:4, :] += _colsum(dvn * vhat)
        vec_ref[4:5, :] += _colsum(dvn)
        dvh = dvn * lg_ref[...]
        dv = rstd * (dvh - jnp.mean(dvh, axis=-1, keepdims=True)
                     - vhat * jnp.mean(dvh * vhat, axis=-1, keepdims=True)) * dgelu_v
        vec_ref[1:2, :] += _colsum(dv)
        dproj_ref[:, d_model:2 * d_model] = dv.astype(BF16)
        dp_prev[...] = dproj_ref[...]

        @pl.when(i < n_tiles)
        def _():
            vec_out[...] += vec_ref[...]
            dws_out[...] += dws_ref[...]
            dbs_out[...] += dbs_ref[...]

        @pl.when(i == n_tiles)
        def _():
            for cp in own_rows():
                cp.wait()
            for cp in sends():
                cp.wait_recv()
            for a, o in enumerate([opa_ref, opb_ref, opo_ref]):
                for q in range(r8 // prow):
                    r = pl.ds(q * prow, prow)
                    tot = own[a, r, :]
                    for k in range(N_DEV - 1):
                        tot = tot + rbuf[k, a, r, :].astype(F32)
                    o[r, :] = tot
            for cp in sends():
                cp.wait_send()

    cur = lambda i: jnp.minimum(i, n_tiles - 1)
    prv = lambda i: jnp.maximum(i - 1, 0)
    blk = lambda j: pl.BlockSpec((tb, d_model), lambda i: (cur(i), j))
    prev_tile = pl.BlockSpec((tb, d_model), lambda i: (prv(i), 0))
    row = pl.BlockSpec((1, d_model), lambda i: (0, 0))
    shard = pl.BlockSpec((r8, d_model), lambda i: (0, 0))
    return pl.pallas_call(
        body, name="branch_b_bwd", grid=(n_tiles + 1,),
        in_specs=[ANY, blk(3), blk(4), blk(5), pl.BlockSpec((tb, d_model), lambda i: (cur(i), 0)), row, row,
                  pl.BlockSpec((n_heads, LANES, LANES), lambda i: (0, 0, 0)),
                  pl.BlockSpec((LANES, n_heads), lambda i: (0, 0)),
                  pl.BlockSpec((tb, 2 * d_model), lambda i: (prv(i), 0)), prev_tile,
                  pl.BlockSpec((3, d_model, d_model), lambda i: (1, 0, 0)),
                  pl.BlockSpec((2, d_model, d_model), lambda i: (3, 0, 0))]
        + [ANY] * 6,
        out_specs=[pl.BlockSpec((tb, 3 * d_model), lambda i: (cur(i), 1)),
                   pl.BlockSpec((B_ROWS, d_model), lambda i: (0, 0)),
                   pl.BlockSpec((n_heads, LANES, LANES), lambda i: (0, 0, 0)),
                   pl.BlockSpec((LANES, n_heads), lambda i: (0, 0)), prev_tile,
                   shard, shard, shard],
        out_shape=[SDS(dproj.shape, BF16), SDS((B_ROWS, d_model), F32),
                   SDS((n_heads, LANES, LANES), F32), SDS((LANES, n_heads), F32),
                   SDS((n_tok, d_model), F32)] + [SDS((r8, d_model), F32)] * 3,
        scratch_shapes=[pltpu.VMEM((tb, d_model), BF16), pltpu.VMEM((tb, d_model), F32),
                        pltpu.VMEM((tb, 3 * d_model), BF16), pltpu.VMEM((B_ROWS, d_model), F32),
                        pltpu.VMEM((n_heads, LANES, LANES), F32), pltpu.VMEM((LANES, n_heads), F32),
                        pltpu.VMEM((N_DEV - 1, 3, r8, d_model), BF16), pltpu.VMEM((3, r8, d_model), F32),
                        pltpu.SemaphoreType.DMA((3, 7)), pltpu.SemaphoreType.DMA((3, 7)),
                        pltpu.SemaphoreType.DMA((3,))],
        input_output_aliases={0: 0},
        compiler_params=_params(1),
    )(dproj, proj, proj, proj, d_s, ln_g, ln_b, w_spatial, b_spatial_t, dp67, dr, w_all, w_all,
      *gp_bf, *gp_f32)


A1_ROWS = 8


def _branch_a_bwd_norm(dproj, proj, h1, d_h3, gn_g, gn_b, gx, w_all, seq):
    n_tok = proj.shape[0]
    d_model = gn_g.shape[1]
    ta = _tile(seq, 256, 16)

    def body(dproj_in, gt_ref, h1_ref, dh3_ref, gg_ref, gb_ref, gx_in, w2_ref,
             dproj_ref, dh1_ref, vec_ref, gx_ref):
        del dproj_in

        @pl.when(pl.program_id(0) == 0)
        def _():
            vec_ref[...] = jnp.zeros_like(vec_ref)

        def group(g, carry):
            sl = pl.ds(pl.multiple_of(g * LANES, LANES), LANES)
            h1 = h1_ref[:, sl]
            mu = jnp.mean(h1, axis=-1, keepdims=True)
            dlt = h1 - mu
            rstd = lax.rsqrt(jnp.mean(dlt * dlt, axis=-1, keepdims=True) + LN_EPS)
            nrm = dlt * rstd
            sw, dsw = _silu_and_grad(nrm * gg_ref[:, sl] + gb_ref[:, sl])
            sg, dsg = _silu_and_grad(gt_ref[:, sl])
            dh3 = dh3_ref[:, sl]
            dgate = dh3 * sw * dsg
            dproj_ref[:, sl] = dgate.astype(BF16)
            vec_ref[0:1, sl] += _colsum(dgate)
            dh2 = dh3 * sg * dsw
            vec_ref[1:2, sl] += _colsum(dh2 * nrm)
            vec_ref[2:3, sl] += _colsum(dh2)
            dn = dh2 * gg_ref[:, sl]
            dh1 = rstd * (dn - jnp.mean(dn, axis=-1, keepdims=True)
                          - nrm * jnp.mean(dn * nrm, axis=-1, keepdims=True))
            vec_ref[3:4, sl] += _colsum(dh1)
            dh1_ref[:, sl] = dh1
            return carry

        lax.fori_loop(0, d_model // LANES, group, 0, unroll=True)
        gx_ref[...] = gx_in[...] + _dot_tb(dproj_ref[...], w2_ref[...])

    tile = pl.BlockSpec((ta, d_model), lambda i: (i, 0))
    row = pl.BlockSpec((1, d_model), lambda i: (0, 0))
    return pl.pallas_call(
        body, name="branch_a_bwd_norm", grid=(n_tok // ta,),
        in_specs=[ANY, pl.BlockSpec((ta, d_model), lambda i: (i, 2)), tile, tile, row, row, tile,
                  pl.BlockSpec((None, d_model, d_model), lambda i: (2, 0, 0))],
        out_specs=[pl.BlockSpec((ta, d_model), lambda i: (i, 2)), tile,
                   pl.BlockSpec((A1_ROWS, d_model), lambda i: (0, 0)), tile],
        out_shape=[SDS(dproj.shape, BF16), SDS((n_tok, d_model), F32), SDS((A1_ROWS, d_model), F32),
                   SDS((n_tok, d_model), F32)],
        input_output_aliases={0: 0, 6: 3},
        compiler_params=_params(1),
    )(dproj, proj, h1, d_h3, gn_g, gn_b, gx, w_all)


A2_ROWS = 8


def _branch_a_bwd_conv(dproj, proj, d_h1, conv_w_full, gx, w_all, seq):
    n_tok = proj.shape[0]
    d_model = conv_w_full.shape[1]
    n_groups = d_model // LANES
    ta = _tile(seq, 256, HALO)
    n_tiles = n_tok // ta
    per_seq = seq // ta
    rc = _conv_rows(ta)
    last_halo = n_tok // HALO - 1

    def body(dproj_in, av_ref, ag_ref, avh_ref, agh_ref, dh1_ref, dh1h_ref, cw_ref, gx_in, w01_ref,
             dproj_ref, vec_ref, dcw_ref, gx_ref, ext_h0, ext_d):
        del dproj_in
        i = pl.program_id(0)

        @pl.when(i == 0)
        def _():
            vec_ref[...] = jnp.zeros_like(vec_ref)
            dcw_ref[...] = jnp.zeros_like(dcw_ref)

        keep_past = jnp.where(i % per_seq == 0, 0.0, 1.0)
        keep_next = jnp.where(i % per_seq == per_seq - 1, 0.0, 1.0)

        def group(g, carry):
            sl = pl.ds(pl.multiple_of(g * LANES, LANES), LANES)
            av = av_ref[:, sl]
            sig = _sigmoid(ag_ref[:, sl])
            ext_h0[0:HALO, :] = avh_ref[:, sl] * _sigmoid(agh_ref[:, sl]) * keep_past
            ext_h0[HALO:HALO + ta, :] = av * sig
            ext_d[0:ta, :] = dh1_ref[:, sl]
            ext_d[ta:ta + HALO, :] = dh1h_ref[:, sl] * keep_next
            for r0 in range(0, ta, rc):
                dh1 = ext_d[pl.ds(r0, rc), :]
                acc = jnp.zeros((rc, LANES), F32)
                for k in range(CONV_K):
                    acc = acc + ext_d[pl.ds(r0 + CONV_K - 1 - k, rc), :] * cw_ref[k:k + 1, sl]
                    prod = dh1 * ext_h0[pl.ds(r0 + HALO - (CONV_K - 1) + k, rc), :]
                    dcw_ref[g, k] += jnp.sum(prod.reshape(rc // SUBLANES, SUBLANES, LANES), axis=0)
                rows = pl.ds(r0, rc)
                sig_r = sig[r0:r0 + rc]
                dav = acc * sig_r
                dag = acc * av[r0:r0 + rc] * sig_r * (1.0 - sig_r)
                dproj_ref[rows, sl] = dav.astype(BF16)
                dproj_ref[rows, pl.ds(pl.multiple_of(d_model + g * LANES, LANES), LANES)] = dag.astype(BF16)
                vec_ref[0:1, sl] += _colsum(dav)
                vec_ref[1:2, sl] += _colsum(dag)
            return carry

        lax.fori_loop(0, n_groups, group, 0, unroll=True)
        gx_ref[...] = (gx_in[...] + _dot_tb(dproj_ref[:, 0:d_model], w01_ref[0])
                       + _dot_tb(dproj_ref[:, d_model:2 * d_model], w01_ref[1]))

    blk = lambda j: pl.BlockSpec((ta, d_model), lambda i: (i, j))
    halo = lambda j: pl.BlockSpec((HALO, d_model), lambda i: (jnp.maximum(i * (ta // HALO) - 1, 0), j))
    tile = pl.BlockSpec((ta, d_model), lambda i: (i, 0))
    return pl.pallas_call(
        body, name="branch_a_bwd_conv", grid=(n_tiles,),
        in_specs=[ANY, blk(0), blk(1), halo(0), halo(1), tile,
                  pl.BlockSpec((HALO, d_model), lambda i: (jnp.minimum((i + 1) * (ta // HALO), last_halo), 0)),
                  pl.BlockSpec((HALO, d_model), lambda i: (0, 0)), tile,
                  pl.BlockSpec((2, d_model, d_model), lambda i: (0, 0, 0))],
        out_specs=[pl.BlockSpec((ta, 2 * d_model), lambda i: (i, 0)),
                   pl.BlockSpec((A2_ROWS, d_model), lambda i: (0, 0)),
                   pl.BlockSpec((n_groups, HALO, SUBLANES, LANES), lambda i: (0, 0, 0, 0)), tile],
        out_shape=[SDS(dproj.shape, BF16), SDS((A2_ROWS, d_model), F32),
                   SDS((n_groups, HALO, SUBLANES, LANES), F32), SDS((n_tok, d_model), F32)],
        scratch_shapes=[pltpu.VMEM((HALO + ta, LANES), F32), pltpu.VMEM((ta + HALO, LANES), F32)],
        input_output_aliases={0: 0, 8: 3},
        compiler_params=_params(1),
    )(dproj, proj, proj, proj, proj, d_h1, d_h1, conv_w_full, gx, w_all)


def _weight_grad(lhs, rhs, name):
    n_tok, d_model = lhs.shape
    tk = _tile(n_tok, 2048, 16)
    n_k = n_tok // tk

    def body(a_ref, g_ref, o_ref, ob_ref):
        part = _dot_ta(a_ref[...], g_ref[...])

        @pl.when(pl.program_id(0) == 0)
        def _():
            o_ref[...] = part

        @pl.when(pl.program_id(0) != 0)
        def _():
            o_ref[...] += part

        @pl.when(pl.program_id(0) == n_k - 1)
        def _():
            ob_ref[...] = o_ref[...].astype(BF16)

    tile = pl.BlockSpec((tk, d_model), lambda i: (i, 0))
    out = pl.BlockSpec((d_model, d_model), lambda i: (0, 0))
    return pl.pallas_call(
        body, name=name, grid=(n_k,), in_specs=[tile, tile], out_specs=[out, out],
        out_shape=[SDS((d_model, d_model), F32), SDS((d_model, d_model), BF16)],
        compiler_params=_params(1),
    )(lhs, rhs)


def _grad_w_in_reduce_scatter(xt_bf, dproj, blk_order):
    n_tok = dproj.shape[0]
    d_model = xt_bf.shape[0]
    dh = d_model // 2
    n_units = 2 * N_DEV

    def body(ord_ref, a_ref, g_ref, o_ref, acc, fb, sbuf, gbuf, tbuf, rfin,
             send_f, send_s, recv_g, recv_t, recv_f, out_sems):
        del ord_ref
        u = pl.program_id(0)
        s = u // 2
        hf = u % 2
        rnd = s // 2
        x, y, c = _mesh_pos()
        sibling = (x, y, 1 - c)
        n1 = (jnp.bitwise_xor(x, c), jnp.bitwise_xor(y, 1 - c), c)
        n2 = (jnp.bitwise_xor(x, 1 - c), jnp.bitwise_xor(y, c), c)

        def feed(r, half):
            return pltpu.make_async_remote_copy(
                src_ref=fb.at[half], dst_ref=gbuf.at[r, half], send_sem=send_f.at[r, half],
                recv_sem=recv_g.at[r, half], device_id=sibling, device_id_type=MESH)

        def feed_sibling(half):
            return pltpu.make_async_remote_copy(
                src_ref=fb.at[half], dst_ref=rfin.at[0, half], send_sem=send_f.at[3, half],
                recv_sem=recv_f.at[0, half], device_id=sibling, device_id_type=MESH)

        def chip_sum(r, half):
            dst = [tbuf.at[half], rfin.at[2, half], rfin.at[1, half]][r]
            sem = [recv_t.at[half], recv_f.at[2, half], recv_f.at[1, half]][r]
            return pltpu.make_async_remote_copy(
                src_ref=sbuf.at[r, half], dst_ref=dst, send_sem=send_s.at[r, half], recv_sem=sem,
                device_id=[n2, n2, n1][r], device_id_type=MESH)

        def out_copy(half):
            return pltpu.make_async_copy(acc.at[half], o_ref.at[:, pl.ds(half * dh, dh)], out_sems.at[half])

        def partial_sum():
            return _dot(a_ref[:, 0:n_tok], g_ref[...])

        for half in range(2):
            for r in range(3):
                @pl.when(u == 4 * r + 4 + half)
                def _(r=r, half=half):
                    feed(r, half).wait_send()

                @pl.when(u == 4 * r + 2 + half)
                def _(r=r, half=half):
                    feed(r, half).wait_recv()
                    if r == 2:
                        chip_sum(0, half).wait_recv()

            @pl.when(u == 14 + half)
            def _(half=half):
                feed_sibling(half).wait_recv()
                chip_sum(2, half).wait_recv()
                chip_sum(1, half).wait_recv()

        @pl.when(jnp.logical_and(s % 2 == 0, s < 7))
        def _():
            fb[hf] = partial_sum().astype(BF16)

        @pl.when(jnp.logical_or(s == 1, s == 3))
        def _():
            sbuf[rnd, hf] = (partial_sum() + gbuf[rnd, hf].astype(F32)).astype(BF16)

        @pl.when(s == 5)
        def _():
            sbuf[2, hf] = (partial_sum() + gbuf[2, hf].astype(F32) + tbuf[hf].astype(F32)).astype(BF16)

        @pl.when(s == 7)
        def _():
            acc[hf] = (partial_sum() + rfin[0, hf].astype(F32) + rfin[1, hf].astype(F32)
                       + rfin[2, hf].astype(F32))

        for half in range(2):
            for r in range(3):
                @pl.when(u == 4 * r + half)
                def _(r=r, half=half):
                    feed(r, half).start()

                @pl.when(u == 4 * r + 2 + half)
                def _(r=r, half=half):
                    chip_sum(r, half).start()

            @pl.when(u == 12 + half)
            def _(half=half):
                feed_sibling(half).start()

        @pl.when(u == 14)
        def _():
            out_copy(0).start()

        @pl.when(u == 15)
        def _():
            out_copy(1).start()
            for half in range(2):
                feed_sibling(half).wait_send()
                for r in range(3):
                    chip_sum(r, half).wait_send()
                out_copy(half).wait()

    grid_spec = pltpu.PrefetchScalarGridSpec(
        num_scalar_prefetch=1, grid=(n_units,),
        in_specs=[VMEM, pl.BlockSpec((n_tok, dh), lambda u, o: (0, 2 * o[u // 2] + u % 2))],
        out_specs=ANY,
        scratch_shapes=[pltpu.VMEM((2, d_model, dh), F32), pltpu.VMEM((2, d_model, dh), BF16),
                        pltpu.VMEM((3, 2, d_model, dh), BF16), pltpu.VMEM((3, 2, d_model, dh), BF16),
                        pltpu.VMEM((2, d_model, dh), BF16), pltpu.VMEM((3, 2, d_model, dh), BF16),
                        pltpu.SemaphoreType.DMA((4, 2)), pltpu.SemaphoreType.DMA((3, 2)),
                        pltpu.SemaphoreType.DMA((3, 2)), pltpu.SemaphoreType.DMA((2,)),
                        pltpu.SemaphoreType.DMA((3, 2)),
                        pltpu.SemaphoreType.DMA((2,))])
    return pl.pallas_call(
        body, name="grad_w_in_reduce_scatter", grid_spec=grid_spec,
        out_shape=SDS((d_model, d_model), F32), compiler_params=_params(1),
    )(blk_order, xt_bf, dproj)


def _adamw_math(w, g, m, v):
    m = ADAM_B1 * m + (1.0 - ADAM_B1) * g
    v = ADAM_B2 * v + (1.0 - ADAM_B2) * (g * g)
    m_hat = m / (1.0 - ADAM_B1 ** ADAM_STEP)
    v_hat = v / (1.0 - ADAM_B2 ** ADAM_STEP)
    delta = -ADAM_LR * (m_hat / (jnp.sqrt(v_hat) + ADAM_EPS) + ADAM_WD * w)
    return delta, m, v


def _adamw_tiled(w, g, m, v):
    n_rows, n_cols = w.shape
    tr = _tile(n_rows, 256, SUBLANES)

    def body(w_ref, g_ref, m_ref, v_ref, d_ref, mo_ref, vo_ref):
        d_ref[...], mo_ref[...], vo_ref[...] = _adamw_math(w_ref[...], g_ref[...], m_ref[...], v_ref[...])

    tile = pl.BlockSpec((tr, n_cols), lambda i: (i, 0))
    return pl.pallas_call(
        body, name="adamw_w_in", grid=(n_rows // tr,), in_specs=[tile] * 4, out_specs=[tile] * 3,
        out_shape=[SDS(w.shape, F32)] * 3, compiler_params=_params(1),
    )(w, g, m, v)


def _adamw_many(groups):
    n = len(groups)

    def body(*refs):
        ins, outs = refs[:4 * n], refs[4 * n:]
        for p in range(n):
            w_ref, g_ref, m_ref, v_ref = ins[4 * p:4 * p + 4]
            d, m, v = _adamw_math(w_ref[...], g_ref[...], m_ref[...], v_ref[...])
            outs[3 * p][...] = d
            outs[3 * p + 1][...] = m
            outs[3 * p + 2][...] = v

    flat = [a for grp in groups for a in grp]
    out_shape = [SDS(grp[0].shape, F32) for grp in groups for _ in range(3)]
    res = pl.pallas_call(
        body, name="adamw_small", in_specs=[VMEM] * (4 * n), out_specs=[VMEM] * (3 * n),
        out_shape=out_shape, compiler_params=_params(),
    )(*flat)
    return [tuple(res[3 * p:3 * p + 3]) for p in range(n)]


def _as_rows(a):
    return a.reshape(-1, LANES)


def kernel(x, w_in, b_in, conv_w, conv_b, gn_g, gn_b, ln_v_g, ln_v_b, w_spatial, b_spatial, w_pa, w_pb, w_o, b_o, ln_out_g, ln_out_b, loss_target, m_w_in, m_b_in, m_conv_w, m_conv_b, m_gn_g, m_gn_b, m_ln_v_g, m_ln_v_b, m_w_spatial, m_b_spatial, m_w_pa, m_w_pb, m_w_o, m_b_o, m_ln_out_g, m_ln_out_b, v_w_in, v_b_in, v_conv_w, v_conv_b, v_gn_g, v_gn_b, v_ln_v_g, v_ln_v_b, v_w_spatial, v_b_spatial, v_w_pa, v_w_pb, v_w_o, v_b_o, v_ln_out_g, v_ln_out_b):
    n_batch, seq, d_model = x.shape
    n_tok = n_batch * seq
    n_heads = d_model // LANES
    dc = conv_w.shape[1]
    me = 4 * lax.axis_index("x") + 2 * lax.axis_index("y") + lax.axis_index("c")
    row = lambda a: a.reshape(1, d_model)

    x2 = x.reshape(n_tok, d_model)
    target2 = loss_target.reshape(n_tok, d_model)
    b_spatial_t = b_spatial.T

    first = jnp.where(lax.axis_index("c") == 1, 4, 2)
    second = 6 - first
    ag_rel = jnp.stack([0 * first, 0 * first + 1, first, second + 1, second, first + 1, 0 * first + 6, 0 * first + 7])
    ag_blocks = jnp.bitwise_xor(me, ag_rel).astype(jnp.int32)
    proj, xt_bf, w_all, wp_all, cw_all = _proj_all_gather(
        x2, w_in, w_pa, w_pb, w_o, conv_w, b_in.reshape(N_DEV, 1, d_model), ag_blocks)
    wp_full = [wp_all[:, a].reshape(d_model, d_model) for a in range(3)]
    conv_w_full = jnp.pad(cw_all.transpose(1, 0, 2).reshape(CONV_K, d_model), ((0, HALO - CONV_K), (0, 0)))

    h3, h1 = _branch_a_fwd(proj, conv_w_full, row(conv_b), row(gn_g), row(gn_b), seq)
    s = _branch_b_fwd(proj, row(ln_v_g), row(ln_v_b), w_spatial, b_spatial_t, seq)

    dproj, d_h3, d_s, dr, dr_bf, mixed, d_ya, d_yb, vec_mid, dp67 = _mid(
        h3, s, proj, x2, target2, *wp_full, row(b_o), row(ln_out_g), row(ln_out_b))
    gp = [_weight_grad(lhs, rhs, name) for lhs, rhs, name in
          ((h3, d_ya, "grad_w_pa"), (s, d_yb, "grad_w_pb"), (mixed, dr_bf, "grad_w_o"))]

    dproj, vec_b, d_ws, d_bs_t, gx, g_w_pa, g_w_pb, g_w_o = _branch_b_bwd(
        dproj, proj, d_s, row(ln_v_g), row(ln_v_b), w_spatial, b_spatial_t, dp67, dr, w_all,
        [g[1] for g in gp], [g[0] for g in gp], seq)
    dproj, d_h1, vec_a1, gx = _branch_a_bwd_norm(dproj, proj, h1, d_h3, row(gn_g), row(gn_b), gx, w_all, seq)
    dproj, vec_a2, d_cw8, grad_x = _branch_a_bwd_conv(dproj, proj, d_h1, conv_w_full, gx, w_all, seq)
    grad_x = grad_x.reshape(x.shape)

    rs_rel = jnp.stack([0 * first + 7, 0 * first + 6, first + 1, second, second + 1, first, 0 * first + 1, 0 * first])
    rs_blocks = jnp.bitwise_xor(me, rs_rel).astype(jnp.int32)
    g_w_in = _grad_w_in_reduce_scatter(xt_bf, dproj, rs_blocks)

    d_cw = jnp.sum(d_cw8, axis=2)
    pieces = [
        _as_rows(jnp.concatenate([vec_a2[0:2], vec_a1[0:1], vec_b[0:3], vec_mid[3:5]], axis=0)),
        _as_rows(jnp.concatenate([vec_a1[3:4], vec_a1[1:3], vec_b[3:5], vec_mid[2:3], vec_mid[0:2]], axis=0)),
        _as_rows(d_bs_t.T), _as_rows(d_ws), _as_rows(d_cw), _as_rows(vec_mid[5:6]),
    ]
    n_rows = sum(p.shape[0] for p in pieces)
    pad_rows = -n_rows % (N_DEV * SUBLANES)
    small = _all_reduce_small(jnp.concatenate(pieces + [jnp.zeros((pad_rows, LANES), F32)], axis=0))
    g_rows = d_model // LANES
    o0 = N_DEV * g_rows
    g_b_in = small[0:o0].reshape(N_DEV * d_model)
    vecs = [small[o0 + a * g_rows:o0 + (a + 1) * g_rows].reshape(d_model) for a in range(8)]
    g_conv_b, g_gn_g, g_gn_b, g_ln_v_g, g_ln_v_b, g_b_o, g_ln_out_g, g_ln_out_b = vecs
    o1 = o0 + 8 * g_rows
    g_b_spatial = small[o1:o1 + n_heads].reshape(n_heads, LANES)
    o2 = o1 + n_heads
    g_w_spatial = small[o2:o2 + n_heads * LANES].reshape(n_heads, LANES, LANES)
    o3 = o2 + n_heads * LANES
    g_cw_full = small[o3:o3 + n_heads * HALO].reshape(n_heads, HALO, LANES).transpose(1, 0, 2).reshape(HALO, d_model)
    g_conv_w = lax.dynamic_slice(g_cw_full, (0, me * dc), (CONV_K, dc))
    o4 = o3 + n_heads * HALO
    loss = jnp.sum(small[o4:o4 + g_rows]) * (0.5 / d_model)

    d_w_in, nm_w_in, nv_w_in = _adamw_tiled(w_in, g_w_in, m_w_in, v_w_in)
    two_d = lambda a: a.reshape(-1, a.shape[-1]) if a.ndim != 1 else (
        a.reshape(-1, LANES) if a.shape[0] % LANES == 0 else a.reshape(1, -1))
    names = ["b_in", "conv_w", "conv_b", "gn_g", "gn_b", "ln_v_g", "ln_v_b", "w_spatial", "b_spatial",
             "w_pa", "w_pb", "w_o", "b_o", "ln_out_g", "ln_out_b"]
    ws = dict(b_in=b_in, conv_w=conv_w, conv_b=conv_b, gn_g=gn_g, gn_b=gn_b, ln_v_g=ln_v_g, ln_v_b=ln_v_b,
              w_spatial=w_spatial, b_spatial=b_spatial, w_pa=w_pa, w_pb=w_pb, w_o=w_o, b_o=b_o,
              ln_out_g=ln_out_g, ln_out_b=ln_out_b)
    gs = dict(b_in=g_b_in, conv_w=g_conv_w, conv_b=g_conv_b, gn_g=g_gn_g, gn_b=g_gn_b, ln_v_g=g_ln_v_g,
              ln_v_b=g_ln_v_b, w_spatial=g_w_spatial, b_spatial=g_b_spatial, w_pa=g_w_pa, w_pb=g_w_pb,
              w_o=g_w_o, b_o=g_b_o, ln_out_g=g_ln_out_g, ln_out_b=g_ln_out_b)
    ms = dict(b_in=m_b_in, conv_w=m_conv_w, conv_b=m_conv_b, gn_g=m_gn_g, gn_b=m_gn_b, ln_v_g=m_ln_v_g,
              ln_v_b=m_ln_v_b, w_spatial=m_w_spatial, b_spatial=m_b_spatial, w_pa=m_w_pa, w_pb=m_w_pb,
              w_o=m_w_o, b_o=m_b_o, ln_out_g=m_ln_out_g, ln_out_b=m_ln_out_b)
    vs = dict(b_in=v_b_in, conv_w=v_conv_w, conv_b=v_conv_b, gn_g=v_gn_g, gn_b=v_gn_b, ln_v_g=v_ln_v_g,
              ln_v_b=v_ln_v_b, w_spatial=v_w_spatial, b_spatial=v_b_spatial, w_pa=v_w_pa, w_pb=v_w_pb,
              w_o=v_w_o, b_o=v_b_o, ln_out_g=v_ln_out_g, ln_out_b=v_ln_out_b)
    upd = _adamw_many([tuple(two_d(d[n]) for d in (ws, gs, ms, vs)) for n in names])
    delta = {n: u[0].reshape(ws[n].shape) for n, u in zip(names, upd)}
    new_m = {n: u[1].reshape(ws[n].shape) for n, u in zip(names, upd)}
    new_v = {n: u[2].reshape(ws[n].shape) for n, u in zip(names, upd)}
    gs["w_in"], delta["w_in"], new_m["w_in"], new_v["w_in"] = g_w_in, d_w_in, nm_w_in, nv_w_in

    order = ["w_in"] + names
    return (loss, grad_x, *[gs[n] for n in order], *[delta[n] for n in order],
            *[new_m[n] for n in order], *[new_v[n] for n in order])
```

```python
import jax
import jax.numpy as jnp
from jax import lax
from jax.experimental import pallas as pl
from jax.experimental.pallas import tpu as pltpu

F32 = jnp.float32
BF16 = jnp.bfloat16
SDS = jax.ShapeDtypeStruct

N_DEV = 8
LANES = 128
SUBLANES = 8
CONV_K = 31
HALO = 32
LN_EPS = 1e-5
DEEPNORM_ALPHA = 2.0 ** 0.25
ADAM_LR, ADAM_B1, ADAM_B2, ADAM_EPS, ADAM_WD, ADAM_STEP = 0.001, 0.9, 0.999, 1e-08, 0.01, 10
GELU_C = 0.7978845608028654
GELU_A = 0.044715
VMEM_LIMIT = 56 * 1024 * 1024
MESH = pl.DeviceIdType.MESH
ANY = pl.BlockSpec(memory_space=pl.ANY)
VMEM = pl.BlockSpec(memory_space=pltpu.VMEM)


def _params(n_grid=0):
    sem = ("arbitrary",) * n_grid if n_grid else None
    return pltpu.CompilerParams(dimension_semantics=sem, vmem_limit_bytes=VMEM_LIMIT)


def _tile(n, pref, mult):
    t = min(n, pref)
    while n % t or t % mult:
        t -= 1
    return t


def _colsum(v):
    return jnp.sum(v, axis=0, keepdims=True)


def _sigmoid(v):
    return jax.nn.sigmoid(v)


def _silu_and_grad(v):
    s = _sigmoid(v)
    return v * s, s * (1.0 + v * (1.0 - s))


def _gelu_and_grad(v):
    inner = GELU_C * (v + GELU_A * v * v * v)
    th = jnp.tanh(inner)
    val = 0.5 * v * (1.0 + th)
    grad = 0.5 * (1.0 + th) + 0.5 * v * (1.0 - th * th) * GELU_C * (1.0 + 3.0 * GELU_A * v * v)
    return val, grad


def _tril_mask():
    r = lax.broadcasted_iota(jnp.int32, (LANES, LANES), 0)
    c = lax.broadcasted_iota(jnp.int32, (LANES, LANES), 1)
    return c <= r


def _dot(a, b):
    return jnp.dot(a, b, preferred_element_type=F32)


def _dot_tb(a, b):
    return lax.dot_general(a, b, (((1,), (1,)), ((), ())), preferred_element_type=F32)


def _dot_ta(a, b):
    return lax.dot_general(a, b, (((0,), (0,)), ((), ())), preferred_element_type=F32)


def _mesh_pos():
    return lax.axis_index("x"), lax.axis_index("y"), lax.axis_index("c")


def _block_of(pos):
    return 4 * pos[0] + 2 * pos[1] + pos[2]


def _peers():
    x, y, c = _mesh_pos()
    out = []
    for k in range(1, N_DEV):
        pos = (1 - x if k & 4 else x, 1 - y if k & 2 else y, 1 - c if k & 1 else c)
        out.append((pos, _block_of(pos)))
    return out


def _proj_all_gather(x2, w_in, w_pa, w_pb, w_o, conv_w, b_in3, blk_order):
    n_tok, d_model = x2.shape
    r8 = w_pa.shape[0]
    kc, dc = conv_w.shape
    tm = _tile(n_tok, 1024, LANES)
    n_t = n_tok // tm

    def body(ord_ref, x_ref, b_ref, win_ref, wpa_ref, wpb_ref, wo_ref, cw_ref,
             proj_ref, xt_ref, wall_ref, wp_ref, cwall_ref,
             wbuf, st_p, send_sems, recv_sems, local_sems, wall_sems):
        s = pl.program_id(0)
        t = pl.program_id(1)
        x, y, c = _mesh_pos()
        me = (x, y, c)
        sibling = (x, y, 1 - c)
        n1 = (jnp.bitwise_xor(x, c), jnp.bitwise_xor(y, 1 - c))
        n2 = (jnp.bitwise_xor(x, 1 - c), jnp.bitwise_xor(y, c))
        dg = (1 - x, 1 - y)
        outs = [wbuf, wp_ref, cwall_ref]
        srcs = [None, st_p, cw_ref]
        consumed = [me, sibling, (*n1, c), (*n2, 1 - c), (*n2, c), (*n1, 1 - c), (*dg, c), (*dg, 1 - c)]
        leaves = [(me, sibling), (me, (*n1, c)), (me, (*n2, c)), ((*n1, c), (*n2, c)),
                  ((*n1, c), sibling), ((*n2, c), sibling), ((*dg, c), sibling)]
        lands = [sibling, (*n1, c), (*n2, c), (*dg, c), (*n2, 1 - c), (*n1, 1 - c), (*dg, 1 - c)]

        def slot(o, pos):
            return o.at[_block_of(pos)]

        def copy(a, k, block, to, src=None):
            o = outs[a]
            return pltpu.make_async_remote_copy(
                src_ref=slot(o, block) if src is None else src, dst_ref=slot(o, block),
                send_sem=send_sems.at[a, k], recv_sem=recv_sems.at[a, k],
                device_id=to, device_id_type=MESH)

        def send(a, k):
            block, to = leaves[k]
            return copy(a, k, block, to, src=srcs[a] if k < 3 else None)

        def recv(a, k):
            return copy(a, k, lands[k], me)

        def local_copies():
            return [pltpu.make_async_copy(srcs[a], slot(outs[a], me), local_sems.at[a]) for a in (1, 2)]

        def to_hbm(step):
            return pltpu.make_async_copy(slot(wbuf, consumed[step]), slot(wall_ref, consumed[step]),
                                         wall_sems.at[step])

        def at_step(step):
            return pl.when(jnp.logical_and(s == step, t == 0))

        @at_step(0)
        def _():
            slot(wbuf, me)[...] = win_ref[...].astype(BF16)
            st_p[0] = wpa_ref[...].astype(BF16)
            st_p[1] = wpb_ref[...].astype(BF16)
            st_p[2] = wo_ref[...].astype(BF16)
            for a in range(3):
                send(a, 0).start()
                send(a, 1).start()
            for cp in local_copies():
                cp.start()
            to_hbm(0).start()

        @at_step(1)
        def _():
            recv(0, 0).wait_recv()
            to_hbm(1).start()

        for rnd in range(3):
            @at_step(2 + 2 * rnd)
            def _(rnd=rnd):
                if rnd == 0:
                    for a in range(3):
                        send(a, 2).start()
                recv(0, 1 + rnd).wait_recv()
                if rnd == 0:
                    send(0, 3).start()
                send(0, 4 + rnd).start()
                to_hbm(2 + 2 * rnd).start()

            @at_step(3 + 2 * rnd)
            def _(rnd=rnd):
                for a in (1, 2):
                    recv(a, 1 + rnd).wait_recv()
                    if rnd == 0:
                        send(a, 3).start()
                    send(a, 4 + rnd).start()
                recv(0, 4 + rnd).wait_recv()
                to_hbm(3 + 2 * rnd).start()

        xb = x_ref[...].astype(BF16)

        @pl.when(s == 0)
        def _():
            xt_ref[...] = xb.T

        proj_ref[...] = _dot(xb, wbuf[ord_ref[s]]) + b_ref[...]

        @pl.when(jnp.logical_and(s == N_DEV - 1, t == n_t - 1))
        def _():
            for a in (1, 2):
                for k in (0, 4, 5, 6):
                    recv(a, k).wait_recv()
            for a in range(3):
                for k in range(7):
                    send(a, k).wait_send()
            for cp in local_copies() + [to_hbm(step) for step in range(N_DEV)]:
                cp.wait()

    grid_spec = pltpu.PrefetchScalarGridSpec(
        num_scalar_prefetch=1, grid=(N_DEV, n_t),
        in_specs=[pl.BlockSpec((tm, d_model), lambda s, t, o: (t, 0)),
                  pl.BlockSpec((None, 1, d_model), lambda s, t, o: (o[s], 0, 0)),
                  VMEM, VMEM, VMEM, VMEM, VMEM],
        out_specs=[pl.BlockSpec((tm, d_model), lambda s, t, o: (t, o[s])),
                   pl.BlockSpec((d_model, tm), lambda s, t, o: (0, jnp.where(s == 0, t, n_t))),
                   ANY, ANY, ANY],
        scratch_shapes=[pltpu.VMEM((N_DEV, d_model, d_model), BF16), pltpu.VMEM((3, r8, d_model), BF16),
                        pltpu.SemaphoreType.DMA((3, 7)), pltpu.SemaphoreType.DMA((3, 7)),
                        pltpu.SemaphoreType.DMA((3,)), pltpu.SemaphoreType.DMA((N_DEV,))])
    return pl.pallas_call(
        body, name="proj_all_gather", grid_spec=grid_spec,
        out_shape=[SDS((n_tok, N_DEV * d_model), F32), SDS((d_model, (n_t + 1) * tm), BF16),
                   SDS((N_DEV, d_model, d_model), BF16), SDS((N_DEV, 3, r8, d_model), BF16),
                   SDS((N_DEV, kc, dc), F32)],
        compiler_params=_params(2),
    )(blk_order, x2, b_in3, w_in, w_pa, w_pb, w_o, conv_w)


def _conv_rows(ta):
    return _tile(ta, 64, SUBLANES)


def _branch_a_fwd(proj, conv_w_full, conv_b, gn_g, gn_b, seq):
    n_tok = proj.shape[0]
    d_model = conv_b.shape[1]
    ta = _tile(seq, 256, HALO)
    per_seq = seq // ta
    rc = _conv_rows(ta)

    def body(av_ref, ag_ref, gt_ref, avh_ref, agh_ref, cw_ref, cb_ref, gg_ref, gb_ref,
             h3_ref, h1_ref, ext):
        keep = jnp.where(pl.program_id(0) % per_seq == 0, 0.0, 1.0)

        def group(g, carry):
            sl = pl.ds(pl.multiple_of(g * LANES, LANES), LANES)
            ext[0:HALO, :] = avh_ref[:, sl] * _sigmoid(agh_ref[:, sl]) * keep
            ext[HALO:HALO + ta, :] = av_ref[:, sl] * _sigmoid(ag_ref[:, sl])
            for r0 in range(0, ta, rc):
                acc = jnp.broadcast_to(cb_ref[:, sl], (rc, LANES))
                for k in range(CONV_K):
                    acc = acc + ext[pl.ds(r0 + HALO - (CONV_K - 1) + k, rc), :] * cw_ref[k:k + 1, sl]
                h1_ref[pl.ds(r0, rc), sl] = acc
            h1 = h1_ref[:, sl]
            mu = jnp.mean(h1, axis=-1, keepdims=True)
            dlt = h1 - mu
            var = jnp.mean(dlt * dlt, axis=-1, keepdims=True)
            h2 = dlt * lax.rsqrt(var + LN_EPS) * gg_ref[:, sl] + gb_ref[:, sl]
            gate = gt_ref[:, sl]
            h3_ref[:, sl] = (h2 * _sigmoid(h2) * gate * _sigmoid(gate)).astype(BF16)
            return carry

        lax.fori_loop(0, d_model // LANES, group, 0, unroll=True)

    blk = lambda j: pl.BlockSpec((ta, d_model), lambda i: (i, j))
    halo = lambda j: pl.BlockSpec((HALO, d_model), lambda i: (jnp.maximum(i * (ta // HALO) - 1, 0), j))
    row = pl.BlockSpec((1, d_model), lambda i: (0, 0))
    return pl.pallas_call(
        body, name="branch_a_fwd", grid=(n_tok // ta,),
        in_specs=[blk(0), blk(1), blk(2), halo(0), halo(1),
                  pl.BlockSpec((HALO, d_model), lambda i: (0, 0)), row, row, row],
        out_specs=[pl.BlockSpec((ta, d_model), lambda i: (i, 0))] * 2,
        out_shape=[SDS((n_tok, d_model), BF16), SDS((n_tok, d_model), F32)],
        scratch_shapes=[pltpu.VMEM((HALO + ta, LANES), F32)],
        compiler_params=_params(1),
    )(proj, proj, proj, proj, proj, conv_w_full, conv_b, gn_g, gn_b)


def _branch_b_fwd(proj, ln_g, ln_b, w_spatial, b_spatial_t, seq):
    n_tok = proj.shape[0]
    d_model = ln_g.shape[1]
    n_heads = d_model // LANES
    tb = _tile(seq, 256, LANES)

    def body(u_ref, v_ref, bg_ref, lg_ref, lb_ref, ws_ref, bs_ref, s_ref, vn_buf):
        v, _ = _gelu_and_grad(v_ref[...])
        mu = jnp.mean(v, axis=-1, keepdims=True)
        dlt = v - mu
        var = jnp.mean(dlt * dlt, axis=-1, keepdims=True)
        vn_buf[...] = (dlt * lax.rsqrt(var + LN_EPS) * lg_ref[...] + lb_ref[...]).astype(BF16)
        tril = _tril_mask()
        for h in range(n_heads):
            cols = slice(h * LANES, (h + 1) * LANES)
            w_h = jnp.where(tril, ws_ref[h], 0.0).astype(BF16)
            bias = bs_ref[:, h:h + 1]
            for ch in range(tb // LANES):
                rows = slice(ch * LANES, (ch + 1) * LANES)
                mix = _dot(w_h, vn_buf[rows, cols]) + bias
                u, _ = _gelu_and_grad(u_ref[rows, cols])
                gate = bg_ref[rows, cols]
                s_ref[rows, cols] = (u * mix * gate * _sigmoid(gate)).astype(BF16)

    blk = lambda j: pl.BlockSpec((tb, d_model), lambda i: (i, j))
    row = pl.BlockSpec((1, d_model), lambda i: (0, 0))
    return pl.pallas_call(
        body, name="branch_b_fwd", grid=(n_tok // tb,),
        in_specs=[blk(3), blk(4), blk(5), row, row,
                  pl.BlockSpec((n_heads, LANES, LANES), lambda i: (0, 0, 0)),
                  pl.BlockSpec((LANES, n_heads), lambda i: (0, 0))],
        out_specs=pl.BlockSpec((tb, d_model), lambda i: (i, 0)),
        out_shape=SDS((n_tok, d_model), BF16),
        scratch_shapes=[pltpu.VMEM((tb, d_model), BF16)],
        compiler_params=_params(1),
    )(proj, proj, proj, ln_g, ln_b, w_spatial, b_spatial_t)


MID_ROWS = 8


def _mid(h3, s, proj, x2, target, w_pa, w_pb, w_o, b_o, lo_g, lo_b):
    n_tok, d_model = x2.shape
    tm = _tile(n_tok, 256, 16)

    def body(h3_ref, s_ref, ma_ref, mb_ref, x_ref, t_ref, wpa_ref, wpb_ref, wo_ref, bo_ref,
             lg_ref, lb_ref, dproj_ref, dh3_ref, ds_ref, dr_ref, lhs3_ref, rhs3_ref, vec_ref):
        @pl.when(pl.program_id(0) == 0)
        def _():
            vec_ref[...] = jnp.zeros_like(vec_ref)

        h3 = h3_ref[...]
        s = s_ref[...]
        ya = _dot(h3, wpa_ref[...])
        yb = _dot(s, wpb_ref[...])
        ga = _sigmoid(ma_ref[...])
        gb = _sigmoid(mb_ref[...])
        mixed = (ga * ya + gb * yb).astype(BF16)
        lhs3_ref[0] = h3
        lhs3_ref[1] = s
        lhs3_ref[2] = mixed
        r = DEEPNORM_ALPHA * x_ref[...] + _dot(mixed, wo_ref[...]) + bo_ref[...]
        mu = jnp.mean(r, axis=-1, keepdims=True)
        dlt = r - mu
        rstd = lax.rsqrt(jnp.mean(dlt * dlt, axis=-1, keepdims=True) + LN_EPS)
        rhat = dlt * rstd
        diff = rhat * lg_ref[...] + lb_ref[...] - t_ref[...]
        dy = diff * (1.0 / d_model)
        vec_ref[0:1, :] += _colsum(dy * rhat)
        vec_ref[1:2, :] += _colsum(dy)
        vec_ref[5:6, :] += _colsum(diff * diff)
        drh = dy * lg_ref[...]
        dr = rstd * (drh - jnp.mean(drh, axis=-1, keepdims=True)
                     - rhat * jnp.mean(drh * rhat, axis=-1, keepdims=True))
        vec_ref[2:3, :] += _colsum(dr)
        dr_ref[...] = dr
        drb = dr.astype(BF16)
        rhs3_ref[2] = drb
        dmixed = _dot_tb(drb, wo_ref[...])
        dma = dmixed * ya * ga * (1.0 - ga)
        dmb = dmixed * yb * gb * (1.0 - gb)
        vec_ref[3:4, :] += _colsum(dma)
        vec_ref[4:5, :] += _colsum(dmb)
        dproj_ref[:, 0:d_model] = dma.astype(BF16)
        dproj_ref[:, d_model:2 * d_model] = dmb.astype(BF16)
        dya = (dmixed * ga).astype(BF16)
        dyb = (dmixed * gb).astype(BF16)
        rhs3_ref[0] = dya
        rhs3_ref[1] = dyb
        dh3_ref[...] = _dot_tb(dya, wpa_ref[...])
        ds_ref[...] = _dot_tb(dyb, wpb_ref[...])

    tile = pl.BlockSpec((tm, d_model), lambda i: (i, 0))
    full = pl.BlockSpec((d_model, d_model), lambda i: (0, 0))
    row = pl.BlockSpec((1, d_model), lambda i: (0, 0))
    stack = pl.BlockSpec((3, tm, d_model), lambda i: (0, i, 0))
    bf3 = SDS((3, n_tok, d_model), BF16)
    f32 = SDS((n_tok, d_model), F32)
    return pl.pallas_call(
        body, name="mid", grid=(n_tok // tm,),
        in_specs=[tile, tile, pl.BlockSpec((tm, d_model), lambda i: (i, 6)),
                  pl.BlockSpec((tm, d_model), lambda i: (i, 7)), tile, tile, full, full, full,
                  row, row, row],
        out_specs=[pl.BlockSpec((tm, 2 * d_model), lambda i: (i, 3)), tile, tile, tile, stack, stack,
                   pl.BlockSpec((MID_ROWS, d_model), lambda i: (0, 0))],
        out_shape=[SDS((n_tok, N_DEV * d_model), BF16), f32, f32, f32, bf3, bf3,
                   SDS((MID_ROWS, d_model), F32)],
        compiler_params=_params(1),
    )(h3, s, proj, proj, x2, target, w_pa, w_pb, w_o, b_o, lo_g, lo_b)


B_ROWS = 8


def _branch_b_bwd(dproj, proj, d_s, ln_g, ln_b, w_spatial, b_spatial_t, seq):
    n_tok = proj.shape[0]
    d_model = ln_g.shape[1]
    n_heads = d_model // LANES
    tb = _tile(seq, 256, LANES)

    def body(dproj_in, u_ref, v_ref, bg_ref, ds_ref, lg_ref, lb_ref, ws_ref, bs_ref,
             dproj_ref, vec_ref, dws_ref, dbs_ref, vn_buf, dv_buf):
        del dproj_in

        @pl.when(pl.program_id(0) == 0)
        def _():
            vec_ref[...] = jnp.zeros_like(vec_ref)
            dws_ref[...] = jnp.zeros_like(dws_ref)
            dbs_ref[...] = jnp.zeros_like(dbs_ref)

        v, dgelu_v = _gelu_and_grad(v_ref[...])
        mu = jnp.mean(v, axis=-1, keepdims=True)
        dlt = v - mu
        rstd = lax.rsqrt(jnp.mean(dlt * dlt, axis=-1, keepdims=True) + LN_EPS)
        vhat = dlt * rstd
        vn_buf[...] = (vhat * lg_ref[...] + lb_ref[...]).astype(BF16)
        tril = _tril_mask()
        for h in range(n_heads):
            cols = slice(h * LANES, (h + 1) * LANES)
            w_h = jnp.where(tril, ws_ref[h], 0.0).astype(BF16)
            bias = bs_ref[:, h:h + 1]
            for ch in range(tb // LANES):
                rows = slice(ch * LANES, (ch + 1) * LANES)
                vn = vn_buf[rows, cols]
                mix = _dot(w_h, vn) + bias
                u, dgelu_u = _gelu_and_grad(u_ref[rows, cols])
                sg, dsilu = _silu_and_grad(bg_ref[rows, cols])
                dsv = ds_ref[rows, cols]
                du = dsv * mix * sg * dgelu_u
                dbg = dsv * u * mix * dsilu
                dmix = dsv * u * sg
                dmix_bf = dmix.astype(BF16)
                dproj_ref[rows, cols] = du.astype(BF16)
                dproj_ref[rows, 2 * d_model + h * LANES:2 * d_model + (h + 1) * LANES] = dbg.astype(BF16)
                vec_ref[0:1, cols] += _colsum(du)
                vec_ref[2:3, cols] += _colsum(dbg)
                dbs_ref[:, h:h + 1] += jnp.sum(dmix, axis=1, keepdims=True)
                dws_ref[h] += jnp.where(tril, _dot_tb(dmix_bf, vn), 0.0)
                dv_buf[rows, cols] = _dot_ta(w_h, dmix_bf)
        dvn = dv_buf[...]
        vec_ref[3:4, :] += _colsum(dvn * vhat)
        vec_ref[4:5, :] += _colsum(dvn)
        dvh = dvn * lg_ref[...]
        dv = rstd * (dvh - jnp.mean(dvh, axis=-1, keepdims=True)
                     - vhat * jnp.mean(dvh * vhat, axis=-1, keepdims=True)) * dgelu_v
        vec_ref[1:2, :] += _colsum(dv)
        dproj_ref[:, d_model:2 * d_model] = dv.astype(BF16)

    blk = lambda j: pl.BlockSpec((tb, d_model), lambda i: (i, j))
    row = pl.BlockSpec((1, d_model), lambda i: (0, 0))
    return pl.pallas_call(
        body, name="branch_b_bwd", grid=(n_tok // tb,),
        in_specs=[ANY, blk(3), blk(4), blk(5), pl.BlockSpec((tb, d_model), lambda i: (i, 0)), row, row,
                  pl.BlockSpec((n_heads, LANES, LANES), lambda i: (0, 0, 0)),
                  pl.BlockSpec((LANES, n_heads), lambda i: (0, 0))],
        out_specs=[pl.BlockSpec((tb, 3 * d_model), lambda i: (i, 1)),
                   pl.BlockSpec((B_ROWS, d_model), lambda i: (0, 0)),
                   pl.BlockSpec((n_heads, LANES, LANES), lambda i: (0, 0, 0)),
                   pl.BlockSpec((LANES, n_heads), lambda i: (0, 0))],
        out_shape=[SDS(dproj.shape, BF16), SDS((B_ROWS, d_model), F32),
                   SDS((n_heads, LANES, LANES), F32), SDS((LANES, n_heads), F32)],
        scratch_shapes=[pltpu.VMEM((tb, d_model), BF16), pltpu.VMEM((tb, d_model), F32)],
        input_output_aliases={0: 0},
        compiler_params=_params(1),
    )(dproj, proj, proj, proj, d_s, ln_g, ln_b, w_spatial, b_spatial_t)


A1_ROWS = 8


def _branch_a_bwd_norm(dproj, proj, h1, d_h3, gn_g, gn_b, seq):
    n_tok = proj.shape[0]
    d_model = gn_g.shape[1]
    ta = _tile(seq, 256, 16)

    def body(dproj_in, gt_ref, h1_ref, dh3_ref, gg_ref, gb_ref, dproj_ref, dh1_ref, vec_ref):
        del dproj_in

        @pl.when(pl.program_id(0) == 0)
        def _():
            vec_ref[...] = jnp.zeros_like(vec_ref)

        def group(g, carry):
            sl = pl.ds(pl.multiple_of(g * LANES, LANES), LANES)
            h1 = h1_ref[:, sl]
            mu = jnp.mean(h1, axis=-1, keepdims=True)
            dlt = h1 - mu
            rstd = lax.rsqrt(jnp.mean(dlt * dlt, axis=-1, keepdims=True) + LN_EPS)
            nrm = dlt * rstd
            sw, dsw = _silu_and_grad(nrm * gg_ref[:, sl] + gb_ref[:, sl])
            sg, dsg = _silu_and_grad(gt_ref[:, sl])
            dh3 = dh3_ref[:, sl]
            dgate = dh3 * sw * dsg
            dproj_ref[:, sl] = dgate.astype(BF16)
            vec_ref[0:1, sl] += _colsum(dgate)
            dh2 = dh3 * sg * dsw
            vec_ref[1:2, sl] += _colsum(dh2 * nrm)
            vec_ref[2:3, sl] += _colsum(dh2)
            dn = dh2 * gg_ref[:, sl]
            dh1 = rstd * (dn - jnp.mean(dn, axis=-1, keepdims=True)
                          - nrm * jnp.mean(dn * nrm, axis=-1, keepdims=True))
            vec_ref[3:4, sl] += _colsum(dh1)
            dh1_ref[:, sl] = dh1
            return carry

        lax.fori_loop(0, d_model // LANES, group, 0, unroll=True)

    tile = pl.BlockSpec((ta, d_model), lambda i: (i, 0))
    row = pl.BlockSpec((1, d_model), lambda i: (0, 0))
    return pl.pallas_call(
        body, name="branch_a_bwd_norm", grid=(n_tok // ta,),
        in_specs=[ANY, pl.BlockSpec((ta, d_model), lambda i: (i, 2)), tile, tile, row, row],
        out_specs=[pl.BlockSpec((ta, d_model), lambda i: (i, 2)), tile,
                   pl.BlockSpec((A1_ROWS, d_model), lambda i: (0, 0))],
        out_shape=[SDS(dproj.shape, BF16), SDS((n_tok, d_model), F32), SDS((A1_ROWS, d_model), F32)],
        input_output_aliases={0: 0},
        compiler_params=_params(1),
    )(dproj, proj, h1, d_h3, gn_g, gn_b)


A2_ROWS = 8


def _branch_a_bwd_conv(dproj, proj, d_h1, conv_w_full, gp_bf, gp_f32, seq):
    n_tok = proj.shape[0]
    d_model = conv_w_full.shape[1]
    n_groups = d_model // LANES
    ta = _tile(seq, 256, HALO)
    n_tiles = n_tok // ta
    per_seq = seq // ta
    rc = _conv_rows(ta)
    last_halo = n_tok // HALO - 1
    r8 = d_model // N_DEV
    prow = _tile(r8, 32, 16)

    def body(dproj_in, av_ref, ag_ref, avh_ref, agh_ref, dh1_ref, dh1h_ref, cw_ref, gp_bf_ref, gp_f32_ref,
             dproj_ref, vec_ref, dcw_ref, opa_ref, opb_ref, opo_ref,
             ext_h0, ext_d, rbuf, own, send_sems, recv_sems, local_sems):
        del dproj_in
        i = pl.program_id(0)
        x, y, c = _mesh_pos()
        me = _block_of((x, y, c))
        peers = _peers()

        def sends():
            return [pltpu.make_async_remote_copy(
                src_ref=gp_bf_ref.at[a, pl.ds(pl.multiple_of(blk * r8, 16), r8), :], dst_ref=rbuf.at[k, a],
                send_sem=send_sems.at[a, k], recv_sem=recv_sems.at[a, k], device_id=pos, device_id_type=MESH)
                for k, (pos, blk) in enumerate(peers) for a in range(3)]

        def own_rows():
            return [pltpu.make_async_copy(gp_f32_ref.at[a, pl.ds(pl.multiple_of(me * r8, 8), r8), :],
                                          own.at[a], local_sems.at[a]) for a in range(3)]

        @pl.when(i == 0)
        def _():
            vec_ref[...] = jnp.zeros_like(vec_ref)
            dcw_ref[...] = jnp.zeros_like(dcw_ref)
            for cp in sends() + own_rows():
                cp.start()

        keep_past = jnp.where(i % per_seq == 0, 0.0, 1.0)
        keep_next = jnp.where(i % per_seq == per_seq - 1, 0.0, 1.0)

        def group(g, carry):
            sl = pl.ds(pl.multiple_of(g * LANES, LANES), LANES)
            av = av_ref[:, sl]
            sig = _sigmoid(ag_ref[:, sl])
            ext_h0[0:HALO, :] = avh_ref[:, sl] * _sigmoid(agh_ref[:, sl]) * keep_past
            ext_h0[HALO:HALO + ta, :] = av * sig
            ext_d[0:ta, :] = dh1_ref[:, sl]
            ext_d[ta:ta + HALO, :] = dh1h_ref[:, sl] * keep_next
            for r0 in range(0, ta, rc):
                dh1 = ext_d[pl.ds(r0, rc), :]
                acc = jnp.zeros((rc, LANES), F32)
                for k in range(CONV_K):
                    acc = acc + ext_d[pl.ds(r0 + CONV_K - 1 - k, rc), :] * cw_ref[k:k + 1, sl]
                    prod = dh1 * ext_h0[pl.ds(r0 + HALO - (CONV_K - 1) + k, rc), :]
                    dcw_ref[g, k] += jnp.sum(prod.reshape(rc // SUBLANES, SUBLANES, LANES), axis=0)
                rows = pl.ds(r0, rc)
                sig_r = sig[r0:r0 + rc]
                dav = acc * sig_r
                dag = acc * av[r0:r0 + rc] * sig_r * (1.0 - sig_r)
                dproj_ref[rows, sl] = dav.astype(BF16)
                dproj_ref[rows, pl.ds(pl.multiple_of(d_model + g * LANES, LANES), LANES)] = dag.astype(BF16)
                vec_ref[0:1, sl] += _colsum(dav)
                vec_ref[1:2, sl] += _colsum(dag)
            return carry

        lax.fori_loop(0, n_groups, group, 0, unroll=True)

        @pl.when(i == n_tiles - 1)
        def _():
            for cp in own_rows():
                cp.wait()
            for cp in sends():
                cp.wait_recv()
            for a, o in enumerate([opa_ref, opb_ref, opo_ref]):
                for q in range(r8 // prow):
                    r = pl.ds(q * prow, prow)
                    tot = own[a, r, :]
                    for k in range(N_DEV - 1):
                        tot = tot + rbuf[k, a, r, :].astype(F32)
                    o[r, :] = tot
            for cp in sends():
                cp.wait_send()

    blk = lambda j: pl.BlockSpec((ta, d_model), lambda i: (i, j))
    halo = lambda j: pl.BlockSpec((HALO, d_model), lambda i: (jnp.maximum(i * (ta // HALO) - 1, 0), j))
    shard = pl.BlockSpec((r8, d_model), lambda i: (0, 0))
    return pl.pallas_call(
        body, name="branch_a_bwd_conv", grid=(n_tiles,),
        in_specs=[ANY, blk(0), blk(1), halo(0), halo(1), pl.BlockSpec((ta, d_model), lambda i: (i, 0)),
                  pl.BlockSpec((HALO, d_model), lambda i: (jnp.minimum((i + 1) * (ta // HALO), last_halo), 0)),
                  pl.BlockSpec((HALO, d_model), lambda i: (0, 0)), ANY, ANY],
        out_specs=[pl.BlockSpec((ta, 2 * d_model), lambda i: (i, 0)),
                   pl.BlockSpec((A2_ROWS, d_model), lambda i: (0, 0)),
                   pl.BlockSpec((n_groups, HALO, SUBLANES, LANES), lambda i: (0, 0, 0, 0)),
                   shard, shard, shard],
        out_shape=[SDS(dproj.shape, BF16), SDS((A2_ROWS, d_model), F32),
                   SDS((n_groups, HALO, SUBLANES, LANES), F32)] + [SDS((r8, d_model), F32)] * 3,
        scratch_shapes=[pltpu.VMEM((HALO + ta, LANES), F32), pltpu.VMEM((ta + HALO, LANES), F32),
                        pltpu.VMEM((N_DEV - 1, 3, r8, d_model), BF16), pltpu.VMEM((3, r8, d_model), F32),
                        pltpu.SemaphoreType.DMA((3, 7)), pltpu.SemaphoreType.DMA((3, 7)),
                        pltpu.SemaphoreType.DMA((3,))],
        input_output_aliases={0: 0},
        compiler_params=_params(1),
    )(dproj, proj, proj, proj, proj, d_h1, d_h1, conv_w_full, gp_bf, gp_f32)


def _weight_grads(lhs3, rhs3):
    n_mat, n_tok, d_model = lhs3.shape
    tk = _tile(n_tok, 1024, 16)
    n_k = n_tok // tk

    def body(a_ref, g_ref, o_ref, ob_ref):
        part = _dot_ta(a_ref[...], g_ref[...])

        @pl.when(pl.program_id(1) == 0)
        def _():
            o_ref[...] = part

        @pl.when(pl.program_id(1) != 0)
        def _():
            o_ref[...] += part

        @pl.when(pl.program_id(1) == n_k - 1)
        def _():
            ob_ref[...] = o_ref[...].astype(BF16)

    tile = pl.BlockSpec((None, tk, d_model), lambda a, i: (a, i, 0))
    out = pl.BlockSpec((None, d_model, d_model), lambda a, i: (a, 0, 0))
    return pl.pallas_call(
        body, name="grad_w_pa_pb_o", grid=(n_mat, n_k), in_specs=[tile, tile], out_specs=[out, out],
        out_shape=[SDS((n_mat, d_model, d_model), F32), SDS((n_mat, d_model, d_model), BF16)],
        compiler_params=_params(2),
    )(lhs3, rhs3)


def _grad_w_in_reduce_scatter(xt_bf, dproj, blk_order, small_part):
    n_tok = dproj.shape[0]
    d_model = xt_bf.shape[0]
    dh = d_model // 2
    n_units = 2 * N_DEV
    rsl = small_part.shape[0] // N_DEV

    def body(ord_ref, a_ref, g_ref, p_ref, o_ref, small_ref, acc, fb, sbuf, gbuf, tbuf, rfin, rbuf_s, red,
             send_f, send_s, recv_g, recv_t, recv_f, out_sems, send1, recv1, send2, recv2):
        del ord_ref
        u = pl.program_id(0)
        s = u // 2
        hf = u % 2
        rnd = s // 2
        x, y, c = _mesh_pos()
        sibling = (x, y, 1 - c)
        me = _block_of((x, y, c))
        peers = _peers()

        def rows_of(blk):
            return pl.ds(pl.multiple_of(blk * rsl, SUBLANES), rsl)

        def scatter():
            return [pltpu.make_async_remote_copy(
                src_ref=p_ref.at[rows_of(blk), :], dst_ref=rbuf_s.at[k], send_sem=send1.at[k],
                recv_sem=recv1.at[k], device_id=pos, device_id_type=MESH) for k, (pos, blk) in enumerate(peers)]

        def gather(dst_block=None):
            return [pltpu.make_async_remote_copy(
                src_ref=red, dst_ref=small_ref.at[rows_of(me if dst_block is None else blk), :],
                send_sem=send2.at[k], recv_sem=recv2.at[k], device_id=pos, device_id_type=MESH)
                for k, (pos, blk) in enumerate(peers)]

        def own_slice():
            return pltpu.make_async_copy(red, small_ref.at[rows_of(me), :], out_sems.at[2])

        @pl.when(u == 0)
        def _():
            for cp in scatter():
                cp.start()

        @pl.when(u == 4)
        def _():
            for cp in scatter():
                cp.wait_recv()
            tot = p_ref[rows_of(me), :]
            for k in range(N_DEV - 1):
                tot = tot + rbuf_s[k]
            red[...] = tot
            own_slice().start()
            for cp in gather():
                cp.start()

        n1 = (jnp.bitwise_xor(x, c), jnp.bitwise_xor(y, 1 - c), c)
        n2 = (jnp.bitwise_xor(x, 1 - c), jnp.bitwise_xor(y, c), c)

        def feed(r, half):
            return pltpu.make_async_remote_copy(
                src_ref=fb.at[half], dst_ref=gbuf.at[r, half], send_sem=send_f.at[r, half],
                recv_sem=recv_g.at[r, half], device_id=sibling, device_id_type=MESH)

        def feed_sibling(half):
            return pltpu.make_async_remote_copy(
                src_ref=fb.at[half], dst_ref=rfin.at[0, half], send_sem=send_f.at[3, half],
                recv_sem=recv_f.at[0, half], device_id=sibling, device_id_type=MESH)

        def chip_sum(r, half):
            dst = [tbuf.at[half], rfin.at[2, half], rfin.at[1, half]][r]
            sem = [recv_t.at[half], recv_f.at[2, half], recv_f.at[1, half]][r]
            return pltpu.make_async_remote_copy(
                src_ref=sbuf.at[r, half], dst_ref=dst, send_sem=send_s.at[r, half], recv_sem=sem,
                device_id=[n2, n2, n1][r], device_id_type=MESH)

        def out_copy(half):
            return pltpu.make_async_copy(acc.at[half], o_ref.at[:, pl.ds(half * dh, dh)], out_sems.at[half])

        def partial_sum():
            return _dot(a_ref[:, 0:n_tok], g_ref[...])

        for half in range(2):
            for r in range(3):
                @pl.when(u == 4 * r + 4 + half)
                def _(r=r, half=half):
                    feed(r, half).wait_send()

                @pl.when(u == 4 * r + 2 + half)
                def _(r=r, half=half):
                    feed(r, half).wait_recv()
                    if r == 2:
                        chip_sum(0, half).wait_recv()

            @pl.when(u == 14 + half)
            def _(half=half):
                feed_sibling(half).wait_recv()
                chip_sum(2, half).wait_recv()
                chip_sum(1, half).wait_recv()

        @pl.when(jnp.logical_and(s % 2 == 0, s < 7))
        def _():
            fb[hf] = partial_sum().astype(BF16)

        @pl.when(jnp.logical_or(s == 1, s == 3))
        def _():
            sbuf[rnd, hf] = (partial_sum() + gbuf[rnd, hf].astype(F32)).astype(BF16)

        @pl.when(s == 5)
        def _():
            sbuf[2, hf] = (partial_sum() + gbuf[2, hf].astype(F32) + tbuf[hf].astype(F32)).astype(BF16)

        @pl.when(s == 7)
        def _():
            acc[hf] = (partial_sum() + rfin[0, hf].astype(F32) + rfin[1, hf].astype(F32)
                       + rfin[2, hf].astype(F32))

        for half in range(2):
            for r in range(3):
                @pl.when(u == 4 * r + half)
                def _(r=r, half=half):
                    feed(r, half).start()

                @pl.when(u == 4 * r + 2 + half)
                def _(r=r, half=half):
                    chip_sum(r, half).start()

            @pl.when(u == 12 + half)
            def _(half=half):
                feed_sibling(half).start()

        @pl.when(u == 14)
        def _():
            out_copy(0).start()

        @pl.when(u == 15)
        def _():
            out_copy(1).start()
            for half in range(2):
                feed_sibling(half).wait_send()
                for r in range(3):
                    chip_sum(r, half).wait_send()
                out_copy(half).wait()
            for cp in gather("theirs"):
                cp.wait_recv()
            for cp in scatter() + gather():
                cp.wait_send()
            own_slice().wait()

    grid_spec = pltpu.PrefetchScalarGridSpec(
        num_scalar_prefetch=1, grid=(n_units,),
        in_specs=[VMEM, pl.BlockSpec((n_tok, dh), lambda u, o: (0, 2 * o[u // 2] + u % 2)), VMEM],
        out_specs=[ANY, ANY],
        scratch_shapes=[pltpu.VMEM((2, d_model, dh), F32), pltpu.VMEM((2, d_model, dh), BF16),
                        pltpu.VMEM((3, 2, d_model, dh), BF16), pltpu.VMEM((3, 2, d_model, dh), BF16),
                        pltpu.VMEM((2, d_model, dh), BF16), pltpu.VMEM((3, 2, d_model, dh), BF16),
                        pltpu.VMEM((N_DEV - 1, rsl, LANES), F32), pltpu.VMEM((rsl, LANES), F32),
                        pltpu.SemaphoreType.DMA((4, 2)), pltpu.SemaphoreType.DMA((3, 2)),
                        pltpu.SemaphoreType.DMA((3, 2)), pltpu.SemaphoreType.DMA((2,)),
                        pltpu.SemaphoreType.DMA((3, 2)),
                        pltpu.SemaphoreType.DMA((3,))] + [pltpu.SemaphoreType.DMA((N_DEV - 1,))] * 4)
    return pl.pallas_call(
        body, name="grad_w_in_reduce_scatter", grid_spec=grid_spec,
        out_shape=[SDS((d_model, d_model), F32), SDS(small_part.shape, F32)], compiler_params=_params(1),
    )(blk_order, xt_bf, dproj, small_part)


def _grad_x(dproj, w_all, dr):
    n_tok, d_model = dr.shape
    tm = _tile(n_tok, 512, 16)

    def body(dp_ref, w_ref, dr_ref, o_ref):
        acc = _dot_tb(dp_ref[:, 0:d_model], w_ref[0])
        for j in range(1, N_DEV):
            acc = acc + _dot_tb(dp_ref[:, j * d_model:(j + 1) * d_model], w_ref[j])
        o_ref[...] = acc + DEEPNORM_ALPHA * dr_ref[...]

    return pl.pallas_call(
        body, name="grad_x", grid=(n_tok // tm,),
        in_specs=[pl.BlockSpec((tm, N_DEV * d_model), lambda i: (i, 0)), VMEM,
                  pl.BlockSpec((tm, d_model), lambda i: (i, 0))],
        out_specs=pl.BlockSpec((tm, d_model), lambda i: (i, 0)),
        out_shape=SDS((n_tok, d_model), F32),
        compiler_params=_params(1),
    )(dproj, w_all, dr)


def _adamw_math(w, g, m, v):
    m = ADAM_B1 * m + (1.0 - ADAM_B1) * g
    v = ADAM_B2 * v + (1.0 - ADAM_B2) * (g * g)
    m_hat = m / (1.0 - ADAM_B1 ** ADAM_STEP)
    v_hat = v / (1.0 - ADAM_B2 ** ADAM_STEP)
    delta = -ADAM_LR * (m_hat / (jnp.sqrt(v_hat) + ADAM_EPS) + ADAM_WD * w)
    return delta, m, v


def _adamw_tiled(w, g, m, v):
    n_rows, n_cols = w.shape
    tr = _tile(n_rows, 256, SUBLANES)

    def body(w_ref, g_ref, m_ref, v_ref, d_ref, mo_ref, vo_ref):
        d_ref[...], mo_ref[...], vo_ref[...] = _adamw_math(w_ref[...], g_ref[...], m_ref[...], v_ref[...])

    tile = pl.BlockSpec((tr, n_cols), lambda i: (i, 0))
    return pl.pallas_call(
        body, name="adamw_w_in", grid=(n_rows // tr,), in_specs=[tile] * 4, out_specs=[tile] * 3,
        out_shape=[SDS(w.shape, F32)] * 3, compiler_params=_params(1),
    )(w, g, m, v)


def _adamw_many(groups):
    n = len(groups)

    def body(*refs):
        ins, outs = refs[:4 * n], refs[4 * n:]
        for p in range(n):
            w_ref, g_ref, m_ref, v_ref = ins[4 * p:4 * p + 4]
            d, m, v = _adamw_math(w_ref[...], g_ref[...], m_ref[...], v_ref[...])
            outs[3 * p][...] = d
            outs[3 * p + 1][...] = m
            outs[3 * p + 2][...] = v

    flat = [a for grp in groups for a in grp]
    out_shape = [SDS(grp[0].shape, F32) for grp in groups for _ in range(3)]
    res = pl.pallas_call(
        body, name="adamw_small", in_specs=[VMEM] * (4 * n), out_specs=[VMEM] * (3 * n),
        out_shape=out_shape, compiler_params=_params(),
    )(*flat)
    return [tuple(res[3 * p:3 * p + 3]) for p in range(n)]


def _as_rows(a):
    return a.reshape(-1, LANES)


def kernel(x, w_in, b_in, conv_w, conv_b, gn_g, gn_b, ln_v_g, ln_v_b, w_spatial, b_spatial, w_pa, w_pb, w_o, b_o, ln_out_g, ln_out_b, loss_target, m_w_in, m_b_in, m_conv_w, m_conv_b, m_gn_g, m_gn_b, m_ln_v_g, m_ln_v_b, m_w_spatial, m_b_spatial, m_w_pa, m_w_pb, m_w_o, m_b_o, m_ln_out_g, m_ln_out_b, v_w_in, v_b_in, v_conv_w, v_conv_b, v_gn_g, v_gn_b, v_ln_v_g, v_ln_v_b, v_w_spatial, v_b_spatial, v_w_pa, v_w_pb, v_w_o, v_b_o, v_ln_out_g, v_ln_out_b):
    n_batch, seq, d_model = x.shape
    n_tok = n_batch * seq
    n_heads = d_model // LANES
    dc = conv_w.shape[1]
    me = 4 * lax.axis_index("x") + 2 * lax.axis_index("y") + lax.axis_index("c")
    row = lambda a: a.reshape(1, d_model)

    x2 = x.reshape(n_tok, d_model)
    target2 = loss_target.reshape(n_tok, d_model)
    b_spatial_t = b_spatial.T

    first = jnp.where(lax.axis_index("c") == 1, 4, 2)
    second = 6 - first
    ag_rel = jnp.stack([0 * first, 0 * first + 1, first, second + 1, second, first + 1, 0 * first + 6, 0 * first + 7])
    ag_blocks = jnp.bitwise_xor(me, ag_rel).astype(jnp.int32)
    proj, xt_bf, w_all, wp_all, cw_all = _proj_all_gather(
        x2, w_in, w_pa, w_pb, w_o, conv_w, b_in.reshape(N_DEV, 1, d_model), ag_blocks)
    wp_full = [wp_all[:, a].reshape(d_model, d_model) for a in range(3)]
    conv_w_full = jnp.pad(cw_all.transpose(1, 0, 2).reshape(CONV_K, d_model), ((0, HALO - CONV_K), (0, 0)))

    h3, h1 = _branch_a_fwd(proj, conv_w_full, row(conv_b), row(gn_g), row(gn_b), seq)
    s = _branch_b_fwd(proj, row(ln_v_g), row(ln_v_b), w_spatial, b_spatial_t, seq)

    dproj, d_h3, d_s, dr, lhs3, rhs3, vec_mid = _mid(
        h3, s, proj, x2, target2, *wp_full, row(b_o), row(ln_out_g), row(ln_out_b))

    dproj, vec_b, d_ws, d_bs_t = _branch_b_bwd(dproj, proj, d_s, row(ln_v_g), row(ln_v_b), w_spatial, b_spatial_t, seq)
    dproj, d_h1, vec_a1 = _branch_a_bwd_norm(dproj, proj, h1, d_h3, row(gn_g), row(gn_b), seq)
    gp_f32, gp_bf = _weight_grads(lhs3, rhs3)
    dproj, vec_a2, d_cw8, g_w_pa, g_w_pb, g_w_o = _branch_a_bwd_conv(
        dproj, proj, d_h1, conv_w_full, gp_bf, gp_f32, seq)

    d_cw = jnp.sum(d_cw8, axis=2)
    pieces = [
        _as_rows(jnp.concatenate([vec_a2[0:2], vec_a1[0:1], vec_b[0:3], vec_mid[3:5]], axis=0)),
        _as_rows(jnp.concatenate([vec_a1[3:4], vec_a1[1:3], vec_b[3:5], vec_mid[2:3], vec_mid[0:2]], axis=0)),
        _as_rows(d_bs_t.T), _as_rows(d_ws), _as_rows(d_cw), _as_rows(vec_mid[5:6]),
    ]
    n_rows = sum(p.shape[0] for p in pieces)
    pad_rows = -n_rows % (N_DEV * SUBLANES)
    small_part = jnp.concatenate(pieces + [jnp.zeros((pad_rows, LANES), F32)], axis=0)

    rs_rel = jnp.stack([0 * first + 7, 0 * first + 6, first + 1, second, second + 1, first, 0 * first + 1, 0 * first])
    rs_blocks = jnp.bitwise_xor(me, rs_rel).astype(jnp.int32)
    g_w_in, small = _grad_w_in_reduce_scatter(xt_bf, dproj, rs_blocks, small_part)
    grad_x = _grad_x(dproj, w_all, dr).reshape(x.shape)

    g_rows = d_model // LANES
    o0 = N_DEV * g_rows
    g_b_in = small[0:o0].reshape(N_DEV * d_model)
    vecs = [small[o0 + a * g_rows:o0 + (a + 1) * g_rows].reshape(d_model) for a in range(8)]
    g_conv_b, g_gn_g, g_gn_b, g_ln_v_g, g_ln_v_b, g_b_o, g_ln_out_g, g_ln_out_b = vecs
    o1 = o0 + 8 * g_rows
    g_b_spatial = small[o1:o1 + n_heads].reshape(n_heads, LANES)
    o2 = o1 + n_heads
    g_w_spatial = small[o2:o2 + n_heads * LANES].reshape(n_heads, LANES, LANES)
    o3 = o2 + n_heads * LANES
    g_cw_full = small[o3:o3 + n_heads * HALO].reshape(n_heads, HALO, LANES).transpose(1, 0, 2).reshape(HALO, d_model)
    g_conv_w = lax.dynamic_slice(g_cw_full, (0, me * dc), (CONV_K, dc))
    o4 = o3 + n_heads * HALO
    loss = jnp.sum(small[o4:o4 + g_rows]) * (0.5 / d_model)

    d_w_in, nm_w_in, nv_w_in = _adamw_tiled(w_in, g_w_in, m_w_in, v_w_in)
    two_d = lambda a: a.reshape(-1, a.shape[-1]) if a.ndim != 1 else (
        a.reshape(-1, LANES) if a.shape[0] % LANES == 0 else a.reshape(1, -1))
    names = ["b_in", "conv_w", "conv_b", "gn_g", "gn_b", "ln_v_g", "ln_v_b", "w_spatial", "b_spatial",
             "w_pa", "w_pb", "w_o", "b_o", "ln_out_g", "ln_out_b"]
    ws = dict(b_in=b_in, conv_w=conv_w, conv_b=conv_b, gn_g=gn_g, gn_b=gn_b, ln_v_g=ln_v_g, ln_v_b=ln_v_b,
              w_spatial=w_spatial, b_spatial=b_spatial, w_pa=w_pa, w_pb=w_pb, w_o=w_o, b_o=b_o,
              ln_out_g=ln_out_g, ln_out_b=ln_out_b)
    gs = dict(b_in=g_b_in, conv_w=g_conv_w, conv_b=g_conv_b, gn_g=g_gn_g, gn_b=g_gn_b, ln_v_g=g_ln_v_g,
              ln_v_b=g_ln_v_b, w_spatial=g_w_spatial, b_spatial=g_b_spatial, w_pa=g_w_pa, w_pb=g_w_pb,
              w_o=g_w_o, b_o=g_b_o, ln_out_g=g_ln_out_g, ln_out_b=g_ln_out_b)
    ms = dict(b_in=m_b_in, conv_w=m_conv_w, conv_b=m_conv_b, gn_g=m_gn_g, gn_b=m_gn_b, ln_v_g=m_ln_v_g,
              ln_v_b=m_ln_v_b, w_spatial=m_w_spatial, b_spatial=m_b_spatial, w_pa=m_w_pa, w_pb=m_w_pb,
              w_o=m_w_o, b_o=m_b_o, ln_out_g=m_ln_out_g, ln_out_b=m_ln_out_b)
    vs = dict(b_in=v_b_in, conv_w=v_conv_w, conv_b=v_conv_b, gn_g=v_gn_g, gn_b=v_gn_b, ln_v_g=v_ln_v_g,
              ln_v_b=v_ln_v_b, w_spatial=v_w_spatial, b_spatial=v_b_spatial, w_pa=v_w_pa, w_pb=v_w_pb,
              w_o=v_w_o, b_o=v_b_o, ln_out_g=v_ln_out_g, ln_out_b=v_ln_out_b)
    upd = _adamw_many([tuple(two_d(d[n]) for d in (ws, gs, ms, vs)) for n in names])
    delta = {n: u[0].reshape(ws[n].shape) for n, u in zip(names, upd)}
    new_m = {n: u[1].reshape(ws[n].shape) for n, u in zip(names, upd)}
    new_v = {n: u[2].reshape(ws[n].shape) for n, u in zip(names, upd)}
    gs["w_in"], delta["w_in"], new_m["w_in"], new_v["w_in"] = g_w_in, d_w_in, nm_w_in, nv_w_in

    order = ["w_in"] + names
    return (loss, grad_x, *[gs[n] for n in order], *[delta[n] for n in order],
            *[new_m[n] for n in order], *[new_v[n] for n in order])
```

```python
import jax
import jax.numpy as jnp
from jax import lax
from jax.experimental import pallas as pl
from jax.experimental.pallas import tpu as pltpu

F32 = jnp.float32
BF16 = jnp.bfloat16
SDS = jax.ShapeDtypeStruct

N_DEV = 8
LANES = 128
SUBLANES = 8
CONV_K = 31
HALO = 32
LN_EPS = 1e-5
DEEPNORM_ALPHA = 2.0 ** 0.25
ADAM_LR, ADAM_B1, ADAM_B2, ADAM_EPS, ADAM_WD, ADAM_STEP = 0.001, 0.9, 0.999, 1e-08, 0.01, 10
GELU_C = 0.7978845608028654
GELU_A = 0.044715
VMEM_LIMIT = 56 * 1024 * 1024
MESH = pl.DeviceIdType.MESH
ANY = pl.BlockSpec(memory_space=pl.ANY)
VMEM = pl.BlockSpec(memory_space=pltpu.VMEM)


def _params(n_grid=0):
    sem = ("arbitrary",) * n_grid if n_grid else None
    return pltpu.CompilerParams(dimension_semantics=sem, vmem_limit_bytes=VMEM_LIMIT)


def _tile(n, pref, mult):
    t = min(n, pref)
    while n % t or t % mult:
        t -= 1
    return t


def _colsum(v):
    return jnp.sum(v, axis=0, keepdims=True)


def _sigmoid(v):
    return jax.nn.sigmoid(v)


def _silu_and_grad(v):
    s = _sigmoid(v)
    return v * s, s * (1.0 + v * (1.0 - s))


def _gelu_and_grad(v):
    inner = GELU_C * (v + GELU_A * v * v * v)
    th = jnp.tanh(inner)
    val = 0.5 * v * (1.0 + th)
    grad = 0.5 * (1.0 + th) + 0.5 * v * (1.0 - th * th) * GELU_C * (1.0 + 3.0 * GELU_A * v * v)
    return val, grad


def _tril_mask():
    r = lax.broadcasted_iota(jnp.int32, (LANES, LANES), 0)
    c = lax.broadcasted_iota(jnp.int32, (LANES, LANES), 1)
    return c <= r


def _dot(a, b):
    return jnp.dot(a, b, preferred_element_type=F32)


def _dot_tb(a, b):
    return lax.dot_general(a, b, (((1,), (1,)), ((), ())), preferred_element_type=F32)


def _dot_ta(a, b):
    return lax.dot_general(a, b, (((0,), (0,)), ((), ())), preferred_element_type=F32)


def _mesh_pos():
    return lax.axis_index("x"), lax.axis_index("y"), lax.axis_index("c")


def _block_of(pos):
    return 4 * pos[0] + 2 * pos[1] + pos[2]


def _peers():
    x, y, c = _mesh_pos()
    out = []
    for k in range(1, N_DEV):
        pos = (1 - x if k & 4 else x, 1 - y if k & 2 else y, 1 - c if k & 1 else c)
        out.append((pos, _block_of(pos)))
    return out


def _proj_all_gather(x2, w_in, conv_w, b_in3, blk_order):
    n_tok, d_model = x2.shape
    kc, dc = conv_w.shape
    tm = _tile(n_tok, 1024, LANES)
    n_t = n_tok // tm
    n_arr = 2

    def body(ord_ref, x_ref, b_ref, win_ref, cw_ref, proj_ref, xt_ref, wall_ref, cwall_ref,
             wbuf, send_sems, recv_sems, local_sem, wall_sems):
        s = pl.program_id(0)
        t = pl.program_id(1)
        x, y, c = _mesh_pos()
        me = (x, y, c)
        sibling = (x, y, 1 - c)
        n1 = (jnp.bitwise_xor(x, c), jnp.bitwise_xor(y, 1 - c))
        n2 = (jnp.bitwise_xor(x, 1 - c), jnp.bitwise_xor(y, c))
        dg = (1 - x, 1 - y)
        outs = [wbuf, cwall_ref]
        srcs = [None, cw_ref]
        consumed = [me, sibling, (*n1, c), (*n2, 1 - c), (*n2, c), (*n1, 1 - c), (*dg, c), (*dg, 1 - c)]
        leaves = [(me, sibling), (me, (*n1, c)), (me, (*n2, c)), ((*n1, c), (*n2, c)),
                  ((*n1, c), sibling), ((*n2, c), sibling), ((*dg, c), sibling)]
        lands = [sibling, (*n1, c), (*n2, c), (*dg, c), (*n2, 1 - c), (*n1, 1 - c), (*dg, 1 - c)]

        def slot(o, pos):
            return o.at[_block_of(pos)]

        def copy(a, k, block, to, src=None):
            o = outs[a]
            return pltpu.make_async_remote_copy(
                src_ref=slot(o, block) if src is None else src, dst_ref=slot(o, block),
                send_sem=send_sems.at[a, k], recv_sem=recv_sems.at[a, k],
                device_id=to, device_id_type=MESH)

        def send(a, k):
            block, to = leaves[k]
            return copy(a, k, block, to, src=srcs[a] if k < 3 else None)

        def recv(a, k):
            return copy(a, k, lands[k], me)

        def local_copies():
            return [pltpu.make_async_copy(cw_ref, slot(cwall_ref, me), local_sem)]

        def to_hbm(step):
            return pltpu.make_async_copy(slot(wbuf, consumed[step]), slot(wall_ref, consumed[step]),
                                         wall_sems.at[step])

        def at_step(step):
            return pl.when(jnp.logical_and(s == step, t == 0))

        @at_step(0)
        def _():
            slot(wbuf, me)[...] = win_ref[...].astype(BF16)
            for a in range(n_arr):
                send(a, 0).start()
                send(a, 1).start()
            for cp in local_copies():
                cp.start()
            to_hbm(0).start()

        @at_step(1)
        def _():
            recv(0, 0).wait_recv()
            to_hbm(1).start()

        for rnd in range(3):
            @at_step(2 + 2 * rnd)
            def _(rnd=rnd):
                if rnd == 0:
                    for a in range(n_arr):
                        send(a, 2).start()
                recv(0, 1 + rnd).wait_recv()
                if rnd == 0:
                    send(0, 3).start()
                send(0, 4 + rnd).start()
                to_hbm(2 + 2 * rnd).start()

            @at_step(3 + 2 * rnd)
            def _(rnd=rnd):
                recv(1, 1 + rnd).wait_recv()
                if rnd == 0:
                    send(1, 3).start()
                send(1, 4 + rnd).start()
                recv(0, 4 + rnd).wait_recv()
                to_hbm(3 + 2 * rnd).start()

        xb = x_ref[...].astype(BF16)

        @pl.when(s == 0)
        def _():
            xt_ref[...] = xb.T

        proj_ref[...] = _dot(xb, wbuf[ord_ref[s]]) + b_ref[...]

        @pl.when(jnp.logical_and(s == N_DEV - 1, t == n_t - 1))
        def _():
            for k in (0, 4, 5, 6):
                recv(1, k).wait_recv()
            for a in range(n_arr):
                for k in range(7):
                    send(a, k).wait_send()
            for cp in local_copies() + [to_hbm(step) for step in range(N_DEV)]:
                cp.wait()

    grid_spec = pltpu.PrefetchScalarGridSpec(
        num_scalar_prefetch=1, grid=(N_DEV, n_t),
        in_specs=[pl.BlockSpec((tm, d_model), lambda s, t, o: (t, 0)),
                  pl.BlockSpec((None, 1, d_model), lambda s, t, o: (o[s], 0, 0)),
                  VMEM, VMEM],
        out_specs=[pl.BlockSpec((tm, d_model), lambda s, t, o: (t, o[s])),
                   pl.BlockSpec((d_model, tm), lambda s, t, o: (0, jnp.where(s == 0, t, n_t))),
                   ANY, ANY],
        scratch_shapes=[pltpu.VMEM((N_DEV, d_model, d_model), BF16),
                        pltpu.SemaphoreType.DMA((n_arr, 7)), pltpu.SemaphoreType.DMA((n_arr, 7)),
                        pltpu.SemaphoreType.DMA(()), pltpu.SemaphoreType.DMA((N_DEV,))])
    return pl.pallas_call(
        body, name="proj_all_gather", grid_spec=grid_spec,
        out_shape=[SDS((n_tok, N_DEV * d_model), F32), SDS((d_model, (n_t + 1) * tm), BF16),
                   SDS((N_DEV, d_model, d_model), BF16), SDS((N_DEV, kc, dc), F32)],
        compiler_params=_params(2),
    )(blk_order, x2, b_in3, w_in, conv_w)


def _conv_rows(ta):
    return _tile(ta, 64, SUBLANES)


def _branch_a_fwd(proj, conv_w_full, conv_b, gn_g, gn_b, w_pa, w_pb, w_o, seq):
    n_tok = proj.shape[0]
    d_model = conv_b.shape[1]
    ta = _tile(seq, 256, HALO)
    n_tiles = n_tok // ta
    per_seq = seq // ta
    rc = _conv_rows(ta)
    r8 = w_pa.shape[0]

    def body(av_ref, ag_ref, gt_ref, avh_ref, agh_ref, cw_ref, cb_ref, gg_ref, gb_ref,
             wpa_ref, wpb_ref, wo_ref, h3_ref, h1_ref, wp_ref, ext, st_p, send_sems, recv_sems, local_sem):
        i = pl.program_id(0)
        keep = jnp.where(i % per_seq == 0, 0.0, 1.0)
        x, y, c = _mesh_pos()
        me = (x, y, c)
        sibling = (x, y, 1 - c)
        chips = [(1 - x, y), (x, 1 - y), (1 - x, 1 - y)]

        def copy(k, block, to, src=None):
            blk = wp_ref.at[_block_of(block)]
            return pltpu.make_async_remote_copy(
                src_ref=blk if src is None else src, dst_ref=blk, send_sem=send_sems.at[k],
                recv_sem=recv_sems.at[k], device_id=to, device_id_type=MESH)

        def first_copies():
            return [copy(0, me, sibling, src=st_p)] + [copy(1 + j, me, (*chip, c), src=st_p)
                                                        for j, chip in enumerate(chips)]

        def passed_copies():
            return [copy(4 + j, (*chip, c), sibling) for j, chip in enumerate(chips)]

        def own_copy():
            return pltpu.make_async_copy(st_p, wp_ref.at[_block_of(me)], local_sem)

        @pl.when(i == 0)
        def _():
            st_p[0] = wpa_ref[...].astype(BF16)
            st_p[1] = wpb_ref[...].astype(BF16)
            st_p[2] = wo_ref[...].astype(BF16)
            own_copy().start()
            for cp in first_copies():
                cp.start()

        @pl.when(i == n_tiles // 2)
        def _():
            for j, chip in enumerate(chips):
                copy(1 + j, (*chip, c), me).wait_recv()
            for cp in passed_copies():
                cp.start()

        @pl.when(i == n_tiles - 1)
        def _():
            copy(0, sibling, me).wait_recv()
            for j, chip in enumerate(chips):
                copy(4 + j, (*chip, 1 - c), me).wait_recv()
            for cp in first_copies() + passed_copies():
                cp.wait_send()
            own_copy().wait()

        def group(g, carry):
            sl = pl.ds(pl.multiple_of(g * LANES, LANES), LANES)
            ext[0:HALO, :] = avh_ref[:, sl] * _sigmoid(agh_ref[:, sl]) * keep
            ext[HALO:HALO + ta, :] = av_ref[:, sl] * _sigmoid(ag_ref[:, sl])
            for r0 in range(0, ta, rc):
                acc = jnp.broadcast_to(cb_ref[:, sl], (rc, LANES))
                for k in range(CONV_K):
                    acc = acc + ext[pl.ds(r0 + HALO - (CONV_K - 1) + k, rc), :] * cw_ref[k:k + 1, sl]
                h1_ref[pl.ds(r0, rc), sl] = acc
            h1 = h1_ref[:, sl]
            mu = jnp.mean(h1, axis=-1, keepdims=True)
            dlt = h1 - mu
            var = jnp.mean(dlt * dlt, axis=-1, keepdims=True)
            h2 = dlt * lax.rsqrt(var + LN_EPS) * gg_ref[:, sl] + gb_ref[:, sl]
            gate = gt_ref[:, sl]
            h3_ref[:, sl] = (h2 * _sigmoid(h2) * gate * _sigmoid(gate)).astype(BF16)
            return carry

        lax.fori_loop(0, d_model // LANES, group, 0, unroll=True)

    blk = lambda j: pl.BlockSpec((ta, d_model), lambda i: (i, j))
    halo = lambda j: pl.BlockSpec((HALO, d_model), lambda i: (jnp.maximum(i * (ta // HALO) - 1, 0), j))
    row = pl.BlockSpec((1, d_model), lambda i: (0, 0))
    return pl.pallas_call(
        body, name="branch_a_fwd", grid=(n_tiles,),
        in_specs=[blk(0), blk(1), blk(2), halo(0), halo(1),
                  pl.BlockSpec((HALO, d_model), lambda i: (0, 0)), row, row, row, VMEM, VMEM, VMEM],
        out_specs=[pl.BlockSpec((ta, d_model), lambda i: (i, 0))] * 2 + [ANY],
        out_shape=[SDS((n_tok, d_model), BF16), SDS((n_tok, d_model), F32),
                   SDS((N_DEV, 3, r8, d_model), BF16)],
        scratch_shapes=[pltpu.VMEM((HALO + ta, LANES), F32), pltpu.VMEM((3, r8, d_model), BF16),
                        pltpu.SemaphoreType.DMA((7,)), pltpu.SemaphoreType.DMA((7,)),
                        pltpu.SemaphoreType.DMA(())],
        compiler_params=_params(1),
    )(proj, proj, proj, proj, proj, conv_w_full, conv_b, gn_g, gn_b, w_pa, w_pb, w_o)


def _branch_b_fwd(proj, ln_g, ln_b, w_spatial, b_spatial_t, seq):
    n_tok = proj.shape[0]
    d_model = ln_g.shape[1]
    n_heads = d_model // LANES
    tb = _tile(seq, 256, LANES)

    def body(u_ref, v_ref, bg_ref, lg_ref, lb_ref, ws_ref, bs_ref, s_ref, vn_buf):
        v, _ = _gelu_and_grad(v_ref[...])
        mu = jnp.mean(v, axis=-1, keepdims=True)
        dlt = v - mu
        var = jnp.mean(dlt * dlt, axis=-1, keepdims=True)
        vn_buf[...] = (dlt * lax.rsqrt(var + LN_EPS) * lg_ref[...] + lb_ref[...]).astype(BF16)
        tril = _tril_mask()
        for h in range(n_heads):
            cols = slice(h * LANES, (h + 1) * LANES)
            w_h = jnp.where(tril, ws_ref[h], 0.0).astype(BF16)
            bias = bs_ref[:, h:h + 1]
            for ch in range(tb // LANES):
                rows = slice(ch * LANES, (ch + 1) * LANES)
                mix = _dot(w_h, vn_buf[rows, cols]) + bias
                u, _ = _gelu_and_grad(u_ref[rows, cols])
                gate = bg_ref[rows, cols]
                s_ref[rows, cols] = (u * mix * gate * _sigmoid(gate)).astype(BF16)

    blk = lambda j: pl.BlockSpec((tb, d_model), lambda i: (i, j))
    row = pl.BlockSpec((1, d_model), lambda i: (0, 0))
    return pl.pallas_call(
        body, name="branch_b_fwd", grid=(n_tok // tb,),
        in_specs=[blk(3), blk(4), blk(5), row, row,
                  pl.BlockSpec((n_heads, LANES, LANES), lambda i: (0, 0, 0)),
                  pl.BlockSpec((LANES, n_heads), lambda i: (0, 0))],
        out_specs=pl.BlockSpec((tb, d_model), lambda i: (i, 0)),
        out_shape=SDS((n_tok, d_model), BF16),
        scratch_shapes=[pltpu.VMEM((tb, d_model), BF16)],
        compiler_params=_params(1),
    )(proj, proj, proj, ln_g, ln_b, w_spatial, b_spatial_t)


MID_ROWS = 8


def _mid(h3, s, proj, x2, target, w_pa, w_pb, w_o, b_o, lo_g, lo_b):
    n_tok, d_model = x2.shape
    tm = _tile(n_tok, 256, 16)

    def body(h3_ref, s_ref, ma_ref, mb_ref, x_ref, t_ref, wpa_ref, wpb_ref, wo_ref, bo_ref,
             lg_ref, lb_ref, dproj_ref, dh3_ref, ds_ref, dr_ref, lhs3_ref, rhs3_ref, vec_ref):
        @pl.when(pl.program_id(0) == 0)
        def _():
            vec_ref[...] = jnp.zeros_like(vec_ref)

        h3 = h3_ref[...]
        s = s_ref[...]
        ya = _dot(h3, wpa_ref[...])
        yb = _dot(s, wpb_ref[...])
        ga = _sigmoid(ma_ref[...])
        gb = _sigmoid(mb_ref[...])
        mixed = (ga * ya + gb * yb).astype(BF16)
        lhs3_ref[0] = h3
        lhs3_ref[1] = s
        lhs3_ref[2] = mixed
        r = DEEPNORM_ALPHA * x_ref[...] + _dot(mixed, wo_ref[...]) + bo_ref[...]
        mu = jnp.mean(r, axis=-1, keepdims=True)
        dlt = r - mu
        rstd = lax.rsqrt(jnp.mean(dlt * dlt, axis=-1, keepdims=True) + LN_EPS)
        rhat = dlt * rstd
        diff = rhat * lg_ref[...] + lb_ref[...] - t_ref[...]
        dy = diff * (1.0 / d_model)
        vec_ref[0:1, :] += _colsum(dy * rhat)
        vec_ref[1:2, :] += _colsum(dy)
        vec_ref[5:6, :] += _colsum(diff * diff)
        drh = dy * lg_ref[...]
        dr = rstd * (drh - jnp.mean(drh, axis=-1, keepdims=True)
                     - rhat * jnp.mean(drh * rhat, axis=-1, keepdims=True))
        vec_ref[2:3, :] += _colsum(dr)
        dr_ref[...] = dr
        drb = dr.astype(BF16)
        rhs3_ref[2] = drb
        dmixed = _dot_tb(drb, wo_ref[...])
        dma = dmixed * ya * ga * (1.0 - ga)
        dmb = dmixed * yb * gb * (1.0 - gb)
        vec_ref[3:4, :] += _colsum(dma)
        vec_ref[4:5, :] += _colsum(dmb)
        dproj_ref[:, 0:d_model] = dma.astype(BF16)
        dproj_ref[:, d_model:2 * d_model] = dmb.astype(BF16)
        dya = (dmixed * ga).astype(BF16)
        dyb = (dmixed * gb).astype(BF16)
        rhs3_ref[0] = dya
        rhs3_ref[1] = dyb
        dh3_ref[...] = _dot_tb(dya, wpa_ref[...])
        ds_ref[...] = _dot_tb(dyb, wpb_ref[...])

    tile = pl.BlockSpec((tm, d_model), lambda i: (i, 0))
    full = pl.BlockSpec((d_model, d_model), lambda i: (0, 0))
    row = pl.BlockSpec((1, d_model), lambda i: (0, 0))
    stack = pl.BlockSpec((3, tm, d_model), lambda i: (0, i, 0))
    bf3 = SDS((3, n_tok, d_model), BF16)
    f32 = SDS((n_tok, d_model), F32)
    return pl.pallas_call(
        body, name="mid", grid=(n_tok // tm,),
        in_specs=[tile, tile, pl.BlockSpec((tm, d_model), lambda i: (i, 6)),
                  pl.BlockSpec((tm, d_model), lambda i: (i, 7)), tile, tile, full, full, full,
                  row, row, row],
        out_specs=[pl.BlockSpec((tm, 2 * d_model), lambda i: (i, 3)), tile, tile, tile, stack, stack,
                   pl.BlockSpec((MID_ROWS, d_model), lambda i: (0, 0))],
        out_shape=[SDS((n_tok, N_DEV * d_model), BF16), f32, f32, f32, bf3, bf3,
                   SDS((MID_ROWS, d_model), F32)],
        compiler_params=_params(1),
    )(h3, s, proj, proj, x2, target, w_pa, w_pb, w_o, b_o, lo_g, lo_b)


B_ROWS = 8


def _branch_b_bwd(dproj, proj, d_s, ln_g, ln_b, w_spatial, b_spatial_t, seq):
    n_tok = proj.shape[0]
    d_model = ln_g.shape[1]
    n_heads = d_model // LANES
    tb = _tile(seq, 256, LANES)

    def body(dproj_in, u_ref, v_ref, bg_ref, ds_ref, lg_ref, lb_ref, ws_ref, bs_ref,
             dproj_ref, vec_ref, dws_ref, dbs_ref, vn_buf, dv_buf):
        del dproj_in

        @pl.when(pl.program_id(0) == 0)
        def _():
            vec_ref[...] = jnp.zeros_like(vec_ref)
            dws_ref[...] = jnp.zeros_like(dws_ref)
            dbs_ref[...] = jnp.zeros_like(dbs_ref)

        v, dgelu_v = _gelu_and_grad(v_ref[...])
        mu = jnp.mean(v, axis=-1, keepdims=True)
        dlt = v - mu
        rstd = lax.rsqrt(jnp.mean(dlt * dlt, axis=-1, keepdims=True) + LN_EPS)
        vhat = dlt * rstd
        vn_buf[...] = (vhat * lg_ref[...] + lb_ref[...]).astype(BF16)
        tril = _tril_mask()
        for h in range(n_heads):
            cols = slice(h * LANES, (h + 1) * LANES)
            w_h = jnp.where(tril, ws_ref[h], 0.0).astype(BF16)
            bias = bs_ref[:, h:h + 1]
            for ch in range(tb // LANES):
                rows = slice(ch * LANES, (ch + 1) * LANES)
                vn = vn_buf[rows, cols]
                mix = _dot(w_h, vn) + bias
                u, dgelu_u = _gelu_and_grad(u_ref[rows, cols])
                sg, dsilu = _silu_and_grad(bg_ref[rows, cols])
                dsv = ds_ref[rows, cols]
                du = dsv * mix * sg * dgelu_u
                dbg = dsv * u * mix * dsilu
                dmix = dsv * u * sg
                dmix_bf = dmix.astype(BF16)
                dproj_ref[rows, cols] = du.astype(BF16)
                dproj_ref[rows, 2 * d_model + h * LANES:2 * d_model + (h + 1) * LANES] = dbg.astype(BF16)
                vec_ref[0:1, cols] += _colsum(du)
                vec_ref[2:3, cols] += _colsum(dbg)
                dbs_ref[:, h:h + 1] += jnp.sum(dmix, axis=1, keepdims=True)
                dws_ref[h] += jnp.where(tril, _dot_tb(dmix_bf, vn), 0.0)
                dv_buf[rows, cols] = _dot_ta(w_h, dmix_bf)
        dvn = dv_buf[...]
        vec_ref[3:4, :] += _colsum(dvn * vhat)
        vec_ref[4:5, :] += _colsum(dvn)
        dvh = dvn * lg_ref[...]
        dv = rstd * (dvh - jnp.mean(dvh, axis=-1, keepdims=True)
                     - vhat * jnp.mean(dvh * vhat, axis=-1, keepdims=True)) * dgelu_v
        vec_ref[1:2, :] += _colsum(dv)
        dproj_ref[:, d_model:2 * d_model] = dv.astype(BF16)

    blk = lambda j: pl.BlockSpec((tb, d_model), lambda i: (i, j))
    row = pl.BlockSpec((1, d_model), lambda i: (0, 0))
    return pl.pallas_call(
        body, name="branch_b_bwd", grid=(n_tok // tb,),
        in_specs=[ANY, blk(3), blk(4), blk(5), pl.BlockSpec((tb, d_model), lambda i: (i, 0)), row, row,
                  pl.BlockSpec((n_heads, LANES, LANES), lambda i: (0, 0, 0)),
                  pl.BlockSpec((LANES, n_heads), lambda i: (0, 0))],
        out_specs=[pl.BlockSpec((tb, 3 * d_model), lambda i: (i, 1)),
                   pl.BlockSpec((B_ROWS, d_model), lambda i: (0, 0)),
                   pl.BlockSpec((n_heads, LANES, LANES), lambda i: (0, 0, 0)),
                   pl.BlockSpec((LANES, n_heads), lambda i: (0, 0))],
        out_shape=[SDS(dproj.shape, BF16), SDS((B_ROWS, d_model), F32),
                   SDS((n_heads, LANES, LANES), F32), SDS((LANES, n_heads), F32)],
        scratch_shapes=[pltpu.VMEM((tb, d_model), BF16), pltpu.VMEM((tb, d_model), F32)],
        input_output_aliases={0: 0},
        compiler_params=_params(1),
    )(dproj, proj, proj, proj, d_s, ln_g, ln_b, w_spatial, b_spatial_t)


A1_ROWS = 8


def _branch_a_bwd_norm(dproj, proj, h1, d_h3, gn_g, gn_b, seq):
    n_tok = proj.shape[0]
    d_model = gn_g.shape[1]
    ta = _tile(seq, 256, 16)

    def body(dproj_in, gt_ref, h1_ref, dh3_ref, gg_ref, gb_ref, dproj_ref, dh1_ref, vec_ref):
        del dproj_in

        @pl.when(pl.program_id(0) == 0)
        def _():
            vec_ref[...] = jnp.zeros_like(vec_ref)

        def group(g, carry):
            sl = pl.ds(pl.multiple_of(g * LANES, LANES), LANES)
            h1 = h1_ref[:, sl]
            mu = jnp.mean(h1, axis=-1, keepdims=True)
            dlt = h1 - mu
            rstd = lax.rsqrt(jnp.mean(dlt * dlt, axis=-1, keepdims=True) + LN_EPS)
            nrm = dlt * rstd
            sw, dsw = _silu_and_grad(nrm * gg_ref[:, sl] + gb_ref[:, sl])
            sg, dsg = _silu_and_grad(gt_ref[:, sl])
            dh3 = dh3_ref[:, sl]
            dgate = dh3 * sw * dsg
            dproj_ref[:, sl] = dgate.astype(BF16)
            vec_ref[0:1, sl] += _colsum(dgate)
            dh2 = dh3 * sg * dsw
            vec_ref[1:2, sl] += _colsum(dh2 * nrm)
            vec_ref[2:3, sl] += _colsum(dh2)
            dn = dh2 * gg_ref[:, sl]
            dh1 = rstd * (dn - jnp.mean(dn, axis=-1, keepdims=True)
                          - nrm * jnp.mean(dn * nrm, axis=-1, keepdims=True))
            vec_ref[3:4, sl] += _colsum(dh1)
            dh1_ref[:, sl] = dh1
            return carry

        lax.fori_loop(0, d_model // LANES, group, 0, unroll=True)

    tile = pl.BlockSpec((ta, d_model), lambda i: (i, 0))
    row = pl.BlockSpec((1, d_model), lambda i: (0, 0))
    return pl.pallas_call(
        body, name="branch_a_bwd_norm", grid=(n_tok // ta,),
        in_specs=[ANY, pl.BlockSpec((ta, d_model), lambda i: (i, 2)), tile, tile, row, row],
        out_specs=[pl.BlockSpec((ta, d_model), lambda i: (i, 2)), tile,
                   pl.BlockSpec((A1_ROWS, d_model), lambda i: (0, 0))],
        out_shape=[SDS(dproj.shape, BF16), SDS((n_tok, d_model), F32), SDS((A1_ROWS, d_model), F32)],
        input_output_aliases={0: 0},
        compiler_params=_params(1),
    )(dproj, proj, h1, d_h3, gn_g, gn_b)


A2_ROWS = 8


def _branch_a_bwd_conv(dproj, proj, d_h1, conv_w_full, gp_bf, gp_f32, seq):
    n_tok = proj.shape[0]
    d_model = conv_w_full.shape[1]
    n_groups = d_model // LANES
    ta = _tile(seq, 256, HALO)
    n_tiles = n_tok // ta
    per_seq = seq // ta
    rc = _conv_rows(ta)
    last_halo = n_tok // HALO - 1
    r8 = d_model // N_DEV
    prow = _tile(r8, 32, 16)

    def body(dproj_in, av_ref, ag_ref, avh_ref, agh_ref, dh1_ref, dh1h_ref, cw_ref, gp_bf_ref, gp_f32_ref,
             dproj_ref, vec_ref, dcw_ref, opa_ref, opb_ref, opo_ref,
             ext_h0, ext_d, rbuf, own, send_sems, recv_sems, local_sems):
        del dproj_in
        i = pl.program_id(0)
        x, y, c = _mesh_pos()
        me = _block_of((x, y, c))
        peers = _peers()

        def sends():
            return [pltpu.make_async_remote_copy(
                src_ref=gp_bf_ref.at[a, pl.ds(pl.multiple_of(blk * r8, 16), r8), :], dst_ref=rbuf.at[k, a],
                send_sem=send_sems.at[a, k], recv_sem=recv_sems.at[a, k], device_id=pos, device_id_type=MESH)
                for k, (pos, blk) in enumerate(peers) for a in range(3)]

        def own_rows():
            return [pltpu.make_async_copy(gp_f32_ref.at[a, pl.ds(pl.multiple_of(me * r8, 8), r8), :],
                                          own.at[a], local_sems.at[a]) for a in range(3)]

        @pl.when(i == 0)
        def _():
            vec_ref[...] = jnp.zeros_like(vec_ref)
            dcw_ref[...] = jnp.zeros_like(dcw_ref)
            for cp in sends() + own_rows():
                cp.start()

        keep_past = jnp.where(i % per_seq == 0, 0.0, 1.0)
        keep_next = jnp.where(i % per_seq == per_seq - 1, 0.0, 1.0)

        def group(g, carry):
            sl = pl.ds(pl.multiple_of(g * LANES, LANES), LANES)
            av = av_ref[:, sl]
            sig = _sigmoid(ag_ref[:, sl])
            ext_h0[0:HALO, :] = avh_ref[:, sl] * _sigmoid(agh_ref[:, sl]) * keep_past
            ext_h0[HALO:HALO + ta, :] = av * sig
            ext_d[0:ta, :] = dh1_ref[:, sl]
            ext_d[ta:ta + HALO, :] = dh1h_ref[:, sl] * keep_next
            for r0 in range(0, ta, rc):
                dh1 = ext_d[pl.ds(r0, rc), :]
                acc = jnp.zeros((rc, LANES), F32)
                for k in range(CONV_K):
                    acc = acc + ext_d[pl.ds(r0 + CONV_K - 1 - k, rc), :] * cw_ref[k:k + 1, sl]
                    prod = dh1 * ext_h0[pl.ds(r0 + HALO - (CONV_K - 1) + k, rc), :]
                    dcw_ref[g, k] += jnp.sum(prod.reshape(rc // SUBLANES, SUBLANES, LANES), axis=0)
                rows = pl.ds(r0, rc)
                sig_r = sig[r0:r0 + rc]
                dav = acc * sig_r
                dag = acc * av[r0:r0 + rc] * sig_r * (1.0 - sig_r)
                dproj_ref[rows, sl] = dav.astype(BF16)
                dproj_ref[rows, pl.ds(pl.multiple_of(d_model + g * LANES, LANES), LANES)] = dag.astype(BF16)
                vec_ref[0:1, sl] += _colsum(dav)
                vec_ref[1:2, sl] += _colsum(dag)
            return carry

        lax.fori_loop(0, n_groups, group, 0, unroll=True)

        @pl.when(i == n_tiles - 1)
        def _():
            for cp in own_rows():
                cp.wait()
            for cp in sends():
                cp.wait_recv()
            for a, o in enumerate([opa_ref, opb_ref, opo_ref]):
                for q in range(r8 // prow):
                    r = pl.ds(q * prow, prow)
                    tot = own[a, r, :]
                    for k in range(N_DEV - 1):
                        tot = tot + rbuf[k, a, r, :].astype(F32)
                    o[r, :] = tot
            for cp in sends():
                cp.wait_send()

    blk = lambda j: pl.BlockSpec((ta, d_model), lambda i: (i, j))
    halo = lambda j: pl.BlockSpec((HALO, d_model), lambda i: (jnp.maximum(i * (ta // HALO) - 1, 0), j))
    shard = pl.BlockSpec((r8, d_model), lambda i: (0, 0))
    return pl.pallas_call(
        body, name="branch_a_bwd_conv", grid=(n_tiles,),
        in_specs=[ANY, blk(0), blk(1), halo(0), halo(1), pl.BlockSpec((ta, d_model), lambda i: (i, 0)),
                  pl.BlockSpec((HALO, d_model), lambda i: (jnp.minimum((i + 1) * (ta // HALO), last_halo), 0)),
                  pl.BlockSpec((HALO, d_model), lambda i: (0, 0)), ANY, ANY],
        out_specs=[pl.BlockSpec((ta, 2 * d_model), lambda i: (i, 0)),
                   pl.BlockSpec((A2_ROWS, d_model), lambda i: (0, 0)),
                   pl.BlockSpec((n_groups, HALO, SUBLANES, LANES), lambda i: (0, 0, 0, 0)),
                   shard, shard, shard],
        out_shape=[SDS(dproj.shape, BF16), SDS((A2_ROWS, d_model), F32),
                   SDS((n_groups, HALO, SUBLANES, LANES), F32)] + [SDS((r8, d_model), F32)] * 3,
        scratch_shapes=[pltpu.VMEM((HALO + ta, LANES), F32), pltpu.VMEM((ta + HALO, LANES), F32),
                        pltpu.VMEM((N_DEV - 1, 3, r8, d_model), BF16), pltpu.VMEM((3, r8, d_model), F32),
                        pltpu.SemaphoreType.DMA((3, 7)), pltpu.SemaphoreType.DMA((3, 7)),
                        pltpu.SemaphoreType.DMA((3,))],
        input_output_aliases={0: 0},
        compiler_params=_params(1),
    )(dproj, proj, proj, proj, proj, d_h1, d_h1, conv_w_full, gp_bf, gp_f32)


def _weight_grads(lhs3, rhs3):
    n_mat, n_tok, d_model = lhs3.shape
    tk = _tile(n_tok, 1024, 16)
    n_k = n_tok // tk

    def body(a_ref, g_ref, o_ref, ob_ref):
        part = _dot_ta(a_ref[...], g_ref[...])

        @pl.when(pl.program_id(1) == 0)
        def _():
            o_ref[...] = part

        @pl.when(pl.program_id(1) != 0)
        def _():
            o_ref[...] += part

        @pl.when(pl.program_id(1) == n_k - 1)
        def _():
            ob_ref[...] = o_ref[...].astype(BF16)

    tile = pl.BlockSpec((None, tk, d_model), lambda a, i: (a, i, 0))
    out = pl.BlockSpec((None, d_model, d_model), lambda a, i: (a, 0, 0))
    return pl.pallas_call(
        body, name="grad_w_pa_pb_o", grid=(n_mat, n_k), in_specs=[tile, tile], out_specs=[out, out],
        out_shape=[SDS((n_mat, d_model, d_model), F32), SDS((n_mat, d_model, d_model), BF16)],
        compiler_params=_params(2),
    )(lhs3, rhs3)


def _grad_w_in_reduce_scatter(xt_bf, dproj, blk_order, small_part):
    n_tok = dproj.shape[0]
    d_model = xt_bf.shape[0]
    dh = d_model // 2
    n_units = 2 * N_DEV
    rsl = small_part.shape[0] // N_DEV

    def body(ord_ref, a_ref, g_ref, p_ref, o_ref, small_ref, acc, fb, sbuf, gbuf, tbuf, rfin, rbuf_s, red,
             send_f, send_s, recv_g, recv_t, recv_f, out_sems, send1, recv1, send2, recv2):
        del ord_ref
        u = pl.program_id(0)
        s = u // 2
        hf = u % 2
        rnd = s // 2
        x, y, c = _mesh_pos()
        sibling = (x, y, 1 - c)
        me = _block_of((x, y, c))
        peers = _peers()

        def rows_of(blk):
            return pl.ds(pl.multiple_of(blk * rsl, SUBLANES), rsl)

        def scatter():
            return [pltpu.make_async_remote_copy(
                src_ref=p_ref.at[rows_of(blk), :], dst_ref=rbuf_s.at[k], send_sem=send1.at[k],
                recv_sem=recv1.at[k], device_id=pos, device_id_type=MESH) for k, (pos, blk) in enumerate(peers)]

        def gather(dst_block=None):
            return [pltpu.make_async_remote_copy(
                src_ref=red, dst_ref=small_ref.at[rows_of(me if dst_block is None else blk), :],
                send_sem=send2.at[k], recv_sem=recv2.at[k], device_id=pos, device_id_type=MESH)
                for k, (pos, blk) in enumerate(peers)]

        def own_slice():
            return pltpu.make_async_copy(red, small_ref.at[rows_of(me), :], out_sems.at[2])

        @pl.when(u == 0)
        def _():
            for cp in scatter():
                cp.start()

        @pl.when(u == 4)
        def _():
            for cp in scatter():
                cp.wait_recv()
            tot = p_ref[rows_of(me), :]
            for k in range(N_DEV - 1):
                tot = tot + rbuf_s[k]
            red[...] = tot
            own_slice().start()
            for cp in gather():
                cp.start()

        n1 = (jnp.bitwise_xor(x, c), jnp.bitwise_xor(y, 1 - c), c)
        n2 = (jnp.bitwise_xor(x, 1 - c), jnp.bitwise_xor(y, c), c)

        def feed(r, half):
            return pltpu.make_async_remote_copy(
                src_ref=fb.at[half], dst_ref=gbuf.at[r, half], send_sem=send_f.at[r, half],
                recv_sem=recv_g.at[r, half], device_id=sibling, device_id_type=MESH)

        def feed_sibling(half):
            return pltpu.make_async_remote_copy(
                src_ref=fb.at[half], dst_ref=rfin.at[0, half], send_sem=send_f.at[3, half],
                recv_sem=recv_f.at[0, half], device_id=sibling, device_id_type=MESH)

        def chip_sum(r, half):
            dst = [tbuf.at[half], rfin.at[2, half], rfin.at[1, half]][r]
            sem = [recv_t.at[half], recv_f.at[2, half], recv_f.at[1, half]][r]
            return pltpu.make_async_remote_copy(
                src_ref=sbuf.at[r, half], dst_ref=dst, send_sem=send_s.at[r, half], recv_sem=sem,
                device_id=[n2, n2, n1][r], device_id_type=MESH)

        def out_copy(half):
            return pltpu.make_async_copy(acc.at[half], o_ref.at[:, pl.ds(half * dh, dh)], out_sems.at[half])

        def partial_sum():
            return _dot(a_ref[:, 0:n_tok], g_ref[...])

        for half in range(2):
            for r in range(3):
                @pl.when(u == 4 * r + 4 + half)
                def _(r=r, half=half):
                    feed(r, half).wait_send()

                @pl.when(u == 4 * r + 2 + half)
                def _(r=r, half=half):
                    feed(r, half).wait_recv()
                    if r == 2:
                        chip_sum(0, half).wait_recv()

            @pl.when(u == 14 + half)
            def _(half=half):
                feed_sibling(half).wait_recv()
                chip_sum(2, half).wait_recv()
                chip_sum(1, half).wait_recv()

        @pl.when(jnp.logical_and(s % 2 == 0, s < 7))
        def _():
            fb[hf] = partial_sum().astype(BF16)

        @pl.when(jnp.logical_or(s == 1, s == 3))
        def _():
            sbuf[rnd, hf] = (partial_sum() + gbuf[rnd, hf].astype(F32)).astype(BF16)

        @pl.when(s == 5)
        def _():
            sbuf[2, hf] = (partial_sum() + gbuf[2, hf].astype(F32) + tbuf[hf].astype(F32)).astype(BF16)

        @pl.when(s == 7)
        def _():
            acc[hf] = (partial_sum() + rfin[0, hf].astype(F32) + rfin[1, hf].astype(F32)
                       + rfin[2, hf].astype(F32))

        for half in range(2):
            for r in range(3):
                @pl.when(u == 4 * r + half)
                def _(r=r, half=half):
                    feed(r, half).start()

                @pl.when(u == 4 * r + 2 + half)
                def _(r=r, half=half):
                    chip_sum(r, half).start()

            @pl.when(u == 12 + half)
            def _(half=half):
                feed_sibling(half).start()

        @pl.when(u == 14)
        def _():
            out_copy(0).start()

        @pl.when(u == 15)
        def _():
            out_copy(1).start()
            for half in range(2):
                feed_sibling(half).wait_send()
                for r in range(3):
                    chip_sum(r, half).wait_send()
                out_copy(half).wait()
            for cp in gather("theirs"):
                cp.wait_recv()
            for cp in scatter() + gather():
                cp.wait_send()
            own_slice().wait()

    grid_spec = pltpu.PrefetchScalarGridSpec(
        num_scalar_prefetch=1, grid=(n_units,),
        in_specs=[VMEM, pl.BlockSpec((n_tok, dh), lambda u, o: (0, 2 * o[u // 2] + u % 2)), VMEM],
        out_specs=[ANY, ANY],
        scratch_shapes=[pltpu.VMEM((2, d_model, dh), F32), pltpu.VMEM((2, d_model, dh), BF16),
                        pltpu.VMEM((3, 2, d_model, dh), BF16), pltpu.VMEM((3, 2, d_model, dh), BF16),
                        pltpu.VMEM((2, d_model, dh), BF16), pltpu.VMEM((3, 2, d_model, dh), BF16),
                        pltpu.VMEM((N_DEV - 1, rsl, LANES), F32), pltpu.VMEM((rsl, LANES), F32),
                        pltpu.SemaphoreType.DMA((4, 2)), pltpu.SemaphoreType.DMA((3, 2)),
                        pltpu.SemaphoreType.DMA((3, 2)), pltpu.SemaphoreType.DMA((2,)),
                        pltpu.SemaphoreType.DMA((3, 2)),
                        pltpu.SemaphoreType.DMA((3,))] + [pltpu.SemaphoreType.DMA((N_DEV - 1,))] * 4)
    return pl.pallas_call(
        body, name="grad_w_in_reduce_scatter", grid_spec=grid_spec,
        out_shape=[SDS((d_model, d_model), F32), SDS(small_part.shape, F32)], compiler_params=_params(1),
    )(blk_order, xt_bf, dproj, small_part)


def _grad_x(dproj, w_all, dr):
    n_tok, d_model = dr.shape
    tm = _tile(n_tok, 512, 16)

    def body(dp_ref, w_ref, dr_ref, o_ref):
        acc = _dot_tb(dp_ref[:, 0:d_model], w_ref[0])
        for j in range(1, N_DEV):
            acc = acc + _dot_tb(dp_ref[:, j * d_model:(j + 1) * d_model], w_ref[j])
        o_ref[...] = acc + DEEPNORM_ALPHA * dr_ref[...]

    return pl.pallas_call(
        body, name="grad_x", grid=(n_tok // tm,),
        in_specs=[pl.BlockSpec((tm, N_DEV * d_model), lambda i: (i, 0)), VMEM,
                  pl.BlockSpec((tm, d_model), lambda i: (i, 0))],
        out_specs=pl.BlockSpec((tm, d_model), lambda i: (i, 0)),
        out_shape=SDS((n_tok, d_model), F32),
        compiler_params=_params(1),
    )(dproj, w_all, dr)


def _adamw_math(w, g, m, v):
    m = ADAM_B1 * m + (1.0 - ADAM_B1) * g
    v = ADAM_B2 * v + (1.0 - ADAM_B2) * (g * g)
    m_hat = m / (1.0 - ADAM_B1 ** ADAM_STEP)
    v_hat = v / (1.0 - ADAM_B2 ** ADAM_STEP)
    delta = -ADAM_LR * (m_hat / (jnp.sqrt(v_hat) + ADAM_EPS) + ADAM_WD * w)
    return delta, m, v


def _adamw_tiled(w, g, m, v):
    n_rows, n_cols = w.shape
    tr = _tile(n_rows, 256, SUBLANES)

    def body(w_ref, g_ref, m_ref, v_ref, d_ref, mo_ref, vo_ref):
        d_ref[...], mo_ref[...], vo_ref[...] = _adamw_math(w_ref[...], g_ref[...], m_ref[...], v_ref[...])

    tile = pl.BlockSpec((tr, n_cols), lambda i: (i, 0))
    return pl.pallas_call(
        body, name="adamw_w_in", grid=(n_rows // tr,), in_specs=[tile] * 4, out_specs=[tile] * 3,
        out_shape=[SDS(w.shape, F32)] * 3, compiler_params=_params(1),
    )(w, g, m, v)


def _adamw_many(groups):
    n = len(groups)

    def body(*refs):
        ins, outs = refs[:4 * n], refs[4 * n:]
        for p in range(n):
            w_ref, g_ref, m_ref, v_ref = ins[4 * p:4 * p + 4]
            d, m, v = _adamw_math(w_ref[...], g_ref[...], m_ref[...], v_ref[...])
            outs[3 * p][...] = d
            outs[3 * p + 1][...] = m
            outs[3 * p + 2][...] = v

    flat = [a for grp in groups for a in grp]
    out_shape = [SDS(grp[0].shape, F32) for grp in groups for _ in range(3)]
    res = pl.pallas_call(
        body, name="adamw_small", in_specs=[VMEM] * (4 * n), out_specs=[VMEM] * (3 * n),
        out_shape=out_shape, compiler_params=_params(),
    )(*flat)
    return [tuple(res[3 * p:3 * p + 3]) for p in range(n)]


def _as_rows(a):
    return a.reshape(-1, LANES)


def kernel(x, w_in, b_in, conv_w, conv_b, gn_g, gn_b, ln_v_g, ln_v_b, w_spatial, b_spatial, w_pa, w_pb, w_o, b_o, ln_out_g, ln_out_b, loss_target, m_w_in, m_b_in, m_conv_w, m_conv_b, m_gn_g, m_gn_b, m_ln_v_g, m_ln_v_b, m_w_spatial, m_b_spatial, m_w_pa, m_w_pb, m_w_o, m_b_o, m_ln_out_g, m_ln_out_b, v_w_in, v_b_in, v_conv_w, v_conv_b, v_gn_g, v_gn_b, v_ln_v_g, v_ln_v_b, v_w_spatial, v_b_spatial, v_w_pa, v_w_pb, v_w_o, v_b_o, v_ln_out_g, v_ln_out_b):
    n_batch, seq, d_model = x.shape
    n_tok = n_batch * seq
    n_heads = d_model // LANES
    dc = conv_w.shape[1]
    me = 4 * lax.axis_index("x") + 2 * lax.axis_index("y") + lax.axis_index("c")
    row = lambda a: a.reshape(1, d_model)

    x2 = x.reshape(n_tok, d_model)
    target2 = loss_target.reshape(n_tok, d_model)
    b_spatial_t = b_spatial.T

    first = jnp.where(lax.axis_index("c") == 1, 4, 2)
    second = 6 - first
    ag_rel = jnp.stack([0 * first, 0 * first + 1, first, second + 1, second, first + 1, 0 * first + 6, 0 * first + 7])
    ag_blocks = jnp.bitwise_xor(me, ag_rel).astype(jnp.int32)
    proj, xt_bf, w_all, cw_all = _proj_all_gather(
        x2, w_in, conv_w, b_in.reshape(N_DEV, 1, d_model), ag_blocks)
    conv_w_full = jnp.pad(cw_all.transpose(1, 0, 2).reshape(CONV_K, d_model), ((0, HALO - CONV_K), (0, 0)))

    h3, h1, wp_all = _branch_a_fwd(proj, conv_w_full, row(conv_b), row(gn_g), row(gn_b), w_pa, w_pb, w_o, seq)
    wp_full = [wp_all[:, a].reshape(d_model, d_model) for a in range(3)]
    s = _branch_b_fwd(proj, row(ln_v_g), row(ln_v_b), w_spatial, b_spatial_t, seq)

    dproj, d_h3, d_s, dr, lhs3, rhs3, vec_mid = _mid(
        h3, s, proj, x2, target2, *wp_full, row(b_o), row(ln_out_g), row(ln_out_b))

    dproj, vec_b, d_ws, d_bs_t = _branch_b_bwd(dproj, proj, d_s, row(ln_v_g), row(ln_v_b), w_spatial, b_spatial_t, seq)
    dproj, d_h1, vec_a1 = _branch_a_bwd_norm(dproj, proj, h1, d_h3, row(gn_g), row(gn_b), seq)
    gp_f32, gp_bf = _weight_grads(lhs3, rhs3)
    dproj, vec_a2, d_cw8, g_w_pa, g_w_pb, g_w_o = _branch_a_bwd_conv(
        dproj, proj, d_h1, conv_w_full, gp_bf, gp_f32, seq)

    d_cw = jnp.sum(d_cw8, axis=2)
    pieces = [
        _as_rows(jnp.concatenate([vec_a2[0:2], vec_a1[0:1], vec_b[0:3], vec_mid[3:5]], axis=0)),
        _as_rows(jnp.concatenate([vec_a1[3:4], vec_a1[1:3], vec_b[3:5], vec_mid[2:3], vec_mid[0:2]], axis=0)),
        _as_rows(d_bs_t.T), _as_rows(d_ws), _as_rows(d_cw), _as_rows(vec_mid[5:6]),
    ]
    n_rows = sum(p.shape[0] for p in pieces)
    pad_rows = -n_rows % (N_DEV * SUBLANES)
    small_part = jnp.concatenate(pieces + [jnp.zeros((pad_rows, LANES), F32)], axis=0)

    rs_rel = jnp.stack([0 * first + 7, 0 * first + 6, first + 1, second, second + 1, first, 0 * first + 1, 0 * first])
    rs_blocks = jnp.bitwise_xor(me, rs_rel).astype(jnp.int32)
    g_w_in, small = _grad_w_in_reduce_scatter(xt_bf, dproj, rs_blocks, small_part)
    grad_x = _grad_x(dproj, w_all, dr).reshape(x.shape)

    g_rows = d_model // LANES
    o0 = N_DEV * g_rows
    g_b_in = small[0:o0].reshape(N_DEV * d_model)
    vecs = [small[o0 + a * g_rows:o0 + (a + 1) * g_rows].reshape(d_model) for a in range(8)]
    g_conv_b, g_gn_g, g_gn_b, g_ln_v_g, g_ln_v_b, g_b_o, g_ln_out_g, g_ln_out_b = vecs
    o1 = o0 + 8 * g_rows
    g_b_spatial = small[o1:o1 + n_heads].reshape(n_heads, LANES)
    o2 = o1 + n_heads
    g_w_spatial = small[o2:o2 + n_heads * LANES].reshape(n_heads, LANES, LANES)
    o3 = o2 + n_heads * LANES
    g_cw_full = small[o3:o3 + n_heads * HALO].reshape(n_heads, HALO, LANES).transpose(1, 0, 2).reshape(HALO, d_model)
    g_conv_w = lax.dynamic_slice(g_cw_full, (0, me * dc), (CONV_K, dc))
    o4 = o3 + n_heads * HALO
    loss = jnp.sum(small[o4:o4 + g_rows]) * (0.5 / d_model)

    d_w_in, nm_w_in, nv_w_in = _adamw_tiled(w_in, g_w_in, m_w_in, v_w_in)
    two_d = lambda a: a.reshape(-1, a.shape[-1]) if a.ndim != 1 else (
        a.reshape(-1, LANES) if a.shape[0] % LANES == 0 else a.reshape(1, -1))
    names = ["b_in", "conv_w", "conv_b", "gn_g", "gn_b", "ln_v_g", "ln_v_b", "w_spatial", "b_spatial",
             "w_pa", "w_pb", "w_o", "b_o", "ln_out_g", "ln_out_b"]
    ws = dict(b_in=b_in, conv_w=conv_w, conv_b=conv_b, gn_g=gn_g, gn_b=gn_b, ln_v_g=ln_v_g, ln_v_b=ln_v_b,
              w_spatial=w_spatial, b_spatial=b_spatial, w_pa=w_pa, w_pb=w_pb, w_o=w_o, b_o=b_o,
              ln_out_g=ln_out_g, ln_out_b=ln_out_b)
    gs = dict(b_in=g_b_in, conv_w=g_conv_w, conv_b=g_conv_b, gn_g=g_gn_g, gn_b=g_gn_b, ln_v_g=g_ln_v_g,
              ln_v_b=g_ln_v_b, w_spatial=g_w_spatial, b_spatial=g_b_spatial, w_pa=g_w_pa, w_pb=g_w_pb,
              w_o=g_w_o, b_o=g_b_o, ln_out_g=g_ln_out_g, ln_out_b=g_ln_out_b)
    ms = dict(b_in=m_b_in, conv_w=m_conv_w, conv_b=m_conv_b, gn_g=m_gn_g, gn_b=m_gn_b, ln_v_g=m_ln_v_g,
              ln_v_b=m_ln_v_b, w_spatial=m_w_spatial, b_spatial=m_b_spatial, w_pa=m_w_pa, w_pb=m_w_pb,
              w_o=m_w_o, b_o=m_b_o, ln_out_g=m_ln_out_g, ln_out_b=m_ln_out_b)
    vs = dict(b_in=v_b_in, conv_w=v_conv_w, conv_b=v_conv_b, gn_g=v_gn_g, gn_b=v_gn_b, ln_v_g=v_ln_v_g,
              ln_v_b=v_ln_v_b, w_spatial=v_w_spatial, b_spatial=v_b_spatial, w_pa=v_w_pa, w_pb=v_w_pb,
              w_o=v_w_o, b_o=v_b_o, ln_out_g=v_ln_out_g, ln_out_b=v_ln_out_b)
    upd = _adamw_many([tuple(two_d(d[n]) for d in (ws, gs, ms, vs)) for n in names])
    delta = {n: u[0].reshape(ws[n].shape) for n, u in zip(names, upd)}
    new_m = {n: u[1].reshape(ws[n].shape) for n, u in zip(names, upd)}
    new_v = {n: u[2].reshape(ws[n].shape) for n, u in zip(names, upd)}
    gs["w_in"], delta["w_in"], new_m["w_in"], new_v["w_in"] = g_w_in, d_w_in, nm_w_in, nv_w_in

    order = ["w_in"] + names
    return (loss, grad_x, *[gs[n] for n in order], *[delta[n] for n in order],
            *[new_m[n] for n in order], *[new_v[n] for n in order])
```

```python
import jax
import jax.numpy as jnp
from jax import lax
from jax.experimental import pallas as pl
from jax.experimental.pallas import tpu as pltpu

F32 = jnp.float32
BF16 = jnp.bfloat16
SDS = jax.ShapeDtypeStruct

N_DEV = 8
LANES = 128
SUBLANES = 8
CONV_K = 31
HALO = 32
LN_EPS = 1e-5
DEEPNORM_ALPHA = 2.0 ** 0.25
ADAM_LR, ADAM_B1, ADAM_B2, ADAM_EPS, ADAM_WD, ADAM_STEP = 0.001, 0.9, 0.999, 1e-08, 0.01, 10
GELU_C = 0.7978845608028654
GELU_A = 0.044715
VMEM_LIMIT = 56 * 1024 * 1024
MESH = pl.DeviceIdType.MESH
ANY = pl.BlockSpec(memory_space=pl.ANY)
VMEM = pl.BlockSpec(memory_space=pltpu.VMEM)


def _params(n_grid=0):
    sem = ("arbitrary",) * n_grid if n_grid else None
    return pltpu.CompilerParams(dimension_semantics=sem, vmem_limit_bytes=VMEM_LIMIT)


def _tile(n, pref, mult):
    t = min(n, pref)
    while n % t or t % mult:
        t -= 1
    return t


def _colsum(v):
    return jnp.sum(v, axis=0, keepdims=True)


def _sigmoid(v):
    return jax.nn.sigmoid(v)


def _silu_and_grad(v):
    s = _sigmoid(v)
    return v * s, s * (1.0 + v * (1.0 - s))


def _gelu_and_grad(v):
    inner = GELU_C * (v + GELU_A * v * v * v)
    th = jnp.tanh(inner)
    val = 0.5 * v * (1.0 + th)
    grad = 0.5 * (1.0 + th) + 0.5 * v * (1.0 - th * th) * GELU_C * (1.0 + 3.0 * GELU_A * v * v)
    return val, grad


def _tril_mask():
    r = lax.broadcasted_iota(jnp.int32, (LANES, LANES), 0)
    c = lax.broadcasted_iota(jnp.int32, (LANES, LANES), 1)
    return c <= r


def _dot(a, b):
    return jnp.dot(a, b, preferred_element_type=F32)


def _dot_tb(a, b):
    return lax.dot_general(a, b, (((1,), (1,)), ((), ())), preferred_element_type=F32)


def _dot_ta(a, b):
    return lax.dot_general(a, b, (((0,), (0,)), ((), ())), preferred_element_type=F32)


def _mesh_pos():
    return lax.axis_index("x"), lax.axis_index("y"), lax.axis_index("c")


def _block_of(pos):
    return 4 * pos[0] + 2 * pos[1] + pos[2]


def _peers():
    x, y, c = _mesh_pos()
    out = []
    for k in range(1, N_DEV):
        pos = (1 - x if k & 4 else x, 1 - y if k & 2 else y, 1 - c if k & 1 else c)
        out.append((pos, _block_of(pos)))
    return out


def _proj_all_gather(x2, w_in, w_pa, w_pb, w_o, conv_w, b_in3, blk_order):
    n_tok, d_model = x2.shape
    r8 = w_pa.shape[0]
    kc, dc = conv_w.shape
    tm = _tile(n_tok, 1024, LANES)
    n_t = n_tok // tm
    n_arr = 3

    def body(ord_ref, x_ref, b_ref, win_ref, wpa_ref, wpb_ref, wo_ref, cw_ref,
             proj_ref, xt_ref, wall_ref, wp_ref, cwall_ref,
             wbuf, xbuf, st_p, send_sems, recv_sems, local_sems, wall_sems):
        s = pl.program_id(0)
        t = pl.program_id(1)
        x, y, c = _mesh_pos()
        me = (x, y, c)
        sibling = (x, y, 1 - c)
        n1 = (jnp.bitwise_xor(x, c), jnp.bitwise_xor(y, 1 - c))
        n2 = (jnp.bitwise_xor(x, 1 - c), jnp.bitwise_xor(y, c))
        dg = (1 - x, 1 - y)
        outs = [wbuf, wp_ref, cwall_ref]
        srcs = [None, st_p, cw_ref]
        consumed = [me, sibling, (*n1, c), (*n2, 1 - c), (*n2, c), (*n1, 1 - c), (*dg, c), (*dg, 1 - c)]
        leaves = [(me, sibling), (me, (*n1, c)), (me, (*n2, c)), ((*n1, c), (*n2, c)),
                  ((*n1, c), sibling), ((*n2, c), sibling), ((*dg, c), sibling)]
        lands = [sibling, (*n1, c), (*n2, c), (*dg, c), (*n2, 1 - c), (*n1, 1 - c), (*dg, 1 - c)]

        def slot(o, pos):
            return o.at[_block_of(pos)]

        def copy(a, k, block, to, src=None):
            o = outs[a]
            return pltpu.make_async_remote_copy(
                src_ref=slot(o, block) if src is None else src, dst_ref=slot(o, block),
                send_sem=send_sems.at[a, k], recv_sem=recv_sems.at[a, k],
                device_id=to, device_id_type=MESH)

        def send(a, k):
            block, to = leaves[k]
            return copy(a, k, block, to, src=srcs[a] if k < 3 else None)

        def recv(a, k):
            return copy(a, k, lands[k], me)

        def local_copies():
            return [pltpu.make_async_copy(srcs[a], slot(outs[a], me), local_sems.at[a]) for a in (1, 2)]

        def to_hbm(step):
            return pltpu.make_async_copy(slot(wbuf, consumed[step]), slot(wall_ref, consumed[step]),
                                         wall_sems.at[step])

        def at_step(step):
            return pl.when(jnp.logical_and(s == step, t == 0))

        @at_step(0)
        def _():
            slot(wbuf, me)[...] = win_ref[...].astype(BF16)
            st_p[0] = wpa_ref[...].astype(BF16)
            st_p[1] = wpb_ref[...].astype(BF16)
            st_p[2] = wo_ref[...].astype(BF16)
            for a in range(n_arr):
                send(a, 0).start()
                send(a, 1).start()
            for cp in local_copies():
                cp.start()
            to_hbm(0).start()

        @at_step(1)
        def _():
            recv(0, 0).wait_recv()
            to_hbm(1).start()

        for rnd in range(3):
            @at_step(2 + 2 * rnd)
            def _(rnd=rnd):
                if rnd == 0:
                    for a in range(n_arr):
                        send(a, 2).start()
                recv(0, 1 + rnd).wait_recv()
                if rnd == 0:
                    send(0, 3).start()
                send(0, 4 + rnd).start()
                to_hbm(2 + 2 * rnd).start()

            @at_step(3 + 2 * rnd)
            def _(rnd=rnd):
                for a in (1, 2):
                    recv(a, 1 + rnd).wait_recv()
                    if rnd == 0:
                        send(a, 3).start()
                    send(a, 4 + rnd).start()
                recv(0, 4 + rnd).wait_recv()
                to_hbm(3 + 2 * rnd).start()

        rows = pl.ds(pl.multiple_of(t * tm, tm), tm)

        @pl.when(s == 0)
        def _():
            xb = x_ref[...].astype(BF16)
            xbuf[rows, :] = xb
            xt_ref[...] = xb.T

        proj_ref[...] = _dot(xbuf[rows, :], wbuf[ord_ref[s]]) + b_ref[...]

        @pl.when(jnp.logical_and(s == N_DEV - 1, t == n_t - 1))
        def _():
            for a in (1, 2):
                for k in (0, 4, 5, 6):
                    recv(a, k).wait_recv()
            for a in range(n_arr):
                for k in range(7):
                    send(a, k).wait_send()
            for cp in local_copies() + [to_hbm(step) for step in range(N_DEV)]:
                cp.wait()

    grid_spec = pltpu.PrefetchScalarGridSpec(
        num_scalar_prefetch=1, grid=(N_DEV, n_t),
        in_specs=[pl.BlockSpec((tm, d_model), lambda s, t, o: (jnp.where(s == 0, t, n_t - 1), 0)),
                  pl.BlockSpec((None, 1, d_model), lambda s, t, o: (o[s], 0, 0)),
                  VMEM, VMEM, VMEM, VMEM, VMEM],
        out_specs=[pl.BlockSpec((tm, d_model), lambda s, t, o: (t, o[s])),
                   pl.BlockSpec((d_model, tm), lambda s, t, o: (0, jnp.where(s == 0, t, n_t))),
                   ANY, ANY, ANY],
        scratch_shapes=[pltpu.VMEM((N_DEV, d_model, d_model), BF16), pltpu.VMEM((n_tok, d_model), BF16),
                        pltpu.VMEM((3, r8, d_model), BF16),
                        pltpu.SemaphoreType.DMA((n_arr, 7)), pltpu.SemaphoreType.DMA((n_arr, 7)),
                        pltpu.SemaphoreType.DMA((3,)), pltpu.SemaphoreType.DMA((N_DEV,))])
    return pl.pallas_call(
        body, name="proj_all_gather", grid_spec=grid_spec,
        out_shape=[SDS((n_tok, N_DEV * d_model), F32), SDS((d_model, (n_t + 1) * tm), BF16),
                   SDS((N_DEV, d_model, d_model), BF16), SDS((N_DEV, 3, r8, d_model), BF16),
                   SDS((N_DEV, kc, dc), F32)],
        compiler_params=_params(2),
    )(blk_order, x2, b_in3, w_in, w_pa, w_pb, w_o, conv_w)


def _conv_rows(ta):
    return _tile(ta, 64, SUBLANES)


def _branch_a_fwd(proj, conv_w_full, conv_b, gn_g, gn_b, seq):
    n_tok = proj.shape[0]
    d_model = conv_b.shape[1]
    ta = _tile(seq, 256, HALO)
    per_seq = seq // ta
    rc = _conv_rows(ta)

    def body(av_ref, ag_ref, gt_ref, avh_ref, agh_ref, cw_ref, cb_ref, gg_ref, gb_ref,
             h3_ref, h1_ref, ext):
        keep = jnp.where(pl.program_id(0) % per_seq == 0, 0.0, 1.0)

        def group(g, carry):
            sl = pl.ds(pl.multiple_of(g * LANES, LANES), LANES)
            ext[0:HALO, :] = avh_ref[:, sl] * _sigmoid(agh_ref[:, sl]) * keep
            ext[HALO:HALO + ta, :] = av_ref[:, sl] * _sigmoid(ag_ref[:, sl])
            for r0 in range(0, ta, rc):
                acc = jnp.broadcast_to(cb_ref[:, sl], (rc, LANES))
                for k in range(CONV_K):
                    acc = acc + ext[pl.ds(r0 + HALO - (CONV_K - 1) + k, rc), :] * cw_ref[k:k + 1, sl]
                h1_ref[pl.ds(r0, rc), sl] = acc
            h1 = h1_ref[:, sl]
            mu = jnp.mean(h1, axis=-1, keepdims=True)
            dlt = h1 - mu
            var = jnp.mean(dlt * dlt, axis=-1, keepdims=True)
            h2 = dlt * lax.rsqrt(var + LN_EPS) * gg_ref[:, sl] + gb_ref[:, sl]
            gate = gt_ref[:, sl]
            h3_ref[:, sl] = (h2 * _sigmoid(h2) * gate * _sigmoid(gate)).astype(BF16)
            return carry

        lax.fori_loop(0, d_model // LANES, group, 0, unroll=True)

    blk = lambda j: pl.BlockSpec((ta, d_model), lambda i: (i, j))
    halo = lambda j: pl.BlockSpec((HALO, d_model), lambda i: (jnp.maximum(i * (ta // HALO) - 1, 0), j))
    row = pl.BlockSpec((1, d_model), lambda i: (0, 0))
    return pl.pallas_call(
        body, name="branch_a_fwd", grid=(n_tok // ta,),
        in_specs=[blk(0), blk(1), blk(2), halo(0), halo(1),
                  pl.BlockSpec((HALO, d_model), lambda i: (0, 0)), row, row, row],
        out_specs=[pl.BlockSpec((ta, d_model), lambda i: (i, 0))] * 2,
        out_shape=[SDS((n_tok, d_model), BF16), SDS((n_tok, d_model), F32)],
        scratch_shapes=[pltpu.VMEM((HALO + ta, LANES), F32)],
        compiler_params=_params(1),
    )(proj, proj, proj, proj, proj, conv_w_full, conv_b, gn_g, gn_b)


def _branch_b_fwd(proj, ln_g, ln_b, w_spatial, b_spatial_t, seq):
    n_tok = proj.shape[0]
    d_model = ln_g.shape[1]
    n_heads = d_model // LANES
    tb = _tile(seq, 256, LANES)

    def body(u_ref, v_ref, bg_ref, lg_ref, lb_ref, ws_ref, bs_ref, s_ref, vn_buf):
        v, _ = _gelu_and_grad(v_ref[...])
        mu = jnp.mean(v, axis=-1, keepdims=True)
        dlt = v - mu
        var = jnp.mean(dlt * dlt, axis=-1, keepdims=True)
        vn_buf[...] = (dlt * lax.rsqrt(var + LN_EPS) * lg_ref[...] + lb_ref[...]).astype(BF16)
        tril = _tril_mask()
        for h in range(n_heads):
            cols = slice(h * LANES, (h + 1) * LANES)
            w_h = jnp.where(tril, ws_ref[h], 0.0).astype(BF16)
            bias = bs_ref[:, h:h + 1]
            for ch in range(tb // LANES):
                rows = slice(ch * LANES, (ch + 1) * LANES)
                mix = _dot(w_h, vn_buf[rows, cols]) + bias
                u, _ = _gelu_and_grad(u_ref[rows, cols])
                gate = bg_ref[rows, cols]
                s_ref[rows, cols] = (u * mix * gate * _sigmoid(gate)).astype(BF16)

    blk = lambda j: pl.BlockSpec((tb, d_model), lambda i: (i, j))
    row = pl.BlockSpec((1, d_model), lambda i: (0, 0))
    return pl.pallas_call(
        body, name="branch_b_fwd", grid=(n_tok // tb,),
        in_specs=[blk(3), blk(4), blk(5), row, row,
                  pl.BlockSpec((n_heads, LANES, LANES), lambda i: (0, 0, 0)),
                  pl.BlockSpec((LANES, n_heads), lambda i: (0, 0))],
        out_specs=pl.BlockSpec((tb, d_model), lambda i: (i, 0)),
        out_shape=SDS((n_tok, d_model), BF16),
        scratch_shapes=[pltpu.VMEM((tb, d_model), BF16)],
        compiler_params=_params(1),
    )(proj, proj, proj, ln_g, ln_b, w_spatial, b_spatial_t)


MID_ROWS = 8


def _mid(h3, s, proj, x2, target, w_pa, w_pb, w_o, b_o, lo_g, lo_b):
    n_tok, d_model = x2.shape
    tm = _tile(n_tok, 256, 16)

    def body(h3_ref, s_ref, ma_ref, mb_ref, x_ref, t_ref, wpa_ref, wpb_ref, wo_ref, bo_ref,
             lg_ref, lb_ref, dproj_ref, dh3_ref, ds_ref, dr_ref, lhs3_ref, rhs3_ref, vec_ref):
        @pl.when(pl.program_id(0) == 0)
        def _():
            vec_ref[...] = jnp.zeros_like(vec_ref)

        h3 = h3_ref[...]
        s = s_ref[...]
        ya = _dot(h3, wpa_ref[...])
        yb = _dot(s, wpb_ref[...])
        ga = _sigmoid(ma_ref[...])
        gb = _sigmoid(mb_ref[...])
        mixed = (ga * ya + gb * yb).astype(BF16)
        lhs3_ref[0] = h3
        lhs3_ref[1] = s
        lhs3_ref[2] = mixed
        r = DEEPNORM_ALPHA * x_ref[...] + _dot(mixed, wo_ref[...]) + bo_ref[...]
        mu = jnp.mean(r, axis=-1, keepdims=True)
        dlt = r - mu
        rstd = lax.rsqrt(jnp.mean(dlt * dlt, axis=-1, keepdims=True) + LN_EPS)
        rhat = dlt * rstd
        diff = rhat * lg_ref[...] + lb_ref[...] - t_ref[...]
        dy = diff * (1.0 / d_model)
        vec_ref[0:1, :] += _colsum(dy * rhat)
        vec_ref[1:2, :] += _colsum(dy)
        vec_ref[5:6, :] += _colsum(diff * diff)
        drh = dy * lg_ref[...]
        dr = rstd * (drh - jnp.mean(drh, axis=-1, keepdims=True)
                     - rhat * jnp.mean(drh * rhat, axis=-1, keepdims=True))
        vec_ref[2:3, :] += _colsum(dr)
        dr_ref[...] = dr
        drb = dr.astype(BF16)
        rhs3_ref[2] = drb
        dmixed = _dot_tb(drb, wo_ref[...])
        dma = dmixed * ya * ga * (1.0 - ga)
        dmb = dmixed * yb * gb * (1.0 - gb)
        vec_ref[3:4, :] += _colsum(dma)
        vec_ref[4:5, :] += _colsum(dmb)
        dproj_ref[:, 0:d_model] = dma.astype(BF16)
        dproj_ref[:, d_model:2 * d_model] = dmb.astype(BF16)
        dya = (dmixed * ga).astype(BF16)
        dyb = (dmixed * gb).astype(BF16)
        rhs3_ref[0] = dya
        rhs3_ref[1] = dyb
        dh3_ref[...] = _dot_tb(dya, wpa_ref[...])
        ds_ref[...] = _dot_tb(dyb, wpb_ref[...])

    tile = pl.BlockSpec((tm, d_model), lambda i: (i, 0))
    full = pl.BlockSpec((d_model, d_model), lambda i: (0, 0))
    row = pl.BlockSpec((1, d_model), lambda i: (0, 0))
    stack = pl.BlockSpec((3, tm, d_model), lambda i: (0, i, 0))
    bf3 = SDS((3, n_tok, d_model), BF16)
    f32 = SDS((n_tok, d_model), F32)
    return pl.pallas_call(
        body, name="mid", grid=(n_tok // tm,),
        in_specs=[tile, tile, pl.BlockSpec((tm, d_model), lambda i: (i, 6)),
                  pl.BlockSpec((tm, d_model), lambda i: (i, 7)), tile, tile, full, full, full,
                  row, row, row],
        out_specs=[pl.BlockSpec((tm, 2 * d_model), lambda i: (i, 3)), tile, tile, tile, stack, stack,
                   pl.BlockSpec((MID_ROWS, d_model), lambda i: (0, 0))],
        out_shape=[SDS((n_tok, N_DEV * d_model), BF16), f32, f32, f32, bf3, bf3,
                   SDS((MID_ROWS, d_model), F32)],
        compiler_params=_params(1),
    )(h3, s, proj, proj, x2, target, w_pa, w_pb, w_o, b_o, lo_g, lo_b)


B_ROWS = 8


def _branch_b_bwd(dproj, proj, d_s, ln_g, ln_b, w_spatial, b_spatial_t, seq):
    n_tok = proj.shape[0]
    d_model = ln_g.shape[1]
    n_heads = d_model // LANES
    tb = _tile(seq, 256, LANES)

    def body(dproj_in, u_ref, v_ref, bg_ref, ds_ref, lg_ref, lb_ref, ws_ref, bs_ref,
             dproj_ref, vec_ref, dws_ref, dbs_ref, vn_buf, dv_buf):
        del dproj_in

        @pl.when(pl.program_id(0) == 0)
        def _():
            vec_ref[...] = jnp.zeros_like(vec_ref)
            dws_ref[...] = jnp.zeros_like(dws_ref)
            dbs_ref[...] = jnp.zeros_like(dbs_ref)

        v, dgelu_v = _gelu_and_grad(v_ref[...])
        mu = jnp.mean(v, axis=-1, keepdims=True)
        dlt = v - mu
        rstd = lax.rsqrt(jnp.mean(dlt * dlt, axis=-1, keepdims=True) + LN_EPS)
        vhat = dlt * rstd
        vn_buf[...] = (vhat * lg_ref[...] + lb_ref[...]).astype(BF16)
        tril = _tril_mask()
        for h in range(n_heads):
            cols = slice(h * LANES, (h + 1) * LANES)
            w_h = jnp.where(tril, ws_ref[h], 0.0).astype(BF16)
            bias = bs_ref[:, h:h + 1]
            for ch in range(tb // LANES):
                rows = slice(ch * LANES, (ch + 1) * LANES)
                vn = vn_buf[rows, cols]
                mix = _dot(w_h, vn) + bias
                u, dgelu_u = _gelu_and_grad(u_ref[rows, cols])
                sg, dsilu = _silu_and_grad(bg_ref[rows, cols])
                dsv = ds_ref[rows, cols]
                du = dsv * mix * sg * dgelu_u
                dbg = dsv * u * mix * dsilu
                dmix = dsv * u * sg
                dmix_bf = dmix.astype(BF16)
                dproj_ref[rows, cols] = du.astype(BF16)
                dproj_ref[rows, 2 * d_model + h * LANES:2 * d_model + (h + 1) * LANES] = dbg.astype(BF16)
                vec_ref[0:1, cols] += _colsum(du)
                vec_ref[2:3, cols] += _colsum(dbg)
                dbs_ref[:, h:h + 1] += jnp.sum(dmix, axis=1, keepdims=True)
                dws_ref[h] += jnp.where(tril, _dot_tb(dmix_bf, vn), 0.0)
                dv_buf[rows, cols] = _dot_ta(w_h, dmix_bf)
        dvn = dv_buf[...]
        vec_ref[3:4, :] += _colsum(dvn * vhat)
        vec_ref[4:5, :] += _colsum(dvn)
        dvh = dvn * lg_ref[...]
        dv = rstd * (dvh - jnp.mean(dvh, axis=-1, keepdims=True)
                     - vhat * jnp.mean(dvh * vhat, axis=-1, keepdims=True)) * dgelu_v
        vec_ref[1:2, :] += _colsum(dv)
        dproj_ref[:, d_model:2 * d_model] = dv.astype(BF16)

    blk = lambda j: pl.BlockSpec((tb, d_model), lambda i: (i, j))
    row = pl.BlockSpec((1, d_model), lambda i: (0, 0))
    return pl.pallas_call(
        body, name="branch_b_bwd", grid=(n_tok // tb,),
        in_specs=[ANY, blk(3), blk(4), blk(5), pl.BlockSpec((tb, d_model), lambda i: (i, 0)), row, row,
                  pl.BlockSpec((n_heads, LANES, LANES), lambda i: (0, 0, 0)),
                  pl.BlockSpec((LANES, n_heads), lambda i: (0, 0))],
        out_specs=[pl.BlockSpec((tb, 3 * d_model), lambda i: (i, 1)),
                   pl.BlockSpec((B_ROWS, d_model), lambda i: (0, 0)),
                   pl.BlockSpec((n_heads, LANES, LANES), lambda i: (0, 0, 0)),
                   pl.BlockSpec((LANES, n_heads), lambda i: (0, 0))],
        out_shape=[SDS(dproj.shape, BF16), SDS((B_ROWS, d_model), F32),
                   SDS((n_heads, LANES, LANES), F32), SDS((LANES, n_heads), F32)],
        scratch_shapes=[pltpu.VMEM((tb, d_model), BF16), pltpu.VMEM((tb, d_model), F32)],
        input_output_aliases={0: 0},
        compiler_params=_params(1),
    )(dproj, proj, proj, proj, d_s, ln_g, ln_b, w_spatial, b_spatial_t)


A1_ROWS = 8


def _branch_a_bwd_norm(dproj, proj, h1, d_h3, gn_g, gn_b, seq):
    n_tok = proj.shape[0]
    d_model = gn_g.shape[1]
    ta = _tile(seq, 256, 16)

    def body(dproj_in, gt_ref, h1_ref, dh3_ref, gg_ref, gb_ref, dproj_ref, dh1_ref, vec_ref):
        del dproj_in

        @pl.when(pl.program_id(0) == 0)
        def _():
            vec_ref[...] = jnp.zeros_like(vec_ref)

        def group(g, carry):
            sl = pl.ds(pl.multiple_of(g * LANES, LANES), LANES)
            h1 = h1_ref[:, sl]
            mu = jnp.mean(h1, axis=-1, keepdims=True)
            dlt = h1 - mu
            rstd = lax.rsqrt(jnp.mean(dlt * dlt, axis=-1, keepdims=True) + LN_EPS)
            nrm = dlt * rstd
            sw, dsw = _silu_and_grad(nrm * gg_ref[:, sl] + gb_ref[:, sl])
            sg, dsg = _silu_and_grad(gt_ref[:, sl])
            dh3 = dh3_ref[:, sl]
            dgate = dh3 * sw * dsg
            dproj_ref[:, sl] = dgate.astype(BF16)
            vec_ref[0:1, sl] += _colsum(dgate)
            dh2 = dh3 * sg * dsw
            vec_ref[1:2, sl] += _colsum(dh2 * nrm)
            vec_ref[2:3, sl] += _colsum(dh2)
            dn = dh2 * gg_ref[:, sl]
            dh1 = rstd * (dn - jnp.mean(dn, axis=-1, keepdims=True)
                          - nrm * jnp.mean(dn * nrm, axis=-1, keepdims=True))
            vec_ref[3:4, sl] += _colsum(dh1)
            dh1_ref[:, sl] = dh1
            return carry

        lax.fori_loop(0, d_model // LANES, group, 0, unroll=True)

    tile = pl.BlockSpec((ta, d_model), lambda i: (i, 0))
    row = pl.BlockSpec((1, d_model), lambda i: (0, 0))
    return pl.pallas_call(
        body, name="branch_a_bwd_norm", grid=(n_tok // ta,),
        in_specs=[ANY, pl.BlockSpec((ta, d_model), lambda i: (i, 2)), tile, tile, row, row],
        out_specs=[pl.BlockSpec((ta, d_model), lambda i: (i, 2)), tile,
                   pl.BlockSpec((A1_ROWS, d_model), lambda i: (0, 0))],
        out_shape=[SDS(dproj.shape, BF16), SDS((n_tok, d_model), F32), SDS((A1_ROWS, d_model), F32)],
        input_output_aliases={0: 0},
        compiler_params=_params(1),
    )(dproj, proj, h1, d_h3, gn_g, gn_b)


A2_ROWS = 8


def _branch_a_bwd_conv(dproj, proj, d_h1, conv_w_full, gp_bf, gp_f32, seq):
    n_tok = proj.shape[0]
    d_model = conv_w_full.shape[1]
    n_groups = d_model // LANES
    ta = _tile(seq, 256, HALO)
    n_tiles = n_tok // ta
    per_seq = seq // ta
    rc = _conv_rows(ta)
    last_halo = n_tok // HALO - 1
    r8 = d_model // N_DEV
    prow = _tile(r8, 32, 16)

    def body(dproj_in, av_ref, ag_ref, avh_ref, agh_ref, dh1_ref, dh1h_ref, cw_ref, gp_bf_ref, gp_f32_ref,
             dproj_ref, vec_ref, dcw_ref, opa_ref, opb_ref, opo_ref,
             ext_h0, ext_d, rbuf, own, send_sems, recv_sems, local_sems):
        del dproj_in
        i = pl.program_id(0)
        x, y, c = _mesh_pos()
        me = _block_of((x, y, c))
        peers = _peers()

        def sends():
            return [pltpu.make_async_remote_copy(
                src_ref=gp_bf_ref.at[a, pl.ds(pl.multiple_of(blk * r8, 16), r8), :], dst_ref=rbuf.at[k, a],
                send_sem=send_sems.at[a, k], recv_sem=recv_sems.at[a, k], device_id=pos, device_id_type=MESH)
                for k, (pos, blk) in enumerate(peers) for a in range(3)]

        def own_rows():
            return [pltpu.make_async_copy(gp_f32_ref.at[a, pl.ds(pl.multiple_of(me * r8, 8), r8), :],
                                          own.at[a], local_sems.at[a]) for a in range(3)]

        @pl.when(i == 0)
        def _():
            vec_ref[...] = jnp.zeros_like(vec_ref)
            dcw_ref[...] = jnp.zeros_like(dcw_ref)
            for cp in sends() + own_rows():
                cp.start()

        keep_past = jnp.where(i % per_seq == 0, 0.0, 1.0)
        keep_next = jnp.where(i % per_seq == per_seq - 1, 0.0, 1.0)

        def group(g, carry):
            sl = pl.ds(pl.multiple_of(g * LANES, LANES), LANES)
            av = av_ref[:, sl]
            sig = _sigmoid(ag_ref[:, sl])
            ext_h0[0:HALO, :] = avh_ref[:, sl] * _sigmoid(agh_ref[:, sl]) * keep_past
            ext_h0[HALO:HALO + ta, :] = av * sig
            ext_d[0:ta, :] = dh1_ref[:, sl]
            ext_d[ta:ta + HALO, :] = dh1h_ref[:, sl] * keep_next
            for r0 in range(0, ta, rc):
                dh1 = ext_d[pl.ds(r0, rc), :]
                acc = jnp.zeros((rc, LANES), F32)
                for k in range(CONV_K):
                    acc = acc + ext_d[pl.ds(r0 + CONV_K - 1 - k, rc), :] * cw_ref[k:k + 1, sl]
                    prod = dh1 * ext_h0[pl.ds(r0 + HALO - (CONV_K - 1) + k, rc), :]
                    dcw_ref[g, k] += jnp.sum(prod.reshape(rc // SUBLANES, SUBLANES, LANES), axis=0)
                rows = pl.ds(r0, rc)
                sig_r = sig[r0:r0 + rc]
                dav = acc * sig_r
                dag = acc * av[r0:r0 + rc] * sig_r * (1.0 - sig_r)
                dproj_ref[rows, sl] = dav.astype(BF16)
                dproj_ref[rows, pl.ds(pl.multiple_of(d_model + g * LANES, LANES), LANES)] = dag.astype(BF16)
                vec_ref[0:1, sl] += _colsum(dav)
                vec_ref[1:2, sl] += _colsum(dag)
            return carry

        lax.fori_loop(0, n_groups, group, 0, unroll=True)

        @pl.when(i == n_tiles - 1)
        def _():
            for cp in own_rows():
                cp.wait()
            for cp in sends():
                cp.wait_recv()
            for a, o in enumerate([opa_ref, opb_ref, opo_ref]):
                for q in range(r8 // prow):
                    r = pl.ds(q * prow, prow)
                    tot = own[a, r, :]
                    for k in range(N_DEV - 1):
                        tot = tot + rbuf[k, a, r, :].astype(F32)
                    o[r, :] = tot
            for cp in sends():
                cp.wait_send()

    blk = lambda j: pl.BlockSpec((ta, d_model), lambda i: (i, j))
    halo = lambda j: pl.BlockSpec((HALO, d_model), lambda i: (jnp.maximum(i * (ta // HALO) - 1, 0), j))
    shard = pl.BlockSpec((r8, d_model), lambda i: (0, 0))
    return pl.pallas_call(
        body, name="branch_a_bwd_conv", grid=(n_tiles,),
        in_specs=[ANY, blk(0), blk(1), halo(0), halo(1), pl.BlockSpec((ta, d_model), lambda i: (i, 0)),
                  pl.BlockSpec((HALO, d_model), lambda i: (jnp.minimum((i + 1) * (ta // HALO), last_halo), 0)),
                  pl.BlockSpec((HALO, d_model), lambda i: (0, 0)), ANY, ANY],
        out_specs=[pl.BlockSpec((ta, 2 * d_model), lambda i: (i, 0)),
                   pl.BlockSpec((A2_ROWS, d_model), lambda i: (0, 0)),
                   pl.BlockSpec((n_groups, HALO, SUBLANES, LANES), lambda i: (0, 0, 0, 0)),
                   shard, shard, shard],
        out_shape=[SDS(dproj.shape, BF16), SDS((A2_ROWS, d_model), F32),
                   SDS((n_groups, HALO, SUBLANES, LANES), F32)] + [SDS((r8, d_model), F32)] * 3,
        scratch_shapes=[pltpu.VMEM((HALO + ta, LANES), F32), pltpu.VMEM((ta + HALO, LANES), F32),
                        pltpu.VMEM((N_DEV - 1, 3, r8, d_model), BF16), pltpu.VMEM((3, r8, d_model), F32),
                        pltpu.SemaphoreType.DMA((3, 7)), pltpu.SemaphoreType.DMA((3, 7)),
                        pltpu.SemaphoreType.DMA((3,))],
        input_output_aliases={0: 0},
        compiler_params=_params(1),
    )(dproj, proj, proj, proj, proj, d_h1, d_h1, conv_w_full, gp_bf, gp_f32)


def _weight_grads(lhs3, rhs3):
    n_mat, n_tok, d_model = lhs3.shape
    tk = _tile(n_tok, 1024, 16)
    n_k = n_tok // tk

    def body(a_ref, g_ref, o_ref, ob_ref):
        part = _dot_ta(a_ref[...], g_ref[...])

        @pl.when(pl.program_id(1) == 0)
        def _():
            o_ref[...] = part

        @pl.when(pl.program_id(1) != 0)
        def _():
            o_ref[...] += part

        @pl.when(pl.program_id(1) == n_k - 1)
        def _():
            ob_ref[...] = o_ref[...].astype(BF16)

    tile = pl.BlockSpec((None, tk, d_model), lambda a, i: (a, i, 0))
    out = pl.BlockSpec((None, d_model, d_model), lambda a, i: (a, 0, 0))
    return pl.pallas_call(
        body, name="grad_w_pa_pb_o", grid=(n_mat, n_k), in_specs=[tile, tile], out_specs=[out, out],
        out_shape=[SDS((n_mat, d_model, d_model), F32), SDS((n_mat, d_model, d_model), BF16)],
        compiler_params=_params(2),
    )(lhs3, rhs3)


def _grad_w_in_reduce_scatter(xt_bf, dproj, blk_order, small_part):
    n_tok = dproj.shape[0]
    d_model = xt_bf.shape[0]
    dh = d_model // 2
    n_units = 2 * N_DEV
    rsl = small_part.shape[0] // N_DEV

    def body(ord_ref, a_ref, g_ref, p_ref, o_ref, small_ref, acc, fb, sbuf, gbuf, tbuf, rfin, rbuf_s, red,
             send_f, send_s, recv_g, recv_t, recv_f, out_sems, send1, recv1, send2, recv2):
        del ord_ref
        u = pl.program_id(0)
        s = u // 2
        hf = u % 2
        rnd = s // 2
        x, y, c = _mesh_pos()
        sibling = (x, y, 1 - c)
        me = _block_of((x, y, c))
        peers = _peers()

        def rows_of(blk):
            return pl.ds(pl.multiple_of(blk * rsl, SUBLANES), rsl)

        def scatter():
            return [pltpu.make_async_remote_copy(
                src_ref=p_ref.at[rows_of(blk), :], dst_ref=rbuf_s.at[k], send_sem=send1.at[k],
                recv_sem=recv1.at[k], device_id=pos, device_id_type=MESH) for k, (pos, blk) in enumerate(peers)]

        def gather(dst_block=None):
            return [pltpu.make_async_remote_copy(
                src_ref=red, dst_ref=small_ref.at[rows_of(me if dst_block is None else blk), :],
                send_sem=send2.at[k], recv_sem=recv2.at[k], device_id=pos, device_id_type=MESH)
                for k, (pos, blk) in enumerate(peers)]

        def own_slice():
            return pltpu.make_async_copy(red, small_ref.at[rows_of(me), :], out_sems.at[2])

        @pl.when(u == 0)
        def _():
            for cp in scatter():
                cp.start()

        @pl.when(u == 4)
        def _():
            for cp in scatter():
                cp.wait_recv()
            tot = p_ref[rows_of(me), :]
            for k in range(N_DEV - 1):
                tot = tot + rbuf_s[k]
            red[...] = tot
            own_slice().start()
            for cp in gather():
                cp.start()

        n1 = (jnp.bitwise_xor(x, c), jnp.bitwise_xor(y, 1 - c), c)
        n2 = (jnp.bitwise_xor(x, 1 - c), jnp.bitwise_xor(y, c), c)

        def feed(r, half):
            return pltpu.make_async_remote_copy(
                src_ref=fb.at[half], dst_ref=gbuf.at[r, half], send_sem=send_f.at[r, half],
                recv_sem=recv_g.at[r, half], device_id=sibling, device_id_type=MESH)

        def feed_sibling(half):
            return pltpu.make_async_remote_copy(
                src_ref=fb.at[half], dst_ref=rfin.at[0, half], send_sem=send_f.at[3, half],
                recv_sem=recv_f.at[0, half], device_id=sibling, device_id_type=MESH)

        def chip_sum(r, half):
            dst = [tbuf.at[half], rfin.at[2, half], rfin.at[1, half]][r]
            sem = [recv_t.at[half], recv_f.at[2, half], recv_f.at[1, half]][r]
            return pltpu.make_async_remote_copy(
                src_ref=sbuf.at[r, half], dst_ref=dst, send_sem=send_s.at[r, half], recv_sem=sem,
                device_id=[n2, n2, n1][r], device_id_type=MESH)

        def out_copy(half):
            return pltpu.make_async_copy(acc.at[half], o_ref.at[:, pl.ds(half * dh, dh)], out_sems.at[half])

        def partial_sum():
            return _dot(a_ref[:, 0:n_tok], g_ref[...])

        for half in range(2):
            for r in range(3):
                @pl.when(u == 4 * r + 4 + half)
                def _(r=r, half=half):
                    feed(r, half).wait_send()

                @pl.when(u == 4 * r + 2 + half)
                def _(r=r, half=half):
                    feed(r, half).wait_recv()
                    if r == 2:
                        chip_sum(0, half).wait_recv()

            @pl.when(u == 14 + half)
            def _(half=half):
                feed_sibling(half).wait_recv()
                chip_sum(2, half).wait_recv()
                chip_sum(1, half).wait_recv()

        @pl.when(jnp.logical_and(s % 2 == 0, s < 7))
        def _():
            fb[hf] = partial_sum().astype(BF16)

        @pl.when(jnp.logical_or(s == 1, s == 3))
        def _():
            sbuf[rnd, hf] = (partial_sum() + gbuf[rnd, hf].astype(F32)).astype(BF16)

        @pl.when(s == 5)
        def _():
            sbuf[2, hf] = (partial_sum() + gbuf[2, hf].astype(F32) + tbuf[hf].astype(F32)).astype(BF16)

        @pl.when(s == 7)
        def _():
            acc[hf] = (partial_sum() + rfin[0, hf].astype(F32) + rfin[1, hf].astype(F32)
                       + rfin[2, hf].astype(F32))

        for half in range(2):
            for r in range(3):
                @pl.when(u == 4 * r + half)
                def _(r=r, half=half):
                    feed(r, half).start()

                @pl.when(u == 4 * r + 2 + half)
                def _(r=r, half=half):
                    chip_sum(r, half).start()

            @pl.when(u == 12 + half)
            def _(half=half):
                feed_sibling(half).start()

        @pl.when(u == 14)
        def _():
            out_copy(0).start()

        @pl.when(u == 15)
        def _():
            out_copy(1).start()
            for half in range(2):
                feed_sibling(half).wait_send()
                for r in range(3):
                    chip_sum(r, half).wait_send()
                out_copy(half).wait()
            for cp in gather("theirs"):
                cp.wait_recv()
            for cp in scatter() + gather():
                cp.wait_send()
            own_slice().wait()

    grid_spec = pltpu.PrefetchScalarGridSpec(
        num_scalar_prefetch=1, grid=(n_units,),
        in_specs=[VMEM, pl.BlockSpec((n_tok, dh), lambda u, o: (0, 2 * o[u // 2] + u % 2)), VMEM],
        out_specs=[ANY, ANY],
        scratch_shapes=[pltpu.VMEM((2, d_model, dh), F32), pltpu.VMEM((2, d_model, dh), BF16),
                        pltpu.VMEM((3, 2, d_model, dh), BF16), pltpu.VMEM((3, 2, d_model, dh), BF16),
                        pltpu.VMEM((2, d_model, dh), BF16), pltpu.VMEM((3, 2, d_model, dh), BF16),
                        pltpu.VMEM((N_DEV - 1, rsl, LANES), F32), pltpu.VMEM((rsl, LANES), F32),
                        pltpu.SemaphoreType.DMA((4, 2)), pltpu.SemaphoreType.DMA((3, 2)),
                        pltpu.SemaphoreType.DMA((3, 2)), pltpu.SemaphoreType.DMA((2,)),
                        pltpu.SemaphoreType.DMA((3, 2)),
                        pltpu.SemaphoreType.DMA((3,))] + [pltpu.SemaphoreType.DMA((N_DEV - 1,))] * 4)
    return pl.pallas_call(
        body, name="grad_w_in_reduce_scatter", grid_spec=grid_spec,
        out_shape=[SDS((d_model, d_model), F32), SDS(small_part.shape, F32)], compiler_params=_params(1),
    )(blk_order, xt_bf, dproj, small_part)


def _grad_x(dproj, w_all, dr):
    n_tok, d_model = dr.shape
    tm = _tile(n_tok, 512, 16)

    def body(dp_ref, w_ref, dr_ref, o_ref):
        acc = _dot_tb(dp_ref[:, 0:d_model], w_ref[0])
        for j in range(1, N_DEV):
            acc = acc + _dot_tb(dp_ref[:, j * d_model:(j + 1) * d_model], w_ref[j])
        o_ref[...] = acc + DEEPNORM_ALPHA * dr_ref[...]

    return pl.pallas_call(
        body, name="grad_x", grid=(n_tok // tm,),
        in_specs=[pl.BlockSpec((tm, N_DEV * d_model), lambda i: (i, 0)), VMEM,
                  pl.BlockSpec((tm, d_model), lambda i: (i, 0))],
        out_specs=pl.BlockSpec((tm, d_model), lambda i: (i, 0)),
        out_shape=SDS((n_tok, d_model), F32),
        compiler_params=_params(1),
    )(dproj, w_all, dr)


def _adamw_math(w, g, m, v):
    m = ADAM_B1 * m + (1.0 - ADAM_B1) * g
    v = ADAM_B2 * v + (1.0 - ADAM_B2) * (g * g)
    m_hat = m / (1.0 - ADAM_B1 ** ADAM_STEP)
    v_hat = v / (1.0 - ADAM_B2 ** ADAM_STEP)
    delta = -ADAM_LR * (m_hat / (jnp.sqrt(v_hat) + ADAM_EPS) + ADAM_WD * w)
    return delta, m, v


def _adamw_tiled(w, g, m, v):
    n_rows, n_cols = w.shape
    tr = _tile(n_rows, 256, SUBLANES)

    def body(w_ref, g_ref, m_ref, v_ref, d_ref, mo_ref, vo_ref):
        d_ref[...], mo_ref[...], vo_ref[...] = _adamw_math(w_ref[...], g_ref[...], m_ref[...], v_ref[...])

    tile = pl.BlockSpec((tr, n_cols), lambda i: (i, 0))
    return pl.pallas_call(
        body, name="adamw_w_in", grid=(n_rows // tr,), in_specs=[tile] * 4, out_specs=[tile] * 3,
        out_shape=[SDS(w.shape, F32)] * 3, compiler_params=_params(1),
    )(w, g, m, v)


def _adamw_many(groups):
    n = len(groups)

    def body(*refs):
        ins, outs = refs[:4 * n], refs[4 * n:]
        for p in range(n):
            w_ref, g_ref, m_ref, v_ref = ins[4 * p:4 * p + 4]
            d, m, v = _adamw_math(w_ref[...], g_ref[...], m_ref[...], v_ref[...])
            outs[3 * p][...] = d
            outs[3 * p + 1][...] = m
            outs[3 * p + 2][...] = v

    flat = [a for grp in groups for a in grp]
    out_shape = [SDS(grp[0].shape, F32) for grp in groups for _ in range(3)]
    res = pl.pallas_call(
        body, name="adamw_small", in_specs=[VMEM] * (4 * n), out_specs=[VMEM] * (3 * n),
        out_shape=out_shape, compiler_params=_params(),
    )(*flat)
    return [tuple(res[3 * p:3 * p + 3]) for p in range(n)]


def _as_rows(a):
    return a.reshape(-1, LANES)


def kernel(x, w_in, b_in, conv_w, conv_b, gn_g, gn_b, ln_v_g, ln_v_b, w_spatial, b_spatial, w_pa, w_pb, w_o, b_o, ln_out_g, ln_out_b, loss_target, m_w_in, m_b_in, m_conv_w, m_conv_b, m_gn_g, m_gn_b, m_ln_v_g, m_ln_v_b, m_w_spatial, m_b_spatial, m_w_pa, m_w_pb, m_w_o, m_b_o, m_ln_out_g, m_ln_out_b, v_w_in, v_b_in, v_conv_w, v_conv_b, v_gn_g, v_gn_b, v_ln_v_g, v_ln_v_b, v_w_spatial, v_b_spatial, v_w_pa, v_w_pb, v_w_o, v_b_o, v_ln_out_g, v_ln_out_b):
    n_batch, seq, d_model = x.shape
    n_tok = n_batch * seq
    n_heads = d_model // LANES
    dc = conv_w.shape[1]
    me = 4 * lax.axis_index("x") + 2 * lax.axis_index("y") + lax.axis_index("c")
    row = lambda a: a.reshape(1, d_model)

    x2 = x.reshape(n_tok, d_model)
    target2 = loss_target.reshape(n_tok, d_model)
    b_spatial_t = b_spatial.T

    first = jnp.where(lax.axis_index("c") == 1, 4, 2)
    second = 6 - first
    ag_rel = jnp.stack([0 * first, 0 * first + 1, first, second + 1, second, first + 1, 0 * first + 6, 0 * first + 7])
    ag_blocks = jnp.bitwise_xor(me, ag_rel).astype(jnp.int32)
    proj, xt_bf, w_all, wp_all, cw_all = _proj_all_gather(
        x2, w_in, w_pa, w_pb, w_o, conv_w, b_in.reshape(N_DEV, 1, d_model), ag_blocks)
    wp_full = [wp_all[:, a].reshape(d_model, d_model) for a in range(3)]
    conv_w_full = jnp.pad(cw_all.transpose(1, 0, 2).reshape(CONV_K, d_model), ((0, HALO - CONV_K), (0, 0)))

    h3, h1 = _branch_a_fwd(proj, conv_w_full, row(conv_b), row(gn_g), row(gn_b), seq)
    s = _branch_b_fwd(proj, row(ln_v_g), row(ln_v_b), w_spatial, b_spatial_t, seq)

    dproj, d_h3, d_s, dr, lhs3, rhs3, vec_mid = _mid(
        h3, s, proj, x2, target2, *wp_full, row(b_o), row(ln_out_g), row(ln_out_b))

    dproj, vec_b, d_ws, d_bs_t = _branch_b_bwd(dproj, proj, d_s, row(ln_v_g), row(ln_v_b), w_spatial, b_spatial_t, seq)
    dproj, d_h1, vec_a1 = _branch_a_bwd_norm(dproj, proj, h1, d_h3, row(gn_g), row(gn_b), seq)
    gp_f32, gp_bf = _weight_grads(lhs3, rhs3)
    dproj, vec_a2, d_cw8, g_w_pa, g_w_pb, g_w_o = _branch_a_bwd_conv(
        dproj, proj, d_h1, conv_w_full, gp_bf, gp_f32, seq)

    d_cw = jnp.sum(d_cw8, axis=2)
    pieces = [
        _as_rows(jnp.concatenate([vec_a2[0:2], vec_a1[0:1], vec_b[0:3], vec_mid[3:5]], axis=0)),
        _as_rows(jnp.concatenate([vec_a1[3:4], vec_a1[1:3], vec_b[3:5], vec_mid[2:3], vec_mid[0:2]], axis=0)),
        _as_rows(d_bs_t.T), _as_rows(d_ws), _as_rows(d_cw), _as_rows(vec_mid[5:6]),
    ]
    n_rows = sum(p.shape[0] for p in pieces)
    pad_rows = -n_rows % (N_DEV * SUBLANES)
    small_part = jnp.concatenate(pieces + [jnp.zeros((pad_rows, LANES), F32)], axis=0)

    rs_rel = jnp.stack([0 * first + 7, 0 * first + 6, first + 1, second, second + 1, first, 0 * first + 1, 0 * first])
    rs_blocks = jnp.bitwise_xor(me, rs_rel).astype(jnp.int32)
    g_w_in, small = _grad_w_in_reduce_scatter(xt_bf, dproj, rs_blocks, small_part)
    grad_x = _grad_x(dproj, w_all, dr).reshape(x.shape)

    g_rows = d_model // LANES
    o0 = N_DEV * g_rows
    g_b_in = small[0:o0].reshape(N_DEV * d_model)
    vecs = [small[o0 + a * g_rows:o0 + (a + 1) * g_rows].reshape(d_model) for a in range(8)]
    g_conv_b, g_gn_g, g_gn_b, g_ln_v_g, g_ln_v_b, g_b_o, g_ln_out_g, g_ln_out_b = vecs
    o1 = o0 + 8 * g_rows
    g_b_spatial = small[o1:o1 + n_heads].reshape(n_heads, LANES)
    o2 = o1 + n_heads
    g_w_spatial = small[o2:o2 + n_heads * LANES].reshape(n_heads, LANES, LANES)
    o3 = o2 + n_heads * LANES
    g_cw_full = small[o3:o3 + n_heads * HALO].reshape(n_heads, HALO, LANES).transpose(1, 0, 2).reshape(HALO, d_model)
    g_conv_w = lax.dynamic_slice(g_cw_full, (0, me * dc), (CONV_K, dc))
    o4 = o3 + n_heads * HALO
    loss = jnp.sum(small[o4:o4 + g_rows]) * (0.5 / d_model)

    d_w_in, nm_w_in, nv_w_in = _adamw_tiled(w_in, g_w_in, m_w_in, v_w_in)
    two_d = lambda a: a.reshape(-1, a.shape[-1]) if a.ndim != 1 else (
        a.reshape(-1, LANES) if a.shape[0] % LANES == 0 else a.reshape(1, -1))
    names = ["b_in", "conv_w", "conv_b", "gn_g", "gn_b", "ln_v_g", "ln_v_b", "w_spatial", "b_spatial",
             "w_pa", "w_pb", "w_o", "b_o", "ln_out_g", "ln_out_b"]
    ws = dict(b_in=b_in, conv_w=conv_w, conv_b=conv_b, gn_g=gn_g, gn_b=gn_b, ln_v_g=ln_v_g, ln_v_b=ln_v_b,
              w_spatial=w_spatial, b_spatial=b_spatial, w_pa=w_pa, w_pb=w_pb, w_o=w_o, b_o=b_o,
              ln_out_g=ln_out_g, ln_out_b=ln_out_b)
    gs = dict(b_in=g_b_in, conv_w=g_conv_w, conv_b=g_conv_b, gn_g=g_gn_g, gn_b=g_gn_b, ln_v_g=g_ln_v_g,
              ln_v_b=g_ln_v_b, w_spatial=g_w_spatial, b_spatial=g_b_spatial, w_pa=g_w_pa, w_pb=g_w_pb,
              w_o=g_w_o, b_o=g_b_o, ln_out_g=g_ln_out_g, ln_out_b=g_ln_out_b)
    ms = dict(b_in=m_b_in, conv_w=m_conv_w, conv_b=m_conv_b, gn_g=m_gn_g, gn_b=m_gn_b, ln_v_g=m_ln_v_g,
              ln_v_b=m_ln_v_b, w_spatial=m_w_spatial, b_spatial=m_b_spatial, w_pa=m_w_pa, w_pb=m_w_pb,
              w_o=m_w_o, b_o=m_b_o, ln_out_g=m_ln_out_g, ln_out_b=m_ln_out_b)
    vs = dict(b_in=v_b_in, conv_w=v_conv_w, conv_b=v_conv_b, gn_g=v_gn_g, gn_b=v_gn_b, ln_v_g=v_ln_v_g,
              ln_v_b=v_ln_v_b, w_spatial=v_w_spatial, b_spatial=v_b_spatial, w_pa=v_w_pa, w_pb=v_w_pb,
              w_o=v_w_o, b_o=v_b_o, ln_out_g=v_ln_out_g, ln_out_b=v_ln_out_b)
    upd = _adamw_many([tuple(two_d(d[n]) for d in (ws, gs, ms, vs)) for n in names])
    delta = {n: u[0].reshape(ws[n].shape) for n, u in zip(names, upd)}
    new_m = {n: u[1].reshape(ws[n].shape) for n, u in zip(names, upd)}
    new_v = {n: u[2].reshape(ws[n].shape) for n, u in zip(names, upd)}
    gs["w_in"], delta["w_in"], new_m["w_in"], new_v["w_in"] = g_w_in, d_w_in, nm_w_in, nv_w_in

    order = ["w_in"] + names
    return (loss, grad_x, *[gs[n] for n in order], *[delta[n] for n in order],
            *[new_m[n] for n in order], *[new_v[n] for n in order])
```

```python
import jax
import jax.numpy as jnp
from jax import lax
from jax.experimental import pallas as pl
from jax.experimental.pallas import tpu as pltpu

F32 = jnp.float32
BF16 = jnp.bfloat16
SDS = jax.ShapeDtypeStruct

N_DEV = 8
LANES = 128
SUBLANES = 8
CONV_K = 31
HALO = 32
LN_EPS = 1e-5
DEEPNORM_ALPHA = 2.0 ** 0.25
ADAM_LR, ADAM_B1, ADAM_B2, ADAM_EPS, ADAM_WD, ADAM_STEP = 0.001, 0.9, 0.999, 1e-08, 0.01, 10
GELU_C = 0.7978845608028654
GELU_A = 0.044715
VMEM_LIMIT = 56 * 1024 * 1024
MESH = pl.DeviceIdType.MESH
ANY = pl.BlockSpec(memory_space=pl.ANY)
VMEM = pl.BlockSpec(memory_space=pltpu.VMEM)


def _params(n_grid=0):
    sem = ("arbitrary",) * n_grid if n_grid else None
    return pltpu.CompilerParams(dimension_semantics=sem, vmem_limit_bytes=VMEM_LIMIT)


def _tile(n, pref, mult):
    t = min(n, pref)
    while n % t or t % mult:
        t -= 1
    return t


def _colsum(v):
    return jnp.sum(v, axis=0, keepdims=True)


def _sigmoid(v):
    return jax.nn.sigmoid(v)


def _silu_and_grad(v):
    s = _sigmoid(v)
    return v * s, s * (1.0 + v * (1.0 - s))


def _gelu_and_grad(v):
    inner = GELU_C * (v + GELU_A * v * v * v)
    th = jnp.tanh(inner)
    val = 0.5 * v * (1.0 + th)
    grad = 0.5 * (1.0 + th) + 0.5 * v * (1.0 - th * th) * GELU_C * (1.0 + 3.0 * GELU_A * v * v)
    return val, grad


def _tril_mask():
    r = lax.broadcasted_iota(jnp.int32, (LANES, LANES), 0)
    c = lax.broadcasted_iota(jnp.int32, (LANES, LANES), 1)
    return c <= r


def _dot(a, b):
    return jnp.dot(a, b, preferred_element_type=F32)


def _dot_tb(a, b):
    return lax.dot_general(a, b, (((1,), (1,)), ((), ())), preferred_element_type=F32)


def _dot_ta(a, b):
    return lax.dot_general(a, b, (((0,), (0,)), ((), ())), preferred_element_type=F32)


def _mesh_pos():
    return lax.axis_index("x"), lax.axis_index("y"), lax.axis_index("c")


def _block_of(pos):
    return 4 * pos[0] + 2 * pos[1] + pos[2]


def _peers():
    x, y, c = _mesh_pos()
    out = []
    for k in range(1, N_DEV):
        pos = (1 - x if k & 4 else x, 1 - y if k & 2 else y, 1 - c if k & 1 else c)
        out.append((pos, _block_of(pos)))
    return out


def _proj_all_gather(x2, w_in, w_pa, w_pb, w_o, conv_w, b_in3, blk_order):
    n_tok, d_model = x2.shape
    r8 = w_pa.shape[0]
    kc, dc = conv_w.shape
    tm = _tile(n_tok, 1024, LANES)
    n_t = n_tok // tm
    n_arr = 3

    def body(ord_ref, x_ref, b_ref, win_ref, wpa_ref, wpb_ref, wo_ref, cw_ref,
             proj_ref, xt_ref, wall_ref, wp_ref, cwall_ref,
             wbuf, xbuf, st_p, send_sems, recv_sems, local_sems, wall_sems):
        s = pl.program_id(0)
        t = pl.program_id(1)
        x, y, c = _mesh_pos()
        me = (x, y, c)
        sibling = (x, y, 1 - c)
        n1 = (jnp.bitwise_xor(x, c), jnp.bitwise_xor(y, 1 - c))
        n2 = (jnp.bitwise_xor(x, 1 - c), jnp.bitwise_xor(y, c))
        dg = (1 - x, 1 - y)
        outs = [wbuf, wp_ref, cwall_ref]
        srcs = [None, st_p, cw_ref]
        consumed = [me, sibling, (*n1, c), (*n2, 1 - c), (*n2, c), (*n1, 1 - c), (*dg, c), (*dg, 1 - c)]
        leaves = [(me, sibling), (me, (*n1, c)), (me, (*n2, c)), ((*n1, c), (*n2, c)),
                  ((*n1, c), sibling), ((*n2, c), sibling), ((*dg, c), sibling)]
        lands = [sibling, (*n1, c), (*n2, c), (*dg, c), (*n2, 1 - c), (*n1, 1 - c), (*dg, 1 - c)]

        def slot(o, pos):
            return o.at[:, _block_of(pos)] if o is wp_ref else o.at[_block_of(pos)]

        def copy(a, k, block, to, src=None):
            o = outs[a]
            return pltpu.make_async_remote_copy(
                src_ref=slot(o, block) if src is None else src, dst_ref=slot(o, block),
                send_sem=send_sems.at[a, k], recv_sem=recv_sems.at[a, k],
                device_id=to, device_id_type=MESH)

        def send(a, k):
            block, to = leaves[k]
            return copy(a, k, block, to, src=srcs[a] if k < 3 else None)

        def recv(a, k):
            return copy(a, k, lands[k], me)

        def local_copies():
            return [pltpu.make_async_copy(srcs[a], slot(outs[a], me), local_sems.at[a]) for a in (1, 2)]

        def to_hbm(step):
            return pltpu.make_async_copy(slot(wbuf, consumed[step]), slot(wall_ref, consumed[step]),
                                         wall_sems.at[step])

        def at_step(step):
            return pl.when(jnp.logical_and(s == step, t == 0))

        @at_step(0)
        def _():
            slot(wbuf, me)[...] = win_ref[...].astype(BF16)
            st_p[0] = wpa_ref[...].astype(BF16)
            st_p[1] = wpb_ref[...].astype(BF16)
            st_p[2] = wo_ref[...].astype(BF16)
            for a in range(n_arr):
                send(a, 0).start()
                send(a, 1).start()
            for cp in local_copies():
                cp.start()
            to_hbm(0).start()

        @at_step(1)
        def _():
            recv(0, 0).wait_recv()
            to_hbm(1).start()

        for rnd in range(3):
            @at_step(2 + 2 * rnd)
            def _(rnd=rnd):
                if rnd == 0:
                    for a in range(n_arr):
                        send(a, 2).start()
                recv(0, 1 + rnd).wait_recv()
                if rnd == 0:
                    send(0, 3).start()
                send(0, 4 + rnd).start()
                to_hbm(2 + 2 * rnd).start()

            @at_step(3 + 2 * rnd)
            def _(rnd=rnd):
                for a in (1, 2):
                    recv(a, 1 + rnd).wait_recv()
                    if rnd == 0:
                        send(a, 3).start()
                    send(a, 4 + rnd).start()
                recv(0, 4 + rnd).wait_recv()
                to_hbm(3 + 2 * rnd).start()

        rows = pl.ds(pl.multiple_of(t * tm, tm), tm)

        @pl.when(s == 0)
        def _():
            xb = x_ref[...].astype(BF16)
            xbuf[rows, :] = xb
            xt_ref[...] = xb.T

        proj_ref[...] = _dot(xbuf[rows, :], wbuf[ord_ref[s]]) + b_ref[...]

        @pl.when(jnp.logical_and(s == N_DEV - 1, t == n_t - 1))
        def _():
            for a in (1, 2):
                for k in (0, 4, 5, 6):
                    recv(a, k).wait_recv()
            for a in range(n_arr):
                for k in range(7):
                    send(a, k).wait_send()
            for cp in local_copies() + [to_hbm(step) for step in range(N_DEV)]:
                cp.wait()

    grid_spec = pltpu.PrefetchScalarGridSpec(
        num_scalar_prefetch=1, grid=(N_DEV, n_t),
        in_specs=[pl.BlockSpec((tm, d_model), lambda s, t, o: (jnp.where(s == 0, t, n_t - 1), 0)),
                  pl.BlockSpec((None, 1, d_model), lambda s, t, o: (o[s], 0, 0)),
                  VMEM, VMEM, VMEM, VMEM, VMEM],
        out_specs=[pl.BlockSpec((tm, d_model), lambda s, t, o: (t, o[s])),
                   pl.BlockSpec((d_model, tm), lambda s, t, o: (0, jnp.where(s == 0, t, n_t))),
                   ANY, ANY, ANY],
        scratch_shapes=[pltpu.VMEM((N_DEV, d_model, d_model), BF16), pltpu.VMEM((n_tok, d_model), BF16),
                        pltpu.VMEM((3, r8, d_model), BF16),
                        pltpu.SemaphoreType.DMA((n_arr, 7)), pltpu.SemaphoreType.DMA((n_arr, 7)),
                        pltpu.SemaphoreType.DMA((3,)), pltpu.SemaphoreType.DMA((N_DEV,))])
    return pl.pallas_call(
        body, name="proj_all_gather", grid_spec=grid_spec,
        out_shape=[SDS((n_tok, N_DEV * d_model), F32), SDS((d_model, (n_t + 1) * tm), BF16),
                   SDS((N_DEV, d_model, d_model), BF16), SDS((3, N_DEV, r8, d_model), BF16),
                   SDS((N_DEV, kc, dc), F32)],
        compiler_params=_params(2),
    )(blk_order, x2, b_in3, w_in, w_pa, w_pb, w_o, conv_w)


def _conv_rows(ta):
    return _tile(ta, 64, SUBLANES)


def _branch_a_fwd(proj, conv_w_full, conv_b, gn_g, gn_b, seq):
    n_tok = proj.shape[0]
    d_model = conv_b.shape[1]
    ta = _tile(seq, 256, HALO)
    per_seq = seq // ta
    rc = _conv_rows(ta)

    def body(av_ref, ag_ref, gt_ref, avh_ref, agh_ref, cw_ref, cb_ref, gg_ref, gb_ref,
             h3_ref, h1_ref, ext):
        keep = jnp.where(pl.program_id(0) % per_seq == 0, 0.0, 1.0)

        def group(g, carry):
            sl = pl.ds(pl.multiple_of(g * LANES, LANES), LANES)
            ext[0:HALO, :] = avh_ref[:, sl] * _sigmoid(agh_ref[:, sl]) * keep
            ext[HALO:HALO + ta, :] = av_ref[:, sl] * _sigmoid(ag_ref[:, sl])
            for r0 in range(0, ta, rc):
                acc = jnp.broadcast_to(cb_ref[:, sl], (rc, LANES))
                for k in range(CONV_K):
                    acc = acc + ext[pl.ds(r0 + HALO - (CONV_K - 1) + k, rc), :] * cw_ref[k:k + 1, sl]
                h1_ref[pl.ds(r0, rc), sl] = acc
            h1 = h1_ref[:, sl]
            mu = jnp.mean(h1, axis=-1, keepdims=True)
            dlt = h1 - mu
            var = jnp.mean(dlt * dlt, axis=-1, keepdims=True)
            h2 = dlt * lax.rsqrt(var + LN_EPS) * gg_ref[:, sl] + gb_ref[:, sl]
            gate = gt_ref[:, sl]
            h3_ref[:, sl] = (h2 * _sigmoid(h2) * gate * _sigmoid(gate)).astype(BF16)
            return carry

        lax.fori_loop(0, d_model // LANES, group, 0, unroll=True)

    blk = lambda j: pl.BlockSpec((ta, d_model), lambda i: (i, j))
    halo = lambda j: pl.BlockSpec((HALO, d_model), lambda i: (jnp.maximum(i * (ta // HALO) - 1, 0), j))
    row = pl.BlockSpec((1, d_model), lambda i: (0, 0))
    return pl.pallas_call(
        body, name="branch_a_fwd", grid=(n_tok // ta,),
        in_specs=[blk(0), blk(1), blk(2), halo(0), halo(1),
                  pl.BlockSpec((HALO, d_model), lambda i: (0, 0)), row, row, row],
        out_specs=[pl.BlockSpec((ta, d_model), lambda i: (i, 0))] * 2,
        out_shape=[SDS((n_tok, d_model), BF16), SDS((n_tok, d_model), F32)],
        scratch_shapes=[pltpu.VMEM((HALO + ta, LANES), F32)],
        compiler_params=_params(1),
    )(proj, proj, proj, proj, proj, conv_w_full, conv_b, gn_g, gn_b)


def _branch_b_fwd(proj, ln_g, ln_b, w_spatial, b_spatial_t, seq):
    n_tok = proj.shape[0]
    d_model = ln_g.shape[1]
    n_heads = d_model // LANES
    tb = _tile(seq, 256, LANES)

    def body(u_ref, v_ref, bg_ref, lg_ref, lb_ref, ws_ref, bs_ref, s_ref, vn_buf):
        v, _ = _gelu_and_grad(v_ref[...])
        mu = jnp.mean(v, axis=-1, keepdims=True)
        dlt = v - mu
        var = jnp.mean(dlt * dlt, axis=-1, keepdims=True)
        vn_buf[...] = (dlt * lax.rsqrt(var + LN_EPS) * lg_ref[...] + lb_ref[...]).astype(BF16)
        tril = _tril_mask()
        for h in range(n_heads):
            cols = slice(h * LANES, (h + 1) * LANES)
            w_h = jnp.where(tril, ws_ref[h], 0.0).astype(BF16)
            bias = bs_ref[:, h:h + 1]
            for ch in range(tb // LANES):
                rows = slice(ch * LANES, (ch + 1) * LANES)
                mix = _dot(w_h, vn_buf[rows, cols]) + bias
                u, _ = _gelu_and_grad(u_ref[rows, cols])
                gate = bg_ref[rows, cols]
                s_ref[rows, cols] = (u * mix * gate * _sigmoid(gate)).astype(BF16)

    blk = lambda j: pl.BlockSpec((tb, d_model), lambda i: (i, j))
    row = pl.BlockSpec((1, d_model), lambda i: (0, 0))
    return pl.pallas_call(
        body, name="branch_b_fwd", grid=(n_tok // tb,),
        in_specs=[blk(3), blk(4), blk(5), row, row,
                  pl.BlockSpec((n_heads, LANES, LANES), lambda i: (0, 0, 0)),
                  pl.BlockSpec((LANES, n_heads), lambda i: (0, 0))],
        out_specs=pl.BlockSpec((tb, d_model), lambda i: (i, 0)),
        out_shape=SDS((n_tok, d_model), BF16),
        scratch_shapes=[pltpu.VMEM((tb, d_model), BF16)],
        compiler_params=_params(1),
    )(proj, proj, proj, ln_g, ln_b, w_spatial, b_spatial_t)


MID_ROWS = 8


def _mid(h3, s, proj, x2, target, wp_full, b_o, lo_g, lo_b):
    n_tok, d_model = x2.shape
    tm = _tile(n_tok, 256, 16)

    def body(h3_ref, s_ref, ma_ref, mb_ref, x_ref, t_ref, wpa_ref, wpb_ref, wo_ref, bo_ref,
             lg_ref, lb_ref, dproj_ref, dh3_ref, ds_ref, dr_ref, lhs3_ref, rhs3_ref, vec_ref):
        @pl.when(pl.program_id(0) == 0)
        def _():
            vec_ref[...] = jnp.zeros_like(vec_ref)

        h3 = h3_ref[...]
        s = s_ref[...]
        ya = _dot(h3, wpa_ref[...])
        yb = _dot(s, wpb_ref[...])
        ga = _sigmoid(ma_ref[...])
        gb = _sigmoid(mb_ref[...])
        mixed = (ga * ya + gb * yb).astype(BF16)
        lhs3_ref[0] = h3
        lhs3_ref[1] = s
        lhs3_ref[2] = mixed
        r = DEEPNORM_ALPHA * x_ref[...] + _dot(mixed, wo_ref[...]) + bo_ref[...]
        mu = jnp.mean(r, axis=-1, keepdims=True)
        dlt = r - mu
        rstd = lax.rsqrt(jnp.mean(dlt * dlt, axis=-1, keepdims=True) + LN_EPS)
        rhat = dlt * rstd
        diff = rhat * lg_ref[...] + lb_ref[...] - t_ref[...]
        dy = diff * (1.0 / d_model)
        vec_ref[0:1, :] += _colsum(dy * rhat)
        vec_ref[1:2, :] += _colsum(dy)
        vec_ref[5:6, :] += _colsum(diff * diff)
        drh = dy * lg_ref[...]
        dr = rstd * (drh - jnp.mean(drh, axis=-1, keepdims=True)
                     - rhat * jnp.mean(drh * rhat, axis=-1, keepdims=True))
        vec_ref[2:3, :] += _colsum(dr)
        dr_ref[...] = dr
        drb = dr.astype(BF16)
        rhs3_ref[2] = drb
        dmixed = _dot_tb(drb, wo_ref[...])
        dma = dmixed * ya * ga * (1.0 - ga)
        dmb = dmixed * yb * gb * (1.0 - gb)
        vec_ref[3:4, :] += _colsum(dma)
        vec_ref[4:5, :] += _colsum(dmb)
        dproj_ref[:, 0:d_model] = dma.astype(BF16)
        dproj_ref[:, d_model:2 * d_model] = dmb.astype(BF16)
        dya = (dmixed * ga).astype(BF16)
        dyb = (dmixed * gb).astype(BF16)
        rhs3_ref[0] = dya
        rhs3_ref[1] = dyb
        dh3_ref[...] = _dot_tb(dya, wpa_ref[...])
        ds_ref[...] = _dot_tb(dyb, wpb_ref[...])

    tile = pl.BlockSpec((tm, d_model), lambda i: (i, 0))
    full = lambda a: pl.BlockSpec((None, d_model, d_model), lambda i: (a, 0, 0))
    row = pl.BlockSpec((1, d_model), lambda i: (0, 0))
    stack = pl.BlockSpec((3, tm, d_model), lambda i: (0, i, 0))
    bf3 = SDS((3, n_tok, d_model), BF16)
    f32 = SDS((n_tok, d_model), F32)
    return pl.pallas_call(
        body, name="mid", grid=(n_tok // tm,),
        in_specs=[tile, tile, pl.BlockSpec((tm, d_model), lambda i: (i, 6)),
                  pl.BlockSpec((tm, d_model), lambda i: (i, 7)), tile, tile, full(0), full(1), full(2),
                  row, row, row],
        out_specs=[pl.BlockSpec((tm, 2 * d_model), lambda i: (i, 3)), tile, tile, tile, stack, stack,
                   pl.BlockSpec((MID_ROWS, d_model), lambda i: (0, 0))],
        out_shape=[SDS((n_tok, N_DEV * d_model), BF16), f32, f32, f32, bf3, bf3,
                   SDS((MID_ROWS, d_model), F32)],
        compiler_params=_params(1),
    )(h3, s, proj, proj, x2, target, wp_full, wp_full, wp_full, b_o, lo_g, lo_b)


B_ROWS = 8


def _branch_b_bwd(dproj, proj, d_s, ln_g, ln_b, w_spatial, b_spatial_t, seq):
    n_tok = proj.shape[0]
    d_model = ln_g.shape[1]
    n_heads = d_model // LANES
    tb = _tile(seq, 256, LANES)

    def body(dproj_in, u_ref, v_ref, bg_ref, ds_ref, lg_ref, lb_ref, ws_ref, bs_ref,
             dproj_ref, vec_ref, dws_ref, dbs_ref, vn_buf, dv_buf):
        del dproj_in

        @pl.when(pl.program_id(0) == 0)
        def _():
            vec_ref[...] = jnp.zeros_like(vec_ref)
            dws_ref[...] = jnp.zeros_like(dws_ref)
            dbs_ref[...] = jnp.zeros_like(dbs_ref)

        v, dgelu_v = _gelu_and_grad(v_ref[...])
        mu = jnp.mean(v, axis=-1, keepdims=True)
        dlt = v - mu
        rstd = lax.rsqrt(jnp.mean(dlt * dlt, axis=-1, keepdims=True) + LN_EPS)
        vhat = dlt * rstd
        vn_buf[...] = (vhat * lg_ref[...] + lb_ref[...]).astype(BF16)
        tril = _tril_mask()
        for h in range(n_heads):
            cols = slice(h * LANES, (h + 1) * LANES)
            w_h = jnp.where(tril, ws_ref[h], 0.0).astype(BF16)
            bias = bs_ref[:, h:h + 1]
            for ch in range(tb // LANES):
                rows = slice(ch * LANES, (ch + 1) * LANES)
                vn = vn_buf[rows, cols]
                mix = _dot(w_h, vn) + bias
                u, dgelu_u = _gelu_and_grad(u_ref[rows, cols])
                sg, dsilu = _silu_and_grad(bg_ref[rows, cols])
                dsv = ds_ref[rows, cols]
                du = dsv * mix * sg * dgelu_u
                dbg = dsv * u * mix * dsilu
                dmix = dsv * u * sg
                dmix_bf = dmix.astype(BF16)
                dproj_ref[rows, cols] = du.astype(BF16)
                dproj_ref[rows, 2 * d_model + h * LANES:2 * d_model + (h + 1) * LANES] = dbg.astype(BF16)
                vec_ref[0:1, cols] += _colsum(du)
                vec_ref[2:3, cols] += _colsum(dbg)
                dbs_ref[:, h:h + 1] += jnp.sum(dmix, axis=1, keepdims=True)
                dws_ref[h] += jnp.where(tril, _dot_tb(dmix_bf, vn), 0.0)
                dv_buf[rows, cols] = _dot_ta(w_h, dmix_bf)
        dvn = dv_buf[...]
        vec_ref[3:4, :] += _colsum(dvn * vhat)
        vec_ref[4:5, :] += _colsum(dvn)
        dvh = dvn * lg_ref[...]
        dv = rstd * (dvh - jnp.mean(dvh, axis=-1, keepdims=True)
                     - vhat * jnp.mean(dvh * vhat, axis=-1, keepdims=True)) * dgelu_v
        vec_ref[1:2, :] += _colsum(dv)
        dproj_ref[:, d_model:2 * d_model] = dv.astype(BF16)

    blk = lambda j: pl.BlockSpec((tb, d_model), lambda i: (i, j))
    row = pl.BlockSpec((1, d_model), lambda i: (0, 0))
    return pl.pallas_call(
        body, name="branch_b_bwd", grid=(n_tok // tb,),
        in_specs=[ANY, blk(3), blk(4), blk(5), pl.BlockSpec((tb, d_model), lambda i: (i, 0)), row, row,
                  pl.BlockSpec((n_heads, LANES, LANES), lambda i: (0, 0, 0)),
                  pl.BlockSpec((LANES, n_heads), lambda i: (0, 0))],
        out_specs=[pl.BlockSpec((tb, 3 * d_model), lambda i: (i, 1)),
                   pl.BlockSpec((B_ROWS, d_model), lambda i: (0, 0)),
                   pl.BlockSpec((n_heads, LANES, LANES), lambda i: (0, 0, 0)),
                   pl.BlockSpec((LANES, n_heads), lambda i: (0, 0))],
        out_shape=[SDS(dproj.shape, BF16), SDS((B_ROWS, d_model), F32),
                   SDS((n_heads, LANES, LANES), F32), SDS((LANES, n_heads), F32)],
        scratch_shapes=[pltpu.VMEM((tb, d_model), BF16), pltpu.VMEM((tb, d_model), F32)],
        input_output_aliases={0: 0},
        compiler_params=_params(1),
    )(dproj, proj, proj, proj, d_s, ln_g, ln_b, w_spatial, b_spatial_t)


A1_ROWS = 8


def _branch_a_bwd_norm(dproj, proj, h1, d_h3, gn_g, gn_b, seq):
    n_tok = proj.shape[0]
    d_model = gn_g.shape[1]
    ta = _tile(seq, 256, 16)

    def body(dproj_in, gt_ref, h1_ref, dh3_ref, gg_ref, gb_ref, dproj_ref, dh1_ref, vec_ref):
        del dproj_in

        @pl.when(pl.program_id(0) == 0)
        def _():
            vec_ref[...] = jnp.zeros_like(vec_ref)

        def group(g, carry):
            sl = pl.ds(pl.multiple_of(g * LANES, LANES), LANES)
            h1 = h1_ref[:, sl]
            mu = jnp.mean(h1, axis=-1, keepdims=True)
            dlt = h1 - mu
            rstd = lax.rsqrt(jnp.mean(dlt * dlt, axis=-1, keepdims=True) + LN_EPS)
            nrm = dlt * rstd
            sw, dsw = _silu_and_grad(nrm * gg_ref[:, sl] + gb_ref[:, sl])
            sg, dsg = _silu_and_grad(gt_ref[:, sl])
            dh3 = dh3_ref[:, sl]
            dgate = dh3 * sw * dsg
            dproj_ref[:, sl] = dgate.astype(BF16)
            vec_ref[0:1, sl] += _colsum(dgate)
            dh2 = dh3 * sg * dsw
            vec_ref[1:2, sl] += _colsum(dh2 * nrm)
            vec_ref[2:3, sl] += _colsum(dh2)
            dn = dh2 * gg_ref[:, sl]
            dh1 = rstd * (dn - jnp.mean(dn, axis=-1, keepdims=True)
                          - nrm * jnp.mean(dn * nrm, axis=-1, keepdims=True))
            vec_ref[3:4, sl] += _colsum(dh1)
            dh1_ref[:, sl] = dh1
            return carry

        lax.fori_loop(0, d_model // LANES, group, 0, unroll=True)

    tile = pl.BlockSpec((ta, d_model), lambda i: (i, 0))
    row = pl.BlockSpec((1, d_model), lambda i: (0, 0))
    return pl.pallas_call(
        body, name="branch_a_bwd_norm", grid=(n_tok // ta,),
        in_specs=[ANY, pl.BlockSpec((ta, d_model), lambda i: (i, 2)), tile, tile, row, row],
        out_specs=[pl.BlockSpec((ta, d_model), lambda i: (i, 2)), tile,
                   pl.BlockSpec((A1_ROWS, d_model), lambda i: (0, 0))],
        out_shape=[SDS(dproj.shape, BF16), SDS((n_tok, d_model), F32), SDS((A1_ROWS, d_model), F32)],
        input_output_aliases={0: 0},
        compiler_params=_params(1),
    )(dproj, proj, h1, d_h3, gn_g, gn_b)


A2_ROWS = 8


def _branch_a_bwd_conv(dproj, proj, d_h1, conv_w_full, gp_bf, gp_f32, seq):
    n_tok = proj.shape[0]
    d_model = conv_w_full.shape[1]
    n_groups = d_model // LANES
    ta = _tile(seq, 256, HALO)
    n_tiles = n_tok // ta
    per_seq = seq // ta
    rc = _conv_rows(ta)
    last_halo = n_tok // HALO - 1
    r8 = d_model // N_DEV
    prow = _tile(r8, 32, 16)

    def body(dproj_in, av_ref, ag_ref, avh_ref, agh_ref, dh1_ref, dh1h_ref, cw_ref, gp_bf_ref, gp_f32_ref,
             dproj_ref, vec_ref, dcw_ref, opa_ref, opb_ref, opo_ref,
             ext_h0, ext_d, rbuf, own, send_sems, recv_sems, local_sems):
        del dproj_in
        i = pl.program_id(0)
        x, y, c = _mesh_pos()
        me = _block_of((x, y, c))
        peers = _peers()

        def sends():
            return [pltpu.make_async_remote_copy(
                src_ref=gp_bf_ref.at[a, pl.ds(pl.multiple_of(blk * r8, 16), r8), :], dst_ref=rbuf.at[k, a],
                send_sem=send_sems.at[a, k], recv_sem=recv_sems.at[a, k], device_id=pos, device_id_type=MESH)
                for k, (pos, blk) in enumerate(peers) for a in range(3)]

        def own_rows():
            return [pltpu.make_async_copy(gp_f32_ref.at[a, pl.ds(pl.multiple_of(me * r8, 8), r8), :],
                                          own.at[a], local_sems.at[a]) for a in range(3)]

        @pl.when(i == 0)
        def _():
            vec_ref[...] = jnp.zeros_like(vec_ref)
            dcw_ref[...] = jnp.zeros_like(dcw_ref)
            for cp in sends() + own_rows():
                cp.start()

        keep_past = jnp.where(i % per_seq == 0, 0.0, 1.0)
        keep_next = jnp.where(i % per_seq == per_seq - 1, 0.0, 1.0)

        def group(g, carry):
            sl = pl.ds(pl.multiple_of(g * LANES, LANES), LANES)
            av = av_ref[:, sl]
            sig = _sigmoid(ag_ref[:, sl])
            ext_h0[0:HALO, :] = avh_ref[:, sl] * _sigmoid(agh_ref[:, sl]) * keep_past
            ext_h0[HALO:HALO + ta, :] = av * sig
            ext_d[0:ta, :] = dh1_ref[:, sl]
            ext_d[ta:ta + HALO, :] = dh1h_ref[:, sl] * keep_next
            for r0 in range(0, ta, rc):
                dh1 = ext_d[pl.ds(r0, rc), :]
                acc = jnp.zeros((rc, LANES), F32)
                for k in range(CONV_K):
                    acc = acc + ext_d[pl.ds(r0 + CONV_K - 1 - k, rc), :] * cw_ref[k:k + 1, sl]
                    prod = dh1 * ext_h0[pl.ds(r0 + HALO - (CONV_K - 1) + k, rc), :]
                    dcw_ref[g, k] += jnp.sum(prod.reshape(rc // SUBLANES, SUBLANES, LANES), axis=0)
                rows = pl.ds(r0, rc)
                sig_r = sig[r0:r0 + rc]
                dav = acc * sig_r
                dag = acc * av[r0:r0 + rc] * sig_r * (1.0 - sig_r)
                dproj_ref[rows, sl] = dav.astype(BF16)
                dproj_ref[rows, pl.ds(pl.multiple_of(d_model + g * LANES, LANES), LANES)] = dag.astype(BF16)
                vec_ref[0:1, sl] += _colsum(dav)
                vec_ref[1:2, sl] += _colsum(dag)
            return carry

        lax.fori_loop(0, n_groups, group, 0, unroll=True)

        @pl.when(i == n_tiles - 1)
        def _():
            for cp in own_rows():
                cp.wait()
            for cp in sends():
                cp.wait_recv()
            for a, o in enumerate([opa_ref, opb_ref, opo_ref]):
                for q in range(r8 // prow):
                    r = pl.ds(q * prow, prow)
                    tot = own[a, r, :]
                    for k in range(N_DEV - 1):
                        tot = tot + rbuf[k, a, r, :].astype(F32)
                    o[r, :] = tot
            for cp in sends():
                cp.wait_send()

    blk = lambda j: pl.BlockSpec((ta, d_model), lambda i: (i, j))
    halo = lambda j: pl.BlockSpec((HALO, d_model), lambda i: (jnp.maximum(i * (ta // HALO) - 1, 0), j))
    shard = pl.BlockSpec((r8, d_model), lambda i: (0, 0))
    return pl.pallas_call(
        body, name="branch_a_bwd_conv", grid=(n_tiles,),
        in_specs=[ANY, blk(0), blk(1), halo(0), halo(1), pl.BlockSpec((ta, d_model), lambda i: (i, 0)),
                  pl.BlockSpec((HALO, d_model), lambda i: (jnp.minimum((i + 1) * (ta // HALO), last_halo), 0)),
                  pl.BlockSpec((HALO, d_model), lambda i: (0, 0)), ANY, ANY],
        out_specs=[pl.BlockSpec((ta, 2 * d_model), lambda i: (i, 0)),
                   pl.BlockSpec((A2_ROWS, d_model), lambda i: (0, 0)),
                   pl.BlockSpec((n_groups, HALO, SUBLANES, LANES), lambda i: (0, 0, 0, 0)),
                   shard, shard, shard],
        out_shape=[SDS(dproj.shape, BF16), SDS((A2_ROWS, d_model), F32),
                   SDS((n_groups, HALO, SUBLANES, LANES), F32)] + [SDS((r8, d_model), F32)] * 3,
        scratch_shapes=[pltpu.VMEM((HALO + ta, LANES), F32), pltpu.VMEM((ta + HALO, LANES), F32),
                        pltpu.VMEM((N_DEV - 1, 3, r8, d_model), BF16), pltpu.VMEM((3, r8, d_model), F32),
                        pltpu.SemaphoreType.DMA((3, 7)), pltpu.SemaphoreType.DMA((3, 7)),
                        pltpu.SemaphoreType.DMA((3,))],
        input_output_aliases={0: 0},
        compiler_params=_params(1),
    )(dproj, proj, proj, proj, proj, d_h1, d_h1, conv_w_full, gp_bf, gp_f32)


def _weight_grads(lhs3, rhs3):
    n_mat, n_tok, d_model = lhs3.shape
    tk = _tile(n_tok, 1024, 16)
    n_k = n_tok // tk

    def body(a_ref, g_ref, o_ref, ob_ref):
        part = _dot_ta(a_ref[...], g_ref[...])

        @pl.when(pl.program_id(1) == 0)
        def _():
            o_ref[...] = part

        @pl.when(pl.program_id(1) != 0)
        def _():
            o_ref[...] += part

        @pl.when(pl.program_id(1) == n_k - 1)
        def _():
            ob_ref[...] = o_ref[...].astype(BF16)

    tile = pl.BlockSpec((None, tk, d_model), lambda a, i: (a, i, 0))
    out = pl.BlockSpec((None, d_model, d_model), lambda a, i: (a, 0, 0))
    return pl.pallas_call(
        body, name="grad_w_pa_pb_o", grid=(n_mat, n_k), in_specs=[tile, tile], out_specs=[out, out],
        out_shape=[SDS((n_mat, d_model, d_model), F32), SDS((n_mat, d_model, d_model), BF16)],
        compiler_params=_params(2),
    )(lhs3, rhs3)


def _grad_w_in_reduce_scatter(xt_bf, dproj, blk_order, small_part):
    n_tok = dproj.shape[0]
    d_model = xt_bf.shape[0]
    dh = d_model // 2
    n_units = 2 * N_DEV
    rsl = small_part.shape[0] // N_DEV

    def body(ord_ref, a_ref, g_ref, p_ref, o_ref, small_ref, acc, fb, sbuf, gbuf, tbuf, rfin, rbuf_s, red,
             send_f, send_s, recv_g, recv_t, recv_f, out_sems, send1, recv1, send2, recv2):
        del ord_ref
        u = pl.program_id(0)
        s = u // 2
        hf = u % 2
        rnd = s // 2
        x, y, c = _mesh_pos()
        sibling = (x, y, 1 - c)
        me = _block_of((x, y, c))
        peers = _peers()

        def rows_of(blk):
            return pl.ds(pl.multiple_of(blk * rsl, SUBLANES), rsl)

        def scatter():
            return [pltpu.make_async_remote_copy(
                src_ref=p_ref.at[rows_of(blk), :], dst_ref=rbuf_s.at[k], send_sem=send1.at[k],
                recv_sem=recv1.at[k], device_id=pos, device_id_type=MESH) for k, (pos, blk) in enumerate(peers)]

        def gather(dst_block=None):
            return [pltpu.make_async_remote_copy(
                src_ref=red, dst_ref=small_ref.at[rows_of(me if dst_block is None else blk), :],
                send_sem=send2.at[k], recv_sem=recv2.at[k], device_id=pos, device_id_type=MESH)
                for k, (pos, blk) in enumerate(peers)]

        def own_slice():
            return pltpu.make_async_copy(red, small_ref.at[rows_of(me), :], out_sems.at[2])

        @pl.when(u == 0)
        def _():
            for cp in scatter():
                cp.start()

        @pl.when(u == 4)
        def _():
            for cp in scatter():
                cp.wait_recv()
            tot = p_ref[rows_of(me), :]
            for k in range(N_DEV - 1):
                tot = tot + rbuf_s[k]
            red[...] = tot
            own_slice().start()
            for cp in gather():
                cp.start()

        n1 = (jnp.bitwise_xor(x, c), jnp.bitwise_xor(y, 1 - c), c)
        n2 = (jnp.bitwise_xor(x, 1 - c), jnp.bitwise_xor(y, c), c)

        def feed(r, half):
            return pltpu.make_async_remote_copy(
                src_ref=fb.at[half], dst_ref=gbuf.at[r, half], send_sem=send_f.at[r, half],
                recv_sem=recv_g.at[r, half], device_id=sibling, device_id_type=MESH)

        def feed_sibling(half):
            return pltpu.make_async_remote_copy(
                src_ref=fb.at[half], dst_ref=rfin.at[0, half], send_sem=send_f.at[3, half],
                recv_sem=recv_f.at[0, half], device_id=sibling, device_id_type=MESH)

        def chip_sum(r, half):
            dst = [tbuf.at[half], rfin.at[2, half], rfin.at[1, half]][r]
            sem = [recv_t.at[half], recv_f.at[2, half], recv_f.at[1, half]][r]
            return pltpu.make_async_remote_copy(
                src_ref=sbuf.at[r, half], dst_ref=dst, send_sem=send_s.at[r, half], recv_sem=sem,
                device_id=[n2, n2, n1][r], device_id_type=MESH)

        def out_copy(half):
            return pltpu.make_async_copy(acc.at[half], o_ref.at[:, pl.ds(half * dh, dh)], out_sems.at[half])

        def partial_sum():
            return _dot(a_ref[:, 0:n_tok], g_ref[...])

        for half in range(2):
            for r in range(3):
                @pl.when(u == 4 * r + 4 + half)
                def _(r=r, half=half):
                    feed(r, half).wait_send()

                @pl.when(u == 4 * r + 2 + half)
                def _(r=r, half=half):
                    feed(r, half).wait_recv()
                    if r == 2:
                        chip_sum(0, half).wait_recv()

            @pl.when(u == 14 + half)
            def _(half=half):
                feed_sibling(half).wait_recv()
                chip_sum(2, half).wait_recv()
                chip_sum(1, half).wait_recv()

        @pl.when(jnp.logical_and(s % 2 == 0, s < 7))
        def _():
            fb[hf] = partial_sum().astype(BF16)

        @pl.when(jnp.logical_or(s == 1, s == 3))
        def _():
            sbuf[rnd, hf] = (partial_sum() + gbuf[rnd, hf].astype(F32)).astype(BF16)

        @pl.when(s == 5)
        def _():
            sbuf[2, hf] = (partial_sum() + gbuf[2, hf].astype(F32) + tbuf[hf].astype(F32)).astype(BF16)

        @pl.when(s == 7)
        def _():
            acc[hf] = (partial_sum() + rfin[0, hf].astype(F32) + rfin[1, hf].astype(F32)
                       + rfin[2, hf].astype(F32))

        for half in range(2):
            for r in range(3):
                @pl.when(u == 4 * r + half)
                def _(r=r, half=half):
                    feed(r, half).start()

                @pl.when(u == 4 * r + 2 + half)
                def _(r=r, half=half):
                    chip_sum(r, half).start()

            @pl.when(u == 12 + half)
            def _(half=half):
                feed_sibling(half).start()

        @pl.when(u == 14)
        def _():
            out_copy(0).start()

        @pl.when(u == 15)
        def _():
            out_copy(1).start()
            for half in range(2):
                feed_sibling(half).wait_send()
                for r in range(3):
                    chip_sum(r, half).wait_send()
                out_copy(half).wait()
            for cp in gather("theirs"):
                cp.wait_recv()
            for cp in scatter() + gather():
                cp.wait_send()
            own_slice().wait()

    grid_spec = pltpu.PrefetchScalarGridSpec(
        num_scalar_prefetch=1, grid=(n_units,),
        in_specs=[VMEM, pl.BlockSpec((n_tok, dh), lambda u, o: (0, 2 * o[u // 2] + u % 2)), VMEM],
        out_specs=[ANY, ANY],
        scratch_shapes=[pltpu.VMEM((2, d_model, dh), F32), pltpu.VMEM((2, d_model, dh), BF16),
                        pltpu.VMEM((3, 2, d_model, dh), BF16), pltpu.VMEM((3, 2, d_model, dh), BF16),
                        pltpu.VMEM((2, d_model, dh), BF16), pltpu.VMEM((3, 2, d_model, dh), BF16),
                        pltpu.VMEM((N_DEV - 1, rsl, LANES), F32), pltpu.VMEM((rsl, LANES), F32),
                        pltpu.SemaphoreType.DMA((4, 2)), pltpu.SemaphoreType.DMA((3, 2)),
                        pltpu.SemaphoreType.DMA((3, 2)), pltpu.SemaphoreType.DMA((2,)),
                        pltpu.SemaphoreType.DMA((3, 2)),
                        pltpu.SemaphoreType.DMA((3,))] + [pltpu.SemaphoreType.DMA((N_DEV - 1,))] * 4)
    return pl.pallas_call(
        body, name="grad_w_in_reduce_scatter", grid_spec=grid_spec,
        out_shape=[SDS((d_model, d_model), F32), SDS(small_part.shape, F32)], compiler_params=_params(1),
    )(blk_order, xt_bf, dproj, small_part)


def _grad_x_adamw(dproj, w_all, dr, w, g, m, v):
    n_tok, d_model = dr.shape
    tm = _tile(n_tok, 512, 16)
    n_steps = n_tok // tm
    tr = w.shape[0] // n_steps

    def body(dp_ref, w_ref, dr_ref, ws_ref, g_ref, m_ref, v_ref, o_ref, d_ref, mo_ref, vo_ref):
        acc = _dot_tb(dp_ref[:, 0:d_model], w_ref[0])
        for j in range(1, N_DEV):
            acc = acc + _dot_tb(dp_ref[:, j * d_model:(j + 1) * d_model], w_ref[j])
        o_ref[...] = acc + DEEPNORM_ALPHA * dr_ref[...]
        d_ref[...], mo_ref[...], vo_ref[...] = _adamw_math(ws_ref[...], g_ref[...], m_ref[...], v_ref[...])

    tile = pl.BlockSpec((tm, d_model), lambda i: (i, 0))
    slab = pl.BlockSpec((tr, w.shape[1]), lambda i: (i, 0))
    return pl.pallas_call(
        body, name="grad_x_adamw", grid=(n_steps,),
        in_specs=[pl.BlockSpec((tm, N_DEV * d_model), lambda i: (i, 0)), VMEM, tile, slab, slab, slab, slab],
        out_specs=[tile, slab, slab, slab],
        out_shape=[SDS((n_tok, d_model), F32)] + [SDS(w.shape, F32)] * 3,
        compiler_params=_params(1),
    )(dproj, w_all, dr, w, g, m, v)


def _adamw_math(w, g, m, v):
    m = ADAM_B1 * m + (1.0 - ADAM_B1) * g
    v = ADAM_B2 * v + (1.0 - ADAM_B2) * (g * g)
    m_hat = m / (1.0 - ADAM_B1 ** ADAM_STEP)
    v_hat = v / (1.0 - ADAM_B2 ** ADAM_STEP)
    delta = -ADAM_LR * (m_hat / (jnp.sqrt(v_hat) + ADAM_EPS) + ADAM_WD * w)
    return delta, m, v


def _adamw_many(groups):
    n = len(groups)

    def body(*refs):
        ins, outs = refs[:4 * n], refs[4 * n:]
        for p in range(n):
            w_ref, g_ref, m_ref, v_ref = ins[4 * p:4 * p + 4]
            d, m, v = _adamw_math(w_ref[...], g_ref[...], m_ref[...], v_ref[...])
            outs[3 * p][...] = d
            outs[3 * p + 1][...] = m
            outs[3 * p + 2][...] = v

    flat = [a for grp in groups for a in grp]
    out_shape = [SDS(grp[0].shape, F32) for grp in groups for _ in range(3)]
    res = pl.pallas_call(
        body, name="adamw_small", in_specs=[VMEM] * (4 * n), out_specs=[VMEM] * (3 * n),
        out_shape=out_shape, compiler_params=_params(),
    )(*flat)
    return [tuple(res[3 * p:3 * p + 3]) for p in range(n)]


def _as_rows(a):
    return a.reshape(-1, LANES)


def kernel(x, w_in, b_in, conv_w, conv_b, gn_g, gn_b, ln_v_g, ln_v_b, w_spatial, b_spatial, w_pa, w_pb, w_o, b_o, ln_out_g, ln_out_b, loss_target, m_w_in, m_b_in, m_conv_w, m_conv_b, m_gn_g, m_gn_b, m_ln_v_g, m_ln_v_b, m_w_spatial, m_b_spatial, m_w_pa, m_w_pb, m_w_o, m_b_o, m_ln_out_g, m_ln_out_b, v_w_in, v_b_in, v_conv_w, v_conv_b, v_gn_g, v_gn_b, v_ln_v_g, v_ln_v_b, v_w_spatial, v_b_spatial, v_w_pa, v_w_pb, v_w_o, v_b_o, v_ln_out_g, v_ln_out_b):
    n_batch, seq, d_model = x.shape
    n_tok = n_batch * seq
    n_heads = d_model // LANES
    dc = conv_w.shape[1]
    me = 4 * lax.axis_index("x") + 2 * lax.axis_index("y") + lax.axis_index("c")
    row = lambda a: a.reshape(1, d_model)

    x2 = x.reshape(n_tok, d_model)
    target2 = loss_target.reshape(n_tok, d_model)
    b_spatial_t = b_spatial.T

    first = jnp.where(lax.axis_index("c") == 1, 4, 2)
    second = 6 - first
    ag_rel = jnp.stack([0 * first, 0 * first + 1, first, second + 1, second, first + 1, 0 * first + 6, 0 * first + 7])
    ag_blocks = jnp.bitwise_xor(me, ag_rel).astype(jnp.int32)
    proj, xt_bf, w_all, wp_all, cw_all = _proj_all_gather(
        x2, w_in, w_pa, w_pb, w_o, conv_w, b_in.reshape(N_DEV, 1, d_model), ag_blocks)
    wp_full = wp_all.reshape(3, d_model, d_model)
    conv_w_full = jnp.pad(cw_all.transpose(1, 0, 2).reshape(CONV_K, d_model), ((0, HALO - CONV_K), (0, 0)))

    h3, h1 = _branch_a_fwd(proj, conv_w_full, row(conv_b), row(gn_g), row(gn_b), seq)
    s = _branch_b_fwd(proj, row(ln_v_g), row(ln_v_b), w_spatial, b_spatial_t, seq)

    dproj, d_h3, d_s, dr, lhs3, rhs3, vec_mid = _mid(
        h3, s, proj, x2, target2, wp_full, row(b_o), row(ln_out_g), row(ln_out_b))

    dproj, vec_b, d_ws, d_bs_t = _branch_b_bwd(dproj, proj, d_s, row(ln_v_g), row(ln_v_b), w_spatial, b_spatial_t, seq)
    dproj, d_h1, vec_a1 = _branch_a_bwd_norm(dproj, proj, h1, d_h3, row(gn_g), row(gn_b), seq)
    gp_f32, gp_bf = _weight_grads(lhs3, rhs3)
    dproj, vec_a2, d_cw8, g_w_pa, g_w_pb, g_w_o = _branch_a_bwd_conv(
        dproj, proj, d_h1, conv_w_full, gp_bf, gp_f32, seq)

    d_cw = jnp.sum(d_cw8, axis=2)
    pieces = [
        _as_rows(jnp.concatenate([vec_a2[0:2], vec_a1[0:1], vec_b[0:3], vec_mid[3:5]], axis=0)),
        _as_rows(jnp.concatenate([vec_a1[3:4], vec_a1[1:3], vec_b[3:5], vec_mid[2:3], vec_mid[0:2]], axis=0)),
        _as_rows(d_bs_t.T), _as_rows(d_ws), _as_rows(d_cw), _as_rows(vec_mid[5:6]),
    ]
    n_rows = sum(p.shape[0] for p in pieces)
    pad_rows = -n_rows % (N_DEV * SUBLANES)
    small_part = jnp.concatenate(pieces + [jnp.zeros((pad_rows, LANES), F32)], axis=0)

    rs_rel = jnp.stack([0 * first + 7, 0 * first + 6, first + 1, second, second + 1, first, 0 * first + 1, 0 * first])
    rs_blocks = jnp.bitwise_xor(me, rs_rel).astype(jnp.int32)
    g_w_in, small = _grad_w_in_reduce_scatter(xt_bf, dproj, rs_blocks, small_part)
    grad_x, d_w_in, nm_w_in, nv_w_in = _grad_x_adamw(dproj, w_all, dr, w_in, g_w_in, m_w_in, v_w_in)
    grad_x = grad_x.reshape(x.shape)

    g_rows = d_model // LANES
    o0 = N_DEV * g_rows
    g_b_in = small[0:o0].reshape(N_DEV * d_model)
    vecs = [small[o0 + a * g_rows:o0 + (a + 1) * g_rows].reshape(d_model) for a in range(8)]
    g_conv_b, g_gn_g, g_gn_b, g_ln_v_g, g_ln_v_b, g_b_o, g_ln_out_g, g_ln_out_b = vecs
    o1 = o0 + 8 * g_rows
    g_b_spatial = small[o1:o1 + n_heads].reshape(n_heads, LANES)
    o2 = o1 + n_heads
    g_w_spatial = small[o2:o2 + n_heads * LANES].reshape(n_heads, LANES, LANES)
    o3 = o2 + n_heads * LANES
    g_cw_full = small[o3:o3 + n_heads * HALO].reshape(n_heads, HALO, LANES).transpose(1, 0, 2).reshape(HALO, d_model)
    g_conv_w = lax.dynamic_slice(g_cw_full, (0, me * dc), (CONV_K, dc))
    o4 = o3 + n_heads * HALO
    loss = jnp.sum(small[o4:o4 + g_rows]) * (0.5 / d_model)

    two_d = lambda a: a.reshape(-1, a.shape[-1]) if a.ndim != 1 else (
        a.reshape(-1, LANES) if a.shape[0] % LANES == 0 else a.reshape(1, -1))
    names = ["b_in", "conv_w", "conv_b", "gn_g", "gn_b", "ln_v_g", "ln_v_b", "w_spatial", "b_spatial",
             "w_pa", "w_pb", "w_o", "b_o", "ln_out_g", "ln_out_b"]
    ws = dict(b_in=b_in, conv_w=conv_w, conv_b=conv_b, gn_g=gn_g, gn_b=gn_b, ln_v_g=ln_v_g, ln_v_b=ln_v_b,
              w_spatial=w_spatial, b_spatial=b_spatial, w_pa=w_pa, w_pb=w_pb, w_o=w_o, b_o=b_o,
              ln_out_g=ln_out_g, ln_out_b=ln_out_b)
    gs = dict(b_in=g_b_in, conv_w=g_conv_w, conv_b=g_conv_b, gn_g=g_gn_g, gn_b=g_gn_b, ln_v_g=g_ln_v_g,
              ln_v_b=g_ln_v_b, w_spatial=g_w_spatial, b_spatial=g_b_spatial, w_pa=g_w_pa, w_pb=g_w_pb,
              w_o=g_w_o, b_o=g_b_o, ln_out_g=g_ln_out_g, ln_out_b=g_ln_out_b)
    ms = dict(b_in=m_b_in, conv_w=m_conv_w, conv_b=m_conv_b, gn_g=m_gn_g, gn_b=m_gn_b, ln_v_g=m_ln_v_g,
              ln_v_b=m_ln_v_b, w_spatial=m_w_spatial, b_spatial=m_b_spatial, w_pa=m_w_pa, w_pb=m_w_pb,
              w_o=m_w_o, b_o=m_b_o, ln_out_g=m_ln_out_g, ln_out_b=m_ln_out_b)
    vs = dict(b_in=v_b_in, conv_w=v_conv_w, conv_b=v_conv_b, gn_g=v_gn_g, gn_b=v_gn_b, ln_v_g=v_ln_v_g,
              ln_v_b=v_ln_v_b, w_spatial=v_w_spatial, b_spatial=v_b_spatial, w_pa=v_w_pa, w_pb=v_w_pb,
              w_o=v_w_o, b_o=v_b_o, ln_out_g=v_ln_out_g, ln_out_b=v_ln_out_b)
    upd = _adamw_many([tuple(two_d(d[n]) for d in (ws, gs, ms, vs)) for n in names])
    delta = {n: u[0].reshape(ws[n].shape) for n, u in zip(names, upd)}
    new_m = {n: u[1].reshape(ws[n].shape) for n, u in zip(names, upd)}
    new_v = {n: u[2].reshape(ws[n].shape) for n, u in zip(names, upd)}
    gs["w_in"], delta["w_in"], new_m["w_in"], new_v["w_in"] = g_w_in, d_w_in, nm_w_in, nv_w_in

    order = ["w_in"] + names
    return (loss, grad_x, *[gs[n] for n in order], *[delta[n] for n in order],
            *[new_m[n] for n in order], *[new_v[n] for n in order])
```

```python
import jax
import jax.numpy as jnp
from jax import lax
from jax.experimental import pallas as pl
from jax.experimental.pallas import tpu as pltpu

F32 = jnp.float32
BF16 = jnp.bfloat16
SDS = jax.ShapeDtypeStruct

N_DEV = 8
LANES = 128
SUBLANES = 8
CONV_K = 31
HALO = 32
LN_EPS = 1e-5
DEEPNORM_ALPHA = 2.0 ** 0.25
ADAM_LR, ADAM_B1, ADAM_B2, ADAM_EPS, ADAM_WD, ADAM_STEP = 0.001, 0.9, 0.999, 1e-08, 0.01, 10
GELU_C = 0.7978845608028654
GELU_A = 0.044715
VMEM_LIMIT = 56 * 1024 * 1024
MESH = pl.DeviceIdType.MESH
ANY = pl.BlockSpec(memory_space=pl.ANY)
VMEM = pl.BlockSpec(memory_space=pltpu.VMEM)


def _params(n_grid=0):
    sem = ("arbitrary",) * n_grid if n_grid else None
    return pltpu.CompilerParams(dimension_semantics=sem, vmem_limit_bytes=VMEM_LIMIT)


def _tile(n, pref, mult):
    t = min(n, pref)
    while n % t or t % mult:
        t -= 1
    return t


def _colsum(v):
    return jnp.sum(v, axis=0, keepdims=True)


def _sigmoid(v):
    return jax.nn.sigmoid(v)


def _silu_and_grad(v):
    s = _sigmoid(v)
    return v * s, s * (1.0 + v * (1.0 - s))


def _gelu_and_grad(v):
    inner = GELU_C * (v + GELU_A * v * v * v)
    th = jnp.tanh(inner)
    val = 0.5 * v * (1.0 + th)
    grad = 0.5 * (1.0 + th) + 0.5 * v * (1.0 - th * th) * GELU_C * (1.0 + 3.0 * GELU_A * v * v)
    return val, grad


def _tril_mask():
    r = lax.broadcasted_iota(jnp.int32, (LANES, LANES), 0)
    c = lax.broadcasted_iota(jnp.int32, (LANES, LANES), 1)
    return c <= r


def _dot(a, b):
    return jnp.dot(a, b, preferred_element_type=F32)


def _dot_tb(a, b):
    return lax.dot_general(a, b, (((1,), (1,)), ((), ())), preferred_element_type=F32)


def _dot_ta(a, b):
    return lax.dot_general(a, b, (((0,), (0,)), ((), ())), preferred_element_type=F32)


def _mesh_pos():
    return lax.axis_index("x"), lax.axis_index("y"), lax.axis_index("c")


def _block_of(pos):
    return 4 * pos[0] + 2 * pos[1] + pos[2]


def _peers():
    x, y, c = _mesh_pos()
    out = []
    for k in range(1, N_DEV):
        pos = (1 - x if k & 4 else x, 1 - y if k & 2 else y, 1 - c if k & 1 else c)
        out.append((pos, _block_of(pos)))
    return out


def _proj_all_gather(x2, w_in, w_pa, w_pb, w_o, conv_w, b_in3, blk_order):
    n_tok, d_model = x2.shape
    r8 = w_pa.shape[0]
    kc, dc = conv_w.shape
    tm = _tile(n_tok, 1024, LANES)
    n_t = n_tok // tm
    n_arr = 3

    def body(ord_ref, x_ref, b_ref, win_ref, wpa_ref, wpb_ref, wo_ref, cw_ref,
             proj_ref, xt_ref, wall_ref, wp_ref, cwall_ref,
             wbuf, xbuf, st_p, send_sems, recv_sems, local_sems, wall_sems):
        s = pl.program_id(0)
        t = pl.program_id(1)
        x, y, c = _mesh_pos()
        me = (x, y, c)
        sibling = (x, y, 1 - c)
        n1 = (jnp.bitwise_xor(x, c), jnp.bitwise_xor(y, 1 - c))
        n2 = (jnp.bitwise_xor(x, 1 - c), jnp.bitwise_xor(y, c))
        dg = (1 - x, 1 - y)
        outs = [wbuf, wp_ref, cwall_ref]
        srcs = [None, st_p, cw_ref]
        consumed = [me, sibling, (*n1, c), (*n2, 1 - c), (*n2, c), (*n1, 1 - c), (*dg, c), (*dg, 1 - c)]
        leaves = [(me, sibling), (me, (*n1, c)), (me, (*n2, c)), ((*n1, c), (*n2, c)),
                  ((*n1, c), sibling), ((*n2, c), sibling), ((*dg, c), sibling)]
        lands = [sibling, (*n1, c), (*n2, c), (*dg, c), (*n2, 1 - c), (*n1, 1 - c), (*dg, 1 - c)]

        def slot(o, pos):
            return o.at[:, _block_of(pos)] if o is wp_ref else o.at[_block_of(pos)]

        def copy(a, k, block, to, src=None):
            o = outs[a]
            return pltpu.make_async_remote_copy(
                src_ref=slot(o, block) if src is None else src, dst_ref=slot(o, block),
                send_sem=send_sems.at[a, k], recv_sem=recv_sems.at[a, k],
                device_id=to, device_id_type=MESH)

        def send(a, k):
            block, to = leaves[k]
            return copy(a, k, block, to, src=srcs[a] if k < 3 else None)

        def recv(a, k):
            return copy(a, k, lands[k], me)

        def local_copies():
            return [pltpu.make_async_copy(srcs[a], slot(outs[a], me), local_sems.at[a]) for a in (1, 2)]

        def to_hbm(step):
            return pltpu.make_async_copy(slot(wbuf, consumed[step]), slot(wall_ref, consumed[step]),
                                         wall_sems.at[step])

        def at_step(step):
            return pl.when(jnp.logical_and(s == step, t == 0))

        @at_step(0)
        def _():
            slot(wbuf, me)[...] = win_ref[...].astype(BF16)
            st_p[0] = wpa_ref[...].astype(BF16)
            st_p[1] = wpb_ref[...].astype(BF16)
            st_p[2] = wo_ref[...].astype(BF16)
            for a in range(n_arr):
                send(a, 0).start()
                send(a, 1).start()
            for cp in local_copies():
                cp.start()
            to_hbm(0).start()

        @at_step(1)
        def _():
            recv(0, 0).wait_recv()
            to_hbm(1).start()

        for rnd in range(3):
            @at_step(2 + 2 * rnd)
            def _(rnd=rnd):
                if rnd == 0:
                    for a in range(n_arr):
                        send(a, 2).start()
                recv(0, 1 + rnd).wait_recv()
                if rnd == 0:
                    send(0, 3).start()
                send(0, 4 + rnd).start()
                to_hbm(2 + 2 * rnd).start()

            @at_step(3 + 2 * rnd)
            def _(rnd=rnd):
                for a in (1, 2):
                    recv(a, 1 + rnd).wait_recv()
                    if rnd == 0:
                        send(a, 3).start()
                    send(a, 4 + rnd).start()
                recv(0, 4 + rnd).wait_recv()
                to_hbm(3 + 2 * rnd).start()

        rows = pl.ds(pl.multiple_of(t * tm, tm), tm)

        @pl.when(s == 0)
        def _():
            xb = x_ref[...].astype(BF16)
            xbuf[rows, :] = xb
            xt_ref[...] = xb.T

        proj_ref[...] = _dot(xbuf[rows, :], wbuf[ord_ref[s]]) + b_ref[...]

        @pl.when(jnp.logical_and(s == N_DEV - 1, t == n_t - 1))
        def _():
            for a in (1, 2):
                for k in (0, 4, 5, 6):
                    recv(a, k).wait_recv()
            for a in range(n_arr):
                for k in range(7):
                    send(a, k).wait_send()
            for cp in local_copies() + [to_hbm(step) for step in range(N_DEV)]:
                cp.wait()

    grid_spec = pltpu.PrefetchScalarGridSpec(
        num_scalar_prefetch=1, grid=(N_DEV, n_t),
        in_specs=[pl.BlockSpec((tm, d_model), lambda s, t, o: (jnp.where(s == 0, t, n_t - 1), 0)),
                  pl.BlockSpec((None, 1, d_model), lambda s, t, o: (o[s], 0, 0)),
                  VMEM, VMEM, VMEM, VMEM, VMEM],
        out_specs=[pl.BlockSpec((tm, d_model), lambda s, t, o: (t, o[s])),
                   pl.BlockSpec((d_model, tm), lambda s, t, o: (0, jnp.where(s == 0, t, n_t))),
                   ANY, ANY, ANY],
        scratch_shapes=[pltpu.VMEM((N_DEV, d_model, d_model), BF16), pltpu.VMEM((n_tok, d_model), BF16),
                        pltpu.VMEM((3, r8, d_model), BF16),
                        pltpu.SemaphoreType.DMA((n_arr, 7)), pltpu.SemaphoreType.DMA((n_arr, 7)),
                        pltpu.SemaphoreType.DMA((3,)), pltpu.SemaphoreType.DMA((N_DEV,))])
    return pl.pallas_call(
        body, name="proj_all_gather", grid_spec=grid_spec,
        out_shape=[SDS((n_tok, N_DEV * d_model), F32), SDS((d_model, (n_t + 1) * tm), BF16),
                   SDS((N_DEV, d_model, d_model), BF16), SDS((3, N_DEV, r8, d_model), BF16),
                   SDS((N_DEV, kc, dc), F32)],
        compiler_params=_params(2),
    )(blk_order, x2, b_in3, w_in, w_pa, w_pb, w_o, conv_w)


def _conv_rows(ta):
    return _tile(ta, 64, SUBLANES)


def _branch_a_fwd(proj, conv_w_full, conv_b, gn_g, gn_b, seq):
    n_tok = proj.shape[0]
    d_model = conv_b.shape[1]
    ta = _tile(seq, 256, HALO)
    per_seq = seq // ta
    rc = _conv_rows(ta)

    def body(av_ref, ag_ref, gt_ref, avh_ref, agh_ref, cw_ref, cb_ref, gg_ref, gb_ref,
             h3_ref, h1_ref, ext):
        keep = jnp.where(pl.program_id(0) % per_seq == 0, 0.0, 1.0)

        def group(g, carry):
            sl = pl.ds(pl.multiple_of(g * LANES, LANES), LANES)
            ext[0:HALO, :] = avh_ref[:, sl] * _sigmoid(agh_ref[:, sl]) * keep
            ext[HALO:HALO + ta, :] = av_ref[:, sl] * _sigmoid(ag_ref[:, sl])
            for r0 in range(0, ta, rc):
                acc = jnp.broadcast_to(cb_ref[:, sl], (rc, LANES))
                for k in range(CONV_K):
                    acc = acc + ext[pl.ds(r0 + HALO - (CONV_K - 1) + k, rc), :] * cw_ref[k:k + 1, sl]
                h1_ref[pl.ds(r0, rc), sl] = acc
            h1 = h1_ref[:, sl]
            mu = jnp.mean(h1, axis=-1, keepdims=True)
            dlt = h1 - mu
            var = jnp.mean(dlt * dlt, axis=-1, keepdims=True)
            h2 = dlt * lax.rsqrt(var + LN_EPS) * gg_ref[:, sl] + gb_ref[:, sl]
            gate = gt_ref[:, sl]
            h3_ref[:, sl] = (h2 * _sigmoid(h2) * gate * _sigmoid(gate)).astype(BF16)
            return carry

        lax.fori_loop(0, d_model // LANES, group, 0, unroll=True)

    blk = lambda j: pl.BlockSpec((ta, d_model), lambda i: (i, j))
    halo = lambda j: pl.BlockSpec((HALO, d_model), lambda i: (jnp.maximum(i * (ta // HALO) - 1, 0), j))
    row = pl.BlockSpec((1, d_model), lambda i: (0, 0))
    return pl.pallas_call(
        body, name="branch_a_fwd", grid=(n_tok // ta,),
        in_specs=[blk(0), blk(1), blk(2), halo(0), halo(1),
                  pl.BlockSpec((HALO, d_model), lambda i: (0, 0)), row, row, row],
        out_specs=[pl.BlockSpec((ta, d_model), lambda i: (i, 0))] * 2,
        out_shape=[SDS((n_tok, d_model), BF16), SDS((n_tok, d_model), F32)],
        scratch_shapes=[pltpu.VMEM((HALO + ta, LANES), F32)],
        compiler_params=_params(1),
    )(proj, proj, proj, proj, proj, conv_w_full, conv_b, gn_g, gn_b)


def _branch_b_fwd(proj, ln_g, ln_b, w_spatial, b_spatial_t, seq):
    n_tok = proj.shape[0]
    d_model = ln_g.shape[1]
    n_heads = d_model // LANES
    tb = _tile(seq, 256, LANES)

    def body(u_ref, v_ref, bg_ref, lg_ref, lb_ref, ws_ref, bs_ref, s_ref, vn_buf):
        v, _ = _gelu_and_grad(v_ref[...])
        mu = jnp.mean(v, axis=-1, keepdims=True)
        dlt = v - mu
        var = jnp.mean(dlt * dlt, axis=-1, keepdims=True)
        vn_buf[...] = (dlt * lax.rsqrt(var + LN_EPS) * lg_ref[...] + lb_ref[...]).astype(BF16)
        tril = _tril_mask()
        for h in range(n_heads):
            cols = slice(h * LANES, (h + 1) * LANES)
            w_h = jnp.where(tril, ws_ref[h], 0.0).astype(BF16)
            bias = bs_ref[:, h:h + 1]
            for ch in range(tb // LANES):
                rows = slice(ch * LANES, (ch + 1) * LANES)
                mix = _dot(w_h, vn_buf[rows, cols]) + bias
                u, _ = _gelu_and_grad(u_ref[rows, cols])
                gate = bg_ref[rows, cols]
                s_ref[rows, cols] = (u * mix * gate * _sigmoid(gate)).astype(BF16)

    blk = lambda j: pl.BlockSpec((tb, d_model), lambda i: (i, j))
    row = pl.BlockSpec((1, d_model), lambda i: (0, 0))
    return pl.pallas_call(
        body, name="branch_b_fwd", grid=(n_tok // tb,),
        in_specs=[blk(3), blk(4), blk(5), row, row,
                  pl.BlockSpec((n_heads, LANES, LANES), lambda i: (0, 0, 0)),
                  pl.BlockSpec((LANES, n_heads), lambda i: (0, 0))],
        out_specs=pl.BlockSpec((tb, d_model), lambda i: (i, 0)),
        out_shape=SDS((n_tok, d_model), BF16),
        scratch_shapes=[pltpu.VMEM((tb, d_model), BF16)],
        compiler_params=_params(1),
    )(proj, proj, proj, ln_g, ln_b, w_spatial, b_spatial_t)


MID_ROWS = 8


def _mid(h3, s, proj, x2, target, wp_full, b_o, lo_g, lo_b):
    n_tok, d_model = x2.shape
    tm = _tile(n_tok, 256, 16)

    def body(h3_ref, s_ref, ma_ref, mb_ref, x_ref, t_ref, wpa_ref, wpb_ref, wo_ref, bo_ref,
             lg_ref, lb_ref, dproj_ref, dh3_ref, ds_ref, dr_ref, lhs3_ref, rhs3_ref, vec_ref):
        @pl.when(pl.program_id(0) == 0)
        def _():
            vec_ref[...] = jnp.zeros_like(vec_ref)

        h3 = h3_ref[...]
        s = s_ref[...]
        ya = _dot(h3, wpa_ref[...])
        yb = _dot(s, wpb_ref[...])
        ga = _sigmoid(ma_ref[...])
        gb = _sigmoid(mb_ref[...])
        mixed = (ga * ya + gb * yb).astype(BF16)
        lhs3_ref[0] = h3
        lhs3_ref[1] = s
        lhs3_ref[2] = mixed
        r = DEEPNORM_ALPHA * x_ref[...] + _dot(mixed, wo_ref[...]) + bo_ref[...]
        mu = jnp.mean(r, axis=-1, keepdims=True)
        dlt = r - mu
        rstd = lax.rsqrt(jnp.mean(dlt * dlt, axis=-1, keepdims=True) + LN_EPS)
        rhat = dlt * rstd
        diff = rhat * lg_ref[...] + lb_ref[...] - t_ref[...]
        dy = diff * (1.0 / d_model)
        vec_ref[0:1, :] += _colsum(dy * rhat)
        vec_ref[1:2, :] += _colsum(dy)
        vec_ref[5:6, :] += _colsum(diff * diff)
        drh = dy * lg_ref[...]
        dr = rstd * (drh - jnp.mean(drh, axis=-1, keepdims=True)
                     - rhat * jnp.mean(drh * rhat, axis=-1, keepdims=True))
        vec_ref[2:3, :] += _colsum(dr)
        dr_ref[...] = dr
        drb = dr.astype(BF16)
        rhs3_ref[2] = drb
        dmixed = _dot_tb(drb, wo_ref[...])
        dma = dmixed * ya * ga * (1.0 - ga)
        dmb = dmixed * yb * gb * (1.0 - gb)
        vec_ref[3:4, :] += _colsum(dma)
        vec_ref[4:5, :] += _colsum(dmb)
        dproj_ref[:, 0:d_model] = dma.astype(BF16)
        dproj_ref[:, d_model:2 * d_model] = dmb.astype(BF16)
        dya = (dmixed * ga).astype(BF16)
        dyb = (dmixed * gb).astype(BF16)
        rhs3_ref[0] = dya
        rhs3_ref[1] = dyb
        dh3_ref[...] = _dot_tb(dya, wpa_ref[...])
        ds_ref[...] = _dot_tb(dyb, wpb_ref[...])

    tile = pl.BlockSpec((tm, d_model), lambda i: (i, 0))
    full = lambda a: pl.BlockSpec((None, d_model, d_model), lambda i: (a, 0, 0))
    row = pl.BlockSpec((1, d_model), lambda i: (0, 0))
    stack = pl.BlockSpec((3, tm, d_model), lambda i: (0, i, 0))
    bf3 = SDS((3, n_tok, d_model), BF16)
    f32 = SDS((n_tok, d_model), F32)
    return pl.pallas_call(
        body, name="mid", grid=(n_tok // tm,),
        in_specs=[tile, tile, pl.BlockSpec((tm, d_model), lambda i: (i, 6)),
                  pl.BlockSpec((tm, d_model), lambda i: (i, 7)), tile, tile, full(0), full(1), full(2),
                  row, row, row],
        out_specs=[pl.BlockSpec((tm, 2 * d_model), lambda i: (i, 3)), tile, tile, tile, stack, stack,
                   pl.BlockSpec((MID_ROWS, d_model), lambda i: (0, 0))],
        out_shape=[SDS((n_tok, N_DEV * d_model), BF16), f32, f32, f32, bf3, bf3,
                   SDS((MID_ROWS, d_model), F32)],
        compiler_params=_params(1),
    )(h3, s, proj, proj, x2, target, wp_full, wp_full, wp_full, b_o, lo_g, lo_b)


B_ROWS = 8


def _branch_b_bwd(dproj, proj, d_s, ln_g, ln_b, w_spatial, b_spatial_t, seq):
    n_tok = proj.shape[0]
    d_model = ln_g.shape[1]
    n_heads = d_model // LANES
    tb = _tile(seq, 256, LANES)

    def body(dproj_in, u_ref, v_ref, bg_ref, ds_ref, lg_ref, lb_ref, ws_ref, bs_ref,
             dproj_ref, vec_ref, dws_ref, dbs_ref, vn_buf, dv_buf):
        del dproj_in

        @pl.when(pl.program_id(0) == 0)
        def _():
            vec_ref[...] = jnp.zeros_like(vec_ref)
            dws_ref[...] = jnp.zeros_like(dws_ref)
            dbs_ref[...] = jnp.zeros_like(dbs_ref)

        v, dgelu_v = _gelu_and_grad(v_ref[...])
        mu = jnp.mean(v, axis=-1, keepdims=True)
        dlt = v - mu
        rstd = lax.rsqrt(jnp.mean(dlt * dlt, axis=-1, keepdims=True) + LN_EPS)
        vhat = dlt * rstd
        vn_buf[...] = (vhat * lg_ref[...] + lb_ref[...]).astype(BF16)
        tril = _tril_mask()
        for h in range(n_heads):
            cols = slice(h * LANES, (h + 1) * LANES)
            w_h = jnp.where(tril, ws_ref[h], 0.0).astype(BF16)
            bias = bs_ref[:, h:h + 1]
            for ch in range(tb // LANES):
                rows = slice(ch * LANES, (ch + 1) * LANES)
                vn = vn_buf[rows, cols]
                mix = _dot(w_h, vn) + bias
                u, dgelu_u = _gelu_and_grad(u_ref[rows, cols])
                sg, dsilu = _silu_and_grad(bg_ref[rows, cols])
                dsv = ds_ref[rows, cols]
                du = dsv * mix * sg * dgelu_u
                dbg = dsv * u * mix * dsilu
                dmix = dsv * u * sg
                dmix_bf = dmix.astype(BF16)
                dproj_ref[rows, cols] = du.astype(BF16)
                dproj_ref[rows, 2 * d_model + h * LANES:2 * d_model + (h + 1) * LANES] = dbg.astype(BF16)
                vec_ref[0:1, cols] += _colsum(du)
                vec_ref[2:3, cols] += _colsum(dbg)
                dbs_ref[:, h:h + 1] += jnp.sum(dmix, axis=1, keepdims=True)
                dws_ref[h] += jnp.where(tril, _dot_tb(dmix_bf, vn), 0.0)
                dv_buf[rows, cols] = _dot_ta(w_h, dmix_bf)
        dvn = dv_buf[...]
        vec_ref[3:4, :] += _colsum(dvn * vhat)
        vec_ref[4:5, :] += _colsum(dvn)
        dvh = dvn * lg_ref[...]
        dv = rstd * (dvh - jnp.mean(dvh, axis=-1, keepdims=True)
                     - vhat * jnp.mean(dvh * vhat, axis=-1, keepdims=True)) * dgelu_v
        vec_ref[1:2, :] += _colsum(dv)
        dproj_ref[:, d_model:2 * d_model] = dv.astype(BF16)

    blk = lambda j: pl.BlockSpec((tb, d_model), lambda i: (i, j))
    row = pl.BlockSpec((1, d_model), lambda i: (0, 0))
    return pl.pallas_call(
        body, name="branch_b_bwd", grid=(n_tok // tb,),
        in_specs=[ANY, blk(3), blk(4), blk(5), pl.BlockSpec((tb, d_model), lambda i: (i, 0)), row, row,
                  pl.BlockSpec((n_heads, LANES, LANES), lambda i: (0, 0, 0)),
                  pl.BlockSpec((LANES, n_heads), lambda i: (0, 0))],
        out_specs=[pl.BlockSpec((tb, 3 * d_model), lambda i: (i, 1)),
                   pl.BlockSpec((B_ROWS, d_model), lambda i: (0, 0)),
                   pl.BlockSpec((n_heads, LANES, LANES), lambda i: (0, 0, 0)),
                   pl.BlockSpec((LANES, n_heads), lambda i: (0, 0))],
        out_shape=[SDS(dproj.shape, BF16), SDS((B_ROWS, d_model), F32),
                   SDS((n_heads, LANES, LANES), F32), SDS((LANES, n_heads), F32)],
        scratch_shapes=[pltpu.VMEM((tb, d_model), BF16), pltpu.VMEM((tb, d_model), F32)],
        input_output_aliases={0: 0},
        compiler_params=_params(1),
    )(dproj, proj, proj, proj, d_s, ln_g, ln_b, w_spatial, b_spatial_t)


A1_ROWS = 8


def _branch_a_bwd_norm(dproj, proj, h1, d_h3, gn_g, gn_b, seq):
    n_tok = proj.shape[0]
    d_model = gn_g.shape[1]
    ta = _tile(seq, 256, 16)

    def body(dproj_in, gt_ref, h1_ref, dh3_ref, gg_ref, gb_ref, dproj_ref, dh1_ref, vec_ref):
        del dproj_in

        @pl.when(pl.program_id(0) == 0)
        def _():
            vec_ref[...] = jnp.zeros_like(vec_ref)

        def group(g, carry):
            sl = pl.ds(pl.multiple_of(g * LANES, LANES), LANES)
            h1 = h1_ref[:, sl]
            mu = jnp.mean(h1, axis=-1, keepdims=True)
            dlt = h1 - mu
            rstd = lax.rsqrt(jnp.mean(dlt * dlt, axis=-1, keepdims=True) + LN_EPS)
            nrm = dlt * rstd
            sw, dsw = _silu_and_grad(nrm * gg_ref[:, sl] + gb_ref[:, sl])
            sg, dsg = _silu_and_grad(gt_ref[:, sl])
            dh3 = dh3_ref[:, sl]
            dgate = dh3 * sw * dsg
            dproj_ref[:, sl] = dgate.astype(BF16)
            vec_ref[0:1, sl] += _colsum(dgate)
            dh2 = dh3 * sg * dsw
            vec_ref[1:2, sl] += _colsum(dh2 * nrm)
            vec_ref[2:3, sl] += _colsum(dh2)
            dn = dh2 * gg_ref[:, sl]
            dh1 = rstd * (dn - jnp.mean(dn, axis=-1, keepdims=True)
                          - nrm * jnp.mean(dn * nrm, axis=-1, keepdims=True))
            vec_ref[3:4, sl] += _colsum(dh1)
            dh1_ref[:, sl] = dh1
            return carry

        lax.fori_loop(0, d_model // LANES, group, 0, unroll=True)

    tile = pl.BlockSpec((ta, d_model), lambda i: (i, 0))
    row = pl.BlockSpec((1, d_model), lambda i: (0, 0))
    return pl.pallas_call(
        body, name="branch_a_bwd_norm", grid=(n_tok // ta,),
        in_specs=[ANY, pl.BlockSpec((ta, d_model), lambda i: (i, 2)), tile, tile, row, row],
        out_specs=[pl.BlockSpec((ta, d_model), lambda i: (i, 2)), tile,
                   pl.BlockSpec((A1_ROWS, d_model), lambda i: (0, 0))],
        out_shape=[SDS(dproj.shape, BF16), SDS((n_tok, d_model), F32), SDS((A1_ROWS, d_model), F32)],
        input_output_aliases={0: 0},
        compiler_params=_params(1),
    )(dproj, proj, h1, d_h3, gn_g, gn_b)


A2_ROWS = 8


def _branch_a_bwd_conv(dproj, proj, d_h1, conv_w_full, gp_bf, gp_f32, seq):
    n_tok = proj.shape[0]
    d_model = conv_w_full.shape[1]
    n_groups = d_model // LANES
    ta = _tile(seq, 256, HALO)
    n_tiles = n_tok // ta
    per_seq = seq // ta
    rc = _conv_rows(ta)
    last_halo = n_tok // HALO - 1
    r8 = d_model // N_DEV
    prow = _tile(r8, 32, 16)

    def body(dproj_in, av_ref, ag_ref, avh_ref, agh_ref, dh1_ref, dh1h_ref, cw_ref, gp_bf_ref, gp_f32_ref,
             dproj_ref, vec_ref, dcw_ref, opa_ref, opb_ref, opo_ref,
             ext_h0, ext_d, rbuf, own, send_sems, recv_sems, local_sems):
        del dproj_in
        i = pl.program_id(0)
        x, y, c = _mesh_pos()
        me = _block_of((x, y, c))
        peers = _peers()

        def sends():
            return [pltpu.make_async_remote_copy(
                src_ref=gp_bf_ref.at[a, pl.ds(pl.multiple_of(blk * r8, 16), r8), :], dst_ref=rbuf.at[k, a],
                send_sem=send_sems.at[a, k], recv_sem=recv_sems.at[a, k], device_id=pos, device_id_type=MESH)
                for k, (pos, blk) in enumerate(peers) for a in range(3)]

        def own_rows():
            return [pltpu.make_async_copy(gp_f32_ref.at[a, pl.ds(pl.multiple_of(me * r8, 8), r8), :],
                                          own.at[a], local_sems.at[a]) for a in range(3)]

        @pl.when(i == 0)
        def _():
            vec_ref[...] = jnp.zeros_like(vec_ref)
            dcw_ref[...] = jnp.zeros_like(dcw_ref)
            for cp in sends() + own_rows():
                cp.start()

        keep_past = jnp.where(i % per_seq == 0, 0.0, 1.0)
        keep_next = jnp.where(i % per_seq == per_seq - 1, 0.0, 1.0)

        def group(g, carry):
            sl = pl.ds(pl.multiple_of(g * LANES, LANES), LANES)
            av = av_ref[:, sl]
            sig = _sigmoid(ag_ref[:, sl])
            ext_h0[0:HALO, :] = avh_ref[:, sl] * _sigmoid(agh_ref[:, sl]) * keep_past
            ext_h0[HALO:HALO + ta, :] = av * sig
            ext_d[0:ta, :] = dh1_ref[:, sl]
            ext_d[ta:ta + HALO, :] = dh1h_ref[:, sl] * keep_next
            for r0 in range(0, ta, rc):
                dh1 = ext_d[pl.ds(r0, rc), :]
                acc = jnp.zeros((rc, LANES), F32)
                for k in range(CONV_K):
                    acc = acc + ext_d[pl.ds(r0 + CONV_K - 1 - k, rc), :] * cw_ref[k:k + 1, sl]
                    prod = dh1 * ext_h0[pl.ds(r0 + HALO - (CONV_K - 1) + k, rc), :]
                    dcw_ref[g, k] += jnp.sum(prod.reshape(rc // SUBLANES, SUBLANES, LANES), axis=0)
                rows = pl.ds(r0, rc)
                sig_r = sig[r0:r0 + rc]
                dav = acc * sig_r
                dag = acc * av[r0:r0 + rc] * sig_r * (1.0 - sig_r)
                dproj_ref[rows, sl] = dav.astype(BF16)
                dproj_ref[rows, pl.ds(pl.multiple_of(d_model + g * LANES, LANES), LANES)] = dag.astype(BF16)
                vec_ref[0:1, sl] += _colsum(dav)
                vec_ref[1:2, sl] += _colsum(dag)
            return carry

        lax.fori_loop(0, n_groups, group, 0, unroll=True)

        @pl.when(i == n_tiles - 1)
        def _():
            for cp in own_rows():
                cp.wait()
            for cp in sends():
                cp.wait_recv()
            for a, o in enumerate([opa_ref, opb_ref, opo_ref]):
                for q in range(r8 // prow):
                    r = pl.ds(q * prow, prow)
                    tot = own[a, r, :]
                    for k in range(N_DEV - 1):
                        tot = tot + rbuf[k, a, r, :].astype(F32)
                    o[r, :] = tot
            for cp in sends():
                cp.wait_send()

    blk = lambda j: pl.BlockSpec((ta, d_model), lambda i: (i, j))
    halo = lambda j: pl.BlockSpec((HALO, d_model), lambda i: (jnp.maximum(i * (ta // HALO) - 1, 0), j))
    shard = pl.BlockSpec((r8, d_model), lambda i: (0, 0))
    return pl.pallas_call(
        body, name="branch_a_bwd_conv", grid=(n_tiles,),
        in_specs=[ANY, blk(0), blk(1), halo(0), halo(1), pl.BlockSpec((ta, d_model), lambda i: (i, 0)),
                  pl.BlockSpec((HALO, d_model), lambda i: (jnp.minimum((i + 1) * (ta // HALO), last_halo), 0)),
                  pl.BlockSpec((HALO, d_model), lambda i: (0, 0)), ANY, ANY],
        out_specs=[pl.BlockSpec((ta, 2 * d_model), lambda i: (i, 0)),
                   pl.BlockSpec((A2_ROWS, d_model), lambda i: (0, 0)),
                   pl.BlockSpec((n_groups, HALO, SUBLANES, LANES), lambda i: (0, 0, 0, 0)),
                   shard, shard, shard],
        out_shape=[SDS(dproj.shape, BF16), SDS((A2_ROWS, d_model), F32),
                   SDS((n_groups, HALO, SUBLANES, LANES), F32)] + [SDS((r8, d_model), F32)] * 3,
        scratch_shapes=[pltpu.VMEM((HALO + ta, LANES), F32), pltpu.VMEM((ta + HALO, LANES), F32),
                        pltpu.VMEM((N_DEV - 1, 3, r8, d_model), BF16), pltpu.VMEM((3, r8, d_model), F32),
                        pltpu.SemaphoreType.DMA((3, 7)), pltpu.SemaphoreType.DMA((3, 7)),
                        pltpu.SemaphoreType.DMA((3,))],
        input_output_aliases={0: 0},
        compiler_params=_params(1),
    )(dproj, proj, proj, proj, proj, d_h1, d_h1, conv_w_full, gp_bf, gp_f32)


def _weight_grads(lhs3, rhs3):
    n_mat, n_tok, d_model = lhs3.shape
    tk = _tile(n_tok, 1024, 16)
    n_k = n_tok // tk

    def body(a_ref, g_ref, o_ref, ob_ref):
        part = _dot_ta(a_ref[...], g_ref[...])

        @pl.when(pl.program_id(1) == 0)
        def _():
            o_ref[...] = part

        @pl.when(pl.program_id(1) != 0)
        def _():
            o_ref[...] += part

        @pl.when(pl.program_id(1) == n_k - 1)
        def _():
            ob_ref[...] = o_ref[...].astype(BF16)

    tile = pl.BlockSpec((None, tk, d_model), lambda a, i: (a, i, 0))
    out = pl.BlockSpec((None, d_model, d_model), lambda a, i: (a, 0, 0))
    return pl.pallas_call(
        body, name="grad_w_pa_pb_o", grid=(n_mat, n_k), in_specs=[tile, tile], out_specs=[out, out],
        out_shape=[SDS((n_mat, d_model, d_model), F32), SDS((n_mat, d_model, d_model), BF16)],
        compiler_params=_params(2),
    )(lhs3, rhs3)


def _grad_w_in_reduce_scatter(xt_bf, dproj, blk_order, small_part):
    n_tok = dproj.shape[0]
    d_model = xt_bf.shape[0]
    dh = d_model // 2
    n_units = 2 * N_DEV
    rsl = small_part.shape[0] // N_DEV

    def body(ord_ref, a_ref, g_ref, p_ref, o_ref, small_ref, acc, fb, sbuf, gbuf, tbuf, rfin, rbuf_s, red,
             send_f, send_s, recv_g, recv_t, recv_f, out_sems, send1, recv1, send2, recv2):
        del ord_ref
        u = pl.program_id(0)
        s = u // 2
        hf = u % 2
        rnd = s // 2
        x, y, c = _mesh_pos()
        sibling = (x, y, 1 - c)
        me = _block_of((x, y, c))
        peers = _peers()

        def rows_of(blk):
            return pl.ds(pl.multiple_of(blk * rsl, SUBLANES), rsl)

        def scatter():
            return [pltpu.make_async_remote_copy(
                src_ref=p_ref.at[rows_of(blk), :], dst_ref=rbuf_s.at[k], send_sem=send1.at[k],
                recv_sem=recv1.at[k], device_id=pos, device_id_type=MESH) for k, (pos, blk) in enumerate(peers)]

        def gather(dst_block=None):
            return [pltpu.make_async_remote_copy(
                src_ref=red, dst_ref=small_ref.at[rows_of(me if dst_block is None else blk), :],
                send_sem=send2.at[k], recv_sem=recv2.at[k], device_id=pos, device_id_type=MESH)
                for k, (pos, blk) in enumerate(peers)]

        def own_slice():
            return pltpu.make_async_copy(red, small_ref.at[rows_of(me), :], out_sems.at[2])

        @pl.when(u == 0)
        def _():
            for cp in scatter():
                cp.start()

        @pl.when(u == 4)
        def _():
            for cp in scatter():
                cp.wait_recv()
            tot = p_ref[rows_of(me), :]
            for k in range(N_DEV - 1):
                tot = tot + rbuf_s[k]
            red[...] = tot
            own_slice().start()
            for cp in gather():
                cp.start()

        n1 = (jnp.bitwise_xor(x, c), jnp.bitwise_xor(y, 1 - c), c)
        n2 = (jnp.bitwise_xor(x, 1 - c), jnp.bitwise_xor(y, c), c)

        def feed(r, half):
            return pltpu.make_async_remote_copy(
                src_ref=fb.at[half], dst_ref=gbuf.at[r, half], send_sem=send_f.at[r, half],
                recv_sem=recv_g.at[r, half], device_id=sibling, device_id_type=MESH)

        def feed_sibling(half):
            return pltpu.make_async_remote_copy(
                src_ref=fb.at[half], dst_ref=rfin.at[0, half], send_sem=send_f.at[3, half],
                recv_sem=recv_f.at[0, half], device_id=sibling, device_id_type=MESH)

        def chip_sum(r, half):
            dst = [tbuf.at[half], rfin.at[2, half], rfin.at[1, half]][r]
            sem = [recv_t.at[half], recv_f.at[2, half], recv_f.at[1, half]][r]
            return pltpu.make_async_remote_copy(
                src_ref=sbuf.at[r, half], dst_ref=dst, send_sem=send_s.at[r, half], recv_sem=sem,
                device_id=[n2, n2, n1][r], device_id_type=MESH)

        def out_copy(half):
            return pltpu.make_async_copy(acc.at[half], o_ref.at[:, pl.ds(half * dh, dh)], out_sems.at[half])

        def partial_sum():
            return _dot(a_ref[:, 0:n_tok], g_ref[...])

        for half in range(2):
            for r in range(3):
                @pl.when(u == 4 * r + 4 + half)
                def _(r=r, half=half):
                    feed(r, half).wait_send()

                @pl.when(u == 4 * r + 2 + half)
                def _(r=r, half=half):
                    feed(r, half).wait_recv()
                    if r == 2:
                        chip_sum(0, half).wait_recv()

            @pl.when(u == 14 + half)
            def _(half=half):
                feed_sibling(half).wait_recv()
                chip_sum(2, half).wait_recv()
                chip_sum(1, half).wait_recv()

        @pl.when(jnp.logical_and(s % 2 == 0, s < 7))
        def _():
            fb[hf] = partial_sum().astype(BF16)

        @pl.when(jnp.logical_or(s == 1, s == 3))
        def _():
            sbuf[rnd, hf] = (partial_sum() + gbuf[rnd, hf].astype(F32)).astype(BF16)

        @pl.when(s == 5)
        def _():
            sbuf[2, hf] = (partial_sum() + gbuf[2, hf].astype(F32) + tbuf[hf].astype(F32)).astype(BF16)

        @pl.when(s == 7)
        def _():
            acc[hf] = (partial_sum() + rfin[0, hf].astype(F32) + rfin[1, hf].astype(F32)
                       + rfin[2, hf].astype(F32))

        for half in range(2):
            for r in range(3):
                @pl.when(u == 4 * r + half)
                def _(r=r, half=half):
                    feed(r, half).start()

                @pl.when(u == 4 * r + 2 + half)
                def _(r=r, half=half):
                    chip_sum(r, half).start()

            @pl.when(u == 12 + half)
            def _(half=half):
                feed_sibling(half).start()

        @pl.when(u == 14)
        def _():
            out_copy(0).start()

        @pl.when(u == 15)
        def _():
            out_copy(1).start()
            for half in range(2):
                feed_sibling(half).wait_send()
                for r in range(3):
                    chip_sum(r, half).wait_send()
                out_copy(half).wait()
            for cp in gather("theirs"):
                cp.wait_recv()
            for cp in scatter() + gather():
                cp.wait_send()
            own_slice().wait()

    grid_spec = pltpu.PrefetchScalarGridSpec(
        num_scalar_prefetch=1, grid=(n_units,),
        in_specs=[VMEM, pl.BlockSpec((n_tok, dh), lambda u, o: (0, 2 * o[u // 2] + u % 2)), VMEM],
        out_specs=[ANY, ANY],
        scratch_shapes=[pltpu.VMEM((2, d_model, dh), F32), pltpu.VMEM((2, d_model, dh), BF16),
                        pltpu.VMEM((3, 2, d_model, dh), BF16), pltpu.VMEM((3, 2, d_model, dh), BF16),
                        pltpu.VMEM((2, d_model, dh), BF16), pltpu.VMEM((3, 2, d_model, dh), BF16),
                        pltpu.VMEM((N_DEV - 1, rsl, LANES), F32), pltpu.VMEM((rsl, LANES), F32),
                        pltpu.SemaphoreType.DMA((4, 2)), pltpu.SemaphoreType.DMA((3, 2)),
                        pltpu.SemaphoreType.DMA((3, 2)), pltpu.SemaphoreType.DMA((2,)),
                        pltpu.SemaphoreType.DMA((3, 2)),
                        pltpu.SemaphoreType.DMA((3,))] + [pltpu.SemaphoreType.DMA((N_DEV - 1,))] * 4)
    return pl.pallas_call(
        body, name="grad_w_in_reduce_scatter", grid_spec=grid_spec,
        out_shape=[SDS((d_model, d_model), F32), SDS(small_part.shape, F32)], compiler_params=_params(1),
    )(blk_order, xt_bf, dproj, small_part)


def _grad_x_adamw(dproj, w_all, dr, w, g, m, v, groups):
    n_tok, d_model = dr.shape
    tm = _tile(n_tok, 256, 16)
    n_steps = n_tok // tm
    tr = w.shape[0] // n_steps
    n_grp = len(groups)

    def body(dp_ref, w_hbm, dr_ref, ws_ref, g_ref, m_ref, v_ref, *rest):
        grp_in, rest = rest[:4 * n_grp], rest[4 * n_grp:]
        o_ref, d_ref, mo_ref, vo_ref = rest[:4]
        grp_out, (w_ref, w_sems) = rest[4:4 + 3 * n_grp], rest[4 + 3 * n_grp:]
        i = pl.program_id(0)

        def fetch(j):
            return pltpu.make_async_copy(w_hbm.at[j], w_ref.at[j], w_sems.at[j])

        def grad_x_tile(first):
            acc = DEEPNORM_ALPHA * dr_ref[...]
            for j in range(N_DEV):
                if first:
                    fetch(j).wait()
                acc = acc + _dot_tb(dp_ref[:, j * d_model:(j + 1) * d_model], w_ref[j])
            o_ref[...] = acc

        @pl.when(i == 0)
        def _():
            for j in range(N_DEV):
                fetch(j).start()
            grad_x_tile(True)

        @pl.when(i != 0)
        def _():
            grad_x_tile(False)

        d_ref[...], mo_ref[...], vo_ref[...] = _adamw_math(ws_ref[...], g_ref[...], m_ref[...], v_ref[...])

        @pl.when(i == n_steps - 1)
        def _():
            for p in range(n_grp):
                pw, pg, pm, pv = grp_in[4 * p:4 * p + 4]
                grp_out[3 * p][...], grp_out[3 * p + 1][...], grp_out[3 * p + 2][...] = _adamw_math(
                    pw[...], pg[...], pm[...], pv[...])

    tile = pl.BlockSpec((tm, d_model), lambda i: (i, 0))
    slab = pl.BlockSpec((tr, w.shape[1]), lambda i: (i, 0))
    flat = [a for grp in groups for a in grp]
    res = pl.pallas_call(
        body, name="grad_x_adamw", grid=(n_steps,),
        in_specs=[pl.BlockSpec((tm, N_DEV * d_model), lambda i: (i, 0)), ANY, tile, slab, slab, slab, slab]
        + [VMEM] * (4 * n_grp),
        out_specs=[tile, slab, slab, slab] + [VMEM] * (3 * n_grp),
        out_shape=[SDS((n_tok, d_model), F32)] + [SDS(w.shape, F32)] * 3
        + [SDS(grp[0].shape, F32) for grp in groups for _ in range(3)],
        scratch_shapes=[pltpu.VMEM(w_all.shape, BF16), pltpu.SemaphoreType.DMA((N_DEV,))],
        compiler_params=_params(1),
    )(dproj, w_all, dr, w, g, m, v, *flat)
    return res[0], tuple(res[1:4]), [tuple(res[4 + 3 * p:7 + 3 * p]) for p in range(n_grp)]


def _adamw_math(w, g, m, v):
    m = ADAM_B1 * m + (1.0 - ADAM_B1) * g
    v = ADAM_B2 * v + (1.0 - ADAM_B2) * (g * g)
    m_hat = m / (1.0 - ADAM_B1 ** ADAM_STEP)
    v_hat = v / (1.0 - ADAM_B2 ** ADAM_STEP)
    delta = -ADAM_LR * (m_hat / (jnp.sqrt(v_hat) + ADAM_EPS) + ADAM_WD * w)
    return delta, m, v


def _as_rows(a):
    return a.reshape(-1, LANES)


def kernel(x, w_in, b_in, conv_w, conv_b, gn_g, gn_b, ln_v_g, ln_v_b, w_spatial, b_spatial, w_pa, w_pb, w_o, b_o, ln_out_g, ln_out_b, loss_target, m_w_in, m_b_in, m_conv_w, m_conv_b, m_gn_g, m_gn_b, m_ln_v_g, m_ln_v_b, m_w_spatial, m_b_spatial, m_w_pa, m_w_pb, m_w_o, m_b_o, m_ln_out_g, m_ln_out_b, v_w_in, v_b_in, v_conv_w, v_conv_b, v_gn_g, v_gn_b, v_ln_v_g, v_ln_v_b, v_w_spatial, v_b_spatial, v_w_pa, v_w_pb, v_w_o, v_b_o, v_ln_out_g, v_ln_out_b):
    n_batch, seq, d_model = x.shape
    n_tok = n_batch * seq
    n_heads = d_model // LANES
    dc = conv_w.shape[1]
    me = 4 * lax.axis_index("x") + 2 * lax.axis_index("y") + lax.axis_index("c")
    row = lambda a: a.reshape(1, d_model)

    x2 = x.reshape(n_tok, d_model)
    target2 = loss_target.reshape(n_tok, d_model)
    b_spatial_t = b_spatial.T

    first = jnp.where(lax.axis_index("c") == 1, 4, 2)
    second = 6 - first
    ag_rel = jnp.stack([0 * first, 0 * first + 1, first, second + 1, second, first + 1, 0 * first + 6, 0 * first + 7])
    ag_blocks = jnp.bitwise_xor(me, ag_rel).astype(jnp.int32)
    proj, xt_bf, w_all, wp_all, cw_all = _proj_all_gather(
        x2, w_in, w_pa, w_pb, w_o, conv_w, b_in.reshape(N_DEV, 1, d_model), ag_blocks)
    wp_full = wp_all.reshape(3, d_model, d_model)
    conv_w_full = jnp.pad(cw_all.transpose(1, 0, 2).reshape(CONV_K, d_model), ((0, HALO - CONV_K), (0, 0)))

    h3, h1 = _branch_a_fwd(proj, conv_w_full, row(conv_b), row(gn_g), row(gn_b), seq)
    s = _branch_b_fwd(proj, row(ln_v_g), row(ln_v_b), w_spatial, b_spatial_t, seq)

    dproj, d_h3, d_s, dr, lhs3, rhs3, vec_mid = _mid(
        h3, s, proj, x2, target2, wp_full, row(b_o), row(ln_out_g), row(ln_out_b))

    dproj, vec_b, d_ws, d_bs_t = _branch_b_bwd(dproj, proj, d_s, row(ln_v_g), row(ln_v_b), w_spatial, b_spatial_t, seq)
    dproj, d_h1, vec_a1 = _branch_a_bwd_norm(dproj, proj, h1, d_h3, row(gn_g), row(gn_b), seq)
    gp_f32, gp_bf = _weight_grads(lhs3, rhs3)
    dproj, vec_a2, d_cw8, g_w_pa, g_w_pb, g_w_o = _branch_a_bwd_conv(
        dproj, proj, d_h1, conv_w_full, gp_bf, gp_f32, seq)

    d_cw = jnp.sum(d_cw8, axis=2)
    pieces = [
        _as_rows(jnp.concatenate([vec_a2[0:2], vec_a1[0:1], vec_b[0:3], vec_mid[3:5]], axis=0)),
        _as_rows(jnp.concatenate([vec_a1[3:4], vec_a1[1:3], vec_b[3:5], vec_mid[2:3], vec_mid[0:2]], axis=0)),
        _as_rows(d_bs_t.T), _as_rows(d_ws), _as_rows(d_cw), _as_rows(vec_mid[5:6]),
    ]
    n_rows = sum(p.shape[0] for p in pieces)
    pad_rows = -n_rows % (N_DEV * SUBLANES)
    small_part = jnp.concatenate(pieces + [jnp.zeros((pad_rows, LANES), F32)], axis=0)

    rs_rel = jnp.stack([0 * first + 7, 0 * first + 6, first + 1, second, second + 1, first, 0 * first + 1, 0 * first])
    rs_blocks = jnp.bitwise_xor(me, rs_rel).astype(jnp.int32)
    g_w_in, small = _grad_w_in_reduce_scatter(xt_bf, dproj, rs_blocks, small_part)

    g_rows = d_model // LANES
    o0 = N_DEV * g_rows
    g_b_in = small[0:o0].reshape(N_DEV * d_model)
    vecs = [small[o0 + a * g_rows:o0 + (a + 1) * g_rows].reshape(d_model) for a in range(8)]
    g_conv_b, g_gn_g, g_gn_b, g_ln_v_g, g_ln_v_b, g_b_o, g_ln_out_g, g_ln_out_b = vecs
    o1 = o0 + 8 * g_rows
    g_b_spatial = small[o1:o1 + n_heads].reshape(n_heads, LANES)
    o2 = o1 + n_heads
    g_w_spatial = small[o2:o2 + n_heads * LANES].reshape(n_heads, LANES, LANES)
    o3 = o2 + n_heads * LANES
    g_cw_full = small[o3:o3 + n_heads * HALO].reshape(n_heads, HALO, LANES).transpose(1, 0, 2).reshape(HALO, d_model)
    g_conv_w = lax.dynamic_slice(g_cw_full, (0, me * dc), (CONV_K, dc))
    o4 = o3 + n_heads * HALO
    loss = jnp.sum(small[o4:o4 + g_rows]) * (0.5 / d_model)

    two_d = lambda a: a.reshape(-1, a.shape[-1]) if a.ndim != 1 else (
        a.reshape(-1, LANES) if a.shape[0] % LANES == 0 else a.reshape(1, -1))
    names = ["b_in", "conv_w", "conv_b", "gn_g", "gn_b", "ln_v_g", "ln_v_b", "w_spatial", "b_spatial",
             "w_pa", "w_pb", "w_o", "b_o", "ln_out_g", "ln_out_b"]
    ws = dict(b_in=b_in, conv_w=conv_w, conv_b=conv_b, gn_g=gn_g, gn_b=gn_b, ln_v_g=ln_v_g, ln_v_b=ln_v_b,
              w_spatial=w_spatial, b_spatial=b_spatial, w_pa=w_pa, w_pb=w_pb, w_o=w_o, b_o=b_o,
              ln_out_g=ln_out_g, ln_out_b=ln_out_b)
    gs = dict(b_in=g_b_in, conv_w=g_conv_w, conv_b=g_conv_b, gn_g=g_gn_g, gn_b=g_gn_b, ln_v_g=g_ln_v_g,
              ln_v_b=g_ln_v_b, w_spatial=g_w_spatial, b_spatial=g_b_spatial, w_pa=g_w_pa, w_pb=g_w_pb,
              w_o=g_w_o, b_o=g_b_o, ln_out_g=g_ln_out_g, ln_out_b=g_ln_out_b)
    ms = dict(b_in=m_b_in, conv_w=m_conv_w, conv_b=m_conv_b, gn_g=m_gn_g, gn_b=m_gn_b, ln_v_g=m_ln_v_g,
              ln_v_b=m_ln_v_b, w_spatial=m_w_spatial, b_spatial=m_b_spatial, w_pa=m_w_pa, w_pb=m_w_pb,
              w_o=m_w_o, b_o=m_b_o, ln_out_g=m_ln_out_g, ln_out_b=m_ln_out_b)
    vs = dict(b_in=v_b_in, conv_w=v_conv_w, conv_b=v_conv_b, gn_g=v_gn_g, gn_b=v_gn_b, ln_v_g=v_ln_v_g,
              ln_v_b=v_ln_v_b, w_spatial=v_w_spatial, b_spatial=v_b_spatial, w_pa=v_w_pa, w_pb=v_w_pb,
              w_o=v_w_o, b_o=v_b_o, ln_out_g=v_ln_out_g, ln_out_b=v_ln_out_b)
    grad_x, (d_w_in, nm_w_in, nv_w_in), upd = _grad_x_adamw(
        dproj, w_all, dr, w_in, g_w_in, m_w_in, v_w_in,
        [tuple(two_d(d[n]) for d in (ws, gs, ms, vs)) for n in names])
    grad_x = grad_x.reshape(x.shape)
    delta = {n: u[0].reshape(ws[n].shape) for n, u in zip(names, upd)}
    new_m = {n: u[1].reshape(ws[n].shape) for n, u in zip(names, upd)}
    new_v = {n: u[2].reshape(ws[n].shape) for n, u in zip(names, upd)}
    gs["w_in"], delta["w_in"], new_m["w_in"], new_v["w_in"] = g_w_in, d_w_in, nm_w_in, nv_w_in

    order = ["w_in"] + names
    return (loss, grad_x, *[gs[n] for n in order], *[delta[n] for n in order],
            *[new_m[n] for n in order], *[new_v[n] for n in order])
```

```python
import jax
import jax.numpy as jnp
from jax import lax
from jax.experimental import pallas as pl
from jax.experimental.pallas import tpu as pltpu

F32 = jnp.float32
BF16 = jnp.bfloat16
SDS = jax.ShapeDtypeStruct

N_DEV = 8
LANES = 128
SUBLANES = 8
CONV_K = 31
HALO = 32
LN_EPS = 1e-5
DEEPNORM_ALPHA = 2.0 ** 0.25
ADAM_LR, ADAM_B1, ADAM_B2, ADAM_EPS, ADAM_WD, ADAM_STEP = 0.001, 0.9, 0.999, 1e-08, 0.01, 10
GELU_C = 0.7978845608028654
GELU_A = 0.044715
VMEM_LIMIT = 56 * 1024 * 1024
MESH = pl.DeviceIdType.MESH
ANY = pl.BlockSpec(memory_space=pl.ANY)
VMEM = pl.BlockSpec(memory_space=pltpu.VMEM)


def _params(n_grid=0):
    sem = ("arbitrary",) * n_grid if n_grid else None
    return pltpu.CompilerParams(dimension_semantics=sem, vmem_limit_bytes=VMEM_LIMIT)


def _tile(n, pref, mult):
    t = min(n, pref)
    while n % t or t % mult:
        t -= 1
    return t


def _colsum(v):
    return jnp.sum(v, axis=0, keepdims=True)


def _sigmoid(v):
    return jax.nn.sigmoid(v)


def _silu_and_grad(v):
    s = _sigmoid(v)
    return v * s, s * (1.0 + v * (1.0 - s))


def _gelu_and_grad(v):
    v2 = v * v
    sg = _sigmoid(v * (2.0 * GELU_C + (2.0 * GELU_C * GELU_A) * v2))
    grad = sg + v * sg * (1.0 - sg) * (2.0 * GELU_C + (6.0 * GELU_C * GELU_A) * v2)
    return v * sg, grad


def _tril_mask():
    r = lax.broadcasted_iota(jnp.int32, (LANES, LANES), 0)
    c = lax.broadcasted_iota(jnp.int32, (LANES, LANES), 1)
    return c <= r


def _dot(a, b):
    return jnp.dot(a, b, preferred_element_type=F32)


def _dot_tb(a, b):
    return lax.dot_general(a, b, (((1,), (1,)), ((), ())), preferred_element_type=F32)


def _dot_ta(a, b):
    return lax.dot_general(a, b, (((0,), (0,)), ((), ())), preferred_element_type=F32)


def _mesh_pos():
    return lax.axis_index("x"), lax.axis_index("y"), lax.axis_index("c")


def _block_of(pos):
    return 4 * pos[0] + 2 * pos[1] + pos[2]


def _peers():
    x, y, c = _mesh_pos()
    out = []
    for k in range(1, N_DEV):
        pos = (1 - x if k & 4 else x, 1 - y if k & 2 else y, 1 - c if k & 1 else c)
        out.append((pos, _block_of(pos)))
    return out


def _proj_all_gather(x2, w_in, w_pa, w_pb, w_o, conv_w, b_in3, blk_order):
    n_tok, d_model = x2.shape
    r8 = w_pa.shape[0]
    kc, dc = conv_w.shape
    tm = _tile(n_tok, 1024, LANES)
    n_t = n_tok // tm
    n_arr = 3

    def body(ord_ref, x_ref, b_ref, win_ref, wpa_ref, wpb_ref, wo_ref, cw_ref,
             proj_ref, xt_ref, wall_ref, wp_ref, cwall_ref,
             wbuf, xbuf, st_p, send_sems, recv_sems, local_sems, wall_sems):
        s = pl.program_id(0)
        t = pl.program_id(1)
        x, y, c = _mesh_pos()
        me = (x, y, c)
        sibling = (x, y, 1 - c)
        n1 = (jnp.bitwise_xor(x, c), jnp.bitwise_xor(y, 1 - c))
        n2 = (jnp.bitwise_xor(x, 1 - c), jnp.bitwise_xor(y, c))
        dg = (1 - x, 1 - y)
        outs = [wbuf, wp_ref, cwall_ref]
        srcs = [None, st_p, cw_ref]
        consumed = [me, sibling, (*n1, c), (*n2, 1 - c), (*n2, c), (*n1, 1 - c), (*dg, c), (*dg, 1 - c)]
        leaves = [(me, sibling), (me, (*n1, c)), (me, (*n2, c)), ((*n1, c), (*n2, c)),
                  ((*n1, c), sibling), ((*n2, c), sibling), ((*dg, c), sibling)]
        lands = [sibling, (*n1, c), (*n2, c), (*dg, c), (*n2, 1 - c), (*n1, 1 - c), (*dg, 1 - c)]

        def slot(o, pos):
            return o.at[:, _block_of(pos)] if o is wp_ref else o.at[_block_of(pos)]

        def copy(a, k, block, to, src=None):
            o = outs[a]
            return pltpu.make_async_remote_copy(
                src_ref=slot(o, block) if src is None else src, dst_ref=slot(o, block),
                send_sem=send_sems.at[a, k], recv_sem=recv_sems.at[a, k],
                device_id=to, device_id_type=MESH)

        def send(a, k):
            block, to = leaves[k]
            return copy(a, k, block, to, src=srcs[a] if k < 3 else None)

        def recv(a, k):
            return copy(a, k, lands[k], me)

        def local_copies():
            return [pltpu.make_async_copy(srcs[a], slot(outs[a], me), local_sems.at[a]) for a in (1, 2)]

        def to_hbm(step):
            return pltpu.make_async_copy(slot(wbuf, consumed[step]), slot(wall_ref, consumed[step]),
                                         wall_sems.at[step])

        def at_step(step):
            return pl.when(jnp.logical_and(s == step, t == 0))

        @at_step(0)
        def _():
            slot(wbuf, me)[...] = win_ref[...].astype(BF16)
            st_p[0] = wpa_ref[...].astype(BF16)
            st_p[1] = wpb_ref[...].astype(BF16)
            st_p[2] = wo_ref[...].astype(BF16)
            for a in range(n_arr):
                send(a, 0).start()
                send(a, 1).start()
            for cp in local_copies():
                cp.start()
            to_hbm(0).start()

        @at_step(1)
        def _():
            recv(0, 0).wait_recv()
            to_hbm(1).start()

        for rnd in range(3):
            @at_step(2 + 2 * rnd)
            def _(rnd=rnd):
                if rnd == 0:
                    for a in range(n_arr):
                        send(a, 2).start()
                recv(0, 1 + rnd).wait_recv()
                if rnd == 0:
                    send(0, 3).start()
                send(0, 4 + rnd).start()
                to_hbm(2 + 2 * rnd).start()

            @at_step(3 + 2 * rnd)
            def _(rnd=rnd):
                for a in (1, 2):
                    recv(a, 1 + rnd).wait_recv()
                    if rnd == 0:
                        send(a, 3).start()
                    send(a, 4 + rnd).start()
                recv(0, 4 + rnd).wait_recv()
                to_hbm(3 + 2 * rnd).start()

        rows = pl.ds(pl.multiple_of(t * tm, tm), tm)

        @pl.when(s == 0)
        def _():
            xb = x_ref[...].astype(BF16)
            xbuf[rows, :] = xb
            xt_ref[...] = xb.T

        proj_ref[...] = _dot(xbuf[rows, :], wbuf[ord_ref[s]]) + b_ref[...]

        @pl.when(jnp.logical_and(s == N_DEV - 1, t == n_t - 1))
        def _():
            for a in (1, 2):
                for k in (0, 4, 5, 6):
                    recv(a, k).wait_recv()
            for a in range(n_arr):
                for k in range(7):
                    send(a, k).wait_send()
            for cp in local_copies() + [to_hbm(step) for step in range(N_DEV)]:
                cp.wait()

    grid_spec = pltpu.PrefetchScalarGridSpec(
        num_scalar_prefetch=1, grid=(N_DEV, n_t),
        in_specs=[pl.BlockSpec((tm, d_model), lambda s, t, o: (jnp.where(s == 0, t, n_t - 1), 0)),
                  pl.BlockSpec((None, 1, d_model), lambda s, t, o: (o[s], 0, 0)),
                  VMEM, VMEM, VMEM, VMEM, VMEM],
        out_specs=[pl.BlockSpec((tm, d_model), lambda s, t, o: (t, o[s])),
                   pl.BlockSpec((d_model, tm), lambda s, t, o: (0, jnp.where(s == 0, t, n_t))),
                   ANY, ANY, ANY],
        scratch_shapes=[pltpu.VMEM((N_DEV, d_model, d_model), BF16), pltpu.VMEM((n_tok, d_model), BF16),
                        pltpu.VMEM((3, r8, d_model), BF16),
                        pltpu.SemaphoreType.DMA((n_arr, 7)), pltpu.SemaphoreType.DMA((n_arr, 7)),
                        pltpu.SemaphoreType.DMA((3,)), pltpu.SemaphoreType.DMA((N_DEV,))])
    return pl.pallas_call(
        body, name="proj_all_gather", grid_spec=grid_spec,
        out_shape=[SDS((n_tok, N_DEV * d_model), F32), SDS((d_model, (n_t + 1) * tm), BF16),
                   SDS((N_DEV, d_model, d_model), BF16), SDS((3, N_DEV, r8, d_model), BF16),
                   SDS((N_DEV, kc, dc), F32)],
        compiler_params=_params(2),
    )(blk_order, x2, b_in3, w_in, w_pa, w_pb, w_o, conv_w)


def _conv_rows(ta):
    return _tile(ta, 64, SUBLANES)


def _branch_a_fwd(proj, conv_w_full, conv_b, gn_g, gn_b, seq):
    n_tok = proj.shape[0]
    d_model = conv_b.shape[1]
    ta = _tile(seq, 256, HALO)
    per_seq = seq // ta
    rc = _conv_rows(ta)

    def body(av_ref, ag_ref, gt_ref, avh_ref, agh_ref, cw_ref, cb_ref, gg_ref, gb_ref,
             h3_ref, h1_ref, ext):
        keep = jnp.where(pl.program_id(0) % per_seq == 0, 0.0, 1.0)

        def group(g, carry):
            sl = pl.ds(pl.multiple_of(g * LANES, LANES), LANES)
            ext[0:HALO, :] = avh_ref[:, sl] * _sigmoid(agh_ref[:, sl]) * keep
            ext[HALO:HALO + ta, :] = av_ref[:, sl] * _sigmoid(ag_ref[:, sl])
            for r0 in range(0, ta, rc):
                acc = jnp.broadcast_to(cb_ref[:, sl], (rc, LANES))
                for k in range(CONV_K):
                    acc = acc + ext[pl.ds(r0 + HALO - (CONV_K - 1) + k, rc), :] * cw_ref[k:k + 1, sl]
                h1_ref[pl.ds(r0, rc), sl] = acc
            h1 = h1_ref[:, sl]
            mu = jnp.mean(h1, axis=-1, keepdims=True)
            dlt = h1 - mu
            var = jnp.mean(dlt * dlt, axis=-1, keepdims=True)
            h2 = dlt * lax.rsqrt(var + LN_EPS) * gg_ref[:, sl] + gb_ref[:, sl]
            gate = gt_ref[:, sl]
            h3_ref[:, sl] = (h2 * _sigmoid(h2) * gate * _sigmoid(gate)).astype(BF16)
            return carry

        lax.fori_loop(0, d_model // LANES, group, 0, unroll=True)

    blk = lambda j: pl.BlockSpec((ta, d_model), lambda i: (i, j))
    halo = lambda j: pl.BlockSpec((HALO, d_model), lambda i: (jnp.maximum(i * (ta // HALO) - 1, 0), j))
    row = pl.BlockSpec((1, d_model), lambda i: (0, 0))
    return pl.pallas_call(
        body, name="branch_a_fwd", grid=(n_tok // ta,),
        in_specs=[blk(0), blk(1), blk(2), halo(0), halo(1),
                  pl.BlockSpec((HALO, d_model), lambda i: (0, 0)), row, row, row],
        out_specs=[pl.BlockSpec((ta, d_model), lambda i: (i, 0))] * 2,
        out_shape=[SDS((n_tok, d_model), BF16), SDS((n_tok, d_model), F32)],
        scratch_shapes=[pltpu.VMEM((HALO + ta, LANES), F32)],
        compiler_params=_params(1),
    )(proj, proj, proj, proj, proj, conv_w_full, conv_b, gn_g, gn_b)


def _branch_b_fwd(proj, ln_g, ln_b, w_spatial, b_spatial_t, seq):
    n_tok = proj.shape[0]
    d_model = ln_g.shape[1]
    n_heads = d_model // LANES
    tb = _tile(seq, 256, LANES)

    def body(u_ref, v_ref, bg_ref, lg_ref, lb_ref, ws_ref, bs_ref, s_ref, vn_buf):
        v, _ = _gelu_and_grad(v_ref[...])
        mu = jnp.mean(v, axis=-1, keepdims=True)
        dlt = v - mu
        var = jnp.mean(dlt * dlt, axis=-1, keepdims=True)
        vn_buf[...] = (dlt * lax.rsqrt(var + LN_EPS) * lg_ref[...] + lb_ref[...]).astype(BF16)
        tril = _tril_mask()
        for h in range(n_heads):
            cols = slice(h * LANES, (h + 1) * LANES)
            w_h = jnp.where(tril, ws_ref[h], 0.0).astype(BF16)
            bias = bs_ref[:, h:h + 1]
            for ch in range(tb // LANES):
                rows = slice(ch * LANES, (ch + 1) * LANES)
                mix = _dot(w_h, vn_buf[rows, cols]) + bias
                u, _ = _gelu_and_grad(u_ref[rows, cols])
                gate = bg_ref[rows, cols]
                s_ref[rows, cols] = (u * mix * gate * _sigmoid(gate)).astype(BF16)

    blk = lambda j: pl.BlockSpec((tb, d_model), lambda i: (i, j))
    row = pl.BlockSpec((1, d_model), lambda i: (0, 0))
    return pl.pallas_call(
        body, name="branch_b_fwd", grid=(n_tok // tb,),
        in_specs=[blk(3), blk(4), blk(5), row, row,
                  pl.BlockSpec((n_heads, LANES, LANES), lambda i: (0, 0, 0)),
                  pl.BlockSpec((LANES, n_heads), lambda i: (0, 0))],
        out_specs=pl.BlockSpec((tb, d_model), lambda i: (i, 0)),
        out_shape=SDS((n_tok, d_model), BF16),
        scratch_shapes=[pltpu.VMEM((tb, d_model), BF16)],
        compiler_params=_params(1),
    )(proj, proj, proj, ln_g, ln_b, w_spatial, b_spatial_t)


MID_ROWS = 8


def _mid(h3, s, proj, x2, target, wp_full, b_o, lo_g, lo_b):
    n_tok, d_model = x2.shape
    tm = _tile(n_tok, 256, 16)

    def body(h3_ref, s_ref, ma_ref, mb_ref, x_ref, t_ref, wpa_ref, wpb_ref, wo_ref, bo_ref,
             lg_ref, lb_ref, dproj_ref, dh3_ref, ds_ref, dr_ref, lhs3_ref, rhs3_ref, vec_ref):
        @pl.when(pl.program_id(0) == 0)
        def _():
            vec_ref[...] = jnp.zeros_like(vec_ref)

        h3 = h3_ref[...]
        s = s_ref[...]
        ya = _dot(h3, wpa_ref[...])
        yb = _dot(s, wpb_ref[...])
        ga = _sigmoid(ma_ref[...])
        gb = _sigmoid(mb_ref[...])
        mixed = (ga * ya + gb * yb).astype(BF16)
        lhs3_ref[0] = h3
        lhs3_ref[1] = s
        lhs3_ref[2] = mixed
        r = DEEPNORM_ALPHA * x_ref[...] + _dot(mixed, wo_ref[...]) + bo_ref[...]
        mu = jnp.mean(r, axis=-1, keepdims=True)
        dlt = r - mu
        rstd = lax.rsqrt(jnp.mean(dlt * dlt, axis=-1, keepdims=True) + LN_EPS)
        rhat = dlt * rstd
        diff = rhat * lg_ref[...] + lb_ref[...] - t_ref[...]
        dy = diff * (1.0 / d_model)
        vec_ref[0:1, :] += _colsum(dy * rhat)
        vec_ref[1:2, :] += _colsum(dy)
        vec_ref[5:6, :] += _colsum(diff * diff)
        drh = dy * lg_ref[...]
        dr = rstd * (drh - jnp.mean(drh, axis=-1, keepdims=True)
                     - rhat * jnp.mean(drh * rhat, axis=-1, keepdims=True))
        vec_ref[2:3, :] += _colsum(dr)
        dr_ref[...] = dr
        drb = dr.astype(BF16)
        rhs3_ref[2] = drb
        dmixed = _dot_tb(drb, wo_ref[...])
        dma = dmixed * ya * ga * (1.0 - ga)
        dmb = dmixed * yb * gb * (1.0 - gb)
        vec_ref[3:4, :] += _colsum(dma)
        vec_ref[4:5, :] += _colsum(dmb)
        dproj_ref[:, 0:d_model] = dma.astype(BF16)
        dproj_ref[:, d_model:2 * d_model] = dmb.astype(BF16)
        dya = (dmixed * ga).astype(BF16)
        dyb = (dmixed * gb).astype(BF16)
        rhs3_ref[0] = dya
        rhs3_ref[1] = dyb
        dh3_ref[...] = _dot_tb(dya, wpa_ref[...])
        ds_ref[...] = _dot_tb(dyb, wpb_ref[...])

    tile = pl.BlockSpec((tm, d_model), lambda i: (i, 0))
    full = lambda a: pl.BlockSpec((None, d_model, d_model), lambda i: (a, 0, 0))
    row = pl.BlockSpec((1, d_model), lambda i: (0, 0))
    stack = pl.BlockSpec((3, tm, d_model), lambda i: (0, i, 0))
    bf3 = SDS((3, n_tok, d_model), BF16)
    f32 = SDS((n_tok, d_model), F32)
    return pl.pallas_call(
        body, name="mid", grid=(n_tok // tm,),
        in_specs=[tile, tile, pl.BlockSpec((tm, d_model), lambda i: (i, 6)),
                  pl.BlockSpec((tm, d_model), lambda i: (i, 7)), tile, tile, full(0), full(1), full(2),
                  row, row, row],
        out_specs=[pl.BlockSpec((tm, 2 * d_model), lambda i: (i, 3)), tile, tile, tile, stack, stack,
                   pl.BlockSpec((MID_ROWS, d_model), lambda i: (0, 0))],
        out_shape=[SDS((n_tok, N_DEV * d_model), BF16), f32, f32, f32, bf3, bf3,
                   SDS((MID_ROWS, d_model), F32)],
        compiler_params=_params(1),
    )(h3, s, proj, proj, x2, target, wp_full, wp_full, wp_full, b_o, lo_g, lo_b)


B_ROWS = 8


def _branch_b_bwd(dproj, proj, d_s, ln_g, ln_b, w_spatial, b_spatial_t, seq):
    n_tok = proj.shape[0]
    d_model = ln_g.shape[1]
    n_heads = d_model // LANES
    tb = _tile(seq, 256, LANES)

    def body(dproj_in, u_ref, v_ref, bg_ref, ds_ref, lg_ref, lb_ref, ws_ref, bs_ref,
             dproj_ref, vec_ref, dws_ref, dbs_ref, vn_buf, dv_buf):
        del dproj_in

        @pl.when(pl.program_id(0) == 0)
        def _():
            vec_ref[...] = jnp.zeros_like(vec_ref)
            dws_ref[...] = jnp.zeros_like(dws_ref)
            dbs_ref[...] = jnp.zeros_like(dbs_ref)

        v, dgelu_v = _gelu_and_grad(v_ref[...])
        mu = jnp.mean(v, axis=-1, keepdims=True)
        dlt = v - mu
        rstd = lax.rsqrt(jnp.mean(dlt * dlt, axis=-1, keepdims=True) + LN_EPS)
        vhat = dlt * rstd
        vn_buf[...] = (vhat * lg_ref[...] + lb_ref[...]).astype(BF16)
        tril = _tril_mask()
        for h in range(n_heads):
            cols = slice(h * LANES, (h + 1) * LANES)
            w_h = jnp.where(tril, ws_ref[h], 0.0).astype(BF16)
            bias = bs_ref[:, h:h + 1]
            for ch in range(tb // LANES):
                rows = slice(ch * LANES, (ch + 1) * LANES)
                vn = vn_buf[rows, cols]
                mix = _dot(w_h, vn) + bias
                u, dgelu_u = _gelu_and_grad(u_ref[rows, cols])
                sg, dsilu = _silu_and_grad(bg_ref[rows, cols])
                dsv = ds_ref[rows, cols]
                du = dsv * mix * sg * dgelu_u
                dbg = dsv * u * mix * dsilu
                dmix = dsv * u * sg
                dmix_bf = dmix.astype(BF16)
                dproj_ref[rows, cols] = du.astype(BF16)
                dproj_ref[rows, 2 * d_model + h * LANES:2 * d_model + (h + 1) * LANES] = dbg.astype(BF16)
                vec_ref[0:1, cols] += _colsum(du)
                vec_ref[2:3, cols] += _colsum(dbg)
                dbs_ref[:, h:h + 1] += jnp.sum(dmix, axis=1, keepdims=True)
                dws_ref[h] += jnp.where(tril, _dot_tb(dmix_bf, vn), 0.0)
                dv_buf[rows, cols] = _dot_ta(w_h, dmix_bf)
        dvn = dv_buf[...]
        vec_ref[3:4, :] += _colsum(dvn * vhat)
        vec_ref[4:5, :] += _colsum(dvn)
        dvh = dvn * lg_ref[...]
        dv = rstd * (dvh - jnp.mean(dvh, axis=-1, keepdims=True)
                     - vhat * jnp.mean(dvh * vhat, axis=-1, keepdims=True)) * dgelu_v
        vec_ref[1:2, :] += _colsum(dv)
        dproj_ref[:, d_model:2 * d_model] = dv.astype(BF16)

    blk = lambda j: pl.BlockSpec((tb, d_model), lambda i: (i, j))
    row = pl.BlockSpec((1, d_model), lambda i: (0, 0))
    return pl.pallas_call(
        body, name="branch_b_bwd", grid=(n_tok // tb,),
        in_specs=[ANY, blk(3), blk(4), blk(5), pl.BlockSpec((tb, d_model), lambda i: (i, 0)), row, row,
                  pl.BlockSpec((n_heads, LANES, LANES), lambda i: (0, 0, 0)),
                  pl.BlockSpec((LANES, n_heads), lambda i: (0, 0))],
        out_specs=[pl.BlockSpec((tb, 3 * d_model), lambda i: (i, 1)),
                   pl.BlockSpec((B_ROWS, d_model), lambda i: (0, 0)),
                   pl.BlockSpec((n_heads, LANES, LANES), lambda i: (0, 0, 0)),
                   pl.BlockSpec((LANES, n_heads), lambda i: (0, 0))],
        out_shape=[SDS(dproj.shape, BF16), SDS((B_ROWS, d_model), F32),
                   SDS((n_heads, LANES, LANES), F32), SDS((LANES, n_heads), F32)],
        scratch_shapes=[pltpu.VMEM((tb, d_model), BF16), pltpu.VMEM((tb, d_model), F32)],
        input_output_aliases={0: 0},
        compiler_params=_params(1),
    )(dproj, proj, proj, proj, d_s, ln_g, ln_b, w_spatial, b_spatial_t)


A1_ROWS = 8


def _branch_a_bwd_norm(dproj, proj, h1, d_h3, gn_g, gn_b, seq):
    n_tok = proj.shape[0]
    d_model = gn_g.shape[1]
    ta = _tile(seq, 256, 16)

    def body(dproj_in, gt_ref, h1_ref, dh3_ref, gg_ref, gb_ref, dproj_ref, dh1_ref, vec_ref):
        del dproj_in

        @pl.when(pl.program_id(0) == 0)
        def _():
            vec_ref[...] = jnp.zeros_like(vec_ref)

        def group(g, carry):
            sl = pl.ds(pl.multiple_of(g * LANES, LANES), LANES)
            h1 = h1_ref[:, sl]
            mu = jnp.mean(h1, axis=-1, keepdims=True)
            dlt = h1 - mu
            rstd = lax.rsqrt(jnp.mean(dlt * dlt, axis=-1, keepdims=True) + LN_EPS)
            nrm = dlt * rstd
            sw, dsw = _silu_and_grad(nrm * gg_ref[:, sl] + gb_ref[:, sl])
            sg, dsg = _silu_and_grad(gt_ref[:, sl])
            dh3 = dh3_ref[:, sl]
            dgate = dh3 * sw * dsg
            dproj_ref[:, sl] = dgate.astype(BF16)
            vec_ref[0:1, sl] += _colsum(dgate)
            dh2 = dh3 * sg * dsw
            vec_ref[1:2, sl] += _colsum(dh2 * nrm)
            vec_ref[2:3, sl] += _colsum(dh2)
            dn = dh2 * gg_ref[:, sl]
            dh1 = rstd * (dn - jnp.mean(dn, axis=-1, keepdims=True)
                          - nrm * jnp.mean(dn * nrm, axis=-1, keepdims=True))
            vec_ref[3:4, sl] += _colsum(dh1)
            dh1_ref[:, sl] = dh1
            return carry

        lax.fori_loop(0, d_model // LANES, group, 0, unroll=True)

    tile = pl.BlockSpec((ta, d_model), lambda i: (i, 0))
    row = pl.BlockSpec((1, d_model), lambda i: (0, 0))
    return pl.pallas_call(
        body, name="branch_a_bwd_norm", grid=(n_tok // ta,),
        in_specs=[ANY, pl.BlockSpec((ta, d_model), lambda i: (i, 2)), tile, tile, row, row],
        out_specs=[pl.BlockSpec((ta, d_model), lambda i: (i, 2)), tile,
                   pl.BlockSpec((A1_ROWS, d_model), lambda i: (0, 0))],
        out_shape=[SDS(dproj.shape, BF16), SDS((n_tok, d_model), F32), SDS((A1_ROWS, d_model), F32)],
        input_output_aliases={0: 0},
        compiler_params=_params(1),
    )(dproj, proj, h1, d_h3, gn_g, gn_b)


A2_ROWS = 8


def _branch_a_bwd_conv(dproj, proj, d_h1, conv_w_full, gp_bf, gp_f32, seq):
    n_tok = proj.shape[0]
    d_model = conv_w_full.shape[1]
    n_groups = d_model // LANES
    ta = _tile(seq, 256, HALO)
    n_tiles = n_tok // ta
    per_seq = seq // ta
    rc = _conv_rows(ta)
    last_halo = n_tok // HALO - 1
    r8 = d_model // N_DEV
    prow = _tile(r8, 32, 16)

    def body(dproj_in, av_ref, ag_ref, avh_ref, agh_ref, dh1_ref, dh1h_ref, cw_ref, gp_bf_ref, gp_f32_ref,
             dproj_ref, vec_ref, dcw_ref, opa_ref, opb_ref, opo_ref,
             ext_h0, ext_d, rbuf, own, send_sems, recv_sems, local_sems):
        del dproj_in
        i = pl.program_id(0)
        x, y, c = _mesh_pos()
        me = _block_of((x, y, c))
        peers = _peers()

        def sends():
            return [pltpu.make_async_remote_copy(
                src_ref=gp_bf_ref.at[a, pl.ds(pl.multiple_of(blk * r8, 16), r8), :], dst_ref=rbuf.at[k, a],
                send_sem=send_sems.at[a, k], recv_sem=recv_sems.at[a, k], device_id=pos, device_id_type=MESH)
                for k, (pos, blk) in enumerate(peers) for a in range(3)]

        def own_rows():
            return [pltpu.make_async_copy(gp_f32_ref.at[a, pl.ds(pl.multiple_of(me * r8, 8), r8), :],
                                          own.at[a], local_sems.at[a]) for a in range(3)]

        @pl.when(i == 0)
        def _():
            vec_ref[...] = jnp.zeros_like(vec_ref)
            dcw_ref[...] = jnp.zeros_like(dcw_ref)
            for cp in sends() + own_rows():
                cp.start()

        keep_past = jnp.where(i % per_seq == 0, 0.0, 1.0)
        keep_next = jnp.where(i % per_seq == per_seq - 1, 0.0, 1.0)

        def group(g, carry):
            sl = pl.ds(pl.multiple_of(g * LANES, LANES), LANES)
            av = av_ref[:, sl]
            sig = _sigmoid(ag_ref[:, sl])
            ext_h0[0:HALO, :] = avh_ref[:, sl] * _sigmoid(agh_ref[:, sl]) * keep_past
            ext_h0[HALO:HALO + ta, :] = av * sig
            ext_d[0:ta, :] = dh1_ref[:, sl]
            ext_d[ta:ta + HALO, :] = dh1h_ref[:, sl] * keep_next
            for r0 in range(0, ta, rc):
                dh1 = ext_d[pl.ds(r0, rc), :]
                acc = jnp.zeros((rc, LANES), F32)
                for k in range(CONV_K):
                    acc = acc + ext_d[pl.ds(r0 + CONV_K - 1 - k, rc), :] * cw_ref[k:k + 1, sl]
                    prod = dh1 * ext_h0[pl.ds(r0 + HALO - (CONV_K - 1) + k, rc), :]
                    dcw_ref[g, k] += jnp.sum(prod.reshape(rc // SUBLANES, SUBLANES, LANES), axis=0)
                rows = pl.ds(r0, rc)
                sig_r = sig[r0:r0 + rc]
                dav = acc * sig_r
                dag = acc * av[r0:r0 + rc] * sig_r * (1.0 - sig_r)
                dproj_ref[rows, sl] = dav.astype(BF16)
                dproj_ref[rows, pl.ds(pl.multiple_of(d_model + g * LANES, LANES), LANES)] = dag.astype(BF16)
                vec_ref[0:1, sl] += _colsum(dav)
                vec_ref[1:2, sl] += _colsum(dag)
            return carry

        lax.fori_loop(0, n_groups, group, 0, unroll=True)

        @pl.when(i == n_tiles - 1)
        def _():
            for cp in own_rows():
                cp.wait()
            for cp in sends():
                cp.wait_recv()
            for a, o in enumerate([opa_ref, opb_ref, opo_ref]):
                for q in range(r8 // prow):
                    r = pl.ds(q * prow, prow)
                    tot = own[a, r, :]
                    for k in range(N_DEV - 1):
                        tot = tot + rbuf[k, a, r, :].astype(F32)
                    o[r, :] = tot
            for cp in sends():
                cp.wait_send()

    blk = lambda j: pl.BlockSpec((ta, d_model), lambda i: (i, j))
    halo = lambda j: pl.BlockSpec((HALO, d_model), lambda i: (jnp.maximum(i * (ta // HALO) - 1, 0), j))
    shard = pl.BlockSpec((r8, d_model), lambda i: (0, 0))
    return pl.pallas_call(
        body, name="branch_a_bwd_conv", grid=(n_tiles,),
        in_specs=[ANY, blk(0), blk(1), halo(0), halo(1), pl.BlockSpec((ta, d_model), lambda i: (i, 0)),
                  pl.BlockSpec((HALO, d_model), lambda i: (jnp.minimum((i + 1) * (ta // HALO), last_halo), 0)),
                  pl.BlockSpec((HALO, d_model), lambda i: (0, 0)), ANY, ANY],
        out_specs=[pl.BlockSpec((ta, 2 * d_model), lambda i: (i, 0)),
                   pl.BlockSpec((A2_ROWS, d_model), lambda i: (0, 0)),
                   pl.BlockSpec((n_groups, HALO, SUBLANES, LANES), lambda i: (0, 0, 0, 0)),
                   shard, shard, shard],
        out_shape=[SDS(dproj.shape, BF16), SDS((A2_ROWS, d_model), F32),
                   SDS((n_groups, HALO, SUBLANES, LANES), F32)] + [SDS((r8, d_model), F32)] * 3,
        scratch_shapes=[pltpu.VMEM((HALO + ta, LANES), F32), pltpu.VMEM((ta + HALO, LANES), F32),
                        pltpu.VMEM((N_DEV - 1, 3, r8, d_model), BF16), pltpu.VMEM((3, r8, d_model), F32),
                        pltpu.SemaphoreType.DMA((3, 7)), pltpu.SemaphoreType.DMA((3, 7)),
                        pltpu.SemaphoreType.DMA((3,))],
        input_output_aliases={0: 0},
        compiler_params=_params(1),
    )(dproj, proj, proj, proj, proj, d_h1, d_h1, conv_w_full, gp_bf, gp_f32)


def _weight_grads(lhs3, rhs3):
    n_mat, n_tok, d_model = lhs3.shape
    tk = _tile(n_tok, 1024, 16)
    n_k = n_tok // tk

    def body(a_ref, g_ref, o_ref, ob_ref):
        part = _dot_ta(a_ref[...], g_ref[...])

        @pl.when(pl.program_id(1) == 0)
        def _():
            o_ref[...] = part

        @pl.when(pl.program_id(1) != 0)
        def _():
            o_ref[...] += part

        @pl.when(pl.program_id(1) == n_k - 1)
        def _():
            ob_ref[...] = o_ref[...].astype(BF16)

    tile = pl.BlockSpec((None, tk, d_model), lambda a, i: (a, i, 0))
    out = pl.BlockSpec((None, d_model, d_model), lambda a, i: (a, 0, 0))
    return pl.pallas_call(
        body, name="grad_w_pa_pb_o", grid=(n_mat, n_k), in_specs=[tile, tile], out_specs=[out, out],
        out_shape=[SDS((n_mat, d_model, d_model), F32), SDS((n_mat, d_model, d_model), BF16)],
        compiler_params=_params(2),
    )(lhs3, rhs3)


def _grad_w_in_reduce_scatter(xt_bf, dproj, blk_order, small_part):
    n_tok = dproj.shape[0]
    d_model = xt_bf.shape[0]
    dh = d_model // 2
    n_units = 2 * N_DEV
    rsl = small_part.shape[0] // N_DEV

    def body(ord_ref, a_ref, g_ref, p_ref, o_ref, small_ref, acc, fb, sbuf, gbuf, tbuf, rfin, rbuf_s, red,
             send_f, send_s, recv_g, recv_t, recv_f, out_sems, send1, recv1, send2, recv2):
        del ord_ref
        u = pl.program_id(0)
        s = u // 2
        hf = u % 2
        rnd = s // 2
        x, y, c = _mesh_pos()
        sibling = (x, y, 1 - c)
        me = _block_of((x, y, c))
        peers = _peers()

        def rows_of(blk):
            return pl.ds(pl.multiple_of(blk * rsl, SUBLANES), rsl)

        def scatter():
            return [pltpu.make_async_remote_copy(
                src_ref=p_ref.at[rows_of(blk), :], dst_ref=rbuf_s.at[k], send_sem=send1.at[k],
                recv_sem=recv1.at[k], device_id=pos, device_id_type=MESH) for k, (pos, blk) in enumerate(peers)]

        def gather(dst_block=None):
            return [pltpu.make_async_remote_copy(
                src_ref=red, dst_ref=small_ref.at[rows_of(me if dst_block is None else blk), :],
                send_sem=send2.at[k], recv_sem=recv2.at[k], device_id=pos, device_id_type=MESH)
                for k, (pos, blk) in enumerate(peers)]

        def own_slice():
            return pltpu.make_async_copy(red, small_ref.at[rows_of(me), :], out_sems.at[2])

        @pl.when(u == 0)
        def _():
            for cp in scatter():
                cp.start()

        @pl.when(u == 4)
        def _():
            for cp in scatter():
                cp.wait_recv()
            tot = p_ref[rows_of(me), :]
            for k in range(N_DEV - 1):
                tot = tot + rbuf_s[k]
            red[...] = tot
            own_slice().start()
            for cp in gather():
                cp.start()

        n1 = (jnp.bitwise_xor(x, c), jnp.bitwise_xor(y, 1 - c), c)
        n2 = (jnp.bitwise_xor(x, 1 - c), jnp.bitwise_xor(y, c), c)

        def feed(r, half):
            return pltpu.make_async_remote_copy(
                src_ref=fb.at[half], dst_ref=gbuf.at[r, half], send_sem=send_f.at[r, half],
                recv_sem=recv_g.at[r, half], device_id=sibling, device_id_type=MESH)

        def feed_sibling(half):
            return pltpu.make_async_remote_copy(
                src_ref=fb.at[half], dst_ref=rfin.at[0, half], send_sem=send_f.at[3, half],
                recv_sem=recv_f.at[0, half], device_id=sibling, device_id_type=MESH)

        def chip_sum(r, half):
            dst = [tbuf.at[half], rfin.at[2, half], rfin.at[1, half]][r]
            sem = [recv_t.at[half], recv_f.at[2, half], recv_f.at[1, half]][r]
            return pltpu.make_async_remote_copy(
                src_ref=sbuf.at[r, half], dst_ref=dst, send_sem=send_s.at[r, half], recv_sem=sem,
                device_id=[n2, n2, n1][r], device_id_type=MESH)

        def out_copy(half):
            return pltpu.make_async_copy(acc.at[half], o_ref.at[:, pl.ds(half * dh, dh)], out_sems.at[half])

        def partial_sum():
            return _dot(a_ref[:, 0:n_tok], g_ref[...])

        for half in range(2):
            for r in range(3):
                @pl.when(u == 4 * r + 4 + half)
                def _(r=r, half=half):
                    feed(r, half).wait_send()

                @pl.when(u == 4 * r + 2 + half)
                def _(r=r, half=half):
                    feed(r, half).wait_recv()
                    if r == 2:
                        chip_sum(0, half).wait_recv()

            @pl.when(u == 14 + half)
            def _(half=half):
                feed_sibling(half).wait_recv()
                chip_sum(2, half).wait_recv()
                chip_sum(1, half).wait_recv()

        @pl.when(jnp.logical_and(s % 2 == 0, s < 7))
        def _():
            fb[hf] = partial_sum().astype(BF16)

        @pl.when(jnp.logical_or(s == 1, s == 3))
        def _():
            sbuf[rnd, hf] = (partial_sum() + gbuf[rnd, hf].astype(F32)).astype(BF16)

        @pl.when(s == 5)
        def _():
            sbuf[2, hf] = (partial_sum() + gbuf[2, hf].astype(F32) + tbuf[hf].astype(F32)).astype(BF16)

        @pl.when(s == 7)
        def _():
            acc[hf] = (partial_sum() + rfin[0, hf].astype(F32) + rfin[1, hf].astype(F32)
                       + rfin[2, hf].astype(F32))

        for half in range(2):
            for r in range(3):
                @pl.when(u == 4 * r + half)
                def _(r=r, half=half):
                    feed(r, half).start()

                @pl.when(u == 4 * r + 2 + half)
                def _(r=r, half=half):
                    chip_sum(r, half).start()

            @pl.when(u == 12 + half)
            def _(half=half):
                feed_sibling(half).start()

        @pl.when(u == 14)
        def _():
            out_copy(0).start()

        @pl.when(u == 15)
        def _():
            out_copy(1).start()
            for half in range(2):
                feed_sibling(half).wait_send()
                for r in range(3):
                    chip_sum(r, half).wait_send()
                out_copy(half).wait()
            for cp in gather("theirs"):
                cp.wait_recv()
            for cp in scatter() + gather():
                cp.wait_send()
            own_slice().wait()

    grid_spec = pltpu.PrefetchScalarGridSpec(
        num_scalar_prefetch=1, grid=(n_units,),
        in_specs=[VMEM, pl.BlockSpec((n_tok, dh), lambda u, o: (0, 2 * o[u // 2] + u % 2)), VMEM],
        out_specs=[ANY, ANY],
        scratch_shapes=[pltpu.VMEM((2, d_model, dh), F32), pltpu.VMEM((2, d_model, dh), BF16),
                        pltpu.VMEM((3, 2, d_model, dh), BF16), pltpu.VMEM((3, 2, d_model, dh), BF16),
                        pltpu.VMEM((2, d_model, dh), BF16), pltpu.VMEM((3, 2, d_model, dh), BF16),
                        pltpu.VMEM((N_DEV - 1, rsl, LANES), F32), pltpu.VMEM((rsl, LANES), F32),
                        pltpu.SemaphoreType.DMA((4, 2)), pltpu.SemaphoreType.DMA((3, 2)),
                        pltpu.SemaphoreType.DMA((3, 2)), pltpu.SemaphoreType.DMA((2,)),
                        pltpu.SemaphoreType.DMA((3, 2)),
                        pltpu.SemaphoreType.DMA((3,))] + [pltpu.SemaphoreType.DMA((N_DEV - 1,))] * 4)
    return pl.pallas_call(
        body, name="grad_w_in_reduce_scatter", grid_spec=grid_spec,
        out_shape=[SDS((d_model, d_model), F32), SDS(small_part.shape, F32)], compiler_params=_params(1),
    )(blk_order, xt_bf, dproj, small_part)


def _grad_x_adamw(dproj, w_all, dr, w, g, m, v, groups):
    n_tok, d_model = dr.shape
    tm = _tile(n_tok, 256, 16)
    n_steps = n_tok // tm
    tr = w.shape[0] // n_steps
    n_grp = len(groups)

    def body(dp_ref, w_hbm, dr_ref, ws_ref, g_ref, m_ref, v_ref, *rest):
        grp_in, rest = rest[:4 * n_grp], rest[4 * n_grp:]
        o_ref, d_ref, mo_ref, vo_ref = rest[:4]
        grp_out, (w_ref, w_sems) = rest[4:4 + 3 * n_grp], rest[4 + 3 * n_grp:]
        i = pl.program_id(0)

        def fetch(j):
            return pltpu.make_async_copy(w_hbm.at[j], w_ref.at[j], w_sems.at[j])

        def grad_x_tile(first):
            acc = DEEPNORM_ALPHA * dr_ref[...]
            for j in range(N_DEV):
                if first:
                    fetch(j).wait()
                acc = acc + _dot_tb(dp_ref[:, j * d_model:(j + 1) * d_model], w_ref[j])
            o_ref[...] = acc

        @pl.when(i == 0)
        def _():
            for j in range(N_DEV):
                fetch(j).start()
            grad_x_tile(True)

        @pl.when(i != 0)
        def _():
            grad_x_tile(False)

        d_ref[...], mo_ref[...], vo_ref[...] = _adamw_math(ws_ref[...], g_ref[...], m_ref[...], v_ref[...])

        @pl.when(i == n_steps - 1)
        def _():
            for p in range(n_grp):
                pw, pg, pm, pv = grp_in[4 * p:4 * p + 4]
                grp_out[3 * p][...], grp_out[3 * p + 1][...], grp_out[3 * p + 2][...] = _adamw_math(
                    pw[...], pg[...], pm[...], pv[...])

    tile = pl.BlockSpec((tm, d_model), lambda i: (i, 0))
    slab = pl.BlockSpec((tr, w.shape[1]), lambda i: (i, 0))
    flat = [a for grp in groups for a in grp]
    res = pl.pallas_call(
        body, name="grad_x_adamw", grid=(n_steps,),
        in_specs=[pl.BlockSpec((tm, N_DEV * d_model), lambda i: (i, 0)), ANY, tile, slab, slab, slab, slab]
        + [VMEM] * (4 * n_grp),
        out_specs=[tile, slab, slab, slab] + [VMEM] * (3 * n_grp),
        out_shape=[SDS((n_tok, d_model), F32)] + [SDS(w.shape, F32)] * 3
        + [SDS(grp[0].shape, F32) for grp in groups for _ in range(3)],
        scratch_shapes=[pltpu.VMEM(w_all.shape, BF16), pltpu.SemaphoreType.DMA((N_DEV,))],
        compiler_params=_params(1),
    )(dproj, w_all, dr, w, g, m, v, *flat)
    return res[0], tuple(res[1:4]), [tuple(res[4 + 3 * p:7 + 3 * p]) for p in range(n_grp)]


def _adamw_math(w, g, m, v):
    m = ADAM_B1 * m + (1.0 - ADAM_B1) * g
    v = ADAM_B2 * v + (1.0 - ADAM_B2) * (g * g)
    m_hat = m / (1.0 - ADAM_B1 ** ADAM_STEP)
    v_hat = v / (1.0 - ADAM_B2 ** ADAM_STEP)
    delta = -ADAM_LR * (m_hat / (jnp.sqrt(v_hat) + ADAM_EPS) + ADAM_WD * w)
    return delta, m, v


def _as_rows(a):
    return a.reshape(-1, LANES)


def kernel(x, w_in, b_in, conv_w, conv_b, gn_g, gn_b, ln_v_g, ln_v_b, w_spatial, b_spatial, w_pa, w_pb, w_o, b_o, ln_out_g, ln_out_b, loss_target, m_w_in, m_b_in, m_conv_w, m_conv_b, m_gn_g, m_gn_b, m_ln_v_g, m_ln_v_b, m_w_spatial, m_b_spatial, m_w_pa, m_w_pb, m_w_o, m_b_o, m_ln_out_g, m_ln_out_b, v_w_in, v_b_in, v_conv_w, v_conv_b, v_gn_g, v_gn_b, v_ln_v_g, v_ln_v_b, v_w_spatial, v_b_spatial, v_w_pa, v_w_pb, v_w_o, v_b_o, v_ln_out_g, v_ln_out_b):
    n_batch, seq, d_model = x.shape
    n_tok = n_batch * seq
    n_heads = d_model // LANES
    dc = conv_w.shape[1]
    me = 4 * lax.axis_index("x") + 2 * lax.axis_index("y") + lax.axis_index("c")
    row = lambda a: a.reshape(1, d_model)

    x2 = x.reshape(n_tok, d_model)
    target2 = loss_target.reshape(n_tok, d_model)
    b_spatial_t = b_spatial.T

    first = jnp.where(lax.axis_index("c") == 1, 4, 2)
    second = 6 - first
    ag_rel = jnp.stack([0 * first, 0 * first + 1, first, second + 1, second, first + 1, 0 * first + 6, 0 * first + 7])
    ag_blocks = jnp.bitwise_xor(me, ag_rel).astype(jnp.int32)
    proj, xt_bf, w_all, wp_all, cw_all = _proj_all_gather(
        x2, w_in, w_pa, w_pb, w_o, conv_w, b_in.reshape(N_DEV, 1, d_model), ag_blocks)
    wp_full = wp_all.reshape(3, d_model, d_model)
    conv_w_full = jnp.pad(cw_all.transpose(1, 0, 2).reshape(CONV_K, d_model), ((0, HALO - CONV_K), (0, 0)))

    h3, h1 = _branch_a_fwd(proj, conv_w_full, row(conv_b), row(gn_g), row(gn_b), seq)
    s = _branch_b_fwd(proj, row(ln_v_g), row(ln_v_b), w_spatial, b_spatial_t, seq)

    dproj, d_h3, d_s, dr, lhs3, rhs3, vec_mid = _mid(
        h3, s, proj, x2, target2, wp_full, row(b_o), row(ln_out_g), row(ln_out_b))

    dproj, vec_b, d_ws, d_bs_t = _branch_b_bwd(dproj, proj, d_s, row(ln_v_g), row(ln_v_b), w_spatial, b_spatial_t, seq)
    dproj, d_h1, vec_a1 = _branch_a_bwd_norm(dproj, proj, h1, d_h3, row(gn_g), row(gn_b), seq)
    gp_f32, gp_bf = _weight_grads(lhs3, rhs3)
    dproj, vec_a2, d_cw8, g_w_pa, g_w_pb, g_w_o = _branch_a_bwd_conv(
        dproj, proj, d_h1, conv_w_full, gp_bf, gp_f32, seq)

    d_cw = jnp.sum(d_cw8, axis=2)
    pieces = [
        _as_rows(jnp.concatenate([vec_a2[0:2], vec_a1[0:1], vec_b[0:3], vec_mid[3:5]], axis=0)),
        _as_rows(jnp.concatenate([vec_a1[3:4], vec_a1[1:3], vec_b[3:5], vec_mid[2:3], vec_mid[0:2]], axis=0)),
        _as_rows(d_bs_t.T), _as_rows(d_ws), _as_rows(d_cw), _as_rows(vec_mid[5:6]),
    ]
    n_rows = sum(p.shape[0] for p in pieces)
    pad_rows = -n_rows % (N_DEV * SUBLANES)
    small_part = jnp.concatenate(pieces + [jnp.zeros((pad_rows, LANES), F32)], axis=0)

    rs_rel = jnp.stack([0 * first + 7, 0 * first + 6, first + 1, second, second + 1, first, 0 * first + 1, 0 * first])
    rs_blocks = jnp.bitwise_xor(me, rs_rel).astype(jnp.int32)
    g_w_in, small = _grad_w_in_reduce_scatter(xt_bf, dproj, rs_blocks, small_part)

    g_rows = d_model // LANES
    o0 = N_DEV * g_rows
    g_b_in = small[0:o0].reshape(N_DEV * d_model)
    vecs = [small[o0 + a * g_rows:o0 + (a + 1) * g_rows].reshape(d_model) for a in range(8)]
    g_conv_b, g_gn_g, g_gn_b, g_ln_v_g, g_ln_v_b, g_b_o, g_ln_out_g, g_ln_out_b = vecs
    o1 = o0 + 8 * g_rows
    g_b_spatial = small[o1:o1 + n_heads].reshape(n_heads, LANES)
    o2 = o1 + n_heads
    g_w_spatial = small[o2:o2 + n_heads * LANES].reshape(n_heads, LANES, LANES)
    o3 = o2 + n_heads * LANES
    g_cw_full = small[o3:o3 + n_heads * HALO].reshape(n_heads, HALO, LANES).transpose(1, 0, 2).reshape(HALO, d_model)
    g_conv_w = lax.dynamic_slice(g_cw_full, (0, me * dc), (CONV_K, dc))
    o4 = o3 + n_heads * HALO
    loss = jnp.sum(small[o4:o4 + g_rows]) * (0.5 / d_model)

    two_d = lambda a: a.reshape(-1, a.shape[-1]) if a.ndim != 1 else (
        a.reshape(-1, LANES) if a.shape[0] % LANES == 0 else a.reshape(1, -1))
    names = ["b_in", "conv_w", "conv_b", "gn_g", "gn_b", "ln_v_g", "ln_v_b", "w_spatial", "b_spatial",
             "w_pa", "w_pb", "w_o", "b_o", "ln_out_g", "ln_out_b"]
    ws = dict(b_in=b_in, conv_w=conv_w, conv_b=conv_b, gn_g=gn_g, gn_b=gn_b, ln_v_g=ln_v_g, ln_v_b=ln_v_b,
              w_spatial=w_spatial, b_spatial=b_spatial, w_pa=w_pa, w_pb=w_pb, w_o=w_o, b_o=b_o,
              ln_out_g=ln_out_g, ln_out_b=ln_out_b)
    gs = dict(b_in=g_b_in, conv_w=g_conv_w, conv_b=g_conv_b, gn_g=g_gn_g, gn_b=g_gn_b, ln_v_g=g_ln_v_g,
              ln_v_b=g_ln_v_b, w_spatial=g_w_spatial, b_spatial=g_b_spatial, w_pa=g_w_pa, w_pb=g_w_pb,
              w_o=g_w_o, b_o=g_b_o, ln_out_g=g_ln_out_g, ln_out_b=g_ln_out_b)
    ms = dict(b_in=m_b_in, conv_w=m_conv_w, conv_b=m_conv_b, gn_g=m_gn_g, gn_b=m_gn_b, ln_v_g=m_ln_v_g,
              ln_v_b=m_ln_v_b, w_spatial=m_w_spatial, b_spatial=m_b_spatial, w_pa=m_w_pa, w_pb=m_w_pb,
              w_o=m_w_o, b_o=m_b_o, ln_out_g=m_ln_out_g, ln_out_b=m_ln_out_b)
    vs = dict(b_in=v_b_in, conv_w=v_conv_w, conv_b=v_conv_b, gn_g=v_gn_g, gn_b=v_gn_b, ln_v_g=v_ln_v_g,
              ln_v_b=v_ln_v_b, w_spatial=v_w_spatial, b_spatial=v_b_spatial, w_pa=v_w_pa, w_pb=v_w_pb,
              w_o=v_w_o, b_o=v_b_o, ln_out_g=v_ln_out_g, ln_out_b=v_ln_out_b)
    grad_x, (d_w_in, nm_w_in, nv_w_in), upd = _grad_x_adamw(
        dproj, w_all, dr, w_in, g_w_in, m_w_in, v_w_in,
        [tuple(two_d(d[n]) for d in (ws, gs, ms, vs)) for n in names])
    grad_x = grad_x.reshape(x.shape)
    delta = {n: u[0].reshape(ws[n].shape) for n, u in zip(names, upd)}
    new_m = {n: u[1].reshape(ws[n].shape) for n, u in zip(names, upd)}
    new_v = {n: u[2].reshape(ws[n].shape) for n, u in zip(names, upd)}
    gs["w_in"], delta["w_in"], new_m["w_in"], new_v["w_in"] = g_w_in, d_w_in, nm_w_in, nv_w_in

    order = ["w_in"] + names
    return (loss, grad_x, *[gs[n] for n in order], *[delta[n] for n in order],
            *[new_m[n] for n in order], *[new_v[n] for n in order])
```

```python
import jax
import jax.numpy as jnp
from jax import lax
from jax.experimental import pallas as pl
from jax.experimental.pallas import tpu as pltpu

F32 = jnp.float32
BF16 = jnp.bfloat16
SDS = jax.ShapeDtypeStruct

N_DEV = 8
LANES = 128
SUBLANES = 8
CONV_K = 31
HALO = 32
ELEMENTWISE_ROWS = 512
LN_EPS = 1e-5
DEEPNORM_ALPHA = 2.0 ** 0.25
ADAM_LR, ADAM_B1, ADAM_B2, ADAM_EPS, ADAM_WD, ADAM_STEP = 0.001, 0.9, 0.999, 1e-08, 0.01, 10
GELU_C = 0.7978845608028654
GELU_A = 0.044715
VMEM_LIMIT = 56 * 1024 * 1024
MESH = pl.DeviceIdType.MESH
ANY = pl.BlockSpec(memory_space=pl.ANY)
VMEM = pl.BlockSpec(memory_space=pltpu.VMEM)


def _params(n_grid=0):
    sem = ("arbitrary",) * n_grid if n_grid else None
    return pltpu.CompilerParams(dimension_semantics=sem, vmem_limit_bytes=VMEM_LIMIT)


def _tile(n, pref, mult):
    t = min(n, pref)
    while n % t or t % mult:
        t -= 1
    return t


def _colsum(v):
    return jnp.sum(v, axis=0, keepdims=True)


def _sigmoid(v):
    return jax.nn.sigmoid(v)


def _silu_and_grad(v):
    s = _sigmoid(v)
    val = v * s
    return val, s + val * (1.0 - s)


def _gelu_and_grad(v):
    v2 = v * v
    sg = _sigmoid(v * (2.0 * GELU_C + (2.0 * GELU_C * GELU_A) * v2))
    grad = sg + v * sg * (1.0 - sg) * (2.0 * GELU_C + (6.0 * GELU_C * GELU_A) * v2)
    return v * sg, grad


def _tril_mask():
    r = lax.broadcasted_iota(jnp.int32, (LANES, LANES), 0)
    c = lax.broadcasted_iota(jnp.int32, (LANES, LANES), 1)
    return c <= r


def _dot(a, b):
    return jnp.dot(a, b, preferred_element_type=F32)


def _dot_tb(a, b):
    return lax.dot_general(a, b, (((1,), (1,)), ((), ())), preferred_element_type=F32)


def _dot_ta(a, b):
    return lax.dot_general(a, b, (((0,), (0,)), ((), ())), preferred_element_type=F32)


def _mesh_pos():
    return lax.axis_index("x"), lax.axis_index("y"), lax.axis_index("c")


def _block_of(pos):
    return 4 * pos[0] + 2 * pos[1] + pos[2]


def _peers():
    x, y, c = _mesh_pos()
    out = []
    for k in range(1, N_DEV):
        pos = (1 - x if k & 4 else x, 1 - y if k & 2 else y, 1 - c if k & 1 else c)
        out.append((pos, _block_of(pos)))
    return out


def _proj_all_gather(x2, w_in, w_pa, w_pb, w_o, conv_w, b_in3, blk_order):
    n_tok, d_model = x2.shape
    r8 = w_pa.shape[0]
    kc, dc = conv_w.shape
    tm = _tile(n_tok, 1024, LANES)
    n_t = n_tok // tm
    n_arr = 3

    def body(ord_ref, x_ref, b_ref, win_ref, wpa_ref, wpb_ref, wo_ref, cw_ref,
             proj_ref, xt_ref, wall_ref, wp_ref, cwall_ref,
             wbuf, xbuf, st_p, send_sems, recv_sems, local_sems, wall_sems):
        s = pl.program_id(0)
        t = pl.program_id(1)
        x, y, c = _mesh_pos()
        me = (x, y, c)
        sibling = (x, y, 1 - c)
        n1 = (jnp.bitwise_xor(x, c), jnp.bitwise_xor(y, 1 - c))
        n2 = (jnp.bitwise_xor(x, 1 - c), jnp.bitwise_xor(y, c))
        dg = (1 - x, 1 - y)
        outs = [wbuf, wp_ref, cwall_ref]
        srcs = [None, st_p, cw_ref]
        consumed = [me, sibling, (*n1, c), (*n2, 1 - c), (*n2, c), (*n1, 1 - c), (*dg, c), (*dg, 1 - c)]
        leaves = [(me, sibling), (me, (*n1, c)), (me, (*n2, c)), ((*n1, c), (*n2, c)),
                  ((*n1, c), sibling), ((*n2, c), sibling), ((*dg, c), sibling)]
        lands = [sibling, (*n1, c), (*n2, c), (*dg, c), (*n2, 1 - c), (*n1, 1 - c), (*dg, 1 - c)]

        def slot(o, pos):
            return o.at[:, _block_of(pos)] if o is wp_ref else o.at[_block_of(pos)]

        def copy(a, k, block, to, src=None):
            o = outs[a]
            return pltpu.make_async_remote_copy(
                src_ref=slot(o, block) if src is None else src, dst_ref=slot(o, block),
                send_sem=send_sems.at[a, k], recv_sem=recv_sems.at[a, k],
                device_id=to, device_id_type=MESH)

        def send(a, k):
            block, to = leaves[k]
            return copy(a, k, block, to, src=srcs[a] if k < 3 else None)

        def recv(a, k):
            return copy(a, k, lands[k], me)

        def local_copies():
            return [pltpu.make_async_copy(srcs[a], slot(outs[a], me), local_sems.at[a]) for a in (1, 2)]

        def to_hbm(step):
            return pltpu.make_async_copy(slot(wbuf, consumed[step]), slot(wall_ref, consumed[step]),
                                         wall_sems.at[step])

        def at_step(step):
            return pl.when(jnp.logical_and(s == step, t == 0))

        @at_step(0)
        def _():
            slot(wbuf, me)[...] = win_ref[...].astype(BF16)
            st_p[0] = wpa_ref[...].astype(BF16)
            st_p[1] = wpb_ref[...].astype(BF16)
            st_p[2] = wo_ref[...].astype(BF16)
            for a in range(n_arr):
                send(a, 0).start()
                send(a, 1).start()
            for cp in local_copies():
                cp.start()
            to_hbm(0).start()

        @at_step(1)
        def _():
            recv(0, 0).wait_recv()
            to_hbm(1).start()

        for rnd in range(3):
            @at_step(2 + 2 * rnd)
            def _(rnd=rnd):
                if rnd == 0:
                    for a in range(n_arr):
                        send(a, 2).start()
                recv(0, 1 + rnd).wait_recv()
                if rnd == 0:
                    send(0, 3).start()
                send(0, 4 + rnd).start()
                to_hbm(2 + 2 * rnd).start()

            @at_step(3 + 2 * rnd)
            def _(rnd=rnd):
                for a in (1, 2):
                    recv(a, 1 + rnd).wait_recv()
                    if rnd == 0:
                        send(a, 3).start()
                    send(a, 4 + rnd).start()
                recv(0, 4 + rnd).wait_recv()
                to_hbm(3 + 2 * rnd).start()

        rows = pl.ds(pl.multiple_of(t * tm, tm), tm)

        @pl.when(s == 0)
        def _():
            xb = x_ref[...].astype(BF16)
            xbuf[rows, :] = xb
            xt_ref[...] = xb.T

        proj_ref[...] = _dot(xbuf[rows, :], wbuf[ord_ref[s]]) + b_ref[...]

        @pl.when(jnp.logical_and(s == N_DEV - 1, t == n_t - 1))
        def _():
            for a in (1, 2):
                for k in (0, 4, 5, 6):
                    recv(a, k).wait_recv()
            for a in range(n_arr):
                for k in range(7):
                    send(a, k).wait_send()
            for cp in local_copies() + [to_hbm(step) for step in range(N_DEV)]:
                cp.wait()

    grid_spec = pltpu.PrefetchScalarGridSpec(
        num_scalar_prefetch=1, grid=(N_DEV, n_t),
        in_specs=[pl.BlockSpec((tm, d_model), lambda s, t, o: (jnp.where(s == 0, t, n_t - 1), 0)),
                  pl.BlockSpec((None, 1, d_model), lambda s, t, o: (o[s], 0, 0)),
                  VMEM, VMEM, VMEM, VMEM, VMEM],
        out_specs=[pl.BlockSpec((tm, d_model), lambda s, t, o: (t, o[s])),
                   pl.BlockSpec((d_model, tm), lambda s, t, o: (0, jnp.where(s == 0, t, n_t))),
                   ANY, ANY, ANY],
        scratch_shapes=[pltpu.VMEM((N_DEV, d_model, d_model), BF16), pltpu.VMEM((n_tok, d_model), BF16),
                        pltpu.VMEM((3, r8, d_model), BF16),
                        pltpu.SemaphoreType.DMA((n_arr, 7)), pltpu.SemaphoreType.DMA((n_arr, 7)),
                        pltpu.SemaphoreType.DMA((3,)), pltpu.SemaphoreType.DMA((N_DEV,))])
    return pl.pallas_call(
        body, name="proj_all_gather", grid_spec=grid_spec,
        out_shape=[SDS((n_tok, N_DEV * d_model), F32), SDS((d_model, (n_t + 1) * tm), BF16),
                   SDS((N_DEV, d_model, d_model), BF16), SDS((3, N_DEV, r8, d_model), BF16),
                   SDS((N_DEV, kc, dc), F32)],
        compiler_params=_params(2),
    )(blk_order, x2, b_in3, w_in, w_pa, w_pb, w_o, conv_w)


def _conv_rows(ta):
    return _tile(ta, 64, SUBLANES)


def _branch_a_fwd(proj, conv_w_full, conv_b, gn_g, gn_b, seq):
    n_tok = proj.shape[0]
    d_model = conv_b.shape[1]
    ta = _tile(seq, ELEMENTWISE_ROWS, HALO)
    per_seq = seq // ta
    rc = _conv_rows(ta)

    def body(av_ref, ag_ref, gt_ref, avh_ref, agh_ref, cw_ref, cb_ref, gg_ref, gb_ref,
             h3_ref, h1_ref, ext):
        keep = jnp.where(pl.program_id(0) % per_seq == 0, 0.0, 1.0)

        def group(g, carry):
            sl = pl.ds(pl.multiple_of(g * LANES, LANES), LANES)
            ext[0:HALO, :] = avh_ref[:, sl] * _sigmoid(agh_ref[:, sl]) * keep
            ext[HALO:HALO + ta, :] = av_ref[:, sl] * _sigmoid(ag_ref[:, sl])
            for r0 in range(0, ta, rc):
                acc = jnp.broadcast_to(cb_ref[:, sl], (rc, LANES))
                for k in range(CONV_K):
                    acc = acc + ext[pl.ds(r0 + HALO - (CONV_K - 1) + k, rc), :] * cw_ref[k:k + 1, sl]
                h1_ref[pl.ds(r0, rc), sl] = acc
            h1 = h1_ref[:, sl]
            mu = jnp.mean(h1, axis=-1, keepdims=True)
            dlt = h1 - mu
            var = jnp.mean(dlt * dlt, axis=-1, keepdims=True)
            h2 = dlt * lax.rsqrt(var + LN_EPS) * gg_ref[:, sl] + gb_ref[:, sl]
            gate = gt_ref[:, sl]
            h3_ref[:, sl] = (h2 * _sigmoid(h2) * gate * _sigmoid(gate)).astype(BF16)
            return carry

        lax.fori_loop(0, d_model // LANES, group, 0, unroll=True)

    blk = lambda j: pl.BlockSpec((ta, d_model), lambda i: (i, j))
    halo = lambda j: pl.BlockSpec((HALO, d_model), lambda i: (jnp.maximum(i * (ta // HALO) - 1, 0), j))
    row = pl.BlockSpec((1, d_model), lambda i: (0, 0))
    return pl.pallas_call(
        body, name="branch_a_fwd", grid=(n_tok // ta,),
        in_specs=[blk(0), blk(1), blk(2), halo(0), halo(1),
                  pl.BlockSpec((HALO, d_model), lambda i: (0, 0)), row, row, row],
        out_specs=[pl.BlockSpec((ta, d_model), lambda i: (i, 0))] * 2,
        out_shape=[SDS((n_tok, d_model), BF16), SDS((n_tok, d_model), F32)],
        scratch_shapes=[pltpu.VMEM((HALO + ta, LANES), F32)],
        compiler_params=_params(1),
    )(proj, proj, proj, proj, proj, conv_w_full, conv_b, gn_g, gn_b)


def _branch_b_fwd(proj, ln_g, ln_b, w_spatial, b_spatial_t, seq):
    n_tok = proj.shape[0]
    d_model = ln_g.shape[1]
    n_heads = d_model // LANES
    tb = _tile(seq, ELEMENTWISE_ROWS, LANES)

    def body(u_ref, v_ref, bg_ref, lg_ref, lb_ref, ws_ref, bs_ref, s_ref, vn_buf):
        v, _ = _gelu_and_grad(v_ref[...])
        mu = jnp.mean(v, axis=-1, keepdims=True)
        dlt = v - mu
        var = jnp.mean(dlt * dlt, axis=-1, keepdims=True)
        vn_buf[...] = (dlt * lax.rsqrt(var + LN_EPS) * lg_ref[...] + lb_ref[...]).astype(BF16)
        tril = _tril_mask()
        for h in range(n_heads):
            cols = slice(h * LANES, (h + 1) * LANES)
            w_h = jnp.where(tril, ws_ref[h], 0.0).astype(BF16)
            bias = bs_ref[:, h:h + 1]
            for ch in range(tb // LANES):
                rows = slice(ch * LANES, (ch + 1) * LANES)
                mix = _dot(w_h, vn_buf[rows, cols]) + bias
                u, _ = _gelu_and_grad(u_ref[rows, cols])
                gate = bg_ref[rows, cols]
                s_ref[rows, cols] = (u * mix * gate * _sigmoid(gate)).astype(BF16)

    blk = lambda j: pl.BlockSpec((tb, d_model), lambda i: (i, j))
    row = pl.BlockSpec((1, d_model), lambda i: (0, 0))
    return pl.pallas_call(
        body, name="branch_b_fwd", grid=(n_tok // tb,),
        in_specs=[blk(3), blk(4), blk(5), row, row,
                  pl.BlockSpec((n_heads, LANES, LANES), lambda i: (0, 0, 0)),
                  pl.BlockSpec((LANES, n_heads), lambda i: (0, 0))],
        out_specs=pl.BlockSpec((tb, d_model), lambda i: (i, 0)),
        out_shape=SDS((n_tok, d_model), BF16),
        scratch_shapes=[pltpu.VMEM((tb, d_model), BF16)],
        compiler_params=_params(1),
    )(proj, proj, proj, ln_g, ln_b, w_spatial, b_spatial_t)


MID_ROWS = 8


def _mid(h3, s, proj, x2, target, wp_full, b_o, lo_g, lo_b):
    n_tok, d_model = x2.shape
    tm = _tile(n_tok, 256, 16)

    def body(h3_ref, s_ref, ma_ref, mb_ref, x_ref, t_ref, wpa_ref, wpb_ref, wo_ref, bo_ref,
             lg_ref, lb_ref, dproj_ref, dh3_ref, ds_ref, dr_ref, lhs3_ref, rhs3_ref, vec_ref):
        @pl.when(pl.program_id(0) == 0)
        def _():
            vec_ref[...] = jnp.zeros_like(vec_ref)

        h3 = h3_ref[...]
        s = s_ref[...]
        ya = _dot(h3, wpa_ref[...])
        yb = _dot(s, wpb_ref[...])
        ga = _sigmoid(ma_ref[...])
        gb = _sigmoid(mb_ref[...])
        mixed = (ga * ya + gb * yb).astype(BF16)
        lhs3_ref[0] = h3
        lhs3_ref[1] = s
        lhs3_ref[2] = mixed
        r = DEEPNORM_ALPHA * x_ref[...] + _dot(mixed, wo_ref[...]) + bo_ref[...]
        mu = jnp.mean(r, axis=-1, keepdims=True)
        dlt = r - mu
        rstd = lax.rsqrt(jnp.mean(dlt * dlt, axis=-1, keepdims=True) + LN_EPS)
        rhat = dlt * rstd
        diff = rhat * lg_ref[...] + lb_ref[...] - t_ref[...]
        dy = diff * (1.0 / d_model)
        vec_ref[0:1, :] += _colsum(dy * rhat)
        vec_ref[1:2, :] += _colsum(dy)
        vec_ref[5:6, :] += _colsum(diff * diff)
        drh = dy * lg_ref[...]
        dr = rstd * (drh - jnp.mean(drh, axis=-1, keepdims=True)
                     - rhat * jnp.mean(drh * rhat, axis=-1, keepdims=True))
        vec_ref[2:3, :] += _colsum(dr)
        dr_ref[...] = dr
        drb = dr.astype(BF16)
        rhs3_ref[2] = drb
        dmixed = _dot_tb(drb, wo_ref[...])
        dya_f = dmixed * ga
        dyb_f = dmixed * gb
        dma = dya_f * ya * (1.0 - ga)
        dmb = dyb_f * yb * (1.0 - gb)
        vec_ref[3:4, :] += _colsum(dma)
        vec_ref[4:5, :] += _colsum(dmb)
        dproj_ref[:, 0:d_model] = dma.astype(BF16)
        dproj_ref[:, d_model:2 * d_model] = dmb.astype(BF16)
        dya = dya_f.astype(BF16)
        dyb = dyb_f.astype(BF16)
        rhs3_ref[0] = dya
        rhs3_ref[1] = dyb
        dh3_ref[...] = _dot_tb(dya, wpa_ref[...])
        ds_ref[...] = _dot_tb(dyb, wpb_ref[...])

    tile = pl.BlockSpec((tm, d_model), lambda i: (i, 0))
    full = lambda a: pl.BlockSpec((None, d_model, d_model), lambda i: (a, 0, 0))
    row = pl.BlockSpec((1, d_model), lambda i: (0, 0))
    stack = pl.BlockSpec((3, tm, d_model), lambda i: (0, i, 0))
    bf3 = SDS((3, n_tok, d_model), BF16)
    f32 = SDS((n_tok, d_model), F32)
    return pl.pallas_call(
        body, name="mid", grid=(n_tok // tm,),
        in_specs=[tile, tile, pl.BlockSpec((tm, d_model), lambda i: (i, 6)),
                  pl.BlockSpec((tm, d_model), lambda i: (i, 7)), tile, tile, full(0), full(1), full(2),
                  row, row, row],
        out_specs=[pl.BlockSpec((tm, 2 * d_model), lambda i: (i, 3)), tile, tile, tile, stack, stack,
                   pl.BlockSpec((MID_ROWS, d_model), lambda i: (0, 0))],
        out_shape=[SDS((n_tok, N_DEV * d_model), BF16), f32, f32, f32, bf3, bf3,
                   SDS((MID_ROWS, d_model), F32)],
        compiler_params=_params(1),
    )(h3, s, proj, proj, x2, target, wp_full, wp_full, wp_full, b_o, lo_g, lo_b)


B_ROWS = 8


def _branch_b_bwd(dproj, proj, d_s, ln_g, ln_b, w_spatial, b_spatial_t, seq):
    n_tok = proj.shape[0]
    d_model = ln_g.shape[1]
    n_heads = d_model // LANES
    tb = _tile(seq, ELEMENTWISE_ROWS, LANES)

    def body(dproj_in, u_ref, v_ref, bg_ref, ds_ref, lg_ref, lb_ref, ws_ref, bs_ref,
             dproj_ref, vec_ref, dws_ref, dbs_ref, vn_buf, dv_buf):
        del dproj_in

        @pl.when(pl.program_id(0) == 0)
        def _():
            vec_ref[...] = jnp.zeros_like(vec_ref)
            dws_ref[...] = jnp.zeros_like(dws_ref)
            dbs_ref[...] = jnp.zeros_like(dbs_ref)

        v, dgelu_v = _gelu_and_grad(v_ref[...])
        mu = jnp.mean(v, axis=-1, keepdims=True)
        dlt = v - mu
        rstd = lax.rsqrt(jnp.mean(dlt * dlt, axis=-1, keepdims=True) + LN_EPS)
        vhat = dlt * rstd
        vn_buf[...] = (vhat * lg_ref[...] + lb_ref[...]).astype(BF16)
        tril = _tril_mask()
        for h in range(n_heads):
            cols = slice(h * LANES, (h + 1) * LANES)
            w_h = jnp.where(tril, ws_ref[h], 0.0).astype(BF16)
            bias = bs_ref[:, h:h + 1]
            for ch in range(tb // LANES):
                rows = slice(ch * LANES, (ch + 1) * LANES)
                vn = vn_buf[rows, cols]
                mix = _dot(w_h, vn) + bias
                u, dgelu_u = _gelu_and_grad(u_ref[rows, cols])
                sg, dsilu = _silu_and_grad(bg_ref[rows, cols])
                dsv = ds_ref[rows, cols]
                du = dsv * mix * sg * dgelu_u
                dbg = dsv * u * mix * dsilu
                dmix = dsv * u * sg
                dmix_bf = dmix.astype(BF16)
                dproj_ref[rows, cols] = du.astype(BF16)
                dproj_ref[rows, 2 * d_model + h * LANES:2 * d_model + (h + 1) * LANES] = dbg.astype(BF16)
                vec_ref[0:1, cols] += _colsum(du)
                vec_ref[2:3, cols] += _colsum(dbg)
                dbs_ref[:, h:h + 1] += jnp.sum(dmix, axis=1, keepdims=True)
                dws_ref[h] += jnp.where(tril, _dot_tb(dmix_bf, vn), 0.0)
                dv_buf[rows, cols] = _dot_ta(w_h, dmix_bf)
        dvn = dv_buf[...]
        vec_ref[3:4, :] += _colsum(dvn * vhat)
        vec_ref[4:5, :] += _colsum(dvn)
        dvh = dvn * lg_ref[...]
        dv = rstd * (dvh - jnp.mean(dvh, axis=-1, keepdims=True)
                     - vhat * jnp.mean(dvh * vhat, axis=-1, keepdims=True)) * dgelu_v
        vec_ref[1:2, :] += _colsum(dv)
        dproj_ref[:, d_model:2 * d_model] = dv.astype(BF16)

    blk = lambda j: pl.BlockSpec((tb, d_model), lambda i: (i, j))
    row = pl.BlockSpec((1, d_model), lambda i: (0, 0))
    return pl.pallas_call(
        body, name="branch_b_bwd", grid=(n_tok // tb,),
        in_specs=[ANY, blk(3), blk(4), blk(5), pl.BlockSpec((tb, d_model), lambda i: (i, 0)), row, row,
                  pl.BlockSpec((n_heads, LANES, LANES), lambda i: (0, 0, 0)),
                  pl.BlockSpec((LANES, n_heads), lambda i: (0, 0))],
        out_specs=[pl.BlockSpec((tb, 3 * d_model), lambda i: (i, 1)),
                   pl.BlockSpec((B_ROWS, d_model), lambda i: (0, 0)),
                   pl.BlockSpec((n_heads, LANES, LANES), lambda i: (0, 0, 0)),
                   pl.BlockSpec((LANES, n_heads), lambda i: (0, 0))],
        out_shape=[SDS(dproj.shape, BF16), SDS((B_ROWS, d_model), F32),
                   SDS((n_heads, LANES, LANES), F32), SDS((LANES, n_heads), F32)],
        scratch_shapes=[pltpu.VMEM((tb, d_model), BF16), pltpu.VMEM((tb, d_model), F32)],
        input_output_aliases={0: 0},
        compiler_params=_params(1),
    )(dproj, proj, proj, proj, d_s, ln_g, ln_b, w_spatial, b_spatial_t)


A1_ROWS = 8


def _branch_a_bwd_norm(dproj, proj, h1, d_h3, gn_g, gn_b, seq):
    n_tok = proj.shape[0]
    d_model = gn_g.shape[1]
    ta = _tile(seq, ELEMENTWISE_ROWS, 16)

    def body(dproj_in, gt_ref, h1_ref, dh3_ref, gg_ref, gb_ref, dproj_ref, dh1_ref, vec_ref):
        del dproj_in

        @pl.when(pl.program_id(0) == 0)
        def _():
            vec_ref[...] = jnp.zeros_like(vec_ref)

        def group(g, carry):
            sl = pl.ds(pl.multiple_of(g * LANES, LANES), LANES)
            h1 = h1_ref[:, sl]
            mu = jnp.mean(h1, axis=-1, keepdims=True)
            dlt = h1 - mu
            rstd = lax.rsqrt(jnp.mean(dlt * dlt, axis=-1, keepdims=True) + LN_EPS)
            nrm = dlt * rstd
            sw, dsw = _silu_and_grad(nrm * gg_ref[:, sl] + gb_ref[:, sl])
            sg, dsg = _silu_and_grad(gt_ref[:, sl])
            dh3 = dh3_ref[:, sl]
            dgate = dh3 * sw * dsg
            dproj_ref[:, sl] = dgate.astype(BF16)
            vec_ref[0:1, sl] += _colsum(dgate)
            dh2 = dh3 * sg * dsw
            vec_ref[1:2, sl] += _colsum(dh2 * nrm)
            vec_ref[2:3, sl] += _colsum(dh2)
            dn = dh2 * gg_ref[:, sl]
            dh1 = rstd * (dn - jnp.mean(dn, axis=-1, keepdims=True)
                          - nrm * jnp.mean(dn * nrm, axis=-1, keepdims=True))
            vec_ref[3:4, sl] += _colsum(dh1)
            dh1_ref[:, sl] = dh1
            return carry

        lax.fori_loop(0, d_model // LANES, group, 0, unroll=True)

    tile = pl.BlockSpec((ta, d_model), lambda i: (i, 0))
    row = pl.BlockSpec((1, d_model), lambda i: (0, 0))
    return pl.pallas_call(
        body, name="branch_a_bwd_norm", grid=(n_tok // ta,),
        in_specs=[ANY, pl.BlockSpec((ta, d_model), lambda i: (i, 2)), tile, tile, row, row],
        out_specs=[pl.BlockSpec((ta, d_model), lambda i: (i, 2)), tile,
                   pl.BlockSpec((A1_ROWS, d_model), lambda i: (0, 0))],
        out_shape=[SDS(dproj.shape, BF16), SDS((n_tok, d_model), F32), SDS((A1_ROWS, d_model), F32)],
        input_output_aliases={0: 0},
        compiler_params=_params(1),
    )(dproj, proj, h1, d_h3, gn_g, gn_b)


A2_ROWS = 8


def _branch_a_bwd_conv(dproj, proj, d_h1, conv_w_full, gp_bf, gp_f32, seq):
    n_tok = proj.shape[0]
    d_model = conv_w_full.shape[1]
    n_groups = d_model // LANES
    ta = _tile(seq, ELEMENTWISE_ROWS, HALO)
    n_tiles = n_tok // ta
    per_seq = seq // ta
    rc = _conv_rows(ta)
    last_halo = n_tok // HALO - 1
    r8 = d_model // N_DEV
    prow = _tile(r8, 32, 16)

    def body(dproj_in, av_ref, ag_ref, avh_ref, agh_ref, dh1_ref, dh1h_ref, cw_ref, gp_bf_ref, gp_f32_ref,
             dproj_ref, vec_ref, dcw_ref, opa_ref, opb_ref, opo_ref,
             ext_h0, ext_d, rbuf, own, send_sems, recv_sems, local_sems):
        del dproj_in
        i = pl.program_id(0)
        x, y, c = _mesh_pos()
        me = _block_of((x, y, c))
        peers = _peers()

        def sends():
            return [pltpu.make_async_remote_copy(
                src_ref=gp_bf_ref.at[a, pl.ds(pl.multiple_of(blk * r8, 16), r8), :], dst_ref=rbuf.at[k, a],
                send_sem=send_sems.at[a, k], recv_sem=recv_sems.at[a, k], device_id=pos, device_id_type=MESH)
                for k, (pos, blk) in enumerate(peers) for a in range(3)]

        def own_rows():
            return [pltpu.make_async_copy(gp_f32_ref.at[a, pl.ds(pl.multiple_of(me * r8, 8), r8), :],
                                          own.at[a], local_sems.at[a]) for a in range(3)]

        @pl.when(i == 0)
        def _():
            vec_ref[...] = jnp.zeros_like(vec_ref)
            dcw_ref[...] = jnp.zeros_like(dcw_ref)
            for cp in sends() + own_rows():
                cp.start()

        keep_past = jnp.where(i % per_seq == 0, 0.0, 1.0)
        keep_next = jnp.where(i % per_seq == per_seq - 1, 0.0, 1.0)

        def group(g, carry):
            sl = pl.ds(pl.multiple_of(g * LANES, LANES), LANES)
            av = av_ref[:, sl]
            sig = _sigmoid(ag_ref[:, sl])
            ext_h0[0:HALO, :] = avh_ref[:, sl] * _sigmoid(agh_ref[:, sl]) * keep_past
            ext_h0[HALO:HALO + ta, :] = av * sig
            ext_d[0:ta, :] = dh1_ref[:, sl]
            ext_d[ta:ta + HALO, :] = dh1h_ref[:, sl] * keep_next
            for r0 in range(0, ta, rc):
                dh1 = ext_d[pl.ds(r0, rc), :]
                acc = jnp.zeros((rc, LANES), F32)
                for k in range(CONV_K):
                    acc = acc + ext_d[pl.ds(r0 + CONV_K - 1 - k, rc), :] * cw_ref[k:k + 1, sl]
                    prod = dh1 * ext_h0[pl.ds(r0 + HALO - (CONV_K - 1) + k, rc), :]
                    dcw_ref[g, k] += jnp.sum(prod.reshape(rc // SUBLANES, SUBLANES, LANES), axis=0)
                rows = pl.ds(r0, rc)
                sig_r = sig[r0:r0 + rc]
                dav = acc * sig_r
                dag = dav * av[r0:r0 + rc] * (1.0 - sig_r)
                dproj_ref[rows, sl] = dav.astype(BF16)
                dproj_ref[rows, pl.ds(pl.multiple_of(d_model + g * LANES, LANES), LANES)] = dag.astype(BF16)
                vec_ref[0:1, sl] += _colsum(dav)
                vec_ref[1:2, sl] += _colsum(dag)
            return carry

        lax.fori_loop(0, n_groups, group, 0, unroll=True)

        @pl.when(i == n_tiles - 1)
        def _():
            for cp in own_rows():
                cp.wait()
            for cp in sends():
                cp.wait_recv()
            for a, o in enumerate([opa_ref, opb_ref, opo_ref]):
                for q in range(r8 // prow):
                    r = pl.ds(q * prow, prow)
                    tot = own[a, r, :]
                    for k in range(N_DEV - 1):
                        tot = tot + rbuf[k, a, r, :].astype(F32)
                    o[r, :] = tot
            for cp in sends():
                cp.wait_send()

    blk = lambda j: pl.BlockSpec((ta, d_model), lambda i: (i, j))
    halo = lambda j: pl.BlockSpec((HALO, d_model), lambda i: (jnp.maximum(i * (ta // HALO) - 1, 0), j))
    shard = pl.BlockSpec((r8, d_model), lambda i: (0, 0))
    return pl.pallas_call(
        body, name="branch_a_bwd_conv", grid=(n_tiles,),
        in_specs=[ANY, blk(0), blk(1), halo(0), halo(1), pl.BlockSpec((ta, d_model), lambda i: (i, 0)),
                  pl.BlockSpec((HALO, d_model), lambda i: (jnp.minimum((i + 1) * (ta // HALO), last_halo), 0)),
                  pl.BlockSpec((HALO, d_model), lambda i: (0, 0)), ANY, ANY],
        out_specs=[pl.BlockSpec((ta, 2 * d_model), lambda i: (i, 0)),
                   pl.BlockSpec((A2_ROWS, d_model), lambda i: (0, 0)),
                   pl.BlockSpec((n_groups, HALO, SUBLANES, LANES), lambda i: (0, 0, 0, 0)),
                   shard, shard, shard],
        out_shape=[SDS(dproj.shape, BF16), SDS((A2_ROWS, d_model), F32),
                   SDS((n_groups, HALO, SUBLANES, LANES), F32)] + [SDS((r8, d_model), F32)] * 3,
        scratch_shapes=[pltpu.VMEM((HALO + ta, LANES), F32), pltpu.VMEM((ta + HALO, LANES), F32),
                        pltpu.VMEM((N_DEV - 1, 3, r8, d_model), BF16), pltpu.VMEM((3, r8, d_model), F32),
                        pltpu.SemaphoreType.DMA((3, 7)), pltpu.SemaphoreType.DMA((3, 7)),
                        pltpu.SemaphoreType.DMA((3,))],
        input_output_aliases={0: 0},
        compiler_params=_params(1),
    )(dproj, proj, proj, proj, proj, d_h1, d_h1, conv_w_full, gp_bf, gp_f32)


def _weight_grads(lhs3, rhs3):
    n_mat, n_tok, d_model = lhs3.shape
    tk = _tile(n_tok, 1024, 16)
    n_k = n_tok // tk

    def body(a_ref, g_ref, o_ref, ob_ref):
        part = _dot_ta(a_ref[...], g_ref[...])

        @pl.when(pl.program_id(1) == 0)
        def _():
            o_ref[...] = part

        @pl.when(pl.program_id(1) != 0)
        def _():
            o_ref[...] += part

        @pl.when(pl.program_id(1) == n_k - 1)
        def _():
            ob_ref[...] = o_ref[...].astype(BF16)

    tile = pl.BlockSpec((None, tk, d_model), lambda a, i: (a, i, 0))
    out = pl.BlockSpec((None, d_model, d_model), lambda a, i: (a, 0, 0))
    return pl.pallas_call(
        body, name="grad_w_pa_pb_o", grid=(n_mat, n_k), in_specs=[tile, tile], out_specs=[out, out],
        out_shape=[SDS((n_mat, d_model, d_model), F32), SDS((n_mat, d_model, d_model), BF16)],
        compiler_params=_params(2),
    )(lhs3, rhs3)


def _grad_w_in_reduce_scatter(xt_bf, dproj, blk_order, small_part):
    n_tok = dproj.shape[0]
    d_model = xt_bf.shape[0]
    dh = d_model // 2
    n_units = 2 * N_DEV
    rsl = small_part.shape[0] // N_DEV

    def body(ord_ref, a_ref, g_ref, p_ref, o_ref, small_ref, acc, fb, sbuf, gbuf, tbuf, rfin, rbuf_s, red,
             send_f, send_s, recv_g, recv_t, recv_f, out_sems, send1, recv1, send2, recv2):
        del ord_ref
        u = pl.program_id(0)
        s = u // 2
        hf = u % 2
        rnd = s // 2
        x, y, c = _mesh_pos()
        sibling = (x, y, 1 - c)
        me = _block_of((x, y, c))
        peers = _peers()

        def rows_of(blk):
            return pl.ds(pl.multiple_of(blk * rsl, SUBLANES), rsl)

        def scatter():
            return [pltpu.make_async_remote_copy(
                src_ref=p_ref.at[rows_of(blk), :], dst_ref=rbuf_s.at[k], send_sem=send1.at[k],
                recv_sem=recv1.at[k], device_id=pos, device_id_type=MESH) for k, (pos, blk) in enumerate(peers)]

        def gather(dst_block=None):
            return [pltpu.make_async_remote_copy(
                src_ref=red, dst_ref=small_ref.at[rows_of(me if dst_block is None else blk), :],
                send_sem=send2.at[k], recv_sem=recv2.at[k], device_id=pos, device_id_type=MESH)
                for k, (pos, blk) in enumerate(peers)]

        def own_slice():
            return pltpu.make_async_copy(red, small_ref.at[rows_of(me), :], out_sems.at[2])

        @pl.when(u == 0)
        def _():
            for cp in scatter():
                cp.start()

        @pl.when(u == 4)
        def _():
            for cp in scatter():
                cp.wait_recv()
            tot = p_ref[rows_of(me), :]
            for k in range(N_DEV - 1):
                tot = tot + rbuf_s[k]
            red[...] = tot
            own_slice().start()
            for cp in gather():
                cp.start()

        n1 = (jnp.bitwise_xor(x, c), jnp.bitwise_xor(y, 1 - c), c)
        n2 = (jnp.bitwise_xor(x, 1 - c), jnp.bitwise_xor(y, c), c)

        def feed(r, half):
            return pltpu.make_async_remote_copy(
                src_ref=fb.at[half], dst_ref=gbuf.at[r, half], send_sem=send_f.at[r, half],
                recv_sem=recv_g.at[r, half], device_id=sibling, device_id_type=MESH)

        def feed_sibling(half):
            return pltpu.make_async_remote_copy(
                src_ref=fb.at[half], dst_ref=rfin.at[0, half], send_sem=send_f.at[3, half],
                recv_sem=recv_f.at[0, half], device_id=sibling, device_id_type=MESH)

        def chip_sum(r, half):
            dst = [tbuf.at[half], rfin.at[2, half], rfin.at[1, half]][r]
            sem = [recv_t.at[half], recv_f.at[2, half], recv_f.at[1, half]][r]
            return pltpu.make_async_remote_copy(
                src_ref=sbuf.at[r, half], dst_ref=dst, send_sem=send_s.at[r, half], recv_sem=sem,
                device_id=[n2, n2, n1][r], device_id_type=MESH)

        def out_copy(half):
            return pltpu.make_async_copy(acc.at[half], o_ref.at[:, pl.ds(half * dh, dh)], out_sems.at[half])

        def partial_sum():
            return _dot(a_ref[:, 0:n_tok], g_ref[...])

        for half in range(2):
            for r in range(3):
                @pl.when(u == 4 * r + 4 + half)
                def _(r=r, half=half):
                    feed(r, half).wait_send()

                @pl.when(u == 4 * r + 2 + half)
                def _(r=r, half=half):
                    feed(r, half).wait_recv()
                    if r == 2:
                        chip_sum(0, half).wait_recv()

            @pl.when(u == 14 + half)
            def _(half=half):
                feed_sibling(half).wait_recv()
                chip_sum(2, half).wait_recv()
                chip_sum(1, half).wait_recv()

        @pl.when(jnp.logical_and(s % 2 == 0, s < 7))
        def _():
            fb[hf] = partial_sum().astype(BF16)

        @pl.when(jnp.logical_or(s == 1, s == 3))
        def _():
            sbuf[rnd, hf] = (partial_sum() + gbuf[rnd, hf].astype(F32)).astype(BF16)

        @pl.when(s == 5)
        def _():
            sbuf[2, hf] = (partial_sum() + gbuf[2, hf].astype(F32) + tbuf[hf].astype(F32)).astype(BF16)

        @pl.when(s == 7)
        def _():
            acc[hf] = (partial_sum() + rfin[0, hf].astype(F32) + rfin[1, hf].astype(F32)
                       + rfin[2, hf].astype(F32))

        for half in range(2):
            for r in range(3):
                @pl.when(u == 4 * r + half)
                def _(r=r, half=half):
                    feed(r, half).start()

                @pl.when(u == 4 * r + 2 + half)
                def _(r=r, half=half):
                    chip_sum(r, half).start()

            @pl.when(u == 12 + half)
            def _(half=half):
                feed_sibling(half).start()

        @pl.when(u == 14)
        def _():
            out_copy(0).start()

        @pl.when(u == 15)
        def _():
            out_copy(1).start()
            for half in range(2):
                feed_sibling(half).wait_send()
                for r in range(3):
                    chip_sum(r, half).wait_send()
                out_copy(half).wait()
            for cp in gather("theirs"):
                cp.wait_recv()
            for cp in scatter() + gather():
                cp.wait_send()
            own_slice().wait()

    grid_spec = pltpu.PrefetchScalarGridSpec(
        num_scalar_prefetch=1, grid=(n_units,),
        in_specs=[VMEM, pl.BlockSpec((n_tok, dh), lambda u, o: (0, 2 * o[u // 2] + u % 2)), VMEM],
        out_specs=[ANY, ANY],
        scratch_shapes=[pltpu.VMEM((2, d_model, dh), F32), pltpu.VMEM((2, d_model, dh), BF16),
                        pltpu.VMEM((3, 2, d_model, dh), BF16), pltpu.VMEM((3, 2, d_model, dh), BF16),
                        pltpu.VMEM((2, d_model, dh), BF16), pltpu.VMEM((3, 2, d_model, dh), BF16),
                        pltpu.VMEM((N_DEV - 1, rsl, LANES), F32), pltpu.VMEM((rsl, LANES), F32),
                        pltpu.SemaphoreType.DMA((4, 2)), pltpu.SemaphoreType.DMA((3, 2)),
                        pltpu.SemaphoreType.DMA((3, 2)), pltpu.SemaphoreType.DMA((2,)),
                        pltpu.SemaphoreType.DMA((3, 2)),
                        pltpu.SemaphoreType.DMA((3,))] + [pltpu.SemaphoreType.DMA((N_DEV - 1,))] * 4)
    return pl.pallas_call(
        body, name="grad_w_in_reduce_scatter", grid_spec=grid_spec,
        out_shape=[SDS((d_model, d_model), F32), SDS(small_part.shape, F32)], compiler_params=_params(1),
    )(blk_order, xt_bf, dproj, small_part)


def _grad_x_adamw(dproj, w_all, dr, w, g, m, v, groups):
    n_tok, d_model = dr.shape
    tm = _tile(n_tok, 256, 16)
    n_steps = n_tok // tm
    tr = w.shape[0] // n_steps
    n_grp = len(groups)

    def body(dp_ref, w_hbm, dr_ref, ws_ref, g_ref, m_ref, v_ref, *rest):
        grp_in, rest = rest[:4 * n_grp], rest[4 * n_grp:]
        o_ref, d_ref, mo_ref, vo_ref = rest[:4]
        grp_out, (w_ref, w_sems) = rest[4:4 + 3 * n_grp], rest[4 + 3 * n_grp:]
        i = pl.program_id(0)

        def fetch(j):
            return pltpu.make_async_copy(w_hbm.at[j], w_ref.at[j], w_sems.at[j])

        def grad_x_tile(first):
            acc = DEEPNORM_ALPHA * dr_ref[...]
            for j in range(N_DEV):
                if first:
                    fetch(j).wait()
                acc = acc + _dot_tb(dp_ref[:, j * d_model:(j + 1) * d_model], w_ref[j])
            o_ref[...] = acc

        @pl.when(i == 0)
        def _():
            for j in range(N_DEV):
                fetch(j).start()
            grad_x_tile(True)

        @pl.when(i != 0)
        def _():
            grad_x_tile(False)

        d_ref[...], mo_ref[...], vo_ref[...] = _adamw_math(ws_ref[...], g_ref[...], m_ref[...], v_ref[...])

        @pl.when(i == n_steps - 1)
        def _():
            for p in range(n_grp):
                pw, pg, pm, pv = grp_in[4 * p:4 * p + 4]
                grp_out[3 * p][...], grp_out[3 * p + 1][...], grp_out[3 * p + 2][...] = _adamw_math(
                    pw[...], pg[...], pm[...], pv[...])

    tile = pl.BlockSpec((tm, d_model), lambda i: (i, 0))
    slab = pl.BlockSpec((tr, w.shape[1]), lambda i: (i, 0))
    flat = [a for grp in groups for a in grp]
    res = pl.pallas_call(
        body, name="grad_x_adamw", grid=(n_steps,),
        in_specs=[pl.BlockSpec((tm, N_DEV * d_model), lambda i: (i, 0)), ANY, tile, slab, slab, slab, slab]
        + [VMEM] * (4 * n_grp),
        out_specs=[tile, slab, slab, slab] + [VMEM] * (3 * n_grp),
        out_shape=[SDS((n_tok, d_model), F32)] + [SDS(w.shape, F32)] * 3
        + [SDS(grp[0].shape, F32) for grp in groups for _ in range(3)],
        scratch_shapes=[pltpu.VMEM(w_all.shape, BF16), pltpu.SemaphoreType.DMA((N_DEV,))],
        compiler_params=_params(1),
    )(dproj, w_all, dr, w, g, m, v, *flat)
    return res[0], tuple(res[1:4]), [tuple(res[4 + 3 * p:7 + 3 * p]) for p in range(n_grp)]


def _adamw_math(w, g, m, v):
    m = ADAM_B1 * m + (1.0 - ADAM_B1) * g
    v = ADAM_B2 * v + (1.0 - ADAM_B2) * (g * g)
    m_hat = m / (1.0 - ADAM_B1 ** ADAM_STEP)
    v_hat = v / (1.0 - ADAM_B2 ** ADAM_STEP)
    delta = -ADAM_LR * (m_hat / (jnp.sqrt(v_hat) + ADAM_EPS) + ADAM_WD * w)
    return delta, m, v


def _as_rows(a):
    return a.reshape(-1, LANES)


def kernel(x, w_in, b_in, conv_w, conv_b, gn_g, gn_b, ln_v_g, ln_v_b, w_spatial, b_spatial, w_pa, w_pb, w_o, b_o, ln_out_g, ln_out_b, loss_target, m_w_in, m_b_in, m_conv_w, m_conv_b, m_gn_g, m_gn_b, m_ln_v_g, m_ln_v_b, m_w_spatial, m_b_spatial, m_w_pa, m_w_pb, m_w_o, m_b_o, m_ln_out_g, m_ln_out_b, v_w_in, v_b_in, v_conv_w, v_conv_b, v_gn_g, v_gn_b, v_ln_v_g, v_ln_v_b, v_w_spatial, v_b_spatial, v_w_pa, v_w_pb, v_w_o, v_b_o, v_ln_out_g, v_ln_out_b):
    n_batch, seq, d_model = x.shape
    n_tok = n_batch * seq
    n_heads = d_model // LANES
    dc = conv_w.shape[1]
    me = 4 * lax.axis_index("x") + 2 * lax.axis_index("y") + lax.axis_index("c")
    row = lambda a: a.reshape(1, d_model)

    x2 = x.reshape(n_tok, d_model)
    target2 = loss_target.reshape(n_tok, d_model)
    b_spatial_t = b_spatial.T

    first = jnp.where(lax.axis_index("c") == 1, 4, 2)
    second = 6 - first
    ag_rel = jnp.stack([0 * first, 0 * first + 1, first, second + 1, second, first + 1, 0 * first + 6, 0 * first + 7])
    ag_blocks = jnp.bitwise_xor(me, ag_rel).astype(jnp.int32)
    proj, xt_bf, w_all, wp_all, cw_all = _proj_all_gather(
        x2, w_in, w_pa, w_pb, w_o, conv_w, b_in.reshape(N_DEV, 1, d_model), ag_blocks)
    wp_full = wp_all.reshape(3, d_model, d_model)
    conv_w_full = jnp.pad(cw_all.transpose(1, 0, 2).reshape(CONV_K, d_model), ((0, HALO - CONV_K), (0, 0)))

    h3, h1 = _branch_a_fwd(proj, conv_w_full, row(conv_b), row(gn_g), row(gn_b), seq)
    s = _branch_b_fwd(proj, row(ln_v_g), row(ln_v_b), w_spatial, b_spatial_t, seq)

    dproj, d_h3, d_s, dr, lhs3, rhs3, vec_mid = _mid(
        h3, s, proj, x2, target2, wp_full, row(b_o), row(ln_out_g), row(ln_out_b))

    dproj, vec_b, d_ws, d_bs_t = _branch_b_bwd(dproj, proj, d_s, row(ln_v_g), row(ln_v_b), w_spatial, b_spatial_t, seq)
    dproj, d_h1, vec_a1 = _branch_a_bwd_norm(dproj, proj, h1, d_h3, row(gn_g), row(gn_b), seq)
    gp_f32, gp_bf = _weight_grads(lhs3, rhs3)
    dproj, vec_a2, d_cw8, g_w_pa, g_w_pb, g_w_o = _branch_a_bwd_conv(
        dproj, proj, d_h1, conv_w_full, gp_bf, gp_f32, seq)

    d_cw = jnp.sum(d_cw8, axis=2)
    pieces = [
        _as_rows(jnp.concatenate([vec_a2[0:2], vec_a1[0:1], vec_b[0:3], vec_mid[3:5]], axis=0)),
        _as_rows(jnp.concatenate([vec_a1[3:4], vec_a1[1:3], vec_b[3:5], vec_mid[2:3], vec_mid[0:2]], axis=0)),
        _as_rows(d_bs_t.T), _as_rows(d_ws), _as_rows(d_cw), _as_rows(vec_mid[5:6]),
    ]
    n_rows = sum(p.shape[0] for p in pieces)
    pad_rows = -n_rows % (N_DEV * SUBLANES)
    small_part = jnp.concatenate(pieces + [jnp.zeros((pad_rows, LANES), F32)], axis=0)

    rs_rel = jnp.stack([0 * first + 7, 0 * first + 6, first + 1, second, second + 1, first, 0 * first + 1, 0 * first])
    rs_blocks = jnp.bitwise_xor(me, rs_rel).astype(jnp.int32)
    g_w_in, small = _grad_w_in_reduce_scatter(xt_bf, dproj, rs_blocks, small_part)

    g_rows = d_model // LANES
    o0 = N_DEV * g_rows
    g_b_in = small[0:o0].reshape(N_DEV * d_model)
    vecs = [small[o0 + a * g_rows:o0 + (a + 1) * g_rows].reshape(d_model) for a in range(8)]
    g_conv_b, g_gn_g, g_gn_b, g_ln_v_g, g_ln_v_b, g_b_o, g_ln_out_g, g_ln_out_b = vecs
    o1 = o0 + 8 * g_rows
    g_b_spatial = small[o1:o1 + n_heads].reshape(n_heads, LANES)
    o2 = o1 + n_heads
    g_w_spatial = small[o2:o2 + n_heads * LANES].reshape(n_heads, LANES, LANES)
    o3 = o2 + n_heads * LANES
    g_cw_full = small[o3:o3 + n_heads * HALO].reshape(n_heads, HALO, LANES).transpose(1, 0, 2).reshape(HALO, d_model)
    g_conv_w = lax.dynamic_slice(g_cw_full, (0, me * dc), (CONV_K, dc))
    o4 = o3 + n_heads * HALO
    loss = jnp.sum(small[o4:o4 + g_rows]) * (0.5 / d_model)

    two_d = lambda a: a.reshape(-1, a.shape[-1]) if a.ndim != 1 else (
        a.reshape(-1, LANES) if a.shape[0] % LANES == 0 else a.reshape(1, -1))
    names = ["b_in", "conv_w", "conv_b", "gn_g", "gn_b", "ln_v_g", "ln_v_b", "w_spatial", "b_spatial",
             "w_pa", "w_pb", "w_o", "b_o", "ln_out_g", "ln_out_b"]
    ws = dict(b_in=b_in, conv_w=conv_w, conv_b=conv_b, gn_g=gn_g, gn_b=gn_b, ln_v_g=ln_v_g, ln_v_b=ln_v_b,
              w_spatial=w_spatial, b_spatial=b_spatial, w_pa=w_pa, w_pb=w_pb, w_o=w_o, b_o=b_o,
              ln_out_g=ln_out_g, ln_out_b=ln_out_b)
    gs = dict(b_in=g_b_in, conv_w=g_conv_w, conv_b=g_conv_b, gn_g=g_gn_g, gn_b=g_gn_b, ln_v_g=g_ln_v_g,
              ln_v_b=g_ln_v_b, w_spatial=g_w_spatial, b_spatial=g_b_spatial, w_pa=g_w_pa, w_pb=g_w_pb,
              w_o=g_w_o, b_o=g_b_o, ln_out_g=g_ln_out_g, ln_out_b=g_ln_out_b)
    ms = dict(b_in=m_b_in, conv_w=m_conv_w, conv_b=m_conv_b, gn_g=m_gn_g, gn_b=m_gn_b, ln_v_g=m_ln_v_g,
              ln_v_b=m_ln_v_b, w_spatial=m_w_spatial, b_spatial=m_b_spatial, w_pa=m_w_pa, w_pb=m_w_pb,
              w_o=m_w_o, b_o=m_b_o, ln_out_g=m_ln_out_g, ln_out_b=m_ln_out_b)
    vs = dict(b_in=v_b_in, conv_w=v_conv_w, conv_b=v_conv_b, gn_g=v_gn_g, gn_b=v_gn_b, ln_v_g=v_ln_v_g,
              ln_v_b=v_ln_v_b, w_spatial=v_w_spatial, b_spatial=v_b_spatial, w_pa=v_w_pa, w_pb=v_w_pb,
              w_o=v_w_o, b_o=v_b_o, ln_out_g=v_ln_out_g, ln_out_b=v_ln_out_b)
    grad_x, (d_w_in, nm_w_in, nv_w_in), upd = _grad_x_adamw(
        dproj, w_all, dr, w_in, g_w_in, m_w_in, v_w_in,
        [tuple(two_d(d[n]) for d in (ws, gs, ms, vs)) for n in names])
    grad_x = grad_x.reshape(x.shape)
    delta = {n: u[0].reshape(ws[n].shape) for n, u in zip(names, upd)}
    new_m = {n: u[1].reshape(ws[n].shape) for n, u in zip(names, upd)}
    new_v = {n: u[2].reshape(ws[n].shape) for n, u in zip(names, upd)}
    gs["w_in"], delta["w_in"], new_m["w_in"], new_v["w_in"] = g_w_in, d_w_in, nm_w_in, nv_w_in

    order = ["w_in"] + names
    return (loss, grad_x, *[gs[n] for n in order], *[delta[n] for n in order],
            *[new_m[n] for n in order], *[new_v[n] for n in order])
```

```python
import jax
import jax.numpy as jnp
from jax import lax
from jax.experimental import pallas as pl
from jax.experimental.pallas import tpu as pltpu

F32 = jnp.float32
BF16 = jnp.bfloat16
SDS = jax.ShapeDtypeStruct

N_DEV = 8
LANES = 128
SUBLANES = 8
CONV_K = 31
HALO = 32
ELEMENTWISE_ROWS = 512
LN_EPS = 1e-5
DEEPNORM_ALPHA = 2.0 ** 0.25
ADAM_LR, ADAM_B1, ADAM_B2, ADAM_EPS, ADAM_WD, ADAM_STEP = 0.001, 0.9, 0.999, 1e-08, 0.01, 10
GELU_C = 0.7978845608028654
GELU_A = 0.044715
VMEM_LIMIT = 56 * 1024 * 1024
MESH = pl.DeviceIdType.MESH
ANY = pl.BlockSpec(memory_space=pl.ANY)
VMEM = pl.BlockSpec(memory_space=pltpu.VMEM)


def _params(n_grid=0):
    sem = ("arbitrary",) * n_grid if n_grid else None
    return pltpu.CompilerParams(dimension_semantics=sem, vmem_limit_bytes=VMEM_LIMIT)


def _tile(n, pref, mult):
    t = min(n, pref)
    while n % t or t % mult:
        t -= 1
    return t


def _colsum(v):
    return jnp.sum(v, axis=0, keepdims=True)


def _sigmoid(v):
    return jax.nn.sigmoid(v)


def _silu_and_grad(v):
    s = _sigmoid(v)
    val = v * s
    return val, s + val * (1.0 - s)


def _gelu_and_grad(v):
    v2 = v * v
    sg = _sigmoid(v * (2.0 * GELU_C + (2.0 * GELU_C * GELU_A) * v2))
    grad = sg + v * sg * (1.0 - sg) * (2.0 * GELU_C + (6.0 * GELU_C * GELU_A) * v2)
    return v * sg, grad


def _tril_mask():
    r = lax.broadcasted_iota(jnp.int32, (LANES, LANES), 0)
    c = lax.broadcasted_iota(jnp.int32, (LANES, LANES), 1)
    return c <= r


def _dot(a, b):
    return jnp.dot(a, b, preferred_element_type=F32)


def _dot_tb(a, b):
    return lax.dot_general(a, b, (((1,), (1,)), ((), ())), preferred_element_type=F32)


def _dot_ta(a, b):
    return lax.dot_general(a, b, (((0,), (0,)), ((), ())), preferred_element_type=F32)


def _mesh_pos():
    return lax.axis_index("x"), lax.axis_index("y"), lax.axis_index("c")


def _block_of(pos):
    return 4 * pos[0] + 2 * pos[1] + pos[2]


def _peers():
    x, y, c = _mesh_pos()
    out = []
    for k in range(1, N_DEV):
        pos = (1 - x if k & 4 else x, 1 - y if k & 2 else y, 1 - c if k & 1 else c)
        out.append((pos, _block_of(pos)))
    return out


def _proj_all_gather(x2, w_in, w_pa, w_pb, w_o, conv_w, b_in3, blk_order):
    n_tok, d_model = x2.shape
    r8 = w_pa.shape[0]
    kc, dc = conv_w.shape
    tm = _tile(n_tok, 1024, LANES)
    n_t = n_tok // tm
    n_arr = 3

    def body(ord_ref, x_ref, b_ref, win_ref, wpa_ref, wpb_ref, wo_ref, cw_ref,
             proj_ref, xt_ref, wall_ref, wp_ref, cwall_ref,
             wbuf, xbuf, st_p, send_sems, recv_sems, local_sems, wall_sems):
        s = pl.program_id(0)
        t = pl.program_id(1)
        x, y, c = _mesh_pos()
        me = (x, y, c)
        sibling = (x, y, 1 - c)
        n1 = (jnp.bitwise_xor(x, c), jnp.bitwise_xor(y, 1 - c))
        n2 = (jnp.bitwise_xor(x, 1 - c), jnp.bitwise_xor(y, c))
        dg = (1 - x, 1 - y)
        outs = [wbuf, wp_ref, cwall_ref]
        srcs = [None, st_p, cw_ref]
        consumed = [me, sibling, (*n1, c), (*n2, 1 - c), (*n2, c), (*n1, 1 - c), (*dg, c), (*dg, 1 - c)]
        leaves = [(me, sibling), (me, (*n1, c)), (me, (*n2, c)), ((*n1, c), (*n2, c)),
                  ((*n1, c), sibling), ((*n2, c), sibling), ((*dg, c), sibling)]
        lands = [sibling, (*n1, c), (*n2, c), (*dg, c), (*n2, 1 - c), (*n1, 1 - c), (*dg, 1 - c)]

        def slot(o, pos):
            return o.at[:, _block_of(pos)] if o is wp_ref else o.at[_block_of(pos)]

        def copy(a, k, block, to, src=None):
            o = outs[a]
            return pltpu.make_async_remote_copy(
                src_ref=slot(o, block) if src is None else src, dst_ref=slot(o, block),
                send_sem=send_sems.at[a, k], recv_sem=recv_sems.at[a, k],
                device_id=to, device_id_type=MESH)

        def send(a, k):
            block, to = leaves[k]
            return copy(a, k, block, to, src=srcs[a] if k < 3 else None)

        def recv(a, k):
            return copy(a, k, lands[k], me)

        def local_copies():
            return [pltpu.make_async_copy(srcs[a], slot(outs[a], me), local_sems.at[a]) for a in (1, 2)]

        def to_hbm(step):
            return pltpu.make_async_copy(slot(wbuf, consumed[step]), slot(wall_ref, consumed[step]),
                                         wall_sems.at[step])

        def at_step(step):
            return pl.when(jnp.logical_and(s == step, t == 0))

        @at_step(0)
        def _():
            slot(wbuf, me)[...] = win_ref[...].astype(BF16)
            st_p[0] = wpa_ref[...].astype(BF16)
            st_p[1] = wpb_ref[...].astype(BF16)
            st_p[2] = wo_ref[...].astype(BF16)
            for a in range(n_arr):
                send(a, 0).start()
                send(a, 1).start()
            for cp in local_copies():
                cp.start()
            to_hbm(0).start()

        @at_step(1)
        def _():
            recv(0, 0).wait_recv()
            to_hbm(1).start()

        for rnd in range(3):
            @at_step(2 + 2 * rnd)
            def _(rnd=rnd):
                if rnd == 0:
                    for a in range(n_arr):
                        send(a, 2).start()
                recv(0, 1 + rnd).wait_recv()
                if rnd == 0:
                    send(0, 3).start()
                send(0, 4 + rnd).start()
                to_hbm(2 + 2 * rnd).start()

            @at_step(3 + 2 * rnd)
            def _(rnd=rnd):
                for a in (1, 2):
                    recv(a, 1 + rnd).wait_recv()
                    if rnd == 0:
                        send(a, 3).start()
                    send(a, 4 + rnd).start()
                recv(0, 4 + rnd).wait_recv()
                to_hbm(3 + 2 * rnd).start()

        rows = pl.ds(pl.multiple_of(t * tm, tm), tm)

        @pl.when(s == 0)
        def _():
            xb = x_ref[...].astype(BF16)
            xbuf[rows, :] = xb
            xt_ref[...] = xb.T

        proj_ref[...] = _dot(xbuf[rows, :], wbuf[ord_ref[s]]) + b_ref[...]

        @pl.when(jnp.logical_and(s == N_DEV - 1, t == n_t - 1))
        def _():
            for a in (1, 2):
                for k in (0, 4, 5, 6):
                    recv(a, k).wait_recv()
            for a in range(n_arr):
                for k in range(7):
                    send(a, k).wait_send()
            for cp in local_copies() + [to_hbm(step) for step in range(N_DEV)]:
                cp.wait()

    grid_spec = pltpu.PrefetchScalarGridSpec(
        num_scalar_prefetch=1, grid=(N_DEV, n_t),
        in_specs=[pl.BlockSpec((tm, d_model), lambda s, t, o: (jnp.where(s == 0, t, n_t - 1), 0)),
                  pl.BlockSpec((None, 1, d_model), lambda s, t, o: (o[s], 0, 0)),
                  VMEM, VMEM, VMEM, VMEM, VMEM],
        out_specs=[pl.BlockSpec((tm, d_model), lambda s, t, o: (t, o[s])),
                   pl.BlockSpec((d_model, tm), lambda s, t, o: (0, jnp.where(s == 0, t, n_t))),
                   ANY, ANY, ANY],
        scratch_shapes=[pltpu.VMEM((N_DEV, d_model, d_model), BF16), pltpu.VMEM((n_tok, d_model), BF16),
                        pltpu.VMEM((3, r8, d_model), BF16),
                        pltpu.SemaphoreType.DMA((n_arr, 7)), pltpu.SemaphoreType.DMA((n_arr, 7)),
                        pltpu.SemaphoreType.DMA((3,)), pltpu.SemaphoreType.DMA((N_DEV,))])
    return pl.pallas_call(
        body, name="proj_all_gather", grid_spec=grid_spec,
        out_shape=[SDS((n_tok, N_DEV * d_model), F32), SDS((d_model, (n_t + 1) * tm), BF16),
                   SDS((N_DEV, d_model, d_model), BF16), SDS((3, N_DEV, r8, d_model), BF16),
                   SDS((N_DEV, kc, dc), F32)],
        compiler_params=_params(2),
    )(blk_order, x2, b_in3, w_in, w_pa, w_pb, w_o, conv_w)


def _conv_rows(ta):
    return _tile(ta, 64, SUBLANES)


def _branch_a_fwd(proj, conv_w_full, conv_b, gn_g, gn_b, seq):
    n_tok = proj.shape[0]
    d_model = conv_b.shape[1]
    ta = _tile(seq, ELEMENTWISE_ROWS, HALO)
    per_seq = seq // ta
    rc = _conv_rows(ta)

    def body(av_ref, ag_ref, gt_ref, avh_ref, agh_ref, cw_ref, cb_ref, gg_ref, gb_ref,
             h3_ref, h1_ref, ext):
        keep = jnp.where(pl.program_id(0) % per_seq == 0, 0.0, 1.0)

        def group(g, carry):
            sl = pl.ds(pl.multiple_of(g * LANES, LANES), LANES)
            ext[0:HALO, :] = avh_ref[:, sl] * _sigmoid(agh_ref[:, sl]) * keep
            ext[HALO:HALO + ta, :] = av_ref[:, sl] * _sigmoid(ag_ref[:, sl])
            for r0 in range(0, ta, rc):
                acc = jnp.broadcast_to(cb_ref[:, sl], (rc, LANES))
                for k in range(CONV_K):
                    acc = acc + ext[pl.ds(r0 + HALO - (CONV_K - 1) + k, rc), :] * cw_ref[k:k + 1, sl]
                h1_ref[pl.ds(r0, rc), sl] = acc
            h1 = h1_ref[:, sl]
            mu = jnp.mean(h1, axis=-1, keepdims=True)
            dlt = h1 - mu
            var = jnp.mean(dlt * dlt, axis=-1, keepdims=True)
            h2 = dlt * lax.rsqrt(var + LN_EPS) * gg_ref[:, sl] + gb_ref[:, sl]
            gate = gt_ref[:, sl]
            h3_ref[:, sl] = (h2 * _sigmoid(h2) * gate * _sigmoid(gate)).astype(BF16)
            return carry

        lax.fori_loop(0, d_model // LANES, group, 0, unroll=True)

    blk = lambda j: pl.BlockSpec((ta, d_model), lambda i: (i, j))
    halo = lambda j: pl.BlockSpec((HALO, d_model), lambda i: (jnp.maximum(i * (ta // HALO) - 1, 0), j))
    row = pl.BlockSpec((1, d_model), lambda i: (0, 0))
    return pl.pallas_call(
        body, name="branch_a_fwd", grid=(n_tok // ta,),
        in_specs=[blk(0), blk(1), blk(2), halo(0), halo(1),
                  pl.BlockSpec((HALO, d_model), lambda i: (0, 0)), row, row, row],
        out_specs=[pl.BlockSpec((ta, d_model), lambda i: (i, 0))] * 2,
        out_shape=[SDS((n_tok, d_model), BF16), SDS((n_tok, d_model), F32)],
        scratch_shapes=[pltpu.VMEM((HALO + ta, LANES), F32)],
        compiler_params=_params(1),
    )(proj, proj, proj, proj, proj, conv_w_full, conv_b, gn_g, gn_b)


def _branch_b_fwd(proj, ln_g, ln_b, w_spatial, b_spatial_t, seq):
    n_tok = proj.shape[0]
    d_model = ln_g.shape[1]
    n_heads = d_model // LANES
    tb = _tile(seq, ELEMENTWISE_ROWS, LANES)

    def body(u_ref, v_ref, bg_ref, lg_ref, lb_ref, ws_ref, bs_ref, s_ref, vn_buf):
        v, _ = _gelu_and_grad(v_ref[...])
        mu = jnp.mean(v, axis=-1, keepdims=True)
        dlt = v - mu
        var = jnp.mean(dlt * dlt, axis=-1, keepdims=True)
        vn_buf[...] = (dlt * lax.rsqrt(var + LN_EPS) * lg_ref[...] + lb_ref[...]).astype(BF16)
        tril = _tril_mask()
        for h in range(n_heads):
            cols = slice(h * LANES, (h + 1) * LANES)
            w_h = jnp.where(tril, ws_ref[h], 0.0).astype(BF16)
            bias = bs_ref[:, h:h + 1]
            for ch in range(tb // LANES):
                rows = slice(ch * LANES, (ch + 1) * LANES)
                mix = _dot(w_h, vn_buf[rows, cols]) + bias
                u, _ = _gelu_and_grad(u_ref[rows, cols])
                gate = bg_ref[rows, cols]
                s_ref[rows, cols] = (u * mix * gate * _sigmoid(gate)).astype(BF16)

    blk = lambda j: pl.BlockSpec((tb, d_model), lambda i: (i, j))
    row = pl.BlockSpec((1, d_model), lambda i: (0, 0))
    return pl.pallas_call(
        body, name="branch_b_fwd", grid=(n_tok // tb,),
        in_specs=[blk(3), blk(4), blk(5), row, row,
                  pl.BlockSpec((n_heads, LANES, LANES), lambda i: (0, 0, 0)),
                  pl.BlockSpec((LANES, n_heads), lambda i: (0, 0))],
        out_specs=pl.BlockSpec((tb, d_model), lambda i: (i, 0)),
        out_shape=SDS((n_tok, d_model), BF16),
        scratch_shapes=[pltpu.VMEM((tb, d_model), BF16)],
        compiler_params=_params(1),
    )(proj, proj, proj, ln_g, ln_b, w_spatial, b_spatial_t)


MID_ROWS = 8


def _mid(h3, s, proj, x2, target, wp_full, b_o, lo_g, lo_b):
    n_tok, d_model = x2.shape
    tm = _tile(n_tok, 256, 16)

    def body(h3_ref, s_ref, ma_ref, mb_ref, x_ref, t_ref, wpa_ref, wpb_ref, wo_ref, bo_ref,
             lg_ref, lb_ref, dproj_ref, dh3_ref, ds_ref, dr_ref, lhs3_ref, rhs3_ref, vec_ref):
        @pl.when(pl.program_id(0) == 0)
        def _():
            vec_ref[...] = jnp.zeros_like(vec_ref)

        h3 = h3_ref[...]
        s = s_ref[...]
        ya = _dot(h3, wpa_ref[...])
        yb = _dot(s, wpb_ref[...])
        ga = _sigmoid(ma_ref[...])
        gb = _sigmoid(mb_ref[...])
        mixed = (ga * ya + gb * yb).astype(BF16)
        lhs3_ref[0] = h3
        lhs3_ref[1] = s
        lhs3_ref[2] = mixed
        r = DEEPNORM_ALPHA * x_ref[...] + _dot(mixed, wo_ref[...]) + bo_ref[...]
        mu = jnp.mean(r, axis=-1, keepdims=True)
        dlt = r - mu
        rstd = lax.rsqrt(jnp.mean(dlt * dlt, axis=-1, keepdims=True) + LN_EPS)
        rhat = dlt * rstd
        diff = rhat * lg_ref[...] + lb_ref[...] - t_ref[...]
        dy = diff * (1.0 / d_model)
        vec_ref[0:1, :] += _colsum(dy * rhat)
        vec_ref[1:2, :] += _colsum(dy)
        vec_ref[5:6, :] += _colsum(diff * diff)
        drh = dy * lg_ref[...]
        dr = rstd * (drh - jnp.mean(drh, axis=-1, keepdims=True)
                     - rhat * jnp.mean(drh * rhat, axis=-1, keepdims=True))
        vec_ref[2:3, :] += _colsum(dr)
        dr_ref[...] = dr
        drb = dr.astype(BF16)
        rhs3_ref[2] = drb
        dmixed = _dot_tb(drb, wo_ref[...])
        dya_f = dmixed * ga
        dyb_f = dmixed * gb
        dma = dya_f * ya * (1.0 - ga)
        dmb = dyb_f * yb * (1.0 - gb)
        vec_ref[3:4, :] += _colsum(dma)
        vec_ref[4:5, :] += _colsum(dmb)
        dproj_ref[:, 0:d_model] = dma.astype(BF16)
        dproj_ref[:, d_model:2 * d_model] = dmb.astype(BF16)
        dya = dya_f.astype(BF16)
        dyb = dyb_f.astype(BF16)
        rhs3_ref[0] = dya
        rhs3_ref[1] = dyb
        dh3_ref[...] = _dot_tb(dya, wpa_ref[...])
        ds_ref[...] = _dot_tb(dyb, wpb_ref[...])

    tile = pl.BlockSpec((tm, d_model), lambda i: (i, 0))
    full = lambda a: pl.BlockSpec((None, d_model, d_model), lambda i: (a, 0, 0))
    row = pl.BlockSpec((1, d_model), lambda i: (0, 0))
    stack = pl.BlockSpec((3, tm, d_model), lambda i: (0, i, 0))
    bf3 = SDS((3, n_tok, d_model), BF16)
    f32 = SDS((n_tok, d_model), F32)
    return pl.pallas_call(
        body, name="mid", grid=(n_tok // tm,),
        in_specs=[tile, tile, pl.BlockSpec((tm, d_model), lambda i: (i, 6)),
                  pl.BlockSpec((tm, d_model), lambda i: (i, 7)), tile, tile, full(0), full(1), full(2),
                  row, row, row],
        out_specs=[pl.BlockSpec((tm, 2 * d_model), lambda i: (i, 3)), tile, tile, tile, stack, stack,
                   pl.BlockSpec((MID_ROWS, d_model), lambda i: (0, 0))],
        out_shape=[SDS((n_tok, N_DEV * d_model), BF16), f32, f32, f32, bf3, bf3,
                   SDS((MID_ROWS, d_model), F32)],
        compiler_params=_params(1),
    )(h3, s, proj, proj, x2, target, wp_full, wp_full, wp_full, b_o, lo_g, lo_b)


B_ROWS = 8


def _branch_b_bwd(dproj, proj, d_s, ln_g, ln_b, w_spatial, b_spatial_t, seq):
    n_tok = proj.shape[0]
    d_model = ln_g.shape[1]
    n_heads = d_model // LANES
    tb = _tile(seq, ELEMENTWISE_ROWS, LANES)

    def body(dproj_in, u_ref, v_ref, bg_ref, ds_ref, lg_ref, lb_ref, ws_ref, bs_ref,
             dproj_ref, vec_ref, dws_ref, dbs_ref, vn_buf, dv_buf):
        del dproj_in

        @pl.when(pl.program_id(0) == 0)
        def _():
            vec_ref[...] = jnp.zeros_like(vec_ref)
            dws_ref[...] = jnp.zeros_like(dws_ref)
            dbs_ref[...] = jnp.zeros_like(dbs_ref)

        v, dgelu_v = _gelu_and_grad(v_ref[...])
        mu = jnp.mean(v, axis=-1, keepdims=True)
        dlt = v - mu
        rstd = lax.rsqrt(jnp.mean(dlt * dlt, axis=-1, keepdims=True) + LN_EPS)
        vhat = dlt * rstd
        vn_buf[...] = (vhat * lg_ref[...] + lb_ref[...]).astype(BF16)
        tril = _tril_mask()
        for h in range(n_heads):
            cols = slice(h * LANES, (h + 1) * LANES)
            w_h = jnp.where(tril, ws_ref[h], 0.0).astype(BF16)
            bias = bs_ref[:, h:h + 1]
            for ch in range(tb // LANES):
                rows = slice(ch * LANES, (ch + 1) * LANES)
                vn = vn_buf[rows, cols]
                mix = _dot(w_h, vn) + bias
                u, dgelu_u = _gelu_and_grad(u_ref[rows, cols])
                sg, dsilu = _silu_and_grad(bg_ref[rows, cols])
                dsv = ds_ref[rows, cols]
                du = dsv * mix * sg * dgelu_u
                dbg = dsv * u * mix * dsilu
                dmix = dsv * u * sg
                dmix_bf = dmix.astype(BF16)
                dproj_ref[rows, cols] = du.astype(BF16)
                dproj_ref[rows, 2 * d_model + h * LANES:2 * d_model + (h + 1) * LANES] = dbg.astype(BF16)
                vec_ref[0:1, cols] += _colsum(du)
                vec_ref[2:3, cols] += _colsum(dbg)
                dbs_ref[:, h:h + 1] += jnp.sum(dmix, axis=1, keepdims=True)
                dws_ref[h] += jnp.where(tril, _dot_tb(dmix_bf, vn), 0.0)
                dv_buf[rows, cols] = _dot_ta(w_h, dmix_bf)
        dvn = dv_buf[...]
        vec_ref[3:4, :] += _colsum(dvn * vhat)
        vec_ref[4:5, :] += _colsum(dvn)
        dvh = dvn * lg_ref[...]
        dv = rstd * (dvh - jnp.mean(dvh, axis=-1, keepdims=True)
                     - vhat * jnp.mean(dvh * vhat, axis=-1, keepdims=True)) * dgelu_v
        vec_ref[1:2, :] += _colsum(dv)
        dproj_ref[:, d_model:2 * d_model] = dv.astype(BF16)

    blk = lambda j: pl.BlockSpec((tb, d_model), lambda i: (i, j))
    row = pl.BlockSpec((1, d_model), lambda i: (0, 0))
    return pl.pallas_call(
        body, name="branch_b_bwd", grid=(n_tok // tb,),
        in_specs=[ANY, blk(3), blk(4), blk(5), pl.BlockSpec((tb, d_model), lambda i: (i, 0)), row, row,
                  pl.BlockSpec((n_heads, LANES, LANES), lambda i: (0, 0, 0)),
                  pl.BlockSpec((LANES, n_heads), lambda i: (0, 0))],
        out_specs=[pl.BlockSpec((tb, 3 * d_model), lambda i: (i, 1)),
                   pl.BlockSpec((B_ROWS, d_model), lambda i: (0, 0)),
                   pl.BlockSpec((n_heads, LANES, LANES), lambda i: (0, 0, 0)),
                   pl.BlockSpec((LANES, n_heads), lambda i: (0, 0))],
        out_shape=[SDS(dproj.shape, BF16), SDS((B_ROWS, d_model), F32),
                   SDS((n_heads, LANES, LANES), F32), SDS((LANES, n_heads), F32)],
        scratch_shapes=[pltpu.VMEM((tb, d_model), BF16), pltpu.VMEM((tb, d_model), F32)],
        input_output_aliases={0: 0},
        compiler_params=_params(1),
    )(dproj, proj, proj, proj, d_s, ln_g, ln_b, w_spatial, b_spatial_t)


A1_ROWS = 8


def _branch_a_bwd_norm(dproj, proj, h1, d_h3, gn_g, gn_b, seq):
    n_tok = proj.shape[0]
    d_model = gn_g.shape[1]
    ta = _tile(seq, ELEMENTWISE_ROWS, 16)

    def body(dproj_in, gt_ref, h1_ref, dh3_ref, gg_ref, gb_ref, dproj_ref, dh1_ref, vec_ref):
        del dproj_in

        @pl.when(pl.program_id(0) == 0)
        def _():
            vec_ref[...] = jnp.zeros_like(vec_ref)

        def group(g, carry):
            sl = pl.ds(pl.multiple_of(g * LANES, LANES), LANES)
            h1 = h1_ref[:, sl]
            mu = jnp.mean(h1, axis=-1, keepdims=True)
            dlt = h1 - mu
            rstd = lax.rsqrt(jnp.mean(dlt * dlt, axis=-1, keepdims=True) + LN_EPS)
            nrm = dlt * rstd
            sw, dsw = _silu_and_grad(nrm * gg_ref[:, sl] + gb_ref[:, sl])
            sg, dsg = _silu_and_grad(gt_ref[:, sl])
            dh3 = dh3_ref[:, sl]
            dgate = dh3 * sw * dsg
            dproj_ref[:, sl] = dgate.astype(BF16)
            vec_ref[0:1, sl] += _colsum(dgate)
            dh2 = dh3 * sg * dsw
            vec_ref[1:2, sl] += _colsum(dh2 * nrm)
            vec_ref[2:3, sl] += _colsum(dh2)
            dn = dh2 * gg_ref[:, sl]
            dh1 = rstd * (dn - jnp.mean(dn, axis=-1, keepdims=True)
                          - nrm * jnp.mean(dn * nrm, axis=-1, keepdims=True))
            vec_ref[3:4, sl] += _colsum(dh1)
            dh1_ref[:, sl] = dh1
            return carry

        lax.fori_loop(0, d_model // LANES, group, 0, unroll=True)

    tile = pl.BlockSpec((ta, d_model), lambda i: (i, 0))
    row = pl.BlockSpec((1, d_model), lambda i: (0, 0))
    return pl.pallas_call(
        body, name="branch_a_bwd_norm", grid=(n_tok // ta,),
        in_specs=[ANY, pl.BlockSpec((ta, d_model), lambda i: (i, 2)), tile, tile, row, row],
        out_specs=[pl.BlockSpec((ta, d_model), lambda i: (i, 2)), tile,
                   pl.BlockSpec((A1_ROWS, d_model), lambda i: (0, 0))],
        out_shape=[SDS(dproj.shape, BF16), SDS((n_tok, d_model), F32), SDS((A1_ROWS, d_model), F32)],
        input_output_aliases={0: 0},
        compiler_params=_params(1),
    )(dproj, proj, h1, d_h3, gn_g, gn_b)


A2_ROWS = 8


def _branch_a_bwd_conv(dproj, proj, d_h1, conv_w_full, gp_bf, gp_f32, seq):
    n_tok = proj.shape[0]
    d_model = conv_w_full.shape[1]
    n_groups = d_model // LANES
    ta = _tile(seq, ELEMENTWISE_ROWS, HALO)
    n_tiles = n_tok // ta
    per_seq = seq // ta
    rc = _conv_rows(ta)
    last_halo = n_tok // HALO - 1
    r8 = d_model // N_DEV
    prow = _tile(r8, 32, 16)

    def body(dproj_in, av_ref, ag_ref, avh_ref, agh_ref, dh1_ref, dh1h_ref, cw_ref, gp_bf_ref, gp_f32_ref,
             dproj_ref, vec_ref, dcw_ref, opa_ref, opb_ref, opo_ref,
             ext_h0, ext_d, rbuf, own, send_sems, recv_sems, local_sems):
        del dproj_in
        i = pl.program_id(0)
        x, y, c = _mesh_pos()
        me = _block_of((x, y, c))
        peers = _peers()

        def sends():
            return [pltpu.make_async_remote_copy(
                src_ref=gp_bf_ref.at[a, pl.ds(pl.multiple_of(blk * r8, 16), r8), :], dst_ref=rbuf.at[k, a],
                send_sem=send_sems.at[a, k], recv_sem=recv_sems.at[a, k], device_id=pos, device_id_type=MESH)
                for k, (pos, blk) in enumerate(peers) for a in range(3)]

        def own_rows():
            return [pltpu.make_async_copy(gp_f32_ref.at[a, pl.ds(pl.multiple_of(me * r8, 8), r8), :],
                                          own.at[a], local_sems.at[a]) for a in range(3)]

        @pl.when(i == 0)
        def _():
            vec_ref[...] = jnp.zeros_like(vec_ref)
            dcw_ref[...] = jnp.zeros_like(dcw_ref)
            for cp in sends() + own_rows():
                cp.start()

        keep_past = jnp.where(i % per_seq == 0, 0.0, 1.0)
        keep_next = jnp.where(i % per_seq == per_seq - 1, 0.0, 1.0)

        def group(g, carry):
            sl = pl.ds(pl.multiple_of(g * LANES, LANES), LANES)
            av = av_ref[:, sl]
            sig = _sigmoid(ag_ref[:, sl])
            ext_h0[0:HALO, :] = avh_ref[:, sl] * _sigmoid(agh_ref[:, sl]) * keep_past
            ext_h0[HALO:HALO + ta, :] = av * sig
            ext_d[0:ta, :] = dh1_ref[:, sl]
            ext_d[ta:ta + HALO, :] = dh1h_ref[:, sl] * keep_next
            for r0 in range(0, ta, rc):
                dh1 = ext_d[pl.ds(r0, rc), :]
                acc = jnp.zeros((rc, LANES), F32)
                for k in range(CONV_K):
                    acc = acc + ext_d[pl.ds(r0 + CONV_K - 1 - k, rc), :] * cw_ref[k:k + 1, sl]
                    prod = dh1 * ext_h0[pl.ds(r0 + HALO - (CONV_K - 1) + k, rc), :]
                    dcw_ref[g, k] += jnp.sum(prod.reshape(rc // SUBLANES, SUBLANES, LANES), axis=0)
                rows = pl.ds(r0, rc)
                sig_r = sig[r0:r0 + rc]
                dav = acc * sig_r
                dag = dav * av[r0:r0 + rc] * (1.0 - sig_r)
                dproj_ref[rows, sl] = dav.astype(BF16)
                dproj_ref[rows, pl.ds(pl.multiple_of(d_model + g * LANES, LANES), LANES)] = dag.astype(BF16)
                vec_ref[0:1, sl] += _colsum(dav)
                vec_ref[1:2, sl] += _colsum(dag)
            return carry

        lax.fori_loop(0, n_groups, group, 0, unroll=True)

        @pl.when(i == n_tiles - 1)
        def _():
            for cp in own_rows():
                cp.wait()
            for cp in sends():
                cp.wait_recv()
            for a, o in enumerate([opa_ref, opb_ref, opo_ref]):
                for q in range(r8 // prow):
                    r = pl.ds(q * prow, prow)
                    tot = own[a, r, :]
                    for k in range(N_DEV - 1):
                        tot = tot + rbuf[k, a, r, :].astype(F32)
                    o[r, :] = tot
            for cp in sends():
                cp.wait_send()

    blk = lambda j: pl.BlockSpec((ta, d_model), lambda i: (i, j))
    halo = lambda j: pl.BlockSpec((HALO, d_model), lambda i: (jnp.maximum(i * (ta // HALO) - 1, 0), j))
    shard = pl.BlockSpec((r8, d_model), lambda i: (0, 0))
    return pl.pallas_call(
        body, name="branch_a_bwd_conv", grid=(n_tiles,),
        in_specs=[ANY, blk(0), blk(1), halo(0), halo(1), pl.BlockSpec((ta, d_model), lambda i: (i, 0)),
                  pl.BlockSpec((HALO, d_model), lambda i: (jnp.minimum((i + 1) * (ta // HALO), last_halo), 0)),
                  pl.BlockSpec((HALO, d_model), lambda i: (0, 0)), ANY, ANY],
        out_specs=[pl.BlockSpec((ta, 2 * d_model), lambda i: (i, 0)),
                   pl.BlockSpec((A2_ROWS, d_model), lambda i: (0, 0)),
                   pl.BlockSpec((n_groups, HALO, SUBLANES, LANES), lambda i: (0, 0, 0, 0)),
                   shard, shard, shard],
        out_shape=[SDS(dproj.shape, BF16), SDS((A2_ROWS, d_model), F32),
                   SDS((n_groups, HALO, SUBLANES, LANES), F32)] + [SDS((r8, d_model), F32)] * 3,
        scratch_shapes=[pltpu.VMEM((HALO + ta, LANES), F32), pltpu.VMEM((ta + HALO, LANES), F32),
                        pltpu.VMEM((N_DEV - 1, 3, r8, d_model), BF16), pltpu.VMEM((3, r8, d_model), F32),
                        pltpu.SemaphoreType.DMA((3, 7)), pltpu.SemaphoreType.DMA((3, 7)),
                        pltpu.SemaphoreType.DMA((3,))],
        input_output_aliases={0: 0},
        compiler_params=_params(1),
    )(dproj, proj, proj, proj, proj, d_h1, d_h1, conv_w_full, gp_bf, gp_f32)


def _weight_grads(lhs3, rhs3):
    n_mat, n_tok, d_model = lhs3.shape
    tk = _tile(n_tok, 2048, 16)
    n_k = n_tok // tk

    def body(a_ref, g_ref, o_ref, ob_ref):
        part = _dot_ta(a_ref[...], g_ref[...])

        @pl.when(pl.program_id(1) == 0)
        def _():
            o_ref[...] = part

        @pl.when(pl.program_id(1) != 0)
        def _():
            o_ref[...] += part

        @pl.when(pl.program_id(1) == n_k - 1)
        def _():
            ob_ref[...] = o_ref[...].astype(BF16)

    tile = pl.BlockSpec((None, tk, d_model), lambda a, i: (a, i, 0))
    out = pl.BlockSpec((None, d_model, d_model), lambda a, i: (a, 0, 0))
    return pl.pallas_call(
        body, name="grad_w_pa_pb_o", grid=(n_mat, n_k), in_specs=[tile, tile], out_specs=[out, out],
        out_shape=[SDS((n_mat, d_model, d_model), F32), SDS((n_mat, d_model, d_model), BF16)],
        compiler_params=_params(2),
    )(lhs3, rhs3)


def _grad_w_in_reduce_scatter(xt_bf, dproj, blk_order, small_part):
    n_tok = dproj.shape[0]
    d_model = xt_bf.shape[0]
    dh = d_model // 2
    n_units = 2 * N_DEV
    rsl = small_part.shape[0] // N_DEV

    def body(ord_ref, a_ref, g_ref, p_ref, o_ref, small_ref, acc, fb, sbuf, gbuf, tbuf, rfin, rbuf_s, red,
             send_f, send_s, recv_g, recv_t, recv_f, out_sems, send1, recv1, send2, recv2):
        del ord_ref
        u = pl.program_id(0)
        s = u // 2
        hf = u % 2
        rnd = s // 2
        x, y, c = _mesh_pos()
        sibling = (x, y, 1 - c)
        me = _block_of((x, y, c))
        peers = _peers()

        def rows_of(blk):
            return pl.ds(pl.multiple_of(blk * rsl, SUBLANES), rsl)

        def scatter():
            return [pltpu.make_async_remote_copy(
                src_ref=p_ref.at[rows_of(blk), :], dst_ref=rbuf_s.at[k], send_sem=send1.at[k],
                recv_sem=recv1.at[k], device_id=pos, device_id_type=MESH) for k, (pos, blk) in enumerate(peers)]

        def gather(dst_block=None):
            return [pltpu.make_async_remote_copy(
                src_ref=red, dst_ref=small_ref.at[rows_of(me if dst_block is None else blk), :],
                send_sem=send2.at[k], recv_sem=recv2.at[k], device_id=pos, device_id_type=MESH)
                for k, (pos, blk) in enumerate(peers)]

        def own_slice():
            return pltpu.make_async_copy(red, small_ref.at[rows_of(me), :], out_sems.at[2])

        @pl.when(u == 0)
        def _():
            for cp in scatter():
                cp.start()

        @pl.when(u == 8)
        def _():
            for cp in scatter():
                cp.wait_recv()
            tot = p_ref[rows_of(me), :]
            for k in range(N_DEV - 1):
                tot = tot + rbuf_s[k]
            red[...] = tot
            own_slice().start()
            for cp in gather():
                cp.start()

        n1 = (jnp.bitwise_xor(x, c), jnp.bitwise_xor(y, 1 - c), c)
        n2 = (jnp.bitwise_xor(x, 1 - c), jnp.bitwise_xor(y, c), c)

        def feed(r, half):
            return pltpu.make_async_remote_copy(
                src_ref=fb.at[half], dst_ref=gbuf.at[r, half], send_sem=send_f.at[r, half],
                recv_sem=recv_g.at[r, half], device_id=sibling, device_id_type=MESH)

        def feed_sibling(half):
            return pltpu.make_async_remote_copy(
                src_ref=fb.at[half], dst_ref=rfin.at[0, half], send_sem=send_f.at[3, half],
                recv_sem=recv_f.at[0, half], device_id=sibling, device_id_type=MESH)

        def chip_sum(r, half):
            dst = [tbuf.at[half], rfin.at[2, half], rfin.at[1, half]][r]
            sem = [recv_t.at[half], recv_f.at[2, half], recv_f.at[1, half]][r]
            return pltpu.make_async_remote_copy(
                src_ref=sbuf.at[r, half], dst_ref=dst, send_sem=send_s.at[r, half], recv_sem=sem,
                device_id=[n2, n2, n1][r], device_id_type=MESH)

        def out_copy(half):
            return pltpu.make_async_copy(acc.at[half], o_ref.at[:, pl.ds(half * dh, dh)], out_sems.at[half])

        def partial_sum():
            return _dot(a_ref[:, 0:n_tok], g_ref[...])

        for half in range(2):
            for r in range(3):
                @pl.when(u == 4 * r + 4 + half)
                def _(r=r, half=half):
                    feed(r, half).wait_send()

                @pl.when(u == 4 * r + 2 + half)
                def _(r=r, half=half):
                    feed(r, half).wait_recv()
                    if r == 2:
                        chip_sum(0, half).wait_recv()

            @pl.when(u == 14 + half)
            def _(half=half):
                feed_sibling(half).wait_recv()
                chip_sum(2, half).wait_recv()
                chip_sum(1, half).wait_recv()

        @pl.when(jnp.logical_and(s % 2 == 0, s < 7))
        def _():
            fb[hf] = partial_sum().astype(BF16)

        @pl.when(jnp.logical_or(s == 1, s == 3))
        def _():
            sbuf[rnd, hf] = (partial_sum() + gbuf[rnd, hf].astype(F32)).astype(BF16)

        @pl.when(s == 5)
        def _():
            sbuf[2, hf] = (partial_sum() + gbuf[2, hf].astype(F32) + tbuf[hf].astype(F32)).astype(BF16)

        @pl.when(s == 7)
        def _():
            acc[hf] = (partial_sum() + rfin[0, hf].astype(F32) + rfin[1, hf].astype(F32)
                       + rfin[2, hf].astype(F32))

        for half in range(2):
            for r in range(3):
                @pl.when(u == 4 * r + half)
                def _(r=r, half=half):
                    feed(r, half).start()

                @pl.when(u == 4 * r + 2 + half)
                def _(r=r, half=half):
                    chip_sum(r, half).start()

            @pl.when(u == 12 + half)
            def _(half=half):
                feed_sibling(half).start()

        @pl.when(u == 14)
        def _():
            out_copy(0).start()

        @pl.when(u == 15)
        def _():
            out_copy(1).start()
            for half in range(2):
                feed_sibling(half).wait_send()
                for r in range(3):
                    chip_sum(r, half).wait_send()
                out_copy(half).wait()
            for cp in gather("theirs"):
                cp.wait_recv()
            for cp in scatter() + gather():
                cp.wait_send()
            own_slice().wait()

    grid_spec = pltpu.PrefetchScalarGridSpec(
        num_scalar_prefetch=1, grid=(n_units,),
        in_specs=[VMEM, pl.BlockSpec((n_tok, dh), lambda u, o: (0, 2 * o[u // 2] + u % 2)), VMEM],
        out_specs=[ANY, ANY],
        scratch_shapes=[pltpu.VMEM((2, d_model, dh), F32), pltpu.VMEM((2, d_model, dh), BF16),
                        pltpu.VMEM((3, 2, d_model, dh), BF16), pltpu.VMEM((3, 2, d_model, dh), BF16),
                        pltpu.VMEM((2, d_model, dh), BF16), pltpu.VMEM((3, 2, d_model, dh), BF16),
                        pltpu.VMEM((N_DEV - 1, rsl, LANES), F32), pltpu.VMEM((rsl, LANES), F32),
                        pltpu.SemaphoreType.DMA((4, 2)), pltpu.SemaphoreType.DMA((3, 2)),
                        pltpu.SemaphoreType.DMA((3, 2)), pltpu.SemaphoreType.DMA((2,)),
                        pltpu.SemaphoreType.DMA((3, 2)),
                        pltpu.SemaphoreType.DMA((3,))] + [pltpu.SemaphoreType.DMA((N_DEV - 1,))] * 4)
    return pl.pallas_call(
        body, name="grad_w_in_reduce_scatter", grid_spec=grid_spec,
        out_shape=[SDS((d_model, d_model), F32), SDS(small_part.shape, F32)], compiler_params=_params(1),
    )(blk_order, xt_bf, dproj, small_part)


def _grad_x_adamw(dproj, w_all, dr, w, g, m, v, groups):
    n_tok, d_model = dr.shape
    tm = _tile(n_tok, 256, 16)
    n_steps = n_tok // tm
    tr = w.shape[0] // n_steps
    n_grp = len(groups)

    def body(dp_ref, w_hbm, dr_ref, ws_ref, g_ref, m_ref, v_ref, *rest):
        grp_in, rest = rest[:4 * n_grp], rest[4 * n_grp:]
        o_ref, d_ref, mo_ref, vo_ref = rest[:4]
        grp_out, (w_ref, w_sems) = rest[4:4 + 3 * n_grp], rest[4 + 3 * n_grp:]
        i = pl.program_id(0)

        def fetch(j):
            return pltpu.make_async_copy(w_hbm.at[j], w_ref.at[j], w_sems.at[j])

        def grad_x_tile(first):
            acc = DEEPNORM_ALPHA * dr_ref[...]
            for j in range(N_DEV):
                if first:
                    fetch(j).wait()
                acc = acc + _dot_tb(dp_ref[:, j * d_model:(j + 1) * d_model], w_ref[j])
            o_ref[...] = acc

        @pl.when(i == 0)
        def _():
            for j in range(N_DEV):
                fetch(j).start()
            grad_x_tile(True)

        @pl.when(i != 0)
        def _():
            grad_x_tile(False)

        d_ref[...], mo_ref[...], vo_ref[...] = _adamw_math(ws_ref[...], g_ref[...], m_ref[...], v_ref[...])

        @pl.when(i == n_steps - 1)
        def _():
            for p in range(n_grp):
                pw, pg, pm, pv = grp_in[4 * p:4 * p + 4]
                grp_out[3 * p][...], grp_out[3 * p + 1][...], grp_out[3 * p + 2][...] = _adamw_math(
                    pw[...], pg[...], pm[...], pv[...])

    tile = pl.BlockSpec((tm, d_model), lambda i: (i, 0))
    slab = pl.BlockSpec((tr, w.shape[1]), lambda i: (i, 0))
    flat = [a for grp in groups for a in grp]
    res = pl.pallas_call(
        body, name="grad_x_adamw", grid=(n_steps,),
        in_specs=[pl.BlockSpec((tm, N_DEV * d_model), lambda i: (i, 0)), ANY, tile, slab, slab, slab, slab]
        + [VMEM] * (4 * n_grp),
        out_specs=[tile, slab, slab, slab] + [VMEM] * (3 * n_grp),
        out_shape=[SDS((n_tok, d_model), F32)] + [SDS(w.shape, F32)] * 3
        + [SDS(grp[0].shape, F32) for grp in groups for _ in range(3)],
        scratch_shapes=[pltpu.VMEM(w_all.shape, BF16), pltpu.SemaphoreType.DMA((N_DEV,))],
        compiler_params=_params(1),
    )(dproj, w_all, dr, w, g, m, v, *flat)
    return res[0], tuple(res[1:4]), [tuple(res[4 + 3 * p:7 + 3 * p]) for p in range(n_grp)]


def _adamw_math(w, g, m, v):
    m = ADAM_B1 * m + (1.0 - ADAM_B1) * g
    v = ADAM_B2 * v + (1.0 - ADAM_B2) * (g * g)
    m_hat = m / (1.0 - ADAM_B1 ** ADAM_STEP)
    v_hat = v / (1.0 - ADAM_B2 ** ADAM_STEP)
    delta = -ADAM_LR * (m_hat / (jnp.sqrt(v_hat) + ADAM_EPS) + ADAM_WD * w)
    return delta, m, v


def _as_rows(a):
    return a.reshape(-1, LANES)


def kernel(x, w_in, b_in, conv_w, conv_b, gn_g, gn_b, ln_v_g, ln_v_b, w_spatial, b_spatial, w_pa, w_pb, w_o, b_o, ln_out_g, ln_out_b, loss_target, m_w_in, m_b_in, m_conv_w, m_conv_b, m_gn_g, m_gn_b, m_ln_v_g, m_ln_v_b, m_w_spatial, m_b_spatial, m_w_pa, m_w_pb, m_w_o, m_b_o, m_ln_out_g, m_ln_out_b, v_w_in, v_b_in, v_conv_w, v_conv_b, v_gn_g, v_gn_b, v_ln_v_g, v_ln_v_b, v_w_spatial, v_b_spatial, v_w_pa, v_w_pb, v_w_o, v_b_o, v_ln_out_g, v_ln_out_b):
    n_batch, seq, d_model = x.shape
    n_tok = n_batch * seq
    n_heads = d_model // LANES
    dc = conv_w.shape[1]
    me = 4 * lax.axis_index("x") + 2 * lax.axis_index("y") + lax.axis_index("c")
    row = lambda a: a.reshape(1, d_model)

    x2 = x.reshape(n_tok, d_model)
    target2 = loss_target.reshape(n_tok, d_model)
    b_spatial_t = b_spatial.T

    first = jnp.where(lax.axis_index("c") == 1, 4, 2)
    second = 6 - first
    ag_rel = jnp.stack([0 * first, 0 * first + 1, first, second + 1, second, first + 1, 0 * first + 6, 0 * first + 7])
    ag_blocks = jnp.bitwise_xor(me, ag_rel).astype(jnp.int32)
    proj, xt_bf, w_all, wp_all, cw_all = _proj_all_gather(
        x2, w_in, w_pa, w_pb, w_o, conv_w, b_in.reshape(N_DEV, 1, d_model), ag_blocks)
    wp_full = wp_all.reshape(3, d_model, d_model)
    conv_w_full = jnp.pad(cw_all.transpose(1, 0, 2).reshape(CONV_K, d_model), ((0, HALO - CONV_K), (0, 0)))

    h3, h1 = _branch_a_fwd(proj, conv_w_full, row(conv_b), row(gn_g), row(gn_b), seq)
    s = _branch_b_fwd(proj, row(ln_v_g), row(ln_v_b), w_spatial, b_spatial_t, seq)

    dproj, d_h3, d_s, dr, lhs3, rhs3, vec_mid = _mid(
        h3, s, proj, x2, target2, wp_full, row(b_o), row(ln_out_g), row(ln_out_b))

    dproj, vec_b, d_ws, d_bs_t = _branch_b_bwd(dproj, proj, d_s, row(ln_v_g), row(ln_v_b), w_spatial, b_spatial_t, seq)
    dproj, d_h1, vec_a1 = _branch_a_bwd_norm(dproj, proj, h1, d_h3, row(gn_g), row(gn_b), seq)
    gp_f32, gp_bf = _weight_grads(lhs3, rhs3)
    dproj, vec_a2, d_cw8, g_w_pa, g_w_pb, g_w_o = _branch_a_bwd_conv(
        dproj, proj, d_h1, conv_w_full, gp_bf, gp_f32, seq)

    d_cw = jnp.sum(d_cw8, axis=2)
    pieces = [
        _as_rows(jnp.concatenate([vec_a2[0:2], vec_a1[0:1], vec_b[0:3], vec_mid[3:5]], axis=0)),
        _as_rows(jnp.concatenate([vec_a1[3:4], vec_a1[1:3], vec_b[3:5], vec_mid[2:3], vec_mid[0:2]], axis=0)),
        _as_rows(d_bs_t.T), _as_rows(d_ws), _as_rows(d_cw), _as_rows(vec_mid[5:6]),
    ]
    n_rows = sum(p.shape[0] for p in pieces)
    pad_rows = -n_rows % (N_DEV * SUBLANES)
    small_part = jnp.concatenate(pieces + [jnp.zeros((pad_rows, LANES), F32)], axis=0)

    rs_rel = jnp.stack([0 * first + 7, 0 * first + 6, first + 1, second, second + 1, first, 0 * first + 1, 0 * first])
    rs_blocks = jnp.bitwise_xor(me, rs_rel).astype(jnp.int32)
    g_w_in, small = _grad_w_in_reduce_scatter(xt_bf, dproj, rs_blocks, small_part)

    g_rows = d_model // LANES
    o0 = N_DEV * g_rows
    g_b_in = small[0:o0].reshape(N_DEV * d_model)
    vecs = [small[o0 + a * g_rows:o0 + (a + 1) * g_rows].reshape(d_model) for a in range(8)]
    g_conv_b, g_gn_g, g_gn_b, g_ln_v_g, g_ln_v_b, g_b_o, g_ln_out_g, g_ln_out_b = vecs
    o1 = o0 + 8 * g_rows
    g_b_spatial = small[o1:o1 + n_heads].reshape(n_heads, LANES)
    o2 = o1 + n_heads
    g_w_spatial = small[o2:o2 + n_heads * LANES].reshape(n_heads, LANES, LANES)
    o3 = o2 + n_heads * LANES
    g_cw_full = small[o3:o3 + n_heads * HALO].reshape(n_heads, HALO, LANES).transpose(1, 0, 2).reshape(HALO, d_model)
    g_conv_w = lax.dynamic_slice(g_cw_full, (0, me * dc), (CONV_K, dc))
    o4 = o3 + n_heads * HALO
    loss = jnp.sum(small[o4:o4 + g_rows]) * (0.5 / d_model)

    two_d = lambda a: a.reshape(-1, a.shape[-1]) if a.ndim != 1 else (
        a.reshape(-1, LANES) if a.shape[0] % LANES == 0 else a.reshape(1, -1))
    names = ["b_in", "conv_w", "conv_b", "gn_g", "gn_b", "ln_v_g", "ln_v_b", "w_spatial", "b_spatial",
             "w_pa", "w_pb", "w_o", "b_o", "ln_out_g", "ln_out_b"]
    ws = dict(b_in=b_in, conv_w=conv_w, conv_b=conv_b, gn_g=gn_g, gn_b=gn_b, ln_v_g=ln_v_g, ln_v_b=ln_v_b,
              w_spatial=w_spatial, b_spatial=b_spatial, w_pa=w_pa, w_pb=w_pb, w_o=w_o, b_o=b_o,
              ln_out_g=ln_out_g, ln_out_b=ln_out_b)
    gs = dict(b_in=g_b_in, conv_w=g_conv_w, conv_b=g_conv_b, gn_g=g_gn_g, gn_b=g_gn_b, ln_v_g=g_ln_v_g,
              ln_v_b=g_ln_v_b, w_spatial=g_w_spatial, b_spatial=g_b_spatial, w_pa=g_w_pa, w_pb=g_w_pb,
              w_o=g_w_o, b_o=g_b_o, ln_out_g=g_ln_out_g, ln_out_b=g_ln_out_b)
    ms = dict(b_in=m_b_in, conv_w=m_conv_w, conv_b=m_conv_b, gn_g=m_gn_g, gn_b=m_gn_b, ln_v_g=m_ln_v_g,
              ln_v_b=m_ln_v_b, w_spatial=m_w_spatial, b_spatial=m_b_spatial, w_pa=m_w_pa, w_pb=m_w_pb,
              w_o=m_w_o, b_o=m_b_o, ln_out_g=m_ln_out_g, ln_out_b=m_ln_out_b)
    vs = dict(b_in=v_b_in, conv_w=v_conv_w, conv_b=v_conv_b, gn_g=v_gn_g, gn_b=v_gn_b, ln_v_g=v_ln_v_g,
              ln_v_b=v_ln_v_b, w_spatial=v_w_spatial, b_spatial=v_b_spatial, w_pa=v_w_pa, w_pb=v_w_pb,
              w_o=v_w_o, b_o=v_b_o, ln_out_g=v_ln_out_g, ln_out_b=v_ln_out_b)
    grad_x, (d_w_in, nm_w_in, nv_w_in), upd = _grad_x_adamw(
        dproj, w_all, dr, w_in, g_w_in, m_w_in, v_w_in,
        [tuple(two_d(d[n]) for d in (ws, gs, ms, vs)) for n in names])
    grad_x = grad_x.reshape(x.shape)
    delta = {n: u[0].reshape(ws[n].shape) for n, u in zip(names, upd)}
    new_m = {n: u[1].reshape(ws[n].shape) for n, u in zip(names, upd)}
    new_v = {n: u[2].reshape(ws[n].shape) for n, u in zip(names, upd)}
    gs["w_in"], delta["w_in"], new_m["w_in"], new_v["w_in"] = g_w_in, d_w_in, nm_w_in, nv_w_in

    order = ["w_in"] + names
    return (loss, grad_x, *[gs[n] for n in order], *[delta[n] for n in order],
            *[new_m[n] for n in order], *[new_v[n] for n in order])
```

```python
import jax
import jax.numpy as jnp
from jax import lax
from jax.experimental import pallas as pl
from jax.experimental.pallas import tpu as pltpu

F32 = jnp.float32
BF16 = jnp.bfloat16
SDS = jax.ShapeDtypeStruct

N_DEV = 8
LANES = 128
SUBLANES = 8
CONV_K = 31
HALO = 32
ELEMENTWISE_ROWS = 512
LN_EPS = 1e-5
DEEPNORM_ALPHA = 2.0 ** 0.25
ADAM_LR, ADAM_B1, ADAM_B2, ADAM_EPS, ADAM_WD, ADAM_STEP = 0.001, 0.9, 0.999, 1e-08, 0.01, 10
GELU_C = 0.7978845608028654
GELU_A = 0.044715
VMEM_LIMIT = 56 * 1024 * 1024
MESH = pl.DeviceIdType.MESH
ANY = pl.BlockSpec(memory_space=pl.ANY)
VMEM = pl.BlockSpec(memory_space=pltpu.VMEM)


def _params(n_grid=0):
    sem = ("arbitrary",) * n_grid if n_grid else None
    return pltpu.CompilerParams(dimension_semantics=sem, vmem_limit_bytes=VMEM_LIMIT)


def _tile(n, pref, mult):
    t = min(n, pref)
    while n % t or t % mult:
        t -= 1
    return t


def _colsum(v):
    return jnp.sum(v, axis=0, keepdims=True)


def _sigmoid(v):
    return jax.nn.sigmoid(v)


def _silu_and_grad(v):
    s = _sigmoid(v)
    val = v * s
    return val, s + val * (1.0 - s)


def _gelu_and_grad(v):
    v2 = v * v
    sg = _sigmoid(v * (2.0 * GELU_C + (2.0 * GELU_C * GELU_A) * v2))
    grad = sg + v * sg * (1.0 - sg) * (2.0 * GELU_C + (6.0 * GELU_C * GELU_A) * v2)
    return v * sg, grad


def _tril_mask():
    r = lax.broadcasted_iota(jnp.int32, (LANES, LANES), 0)
    c = lax.broadcasted_iota(jnp.int32, (LANES, LANES), 1)
    return c <= r


def _dot(a, b):
    return jnp.dot(a, b, preferred_element_type=F32)


def _dot_tb(a, b):
    return lax.dot_general(a, b, (((1,), (1,)), ((), ())), preferred_element_type=F32)


def _dot_ta(a, b):
    return lax.dot_general(a, b, (((0,), (0,)), ((), ())), preferred_element_type=F32)


def _mesh_pos():
    return lax.axis_index("x"), lax.axis_index("y"), lax.axis_index("c")


def _block_of(pos):
    return 4 * pos[0] + 2 * pos[1] + pos[2]


def _peers():
    x, y, c = _mesh_pos()
    out = []
    for k in range(1, N_DEV):
        pos = (1 - x if k & 4 else x, 1 - y if k & 2 else y, 1 - c if k & 1 else c)
        out.append((pos, _block_of(pos)))
    return out


def _proj_all_gather(x2, w_in, w_pa, w_pb, w_o, conv_w, b_in3, blk_order):
    n_tok, d_model = x2.shape
    r8 = w_pa.shape[0]
    kc, dc = conv_w.shape
    tm = _tile(n_tok, 1024, LANES)
    n_t = n_tok // tm
    n_arr = 3

    def body(ord_ref, x_ref, b_ref, win_ref, wpa_ref, wpb_ref, wo_ref, cw_ref,
             proj_ref, xt_ref, wall_ref, wp_ref, cwall_ref,
             wbuf, xbuf, st_p, send_sems, recv_sems, local_sems, wall_sems):
        s = pl.program_id(0)
        t = pl.program_id(1)
        x, y, c = _mesh_pos()
        me = (x, y, c)
        sibling = (x, y, 1 - c)
        n1 = (jnp.bitwise_xor(x, c), jnp.bitwise_xor(y, 1 - c))
        n2 = (jnp.bitwise_xor(x, 1 - c), jnp.bitwise_xor(y, c))
        dg = (1 - x, 1 - y)
        outs = [wbuf, wp_ref, cwall_ref]
        srcs = [None, st_p, cw_ref]
        consumed = [me, sibling, (*n1, c), (*n2, 1 - c), (*n2, c), (*n1, 1 - c), (*dg, c), (*dg, 1 - c)]
        leaves = [(me, sibling), (me, (*n1, c)), (me, (*n2, c)), ((*n1, c), (*n2, c)),
                  ((*n1, c), sibling), ((*n2, c), sibling), ((*dg, c), sibling)]
        lands = [sibling, (*n1, c), (*n2, c), (*dg, c), (*n2, 1 - c), (*n1, 1 - c), (*dg, 1 - c)]

        def slot(o, pos):
            return o.at[:, _block_of(pos)] if o is wp_ref else o.at[_block_of(pos)]

        def copy(a, k, block, to, src=None):
            o = outs[a]
            return pltpu.make_async_remote_copy(
                src_ref=slot(o, block) if src is None else src, dst_ref=slot(o, block),
                send_sem=send_sems.at[a, k], recv_sem=recv_sems.at[a, k],
                device_id=to, device_id_type=MESH)

        def send(a, k):
            block, to = leaves[k]
            return copy(a, k, block, to, src=srcs[a] if k < 3 else None)

        def recv(a, k):
            return copy(a, k, lands[k], me)

        def local_copies():
            return [pltpu.make_async_copy(srcs[a], slot(outs[a], me), local_sems.at[a]) for a in (1, 2)]

        def to_hbm(step):
            return pltpu.make_async_copy(slot(wbuf, consumed[step]), slot(wall_ref, consumed[step]),
                                         wall_sems.at[step])

        def at_step(step):
            return pl.when(jnp.logical_and(s == step, t == 0))

        @at_step(0)
        def _():
            slot(wbuf, me)[...] = win_ref[...].astype(BF16)
            st_p[0] = wpa_ref[...].astype(BF16)
            st_p[1] = wpb_ref[...].astype(BF16)
            st_p[2] = wo_ref[...].astype(BF16)
            for a in range(n_arr):
                send(a, 0).start()
                send(a, 1).start()
            for cp in local_copies():
                cp.start()
            to_hbm(0).start()

        @at_step(1)
        def _():
            recv(0, 0).wait_recv()
            to_hbm(1).start()

        for rnd in range(3):
            @at_step(2 + 2 * rnd)
            def _(rnd=rnd):
                if rnd == 0:
                    for a in range(n_arr):
                        send(a, 2).start()
                recv(0, 1 + rnd).wait_recv()
                if rnd == 0:
                    send(0, 3).start()
                send(0, 4 + rnd).start()
                to_hbm(2 + 2 * rnd).start()

            @at_step(3 + 2 * rnd)
            def _(rnd=rnd):
                for a in (1, 2):
                    recv(a, 1 + rnd).wait_recv()
                    if rnd == 0:
                        send(a, 3).start()
                    send(a, 4 + rnd).start()
                recv(0, 4 + rnd).wait_recv()
                to_hbm(3 + 2 * rnd).start()

        rows = pl.ds(pl.multiple_of(t * tm, tm), tm)

        @pl.when(s == 0)
        def _():
            xb = x_ref[...].astype(BF16)
            xbuf[rows, :] = xb
            xt_ref[...] = xb.T

        proj_ref[...] = _dot(xbuf[rows, :], wbuf[ord_ref[s]]) + b_ref[...]

        @pl.when(jnp.logical_and(s == N_DEV - 1, t == n_t - 1))
        def _():
            for a in (1, 2):
                for k in (0, 4, 5, 6):
                    recv(a, k).wait_recv()
            for a in range(n_arr):
                for k in range(7):
                    send(a, k).wait_send()
            for cp in local_copies() + [to_hbm(step) for step in range(N_DEV)]:
                cp.wait()

    grid_spec = pltpu.PrefetchScalarGridSpec(
        num_scalar_prefetch=1, grid=(N_DEV, n_t),
        in_specs=[pl.BlockSpec((tm, d_model), lambda s, t, o: (jnp.where(s == 0, t, n_t - 1), 0)),
                  pl.BlockSpec((None, 1, d_model), lambda s, t, o: (o[s], 0, 0)),
                  VMEM, VMEM, VMEM, VMEM, VMEM],
        out_specs=[pl.BlockSpec((tm, d_model), lambda s, t, o: (t, o[s])),
                   pl.BlockSpec((d_model, tm), lambda s, t, o: (0, jnp.where(s == 0, t, n_t))),
                   ANY, ANY, ANY],
        scratch_shapes=[pltpu.VMEM((N_DEV, d_model, d_model), BF16), pltpu.VMEM((n_tok, d_model), BF16),
                        pltpu.VMEM((3, r8, d_model), BF16),
                        pltpu.SemaphoreType.DMA((n_arr, 7)), pltpu.SemaphoreType.DMA((n_arr, 7)),
                        pltpu.SemaphoreType.DMA((3,)), pltpu.SemaphoreType.DMA((N_DEV,))])
    return pl.pallas_call(
        body, name="proj_all_gather", grid_spec=grid_spec,
        out_shape=[SDS((n_tok, N_DEV * d_model), F32), SDS((d_model, (n_t + 1) * tm), BF16),
                   SDS((N_DEV, d_model, d_model), BF16), SDS((3, N_DEV, r8, d_model), BF16),
                   SDS((N_DEV, kc, dc), F32)],
        compiler_params=_params(2),
    )(blk_order, x2, b_in3, w_in, w_pa, w_pb, w_o, conv_w)


def _conv_rows(ta):
    return _tile(ta, 64, SUBLANES)


def _branch_a_fwd(proj, conv_w_full, conv_b, gn_g, gn_b, seq):
    n_tok = proj.shape[0]
    d_model = conv_b.shape[1]
    ta = _tile(seq, ELEMENTWISE_ROWS, HALO)
    per_seq = seq // ta
    rc = _conv_rows(ta)

    def body(av_ref, ag_ref, gt_ref, avh_ref, agh_ref, cw_ref, cb_ref, gg_ref, gb_ref,
             h3_ref, h1_ref, ext):
        keep = jnp.where(pl.program_id(0) % per_seq == 0, 0.0, 1.0)

        def group(g, carry):
            sl = pl.ds(pl.multiple_of(g * LANES, LANES), LANES)
            ext[0:HALO, :] = avh_ref[:, sl] * _sigmoid(agh_ref[:, sl]) * keep
            ext[HALO:HALO + ta, :] = av_ref[:, sl] * _sigmoid(ag_ref[:, sl])
            for r0 in range(0, ta, rc):
                acc = jnp.broadcast_to(cb_ref[:, sl], (rc, LANES))
                for k in range(CONV_K):
                    acc = acc + ext[pl.ds(r0 + HALO - (CONV_K - 1) + k, rc), :] * cw_ref[k:k + 1, sl]
                h1_ref[pl.ds(r0, rc), sl] = acc
            h1 = h1_ref[:, sl]
            mu = jnp.mean(h1, axis=-1, keepdims=True)
            dlt = h1 - mu
            var = jnp.mean(dlt * dlt, axis=-1, keepdims=True)
            h2 = dlt * lax.rsqrt(var + LN_EPS) * gg_ref[:, sl] + gb_ref[:, sl]
            gate = gt_ref[:, sl]
            h3_ref[:, sl] = (h2 * _sigmoid(h2) * gate * _sigmoid(gate)).astype(BF16)
            return carry

        lax.fori_loop(0, d_model // LANES, group, 0, unroll=True)

    blk = lambda j: pl.BlockSpec((ta, d_model), lambda i: (i, j))
    halo = lambda j: pl.BlockSpec((HALO, d_model), lambda i: (jnp.maximum(i * (ta // HALO) - 1, 0), j))
    row = pl.BlockSpec((1, d_model), lambda i: (0, 0))
    return pl.pallas_call(
        body, name="branch_a_fwd", grid=(n_tok // ta,),
        in_specs=[blk(0), blk(1), blk(2), halo(0), halo(1),
                  pl.BlockSpec((HALO, d_model), lambda i: (0, 0)), row, row, row],
        out_specs=[pl.BlockSpec((ta, d_model), lambda i: (i, 0))] * 2,
        out_shape=[SDS((n_tok, d_model), BF16), SDS((n_tok, d_model), F32)],
        scratch_shapes=[pltpu.VMEM((HALO + ta, LANES), F32)],
        compiler_params=_params(1),
    )(proj, proj, proj, proj, proj, conv_w_full, conv_b, gn_g, gn_b)


def _branch_b_fwd(proj, ln_g, ln_b, w_spatial, b_spatial_t, seq):
    n_tok = proj.shape[0]
    d_model = ln_g.shape[1]
    n_heads = d_model // LANES
    tb = _tile(seq, ELEMENTWISE_ROWS, LANES)

    def body(u_ref, v_ref, bg_ref, lg_ref, lb_ref, ws_ref, bs_ref, s_ref, vn_buf):
        v, _ = _gelu_and_grad(v_ref[...])
        mu = jnp.mean(v, axis=-1, keepdims=True)
        dlt = v - mu
        var = jnp.mean(dlt * dlt, axis=-1, keepdims=True)
        vn_buf[...] = (dlt * lax.rsqrt(var + LN_EPS) * lg_ref[...] + lb_ref[...]).astype(BF16)
        tril = _tril_mask()
        for h in range(n_heads):
            cols = slice(h * LANES, (h + 1) * LANES)
            w_h = jnp.where(tril, ws_ref[h], 0.0).astype(BF16)
            bias = bs_ref[:, h:h + 1]
            for ch in range(tb // LANES):
                rows = slice(ch * LANES, (ch + 1) * LANES)
                mix = _dot(w_h, vn_buf[rows, cols]) + bias
                u, _ = _gelu_and_grad(u_ref[rows, cols])
                gate = bg_ref[rows, cols]
                s_ref[rows, cols] = (u * mix * gate * _sigmoid(gate)).astype(BF16)

    blk = lambda j: pl.BlockSpec((tb, d_model), lambda i: (i, j))
    row = pl.BlockSpec((1, d_model), lambda i: (0, 0))
    return pl.pallas_call(
        body, name="branch_b_fwd", grid=(n_tok // tb,),
        in_specs=[blk(3), blk(4), blk(5), row, row,
                  pl.BlockSpec((n_heads, LANES, LANES), lambda i: (0, 0, 0)),
                  pl.BlockSpec((LANES, n_heads), lambda i: (0, 0))],
        out_specs=pl.BlockSpec((tb, d_model), lambda i: (i, 0)),
        out_shape=SDS((n_tok, d_model), BF16),
        scratch_shapes=[pltpu.VMEM((tb, d_model), BF16)],
        compiler_params=_params(1),
    )(proj, proj, proj, ln_g, ln_b, w_spatial, b_spatial_t)


MID_ROWS = 8


def _mid(h3, s, proj, x2, target, wp_full, b_o, lo_g, lo_b):
    n_tok, d_model = x2.shape
    tm = _tile(n_tok, 256, 16)

    def body(h3_ref, s_ref, ma_ref, mb_ref, x_ref, t_ref, wpa_ref, wpb_ref, wo_ref, bo_ref,
             lg_ref, lb_ref, dproj_ref, dh3_ref, ds_ref, dr_ref, lhs3_ref, rhs3_ref, vec_ref):
        @pl.when(pl.program_id(0) == 0)
        def _():
            vec_ref[...] = jnp.zeros_like(vec_ref)

        h3 = h3_ref[...]
        s = s_ref[...]
        ya = _dot(h3, wpa_ref[...])
        yb = _dot(s, wpb_ref[...])
        ga = _sigmoid(ma_ref[...])
        gb = _sigmoid(mb_ref[...])
        mixed = (ga * ya + gb * yb).astype(BF16)
        lhs3_ref[0] = h3
        lhs3_ref[1] = s
        lhs3_ref[2] = mixed
        r = DEEPNORM_ALPHA * x_ref[...] + _dot(mixed, wo_ref[...]) + bo_ref[...]
        mu = jnp.mean(r, axis=-1, keepdims=True)
        dlt = r - mu
        rstd = lax.rsqrt(jnp.mean(dlt * dlt, axis=-1, keepdims=True) + LN_EPS)
        rhat = dlt * rstd
        diff = rhat * lg_ref[...] + lb_ref[...] - t_ref[...]
        dy = diff * (1.0 / d_model)
        vec_ref[0:1, :] += _colsum(dy * rhat)
        vec_ref[1:2, :] += _colsum(dy)
        vec_ref[5:6, :] += _colsum(diff * diff)
        drh = dy * lg_ref[...]
        dr = rstd * (drh - jnp.mean(drh, axis=-1, keepdims=True)
                     - rhat * jnp.mean(drh * rhat, axis=-1, keepdims=True))
        vec_ref[2:3, :] += _colsum(dr)
        dr_ref[...] = dr
        drb = dr.astype(BF16)
        rhs3_ref[2] = drb
        dmixed = _dot_tb(drb, wo_ref[...])
        dya_f = dmixed * ga
        dyb_f = dmixed * gb
        dma = dya_f * ya * (1.0 - ga)
        dmb = dyb_f * yb * (1.0 - gb)
        vec_ref[3:4, :] += _colsum(dma)
        vec_ref[4:5, :] += _colsum(dmb)
        dproj_ref[:, 0:d_model] = dma.astype(BF16)
        dproj_ref[:, d_model:2 * d_model] = dmb.astype(BF16)
        dya = dya_f.astype(BF16)
        dyb = dyb_f.astype(BF16)
        rhs3_ref[0] = dya
        rhs3_ref[1] = dyb
        dh3_ref[...] = _dot_tb(dya, wpa_ref[...])
        ds_ref[...] = _dot_tb(dyb, wpb_ref[...])

    tile = pl.BlockSpec((tm, d_model), lambda i: (i, 0))
    full = lambda a: pl.BlockSpec((None, d_model, d_model), lambda i: (a, 0, 0))
    row = pl.BlockSpec((1, d_model), lambda i: (0, 0))
    stack = pl.BlockSpec((3, tm, d_model), lambda i: (0, i, 0))
    bf3 = SDS((3, n_tok, d_model), BF16)
    f32 = SDS((n_tok, d_model), F32)
    return pl.pallas_call(
        body, name="mid", grid=(n_tok // tm,),
        in_specs=[tile, tile, pl.BlockSpec((tm, d_model), lambda i: (i, 6)),
                  pl.BlockSpec((tm, d_model), lambda i: (i, 7)), tile, tile, full(0), full(1), full(2),
                  row, row, row],
        out_specs=[pl.BlockSpec((tm, 2 * d_model), lambda i: (i, 3)), tile, tile, tile, stack, stack,
                   pl.BlockSpec((MID_ROWS, d_model), lambda i: (0, 0))],
        out_shape=[SDS((n_tok, N_DEV * d_model), BF16), f32, f32, f32, bf3, bf3,
                   SDS((MID_ROWS, d_model), F32)],
        compiler_params=_params(1),
    )(h3, s, proj, proj, x2, target, wp_full, wp_full, wp_full, b_o, lo_g, lo_b)


B_ROWS = 8


def _branch_b_bwd(dproj, proj, d_s, ln_g, ln_b, w_spatial, b_spatial_t, seq):
    n_tok = proj.shape[0]
    d_model = ln_g.shape[1]
    n_heads = d_model // LANES
    tb = _tile(seq, ELEMENTWISE_ROWS, LANES)

    def body(dproj_in, u_ref, v_ref, bg_ref, ds_ref, lg_ref, lb_ref, ws_ref, bs_ref,
             dproj_ref, vec_ref, dws_ref, dbs_ref, vn_buf, dv_buf):
        del dproj_in

        @pl.when(pl.program_id(0) == 0)
        def _():
            vec_ref[...] = jnp.zeros_like(vec_ref)
            dws_ref[...] = jnp.zeros_like(dws_ref)
            dbs_ref[...] = jnp.zeros_like(dbs_ref)

        v, dgelu_v = _gelu_and_grad(v_ref[...])
        mu = jnp.mean(v, axis=-1, keepdims=True)
        dlt = v - mu
        rstd = lax.rsqrt(jnp.mean(dlt * dlt, axis=-1, keepdims=True) + LN_EPS)
        vhat = dlt * rstd
        vn_buf[...] = (vhat * lg_ref[...] + lb_ref[...]).astype(BF16)
        tril = _tril_mask()
        for h in range(n_heads):
            cols = slice(h * LANES, (h + 1) * LANES)
            w_h = jnp.where(tril, ws_ref[h], 0.0).astype(BF16)
            bias = bs_ref[:, h:h + 1]
            for ch in range(tb // LANES):
                rows = slice(ch * LANES, (ch + 1) * LANES)
                vn = vn_buf[rows, cols]
                mix = _dot(w_h, vn) + bias
                u, dgelu_u = _gelu_and_grad(u_ref[rows, cols])
                sg, dsilu = _silu_and_grad(bg_ref[rows, cols])
                dsv = ds_ref[rows, cols]
                du = dsv * mix * sg * dgelu_u
                dbg = dsv * u * mix * dsilu
                dmix = dsv * u * sg
                dmix_bf = dmix.astype(BF16)
                dproj_ref[rows, cols] = du.astype(BF16)
                dproj_ref[rows, 2 * d_model + h * LANES:2 * d_model + (h + 1) * LANES] = dbg.astype(BF16)
                vec_ref[0:1, cols] += _colsum(du)
                vec_ref[2:3, cols] += _colsum(dbg)
                dbs_ref[:, h:h + 1] += jnp.sum(dmix, axis=1, keepdims=True)
                dws_ref[h] += jnp.where(tril, _dot_tb(dmix_bf, vn), 0.0)
                dv_buf[rows, cols] = _dot_ta(w_h, dmix_bf)
        dvn = dv_buf[...]
        vec_ref[3:4, :] += _colsum(dvn * vhat)
        vec_ref[4:5, :] += _colsum(dvn)
        dvh = dvn * lg_ref[...]
        dv = rstd * (dvh - jnp.mean(dvh, axis=-1, keepdims=True)
                     - vhat * jnp.mean(dvh * vhat, axis=-1, keepdims=True)) * dgelu_v
        vec_ref[1:2, :] += _colsum(dv)
        dproj_ref[:, d_model:2 * d_model] = dv.astype(BF16)

    blk = lambda j: pl.BlockSpec((tb, d_model), lambda i: (i, j))
    row = pl.BlockSpec((1, d_model), lambda i: (0, 0))
    return pl.pallas_call(
        body, name="branch_b_bwd", grid=(n_tok // tb,),
        in_specs=[ANY, blk(3), blk(4), blk(5), pl.BlockSpec((tb, d_model), lambda i: (i, 0)), row, row,
                  pl.BlockSpec((n_heads, LANES, LANES), lambda i: (0, 0, 0)),
                  pl.BlockSpec((LANES, n_heads), lambda i: (0, 0))],
        out_specs=[pl.BlockSpec((tb, 3 * d_model), lambda i: (i, 1)),
                   pl.BlockSpec((B_ROWS, d_model), lambda i: (0, 0)),
                   pl.BlockSpec((n_heads, LANES, LANES), lambda i: (0, 0, 0)),
                   pl.BlockSpec((LANES, n_heads), lambda i: (0, 0))],
        out_shape=[SDS(dproj.shape, BF16), SDS((B_ROWS, d_model), F32),
                   SDS((n_heads, LANES, LANES), F32), SDS((LANES, n_heads), F32)],
        scratch_shapes=[pltpu.VMEM((tb, d_model), BF16), pltpu.VMEM((tb, d_model), F32)],
        input_output_aliases={0: 0},
        compiler_params=_params(1),
    )(dproj, proj, proj, proj, d_s, ln_g, ln_b, w_spatial, b_spatial_t)


A1_ROWS = 8


def _branch_a_bwd_norm(dproj, proj, h1, d_h3, gn_g, gn_b, seq):
    n_tok = proj.shape[0]
    d_model = gn_g.shape[1]
    ta = _tile(seq, ELEMENTWISE_ROWS, 16)

    def body(dproj_in, gt_ref, h1_ref, dh3_ref, gg_ref, gb_ref, dproj_ref, dh1_ref, vec_ref):
        del dproj_in

        @pl.when(pl.program_id(0) == 0)
        def _():
            vec_ref[...] = jnp.zeros_like(vec_ref)

        def group(g, carry):
            sl = pl.ds(pl.multiple_of(g * LANES, LANES), LANES)
            h1 = h1_ref[:, sl]
            mu = jnp.mean(h1, axis=-1, keepdims=True)
            dlt = h1 - mu
            rstd = lax.rsqrt(jnp.mean(dlt * dlt, axis=-1, keepdims=True) + LN_EPS)
            nrm = dlt * rstd
            sw, dsw = _silu_and_grad(nrm * gg_ref[:, sl] + gb_ref[:, sl])
            sg, dsg = _silu_and_grad(gt_ref[:, sl])
            dh3 = dh3_ref[:, sl]
            dgate = dh3 * sw * dsg
            dproj_ref[:, sl] = dgate.astype(BF16)
            vec_ref[0:1, sl] += _colsum(dgate)
            dh2 = dh3 * sg * dsw
            vec_ref[1:2, sl] += _colsum(dh2 * nrm)
            vec_ref[2:3, sl] += _colsum(dh2)
            dn = dh2 * gg_ref[:, sl]
            dh1 = rstd * (dn - jnp.mean(dn, axis=-1, keepdims=True)
                          - nrm * jnp.mean(dn * nrm, axis=-1, keepdims=True))
            vec_ref[3:4, sl] += _colsum(dh1)
            dh1_ref[:, sl] = dh1
            return carry

        lax.fori_loop(0, d_model // LANES, group, 0, unroll=True)

    tile = pl.BlockSpec((ta, d_model), lambda i: (i, 0))
    row = pl.BlockSpec((1, d_model), lambda i: (0, 0))
    return pl.pallas_call(
        body, name="branch_a_bwd_norm", grid=(n_tok // ta,),
        in_specs=[ANY, pl.BlockSpec((ta, d_model), lambda i: (i, 2)), tile, tile, row, row],
        out_specs=[pl.BlockSpec((ta, d_model), lambda i: (i, 2)), tile,
                   pl.BlockSpec((A1_ROWS, d_model), lambda i: (0, 0))],
        out_shape=[SDS(dproj.shape, BF16), SDS((n_tok, d_model), F32), SDS((A1_ROWS, d_model), F32)],
        input_output_aliases={0: 0},
        compiler_params=_params(1),
    )(dproj, proj, h1, d_h3, gn_g, gn_b)


A2_ROWS = 8


def _branch_a_bwd_conv(dproj, proj, d_h1, conv_w_full, gp_bf, gp_f32, seq):
    n_tok = proj.shape[0]
    d_model = conv_w_full.shape[1]
    n_groups = d_model // LANES
    ta = _tile(seq, ELEMENTWISE_ROWS, HALO)
    n_tiles = n_tok // ta
    per_seq = seq // ta
    rc = _conv_rows(ta)
    last_halo = n_tok // HALO - 1
    r8 = d_model // N_DEV
    prow = _tile(r8, 32, 16)

    def body(dproj_in, av_ref, ag_ref, avh_ref, agh_ref, dh1_ref, dh1h_ref, cw_ref, gp_bf_ref, gp_f32_ref,
             dproj_ref, vec_ref, dcw_ref, opa_ref, opb_ref, opo_ref,
             ext_h0, ext_d, rbuf, own, send_sems, recv_sems, local_sems):
        del dproj_in
        i = pl.program_id(0)
        x, y, c = _mesh_pos()
        me = _block_of((x, y, c))
        peers = _peers()

        def sends():
            return [pltpu.make_async_remote_copy(
                src_ref=gp_bf_ref.at[a, pl.ds(pl.multiple_of(blk * r8, 16), r8), :], dst_ref=rbuf.at[k, a],
                send_sem=send_sems.at[a, k], recv_sem=recv_sems.at[a, k], device_id=pos, device_id_type=MESH)
                for k, (pos, blk) in enumerate(peers) for a in range(3)]

        def own_rows():
            return [pltpu.make_async_copy(gp_f32_ref.at[a, pl.ds(pl.multiple_of(me * r8, 8), r8), :],
                                          own.at[a], local_sems.at[a]) for a in range(3)]

        @pl.when(i == 0)
        def _():
            vec_ref[...] = jnp.zeros_like(vec_ref)
            dcw_ref[...] = jnp.zeros_like(dcw_ref)
            for cp in sends() + own_rows():
                cp.start()

        keep_past = jnp.where(i % per_seq == 0, 0.0, 1.0)
        keep_next = jnp.where(i % per_seq == per_seq - 1, 0.0, 1.0)

        def group(g, carry):
            sl = pl.ds(pl.multiple_of(g * LANES, LANES), LANES)
            av = av_ref[:, sl]
            sig = _sigmoid(ag_ref[:, sl])
            ext_h0[0:HALO, :] = avh_ref[:, sl] * _sigmoid(agh_ref[:, sl]) * keep_past
            ext_h0[HALO:HALO + ta, :] = av * sig
            ext_d[0:ta, :] = dh1_ref[:, sl]
            ext_d[ta:ta + HALO, :] = dh1h_ref[:, sl] * keep_next
            for r0 in range(0, ta, rc):
                dh1 = ext_d[pl.ds(r0, rc), :]
                acc = jnp.zeros((rc, LANES), F32)
                for k in range(CONV_K):
                    acc = acc + ext_d[pl.ds(r0 + CONV_K - 1 - k, rc), :] * cw_ref[k:k + 1, sl]
                    prod = dh1 * ext_h0[pl.ds(r0 + HALO - (CONV_K - 1) + k, rc), :]
                    dcw_ref[g, k] += jnp.sum(prod.reshape(rc // SUBLANES, SUBLANES, LANES), axis=0)
                rows = pl.ds(r0, rc)
                sig_r = sig[r0:r0 + rc]
                dav = acc * sig_r
                dag = dav * av[r0:r0 + rc] * (1.0 - sig_r)
                dproj_ref[rows, sl] = dav.astype(BF16)
                dproj_ref[rows, pl.ds(pl.multiple_of(d_model + g * LANES, LANES), LANES)] = dag.astype(BF16)
                vec_ref[0:1, sl] += _colsum(dav)
                vec_ref[1:2, sl] += _colsum(dag)
            return carry

        lax.fori_loop(0, n_groups, group, 0, unroll=True)

        @pl.when(i == n_tiles - 1)
        def _():
            for cp in own_rows():
                cp.wait()
            for cp in sends():
                cp.wait_recv()
            for a, o in enumerate([opa_ref, opb_ref, opo_ref]):
                for q in range(r8 // prow):
                    r = pl.ds(q * prow, prow)
                    tot = own[a, r, :]
                    for k in range(N_DEV - 1):
                        tot = tot + rbuf[k, a, r, :].astype(F32)
                    o[r, :] = tot
            for cp in sends():
                cp.wait_send()

    blk = lambda j: pl.BlockSpec((ta, d_model), lambda i: (i, j))
    halo = lambda j: pl.BlockSpec((HALO, d_model), lambda i: (jnp.maximum(i * (ta // HALO) - 1, 0), j))
    shard = pl.BlockSpec((r8, d_model), lambda i: (0, 0))
    return pl.pallas_call(
        body, name="branch_a_bwd_conv", grid=(n_tiles,),
        in_specs=[ANY, blk(0), blk(1), halo(0), halo(1), pl.BlockSpec((ta, d_model), lambda i: (i, 0)),
                  pl.BlockSpec((HALO, d_model), lambda i: (jnp.minimum((i + 1) * (ta // HALO), last_halo), 0)),
                  pl.BlockSpec((HALO, d_model), lambda i: (0, 0)), ANY, ANY],
        out_specs=[pl.BlockSpec((ta, 2 * d_model), lambda i: (i, 0)),
                   pl.BlockSpec((A2_ROWS, d_model), lambda i: (0, 0)),
                   pl.BlockSpec((n_groups, HALO, SUBLANES, LANES), lambda i: (0, 0, 0, 0)),
                   shard, shard, shard],
        out_shape=[SDS(dproj.shape, BF16), SDS((A2_ROWS, d_model), F32),
                   SDS((n_groups, HALO, SUBLANES, LANES), F32)] + [SDS((r8, d_model), F32)] * 3,
        scratch_shapes=[pltpu.VMEM((HALO + ta, LANES), F32), pltpu.VMEM((ta + HALO, LANES), F32),
                        pltpu.VMEM((N_DEV - 1, 3, r8, d_model), BF16), pltpu.VMEM((3, r8, d_model), F32),
                        pltpu.SemaphoreType.DMA((3, 7)), pltpu.SemaphoreType.DMA((3, 7)),
                        pltpu.SemaphoreType.DMA((3,))],
        input_output_aliases={0: 0},
        compiler_params=_params(1),
    )(dproj, proj, proj, proj, proj, d_h1, d_h1, conv_w_full, gp_bf, gp_f32)


def _weight_grads(lhs3, rhs3):
    n_mat, n_tok, d_model = lhs3.shape
    tk = _tile(n_tok, 2048, 16)
    n_k = n_tok // tk

    def body(a_ref, g_ref, o_ref, ob_ref):
        part = _dot_ta(a_ref[...], g_ref[...])

        @pl.when(pl.program_id(1) == 0)
        def _():
            o_ref[...] = part

        @pl.when(pl.program_id(1) != 0)
        def _():
            o_ref[...] += part

        @pl.when(pl.program_id(1) == n_k - 1)
        def _():
            ob_ref[...] = o_ref[...].astype(BF16)

    tile = pl.BlockSpec((None, tk, d_model), lambda a, i: (a, i, 0))
    out = pl.BlockSpec((None, d_model, d_model), lambda a, i: (a, 0, 0))
    return pl.pallas_call(
        body, name="grad_w_pa_pb_o", grid=(n_mat, n_k), in_specs=[tile, tile], out_specs=[out, out],
        out_shape=[SDS((n_mat, d_model, d_model), F32), SDS((n_mat, d_model, d_model), BF16)],
        compiler_params=_params(2),
    )(lhs3, rhs3)


def _grad_w_in_reduce_scatter(xt_bf, dproj, blk_order, small_part):
    n_tok = dproj.shape[0]
    d_model = xt_bf.shape[0]
    dh = d_model // 2
    n_units = 2 * N_DEV
    rsl = small_part.shape[0] // N_DEV

    def body(ord_ref, a_ref, g_ref, p_ref, o_ref, small_ref, acc, fb, sbuf, gbuf, tbuf, rfin, rbuf_s, red,
             send_f, send_s, recv_g, recv_t, recv_f, out_sems, send1, recv1, send2, recv2):
        del ord_ref
        u = pl.program_id(0)
        s = u // 2
        hf = u % 2
        rnd = s // 2
        x, y, c = _mesh_pos()
        sibling = (x, y, 1 - c)
        me = _block_of((x, y, c))
        peers = _peers()

        def rows_of(blk):
            return pl.ds(pl.multiple_of(blk * rsl, SUBLANES), rsl)

        def scatter():
            return [pltpu.make_async_remote_copy(
                src_ref=p_ref.at[rows_of(blk), :], dst_ref=rbuf_s.at[k], send_sem=send1.at[k],
                recv_sem=recv1.at[k], device_id=pos, device_id_type=MESH) for k, (pos, blk) in enumerate(peers)]

        def gather(dst_block=None):
            return [pltpu.make_async_remote_copy(
                src_ref=red, dst_ref=small_ref.at[rows_of(me if dst_block is None else blk), :],
                send_sem=send2.at[k], recv_sem=recv2.at[k], device_id=pos, device_id_type=MESH)
                for k, (pos, blk) in enumerate(peers)]

        def own_slice():
            return pltpu.make_async_copy(red, small_ref.at[rows_of(me), :], out_sems.at[2])

        @pl.when(u == 0)
        def _():
            for cp in scatter():
                cp.start()

        @pl.when(u == 8)
        def _():
            for cp in scatter():
                cp.wait_recv()
            tot = p_ref[rows_of(me), :]
            for k in range(N_DEV - 1):
                tot = tot + rbuf_s[k]
            red[...] = tot
            own_slice().start()
            for cp in gather():
                cp.start()

        n1 = (jnp.bitwise_xor(x, c), jnp.bitwise_xor(y, 1 - c), c)
        n2 = (jnp.bitwise_xor(x, 1 - c), jnp.bitwise_xor(y, c), c)

        def feed(r, half):
            return pltpu.make_async_remote_copy(
                src_ref=fb.at[half], dst_ref=gbuf.at[r, half], send_sem=send_f.at[r, half],
                recv_sem=recv_g.at[r, half], device_id=sibling, device_id_type=MESH)

        def feed_sibling(half):
            return pltpu.make_async_remote_copy(
                src_ref=fb.at[half], dst_ref=rfin.at[0, half], send_sem=send_f.at[3, half],
                recv_sem=recv_f.at[0, half], device_id=sibling, device_id_type=MESH)

        def chip_sum(r, half):
            dst = [tbuf.at[half], rfin.at[2, half], rfin.at[1, half]][r]
            sem = [recv_t.at[half], recv_f.at[2, half], recv_f.at[1, half]][r]
            return pltpu.make_async_remote_copy(
                src_ref=sbuf.at[r, half], dst_ref=dst, send_sem=send_s.at[r, half], recv_sem=sem,
                device_id=[n2, n2, n1][r], device_id_type=MESH)

        def out_copy(half):
            return pltpu.make_async_copy(acc.at[half], o_ref.at[:, pl.ds(half * dh, dh)], out_sems.at[half])

        def partial_sum():
            return _dot(a_ref[:, 0:n_tok], g_ref[...])

        for half in range(2):
            for r in range(3):
                @pl.when(u == 4 * r + 4 + half)
                def _(r=r, half=half):
                    feed(r, half).wait_send()

                @pl.when(u == 4 * r + 2 + half)
                def _(r=r, half=half):
                    feed(r, half).wait_recv()
                    if r == 2:
                        chip_sum(0, half).wait_recv()

            @pl.when(u == 14 + half)
            def _(half=half):
                feed_sibling(half).wait_recv()
                chip_sum(2, half).wait_recv()
                chip_sum(1, half).wait_recv()

        @pl.when(jnp.logical_and(s % 2 == 0, s < 7))
        def _():
            fb[hf] = partial_sum().astype(BF16)

        @pl.when(jnp.logical_or(s == 1, s == 3))
        def _():
            sbuf[rnd, hf] = (partial_sum() + gbuf[rnd, hf].astype(F32)).astype(BF16)

        @pl.when(s == 5)
        def _():
            sbuf[2, hf] = (partial_sum() + gbuf[2, hf].astype(F32) + tbuf[hf].astype(F32)).astype(BF16)

        @pl.when(s == 7)
        def _():
            acc[hf] = (partial_sum() + rfin[0, hf].astype(F32) + rfin[1, hf].astype(F32)
                       + rfin[2, hf].astype(F32))

        for half in range(2):
            for r in range(3):
                @pl.when(u == 4 * r + half)
                def _(r=r, half=half):
                    feed(r, half).start()

                @pl.when(u == 4 * r + 2 + half)
                def _(r=r, half=half):
                    chip_sum(r, half).start()

            @pl.when(u == 12 + half)
            def _(half=half):
                feed_sibling(half).start()

        @pl.when(u == 14)
        def _():
            out_copy(0).start()

        @pl.when(u == 15)
        def _():
            out_copy(1).start()
            for half in range(2):
                feed_sibling(half).wait_send()
                for r in range(3):
                    chip_sum(r, half).wait_send()
                out_copy(half).wait()
            for cp in gather("theirs"):
                cp.wait_recv()
            for cp in scatter() + gather():
                cp.wait_send()
            own_slice().wait()

    grid_spec = pltpu.PrefetchScalarGridSpec(
        num_scalar_prefetch=1, grid=(n_units,),
        in_specs=[VMEM, pl.BlockSpec((n_tok, dh), lambda u, o: (0, 2 * o[u // 2] + u % 2)), VMEM],
        out_specs=[ANY, ANY],
        scratch_shapes=[pltpu.VMEM((2, d_model, dh), F32), pltpu.VMEM((2, d_model, dh), BF16),
                        pltpu.VMEM((3, 2, d_model, dh), BF16), pltpu.VMEM((3, 2, d_model, dh), BF16),
                        pltpu.VMEM((2, d_model, dh), BF16), pltpu.VMEM((3, 2, d_model, dh), BF16),
                        pltpu.VMEM((N_DEV - 1, rsl, LANES), F32), pltpu.VMEM((rsl, LANES), F32),
                        pltpu.SemaphoreType.DMA((4, 2)), pltpu.SemaphoreType.DMA((3, 2)),
                        pltpu.SemaphoreType.DMA((3, 2)), pltpu.SemaphoreType.DMA((2,)),
                        pltpu.SemaphoreType.DMA((3, 2)),
                        pltpu.SemaphoreType.DMA((3,))] + [pltpu.SemaphoreType.DMA((N_DEV - 1,))] * 4)
    return pl.pallas_call(
        body, name="grad_w_in_reduce_scatter", grid_spec=grid_spec,
        out_shape=[SDS((d_model, d_model), F32), SDS(small_part.shape, F32)], compiler_params=_params(1),
    )(blk_order, xt_bf, dproj, small_part)


def _grad_x_adamw(dproj, w_all, dr, w, g, m, v, small, loss_rows, packed, groups):
    n_tok, d_model = dr.shape
    tm = _tile(n_tok, 256, 16)
    n_steps = n_tok // tm
    tr = w.shape[0] // n_steps
    n_pk, n_grp = len(packed), len(groups)
    n_out = 4 * (n_pk + n_grp)

    def body(dp_ref, w_hbm, dr_ref, ws_ref, g_ref, m_ref, v_ref, small_ref, *rest):
        pk_in, rest = rest[:3 * n_pk], rest[3 * n_pk:]
        grp_in, rest = rest[:4 * n_grp], rest[4 * n_grp:]
        o_ref, gp_ref, d_ref, mo_ref, vo_ref, loss_ref = rest[:6]
        outs, (w_ref, w_sems) = rest[6:6 + n_out], rest[6 + n_out:]
        i = pl.program_id(0)

        def fetch(j):
            return pltpu.make_async_copy(w_hbm.at[j], w_ref.at[j], w_sems.at[j])

        def grad_x_tile(first):
            acc = DEEPNORM_ALPHA * dr_ref[...]
            for j in range(N_DEV):
                if first:
                    fetch(j).wait()
                acc = acc + _dot_tb(dp_ref[:, j * d_model:(j + 1) * d_model], w_ref[j])
            o_ref[...] = acc

        @pl.when(i == 0)
        def _():
            for j in range(N_DEV):
                fetch(j).start()
            grad_x_tile(True)

        @pl.when(i != 0)
        def _():
            grad_x_tile(False)

        grad = g_ref[...]
        gp_ref[...] = grad
        d_ref[...], mo_ref[...], vo_ref[...] = _adamw_math(ws_ref[...], grad, m_ref[...], v_ref[...])

        @pl.when(i == n_steps - 1)
        def _():
            def update(p, pw, grad, pm, pv):
                og, od, om, ov = outs[4 * p:4 * p + 4]
                og[...] = grad
                od[...], om[...], ov[...] = _adamw_math(pw[...], grad, pm[...], pv[...])

            for p, (pw_arr, _, _, row0) in enumerate(packed):
                pw, pm, pv = pk_in[3 * p:3 * p + 3]
                update(p, pw, small_ref[row0:row0 + pw_arr.shape[0], :], pm, pv)
            for p in range(n_grp):
                pw, pg, pm, pv = grp_in[4 * p:4 * p + 4]
                update(n_pk + p, pw, pg[...], pm, pv)
            sq = small_ref[loss_rows[0]:loss_rows[0] + loss_rows[1], :]
            loss_ref[...] = jnp.sum(_colsum(sq), axis=1, keepdims=True) * (0.5 / d_model)

    tile = pl.BlockSpec((tm, d_model), lambda i: (i, 0))
    slab = pl.BlockSpec((tr, w.shape[1]), lambda i: (i, 0))
    flat = [a for pk in packed for a in pk[:3]] + [a for grp in groups for a in grp]
    shapes = [pk[0].shape for pk in packed] + [grp[0].shape for grp in groups]
    res = pl.pallas_call(
        body, name="grad_x_adamw", grid=(n_steps,),
        in_specs=[pl.BlockSpec((tm, N_DEV * d_model), lambda i: (i, 0)), ANY, tile, slab, slab, slab, slab]
        + [VMEM] * (1 + len(flat)),
        out_specs=[tile, slab, slab, slab, slab, VMEM] + [VMEM] * n_out,
        out_shape=[SDS((n_tok, d_model), F32)] + [SDS(w.shape, F32)] * 4 + [SDS((1, 1), F32)]
        + [SDS(shape, F32) for shape in shapes for _ in range(4)],
        scratch_shapes=[pltpu.VMEM(w_all.shape, BF16), pltpu.SemaphoreType.DMA((N_DEV,))],
        compiler_params=_params(1),
    )(dproj, w_all, dr, w, g, m, v, small, *flat)
    return (res[0], tuple(res[1:5]), [tuple(res[6 + 4 * p:10 + 4 * p]) for p in range(n_pk + n_grp)],
            res[5].reshape(()))


def _adamw_math(w, g, m, v):
    m = ADAM_B1 * m + (1.0 - ADAM_B1) * g
    v = ADAM_B2 * v + (1.0 - ADAM_B2) * (g * g)
    m_hat = m / (1.0 - ADAM_B1 ** ADAM_STEP)
    v_hat = v / (1.0 - ADAM_B2 ** ADAM_STEP)
    delta = -ADAM_LR * (m_hat / (jnp.sqrt(v_hat) + ADAM_EPS) + ADAM_WD * w)
    return delta, m, v


def _as_rows(a):
    return a.reshape(-1, LANES)


def kernel(x, w_in, b_in, conv_w, conv_b, gn_g, gn_b, ln_v_g, ln_v_b, w_spatial, b_spatial, w_pa, w_pb, w_o, b_o, ln_out_g, ln_out_b, loss_target, m_w_in, m_b_in, m_conv_w, m_conv_b, m_gn_g, m_gn_b, m_ln_v_g, m_ln_v_b, m_w_spatial, m_b_spatial, m_w_pa, m_w_pb, m_w_o, m_b_o, m_ln_out_g, m_ln_out_b, v_w_in, v_b_in, v_conv_w, v_conv_b, v_gn_g, v_gn_b, v_ln_v_g, v_ln_v_b, v_w_spatial, v_b_spatial, v_w_pa, v_w_pb, v_w_o, v_b_o, v_ln_out_g, v_ln_out_b):
    n_batch, seq, d_model = x.shape
    n_tok = n_batch * seq
    n_heads = d_model // LANES
    dc = conv_w.shape[1]
    me = 4 * lax.axis_index("x") + 2 * lax.axis_index("y") + lax.axis_index("c")
    row = lambda a: a.reshape(1, d_model)

    x2 = x.reshape(n_tok, d_model)
    target2 = loss_target.reshape(n_tok, d_model)
    b_spatial_t = b_spatial.T

    first = jnp.where(lax.axis_index("c") == 1, 4, 2)
    second = 6 - first
    ag_rel = jnp.stack([0 * first, 0 * first + 1, first, second + 1, second, first + 1, 0 * first + 6, 0 * first + 7])
    ag_blocks = jnp.bitwise_xor(me, ag_rel).astype(jnp.int32)
    proj, xt_bf, w_all, wp_all, cw_all = _proj_all_gather(
        x2, w_in, w_pa, w_pb, w_o, conv_w, b_in.reshape(N_DEV, 1, d_model), ag_blocks)
    wp_full = wp_all.reshape(3, d_model, d_model)
    conv_w_full = jnp.pad(cw_all.transpose(1, 0, 2).reshape(CONV_K, d_model), ((0, HALO - CONV_K), (0, 0)))

    h3, h1 = _branch_a_fwd(proj, conv_w_full, row(conv_b), row(gn_g), row(gn_b), seq)
    s = _branch_b_fwd(proj, row(ln_v_g), row(ln_v_b), w_spatial, b_spatial_t, seq)

    dproj, d_h3, d_s, dr, lhs3, rhs3, vec_mid = _mid(
        h3, s, proj, x2, target2, wp_full, row(b_o), row(ln_out_g), row(ln_out_b))

    dproj, vec_b, d_ws, d_bs_t = _branch_b_bwd(dproj, proj, d_s, row(ln_v_g), row(ln_v_b), w_spatial, b_spatial_t, seq)
    dproj, d_h1, vec_a1 = _branch_a_bwd_norm(dproj, proj, h1, d_h3, row(gn_g), row(gn_b), seq)
    gp_f32, gp_bf = _weight_grads(lhs3, rhs3)
    dproj, vec_a2, d_cw8, g_w_pa, g_w_pb, g_w_o = _branch_a_bwd_conv(
        dproj, proj, d_h1, conv_w_full, gp_bf, gp_f32, seq)

    d_cw = jnp.sum(d_cw8, axis=2)
    pieces = [
        _as_rows(jnp.concatenate([vec_a2[0:2], vec_a1[0:1], vec_b[0:3], vec_mid[3:5]], axis=0)),
        _as_rows(jnp.concatenate([vec_a1[3:4], vec_a1[1:3], vec_b[3:5], vec_mid[2:3], vec_mid[0:2]], axis=0)),
        _as_rows(d_bs_t.T), _as_rows(d_ws), _as_rows(d_cw), _as_rows(vec_mid[5:6]),
    ]
    n_rows = sum(p.shape[0] for p in pieces)
    pad_rows = -n_rows % (N_DEV * SUBLANES)
    small_part = jnp.concatenate(pieces + [jnp.zeros((pad_rows, LANES), F32)], axis=0)

    rs_rel = jnp.stack([0 * first + 7, 0 * first + 6, first + 1, second, second + 1, first, 0 * first + 1, 0 * first])
    rs_blocks = jnp.bitwise_xor(me, rs_rel).astype(jnp.int32)
    g_w_in, small = _grad_w_in_reduce_scatter(xt_bf, dproj, rs_blocks, small_part)

    g_rows = d_model // LANES
    o0 = N_DEV * g_rows
    o1 = o0 + 8 * g_rows
    o2 = o1 + n_heads
    o3 = o2 + n_heads * LANES
    o4 = o3 + n_heads * HALO
    g_cw_full = small[o3:o4].reshape(n_heads, HALO, LANES).transpose(1, 0, 2).reshape(HALO, d_model)
    g_conv_w = lax.dynamic_slice(g_cw_full, (0, me * dc), (CONV_K, dc))

    two_d = lambda a: a.reshape(-1, a.shape[-1]) if a.ndim != 1 else (
        a.reshape(-1, LANES) if a.shape[0] % LANES == 0 else a.reshape(1, -1))
    names = ["b_in", "conv_w", "conv_b", "gn_g", "gn_b", "ln_v_g", "ln_v_b", "w_spatial", "b_spatial",
             "w_pa", "w_pb", "w_o", "b_o", "ln_out_g", "ln_out_b"]
    ws = dict(b_in=b_in, conv_w=conv_w, conv_b=conv_b, gn_g=gn_g, gn_b=gn_b, ln_v_g=ln_v_g, ln_v_b=ln_v_b,
              w_spatial=w_spatial, b_spatial=b_spatial, w_pa=w_pa, w_pb=w_pb, w_o=w_o, b_o=b_o,
              ln_out_g=ln_out_g, ln_out_b=ln_out_b)
    ms = dict(b_in=m_b_in, conv_w=m_conv_w, conv_b=m_conv_b, gn_g=m_gn_g, gn_b=m_gn_b, ln_v_g=m_ln_v_g,
              ln_v_b=m_ln_v_b, w_spatial=m_w_spatial, b_spatial=m_b_spatial, w_pa=m_w_pa, w_pb=m_w_pb,
              w_o=m_w_o, b_o=m_b_o, ln_out_g=m_ln_out_g, ln_out_b=m_ln_out_b)
    vs = dict(b_in=v_b_in, conv_w=v_conv_w, conv_b=v_conv_b, gn_g=v_gn_g, gn_b=v_gn_b, ln_v_g=v_ln_v_g,
              ln_v_b=v_ln_v_b, w_spatial=v_w_spatial, b_spatial=v_b_spatial, w_pa=v_w_pa, w_pb=v_w_pb,
              w_o=v_w_o, b_o=v_b_o, ln_out_g=v_ln_out_g, ln_out_b=v_ln_out_b)
    vec_names = ["conv_b", "gn_g", "gn_b", "ln_v_g", "ln_v_b", "b_o", "ln_out_g", "ln_out_b"]
    first_row = dict(b_in=0, b_spatial=o1, w_spatial=o2, **{n: o0 + a * g_rows for a, n in enumerate(vec_names)})
    given = dict(conv_w=g_conv_w, w_pa=g_w_pa, w_pb=g_w_pb, w_o=g_w_o)
    grad_x, upd_w_in, upd, loss = _grad_x_adamw(
        dproj, w_all, dr, w_in, g_w_in, m_w_in, v_w_in, small, (o4, g_rows),
        [(two_d(ws[n]), two_d(ms[n]), two_d(vs[n]), row0) for n, row0 in first_row.items()],
        [tuple(two_d(a) for a in (ws[n], grad, ms[n], vs[n])) for n, grad in given.items()])
    grad_x = grad_x.reshape(x.shape)
    res = {n: tuple(a.reshape(ws[n].shape) for a in u) for n, u in zip([*first_row, *given], upd)}
    res["w_in"] = upd_w_in

    order = ["w_in"] + names
    return (loss, grad_x, *[res[n][k] for k in range(4) for n in order])
```

```python
import jax
import jax.numpy as jnp
from jax import lax
from jax.experimental import pallas as pl
from jax.experimental.pallas import tpu as pltpu

F32 = jnp.float32
BF16 = jnp.bfloat16
SDS = jax.ShapeDtypeStruct

N_DEV = 8
LANES = 128
SUBLANES = 8
CONV_K = 31
HALO = 32
ELEMENTWISE_ROWS = 512
LN_EPS = 1e-5
DEEPNORM_ALPHA = 2.0 ** 0.25
ADAM_LR, ADAM_B1, ADAM_B2, ADAM_EPS, ADAM_WD, ADAM_STEP = 0.001, 0.9, 0.999, 1e-08, 0.01, 10
GELU_C = 0.7978845608028654
GELU_A = 0.044715
VMEM_LIMIT = 56 * 1024 * 1024
MESH = pl.DeviceIdType.MESH
ANY = pl.BlockSpec(memory_space=pl.ANY)
VMEM = pl.BlockSpec(memory_space=pltpu.VMEM)


def _params(n_grid=0):
    sem = ("arbitrary",) * n_grid if n_grid else None
    return pltpu.CompilerParams(dimension_semantics=sem, vmem_limit_bytes=VMEM_LIMIT)


def _tile(n, pref, mult):
    t = min(n, pref)
    while n % t or t % mult:
        t -= 1
    return t


def _colsum(v):
    return jnp.sum(v, axis=0, keepdims=True)


def _sigmoid(v):
    return jax.nn.sigmoid(v)


def _silu_and_grad(v):
    s = _sigmoid(v)
    val = v * s
    return val, s + val * (1.0 - s)


def _gelu_and_grad(v):
    v2 = v * v
    sg = _sigmoid(v * (2.0 * GELU_C + (2.0 * GELU_C * GELU_A) * v2))
    grad = sg + v * sg * (1.0 - sg) * (2.0 * GELU_C + (6.0 * GELU_C * GELU_A) * v2)
    return v * sg, grad


def _tril_mask():
    r = lax.broadcasted_iota(jnp.int32, (LANES, LANES), 0)
    c = lax.broadcasted_iota(jnp.int32, (LANES, LANES), 1)
    return c <= r


def _dot(a, b):
    return jnp.dot(a, b, preferred_element_type=F32)


def _dot_tb(a, b):
    return lax.dot_general(a, b, (((1,), (1,)), ((), ())), preferred_element_type=F32)


def _dot_ta(a, b):
    return lax.dot_general(a, b, (((0,), (0,)), ((), ())), preferred_element_type=F32)


def _mesh_pos():
    return lax.axis_index("x"), lax.axis_index("y"), lax.axis_index("c")


def _block_of(pos):
    return 4 * pos[0] + 2 * pos[1] + pos[2]


def _peers():
    x, y, c = _mesh_pos()
    out = []
    for k in range(1, N_DEV):
        pos = (1 - x if k & 4 else x, 1 - y if k & 2 else y, 1 - c if k & 1 else c)
        out.append((pos, _block_of(pos)))
    return out


def _proj_all_gather(x2, w_in, w_pa, w_pb, w_o, conv_w, b_in3, blk_order):
    n_tok, d_model = x2.shape
    r8 = w_pa.shape[0]
    kc, dc = conv_w.shape
    tm = _tile(n_tok, 1024, LANES)
    n_t = n_tok // tm
    n_arr = 3

    def body(ord_ref, x_ref, b_ref, win_ref, wpa_ref, wpb_ref, wo_ref, cw_ref,
             proj_ref, xt_ref, wall_ref, wp_ref, cwall_ref,
             wbuf, xbuf, st_p, send_sems, recv_sems, local_sems, wall_sems):
        s = pl.program_id(0)
        t = pl.program_id(1)
        x, y, c = _mesh_pos()
        me = (x, y, c)
        sibling = (x, y, 1 - c)
        n1 = (jnp.bitwise_xor(x, c), jnp.bitwise_xor(y, 1 - c))
        n2 = (jnp.bitwise_xor(x, 1 - c), jnp.bitwise_xor(y, c))
        dg = (1 - x, 1 - y)
        outs = [wbuf, wp_ref, cwall_ref]
        srcs = [None, st_p, cw_ref]
        consumed = [me, sibling, (*n1, c), (*n2, 1 - c), (*n2, c), (*n1, 1 - c), (*dg, c), (*dg, 1 - c)]
        leaves = [(me, sibling), (me, (*n1, c)), (me, (*n2, c)), ((*n1, c), (*n2, c)),
                  ((*n1, c), sibling), ((*n2, c), sibling), ((*dg, c), sibling)]
        lands = [sibling, (*n1, c), (*n2, c), (*dg, c), (*n2, 1 - c), (*n1, 1 - c), (*dg, 1 - c)]

        def slot(o, pos):
            return o.at[:, _block_of(pos)] if o is wp_ref else o.at[_block_of(pos)]

        def copy(a, k, block, to, src=None):
            o = outs[a]
            return pltpu.make_async_remote_copy(
                src_ref=slot(o, block) if src is None else src, dst_ref=slot(o, block),
                send_sem=send_sems.at[a, k], recv_sem=recv_sems.at[a, k],
                device_id=to, device_id_type=MESH)

        def send(a, k):
            block, to = leaves[k]
            return copy(a, k, block, to, src=srcs[a] if k < 3 else None)

        def recv(a, k):
            return copy(a, k, lands[k], me)

        def local_copies():
            return [pltpu.make_async_copy(srcs[a], slot(outs[a], me), local_sems.at[a]) for a in (1, 2)]

        def to_hbm(step):
            return pltpu.make_async_copy(slot(wbuf, consumed[step]), slot(wall_ref, consumed[step]),
                                         wall_sems.at[step])

        def at_step(step):
            return pl.when(jnp.logical_and(s == step, t == 0))

        @at_step(0)
        def _():
            slot(wbuf, me)[...] = win_ref[...].astype(BF16)
            st_p[0] = wpa_ref[...].astype(BF16)
            st_p[1] = wpb_ref[...].astype(BF16)
            st_p[2] = wo_ref[...].astype(BF16)
            for a in range(n_arr):
                send(a, 0).start()
                send(a, 1).start()
            for cp in local_copies():
                cp.start()
            to_hbm(0).start()

        @at_step(1)
        def _():
            recv(0, 0).wait_recv()
            to_hbm(1).start()

        for rnd in range(3):
            @at_step(2 + 2 * rnd)
            def _(rnd=rnd):
                if rnd == 0:
                    for a in range(n_arr):
                        send(a, 2).start()
                recv(0, 1 + rnd).wait_recv()
                if rnd == 0:
                    send(0, 3).start()
                send(0, 4 + rnd).start()
                to_hbm(2 + 2 * rnd).start()

            @at_step(3 + 2 * rnd)
            def _(rnd=rnd):
                for a in (1, 2):
                    recv(a, 1 + rnd).wait_recv()
                    if rnd == 0:
                        send(a, 3).start()
                    send(a, 4 + rnd).start()
                recv(0, 4 + rnd).wait_recv()
                to_hbm(3 + 2 * rnd).start()

        rows = pl.ds(pl.multiple_of(t * tm, tm), tm)

        @pl.when(s == 0)
        def _():
            xb = x_ref[...].astype(BF16)
            xbuf[rows, :] = xb
            xt_ref[...] = xb.T

        proj_ref[...] = _dot(xbuf[rows, :], wbuf[ord_ref[s]]) + b_ref[...]

        @pl.when(jnp.logical_and(s == N_DEV - 1, t == n_t - 1))
        def _():
            for a in (1, 2):
                for k in (0, 4, 5, 6):
                    recv(a, k).wait_recv()
            for a in range(n_arr):
                for k in range(7):
                    send(a, k).wait_send()
            for cp in local_copies() + [to_hbm(step) for step in range(N_DEV)]:
                cp.wait()

    grid_spec = pltpu.PrefetchScalarGridSpec(
        num_scalar_prefetch=1, grid=(N_DEV, n_t),
        in_specs=[pl.BlockSpec((tm, d_model), lambda s, t, o: (jnp.where(s == 0, t, n_t - 1), 0)),
                  pl.BlockSpec((None, 1, d_model), lambda s, t, o: (o[s], 0, 0)),
                  VMEM, VMEM, VMEM, VMEM, VMEM],
        out_specs=[pl.BlockSpec((tm, d_model), lambda s, t, o: (t, o[s])),
                   pl.BlockSpec((d_model, tm), lambda s, t, o: (0, jnp.where(s == 0, t, n_t))),
                   ANY, ANY, ANY],
        scratch_shapes=[pltpu.VMEM((N_DEV, d_model, d_model), BF16), pltpu.VMEM((n_tok, d_model), BF16),
                        pltpu.VMEM((3, r8, d_model), BF16),
                        pltpu.SemaphoreType.DMA((n_arr, 7)), pltpu.SemaphoreType.DMA((n_arr, 7)),
                        pltpu.SemaphoreType.DMA((3,)), pltpu.SemaphoreType.DMA((N_DEV,))])
    return pl.pallas_call(
        body, name="proj_all_gather", grid_spec=grid_spec,
        out_shape=[SDS((n_tok, N_DEV * d_model), F32), SDS((d_model, (n_t + 1) * tm), BF16),
                   SDS((N_DEV, d_model, d_model), BF16), SDS((3, N_DEV, r8, d_model), BF16),
                   SDS((N_DEV, kc, dc), F32)],
        compiler_params=_params(2),
    )(blk_order, x2, b_in3, w_in, w_pa, w_pb, w_o, conv_w)


def _conv_rows(ta):
    return _tile(ta, 64, SUBLANES)


def _branch_a_fwd(proj, conv_w_full, conv_b, gn_g, gn_b, seq):
    n_tok = proj.shape[0]
    d_model = conv_b.shape[1]
    ta = _tile(seq, ELEMENTWISE_ROWS, HALO)
    per_seq = seq // ta
    rc = _conv_rows(ta)

    def body(av_ref, ag_ref, gt_ref, avh_ref, agh_ref, cw_ref, cb_ref, gg_ref, gb_ref,
             h3_ref, h1_ref, ext):
        keep = jnp.where(pl.program_id(0) % per_seq == 0, 0.0, 1.0)

        def group(g, carry):
            sl = pl.ds(pl.multiple_of(g * LANES, LANES), LANES)
            ext[0:HALO, :] = avh_ref[:, sl] * _sigmoid(agh_ref[:, sl]) * keep
            ext[HALO:HALO + ta, :] = av_ref[:, sl] * _sigmoid(ag_ref[:, sl])
            for r0 in range(0, ta, rc):
                acc = jnp.broadcast_to(cb_ref[:, sl], (rc, LANES))
                for k in range(CONV_K):
                    acc = acc + ext[pl.ds(r0 + HALO - (CONV_K - 1) + k, rc), :] * cw_ref[k:k + 1, sl]
                h1_ref[pl.ds(r0, rc), sl] = acc
            h1 = h1_ref[:, sl]
            mu = jnp.mean(h1, axis=-1, keepdims=True)
            dlt = h1 - mu
            var = jnp.mean(dlt * dlt, axis=-1, keepdims=True)
            h2 = dlt * lax.rsqrt(var + LN_EPS) * gg_ref[:, sl] + gb_ref[:, sl]
            gate = gt_ref[:, sl]
            h3_ref[:, sl] = (h2 * _sigmoid(h2) * gate * _sigmoid(gate)).astype(BF16)
            return carry

        lax.fori_loop(0, d_model // LANES, group, 0, unroll=True)

    blk = lambda j: pl.BlockSpec((ta, d_model), lambda i: (i, j))
    halo = lambda j: pl.BlockSpec((HALO, d_model), lambda i: (jnp.maximum(i * (ta // HALO) - 1, 0), j))
    row = pl.BlockSpec((1, d_model), lambda i: (0, 0))
    return pl.pallas_call(
        body, name="branch_a_fwd", grid=(n_tok // ta,),
        in_specs=[blk(0), blk(1), blk(2), halo(0), halo(1),
                  pl.BlockSpec((HALO, d_model), lambda i: (0, 0)), row, row, row],
        out_specs=[pl.BlockSpec((ta, d_model), lambda i: (i, 0))] * 2,
        out_shape=[SDS((n_tok, d_model), BF16), SDS((n_tok, d_model), F32)],
        scratch_shapes=[pltpu.VMEM((HALO + ta, LANES), F32)],
        compiler_params=_params(1),
    )(proj, proj, proj, proj, proj, conv_w_full, conv_b, gn_g, gn_b)


def _branch_b_fwd(proj, ln_g, ln_b, w_spatial, b_spatial_t, seq):
    n_tok = proj.shape[0]
    d_model = ln_g.shape[1]
    n_heads = d_model // LANES
    tb = _tile(seq, ELEMENTWISE_ROWS, LANES)

    def body(u_ref, v_ref, bg_ref, lg_ref, lb_ref, ws_ref, bs_ref, s_ref, vn_buf):
        v, _ = _gelu_and_grad(v_ref[...])
        mu = jnp.mean(v, axis=-1, keepdims=True)
        dlt = v - mu
        var = jnp.mean(dlt * dlt, axis=-1, keepdims=True)
        vn_buf[...] = (dlt * lax.rsqrt(var + LN_EPS) * lg_ref[...] + lb_ref[...]).astype(BF16)
        tril = _tril_mask()
        for h in range(n_heads):
            cols = slice(h * LANES, (h + 1) * LANES)
            w_h = jnp.where(tril, ws_ref[h], 0.0).astype(BF16)
            bias = bs_ref[:, h:h + 1]
            for ch in range(tb // LANES):
                rows = slice(ch * LANES, (ch + 1) * LANES)
                mix = _dot(w_h, vn_buf[rows, cols]) + bias
                u, _ = _gelu_and_grad(u_ref[rows, cols])
                gate = bg_ref[rows, cols]
                s_ref[rows, cols] = (u * mix * gate * _sigmoid(gate)).astype(BF16)

    blk = lambda j: pl.BlockSpec((tb, d_model), lambda i: (i, j))
    row = pl.BlockSpec((1, d_model), lambda i: (0, 0))
    return pl.pallas_call(
        body, name="branch_b_fwd", grid=(n_tok // tb,),
        in_specs=[blk(3), blk(4), blk(5), row, row,
                  pl.BlockSpec((n_heads, LANES, LANES), lambda i: (0, 0, 0)),
                  pl.BlockSpec((LANES, n_heads), lambda i: (0, 0))],
        out_specs=pl.BlockSpec((tb, d_model), lambda i: (i, 0)),
        out_shape=SDS((n_tok, d_model), BF16),
        scratch_shapes=[pltpu.VMEM((tb, d_model), BF16)],
        compiler_params=_params(1),
    )(proj, proj, proj, ln_g, ln_b, w_spatial, b_spatial_t)


MID_ROWS = 8


def _mid(h3, s, proj, x2, target, wp_full, b_o, lo_g, lo_b):
    n_tok, d_model = x2.shape
    tm = _tile(n_tok, 256, 16)

    def body(h3_ref, s_ref, ma_ref, mb_ref, x_ref, t_ref, wpa_ref, wpb_ref, wo_ref, bo_ref,
             lg_ref, lb_ref, dproj_ref, dh3_ref, ds_ref, dr_ref, lhs3_ref, rhs3_ref, vec_ref):
        @pl.when(pl.program_id(0) == 0)
        def _():
            vec_ref[...] = jnp.zeros_like(vec_ref)

        h3 = h3_ref[...]
        s = s_ref[...]
        ya = _dot(h3, wpa_ref[...])
        yb = _dot(s, wpb_ref[...])
        ga = _sigmoid(ma_ref[...])
        gb = _sigmoid(mb_ref[...])
        mixed = (ga * ya + gb * yb).astype(BF16)
        lhs3_ref[0] = h3
        lhs3_ref[1] = s
        lhs3_ref[2] = mixed
        r = DEEPNORM_ALPHA * x_ref[...] + _dot(mixed, wo_ref[...]) + bo_ref[...]
        mu = jnp.mean(r, axis=-1, keepdims=True)
        dlt = r - mu
        rstd = lax.rsqrt(jnp.mean(dlt * dlt, axis=-1, keepdims=True) + LN_EPS)
        rhat = dlt * rstd
        diff = rhat * lg_ref[...] + lb_ref[...] - t_ref[...]
        dy = diff * (1.0 / d_model)
        vec_ref[0:1, :] += _colsum(dy * rhat)
        vec_ref[1:2, :] += _colsum(dy)
        vec_ref[5:6, :] += _colsum(diff * diff)
        drh = dy * lg_ref[...]
        dr = rstd * (drh - jnp.mean(drh, axis=-1, keepdims=True)
                     - rhat * jnp.mean(drh * rhat, axis=-1, keepdims=True))
        vec_ref[2:3, :] += _colsum(dr)
        dr_ref[...] = dr
        drb = dr.astype(BF16)
        rhs3_ref[2] = drb
        dmixed = _dot_tb(drb, wo_ref[...])
        dya_f = dmixed * ga
        dyb_f = dmixed * gb
        dma = dya_f * ya * (1.0 - ga)
        dmb = dyb_f * yb * (1.0 - gb)
        vec_ref[3:4, :] += _colsum(dma)
        vec_ref[4:5, :] += _colsum(dmb)
        dproj_ref[:, 0:d_model] = dma.astype(BF16)
        dproj_ref[:, d_model:2 * d_model] = dmb.astype(BF16)
        dya = dya_f.astype(BF16)
        dyb = dyb_f.astype(BF16)
        rhs3_ref[0] = dya
        rhs3_ref[1] = dyb
        dh3_ref[...] = _dot_tb(dya, wpa_ref[...])
        ds_ref[...] = _dot_tb(dyb, wpb_ref[...])

    tile = pl.BlockSpec((tm, d_model), lambda i: (i, 0))
    full = lambda a: pl.BlockSpec((None, d_model, d_model), lambda i: (a, 0, 0))
    row = pl.BlockSpec((1, d_model), lambda i: (0, 0))
    stack = pl.BlockSpec((3, tm, d_model), lambda i: (0, i, 0))
    bf3 = SDS((3, n_tok, d_model), BF16)
    f32 = SDS((n_tok, d_model), F32)
    return pl.pallas_call(
        body, name="mid", grid=(n_tok // tm,),
        in_specs=[tile, tile, pl.BlockSpec((tm, d_model), lambda i: (i, 6)),
                  pl.BlockSpec((tm, d_model), lambda i: (i, 7)), tile, tile, full(0), full(1), full(2),
                  row, row, row],
        out_specs=[pl.BlockSpec((tm, 2 * d_model), lambda i: (i, 3)), tile, tile, tile, stack, stack,
                   pl.BlockSpec((MID_ROWS, d_model), lambda i: (0, 0))],
        out_shape=[SDS((n_tok, N_DEV * d_model), BF16), f32, f32, f32, bf3, bf3,
                   SDS((MID_ROWS, d_model), F32)],
        compiler_params=_params(1),
    )(h3, s, proj, proj, x2, target, wp_full, wp_full, wp_full, b_o, lo_g, lo_b)


B_ROWS = 8


def _branch_b_bwd(dproj, proj, d_s, ln_g, ln_b, w_spatial, b_spatial_t, seq):
    n_tok = proj.shape[0]
    d_model = ln_g.shape[1]
    n_heads = d_model // LANES
    tb = _tile(seq, ELEMENTWISE_ROWS, LANES)

    def body(dproj_in, u_ref, v_ref, bg_ref, ds_ref, lg_ref, lb_ref, ws_ref, bs_ref,
             dproj_ref, vec_ref, dws_ref, dbs_ref, vn_buf, dv_buf):
        del dproj_in

        @pl.when(pl.program_id(0) == 0)
        def _():
            vec_ref[...] = jnp.zeros_like(vec_ref)
            dws_ref[...] = jnp.zeros_like(dws_ref)
            dbs_ref[...] = jnp.zeros_like(dbs_ref)

        v, dgelu_v = _gelu_and_grad(v_ref[...])
        mu = jnp.mean(v, axis=-1, keepdims=True)
        dlt = v - mu
        rstd = lax.rsqrt(jnp.mean(dlt * dlt, axis=-1, keepdims=True) + LN_EPS)
        vhat = dlt * rstd
        vn_buf[...] = (vhat * lg_ref[...] + lb_ref[...]).astype(BF16)
        tril = _tril_mask()
        for h in range(n_heads):
            cols = slice(h * LANES, (h + 1) * LANES)
            w_h = jnp.where(tril, ws_ref[h], 0.0).astype(BF16)
            bias = bs_ref[:, h:h + 1]
            for ch in range(tb // LANES):
                rows = slice(ch * LANES, (ch + 1) * LANES)
                vn = vn_buf[rows, cols]
                mix = _dot(w_h, vn) + bias
                u, dgelu_u = _gelu_and_grad(u_ref[rows, cols])
                sg, dsilu = _silu_and_grad(bg_ref[rows, cols])
                dsv = ds_ref[rows, cols]
                du = dsv * mix * sg * dgelu_u
                dbg = dsv * u * mix * dsilu
                dmix = dsv * u * sg
                dmix_bf = dmix.astype(BF16)
                dproj_ref[rows, cols] = du.astype(BF16)
                dproj_ref[rows, 2 * d_model + h * LANES:2 * d_model + (h + 1) * LANES] = dbg.astype(BF16)
                vec_ref[0:1, cols] += _colsum(du)
                vec_ref[2:3, cols] += _colsum(dbg)
                dbs_ref[:, h:h + 1] += jnp.sum(dmix, axis=1, keepdims=True)
                dws_ref[h] += jnp.where(tril, _dot_tb(dmix_bf, vn), 0.0)
                dv_buf[rows, cols] = _dot_ta(w_h, dmix_bf)
        dvn = dv_buf[...]
        vec_ref[3:4, :] += _colsum(dvn * vhat)
        vec_ref[4:5, :] += _colsum(dvn)
        dvh = dvn * lg_ref[...]
        dv = rstd * (dvh - jnp.mean(dvh, axis=-1, keepdims=True)
                     - vhat * jnp.mean(dvh * vhat, axis=-1, keepdims=True)) * dgelu_v
        vec_ref[1:2, :] += _colsum(dv)
        dproj_ref[:, d_model:2 * d_model] = dv.astype(BF16)

    blk = lambda j: pl.BlockSpec((tb, d_model), lambda i: (i, j))
    row = pl.BlockSpec((1, d_model), lambda i: (0, 0))
    return pl.pallas_call(
        body, name="branch_b_bwd", grid=(n_tok // tb,),
        in_specs=[ANY, blk(3), blk(4), blk(5), pl.BlockSpec((tb, d_model), lambda i: (i, 0)), row, row,
                  pl.BlockSpec((n_heads, LANES, LANES), lambda i: (0, 0, 0)),
                  pl.BlockSpec((LANES, n_heads), lambda i: (0, 0))],
        out_specs=[pl.BlockSpec((tb, 3 * d_model), lambda i: (i, 1)),
                   pl.BlockSpec((B_ROWS, d_model), lambda i: (0, 0)),
                   pl.BlockSpec((n_heads, LANES, LANES), lambda i: (0, 0, 0)),
                   pl.BlockSpec((LANES, n_heads), lambda i: (0, 0))],
        out_shape=[SDS(dproj.shape, BF16), SDS((B_ROWS, d_model), F32),
                   SDS((n_heads, LANES, LANES), F32), SDS((LANES, n_heads), F32)],
        scratch_shapes=[pltpu.VMEM((tb, d_model), BF16), pltpu.VMEM((tb, d_model), F32)],
        input_output_aliases={0: 0},
        compiler_params=_params(1),
    )(dproj, proj, proj, proj, d_s, ln_g, ln_b, w_spatial, b_spatial_t)


A1_ROWS = 8


def _branch_a_bwd_norm(dproj, proj, h1, d_h3, gn_g, gn_b, seq):
    n_tok = proj.shape[0]
    d_model = gn_g.shape[1]
    ta = _tile(seq, ELEMENTWISE_ROWS, 16)

    def body(dproj_in, gt_ref, h1_ref, dh3_ref, gg_ref, gb_ref, dproj_ref, dh1_ref, vec_ref):
        del dproj_in

        @pl.when(pl.program_id(0) == 0)
        def _():
            vec_ref[...] = jnp.zeros_like(vec_ref)

        def group(g, carry):
            sl = pl.ds(pl.multiple_of(g * LANES, LANES), LANES)
            h1 = h1_ref[:, sl]
            mu = jnp.mean(h1, axis=-1, keepdims=True)
            dlt = h1 - mu
            rstd = lax.rsqrt(jnp.mean(dlt * dlt, axis=-1, keepdims=True) + LN_EPS)
            nrm = dlt * rstd
            sw, dsw = _silu_and_grad(nrm * gg_ref[:, sl] + gb_ref[:, sl])
            sg, dsg = _silu_and_grad(gt_ref[:, sl])
            dh3 = dh3_ref[:, sl]
            dgate = dh3 * sw * dsg
            dproj_ref[:, sl] = dgate.astype(BF16)
            vec_ref[0:1, sl] += _colsum(dgate)
            dh2 = dh3 * sg * dsw
            vec_ref[1:2, sl] += _colsum(dh2 * nrm)
            vec_ref[2:3, sl] += _colsum(dh2)
            dn = dh2 * gg_ref[:, sl]
            dh1 = rstd * (dn - jnp.mean(dn, axis=-1, keepdims=True)
                          - nrm * jnp.mean(dn * nrm, axis=-1, keepdims=True))
            vec_ref[3:4, sl] += _colsum(dh1)
            dh1_ref[:, sl] = dh1
            return carry

        lax.fori_loop(0, d_model // LANES, group, 0, unroll=True)

    tile = pl.BlockSpec((ta, d_model), lambda i: (i, 0))
    row = pl.BlockSpec((1, d_model), lambda i: (0, 0))
    return pl.pallas_call(
        body, name="branch_a_bwd_norm", grid=(n_tok // ta,),
        in_specs=[ANY, pl.BlockSpec((ta, d_model), lambda i: (i, 2)), tile, tile, row, row],
        out_specs=[pl.BlockSpec((ta, d_model), lambda i: (i, 2)), tile,
                   pl.BlockSpec((A1_ROWS, d_model), lambda i: (0, 0))],
        out_shape=[SDS(dproj.shape, BF16), SDS((n_tok, d_model), F32), SDS((A1_ROWS, d_model), F32)],
        input_output_aliases={0: 0},
        compiler_params=_params(1),
    )(dproj, proj, h1, d_h3, gn_g, gn_b)


A2_ROWS = 8


def _branch_a_bwd_conv(dproj, proj, d_h1, conv_w_full, gp_bf, gp_f32, seq):
    n_tok = proj.shape[0]
    d_model = conv_w_full.shape[1]
    n_groups = d_model // LANES
    ta = _tile(seq, ELEMENTWISE_ROWS, HALO)
    n_tiles = n_tok // ta
    per_seq = seq // ta
    rc = _conv_rows(ta)
    last_halo = n_tok // HALO - 1
    r8 = d_model // N_DEV
    prow = _tile(r8, 32, 16)

    def body(dproj_in, av_ref, ag_ref, avh_ref, agh_ref, dh1_ref, dh1h_ref, cw_ref, gp_bf_ref, gp_f32_ref,
             dproj_ref, vec_ref, dcw_ref, opa_ref, opb_ref, opo_ref,
             ext_h0, ext_d, rbuf, own, send_sems, recv_sems, local_sems):
        del dproj_in
        i = pl.program_id(0)
        x, y, c = _mesh_pos()
        me = _block_of((x, y, c))
        peers = _peers()

        def sends():
            return [pltpu.make_async_remote_copy(
                src_ref=gp_bf_ref.at[a, pl.ds(pl.multiple_of(blk * r8, 16), r8), :], dst_ref=rbuf.at[k, a],
                send_sem=send_sems.at[a, k], recv_sem=recv_sems.at[a, k], device_id=pos, device_id_type=MESH)
                for k, (pos, blk) in enumerate(peers) for a in range(3)]

        def own_rows():
            return [pltpu.make_async_copy(gp_f32_ref.at[a, pl.ds(pl.multiple_of(me * r8, 8), r8), :],
                                          own.at[a], local_sems.at[a]) for a in range(3)]

        @pl.when(i == 0)
        def _():
            vec_ref[...] = jnp.zeros_like(vec_ref)
            dcw_ref[...] = jnp.zeros_like(dcw_ref)
            for cp in sends() + own_rows():
                cp.start()

        keep_past = jnp.where(i % per_seq == 0, 0.0, 1.0)
        keep_next = jnp.where(i % per_seq == per_seq - 1, 0.0, 1.0)

        def group(g, carry):
            sl = pl.ds(pl.multiple_of(g * LANES, LANES), LANES)
            av = av_ref[:, sl]
            sig = _sigmoid(ag_ref[:, sl])
            ext_h0[0:HALO, :] = avh_ref[:, sl] * _sigmoid(agh_ref[:, sl]) * keep_past
            ext_h0[HALO:HALO + ta, :] = av * sig
            ext_d[0:ta, :] = dh1_ref[:, sl]
            ext_d[ta:ta + HALO, :] = dh1h_ref[:, sl] * keep_next
            for r0 in range(0, ta, rc):
                dh1 = ext_d[pl.ds(r0, rc), :]
                acc = jnp.zeros((rc, LANES), F32)
                for k in range(CONV_K):
                    acc = acc + ext_d[pl.ds(r0 + CONV_K - 1 - k, rc), :] * cw_ref[k:k + 1, sl]
                    prod = dh1 * ext_h0[pl.ds(r0 + HALO - (CONV_K - 1) + k, rc), :]
                    dcw_ref[g, k] += jnp.sum(prod.reshape(rc // SUBLANES, SUBLANES, LANES), axis=0)
                rows = pl.ds(r0, rc)
                sig_r = sig[r0:r0 + rc]
                dav = acc * sig_r
                dag = dav * av[r0:r0 + rc] * (1.0 - sig_r)
                dproj_ref[rows, sl] = dav.astype(BF16)
                dproj_ref[rows, pl.ds(pl.multiple_of(d_model + g * LANES, LANES), LANES)] = dag.astype(BF16)
                vec_ref[0:1, sl] += _colsum(dav)
                vec_ref[1:2, sl] += _colsum(dag)
            return carry

        lax.fori_loop(0, n_groups, group, 0, unroll=True)

        @pl.when(i == n_tiles - 1)
        def _():
            for cp in own_rows():
                cp.wait()
            for cp in sends():
                cp.wait_recv()
            for a, o in enumerate([opa_ref, opb_ref, opo_ref]):
                for q in range(r8 // prow):
                    r = pl.ds(q * prow, prow)
                    tot = own[a, r, :]
                    for k in range(N_DEV - 1):
                        tot = tot + rbuf[k, a, r, :].astype(F32)
                    o[r, :] = tot
            for cp in sends():
                cp.wait_send()

    blk = lambda j: pl.BlockSpec((ta, d_model), lambda i: (i, j))
    halo = lambda j: pl.BlockSpec((HALO, d_model), lambda i: (jnp.maximum(i * (ta // HALO) - 1, 0), j))
    shard = pl.BlockSpec((r8, d_model), lambda i: (0, 0))
    return pl.pallas_call(
        body, name="branch_a_bwd_conv", grid=(n_tiles,),
        in_specs=[ANY, blk(0), blk(1), halo(0), halo(1), pl.BlockSpec((ta, d_model), lambda i: (i, 0)),
                  pl.BlockSpec((HALO, d_model), lambda i: (jnp.minimum((i + 1) * (ta // HALO), last_halo), 0)),
                  pl.BlockSpec((HALO, d_model), lambda i: (0, 0)), ANY, ANY],
        out_specs=[pl.BlockSpec((ta, 2 * d_model), lambda i: (i, 0)),
                   pl.BlockSpec((A2_ROWS, d_model), lambda i: (0, 0)),
                   pl.BlockSpec((n_groups, HALO, SUBLANES, LANES), lambda i: (0, 0, 0, 0)),
                   shard, shard, shard],
        out_shape=[SDS(dproj.shape, BF16), SDS((A2_ROWS, d_model), F32),
                   SDS((n_groups, HALO, SUBLANES, LANES), F32)] + [SDS((r8, d_model), F32)] * 3,
        scratch_shapes=[pltpu.VMEM((HALO + ta, LANES), F32), pltpu.VMEM((ta + HALO, LANES), F32),
                        pltpu.VMEM((N_DEV - 1, 3, r8, d_model), BF16), pltpu.VMEM((3, r8, d_model), F32),
                        pltpu.SemaphoreType.DMA((3, 7)), pltpu.SemaphoreType.DMA((3, 7)),
                        pltpu.SemaphoreType.DMA((3,))],
        input_output_aliases={0: 0},
        compiler_params=_params(1),
    )(dproj, proj, proj, proj, proj, d_h1, d_h1, conv_w_full, gp_bf, gp_f32)


def _weight_grads(lhs3, rhs3):
    n_mat, n_tok, d_model = lhs3.shape
    tk = _tile(n_tok, 2048, 16)
    n_k = n_tok // tk

    def body(a_ref, g_ref, o_ref, ob_ref):
        part = _dot_ta(a_ref[...], g_ref[...])

        @pl.when(pl.program_id(1) == 0)
        def _():
            o_ref[...] = part

        @pl.when(pl.program_id(1) != 0)
        def _():
            o_ref[...] += part

        @pl.when(pl.program_id(1) == n_k - 1)
        def _():
            ob_ref[...] = o_ref[...].astype(BF16)

    tile = pl.BlockSpec((None, tk, d_model), lambda a, i: (a, i, 0))
    out = pl.BlockSpec((None, d_model, d_model), lambda a, i: (a, 0, 0))
    return pl.pallas_call(
        body, name="grad_w_pa_pb_o", grid=(n_mat, n_k), in_specs=[tile, tile], out_specs=[out, out],
        out_shape=[SDS((n_mat, d_model, d_model), F32), SDS((n_mat, d_model, d_model), BF16)],
        compiler_params=_params(2),
    )(lhs3, rhs3)


def _packed_rows(pieces, plan, d_model):
    rows = {"row": lambda a: d_model // LANES, "block": lambda a: a.shape[0],
            "tiles": lambda a: a.shape[0] * a.shape[1]}
    return sum(rows[kind](pieces[p]) for kind, p, _ in plan)


def _grad_w_in_reduce_scatter(xt_bf, dproj, blk_order, pieces, plan):
    n_tok = dproj.shape[0]
    d_model = xt_bf.shape[0]
    dh = d_model // 2
    n_units = 2 * N_DEV
    n_pc = len(pieces)
    n_packed = _packed_rows(pieces, plan, d_model)
    n_part = n_packed + -n_packed % (N_DEV * SUBLANES)
    rsl = n_part // N_DEV

    def body(ord_ref, a_ref, g_ref, *rest):
        pc_refs, rest = rest[:n_pc], rest[n_pc:]
        (o_ref, small_ref, acc, fb, sbuf, gbuf, tbuf, rfin, rbuf_s, red, p_ref,
         send_f, send_s, recv_g, recv_t, recv_f, out_sems, send1, recv1, send2, recv2) = rest
        del ord_ref
        u = pl.program_id(0)
        s = u // 2
        hf = u % 2
        rnd = s // 2
        x, y, c = _mesh_pos()
        sibling = (x, y, 1 - c)
        me = _block_of((x, y, c))
        peers = _peers()

        def rows_of(blk):
            return pl.ds(pl.multiple_of(blk * rsl, SUBLANES), rsl)

        def scatter():
            return [pltpu.make_async_remote_copy(
                src_ref=p_ref.at[rows_of(blk), :], dst_ref=rbuf_s.at[k], send_sem=send1.at[k],
                recv_sem=recv1.at[k], device_id=pos, device_id_type=MESH) for k, (pos, blk) in enumerate(peers)]

        def gather(dst_block=None):
            return [pltpu.make_async_remote_copy(
                src_ref=red, dst_ref=small_ref.at[rows_of(me if dst_block is None else blk), :],
                send_sem=send2.at[k], recv_sem=recv2.at[k], device_id=pos, device_id_type=MESH)
                for k, (pos, blk) in enumerate(peers)]

        def own_slice():
            return pltpu.make_async_copy(red, small_ref.at[rows_of(me), :], out_sems.at[2])

        def pack():
            at = 0
            for kind, p, r in plan:
                ref = pc_refs[p]
                if kind == "row":
                    for q in range(d_model // LANES):
                        p_ref[at + q:at + q + 1, :] = ref[r:r + 1, q * LANES:(q + 1) * LANES]
                    at += d_model // LANES
                elif kind == "block":
                    p_ref[at:at + ref.shape[0], :] = ref[...]
                    at += ref.shape[0]
                else:
                    for tile in range(ref.shape[0] * ref.shape[1]):
                        p_ref[at + tile:at + tile + 1, :] = _colsum(ref[tile // ref.shape[1], tile % ref.shape[1]])
                    at += ref.shape[0] * ref.shape[1]
            if at < n_part:
                p_ref[at:n_part, :] = jnp.zeros((n_part - at, LANES), F32)

        @pl.when(u == 0)
        def _():
            pack()
            for cp in scatter():
                cp.start()

        @pl.when(u == 8)
        def _():
            for cp in scatter():
                cp.wait_recv()
            tot = p_ref[rows_of(me), :]
            for k in range(N_DEV - 1):
                tot = tot + rbuf_s[k]
            red[...] = tot
            own_slice().start()
            for cp in gather():
                cp.start()

        n1 = (jnp.bitwise_xor(x, c), jnp.bitwise_xor(y, 1 - c), c)
        n2 = (jnp.bitwise_xor(x, 1 - c), jnp.bitwise_xor(y, c), c)

        def feed(r, half):
            return pltpu.make_async_remote_copy(
                src_ref=fb.at[half], dst_ref=gbuf.at[r, half], send_sem=send_f.at[r, half],
                recv_sem=recv_g.at[r, half], device_id=sibling, device_id_type=MESH)

        def feed_sibling(half):
            return pltpu.make_async_remote_copy(
                src_ref=fb.at[half], dst_ref=rfin.at[0, half], send_sem=send_f.at[3, half],
                recv_sem=recv_f.at[0, half], device_id=sibling, device_id_type=MESH)

        def chip_sum(r, half):
            dst = [tbuf.at[half], rfin.at[2, half], rfin.at[1, half]][r]
            sem = [recv_t.at[half], recv_f.at[2, half], recv_f.at[1, half]][r]
            return pltpu.make_async_remote_copy(
                src_ref=sbuf.at[r, half], dst_ref=dst, send_sem=send_s.at[r, half], recv_sem=sem,
                device_id=[n2, n2, n1][r], device_id_type=MESH)

        def out_copy(half):
            return pltpu.make_async_copy(acc.at[half], o_ref.at[:, pl.ds(half * dh, dh)], out_sems.at[half])

        def partial_sum():
            return _dot(a_ref[:, 0:n_tok], g_ref[...])

        for half in range(2):
            for r in range(3):
                @pl.when(u == 4 * r + 4 + half)
                def _(r=r, half=half):
                    feed(r, half).wait_send()

                @pl.when(u == 4 * r + 2 + half)
                def _(r=r, half=half):
                    feed(r, half).wait_recv()
                    if r == 2:
                        chip_sum(0, half).wait_recv()

            @pl.when(u == 14 + half)
            def _(half=half):
                feed_sibling(half).wait_recv()
                chip_sum(2, half).wait_recv()
                chip_sum(1, half).wait_recv()

        @pl.when(jnp.logical_and(s % 2 == 0, s < 7))
        def _():
            fb[hf] = partial_sum().astype(BF16)

        @pl.when(jnp.logical_or(s == 1, s == 3))
        def _():
            sbuf[rnd, hf] = (partial_sum() + gbuf[rnd, hf].astype(F32)).astype(BF16)

        @pl.when(s == 5)
        def _():
            sbuf[2, hf] = (partial_sum() + gbuf[2, hf].astype(F32) + tbuf[hf].astype(F32)).astype(BF16)

        @pl.when(s == 7)
        def _():
            acc[hf] = (partial_sum() + rfin[0, hf].astype(F32) + rfin[1, hf].astype(F32)
                       + rfin[2, hf].astype(F32))

        for half in range(2):
            for r in range(3):
                @pl.when(u == 4 * r + half)
                def _(r=r, half=half):
                    feed(r, half).start()

                @pl.when(u == 4 * r + 2 + half)
                def _(r=r, half=half):
                    chip_sum(r, half).start()

            @pl.when(u == 12 + half)
            def _(half=half):
                feed_sibling(half).start()

        @pl.when(u == 14)
        def _():
            out_copy(0).start()

        @pl.when(u == 15)
        def _():
            out_copy(1).start()
            for half in range(2):
                feed_sibling(half).wait_send()
                for r in range(3):
                    chip_sum(r, half).wait_send()
                out_copy(half).wait()
            for cp in gather("theirs"):
                cp.wait_recv()
            for cp in scatter() + gather():
                cp.wait_send()
            own_slice().wait()

    grid_spec = pltpu.PrefetchScalarGridSpec(
        num_scalar_prefetch=1, grid=(n_units,),
        in_specs=[VMEM, pl.BlockSpec((n_tok, dh), lambda u, o: (0, 2 * o[u // 2] + u % 2))] + [VMEM] * n_pc,
        out_specs=[ANY, ANY],
        scratch_shapes=[pltpu.VMEM((2, d_model, dh), F32), pltpu.VMEM((2, d_model, dh), BF16),
                        pltpu.VMEM((3, 2, d_model, dh), BF16), pltpu.VMEM((3, 2, d_model, dh), BF16),
                        pltpu.VMEM((2, d_model, dh), BF16), pltpu.VMEM((3, 2, d_model, dh), BF16),
                        pltpu.VMEM((N_DEV - 1, rsl, LANES), F32), pltpu.VMEM((rsl, LANES), F32),
                        pltpu.VMEM((n_part, LANES), F32),
                        pltpu.SemaphoreType.DMA((4, 2)), pltpu.SemaphoreType.DMA((3, 2)),
                        pltpu.SemaphoreType.DMA((3, 2)), pltpu.SemaphoreType.DMA((2,)),
                        pltpu.SemaphoreType.DMA((3, 2)),
                        pltpu.SemaphoreType.DMA((3,))] + [pltpu.SemaphoreType.DMA((N_DEV - 1,))] * 4)
    return pl.pallas_call(
        body, name="grad_w_in_reduce_scatter", grid_spec=grid_spec,
        out_shape=[SDS((d_model, d_model), F32), SDS((n_part, LANES), F32)], compiler_params=_params(1),
    )(blk_order, xt_bf, dproj, *pieces)


def _grad_x_adamw(dproj, w_all, dr, w, g, m, v, small, loss_rows, packed, groups):
    n_tok, d_model = dr.shape
    tm = _tile(n_tok, 256, 16)
    n_steps = n_tok // tm
    tr = w.shape[0] // n_steps
    n_pk, n_grp = len(packed), len(groups)
    n_out = 4 * (n_pk + n_grp)

    def body(dp_ref, w_hbm, dr_ref, ws_ref, g_ref, m_ref, v_ref, small_ref, *rest):
        pk_in, rest = rest[:3 * n_pk], rest[3 * n_pk:]
        grp_in, rest = rest[:4 * n_grp], rest[4 * n_grp:]
        o_ref, gp_ref, d_ref, mo_ref, vo_ref, loss_ref = rest[:6]
        outs, (w_ref, w_sems) = rest[6:6 + n_out], rest[6 + n_out:]
        i = pl.program_id(0)

        def fetch(j):
            return pltpu.make_async_copy(w_hbm.at[j], w_ref.at[j], w_sems.at[j])

        def grad_x_tile(first):
            acc = DEEPNORM_ALPHA * dr_ref[...]
            for j in range(N_DEV):
                if first:
                    fetch(j).wait()
                acc = acc + _dot_tb(dp_ref[:, j * d_model:(j + 1) * d_model], w_ref[j])
            o_ref[...] = acc

        @pl.when(i == 0)
        def _():
            for j in range(N_DEV):
                fetch(j).start()
            grad_x_tile(True)

        @pl.when(i != 0)
        def _():
            grad_x_tile(False)

        grad = g_ref[...]
        gp_ref[...] = grad
        d_ref[...], mo_ref[...], vo_ref[...] = _adamw_math(ws_ref[...], grad, m_ref[...], v_ref[...])

        @pl.when(i == n_steps - 1)
        def _():
            def update(p, pw, grad, pm, pv):
                og, od, om, ov = outs[4 * p:4 * p + 4]
                og[...] = grad
                od[...], om[...], ov[...] = _adamw_math(pw[...], grad, pm[...], pv[...])

            for p, (pw_arr, _, _, row0) in enumerate(packed):
                pw, pm, pv = pk_in[3 * p:3 * p + 3]
                update(p, pw, small_ref[row0:row0 + pw_arr.shape[0], :], pm, pv)
            for p in range(n_grp):
                pw, pg, pm, pv = grp_in[4 * p:4 * p + 4]
                update(n_pk + p, pw, pg[...], pm, pv)
            sq = small_ref[loss_rows[0]:loss_rows[0] + loss_rows[1], :]
            loss_ref[...] = jnp.sum(_colsum(sq), axis=1, keepdims=True) * (0.5 / d_model)

    tile = pl.BlockSpec((tm, d_model), lambda i: (i, 0))
    slab = pl.BlockSpec((tr, w.shape[1]), lambda i: (i, 0))
    flat = [a for pk in packed for a in pk[:3]] + [a for grp in groups for a in grp]
    shapes = [pk[0].shape for pk in packed] + [grp[0].shape for grp in groups]
    res = pl.pallas_call(
        body, name="grad_x_adamw", grid=(n_steps,),
        in_specs=[pl.BlockSpec((tm, N_DEV * d_model), lambda i: (i, 0)), ANY, tile, slab, slab, slab, slab]
        + [VMEM] * (1 + len(flat)),
        out_specs=[tile, slab, slab, slab, slab, VMEM] + [VMEM] * n_out,
        out_shape=[SDS((n_tok, d_model), F32)] + [SDS(w.shape, F32)] * 4 + [SDS((1, 1), F32)]
        + [SDS(shape, F32) for shape in shapes for _ in range(4)],
        scratch_shapes=[pltpu.VMEM(w_all.shape, BF16), pltpu.SemaphoreType.DMA((N_DEV,))],
        compiler_params=_params(1),
    )(dproj, w_all, dr, w, g, m, v, small, *flat)
    return (res[0], tuple(res[1:5]), [tuple(res[6 + 4 * p:10 + 4 * p]) for p in range(n_pk + n_grp)],
            res[5].reshape(()))


def _adamw_math(w, g, m, v):
    m = ADAM_B1 * m + (1.0 - ADAM_B1) * g
    v = ADAM_B2 * v + (1.0 - ADAM_B2) * (g * g)
    m_hat = m / (1.0 - ADAM_B1 ** ADAM_STEP)
    v_hat = v / (1.0 - ADAM_B2 ** ADAM_STEP)
    delta = -ADAM_LR * (m_hat / (jnp.sqrt(v_hat) + ADAM_EPS) + ADAM_WD * w)
    return delta, m, v


def _as_rows(a):
    return a.reshape(-1, LANES)


def kernel(x, w_in, b_in, conv_w, conv_b, gn_g, gn_b, ln_v_g, ln_v_b, w_spatial, b_spatial, w_pa, w_pb, w_o, b_o, ln_out_g, ln_out_b, loss_target, m_w_in, m_b_in, m_conv_w, m_conv_b, m_gn_g, m_gn_b, m_ln_v_g, m_ln_v_b, m_w_spatial, m_b_spatial, m_w_pa, m_w_pb, m_w_o, m_b_o, m_ln_out_g, m_ln_out_b, v_w_in, v_b_in, v_conv_w, v_conv_b, v_gn_g, v_gn_b, v_ln_v_g, v_ln_v_b, v_w_spatial, v_b_spatial, v_w_pa, v_w_pb, v_w_o, v_b_o, v_ln_out_g, v_ln_out_b):
    n_batch, seq, d_model = x.shape
    n_tok = n_batch * seq
    n_heads = d_model // LANES
    dc = conv_w.shape[1]
    me = 4 * lax.axis_index("x") + 2 * lax.axis_index("y") + lax.axis_index("c")
    row = lambda a: a.reshape(1, d_model)

    x2 = x.reshape(n_tok, d_model)
    target2 = loss_target.reshape(n_tok, d_model)
    b_spatial_t = b_spatial.T

    first = jnp.where(lax.axis_index("c") == 1, 4, 2)
    second = 6 - first
    ag_rel = jnp.stack([0 * first, 0 * first + 1, first, second + 1, second, first + 1, 0 * first + 6, 0 * first + 7])
    ag_blocks = jnp.bitwise_xor(me, ag_rel).astype(jnp.int32)
    proj, xt_bf, w_all, wp_all, cw_all = _proj_all_gather(
        x2, w_in, w_pa, w_pb, w_o, conv_w, b_in.reshape(N_DEV, 1, d_model), ag_blocks)
    wp_full = wp_all.reshape(3, d_model, d_model)
    conv_w_full = jnp.pad(cw_all.transpose(1, 0, 2).reshape(CONV_K, d_model), ((0, HALO - CONV_K), (0, 0)))

    h3, h1 = _branch_a_fwd(proj, conv_w_full, row(conv_b), row(gn_g), row(gn_b), seq)
    s = _branch_b_fwd(proj, row(ln_v_g), row(ln_v_b), w_spatial, b_spatial_t, seq)

    dproj, d_h3, d_s, dr, lhs3, rhs3, vec_mid = _mid(
        h3, s, proj, x2, target2, wp_full, row(b_o), row(ln_out_g), row(ln_out_b))

    dproj, vec_b, d_ws, d_bs_t = _branch_b_bwd(dproj, proj, d_s, row(ln_v_g), row(ln_v_b), w_spatial, b_spatial_t, seq)
    dproj, d_h1, vec_a1 = _branch_a_bwd_norm(dproj, proj, h1, d_h3, row(gn_g), row(gn_b), seq)
    gp_f32, gp_bf = _weight_grads(lhs3, rhs3)
    dproj, vec_a2, d_cw8, g_w_pa, g_w_pb, g_w_o = _branch_a_bwd_conv(
        dproj, proj, d_h1, conv_w_full, gp_bf, gp_f32, seq)

    pieces = [vec_a2, vec_a1, vec_b, vec_mid, d_bs_t.T, _as_rows(d_ws), d_cw8]
    a2, a1, vb, mid = 0, 1, 2, 3
    plan = ([("row", p, r) for p, r in [(a2, 0), (a2, 1), (a1, 0), (vb, 0), (vb, 1), (vb, 2), (mid, 3), (mid, 4)]]
            + [("row", p, r) for p, r in [(a1, 3), (a1, 1), (a1, 2), (vb, 3), (vb, 4), (mid, 2), (mid, 0), (mid, 1)]]
            + [("block", 4, None), ("block", 5, None), ("tiles", 6, None), ("row", mid, 5)])

    rs_rel = jnp.stack([0 * first + 7, 0 * first + 6, first + 1, second, second + 1, first, 0 * first + 1, 0 * first])
    rs_blocks = jnp.bitwise_xor(me, rs_rel).astype(jnp.int32)
    g_w_in, small = _grad_w_in_reduce_scatter(xt_bf, dproj, rs_blocks, pieces, plan)

    g_rows = d_model // LANES
    o0 = N_DEV * g_rows
    o1 = o0 + 8 * g_rows
    o2 = o1 + n_heads
    o3 = o2 + n_heads * LANES
    o4 = o3 + n_heads * HALO
    g_cw_full = small[o3:o4].reshape(n_heads, HALO, LANES).transpose(1, 0, 2).reshape(HALO, d_model)
    g_conv_w = lax.dynamic_slice(g_cw_full, (0, me * dc), (CONV_K, dc))

    two_d = lambda a: a.reshape(-1, a.shape[-1]) if a.ndim != 1 else (
        a.reshape(-1, LANES) if a.shape[0] % LANES == 0 else a.reshape(1, -1))
    names = ["b_in", "conv_w", "conv_b", "gn_g", "gn_b", "ln_v_g", "ln_v_b", "w_spatial", "b_spatial",
             "w_pa", "w_pb", "w_o", "b_o", "ln_out_g", "ln_out_b"]
    ws = dict(b_in=b_in, conv_w=conv_w, conv_b=conv_b, gn_g=gn_g, gn_b=gn_b, ln_v_g=ln_v_g, ln_v_b=ln_v_b,
              w_spatial=w_spatial, b_spatial=b_spatial, w_pa=w_pa, w_pb=w_pb, w_o=w_o, b_o=b_o,
              ln_out_g=ln_out_g, ln_out_b=ln_out_b)
    ms = dict(b_in=m_b_in, conv_w=m_conv_w, conv_b=m_conv_b, gn_g=m_gn_g, gn_b=m_gn_b, ln_v_g=m_ln_v_g,
              ln_v_b=m_ln_v_b, w_spatial=m_w_spatial, b_spatial=m_b_spatial, w_pa=m_w_pa, w_pb=m_w_pb,
              w_o=m_w_o, b_o=m_b_o, ln_out_g=m_ln_out_g, ln_out_b=m_ln_out_b)
    vs = dict(b_in=v_b_in, conv_w=v_conv_w, conv_b=v_conv_b, gn_g=v_gn_g, gn_b=v_gn_b, ln_v_g=v_ln_v_g,
              ln_v_b=v_ln_v_b, w_spatial=v_w_spatial, b_spatial=v_b_spatial, w_pa=v_w_pa, w_pb=v_w_pb,
              w_o=v_w_o, b_o=v_b_o, ln_out_g=v_ln_out_g, ln_out_b=v_ln_out_b)
    vec_names = ["conv_b", "gn_g", "gn_b", "ln_v_g", "ln_v_b", "b_o", "ln_out_g", "ln_out_b"]
    first_row = dict(b_in=0, b_spatial=o1, w_spatial=o2, **{n: o0 + a * g_rows for a, n in enumerate(vec_names)})
    given = dict(conv_w=g_conv_w, w_pa=g_w_pa, w_pb=g_w_pb, w_o=g_w_o)
    grad_x, upd_w_in, upd, loss = _grad_x_adamw(
        dproj, w_all, dr, w_in, g_w_in, m_w_in, v_w_in, small, (o4, g_rows),
        [(two_d(ws[n]), two_d(ms[n]), two_d(vs[n]), row0) for n, row0 in first_row.items()],
        [tuple(two_d(a) for a in (ws[n], grad, ms[n], vs[n])) for n, grad in given.items()])
    grad_x = grad_x.reshape(x.shape)
    res = {n: tuple(a.reshape(ws[n].shape) for a in u) for n, u in zip([*first_row, *given], upd)}
    res["w_in"] = upd_w_in

    order = ["w_in"] + names
    return (loss, grad_x, *[res[n][k] for k in range(4) for n in order])
```

```python
import jax
import jax.numpy as jnp
from jax import lax
from jax.experimental import pallas as pl
from jax.experimental.pallas import tpu as pltpu

F32 = jnp.float32
BF16 = jnp.bfloat16
SDS = jax.ShapeDtypeStruct

N_DEV = 8
LANES = 128
SUBLANES = 8
CONV_K = 31
HALO = 32
ELEMENTWISE_ROWS = 512
LN_EPS = 1e-5
DEEPNORM_ALPHA = 2.0 ** 0.25
ADAM_LR, ADAM_B1, ADAM_B2, ADAM_EPS, ADAM_WD, ADAM_STEP = 0.001, 0.9, 0.999, 1e-08, 0.01, 10
GELU_C = 0.7978845608028654
GELU_A = 0.044715
VMEM_LIMIT = 56 * 1024 * 1024
MESH = pl.DeviceIdType.MESH
ANY = pl.BlockSpec(memory_space=pl.ANY)
VMEM = pl.BlockSpec(memory_space=pltpu.VMEM)


def _params(n_grid=0):
    sem = ("arbitrary",) * n_grid if n_grid else None
    return pltpu.CompilerParams(dimension_semantics=sem, vmem_limit_bytes=VMEM_LIMIT)


def _tile(n, pref, mult):
    t = min(n, pref)
    while n % t or t % mult:
        t -= 1
    return t


def _colsum(v):
    return jnp.sum(v, axis=0, keepdims=True)


def _sigmoid(v):
    return jax.nn.sigmoid(v)


def _silu_and_grad(v):
    s = _sigmoid(v)
    val = v * s
    return val, s + val * (1.0 - s)


def _gelu_and_grad(v):
    v2 = v * v
    sg = _sigmoid(v * (2.0 * GELU_C + (2.0 * GELU_C * GELU_A) * v2))
    grad = sg + v * sg * (1.0 - sg) * (2.0 * GELU_C + (6.0 * GELU_C * GELU_A) * v2)
    return v * sg, grad


def _tril_mask():
    r = lax.broadcasted_iota(jnp.int32, (LANES, LANES), 0)
    c = lax.broadcasted_iota(jnp.int32, (LANES, LANES), 1)
    return c <= r


def _dot(a, b):
    return jnp.dot(a, b, preferred_element_type=F32)


def _dot_tb(a, b):
    return lax.dot_general(a, b, (((1,), (1,)), ((), ())), preferred_element_type=F32)


def _dot_ta(a, b):
    return lax.dot_general(a, b, (((0,), (0,)), ((), ())), preferred_element_type=F32)


def _mesh_pos():
    return lax.axis_index("x"), lax.axis_index("y"), lax.axis_index("c")


def _block_of(pos):
    return 4 * pos[0] + 2 * pos[1] + pos[2]


def _peers():
    x, y, c = _mesh_pos()
    out = []
    for k in range(1, N_DEV):
        pos = (1 - x if k & 4 else x, 1 - y if k & 2 else y, 1 - c if k & 1 else c)
        out.append((pos, _block_of(pos)))
    return out


def _proj_all_gather(x2, w_in, w_pa, w_pb, w_o, conv_w, b_in3, blk_order):
    n_tok, d_model = x2.shape
    r8 = w_pa.shape[0]
    kc, dc = conv_w.shape
    tm = _tile(n_tok, 1024, LANES)
    n_t = n_tok // tm
    n_arr = 3

    def body(ord_ref, x_ref, b_ref, win_ref, wpa_ref, wpb_ref, wo_ref, cw_ref,
             proj_ref, xt_ref, wall_ref, wp_ref, cwall_ref,
             wbuf, xbuf, st_p, send_sems, recv_sems, local_sems, wall_sems):
        s = pl.program_id(0)
        t = pl.program_id(1)
        x, y, c = _mesh_pos()
        me = (x, y, c)
        sibling = (x, y, 1 - c)
        n1 = (jnp.bitwise_xor(x, c), jnp.bitwise_xor(y, 1 - c))
        n2 = (jnp.bitwise_xor(x, 1 - c), jnp.bitwise_xor(y, c))
        dg = (1 - x, 1 - y)
        outs = [wbuf, wp_ref, cwall_ref]
        srcs = [None, st_p, cw_ref]
        consumed = [me, sibling, (*n1, c), (*n2, 1 - c), (*n2, c), (*n1, 1 - c), (*dg, c), (*dg, 1 - c)]
        leaves = [(me, sibling), (me, (*n1, c)), (me, (*n2, c)), ((*n1, c), (*n2, c)),
                  ((*n1, c), sibling), ((*n2, c), sibling), ((*dg, c), sibling)]
        lands = [sibling, (*n1, c), (*n2, c), (*dg, c), (*n2, 1 - c), (*n1, 1 - c), (*dg, 1 - c)]

        def slot(o, pos):
            return o.at[:, _block_of(pos)] if o is wp_ref else o.at[_block_of(pos)]

        def copy(a, k, block, to, src=None):
            o = outs[a]
            return pltpu.make_async_remote_copy(
                src_ref=slot(o, block) if src is None else src, dst_ref=slot(o, block),
                send_sem=send_sems.at[a, k], recv_sem=recv_sems.at[a, k],
                device_id=to, device_id_type=MESH)

        def send(a, k):
            block, to = leaves[k]
            return copy(a, k, block, to, src=srcs[a] if k < 3 else None)

        def recv(a, k):
            return copy(a, k, lands[k], me)

        def local_copies():
            return [pltpu.make_async_copy(srcs[a], slot(outs[a], me), local_sems.at[a]) for a in (1, 2)]

        def to_hbm(step):
            return pltpu.make_async_copy(slot(wbuf, consumed[step]), slot(wall_ref, consumed[step]),
                                         wall_sems.at[step])

        def at_step(step):
            return pl.when(jnp.logical_and(s == step, t == 0))

        @at_step(0)
        def _():
            slot(wbuf, me)[...] = win_ref[...].astype(BF16)
            st_p[0] = wpa_ref[...].astype(BF16)
            st_p[1] = wpb_ref[...].astype(BF16)
            st_p[2] = wo_ref[...].astype(BF16)
            for a in range(n_arr):
                send(a, 0).start()
                send(a, 1).start()
            for cp in local_copies():
                cp.start()
            to_hbm(0).start()

        @at_step(1)
        def _():
            recv(0, 0).wait_recv()
            to_hbm(1).start()

        for rnd in range(3):
            @at_step(2 + 2 * rnd)
            def _(rnd=rnd):
                if rnd == 0:
                    for a in range(n_arr):
                        send(a, 2).start()
                recv(0, 1 + rnd).wait_recv()
                if rnd == 0:
                    send(0, 3).start()
                send(0, 4 + rnd).start()
                to_hbm(2 + 2 * rnd).start()

            @at_step(3 + 2 * rnd)
            def _(rnd=rnd):
                for a in (1, 2):
                    recv(a, 1 + rnd).wait_recv()
                    if rnd == 0:
                        send(a, 3).start()
                    send(a, 4 + rnd).start()
                recv(0, 4 + rnd).wait_recv()
                to_hbm(3 + 2 * rnd).start()

        rows = pl.ds(pl.multiple_of(t * tm, tm), tm)

        @pl.when(s == 0)
        def _():
            xb = x_ref[...].astype(BF16)
            xbuf[rows, :] = xb
            xt_ref[...] = xb.T

        proj_ref[...] = _dot(xbuf[rows, :], wbuf[ord_ref[s]]) + b_ref[...]

        @pl.when(jnp.logical_and(s == N_DEV - 1, t == n_t - 1))
        def _():
            for a in (1, 2):
                for k in (0, 4, 5, 6):
                    recv(a, k).wait_recv()
            for a in range(n_arr):
                for k in range(7):
                    send(a, k).wait_send()
            for cp in local_copies() + [to_hbm(step) for step in range(N_DEV)]:
                cp.wait()

    grid_spec = pltpu.PrefetchScalarGridSpec(
        num_scalar_prefetch=1, grid=(N_DEV, n_t),
        in_specs=[pl.BlockSpec((tm, d_model), lambda s, t, o: (jnp.where(s == 0, t, n_t - 1), 0)),
                  pl.BlockSpec((None, 1, d_model), lambda s, t, o: (o[s], 0, 0)),
                  VMEM, VMEM, VMEM, VMEM, VMEM],
        out_specs=[pl.BlockSpec((tm, d_model), lambda s, t, o: (t, o[s])),
                   pl.BlockSpec((d_model, tm), lambda s, t, o: (0, jnp.where(s == 0, t, n_t))),
                   ANY, ANY, ANY],
        scratch_shapes=[pltpu.VMEM((N_DEV, d_model, d_model), BF16), pltpu.VMEM((n_tok, d_model), BF16),
                        pltpu.VMEM((3, r8, d_model), BF16),
                        pltpu.SemaphoreType.DMA((n_arr, 7)), pltpu.SemaphoreType.DMA((n_arr, 7)),
                        pltpu.SemaphoreType.DMA((3,)), pltpu.SemaphoreType.DMA((N_DEV,))])
    return pl.pallas_call(
        body, name="proj_all_gather", grid_spec=grid_spec,
        out_shape=[SDS((n_tok, N_DEV * d_model), F32), SDS((d_model, (n_t + 1) * tm), BF16),
                   SDS((N_DEV, d_model, d_model), BF16), SDS((3, N_DEV, r8, d_model), BF16),
                   SDS((N_DEV, kc, dc), F32)],
        compiler_params=_params(2),
    )(blk_order, x2, b_in3, w_in, w_pa, w_pb, w_o, conv_w)


def _conv_rows(ta):
    return _tile(ta, 64, SUBLANES)


def _branch_a_fwd(proj, conv_w_full, conv_b, gn_g, gn_b, seq):
    n_tok = proj.shape[0]
    d_model = conv_b.shape[1]
    ta = _tile(seq, ELEMENTWISE_ROWS, HALO)
    per_seq = seq // ta
    rc = _conv_rows(ta)

    def body(av_ref, ag_ref, gt_ref, avh_ref, agh_ref, cw_ref, cb_ref, gg_ref, gb_ref,
             h3_ref, h1_ref, ext):
        keep = jnp.where(pl.program_id(0) % per_seq == 0, 0.0, 1.0)

        def group(g, carry):
            sl = pl.ds(pl.multiple_of(g * LANES, LANES), LANES)
            ext[0:HALO, :] = avh_ref[:, sl] * _sigmoid(agh_ref[:, sl]) * keep
            ext[HALO:HALO + ta, :] = av_ref[:, sl] * _sigmoid(ag_ref[:, sl])
            for r0 in range(0, ta, rc):
                acc = jnp.broadcast_to(cb_ref[:, sl], (rc, LANES))
                for k in range(CONV_K):
                    acc = acc + ext[pl.ds(r0 + HALO - (CONV_K - 1) + k, rc), :] * cw_ref[k:k + 1, sl]
                h1_ref[pl.ds(r0, rc), sl] = acc
            h1 = h1_ref[:, sl]
            mu = jnp.mean(h1, axis=-1, keepdims=True)
            dlt = h1 - mu
            var = jnp.mean(dlt * dlt, axis=-1, keepdims=True)
            h2 = dlt * lax.rsqrt(var + LN_EPS) * gg_ref[:, sl] + gb_ref[:, sl]
            gate = gt_ref[:, sl]
            h3_ref[:, sl] = (h2 * _sigmoid(h2) * gate * _sigmoid(gate)).astype(BF16)
            return carry

        lax.fori_loop(0, d_model // LANES, group, 0, unroll=True)

    blk = lambda j: pl.BlockSpec((ta, d_model), lambda i: (i, j))
    halo = lambda j: pl.BlockSpec((HALO, d_model), lambda i: (jnp.maximum(i * (ta // HALO) - 1, 0), j))
    row = pl.BlockSpec((1, d_model), lambda i: (0, 0))
    return pl.pallas_call(
        body, name="branch_a_fwd", grid=(n_tok // ta,),
        in_specs=[blk(0), blk(1), blk(2), halo(0), halo(1),
                  pl.BlockSpec((HALO, d_model), lambda i: (0, 0)), row, row, row],
        out_specs=[pl.BlockSpec((ta, d_model), lambda i: (i, 0))] * 2,
        out_shape=[SDS((n_tok, d_model), BF16), SDS((n_tok, d_model), F32)],
        scratch_shapes=[pltpu.VMEM((HALO + ta, LANES), F32)],
        compiler_params=_params(1),
    )(proj, proj, proj, proj, proj, conv_w_full, conv_b, gn_g, gn_b)


def _branch_b_fwd(proj, ln_g, ln_b, w_spatial, b_spatial_t, seq):
    n_tok = proj.shape[0]
    d_model = ln_g.shape[1]
    n_heads = d_model // LANES
    tb = _tile(seq, ELEMENTWISE_ROWS, LANES)

    def body(u_ref, v_ref, bg_ref, lg_ref, lb_ref, ws_ref, bs_ref, s_ref, vn_buf):
        v, _ = _gelu_and_grad(v_ref[...])
        mu = jnp.mean(v, axis=-1, keepdims=True)
        dlt = v - mu
        var = jnp.mean(dlt * dlt, axis=-1, keepdims=True)
        vn_buf[...] = (dlt * lax.rsqrt(var + LN_EPS) * lg_ref[...] + lb_ref[...]).astype(BF16)
        tril = _tril_mask()
        for h in range(n_heads):
            cols = slice(h * LANES, (h + 1) * LANES)
            w_h = jnp.where(tril, ws_ref[h], 0.0).astype(BF16)
            bias = bs_ref[:, h:h + 1]
            for ch in range(tb // LANES):
                rows = slice(ch * LANES, (ch + 1) * LANES)
                mix = _dot(w_h, vn_buf[rows, cols]) + bias
                u, _ = _gelu_and_grad(u_ref[rows, cols])
                gate = bg_ref[rows, cols]
                s_ref[rows, cols] = (u * mix * gate * _sigmoid(gate)).astype(BF16)

    blk = lambda j: pl.BlockSpec((tb, d_model), lambda i: (i, j))
    row = pl.BlockSpec((1, d_model), lambda i: (0, 0))
    return pl.pallas_call(
        body, name="branch_b_fwd", grid=(n_tok // tb,),
        in_specs=[blk(3), blk(4), blk(5), row, row,
                  pl.BlockSpec((n_heads, LANES, LANES), lambda i: (0, 0, 0)),
                  pl.BlockSpec((LANES, n_heads), lambda i: (0, 0))],
        out_specs=pl.BlockSpec((tb, d_model), lambda i: (i, 0)),
        out_shape=SDS((n_tok, d_model), BF16),
        scratch_shapes=[pltpu.VMEM((tb, d_model), BF16)],
        compiler_params=_params(1),
    )(proj, proj, proj, ln_g, ln_b, w_spatial, b_spatial_t)


MID_ROWS = 8


def _mid(h3, s, proj, x2, target, wp_full, b_o, lo_g, lo_b):
    n_tok, d_model = x2.shape
    tm = _tile(n_tok, 256, 16)

    def body(h3_ref, s_ref, ma_ref, mb_ref, x_ref, t_ref, wpa_ref, wpb_ref, wo_ref, bo_ref,
             lg_ref, lb_ref, dproj_ref, dh3_ref, ds_ref, dr_ref, lhs3_ref, rhs3_ref, vec_ref):
        @pl.when(pl.program_id(0) == 0)
        def _():
            vec_ref[...] = jnp.zeros_like(vec_ref)

        h3 = h3_ref[...]
        s = s_ref[...]
        ya = _dot(h3, wpa_ref[...])
        yb = _dot(s, wpb_ref[...])
        ga = _sigmoid(ma_ref[...])
        gb = _sigmoid(mb_ref[...])
        mixed = (ga * ya + gb * yb).astype(BF16)
        lhs3_ref[0] = h3
        lhs3_ref[1] = s
        lhs3_ref[2] = mixed
        r = DEEPNORM_ALPHA * x_ref[...] + _dot(mixed, wo_ref[...]) + bo_ref[...]
        mu = jnp.mean(r, axis=-1, keepdims=True)
        dlt = r - mu
        rstd = lax.rsqrt(jnp.mean(dlt * dlt, axis=-1, keepdims=True) + LN_EPS)
        rhat = dlt * rstd
        diff = rhat * lg_ref[...] + lb_ref[...] - t_ref[...]
        dy = diff * (1.0 / d_model)
        vec_ref[0:1, :] += _colsum(dy * rhat)
        vec_ref[1:2, :] += _colsum(dy)
        vec_ref[5:6, :] += _colsum(diff * diff)
        drh = dy * lg_ref[...]
        dr = rstd * (drh - jnp.mean(drh, axis=-1, keepdims=True)
                     - rhat * jnp.mean(drh * rhat, axis=-1, keepdims=True))
        vec_ref[2:3, :] += _colsum(dr)
        dr_ref[...] = dr
        drb = dr.astype(BF16)
        rhs3_ref[2] = drb
        dmixed = _dot_tb(drb, wo_ref[...])
        dya_f = dmixed * ga
        dyb_f = dmixed * gb
        dma = dya_f * ya * (1.0 - ga)
        dmb = dyb_f * yb * (1.0 - gb)
        vec_ref[3:4, :] += _colsum(dma)
        vec_ref[4:5, :] += _colsum(dmb)
        dproj_ref[:, 0:d_model] = dma.astype(BF16)
        dproj_ref[:, d_model:2 * d_model] = dmb.astype(BF16)
        dya = dya_f.astype(BF16)
        dyb = dyb_f.astype(BF16)
        rhs3_ref[0] = dya
        rhs3_ref[1] = dyb
        dh3_ref[...] = _dot_tb(dya, wpa_ref[...])
        ds_ref[...] = _dot_tb(dyb, wpb_ref[...])

    tile = pl.BlockSpec((tm, d_model), lambda i: (i, 0))
    full = lambda a: pl.BlockSpec((None, d_model, d_model), lambda i: (a, 0, 0))
    row = pl.BlockSpec((1, d_model), lambda i: (0, 0))
    stack = pl.BlockSpec((3, tm, d_model), lambda i: (0, i, 0))
    bf3 = SDS((3, n_tok, d_model), BF16)
    f32 = SDS((n_tok, d_model), F32)
    return pl.pallas_call(
        body, name="mid", grid=(n_tok // tm,),
        in_specs=[tile, tile, pl.BlockSpec((tm, d_model), lambda i: (i, 6)),
                  pl.BlockSpec((tm, d_model), lambda i: (i, 7)), tile, tile, full(0), full(1), full(2),
                  row, row, row],
        out_specs=[pl.BlockSpec((tm, 2 * d_model), lambda i: (i, 3)), tile, tile, tile, stack, stack,
                   pl.BlockSpec((MID_ROWS, d_model), lambda i: (0, 0))],
        out_shape=[SDS((n_tok, N_DEV * d_model), BF16), f32, f32, f32, bf3, bf3,
                   SDS((MID_ROWS, d_model), F32)],
        compiler_params=_params(1),
    )(h3, s, proj, proj, x2, target, wp_full, wp_full, wp_full, b_o, lo_g, lo_b)


B_ROWS = 8


def _branch_b_bwd(dproj, proj, d_s, ln_g, ln_b, w_spatial, b_spatial_t, seq):
    n_tok = proj.shape[0]
    d_model = ln_g.shape[1]
    n_heads = d_model // LANES
    tb = _tile(seq, ELEMENTWISE_ROWS, LANES)

    def body(dproj_in, u_ref, v_ref, bg_ref, ds_ref, lg_ref, lb_ref, ws_ref, bs_ref,
             dproj_ref, vec_ref, dws_ref, dbs_ref, vn_buf, dv_buf):
        del dproj_in

        @pl.when(pl.program_id(0) == 0)
        def _():
            vec_ref[...] = jnp.zeros_like(vec_ref)
            dws_ref[...] = jnp.zeros_like(dws_ref)
            dbs_ref[...] = jnp.zeros_like(dbs_ref)

        v, dgelu_v = _gelu_and_grad(v_ref[...])
        mu = jnp.mean(v, axis=-1, keepdims=True)
        dlt = v - mu
        rstd = lax.rsqrt(jnp.mean(dlt * dlt, axis=-1, keepdims=True) + LN_EPS)
        vhat = dlt * rstd
        vn_buf[...] = (vhat * lg_ref[...] + lb_ref[...]).astype(BF16)
        tril = _tril_mask()
        for h in range(n_heads):
            cols = slice(h * LANES, (h + 1) * LANES)
            w_h = jnp.where(tril, ws_ref[h], 0.0).astype(BF16)
            bias = bs_ref[:, h:h + 1]
            for ch in range(tb // LANES):
                rows = slice(ch * LANES, (ch + 1) * LANES)
                vn = vn_buf[rows, cols]
                mix = _dot(w_h, vn) + bias
                u, dgelu_u = _gelu_and_grad(u_ref[rows, cols])
                sg, dsilu = _silu_and_grad(bg_ref[rows, cols])
                dsv = ds_ref[rows, cols]
                du = dsv * mix * sg * dgelu_u
                dbg = dsv * u * mix * dsilu
                dmix = dsv * u * sg
                dmix_bf = dmix.astype(BF16)
                dproj_ref[rows, cols] = du.astype(BF16)
                dproj_ref[rows, 2 * d_model + h * LANES:2 * d_model + (h + 1) * LANES] = dbg.astype(BF16)
                vec_ref[0:1, cols] += _colsum(du)
                vec_ref[2:3, cols] += _colsum(dbg)
                dbs_ref[:, h:h + 1] += jnp.sum(dmix, axis=1, keepdims=True)
                dws_ref[h] += jnp.where(tril, _dot_tb(dmix_bf, vn), 0.0)
                dv_buf[rows, cols] = _dot_ta(w_h, dmix_bf)
        dvn = dv_buf[...]
        vec_ref[3:4, :] += _colsum(dvn * vhat)
        vec_ref[4:5, :] += _colsum(dvn)
        dvh = dvn * lg_ref[...]
        dv = rstd * (dvh - jnp.mean(dvh, axis=-1, keepdims=True)
                     - vhat * jnp.mean(dvh * vhat, axis=-1, keepdims=True)) * dgelu_v
        vec_ref[1:2, :] += _colsum(dv)
        dproj_ref[:, d_model:2 * d_model] = dv.astype(BF16)

    blk = lambda j: pl.BlockSpec((tb, d_model), lambda i: (i, j))
    row = pl.BlockSpec((1, d_model), lambda i: (0, 0))
    return pl.pallas_call(
        body, name="branch_b_bwd", grid=(n_tok // tb,),
        in_specs=[ANY, blk(3), blk(4), blk(5), pl.BlockSpec((tb, d_model), lambda i: (i, 0)), row, row,
                  pl.BlockSpec((n_heads, LANES, LANES), lambda i: (0, 0, 0)),
                  pl.BlockSpec((LANES, n_heads), lambda i: (0, 0))],
        out_specs=[pl.BlockSpec((tb, 3 * d_model), lambda i: (i, 1)),
                   pl.BlockSpec((B_ROWS, d_model), lambda i: (0, 0)),
                   pl.BlockSpec((n_heads, LANES, LANES), lambda i: (0, 0, 0)),
                   pl.BlockSpec((LANES, n_heads), lambda i: (0, 0))],
        out_shape=[SDS(dproj.shape, BF16), SDS((B_ROWS, d_model), F32),
                   SDS((n_heads, LANES, LANES), F32), SDS((LANES, n_heads), F32)],
        scratch_shapes=[pltpu.VMEM((tb, d_model), BF16), pltpu.VMEM((tb, d_model), F32)],
        input_output_aliases={0: 0},
        compiler_params=_params(1),
    )(dproj, proj, proj, proj, d_s, ln_g, ln_b, w_spatial, b_spatial_t)


A1_ROWS = 8


def _branch_a_bwd_norm(dproj, proj, h1, d_h3, gn_g, gn_b, seq):
    n_tok = proj.shape[0]
    d_model = gn_g.shape[1]
    ta = _tile(seq, ELEMENTWISE_ROWS, 16)

    def body(dproj_in, gt_ref, h1_ref, dh3_ref, gg_ref, gb_ref, dproj_ref, dh1_ref, vec_ref):
        del dproj_in

        @pl.when(pl.program_id(0) == 0)
        def _():
            vec_ref[...] = jnp.zeros_like(vec_ref)

        def group(g, carry):
            sl = pl.ds(pl.multiple_of(g * LANES, LANES), LANES)
            h1 = h1_ref[:, sl]
            mu = jnp.mean(h1, axis=-1, keepdims=True)
            dlt = h1 - mu
            rstd = lax.rsqrt(jnp.mean(dlt * dlt, axis=-1, keepdims=True) + LN_EPS)
            nrm = dlt * rstd
            sw, dsw = _silu_and_grad(nrm * gg_ref[:, sl] + gb_ref[:, sl])
            sg, dsg = _silu_and_grad(gt_ref[:, sl])
            dh3 = dh3_ref[:, sl]
            dgate = dh3 * sw * dsg
            dproj_ref[:, sl] = dgate.astype(BF16)
            vec_ref[0:1, sl] += _colsum(dgate)
            dh2 = dh3 * sg * dsw
            vec_ref[1:2, sl] += _colsum(dh2 * nrm)
            vec_ref[2:3, sl] += _colsum(dh2)
            dn = dh2 * gg_ref[:, sl]
            dh1 = rstd * (dn - jnp.mean(dn, axis=-1, keepdims=True)
                          - nrm * jnp.mean(dn * nrm, axis=-1, keepdims=True))
            vec_ref[3:4, sl] += _colsum(dh1)
            dh1_ref[:, sl] = dh1
            return carry

        lax.fori_loop(0, d_model // LANES, group, 0, unroll=True)

    tile = pl.BlockSpec((ta, d_model), lambda i: (i, 0))
    row = pl.BlockSpec((1, d_model), lambda i: (0, 0))
    return pl.pallas_call(
        body, name="branch_a_bwd_norm", grid=(n_tok // ta,),
        in_specs=[ANY, pl.BlockSpec((ta, d_model), lambda i: (i, 2)), tile, tile, row, row],
        out_specs=[pl.BlockSpec((ta, d_model), lambda i: (i, 2)), tile,
                   pl.BlockSpec((A1_ROWS, d_model), lambda i: (0, 0))],
        out_shape=[SDS(dproj.shape, BF16), SDS((n_tok, d_model), F32), SDS((A1_ROWS, d_model), F32)],
        input_output_aliases={0: 0},
        compiler_params=_params(1),
    )(dproj, proj, h1, d_h3, gn_g, gn_b)


A2_ROWS = 8


def _branch_a_bwd_conv(dproj, proj, d_h1, conv_w_full, gp_bf, gp_f32, seq):
    n_tok = proj.shape[0]
    d_model = conv_w_full.shape[1]
    n_groups = d_model // LANES
    ta = _tile(seq, ELEMENTWISE_ROWS, HALO)
    n_tiles = n_tok // ta
    per_seq = seq // ta
    rc = _conv_rows(ta)
    last_halo = n_tok // HALO - 1
    r8 = d_model // N_DEV
    prow = _tile(r8, 32, 16)

    def body(dproj_in, av_ref, ag_ref, avh_ref, agh_ref, dh1_ref, dh1h_ref, cw_ref, gp_bf_ref, gp_f32_ref,
             dproj_ref, vec_ref, dcw_ref, opa_ref, opb_ref, opo_ref,
             ext_h0, ext_d, rbuf, own, send_sems, recv_sems, local_sems):
        del dproj_in
        i = pl.program_id(0)
        x, y, c = _mesh_pos()
        me = _block_of((x, y, c))
        peers = _peers()

        def sends():
            return [pltpu.make_async_remote_copy(
                src_ref=gp_bf_ref.at[a, pl.ds(pl.multiple_of(blk * r8, 16), r8), :], dst_ref=rbuf.at[k, a],
                send_sem=send_sems.at[a, k], recv_sem=recv_sems.at[a, k], device_id=pos, device_id_type=MESH)
                for k, (pos, blk) in enumerate(peers) for a in range(3)]

        def own_rows():
            return [pltpu.make_async_copy(gp_f32_ref.at[a, pl.ds(pl.multiple_of(me * r8, 8), r8), :],
                                          own.at[a], local_sems.at[a]) for a in range(3)]

        @pl.when(i == 0)
        def _():
            vec_ref[...] = jnp.zeros_like(vec_ref)
            dcw_ref[...] = jnp.zeros_like(dcw_ref)
            for cp in sends() + own_rows():
                cp.start()

        keep_past = jnp.where(i % per_seq == 0, 0.0, 1.0)
        keep_next = jnp.where(i % per_seq == per_seq - 1, 0.0, 1.0)

        def group(g, carry):
            sl = pl.ds(pl.multiple_of(g * LANES, LANES), LANES)
            av = av_ref[:, sl]
            sig = _sigmoid(ag_ref[:, sl])
            ext_h0[0:HALO, :] = avh_ref[:, sl] * _sigmoid(agh_ref[:, sl]) * keep_past
            ext_h0[HALO:HALO + ta, :] = av * sig
            ext_d[0:ta, :] = dh1_ref[:, sl]
            ext_d[ta:ta + HALO, :] = dh1h_ref[:, sl] * keep_next
            for r0 in range(0, ta, rc):
                dh1 = ext_d[pl.ds(r0, rc), :]
                acc = jnp.zeros((rc, LANES), F32)
                for k in range(CONV_K):
                    acc = acc + ext_d[pl.ds(r0 + CONV_K - 1 - k, rc), :] * cw_ref[k:k + 1, sl]
                    prod = dh1 * ext_h0[pl.ds(r0 + HALO - (CONV_K - 1) + k, rc), :]
                    dcw_ref[g, k] += jnp.sum(prod.reshape(rc // SUBLANES, SUBLANES, LANES), axis=0)
                rows = pl.ds(r0, rc)
                sig_r = sig[r0:r0 + rc]
                dav = acc * sig_r
                dag = dav * av[r0:r0 + rc] * (1.0 - sig_r)
                dproj_ref[rows, sl] = dav.astype(BF16)
                dproj_ref[rows, pl.ds(pl.multiple_of(d_model + g * LANES, LANES), LANES)] = dag.astype(BF16)
                vec_ref[0:1, sl] += _colsum(dav)
                vec_ref[1:2, sl] += _colsum(dag)
            return carry

        lax.fori_loop(0, n_groups, group, 0, unroll=True)

        @pl.when(i == n_tiles - 1)
        def _():
            for cp in own_rows():
                cp.wait()
            for cp in sends():
                cp.wait_recv()
            for a, o in enumerate([opa_ref, opb_ref, opo_ref]):
                for q in range(r8 // prow):
                    r = pl.ds(q * prow, prow)
                    tot = own[a, r, :]
                    for k in range(N_DEV - 1):
                        tot = tot + rbuf[k, a, r, :].astype(F32)
                    o[r, :] = tot
            for cp in sends():
                cp.wait_send()

    blk = lambda j: pl.BlockSpec((ta, d_model), lambda i: (i, j))
    halo = lambda j: pl.BlockSpec((HALO, d_model), lambda i: (jnp.maximum(i * (ta // HALO) - 1, 0), j))
    shard = pl.BlockSpec((r8, d_model), lambda i: (0, 0))
    return pl.pallas_call(
        body, name="branch_a_bwd_conv", grid=(n_tiles,),
        in_specs=[ANY, blk(0), blk(1), halo(0), halo(1), pl.BlockSpec((ta, d_model), lambda i: (i, 0)),
                  pl.BlockSpec((HALO, d_model), lambda i: (jnp.minimum((i + 1) * (ta // HALO), last_halo), 0)),
                  pl.BlockSpec((HALO, d_model), lambda i: (0, 0)), ANY, ANY],
        out_specs=[pl.BlockSpec((ta, 2 * d_model), lambda i: (i, 0)),
                   pl.BlockSpec((A2_ROWS, d_model), lambda i: (0, 0)),
                   pl.BlockSpec((n_groups, HALO, SUBLANES, LANES), lambda i: (0, 0, 0, 0)),
                   shard, shard, shard],
        out_shape=[SDS(dproj.shape, BF16), SDS((A2_ROWS, d_model), F32),
                   SDS((n_groups, HALO, SUBLANES, LANES), F32)] + [SDS((r8, d_model), F32)] * 3,
        scratch_shapes=[pltpu.VMEM((HALO + ta, LANES), F32), pltpu.VMEM((ta + HALO, LANES), F32),
                        pltpu.VMEM((N_DEV - 1, 3, r8, d_model), BF16), pltpu.VMEM((3, r8, d_model), F32),
                        pltpu.SemaphoreType.DMA((3, 7)), pltpu.SemaphoreType.DMA((3, 7)),
                        pltpu.SemaphoreType.DMA((3,))],
        input_output_aliases={0: 0},
        compiler_params=_params(1),
    )(dproj, proj, proj, proj, proj, d_h1, d_h1, conv_w_full, gp_bf, gp_f32)


def _weight_grads(lhs3, rhs3):
    n_mat, n_tok, d_model = lhs3.shape
    tk = _tile(n_tok, 2048, 16)
    n_k = n_tok // tk

    def body(a_ref, g_ref, o_ref, ob_ref):
        part = _dot_ta(a_ref[...], g_ref[...])

        @pl.when(pl.program_id(1) == 0)
        def _():
            o_ref[...] = part

        @pl.when(pl.program_id(1) != 0)
        def _():
            o_ref[...] += part

        @pl.when(pl.program_id(1) == n_k - 1)
        def _():
            ob_ref[...] = o_ref[...].astype(BF16)

    tile = pl.BlockSpec((None, tk, d_model), lambda a, i: (a, i, 0))
    out = pl.BlockSpec((None, d_model, d_model), lambda a, i: (a, 0, 0))
    return pl.pallas_call(
        body, name="grad_w_pa_pb_o", grid=(n_mat, n_k), in_specs=[tile, tile], out_specs=[out, out],
        out_shape=[SDS((n_mat, d_model, d_model), F32), SDS((n_mat, d_model, d_model), BF16)],
        compiler_params=_params(2),
    )(lhs3, rhs3)


def _packed_rows(pieces, plan, d_model):
    rows = {"row": lambda a: d_model // LANES, "block": lambda a: a.shape[0],
            "tiles": lambda a: a.shape[0] * a.shape[1]}
    return sum(rows[kind](pieces[p]) for kind, p, _ in plan)


def _grad_w_in_reduce_scatter(xt_bf, dproj, blk_order, pieces, plan):
    n_tok = dproj.shape[0]
    d_model = xt_bf.shape[0]
    dh = d_model // 2
    n_units = 2 * N_DEV
    n_pc = len(pieces)
    n_packed = _packed_rows(pieces, plan, d_model)
    n_part = n_packed + -n_packed % (N_DEV * SUBLANES)
    rsl = n_part // N_DEV

    def body(ord_ref, a_ref, g_ref, *rest):
        pc_refs, rest = rest[:n_pc], rest[n_pc:]
        (o_ref, small_ref, acc, fb, sbuf, gbuf, tbuf, rfin, rbuf_s, red, p_ref,
         send_f, send_s, recv_g, recv_t, recv_f, out_sems, send1, recv1, send2, recv2) = rest
        del ord_ref
        u = pl.program_id(0)
        s = u // 2
        hf = u % 2
        rnd = s // 2
        x, y, c = _mesh_pos()
        sibling = (x, y, 1 - c)
        me = _block_of((x, y, c))
        peers = _peers()

        def rows_of(blk):
            return pl.ds(pl.multiple_of(blk * rsl, SUBLANES), rsl)

        def scatter():
            return [pltpu.make_async_remote_copy(
                src_ref=p_ref.at[rows_of(blk), :], dst_ref=rbuf_s.at[k], send_sem=send1.at[k],
                recv_sem=recv1.at[k], device_id=pos, device_id_type=MESH) for k, (pos, blk) in enumerate(peers)]

        def gather(dst_block=None):
            return [pltpu.make_async_remote_copy(
                src_ref=red, dst_ref=small_ref.at[rows_of(me if dst_block is None else blk), :],
                send_sem=send2.at[k], recv_sem=recv2.at[k], device_id=pos, device_id_type=MESH)
                for k, (pos, blk) in enumerate(peers)]

        def own_slice():
            return pltpu.make_async_copy(red, small_ref.at[rows_of(me), :], out_sems.at[2])

        def pack():
            at = 0
            for kind, p, r in plan:
                ref = pc_refs[p]
                if kind == "row":
                    for q in range(d_model // LANES):
                        p_ref[at + q:at + q + 1, :] = ref[r:r + 1, q * LANES:(q + 1) * LANES]
                    at += d_model // LANES
                elif kind == "block":
                    p_ref[at:at + ref.shape[0], :] = ref[...]
                    at += ref.shape[0]
                else:
                    for tile in range(ref.shape[0] * ref.shape[1]):
                        p_ref[at + tile:at + tile + 1, :] = _colsum(ref[tile // ref.shape[1], tile % ref.shape[1]])
                    at += ref.shape[0] * ref.shape[1]
            if at < n_part:
                p_ref[at:n_part, :] = jnp.zeros((n_part - at, LANES), F32)

        @pl.when(u == 0)
        def _():
            pack()
            for cp in scatter():
                cp.start()

        @pl.when(u == 8)
        def _():
            for cp in scatter():
                cp.wait_recv()
            tot = p_ref[rows_of(me), :]
            for k in range(N_DEV - 1):
                tot = tot + rbuf_s[k]
            red[...] = tot
            own_slice().start()
            for cp in gather():
                cp.start()

        n1 = (jnp.bitwise_xor(x, c), jnp.bitwise_xor(y, 1 - c), c)
        n2 = (jnp.bitwise_xor(x, 1 - c), jnp.bitwise_xor(y, c), c)

        def feed(r, half):
            return pltpu.make_async_remote_copy(
                src_ref=fb.at[half], dst_ref=gbuf.at[r, half], send_sem=send_f.at[r, half],
                recv_sem=recv_g.at[r, half], device_id=sibling, device_id_type=MESH)

        def feed_sibling(half):
            return pltpu.make_async_remote_copy(
                src_ref=fb.at[half], dst_ref=rfin.at[0, half], send_sem=send_f.at[3, half],
                recv_sem=recv_f.at[0, half], device_id=sibling, device_id_type=MESH)

        def chip_sum(r, half):
            dst = [tbuf.at[half], rfin.at[2, half], rfin.at[1, half]][r]
            sem = [recv_t.at[half], recv_f.at[2, half], recv_f.at[1, half]][r]
            return pltpu.make_async_remote_copy(
                src_ref=sbuf.at[r, half], dst_ref=dst, send_sem=send_s.at[r, half], recv_sem=sem,
                device_id=[n2, n2, n1][r], device_id_type=MESH)

        def out_copy(half):
            return pltpu.make_async_copy(acc.at[half], o_ref.at[:, pl.ds(half * dh, dh)], out_sems.at[half])

        def partial_sum():
            return _dot(a_ref[:, 0:n_tok], g_ref[...])

        for half in range(2):
            for r in range(3):
                @pl.when(u == 4 * r + 4 + half)
                def _(r=r, half=half):
                    feed(r, half).wait_send()

                @pl.when(u == 4 * r + 2 + half)
                def _(r=r, half=half):
                    feed(r, half).wait_recv()
                    if r == 2:
                        chip_sum(0, half).wait_recv()

            @pl.when(u == 14 + half)
            def _(half=half):
                feed_sibling(half).wait_recv()
                chip_sum(2, half).wait_recv()
                chip_sum(1, half).wait_recv()

        @pl.when(jnp.logical_and(s % 2 == 0, s < 7))
        def _():
            fb[hf] = partial_sum().astype(BF16)

        @pl.when(jnp.logical_or(s == 1, s == 3))
        def _():
            sbuf[rnd, hf] = (partial_sum() + gbuf[rnd, hf].astype(F32)).astype(BF16)

        @pl.when(s == 5)
        def _():
            sbuf[2, hf] = (partial_sum() + gbuf[2, hf].astype(F32) + tbuf[hf].astype(F32)).astype(BF16)

        @pl.when(s == 7)
        def _():
            acc[hf] = (partial_sum() + rfin[0, hf].astype(F32) + rfin[1, hf].astype(F32)
                       + rfin[2, hf].astype(F32))

        for half in range(2):
            for r in range(3):
                @pl.when(u == 4 * r + half)
                def _(r=r, half=half):
                    feed(r, half).start()

                @pl.when(u == 4 * r + 2 + half)
                def _(r=r, half=half):
                    chip_sum(r, half).start()

            @pl.when(u == 12 + half)
            def _(half=half):
                feed_sibling(half).start()

        @pl.when(u == 14)
        def _():
            out_copy(0).start()

        @pl.when(u == 15)
        def _():
            out_copy(1).start()
            for half in range(2):
                feed_sibling(half).wait_send()
                for r in range(3):
                    chip_sum(r, half).wait_send()
                out_copy(half).wait()
            for cp in gather("theirs"):
                cp.wait_recv()
            for cp in scatter() + gather():
                cp.wait_send()
            own_slice().wait()

    grid_spec = pltpu.PrefetchScalarGridSpec(
        num_scalar_prefetch=1, grid=(n_units,),
        in_specs=[VMEM, pl.BlockSpec((n_tok, dh), lambda u, o: (0, 2 * o[u // 2] + u % 2))] + [VMEM] * n_pc,
        out_specs=[ANY, ANY],
        scratch_shapes=[pltpu.VMEM((2, d_model, dh), F32), pltpu.VMEM((2, d_model, dh), BF16),
                        pltpu.VMEM((3, 2, d_model, dh), BF16), pltpu.VMEM((3, 2, d_model, dh), BF16),
                        pltpu.VMEM((2, d_model, dh), BF16), pltpu.VMEM((3, 2, d_model, dh), BF16),
                        pltpu.VMEM((N_DEV - 1, rsl, LANES), F32), pltpu.VMEM((rsl, LANES), F32),
                        pltpu.VMEM((n_part, LANES), F32),
                        pltpu.SemaphoreType.DMA((4, 2)), pltpu.SemaphoreType.DMA((3, 2)),
                        pltpu.SemaphoreType.DMA((3, 2)), pltpu.SemaphoreType.DMA((2,)),
                        pltpu.SemaphoreType.DMA((3, 2)),
                        pltpu.SemaphoreType.DMA((3,))] + [pltpu.SemaphoreType.DMA((N_DEV - 1,))] * 4)
    return pl.pallas_call(
        body, name="grad_w_in_reduce_scatter", grid_spec=grid_spec,
        out_shape=[SDS((d_model, d_model), F32), SDS((n_part, LANES), F32)], compiler_params=_params(1),
    )(blk_order, xt_bf, dproj, *pieces)


def _grad_x_adamw(dproj, w_all, dr, w, g, m, v, small, loss_rows, packed, groups):
    n_tok, d_model = dr.shape
    tm = _tile(n_tok, 512, 16)
    n_col = 2
    per = N_DEV // n_col
    n_steps = (n_tok // tm) * n_col
    tr = w.shape[0] // n_steps
    n_pk, n_grp = len(packed), len(groups)
    n_out = 4 * (n_pk + n_grp)

    def body(dp_ref, w_hbm, dr_ref, ws_ref, g_ref, m_ref, v_ref, small_ref, *rest):
        pk_in, rest = rest[:3 * n_pk], rest[3 * n_pk:]
        grp_in, rest = rest[:4 * n_grp], rest[4 * n_grp:]
        o_ref, gp_ref, d_ref, mo_ref, vo_ref, loss_ref = rest[:6]
        outs, (w_ref, w_sems) = rest[6:6 + n_out], rest[6 + n_out:]
        col = pl.program_id(1)
        i = pl.program_id(0) * n_col + col

        def fetch(j):
            return pltpu.make_async_copy(w_hbm.at[j], w_ref.at[j], w_sems.at[j])

        def grad_x_tile(first, c):
            acc = DEEPNORM_ALPHA * dr_ref[...] if c == 0 else o_ref[...]
            for j in range(per):
                if first:
                    fetch(c * per + j).wait()
                acc = acc + _dot_tb(dp_ref[:, j * d_model:(j + 1) * d_model], w_ref[c * per + j])
            o_ref[...] = acc

        for c in range(n_col):
            @pl.when(i == c)
            def _(c=c):
                if c == 0:
                    for j in range(N_DEV):
                        fetch(j).start()
                grad_x_tile(True, c)

            @pl.when(jnp.logical_and(i >= n_col, col == c))
            def _(c=c):
                grad_x_tile(False, c)

        grad = g_ref[...]
        gp_ref[...] = grad
        d_ref[...], mo_ref[...], vo_ref[...] = _adamw_math(ws_ref[...], grad, m_ref[...], v_ref[...])

        @pl.when(i == n_steps - 1)
        def _():
            def update(p, pw, grad, pm, pv):
                og, od, om, ov = outs[4 * p:4 * p + 4]
                og[...] = grad
                od[...], om[...], ov[...] = _adamw_math(pw[...], grad, pm[...], pv[...])

            for p, (pw_arr, _, _, row0) in enumerate(packed):
                pw, pm, pv = pk_in[3 * p:3 * p + 3]
                update(p, pw, small_ref[row0:row0 + pw_arr.shape[0], :], pm, pv)
            for p in range(n_grp):
                pw, pg, pm, pv = grp_in[4 * p:4 * p + 4]
                update(n_pk + p, pw, pg[...], pm, pv)
            sq = small_ref[loss_rows[0]:loss_rows[0] + loss_rows[1], :]
            loss_ref[...] = jnp.sum(_colsum(sq), axis=1, keepdims=True) * (0.5 / d_model)

    tile = pl.BlockSpec((tm, d_model), lambda r, c: (r, 0))
    slab = pl.BlockSpec((tr, w.shape[1]), lambda r, c: (r * n_col + c, 0))
    flat = [a for pk in packed for a in pk[:3]] + [a for grp in groups for a in grp]
    shapes = [pk[0].shape for pk in packed] + [grp[0].shape for grp in groups]
    res = pl.pallas_call(
        body, name="grad_x_adamw", grid=(n_tok // tm, n_col),
        in_specs=[pl.BlockSpec((tm, per * d_model), lambda r, c: (r, c)), ANY, tile, slab, slab, slab, slab]
        + [VMEM] * (1 + len(flat)),
        out_specs=[tile, slab, slab, slab, slab, VMEM] + [VMEM] * n_out,
        out_shape=[SDS((n_tok, d_model), F32)] + [SDS(w.shape, F32)] * 4 + [SDS((1, 1), F32)]
        + [SDS(shape, F32) for shape in shapes for _ in range(4)],
        scratch_shapes=[pltpu.VMEM(w_all.shape, BF16), pltpu.SemaphoreType.DMA((N_DEV,))],
        compiler_params=_params(2),
    )(dproj, w_all, dr, w, g, m, v, small, *flat)
    return (res[0], tuple(res[1:5]), [tuple(res[6 + 4 * p:10 + 4 * p]) for p in range(n_pk + n_grp)],
            res[5].reshape(()))


def _adamw_math(w, g, m, v):
    m = ADAM_B1 * m + (1.0 - ADAM_B1) * g
    v = ADAM_B2 * v + (1.0 - ADAM_B2) * (g * g)
    m_hat = m / (1.0 - ADAM_B1 ** ADAM_STEP)
    v_hat = v / (1.0 - ADAM_B2 ** ADAM_STEP)
    delta = -ADAM_LR * (m_hat / (jnp.sqrt(v_hat) + ADAM_EPS) + ADAM_WD * w)
    return delta, m, v


def _as_rows(a):
    return a.reshape(-1, LANES)


def kernel(x, w_in, b_in, conv_w, conv_b, gn_g, gn_b, ln_v_g, ln_v_b, w_spatial, b_spatial, w_pa, w_pb, w_o, b_o, ln_out_g, ln_out_b, loss_target, m_w_in, m_b_in, m_conv_w, m_conv_b, m_gn_g, m_gn_b, m_ln_v_g, m_ln_v_b, m_w_spatial, m_b_spatial, m_w_pa, m_w_pb, m_w_o, m_b_o, m_ln_out_g, m_ln_out_b, v_w_in, v_b_in, v_conv_w, v_conv_b, v_gn_g, v_gn_b, v_ln_v_g, v_ln_v_b, v_w_spatial, v_b_spatial, v_w_pa, v_w_pb, v_w_o, v_b_o, v_ln_out_g, v_ln_out_b):
    n_batch, seq, d_model = x.shape
    n_tok = n_batch * seq
    n_heads = d_model // LANES
    dc = conv_w.shape[1]
    me = 4 * lax.axis_index("x") + 2 * lax.axis_index("y") + lax.axis_index("c")
    row = lambda a: a.reshape(1, d_model)

    x2 = x.reshape(n_tok, d_model)
    target2 = loss_target.reshape(n_tok, d_model)
    b_spatial_t = b_spatial.T

    first = jnp.where(lax.axis_index("c") == 1, 4, 2)
    second = 6 - first
    ag_rel = jnp.stack([0 * first, 0 * first + 1, first, second + 1, second, first + 1, 0 * first + 6, 0 * first + 7])
    ag_blocks = jnp.bitwise_xor(me, ag_rel).astype(jnp.int32)
    proj, xt_bf, w_all, wp_all, cw_all = _proj_all_gather(
        x2, w_in, w_pa, w_pb, w_o, conv_w, b_in.reshape(N_DEV, 1, d_model), ag_blocks)
    wp_full = wp_all.reshape(3, d_model, d_model)
    conv_w_full = jnp.pad(cw_all.transpose(1, 0, 2).reshape(CONV_K, d_model), ((0, HALO - CONV_K), (0, 0)))

    h3, h1 = _branch_a_fwd(proj, conv_w_full, row(conv_b), row(gn_g), row(gn_b), seq)
    s = _branch_b_fwd(proj, row(ln_v_g), row(ln_v_b), w_spatial, b_spatial_t, seq)

    dproj, d_h3, d_s, dr, lhs3, rhs3, vec_mid = _mid(
        h3, s, proj, x2, target2, wp_full, row(b_o), row(ln_out_g), row(ln_out_b))

    dproj, vec_b, d_ws, d_bs_t = _branch_b_bwd(dproj, proj, d_s, row(ln_v_g), row(ln_v_b), w_spatial, b_spatial_t, seq)
    dproj, d_h1, vec_a1 = _branch_a_bwd_norm(dproj, proj, h1, d_h3, row(gn_g), row(gn_b), seq)
    gp_f32, gp_bf = _weight_grads(lhs3, rhs3)
    dproj, vec_a2, d_cw8, g_w_pa, g_w_pb, g_w_o = _branch_a_bwd_conv(
        dproj, proj, d_h1, conv_w_full, gp_bf, gp_f32, seq)

    pieces = [vec_a2, vec_a1, vec_b, vec_mid, d_bs_t.T, _as_rows(d_ws), d_cw8]
    a2, a1, vb, mid = 0, 1, 2, 3
    plan = ([("row", p, r) for p, r in [(a2, 0), (a2, 1), (a1, 0), (vb, 0), (vb, 1), (vb, 2), (mid, 3), (mid, 4)]]
            + [("row", p, r) for p, r in [(a1, 3), (a1, 1), (a1, 2), (vb, 3), (vb, 4), (mid, 2), (mid, 0), (mid, 1)]]
            + [("block", 4, None), ("block", 5, None), ("tiles", 6, None), ("row", mid, 5)])

    rs_rel = jnp.stack([0 * first + 7, 0 * first + 6, first + 1, second, second + 1, first, 0 * first + 1, 0 * first])
    rs_blocks = jnp.bitwise_xor(me, rs_rel).astype(jnp.int32)
    g_w_in, small = _grad_w_in_reduce_scatter(xt_bf, dproj, rs_blocks, pieces, plan)

    g_rows = d_model // LANES
    o0 = N_DEV * g_rows
    o1 = o0 + 8 * g_rows
    o2 = o1 + n_heads
    o3 = o2 + n_heads * LANES
    o4 = o3 + n_heads * HALO
    g_cw_full = small[o3:o4].reshape(n_heads, HALO, LANES).transpose(1, 0, 2).reshape(HALO, d_model)
    g_conv_w = lax.dynamic_slice(g_cw_full, (0, me * dc), (CONV_K, dc))

    two_d = lambda a: a.reshape(-1, a.shape[-1]) if a.ndim != 1 else (
        a.reshape(-1, LANES) if a.shape[0] % LANES == 0 else a.reshape(1, -1))
    names = ["b_in", "conv_w", "conv_b", "gn_g", "gn_b", "ln_v_g", "ln_v_b", "w_spatial", "b_spatial",
             "w_pa", "w_pb", "w_o", "b_o", "ln_out_g", "ln_out_b"]
    ws = dict(b_in=b_in, conv_w=conv_w, conv_b=conv_b, gn_g=gn_g, gn_b=gn_b, ln_v_g=ln_v_g, ln_v_b=ln_v_b,
              w_spatial=w_spatial, b_spatial=b_spatial, w_pa=w_pa, w_pb=w_pb, w_o=w_o, b_o=b_o,
              ln_out_g=ln_out_g, ln_out_b=ln_out_b)
    ms = dict(b_in=m_b_in, conv_w=m_conv_w, conv_b=m_conv_b, gn_g=m_gn_g, gn_b=m_gn_b, ln_v_g=m_ln_v_g,
              ln_v_b=m_ln_v_b, w_spatial=m_w_spatial, b_spatial=m_b_spatial, w_pa=m_w_pa, w_pb=m_w_pb,
              w_o=m_w_o, b_o=m_b_o, ln_out_g=m_ln_out_g, ln_out_b=m_ln_out_b)
    vs = dict(b_in=v_b_in, conv_w=v_conv_w, conv_b=v_conv_b, gn_g=v_gn_g, gn_b=v_gn_b, ln_v_g=v_ln_v_g,
              ln_v_b=v_ln_v_b, w_spatial=v_w_spatial, b_spatial=v_b_spatial, w_pa=v_w_pa, w_pb=v_w_pb,
              w_o=v_w_o, b_o=v_b_o, ln_out_g=v_ln_out_g, ln_out_b=v_ln_out_b)
    vec_names = ["conv_b", "gn_g", "gn_b", "ln_v_g", "ln_v_b", "b_o", "ln_out_g", "ln_out_b"]
    first_row = dict(b_in=0, b_spatial=o1, w_spatial=o2, **{n: o0 + a * g_rows for a, n in enumerate(vec_names)})
    given = dict(conv_w=g_conv_w, w_pa=g_w_pa, w_pb=g_w_pb, w_o=g_w_o)
    grad_x, upd_w_in, upd, loss = _grad_x_adamw(
        dproj, w_all, dr, w_in, g_w_in, m_w_in, v_w_in, small, (o4, g_rows),
        [(two_d(ws[n]), two_d(ms[n]), two_d(vs[n]), row0) for n, row0 in first_row.items()],
        [tuple(two_d(a) for a in (ws[n], grad, ms[n], vs[n])) for n, grad in given.items()])
    grad_x = grad_x.reshape(x.shape)
    res = {n: tuple(a.reshape(ws[n].shape) for a in u) for n, u in zip([*first_row, *given], upd)}
    res["w_in"] = upd_w_in

    order = ["w_in"] + names
    return (loss, grad_x, *[res[n][k] for k in range(4) for n in order])
```

```python
import jax
import jax.numpy as jnp
from jax import lax
from jax.experimental import pallas as pl
from jax.experimental.pallas import tpu as pltpu

F32 = jnp.float32
BF16 = jnp.bfloat16
SDS = jax.ShapeDtypeStruct

N_DEV = 8
LANES = 128
SUBLANES = 8
CONV_K = 31
HALO = 32
ELEMENTWISE_ROWS = 512
LN_EPS = 1e-5
DEEPNORM_ALPHA = 2.0 ** 0.25
ADAM_LR, ADAM_B1, ADAM_B2, ADAM_EPS, ADAM_WD, ADAM_STEP = 0.001, 0.9, 0.999, 1e-08, 0.01, 10
GELU_C = 0.7978845608028654
GELU_A = 0.044715
VMEM_LIMIT = 56 * 1024 * 1024
MESH = pl.DeviceIdType.MESH
ANY = pl.BlockSpec(memory_space=pl.ANY)
VMEM = pl.BlockSpec(memory_space=pltpu.VMEM)


def _params(n_grid=0):
    sem = ("arbitrary",) * n_grid if n_grid else None
    return pltpu.CompilerParams(dimension_semantics=sem, vmem_limit_bytes=VMEM_LIMIT)


def _tile(n, pref, mult):
    t = min(n, pref)
    while n % t or t % mult:
        t -= 1
    return t


def _colsum(v):
    return jnp.sum(v, axis=0, keepdims=True)


def _sigmoid(v):
    return jax.nn.sigmoid(v)


def _silu_and_grad(v):
    s = _sigmoid(v)
    val = v * s
    return val, s + val * (1.0 - s)


def _gelu_and_grad(v):
    v2 = v * v
    sg = _sigmoid(v * (2.0 * GELU_C + (2.0 * GELU_C * GELU_A) * v2))
    grad = sg + v * sg * (1.0 - sg) * (2.0 * GELU_C + (6.0 * GELU_C * GELU_A) * v2)
    return v * sg, grad


def _tril_mask():
    r = lax.broadcasted_iota(jnp.int32, (LANES, LANES), 0)
    c = lax.broadcasted_iota(jnp.int32, (LANES, LANES), 1)
    return c <= r


def _dot(a, b):
    return jnp.dot(a, b, preferred_element_type=F32)


def _dot_tb(a, b):
    return lax.dot_general(a, b, (((1,), (1,)), ((), ())), preferred_element_type=F32)


def _dot_ta(a, b):
    return lax.dot_general(a, b, (((0,), (0,)), ((), ())), preferred_element_type=F32)


def _mesh_pos():
    return lax.axis_index("x"), lax.axis_index("y"), lax.axis_index("c")


def _block_of(pos):
    return 4 * pos[0] + 2 * pos[1] + pos[2]


def _peers():
    x, y, c = _mesh_pos()
    out = []
    for k in range(1, N_DEV):
        pos = (1 - x if k & 4 else x, 1 - y if k & 2 else y, 1 - c if k & 1 else c)
        out.append((pos, _block_of(pos)))
    return out


def _proj_all_gather(x2, w_in, w_pa, w_pb, w_o, conv_w, b_in3, blk_order):
    n_tok, d_model = x2.shape
    r8 = w_pa.shape[0]
    kc, dc = conv_w.shape
    tm = _tile(n_tok, 1024, LANES)
    n_t = n_tok // tm
    n_arr = 3

    def body(ord_ref, x_ref, b_ref, win_ref, wpa_ref, wpb_ref, wo_ref, cw_ref,
             proj_ref, xt_ref, wall_ref, wp_ref, cwall_ref,
             wbuf, xbuf, st_p, send_sems, recv_sems, local_sems, wall_sems):
        s = pl.program_id(0)
        t = pl.program_id(1)
        x, y, c = _mesh_pos()
        me = (x, y, c)
        sibling = (x, y, 1 - c)
        n1 = (jnp.bitwise_xor(x, c), jnp.bitwise_xor(y, 1 - c))
        n2 = (jnp.bitwise_xor(x, 1 - c), jnp.bitwise_xor(y, c))
        dg = (1 - x, 1 - y)
        outs = [wbuf, wp_ref, cwall_ref]
        srcs = [None, st_p, cw_ref]
        consumed = [me, sibling, (*n1, c), (*n2, 1 - c), (*n2, c), (*n1, 1 - c), (*dg, c), (*dg, 1 - c)]
        leaves = [(me, sibling), (me, (*n1, c)), (me, (*n2, c)), ((*n1, c), (*n2, c)),
                  ((*n1, c), sibling), ((*n2, c), sibling), ((*dg, c), sibling)]
        lands = [sibling, (*n1, c), (*n2, c), (*dg, c), (*n2, 1 - c), (*n1, 1 - c), (*dg, 1 - c)]

        def slot(o, pos):
            return o.at[:, _block_of(pos)] if o is wp_ref else o.at[_block_of(pos)]

        def copy(a, k, block, to, src=None):
            o = outs[a]
            return pltpu.make_async_remote_copy(
                src_ref=slot(o, block) if src is None else src, dst_ref=slot(o, block),
                send_sem=send_sems.at[a, k], recv_sem=recv_sems.at[a, k],
                device_id=to, device_id_type=MESH)

        def send(a, k):
            block, to = leaves[k]
            return copy(a, k, block, to, src=srcs[a] if k < 3 else None)

        def recv(a, k):
            return copy(a, k, lands[k], me)

        def local_copies():
            return [pltpu.make_async_copy(srcs[a], slot(outs[a], me), local_sems.at[a]) for a in (1, 2)]

        def to_hbm(step):
            return pltpu.make_async_copy(slot(wbuf, consumed[step]), slot(wall_ref, consumed[step]),
                                         wall_sems.at[step])

        def at_step(step):
            return pl.when(jnp.logical_and(s == step, t == 0))

        @at_step(0)
        def _():
            slot(wbuf, me)[...] = win_ref[...].astype(BF16)
            st_p[0] = wpa_ref[...].astype(BF16)
            st_p[1] = wpb_ref[...].astype(BF16)
            st_p[2] = wo_ref[...].astype(BF16)
            for a in range(n_arr):
                send(a, 0).start()
                send(a, 1).start()
            for cp in local_copies():
                cp.start()
            to_hbm(0).start()

        @at_step(1)
        def _():
            recv(0, 0).wait_recv()
            to_hbm(1).start()

        for rnd in range(3):
            @at_step(2 + 2 * rnd)
            def _(rnd=rnd):
                if rnd == 0:
                    for a in range(n_arr):
                        send(a, 2).start()
                if rnd == 1:
                    send(0, 2).wait_send()
                    send(0, 3).start()
                recv(0, 1 + rnd).wait_recv()
                send(0, 4 + rnd).start()
                to_hbm(2 + 2 * rnd).start()

            @at_step(3 + 2 * rnd)
            def _(rnd=rnd):
                for a in (1, 2):
                    recv(a, 1 + rnd).wait_recv()
                    if rnd == 0:
                        send(a, 3).start()
                    send(a, 4 + rnd).start()
                recv(0, 4 + rnd).wait_recv()
                to_hbm(3 + 2 * rnd).start()

        rows = pl.ds(pl.multiple_of(t * tm, tm), tm)

        @pl.when(s == 0)
        def _():
            xb = x_ref[...].astype(BF16)
            xbuf[rows, :] = xb
            xt_ref[...] = xb.T

        proj_ref[...] = _dot(xbuf[rows, :], wbuf[ord_ref[s]]) + b_ref[...]

        @pl.when(jnp.logical_and(s == N_DEV - 1, t == n_t - 1))
        def _():
            for a in (1, 2):
                for k in (0, 4, 5, 6):
                    recv(a, k).wait_recv()
            for a in range(n_arr):
                for k in range(7):
                    if (a, k) != (0, 2):
                        send(a, k).wait_send()
            for cp in local_copies() + [to_hbm(step) for step in range(N_DEV)]:
                cp.wait()

    grid_spec = pltpu.PrefetchScalarGridSpec(
        num_scalar_prefetch=1, grid=(N_DEV, n_t),
        in_specs=[pl.BlockSpec((tm, d_model), lambda s, t, o: (jnp.where(s == 0, t, n_t - 1), 0)),
                  pl.BlockSpec((None, 1, d_model), lambda s, t, o: (o[s], 0, 0)),
                  VMEM, VMEM, VMEM, VMEM, VMEM],
        out_specs=[pl.BlockSpec((tm, d_model), lambda s, t, o: (t, o[s])),
                   pl.BlockSpec((d_model, tm), lambda s, t, o: (0, jnp.where(s == 0, t, n_t))),
                   ANY, ANY, ANY],
        scratch_shapes=[pltpu.VMEM((N_DEV, d_model, d_model), BF16), pltpu.VMEM((n_tok, d_model), BF16),
                        pltpu.VMEM((3, r8, d_model), BF16),
                        pltpu.SemaphoreType.DMA((n_arr, 7)), pltpu.SemaphoreType.DMA((n_arr, 7)),
                        pltpu.SemaphoreType.DMA((3,)), pltpu.SemaphoreType.DMA((N_DEV,))])
    return pl.pallas_call(
        body, name="proj_all_gather", grid_spec=grid_spec,
        out_shape=[SDS((n_tok, N_DEV * d_model), F32), SDS((d_model, (n_t + 1) * tm), BF16),
                   SDS((N_DEV, d_model, d_model), BF16), SDS((3, N_DEV, r8, d_model), BF16),
                   SDS((N_DEV, kc, dc), F32)],
        compiler_params=_params(2),
    )(blk_order, x2, b_in3, w_in, w_pa, w_pb, w_o, conv_w)


def _conv_rows(ta):
    return _tile(ta, 64, SUBLANES)


def _branch_a_fwd(proj, conv_w_full, conv_b, gn_g, gn_b, seq):
    n_tok = proj.shape[0]
    d_model = conv_b.shape[1]
    ta = _tile(seq, ELEMENTWISE_ROWS, HALO)
    per_seq = seq // ta
    rc = _conv_rows(ta)

    def body(av_ref, ag_ref, gt_ref, avh_ref, agh_ref, cw_ref, cb_ref, gg_ref, gb_ref,
             h3_ref, h1_ref, ext):
        keep = jnp.where(pl.program_id(0) % per_seq == 0, 0.0, 1.0)

        def group(g, carry):
            sl = pl.ds(pl.multiple_of(g * LANES, LANES), LANES)
            ext[0:HALO, :] = avh_ref[:, sl] * _sigmoid(agh_ref[:, sl]) * keep
            ext[HALO:HALO + ta, :] = av_ref[:, sl] * _sigmoid(ag_ref[:, sl])
            for r0 in range(0, ta, rc):
                acc = jnp.broadcast_to(cb_ref[:, sl], (rc, LANES))
                for k in range(CONV_K):
                    acc = acc + ext[pl.ds(r0 + HALO - (CONV_K - 1) + k, rc), :] * cw_ref[k:k + 1, sl]
                h1_ref[pl.ds(r0, rc), sl] = acc
            h1 = h1_ref[:, sl]
            mu = jnp.mean(h1, axis=-1, keepdims=True)
            dlt = h1 - mu
            var = jnp.mean(dlt * dlt, axis=-1, keepdims=True)
            h2 = dlt * lax.rsqrt(var + LN_EPS) * gg_ref[:, sl] + gb_ref[:, sl]
            gate = gt_ref[:, sl]
            h3_ref[:, sl] = (h2 * _sigmoid(h2) * gate * _sigmoid(gate)).astype(BF16)
            return carry

        lax.fori_loop(0, d_model // LANES, group, 0, unroll=True)

    blk = lambda j: pl.BlockSpec((ta, d_model), lambda i: (i, j))
    halo = lambda j: pl.BlockSpec((HALO, d_model), lambda i: (jnp.maximum(i * (ta // HALO) - 1, 0), j))
    row = pl.BlockSpec((1, d_model), lambda i: (0, 0))
    return pl.pallas_call(
        body, name="branch_a_fwd", grid=(n_tok // ta,),
        in_specs=[blk(0), blk(1), blk(2), halo(0), halo(1),
                  pl.BlockSpec((HALO, d_model), lambda i: (0, 0)), row, row, row],
        out_specs=[pl.BlockSpec((ta, d_model), lambda i: (i, 0))] * 2,
        out_shape=[SDS((n_tok, d_model), BF16), SDS((n_tok, d_model), F32)],
        scratch_shapes=[pltpu.VMEM((HALO + ta, LANES), F32)],
        compiler_params=_params(1),
    )(proj, proj, proj, proj, proj, conv_w_full, conv_b, gn_g, gn_b)


def _branch_b_fwd(proj, ln_g, ln_b, w_spatial, b_spatial_t, seq):
    n_tok = proj.shape[0]
    d_model = ln_g.shape[1]
    n_heads = d_model // LANES
    tb = _tile(seq, ELEMENTWISE_ROWS, LANES)

    def body(u_ref, v_ref, bg_ref, lg_ref, lb_ref, ws_ref, bs_ref, s_ref, vn_buf):
        v, _ = _gelu_and_grad(v_ref[...])
        mu = jnp.mean(v, axis=-1, keepdims=True)
        dlt = v - mu
        var = jnp.mean(dlt * dlt, axis=-1, keepdims=True)
        vn_buf[...] = (dlt * lax.rsqrt(var + LN_EPS) * lg_ref[...] + lb_ref[...]).astype(BF16)
        tril = _tril_mask()
        for h in range(n_heads):
            cols = slice(h * LANES, (h + 1) * LANES)
            w_h = jnp.where(tril, ws_ref[h], 0.0).astype(BF16)
            bias = bs_ref[:, h:h + 1]
            for ch in range(tb // LANES):
                rows = slice(ch * LANES, (ch + 1) * LANES)
                mix = _dot(w_h, vn_buf[rows, cols]) + bias
                u, _ = _gelu_and_grad(u_ref[rows, cols])
                gate = bg_ref[rows, cols]
                s_ref[rows, cols] = (u * mix * gate * _sigmoid(gate)).astype(BF16)

    blk = lambda j: pl.BlockSpec((tb, d_model), lambda i: (i, j))
    row = pl.BlockSpec((1, d_model), lambda i: (0, 0))
    return pl.pallas_call(
        body, name="branch_b_fwd", grid=(n_tok // tb,),
        in_specs=[blk(3), blk(4), blk(5), row, row,
                  pl.BlockSpec((n_heads, LANES, LANES), lambda i: (0, 0, 0)),
                  pl.BlockSpec((LANES, n_heads), lambda i: (0, 0))],
        out_specs=pl.BlockSpec((tb, d_model), lambda i: (i, 0)),
        out_shape=SDS((n_tok, d_model), BF16),
        scratch_shapes=[pltpu.VMEM((tb, d_model), BF16)],
        compiler_params=_params(1),
    )(proj, proj, proj, ln_g, ln_b, w_spatial, b_spatial_t)


MID_ROWS = 8


def _mid(h3, s, proj, x2, target, wp_full, b_o, lo_g, lo_b):
    n_tok, d_model = x2.shape
    tm = _tile(n_tok, 256, 16)

    def body(h3_ref, s_ref, ma_ref, mb_ref, x_ref, t_ref, wpa_ref, wpb_ref, wo_ref, bo_ref,
             lg_ref, lb_ref, dproj_ref, dh3_ref, ds_ref, dr_ref, lhs3_ref, rhs3_ref, vec_ref):
        @pl.when(pl.program_id(0) == 0)
        def _():
            vec_ref[...] = jnp.zeros_like(vec_ref)

        h3 = h3_ref[...]
        s = s_ref[...]
        ya = _dot(h3, wpa_ref[...])
        yb = _dot(s, wpb_ref[...])
        ga = _sigmoid(ma_ref[...])
        gb = _sigmoid(mb_ref[...])
        mixed = (ga * ya + gb * yb).astype(BF16)
        lhs3_ref[0] = h3
        lhs3_ref[1] = s
        lhs3_ref[2] = mixed
        r = DEEPNORM_ALPHA * x_ref[...] + _dot(mixed, wo_ref[...]) + bo_ref[...]
        mu = jnp.mean(r, axis=-1, keepdims=True)
        dlt = r - mu
        rstd = lax.rsqrt(jnp.mean(dlt * dlt, axis=-1, keepdims=True) + LN_EPS)
        rhat = dlt * rstd
        diff = rhat * lg_ref[...] + lb_ref[...] - t_ref[...]
        dy = diff * (1.0 / d_model)
        vec_ref[0:1, :] += _colsum(dy * rhat)
        vec_ref[1:2, :] += _colsum(dy)
        vec_ref[5:6, :] += _colsum(diff * diff)
        drh = dy * lg_ref[...]
        dr = rstd * (drh - jnp.mean(drh, axis=-1, keepdims=True)
                     - rhat * jnp.mean(drh * rhat, axis=-1, keepdims=True))
        vec_ref[2:3, :] += _colsum(dr)
        dr_ref[...] = dr
        drb = dr.astype(BF16)
        rhs3_ref[2] = drb
        dmixed = _dot_tb(drb, wo_ref[...])
        dya_f = dmixed * ga
        dyb_f = dmixed * gb
        dma = dya_f * ya * (1.0 - ga)
        dmb = dyb_f * yb * (1.0 - gb)
        vec_ref[3:4, :] += _colsum(dma)
        vec_ref[4:5, :] += _colsum(dmb)
        dproj_ref[:, 0:d_model] = dma.astype(BF16)
        dproj_ref[:, d_model:2 * d_model] = dmb.astype(BF16)
        dya = dya_f.astype(BF16)
        dyb = dyb_f.astype(BF16)
        rhs3_ref[0] = dya
        rhs3_ref[1] = dyb
        dh3_ref[...] = _dot_tb(dya, wpa_ref[...])
        ds_ref[...] = _dot_tb(dyb, wpb_ref[...])

    tile = pl.BlockSpec((tm, d_model), lambda i: (i, 0))
    full = lambda a: pl.BlockSpec((None, d_model, d_model), lambda i: (a, 0, 0))
    row = pl.BlockSpec((1, d_model), lambda i: (0, 0))
    stack = pl.BlockSpec((3, tm, d_model), lambda i: (0, i, 0))
    bf3 = SDS((3, n_tok, d_model), BF16)
    f32 = SDS((n_tok, d_model), F32)
    return pl.pallas_call(
        body, name="mid", grid=(n_tok // tm,),
        in_specs=[tile, tile, pl.BlockSpec((tm, d_model), lambda i: (i, 6)),
                  pl.BlockSpec((tm, d_model), lambda i: (i, 7)), tile, tile, full(0), full(1), full(2),
                  row, row, row],
        out_specs=[pl.BlockSpec((tm, 2 * d_model), lambda i: (i, 3)), tile, tile, tile, stack, stack,
                   pl.BlockSpec((MID_ROWS, d_model), lambda i: (0, 0))],
        out_shape=[SDS((n_tok, N_DEV * d_model), BF16), f32, f32, f32, bf3, bf3,
                   SDS((MID_ROWS, d_model), F32)],
        compiler_params=_params(1),
    )(h3, s, proj, proj, x2, target, wp_full, wp_full, wp_full, b_o, lo_g, lo_b)


B_ROWS = 8


def _branch_b_bwd(dproj, proj, d_s, ln_g, ln_b, w_spatial, b_spatial_t, seq):
    n_tok = proj.shape[0]
    d_model = ln_g.shape[1]
    n_heads = d_model // LANES
    tb = _tile(seq, ELEMENTWISE_ROWS, LANES)

    def body(dproj_in, u_ref, v_ref, bg_ref, ds_ref, lg_ref, lb_ref, ws_ref, bs_ref,
             dproj_ref, vec_ref, dws_ref, dbs_ref, vn_buf, dv_buf):
        del dproj_in

        @pl.when(pl.program_id(0) == 0)
        def _():
            vec_ref[...] = jnp.zeros_like(vec_ref)
            dws_ref[...] = jnp.zeros_like(dws_ref)
            dbs_ref[...] = jnp.zeros_like(dbs_ref)

        v, dgelu_v = _gelu_and_grad(v_ref[...])
        mu = jnp.mean(v, axis=-1, keepdims=True)
        dlt = v - mu
        rstd = lax.rsqrt(jnp.mean(dlt * dlt, axis=-1, keepdims=True) + LN_EPS)
        vhat = dlt * rstd
        vn_buf[...] = (vhat * lg_ref[...] + lb_ref[...]).astype(BF16)
        tril = _tril_mask()
        for h in range(n_heads):
            cols = slice(h * LANES, (h + 1) * LANES)
            w_h = jnp.where(tril, ws_ref[h], 0.0).astype(BF16)
            bias = bs_ref[:, h:h + 1]
            for ch in range(tb // LANES):
                rows = slice(ch * LANES, (ch + 1) * LANES)
                vn = vn_buf[rows, cols]
                mix = _dot(w_h, vn) + bias
                u, dgelu_u = _gelu_and_grad(u_ref[rows, cols])
                sg, dsilu = _silu_and_grad(bg_ref[rows, cols])
                dsv = ds_ref[rows, cols]
                du = dsv * mix * sg * dgelu_u
                dbg = dsv * u * mix * dsilu
                dmix = dsv * u * sg
                dmix_bf = dmix.astype(BF16)
                dproj_ref[rows, cols] = du.astype(BF16)
                dproj_ref[rows, 2 * d_model + h * LANES:2 * d_model + (h + 1) * LANES] = dbg.astype(BF16)
                vec_ref[0:1, cols] += _colsum(du)
                vec_ref[2:3, cols] += _colsum(dbg)
                dbs_ref[:, h:h + 1] += jnp.sum(dmix, axis=1, keepdims=True)
                dws_ref[h] += jnp.where(tril, _dot_tb(dmix_bf, vn), 0.0)
                dv_buf[rows, cols] = _dot_ta(w_h, dmix_bf)
        dvn = dv_buf[...]
        vec_ref[3:4, :] += _colsum(dvn * vhat)
        vec_ref[4:5, :] += _colsum(dvn)
        dvh = dvn * lg_ref[...]
        dv = rstd * (dvh - jnp.mean(dvh, axis=-1, keepdims=True)
                     - vhat * jnp.mean(dvh * vhat, axis=-1, keepdims=True)) * dgelu_v
        vec_ref[1:2, :] += _colsum(dv)
        dproj_ref[:, d_model:2 * d_model] = dv.astype(BF16)

    blk = lambda j: pl.BlockSpec((tb, d_model), lambda i: (i, j))
    row = pl.BlockSpec((1, d_model), lambda i: (0, 0))
    return pl.pallas_call(
        body, name="branch_b_bwd", grid=(n_tok // tb,),
        in_specs=[ANY, blk(3), blk(4), blk(5), pl.BlockSpec((tb, d_model), lambda i: (i, 0)), row, row,
                  pl.BlockSpec((n_heads, LANES, LANES), lambda i: (0, 0, 0)),
                  pl.BlockSpec((LANES, n_heads), lambda i: (0, 0))],
        out_specs=[pl.BlockSpec((tb, 3 * d_model), lambda i: (i, 1)),
                   pl.BlockSpec((B_ROWS, d_model), lambda i: (0, 0)),
                   pl.BlockSpec((n_heads, LANES, LANES), lambda i: (0, 0, 0)),
                   pl.BlockSpec((LANES, n_heads), lambda i: (0, 0))],
        out_shape=[SDS(dproj.shape, BF16), SDS((B_ROWS, d_model), F32),
                   SDS((n_heads, LANES, LANES), F32), SDS((LANES, n_heads), F32)],
        scratch_shapes=[pltpu.VMEM((tb, d_model), BF16), pltpu.VMEM((tb, d_model), F32)],
        input_output_aliases={0: 0},
        compiler_params=_params(1),
    )(dproj, proj, proj, proj, d_s, ln_g, ln_b, w_spatial, b_spatial_t)


A1_ROWS = 8


def _branch_a_bwd_norm(dproj, proj, h1, d_h3, gn_g, gn_b, seq):
    n_tok = proj.shape[0]
    d_model = gn_g.shape[1]
    ta = _tile(seq, ELEMENTWISE_ROWS, 16)

    def body(dproj_in, gt_ref, h1_ref, dh3_ref, gg_ref, gb_ref, dproj_ref, dh1_ref, vec_ref):
        del dproj_in

        @pl.when(pl.program_id(0) == 0)
        def _():
            vec_ref[...] = jnp.zeros_like(vec_ref)

        def group(g, carry):
            sl = pl.ds(pl.multiple_of(g * LANES, LANES), LANES)
            h1 = h1_ref[:, sl]
            mu = jnp.mean(h1, axis=-1, keepdims=True)
            dlt = h1 - mu
            rstd = lax.rsqrt(jnp.mean(dlt * dlt, axis=-1, keepdims=True) + LN_EPS)
            nrm = dlt * rstd
            sw, dsw = _silu_and_grad(nrm * gg_ref[:, sl] + gb_ref[:, sl])
            sg, dsg = _silu_and_grad(gt_ref[:, sl])
            dh3 = dh3_ref[:, sl]
            dgate = dh3 * sw * dsg
            dproj_ref[:, sl] = dgate.astype(BF16)
            vec_ref[0:1, sl] += _colsum(dgate)
            dh2 = dh3 * sg * dsw
            vec_ref[1:2, sl] += _colsum(dh2 * nrm)
            vec_ref[2:3, sl] += _colsum(dh2)
            dn = dh2 * gg_ref[:, sl]
            dh1 = rstd * (dn - jnp.mean(dn, axis=-1, keepdims=True)
                          - nrm * jnp.mean(dn * nrm, axis=-1, keepdims=True))
            vec_ref[3:4, sl] += _colsum(dh1)
            dh1_ref[:, sl] = dh1
            return carry

        lax.fori_loop(0, d_model // LANES, group, 0, unroll=True)

    tile = pl.BlockSpec((ta, d_model), lambda i: (i, 0))
    row = pl.BlockSpec((1, d_model), lambda i: (0, 0))
    return pl.pallas_call(
        body, name="branch_a_bwd_norm", grid=(n_tok // ta,),
        in_specs=[ANY, pl.BlockSpec((ta, d_model), lambda i: (i, 2)), tile, tile, row, row],
        out_specs=[pl.BlockSpec((ta, d_model), lambda i: (i, 2)), tile,
                   pl.BlockSpec((A1_ROWS, d_model), lambda i: (0, 0))],
        out_shape=[SDS(dproj.shape, BF16), SDS((n_tok, d_model), F32), SDS((A1_ROWS, d_model), F32)],
        input_output_aliases={0: 0},
        compiler_params=_params(1),
    )(dproj, proj, h1, d_h3, gn_g, gn_b)


A2_ROWS = 8


def _branch_a_bwd_conv(dproj, proj, d_h1, conv_w_full, gp_bf, gp_f32, seq):
    n_tok = proj.shape[0]
    d_model = conv_w_full.shape[1]
    n_groups = d_model // LANES
    ta = _tile(seq, ELEMENTWISE_ROWS, HALO)
    n_tiles = n_tok // ta
    per_seq = seq // ta
    rc = _conv_rows(ta)
    last_halo = n_tok // HALO - 1
    r8 = d_model // N_DEV
    prow = _tile(r8, 32, 16)

    def body(dproj_in, av_ref, ag_ref, avh_ref, agh_ref, dh1_ref, dh1h_ref, cw_ref, gp_bf_ref, gp_f32_ref,
             dproj_ref, vec_ref, dcw_ref, opa_ref, opb_ref, opo_ref,
             ext_h0, ext_d, rbuf, own, send_sems, recv_sems, local_sems):
        del dproj_in
        i = pl.program_id(0)
        x, y, c = _mesh_pos()
        me = _block_of((x, y, c))
        peers = _peers()

        def sends():
            return [pltpu.make_async_remote_copy(
                src_ref=gp_bf_ref.at[a, pl.ds(pl.multiple_of(blk * r8, 16), r8), :], dst_ref=rbuf.at[k, a],
                send_sem=send_sems.at[a, k], recv_sem=recv_sems.at[a, k], device_id=pos, device_id_type=MESH)
                for k, (pos, blk) in enumerate(peers) for a in range(3)]

        def own_rows():
            return [pltpu.make_async_copy(gp_f32_ref.at[a, pl.ds(pl.multiple_of(me * r8, 8), r8), :],
                                          own.at[a], local_sems.at[a]) for a in range(3)]

        @pl.when(i == 0)
        def _():
            vec_ref[...] = jnp.zeros_like(vec_ref)
            dcw_ref[...] = jnp.zeros_like(dcw_ref)
            for cp in sends() + own_rows():
                cp.start()

        keep_past = jnp.where(i % per_seq == 0, 0.0, 1.0)
        keep_next = jnp.where(i % per_seq == per_seq - 1, 0.0, 1.0)

        def group(g, carry):
            sl = pl.ds(pl.multiple_of(g * LANES, LANES), LANES)
            av = av_ref[:, sl]
            sig = _sigmoid(ag_ref[:, sl])
            ext_h0[0:HALO, :] = avh_ref[:, sl] * _sigmoid(agh_ref[:, sl]) * keep_past
            ext_h0[HALO:HALO + ta, :] = av * sig
            ext_d[0:ta, :] = dh1_ref[:, sl]
            ext_d[ta:ta + HALO, :] = dh1h_ref[:, sl] * keep_next
            for r0 in range(0, ta, rc):
                dh1 = ext_d[pl.ds(r0, rc), :]
                acc = jnp.zeros((rc, LANES), F32)
                for k in range(CONV_K):
                    acc = acc + ext_d[pl.ds(r0 + CONV_K - 1 - k, rc), :] * cw_ref[k:k + 1, sl]
                    prod = dh1 * ext_h0[pl.ds(r0 + HALO - (CONV_K - 1) + k, rc), :]
                    dcw_ref[g, k] += jnp.sum(prod.reshape(rc // SUBLANES, SUBLANES, LANES), axis=0)
                rows = pl.ds(r0, rc)
                sig_r = sig[r0:r0 + rc]
                dav = acc * sig_r
                dag = dav * av[r0:r0 + rc] * (1.0 - sig_r)
                dproj_ref[rows, sl] = dav.astype(BF16)
                dproj_ref[rows, pl.ds(pl.multiple_of(d_model + g * LANES, LANES), LANES)] = dag.astype(BF16)
                vec_ref[0:1, sl] += _colsum(dav)
                vec_ref[1:2, sl] += _colsum(dag)
            return carry

        lax.fori_loop(0, n_groups, group, 0, unroll=True)

        @pl.when(i == n_tiles - 1)
        def _():
            for cp in own_rows():
                cp.wait()
            for cp in sends():
                cp.wait_recv()
            for a, o in enumerate([opa_ref, opb_ref, opo_ref]):
                for q in range(r8 // prow):
                    r = pl.ds(q * prow, prow)
                    tot = own[a, r, :]
                    for k in range(N_DEV - 1):
                        tot = tot + rbuf[k, a, r, :].astype(F32)
                    o[r, :] = tot
            for cp in sends():
                cp.wait_send()

    blk = lambda j: pl.BlockSpec((ta, d_model), lambda i: (i, j))
    halo = lambda j: pl.BlockSpec((HALO, d_model), lambda i: (jnp.maximum(i * (ta // HALO) - 1, 0), j))
    shard = pl.BlockSpec((r8, d_model), lambda i: (0, 0))
    return pl.pallas_call(
        body, name="branch_a_bwd_conv", grid=(n_tiles,),
        in_specs=[ANY, blk(0), blk(1), halo(0), halo(1), pl.BlockSpec((ta, d_model), lambda i: (i, 0)),
                  pl.BlockSpec((HALO, d_model), lambda i: (jnp.minimum((i + 1) * (ta // HALO), last_halo), 0)),
                  pl.BlockSpec((HALO, d_model), lambda i: (0, 0)), ANY, ANY],
        out_specs=[pl.BlockSpec((ta, 2 * d_model), lambda i: (i, 0)),
                   pl.BlockSpec((A2_ROWS, d_model), lambda i: (0, 0)),
                   pl.BlockSpec((n_groups, HALO, SUBLANES, LANES), lambda i: (0, 0, 0, 0)),
                   shard, shard, shard],
        out_shape=[SDS(dproj.shape, BF16), SDS((A2_ROWS, d_model), F32),
                   SDS((n_groups, HALO, SUBLANES, LANES), F32)] + [SDS((r8, d_model), F32)] * 3,
        scratch_shapes=[pltpu.VMEM((HALO + ta, LANES), F32), pltpu.VMEM((ta + HALO, LANES), F32),
                        pltpu.VMEM((N_DEV - 1, 3, r8, d_model), BF16), pltpu.VMEM((3, r8, d_model), F32),
                        pltpu.SemaphoreType.DMA((3, 7)), pltpu.SemaphoreType.DMA((3, 7)),
                        pltpu.SemaphoreType.DMA((3,))],
        input_output_aliases={0: 0},
        compiler_params=_params(1),
    )(dproj, proj, proj, proj, proj, d_h1, d_h1, conv_w_full, gp_bf, gp_f32)


def _weight_grads(lhs3, rhs3):
    n_mat, n_tok, d_model = lhs3.shape
    tk = _tile(n_tok, 2048, 16)
    n_k = n_tok // tk

    def body(a_ref, g_ref, o_ref, ob_ref):
        part = _dot_ta(a_ref[...], g_ref[...])

        @pl.when(pl.program_id(1) == 0)
        def _():
            o_ref[...] = part

        @pl.when(pl.program_id(1) != 0)
        def _():
            o_ref[...] += part

        @pl.when(pl.program_id(1) == n_k - 1)
        def _():
            ob_ref[...] = o_ref[...].astype(BF16)

    tile = pl.BlockSpec((None, tk, d_model), lambda a, i: (a, i, 0))
    out = pl.BlockSpec((None, d_model, d_model), lambda a, i: (a, 0, 0))
    return pl.pallas_call(
        body, name="grad_w_pa_pb_o", grid=(n_mat, n_k), in_specs=[tile, tile], out_specs=[out, out],
        out_shape=[SDS((n_mat, d_model, d_model), F32), SDS((n_mat, d_model, d_model), BF16)],
        compiler_params=_params(2),
    )(lhs3, rhs3)


def _packed_rows(pieces, plan, d_model):
    rows = {"row": lambda a: d_model // LANES, "block": lambda a: a.shape[0],
            "tiles": lambda a: a.shape[0] * a.shape[1]}
    return sum(rows[kind](pieces[p]) for kind, p, _ in plan)


def _grad_w_in_reduce_scatter(xt_bf, dproj, blk_order, pieces, plan):
    n_tok = dproj.shape[0]
    d_model = xt_bf.shape[0]
    dh = d_model // 2
    n_units = 2 * N_DEV
    n_pc = len(pieces)
    n_packed = _packed_rows(pieces, plan, d_model)
    n_part = n_packed + -n_packed % (N_DEV * SUBLANES)
    rsl = n_part // N_DEV

    def body(ord_ref, a_ref, g_ref, *rest):
        pc_refs, rest = rest[:n_pc], rest[n_pc:]
        (o_ref, small_ref, acc, fb, sbuf, gbuf, tbuf, rfin, rbuf_s, red, p_ref,
         send_f, send_s, recv_g, recv_t, recv_f, out_sems, send1, recv1, send2, recv2) = rest
        del ord_ref
        u = pl.program_id(0)
        s = u // 2
        hf = u % 2
        rnd = s // 2
        x, y, c = _mesh_pos()
        sibling = (x, y, 1 - c)
        me = _block_of((x, y, c))
        peers = _peers()

        def rows_of(blk):
            return pl.ds(pl.multiple_of(blk * rsl, SUBLANES), rsl)

        def scatter():
            return [pltpu.make_async_remote_copy(
                src_ref=p_ref.at[rows_of(blk), :], dst_ref=rbuf_s.at[k], send_sem=send1.at[k],
                recv_sem=recv1.at[k], device_id=pos, device_id_type=MESH) for k, (pos, blk) in enumerate(peers)]

        def gather(dst_block=None):
            return [pltpu.make_async_remote_copy(
                src_ref=red, dst_ref=small_ref.at[rows_of(me if dst_block is None else blk), :],
                send_sem=send2.at[k], recv_sem=recv2.at[k], device_id=pos, device_id_type=MESH)
                for k, (pos, blk) in enumerate(peers)]

        def own_slice():
            return pltpu.make_async_copy(red, small_ref.at[rows_of(me), :], out_sems.at[2])

        def pack():
            at = 0
            for kind, p, r in plan:
                ref = pc_refs[p]
                if kind == "row":
                    for q in range(d_model // LANES):
                        p_ref[at + q:at + q + 1, :] = ref[r:r + 1, q * LANES:(q + 1) * LANES]
                    at += d_model // LANES
                elif kind == "block":
                    p_ref[at:at + ref.shape[0], :] = ref[...]
                    at += ref.shape[0]
                else:
                    for tile in range(ref.shape[0] * ref.shape[1]):
                        p_ref[at + tile:at + tile + 1, :] = _colsum(ref[tile // ref.shape[1], tile % ref.shape[1]])
                    at += ref.shape[0] * ref.shape[1]
            if at < n_part:
                p_ref[at:n_part, :] = jnp.zeros((n_part - at, LANES), F32)

        @pl.when(u == 0)
        def _():
            pack()
            for cp in scatter():
                cp.start()

        @pl.when(u == 8)
        def _():
            for cp in scatter():
                cp.wait_recv()
            tot = p_ref[rows_of(me), :]
            for k in range(N_DEV - 1):
                tot = tot + rbuf_s[k]
            red[...] = tot
            own_slice().start()
            for cp in gather():
                cp.start()

        n1 = (jnp.bitwise_xor(x, c), jnp.bitwise_xor(y, 1 - c), c)
        n2 = (jnp.bitwise_xor(x, 1 - c), jnp.bitwise_xor(y, c), c)

        def feed(r, half):
            return pltpu.make_async_remote_copy(
                src_ref=fb.at[half], dst_ref=gbuf.at[r, half], send_sem=send_f.at[r, half],
                recv_sem=recv_g.at[r, half], device_id=sibling, device_id_type=MESH)

        def feed_sibling(half):
            return pltpu.make_async_remote_copy(
                src_ref=fb.at[half], dst_ref=rfin.at[0, half], send_sem=send_f.at[3, half],
                recv_sem=recv_f.at[0, half], device_id=sibling, device_id_type=MESH)

        def chip_sum(r, half):
            dst = [tbuf.at[half], rfin.at[2, half], rfin.at[1, half]][r]
            sem = [recv_t.at[half], recv_f.at[2, half], recv_f.at[1, half]][r]
            return pltpu.make_async_remote_copy(
                src_ref=sbuf.at[r, half], dst_ref=dst, send_sem=send_s.at[r, half], recv_sem=sem,
                device_id=[n2, n2, n1][r], device_id_type=MESH)

        def out_copy(half):
            return pltpu.make_async_copy(acc.at[half], o_ref.at[:, pl.ds(half * dh, dh)], out_sems.at[half])

        def partial_sum():
            return _dot(a_ref[:, 0:n_tok], g_ref[...])

        for half in range(2):
            for r in range(3):
                @pl.when(u == 4 * r + 4 + half)
                def _(r=r, half=half):
                    feed(r, half).wait_send()

                @pl.when(u == 4 * r + 2 + half)
                def _(r=r, half=half):
                    feed(r, half).wait_recv()
                    if r == 2:
                        chip_sum(0, half).wait_recv()

            @pl.when(u == 14 + half)
            def _(half=half):
                feed_sibling(half).wait_recv()
                chip_sum(2, half).wait_recv()
                chip_sum(1, half).wait_recv()

        @pl.when(jnp.logical_and(s % 2 == 0, s < 7))
        def _():
            fb[hf] = partial_sum().astype(BF16)

        @pl.when(jnp.logical_or(s == 1, s == 3))
        def _():
            sbuf[rnd, hf] = (partial_sum() + gbuf[rnd, hf].astype(F32)).astype(BF16)

        @pl.when(s == 5)
        def _():
            sbuf[2, hf] = (partial_sum() + gbuf[2, hf].astype(F32) + tbuf[hf].astype(F32)).astype(BF16)

        @pl.when(s == 7)
        def _():
            acc[hf] = (partial_sum() + rfin[0, hf].astype(F32) + rfin[1, hf].astype(F32)
                       + rfin[2, hf].astype(F32))

        for half in range(2):
            for r in range(3):
                @pl.when(u == 4 * r + half)
                def _(r=r, half=half):
                    feed(r, half).start()

                @pl.when(u == 4 * r + 2 + half)
                def _(r=r, half=half):
                    chip_sum(r, half).start()

            @pl.when(u == 12 + half)
            def _(half=half):
                feed_sibling(half).start()

        @pl.when(u == 14)
        def _():
            out_copy(0).start()

        @pl.when(u == 15)
        def _():
            out_copy(1).start()
            for half in range(2):
                feed_sibling(half).wait_send()
                for r in range(3):
                    chip_sum(r, half).wait_send()
                out_copy(half).wait()
            for cp in gather("theirs"):
                cp.wait_recv()
            for cp in scatter() + gather():
                cp.wait_send()
            own_slice().wait()

    grid_spec = pltpu.PrefetchScalarGridSpec(
        num_scalar_prefetch=1, grid=(n_units,),
        in_specs=[VMEM, pl.BlockSpec((n_tok, dh), lambda u, o: (0, 2 * o[u // 2] + u % 2))] + [VMEM] * n_pc,
        out_specs=[ANY, ANY],
        scratch_shapes=[pltpu.VMEM((2, d_model, dh), F32), pltpu.VMEM((2, d_model, dh), BF16),
                        pltpu.VMEM((3, 2, d_model, dh), BF16), pltpu.VMEM((3, 2, d_model, dh), BF16),
                        pltpu.VMEM((2, d_model, dh), BF16), pltpu.VMEM((3, 2, d_model, dh), BF16),
                        pltpu.VMEM((N_DEV - 1, rsl, LANES), F32), pltpu.VMEM((rsl, LANES), F32),
                        pltpu.VMEM((n_part, LANES), F32),
                        pltpu.SemaphoreType.DMA((4, 2)), pltpu.SemaphoreType.DMA((3, 2)),
                        pltpu.SemaphoreType.DMA((3, 2)), pltpu.SemaphoreType.DMA((2,)),
                        pltpu.SemaphoreType.DMA((3, 2)),
                        pltpu.SemaphoreType.DMA((3,))] + [pltpu.SemaphoreType.DMA((N_DEV - 1,))] * 4)
    return pl.pallas_call(
        body, name="grad_w_in_reduce_scatter", grid_spec=grid_spec,
        out_shape=[SDS((d_model, d_model), F32), SDS((n_part, LANES), F32)], compiler_params=_params(1),
    )(blk_order, xt_bf, dproj, *pieces)


def _grad_x_adamw(dproj, w_all, dr, w, g, m, v, small, loss_rows, packed, groups):
    n_tok, d_model = dr.shape
    tm = _tile(n_tok, 256, 16)
    n_steps = n_tok // tm
    tr = w.shape[0] // n_steps
    n_pk, n_grp = len(packed), len(groups)
    n_out = 4 * (n_pk + n_grp)

    def body(dp_ref, w_hbm, dr_ref, ws_ref, g_ref, m_ref, v_ref, small_ref, *rest):
        pk_in, rest = rest[:3 * n_pk], rest[3 * n_pk:]
        grp_in, rest = rest[:4 * n_grp], rest[4 * n_grp:]
        o_ref, gp_ref, d_ref, mo_ref, vo_ref, loss_ref = rest[:6]
        outs, (w_ref, w_sems) = rest[6:6 + n_out], rest[6 + n_out:]
        i = pl.program_id(0)

        def fetch(j):
            return pltpu.make_async_copy(w_hbm.at[j], w_ref.at[j], w_sems.at[j])

        def grad_x_tile(first):
            acc = DEEPNORM_ALPHA * dr_ref[...]
            for j in range(N_DEV):
                if first:
                    fetch(j).wait()
                acc = acc + _dot_tb(dp_ref[:, j * d_model:(j + 1) * d_model], w_ref[j])
            o_ref[...] = acc

        @pl.when(i == 0)
        def _():
            for j in range(N_DEV):
                fetch(j).start()
            grad_x_tile(True)

        @pl.when(i != 0)
        def _():
            grad_x_tile(False)

        grad = g_ref[...]
        gp_ref[...] = grad
        d_ref[...], mo_ref[...], vo_ref[...] = _adamw_math(ws_ref[...], grad, m_ref[...], v_ref[...])

        @pl.when(i == n_steps - 1)
        def _():
            def update(p, pw, grad, pm, pv):
                og, od, om, ov = outs[4 * p:4 * p + 4]
                og[...] = grad
                od[...], om[...], ov[...] = _adamw_math(pw[...], grad, pm[...], pv[...])

            for p, (pw_arr, _, _, row0) in enumerate(packed):
                pw, pm, pv = pk_in[3 * p:3 * p + 3]
                update(p, pw, small_ref[row0:row0 + pw_arr.shape[0], :], pm, pv)
            for p in range(n_grp):
                pw, pg, pm, pv = grp_in[4 * p:4 * p + 4]
                update(n_pk + p, pw, pg[...], pm, pv)
            sq = small_ref[loss_rows[0]:loss_rows[0] + loss_rows[1], :]
            loss_ref[...] = jnp.sum(_colsum(sq), axis=1, keepdims=True) * (0.5 / d_model)

    tile = pl.BlockSpec((tm, d_model), lambda i: (i, 0))
    slab = pl.BlockSpec((tr, w.shape[1]), lambda i: (i, 0))
    flat = [a for pk in packed for a in pk[:3]] + [a for grp in groups for a in grp]
    shapes = [pk[0].shape for pk in packed] + [grp[0].shape for grp in groups]
    res = pl.pallas_call(
        body, name="grad_x_adamw", grid=(n_steps,),
        in_specs=[pl.BlockSpec((tm, N_DEV * d_model), lambda i: (i, 0)), ANY, tile, slab, slab, slab, slab]
        + [VMEM] * (1 + len(flat)),
        out_specs=[tile, slab, slab, slab, slab, VMEM] + [VMEM] * n_out,
        out_shape=[SDS((n_tok, d_model), F32)] + [SDS(w.shape, F32)] * 4 + [SDS((1, 1), F32)]
        + [SDS(shape, F32) for shape in shapes for _ in range(4)],
        scratch_shapes=[pltpu.VMEM(w_all.shape, BF16), pltpu.SemaphoreType.DMA((N_DEV,))],
        compiler_params=_params(1),
    )(dproj, w_all, dr, w, g, m, v, small, *flat)
    return (res[0], tuple(res[1:5]), [tuple(res[6 + 4 * p:10 + 4 * p]) for p in range(n_pk + n_grp)],
            res[5].reshape(()))


def _adamw_math(w, g, m, v):
    m = ADAM_B1 * m + (1.0 - ADAM_B1) * g
    v = ADAM_B2 * v + (1.0 - ADAM_B2) * (g * g)
    m_hat = m / (1.0 - ADAM_B1 ** ADAM_STEP)
    v_hat = v / (1.0 - ADAM_B2 ** ADAM_STEP)
    delta = -ADAM_LR * (m_hat / (jnp.sqrt(v_hat) + ADAM_EPS) + ADAM_WD * w)
    return delta, m, v


def _as_rows(a):
    return a.reshape(-1, LANES)


def kernel(x, w_in, b_in, conv_w, conv_b, gn_g, gn_b, ln_v_g, ln_v_b, w_spatial, b_spatial, w_pa, w_pb, w_o, b_o, ln_out_g, ln_out_b, loss_target, m_w_in, m_b_in, m_conv_w, m_conv_b, m_gn_g, m_gn_b, m_ln_v_g, m_ln_v_b, m_w_spatial, m_b_spatial, m_w_pa, m_w_pb, m_w_o, m_b_o, m_ln_out_g, m_ln_out_b, v_w_in, v_b_in, v_conv_w, v_conv_b, v_gn_g, v_gn_b, v_ln_v_g, v_ln_v_b, v_w_spatial, v_b_spatial, v_w_pa, v_w_pb, v_w_o, v_b_o, v_ln_out_g, v_ln_out_b):
    n_batch, seq, d_model = x.shape
    n_tok = n_batch * seq
    n_heads = d_model // LANES
    dc = conv_w.shape[1]
    me = 4 * lax.axis_index("x") + 2 * lax.axis_index("y") + lax.axis_index("c")
    row = lambda a: a.reshape(1, d_model)

    x2 = x.reshape(n_tok, d_model)
    target2 = loss_target.reshape(n_tok, d_model)
    b_spatial_t = b_spatial.T

    first = jnp.where(lax.axis_index("c") == 1, 4, 2)
    second = 6 - first
    ag_rel = jnp.stack([0 * first, 0 * first + 1, first, second + 1, second, first + 1, 0 * first + 6, 0 * first + 7])
    ag_blocks = jnp.bitwise_xor(me, ag_rel).astype(jnp.int32)
    proj, xt_bf, w_all, wp_all, cw_all = _proj_all_gather(
        x2, w_in, w_pa, w_pb, w_o, conv_w, b_in.reshape(N_DEV, 1, d_model), ag_blocks)
    wp_full = wp_all.reshape(3, d_model, d_model)
    conv_w_full = jnp.pad(cw_all.transpose(1, 0, 2).reshape(CONV_K, d_model), ((0, HALO - CONV_K), (0, 0)))

    h3, h1 = _branch_a_fwd(proj, conv_w_full, row(conv_b), row(gn_g), row(gn_b), seq)
    s = _branch_b_fwd(proj, row(ln_v_g), row(ln_v_b), w_spatial, b_spatial_t, seq)

    dproj, d_h3, d_s, dr, lhs3, rhs3, vec_mid = _mid(
        h3, s, proj, x2, target2, wp_full, row(b_o), row(ln_out_g), row(ln_out_b))

    dproj, vec_b, d_ws, d_bs_t = _branch_b_bwd(dproj, proj, d_s, row(ln_v_g), row(ln_v_b), w_spatial, b_spatial_t, seq)
    dproj, d_h1, vec_a1 = _branch_a_bwd_norm(dproj, proj, h1, d_h3, row(gn_g), row(gn_b), seq)
    gp_f32, gp_bf = _weight_grads(lhs3, rhs3)
    dproj, vec_a2, d_cw8, g_w_pa, g_w_pb, g_w_o = _branch_a_bwd_conv(
        dproj, proj, d_h1, conv_w_full, gp_bf, gp_f32, seq)

    pieces = [vec_a2, vec_a1, vec_b, vec_mid, d_bs_t.T, _as_rows(d_ws), d_cw8]
    a2, a1, vb, mid = 0, 1, 2, 3
    plan = ([("row", p, r) for p, r in [(a2, 0), (a2, 1), (a1, 0), (vb, 0), (vb, 1), (vb, 2), (mid, 3), (mid, 4)]]
            + [("row", p, r) for p, r in [(a1, 3), (a1, 1), (a1, 2), (vb, 3), (vb, 4), (mid, 2), (mid, 0), (mid, 1)]]
            + [("block", 4, None), ("block", 5, None), ("tiles", 6, None), ("row", mid, 5)])

    rs_rel = jnp.stack([0 * first + 7, 0 * first + 6, first + 1, second, second + 1, first, 0 * first + 1, 0 * first])
    rs_blocks = jnp.bitwise_xor(me, rs_rel).astype(jnp.int32)
    g_w_in, small = _grad_w_in_reduce_scatter(xt_bf, dproj, rs_blocks, pieces, plan)

    g_rows = d_model // LANES
    o0 = N_DEV * g_rows
    o1 = o0 + 8 * g_rows
    o2 = o1 + n_heads
    o3 = o2 + n_heads * LANES
    o4 = o3 + n_heads * HALO
    g_cw_full = small[o3:o4].reshape(n_heads, HALO, LANES).transpose(1, 0, 2).reshape(HALO, d_model)
    g_conv_w = lax.dynamic_slice(g_cw_full, (0, me * dc), (CONV_K, dc))

    two_d = lambda a: a.reshape(-1, a.shape[-1]) if a.ndim != 1 else (
        a.reshape(-1, LANES) if a.shape[0] % LANES == 0 else a.reshape(1, -1))
    names = ["b_in", "conv_w", "conv_b", "gn_g", "gn_b", "ln_v_g", "ln_v_b", "w_spatial", "b_spatial",
             "w_pa", "w_pb", "w_o", "b_o", "ln_out_g", "ln_out_b"]
    ws = dict(b_in=b_in, conv_w=conv_w, conv_b=conv_b, gn_g=gn_g, gn_b=gn_b, ln_v_g=ln_v_g, ln_v_b=ln_v_b,
              w_spatial=w_spatial, b_spatial=b_spatial, w_pa=w_pa, w_pb=w_pb, w_o=w_o, b_o=b_o,
              ln_out_g=ln_out_g, ln_out_b=ln_out_b)
    ms = dict(b_in=m_b_in, conv_w=m_conv_w, conv_b=m_conv_b, gn_g=m_gn_g, gn_b=m_gn_b, ln_v_g=m_ln_v_g,
              ln_v_b=m_ln_v_b, w_spatial=m_w_spatial, b_spatial=m_b_spatial, w_pa=m_w_pa, w_pb=m_w_pb,
              w_o=m_w_o, b_o=m_b_o, ln_out_g=m_ln_out_g, ln_out_b=m_ln_out_b)
    vs = dict(b_in=v_b_in, conv_w=v_conv_w, conv_b=v_conv_b, gn_g=v_gn_g, gn_b=v_gn_b, ln_v_g=v_ln_v_g,
              ln_v_b=v_ln_v_b, w_spatial=v_w_spatial, b_spatial=v_b_spatial, w_pa=v_w_pa, w_pb=v_w_pb,
              w_o=v_w_o, b_o=v_b_o, ln_out_g=v_ln_out_g, ln_out_b=v_ln_out_b)
    vec_names = ["conv_b", "gn_g", "gn_b", "ln_v_g", "ln_v_b", "b_o", "ln_out_g", "ln_out_b"]
    first_row = dict(b_in=0, b_spatial=o1, w_spatial=o2, **{n: o0 + a * g_rows for a, n in enumerate(vec_names)})
    given = dict(conv_w=g_conv_w, w_pa=g_w_pa, w_pb=g_w_pb, w_o=g_w_o)
    grad_x, upd_w_in, upd, loss = _grad_x_adamw(
        dproj, w_all, dr, w_in, g_w_in, m_w_in, v_w_in, small, (o4, g_rows),
        [(two_d(ws[n]), two_d(ms[n]), two_d(vs[n]), row0) for n, row0 in first_row.items()],
        [tuple(two_d(a) for a in (ws[n], grad, ms[n], vs[n])) for n, grad in given.items()])
    grad_x = grad_x.reshape(x.shape)
    res = {n: tuple(a.reshape(ws[n].shape) for a in u) for n, u in zip([*first_row, *given], upd)}
    res["w_in"] = upd_w_in

    order = ["w_in"] + names
    return (loss, grad_x, *[res[n][k] for k in range(4) for n in order])
```

```python
import jax
import jax.numpy as jnp
from jax import lax
from jax.experimental import pallas as pl
from jax.experimental.pallas import tpu as pltpu

F32 = jnp.float32
BF16 = jnp.bfloat16
SDS = jax.ShapeDtypeStruct

N_DEV = 8
LANES = 128
SUBLANES = 8
CONV_K = 31
HALO = 32
ELEMENTWISE_ROWS = 512
LN_EPS = 1e-5
DEEPNORM_ALPHA = 2.0 ** 0.25
ADAM_LR, ADAM_B1, ADAM_B2, ADAM_EPS, ADAM_WD, ADAM_STEP = 0.001, 0.9, 0.999, 1e-08, 0.01, 10
GELU_C = 0.7978845608028654
GELU_A = 0.044715
VMEM_LIMIT = 56 * 1024 * 1024
MESH = pl.DeviceIdType.MESH
ANY = pl.BlockSpec(memory_space=pl.ANY)
VMEM = pl.BlockSpec(memory_space=pltpu.VMEM)


def _params(n_grid=0):
    sem = ("arbitrary",) * n_grid if n_grid else None
    return pltpu.CompilerParams(dimension_semantics=sem, vmem_limit_bytes=VMEM_LIMIT)


def _tile(n, pref, mult):
    t = min(n, pref)
    while n % t or t % mult:
        t -= 1
    return t


def _colsum(v):
    return jnp.sum(v, axis=0, keepdims=True)


def _sigmoid(v):
    return jax.nn.sigmoid(v)


def _silu_and_grad(v):
    s = _sigmoid(v)
    val = v * s
    return val, s + val * (1.0 - s)


def _gelu_and_grad(v):
    v2 = v * v
    sg = _sigmoid(v * (2.0 * GELU_C + (2.0 * GELU_C * GELU_A) * v2))
    grad = sg + v * sg * (1.0 - sg) * (2.0 * GELU_C + (6.0 * GELU_C * GELU_A) * v2)
    return v * sg, grad


def _tril_mask():
    r = lax.broadcasted_iota(jnp.int32, (LANES, LANES), 0)
    c = lax.broadcasted_iota(jnp.int32, (LANES, LANES), 1)
    return c <= r


def _dot(a, b):
    return jnp.dot(a, b, preferred_element_type=F32)


def _dot_tb(a, b):
    return lax.dot_general(a, b, (((1,), (1,)), ((), ())), preferred_element_type=F32)


def _dot_ta(a, b):
    return lax.dot_general(a, b, (((0,), (0,)), ((), ())), preferred_element_type=F32)


def _mesh_pos():
    return lax.axis_index("x"), lax.axis_index("y"), lax.axis_index("c")


def _block_of(pos):
    return 4 * pos[0] + 2 * pos[1] + pos[2]


def _peers():
    x, y, c = _mesh_pos()
    out = []
    for k in range(1, N_DEV):
        pos = (1 - x if k & 4 else x, 1 - y if k & 2 else y, 1 - c if k & 1 else c)
        out.append((pos, _block_of(pos)))
    return out


def _proj_all_gather(x2, w_in, w_pa, w_pb, w_o, conv_w, b_in3, blk_order):
    n_tok, d_model = x2.shape
    r8 = w_pa.shape[0]
    kc, dc = conv_w.shape
    tm = _tile(n_tok, 1024, LANES)
    n_t = n_tok // tm
    n_arr = 4
    w_halves = (0, 3)

    def body(ord_ref, x_ref, b_ref, win_ref, wpa_ref, wpb_ref, wo_ref, cw_ref,
             proj_ref, xt_ref, wall_ref, wp_ref, cwall_ref,
             wbuf, xbuf, st_p, send_sems, recv_sems, local_sems, wall_sems):
        s = pl.program_id(0)
        t = pl.program_id(1)
        x, y, c = _mesh_pos()
        me = (x, y, c)
        sibling = (x, y, 1 - c)
        n1 = (jnp.bitwise_xor(x, c), jnp.bitwise_xor(y, 1 - c))
        n2 = (jnp.bitwise_xor(x, 1 - c), jnp.bitwise_xor(y, c))
        dg = (1 - x, 1 - y)
        outs = [wbuf, wp_ref, cwall_ref, wbuf]
        srcs = [None, st_p, cw_ref, None]
        consumed = [me, sibling, (*n1, c), (*n2, 1 - c), (*n2, c), (*n1, 1 - c), (*dg, c), (*dg, 1 - c)]
        leaves = [(me, sibling), (me, (*n1, c)), (me, (*n2, c)), ((*n1, c), (*n2, c)),
                  ((*n1, c), sibling), ((*n2, c), sibling), ((*dg, c), sibling)]
        lands = [sibling, (*n1, c), (*n2, c), (*dg, c), (*n2, 1 - c), (*n1, 1 - c), (*dg, 1 - c)]

        def slot(o, pos):
            return o.at[:, _block_of(pos)] if o is wp_ref else o.at[_block_of(pos)]

        def rows_of(a, ref):
            half = d_model // 2
            return ref.at[pl.ds(w_halves.index(a) * half, half)] if a in w_halves else ref

        def copy(a, k, block, to, src=None):
            dst = rows_of(a, slot(outs[a], block))
            return pltpu.make_async_remote_copy(
                src_ref=dst if src is None else src, dst_ref=dst,
                send_sem=send_sems.at[a, k], recv_sem=recv_sems.at[a, k],
                device_id=to, device_id_type=MESH)

        def send(a, k):
            block, to = leaves[k]
            return copy(a, k, block, to, src=srcs[a] if k < 3 else None)

        def recv(a, k):
            return copy(a, k, lands[k], me)

        def local_copies():
            return [pltpu.make_async_copy(srcs[a], slot(outs[a], me), local_sems.at[a]) for a in (1, 2)]

        def to_hbm(step):
            return pltpu.make_async_copy(slot(wbuf, consumed[step]), slot(wall_ref, consumed[step]),
                                         wall_sems.at[step])

        def at_step(step):
            return pl.when(jnp.logical_and(s == step, t == 0))

        @at_step(0)
        def _():
            slot(wbuf, me)[...] = win_ref[...].astype(BF16)
            st_p[0] = wpa_ref[...].astype(BF16)
            st_p[1] = wpb_ref[...].astype(BF16)
            st_p[2] = wo_ref[...].astype(BF16)
            for a in (*w_halves, 1, 2):
                send(a, 0).start()
                send(a, 1).start()
            for cp in local_copies():
                cp.start()
            to_hbm(0).start()

        @at_step(1)
        def _():
            for a in w_halves:
                recv(a, 0).wait_recv()
            to_hbm(1).start()

        for rnd in range(3):
            @at_step(2 + 2 * rnd)
            def _(rnd=rnd):
                if rnd == 0:
                    for a in (*w_halves, 1, 2):
                        send(a, 2).start()
                for a in w_halves:
                    recv(a, 1 + rnd).wait_recv()
                    if rnd == 0:
                        send(a, 3).start()
                    send(a, 4 + rnd).start()
                to_hbm(2 + 2 * rnd).start()

            @at_step(3 + 2 * rnd)
            def _(rnd=rnd):
                for a in (1, 2):
                    recv(a, 1 + rnd).wait_recv()
                    if rnd == 0:
                        send(a, 3).start()
                    send(a, 4 + rnd).start()
                for a in w_halves:
                    recv(a, 4 + rnd).wait_recv()
                to_hbm(3 + 2 * rnd).start()

        rows = pl.ds(pl.multiple_of(t * tm, tm), tm)

        @pl.when(s == 0)
        def _():
            xb = x_ref[...].astype(BF16)
            xbuf[rows, :] = xb
            xt_ref[...] = xb.T

        proj_ref[...] = _dot(xbuf[rows, :], wbuf[ord_ref[s]]) + b_ref[...]

        @pl.when(jnp.logical_and(s == N_DEV - 1, t == n_t - 1))
        def _():
            for a in (1, 2):
                for k in (0, 4, 5, 6):
                    recv(a, k).wait_recv()
            for a in range(n_arr):
                for k in range(7):
                    send(a, k).wait_send()
            for cp in local_copies() + [to_hbm(step) for step in range(N_DEV)]:
                cp.wait()

    grid_spec = pltpu.PrefetchScalarGridSpec(
        num_scalar_prefetch=1, grid=(N_DEV, n_t),
        in_specs=[pl.BlockSpec((tm, d_model), lambda s, t, o: (jnp.where(s == 0, t, n_t - 1), 0)),
                  pl.BlockSpec((None, 1, d_model), lambda s, t, o: (o[s], 0, 0)),
                  VMEM, VMEM, VMEM, VMEM, VMEM],
        out_specs=[pl.BlockSpec((tm, d_model), lambda s, t, o: (t, o[s])),
                   pl.BlockSpec((d_model, tm), lambda s, t, o: (0, jnp.where(s == 0, t, n_t))),
                   ANY, ANY, ANY],
        scratch_shapes=[pltpu.VMEM((N_DEV, d_model, d_model), BF16), pltpu.VMEM((n_tok, d_model), BF16),
                        pltpu.VMEM((3, r8, d_model), BF16),
                        pltpu.SemaphoreType.DMA((n_arr, 7)), pltpu.SemaphoreType.DMA((n_arr, 7)),
                        pltpu.SemaphoreType.DMA((3,)), pltpu.SemaphoreType.DMA((N_DEV,))])
    return pl.pallas_call(
        body, name="proj_all_gather", grid_spec=grid_spec,
        out_shape=[SDS((n_tok, N_DEV * d_model), F32), SDS((d_model, (n_t + 1) * tm), BF16),
                   SDS((N_DEV, d_model, d_model), BF16), SDS((3, N_DEV, r8, d_model), BF16),
                   SDS((N_DEV, kc, dc), F32)],
        compiler_params=_params(2),
    )(blk_order, x2, b_in3, w_in, w_pa, w_pb, w_o, conv_w)


def _conv_rows(ta):
    return _tile(ta, 64, SUBLANES)


def _branch_a_fwd(proj, conv_w_full, conv_b, gn_g, gn_b, seq):
    n_tok = proj.shape[0]
    d_model = conv_b.shape[1]
    ta = _tile(seq, ELEMENTWISE_ROWS, HALO)
    per_seq = seq // ta
    rc = _conv_rows(ta)

    def body(av_ref, ag_ref, gt_ref, avh_ref, agh_ref, cw_ref, cb_ref, gg_ref, gb_ref,
             h3_ref, h1_ref, ext):
        keep = jnp.where(pl.program_id(0) % per_seq == 0, 0.0, 1.0)

        def group(g, carry):
            sl = pl.ds(pl.multiple_of(g * LANES, LANES), LANES)
            ext[0:HALO, :] = avh_ref[:, sl] * _sigmoid(agh_ref[:, sl]) * keep
            ext[HALO:HALO + ta, :] = av_ref[:, sl] * _sigmoid(ag_ref[:, sl])
            for r0 in range(0, ta, rc):
                acc = jnp.broadcast_to(cb_ref[:, sl], (rc, LANES))
                for k in range(CONV_K):
                    acc = acc + ext[pl.ds(r0 + HALO - (CONV_K - 1) + k, rc), :] * cw_ref[k:k + 1, sl]
                h1_ref[pl.ds(r0, rc), sl] = acc
            h1 = h1_ref[:, sl]
            mu = jnp.mean(h1, axis=-1, keepdims=True)
            dlt = h1 - mu
            var = jnp.mean(dlt * dlt, axis=-1, keepdims=True)
            h2 = dlt * lax.rsqrt(var + LN_EPS) * gg_ref[:, sl] + gb_ref[:, sl]
            gate = gt_ref[:, sl]
            h3_ref[:, sl] = (h2 * _sigmoid(h2) * gate * _sigmoid(gate)).astype(BF16)
            return carry

        lax.fori_loop(0, d_model // LANES, group, 0, unroll=True)

    blk = lambda j: pl.BlockSpec((ta, d_model), lambda i: (i, j))
    halo = lambda j: pl.BlockSpec((HALO, d_model), lambda i: (jnp.maximum(i * (ta // HALO) - 1, 0), j))
    row = pl.BlockSpec((1, d_model), lambda i: (0, 0))
    return pl.pallas_call(
        body, name="branch_a_fwd", grid=(n_tok // ta,),
        in_specs=[blk(0), blk(1), blk(2), halo(0), halo(1),
                  pl.BlockSpec((HALO, d_model), lambda i: (0, 0)), row, row, row],
        out_specs=[pl.BlockSpec((ta, d_model), lambda i: (i, 0))] * 2,
        out_shape=[SDS((n_tok, d_model), BF16), SDS((n_tok, d_model), F32)],
        scratch_shapes=[pltpu.VMEM((HALO + ta, LANES), F32)],
        compiler_params=_params(1),
    )(proj, proj, proj, proj, proj, conv_w_full, conv_b, gn_g, gn_b)


def _branch_b_fwd(proj, ln_g, ln_b, w_spatial, b_spatial_t, seq):
    n_tok = proj.shape[0]
    d_model = ln_g.shape[1]
    n_heads = d_model // LANES
    tb = _tile(seq, ELEMENTWISE_ROWS, LANES)

    def body(u_ref, v_ref, bg_ref, lg_ref, lb_ref, ws_ref, bs_ref, s_ref, vn_buf):
        v, _ = _gelu_and_grad(v_ref[...])
        mu = jnp.mean(v, axis=-1, keepdims=True)
        dlt = v - mu
        var = jnp.mean(dlt * dlt, axis=-1, keepdims=True)
        vn_buf[...] = (dlt * lax.rsqrt(var + LN_EPS) * lg_ref[...] + lb_ref[...]).astype(BF16)
        tril = _tril_mask()
        for h in range(n_heads):
            cols = slice(h * LANES, (h + 1) * LANES)
            w_h = jnp.where(tril, ws_ref[h], 0.0).astype(BF16)
            bias = bs_ref[:, h:h + 1]
            for ch in range(tb // LANES):
                rows = slice(ch * LANES, (ch + 1) * LANES)
                mix = _dot(w_h, vn_buf[rows, cols]) + bias
                u, _ = _gelu_and_grad(u_ref[rows, cols])
                gate = bg_ref[rows, cols]
                s_ref[rows, cols] = (u * mix * gate * _sigmoid(gate)).astype(BF16)

    blk = lambda j: pl.BlockSpec((tb, d_model), lambda i: (i, j))
    row = pl.BlockSpec((1, d_model), lambda i: (0, 0))
    return pl.pallas_call(
        body, name="branch_b_fwd", grid=(n_tok // tb,),
        in_specs=[blk(3), blk(4), blk(5), row, row,
                  pl.BlockSpec((n_heads, LANES, LANES), lambda i: (0, 0, 0)),
                  pl.BlockSpec((LANES, n_heads), lambda i: (0, 0))],
        out_specs=pl.BlockSpec((tb, d_model), lambda i: (i, 0)),
        out_shape=SDS((n_tok, d_model), BF16),
        scratch_shapes=[pltpu.VMEM((tb, d_model), BF16)],
        compiler_params=_params(1),
    )(proj, proj, proj, ln_g, ln_b, w_spatial, b_spatial_t)


MID_ROWS = 8


def _mid(h3, s, proj, x2, target, wp_full, b_o, lo_g, lo_b):
    n_tok, d_model = x2.shape
    tm = _tile(n_tok, 256, 16)

    def body(h3_ref, s_ref, ma_ref, mb_ref, x_ref, t_ref, wpa_ref, wpb_ref, wo_ref, bo_ref,
             lg_ref, lb_ref, dproj_ref, dh3_ref, ds_ref, dr_ref, lhs3_ref, rhs3_ref, vec_ref):
        @pl.when(pl.program_id(0) == 0)
        def _():
            vec_ref[...] = jnp.zeros_like(vec_ref)

        h3 = h3_ref[...]
        s = s_ref[...]
        ya = _dot(h3, wpa_ref[...])
        yb = _dot(s, wpb_ref[...])
        ga = _sigmoid(ma_ref[...])
        gb = _sigmoid(mb_ref[...])
        mixed = (ga * ya + gb * yb).astype(BF16)
        lhs3_ref[0] = h3
        lhs3_ref[1] = s
        lhs3_ref[2] = mixed
        r = DEEPNORM_ALPHA * x_ref[...] + _dot(mixed, wo_ref[...]) + bo_ref[...]
        mu = jnp.mean(r, axis=-1, keepdims=True)
        dlt = r - mu
        rstd = lax.rsqrt(jnp.mean(dlt * dlt, axis=-1, keepdims=True) + LN_EPS)
        rhat = dlt * rstd
        diff = rhat * lg_ref[...] + lb_ref[...] - t_ref[...]
        dy = diff * (1.0 / d_model)
        vec_ref[0:1, :] += _colsum(dy * rhat)
        vec_ref[1:2, :] += _colsum(dy)
        vec_ref[5:6, :] += _colsum(diff * diff)
        drh = dy * lg_ref[...]
        dr = rstd * (drh - jnp.mean(drh, axis=-1, keepdims=True)
                     - rhat * jnp.mean(drh * rhat, axis=-1, keepdims=True))
        vec_ref[2:3, :] += _colsum(dr)
        dr_ref[...] = dr
        drb = dr.astype(BF16)
        rhs3_ref[2] = drb
        dmixed = _dot_tb(drb, wo_ref[...])
        dya_f = dmixed * ga
        dyb_f = dmixed * gb
        dma = dya_f * ya * (1.0 - ga)
        dmb = dyb_f * yb * (1.0 - gb)
        vec_ref[3:4, :] += _colsum(dma)
        vec_ref[4:5, :] += _colsum(dmb)
        dproj_ref[:, 0:d_model] = dma.astype(BF16)
        dproj_ref[:, d_model:2 * d_model] = dmb.astype(BF16)
        dya = dya_f.astype(BF16)
        dyb = dyb_f.astype(BF16)
        rhs3_ref[0] = dya
        rhs3_ref[1] = dyb
        dh3_ref[...] = _dot_tb(dya, wpa_ref[...])
        ds_ref[...] = _dot_tb(dyb, wpb_ref[...])

    tile = pl.BlockSpec((tm, d_model), lambda i: (i, 0))
    full = lambda a: pl.BlockSpec((None, d_model, d_model), lambda i: (a, 0, 0))
    row = pl.BlockSpec((1, d_model), lambda i: (0, 0))
    stack = pl.BlockSpec((3, tm, d_model), lambda i: (0, i, 0))
    bf3 = SDS((3, n_tok, d_model), BF16)
    f32 = SDS((n_tok, d_model), F32)
    return pl.pallas_call(
        body, name="mid", grid=(n_tok // tm,),
        in_specs=[tile, tile, pl.BlockSpec((tm, d_model), lambda i: (i, 6)),
                  pl.BlockSpec((tm, d_model), lambda i: (i, 7)), tile, tile, full(0), full(1), full(2),
                  row, row, row],
        out_specs=[pl.BlockSpec((tm, 2 * d_model), lambda i: (i, 3)), tile, tile, tile, stack, stack,
                   pl.BlockSpec((MID_ROWS, d_model), lambda i: (0, 0))],
        out_shape=[SDS((n_tok, N_DEV * d_model), BF16), f32, f32, f32, bf3, bf3,
                   SDS((MID_ROWS, d_model), F32)],
        compiler_params=_params(1),
    )(h3, s, proj, proj, x2, target, wp_full, wp_full, wp_full, b_o, lo_g, lo_b)


B_ROWS = 8


def _branch_b_bwd(dproj, proj, d_s, ln_g, ln_b, w_spatial, b_spatial_t, seq):
    n_tok = proj.shape[0]
    d_model = ln_g.shape[1]
    n_heads = d_model // LANES
    tb = _tile(seq, ELEMENTWISE_ROWS, LANES)

    def body(dproj_in, u_ref, v_ref, bg_ref, ds_ref, lg_ref, lb_ref, ws_ref, bs_ref,
             dproj_ref, vec_ref, dws_ref, dbs_ref, vn_buf, dv_buf):
        del dproj_in

        @pl.when(pl.program_id(0) == 0)
        def _():
            vec_ref[...] = jnp.zeros_like(vec_ref)
            dws_ref[...] = jnp.zeros_like(dws_ref)
            dbs_ref[...] = jnp.zeros_like(dbs_ref)

        v, dgelu_v = _gelu_and_grad(v_ref[...])
        mu = jnp.mean(v, axis=-1, keepdims=True)
        dlt = v - mu
        rstd = lax.rsqrt(jnp.mean(dlt * dlt, axis=-1, keepdims=True) + LN_EPS)
        vhat = dlt * rstd
        vn_buf[...] = (vhat * lg_ref[...] + lb_ref[...]).astype(BF16)
        tril = _tril_mask()
        for h in range(n_heads):
            cols = slice(h * LANES, (h + 1) * LANES)
            w_h = jnp.where(tril, ws_ref[h], 0.0).astype(BF16)
            bias = bs_ref[:, h:h + 1]
            for ch in range(tb // LANES):
                rows = slice(ch * LANES, (ch + 1) * LANES)
                vn = vn_buf[rows, cols]
                mix = _dot(w_h, vn) + bias
                u, dgelu_u = _gelu_and_grad(u_ref[rows, cols])
                sg, dsilu = _silu_and_grad(bg_ref[rows, cols])
                dsv = ds_ref[rows, cols]
                du = dsv * mix * sg * dgelu_u
                dbg = dsv * u * mix * dsilu
                dmix = dsv * u * sg
                dmix_bf = dmix.astype(BF16)
                dproj_ref[rows, cols] = du.astype(BF16)
                dproj_ref[rows, 2 * d_model + h * LANES:2 * d_model + (h + 1) * LANES] = dbg.astype(BF16)
                vec_ref[0:1, cols] += _colsum(du)
                vec_ref[2:3, cols] += _colsum(dbg)
                dbs_ref[:, h:h + 1] += jnp.sum(dmix, axis=1, keepdims=True)
                dws_ref[h] += jnp.where(tril, _dot_tb(dmix_bf, vn), 0.0)
                dv_buf[rows, cols] = _dot_ta(w_h, dmix_bf)
        dvn = dv_buf[...]
        vec_ref[3:4, :] += _colsum(dvn * vhat)
        vec_ref[4:5, :] += _colsum(dvn)
        dvh = dvn * lg_ref[...]
        dv = rstd * (dvh - jnp.mean(dvh, axis=-1, keepdims=True)
                     - vhat * jnp.mean(dvh * vhat, axis=-1, keepdims=True)) * dgelu_v
        vec_ref[1:2, :] += _colsum(dv)
        dproj_ref[:, d_model:2 * d_model] = dv.astype(BF16)

    blk = lambda j: pl.BlockSpec((tb, d_model), lambda i: (i, j))
    row = pl.BlockSpec((1, d_model), lambda i: (0, 0))
    return pl.pallas_call(
        body, name="branch_b_bwd", grid=(n_tok // tb,),
        in_specs=[ANY, blk(3), blk(4), blk(5), pl.BlockSpec((tb, d_model), lambda i: (i, 0)), row, row,
                  pl.BlockSpec((n_heads, LANES, LANES), lambda i: (0, 0, 0)),
                  pl.BlockSpec((LANES, n_heads), lambda i: (0, 0))],
        out_specs=[pl.BlockSpec((tb, 3 * d_model), lambda i: (i, 1)),
                   pl.BlockSpec((B_ROWS, d_model), lambda i: (0, 0)),
                   pl.BlockSpec((n_heads, LANES, LANES), lambda i: (0, 0, 0)),
                   pl.BlockSpec((LANES, n_heads), lambda i: (0, 0))],
        out_shape=[SDS(dproj.shape, BF16), SDS((B_ROWS, d_model), F32),
                   SDS((n_heads, LANES, LANES), F32), SDS((LANES, n_heads), F32)],
        scratch_shapes=[pltpu.VMEM((tb, d_model), BF16), pltpu.VMEM((tb, d_model), F32)],
        input_output_aliases={0: 0},
        compiler_params=_params(1),
    )(dproj, proj, proj, proj, d_s, ln_g, ln_b, w_spatial, b_spatial_t)


A1_ROWS = 8


def _branch_a_bwd_norm(dproj, proj, h1, d_h3, gn_g, gn_b, seq):
    n_tok = proj.shape[0]
    d_model = gn_g.shape[1]
    ta = _tile(seq, ELEMENTWISE_ROWS, 16)

    def body(dproj_in, gt_ref, h1_ref, dh3_ref, gg_ref, gb_ref, dproj_ref, dh1_ref, vec_ref):
        del dproj_in

        @pl.when(pl.program_id(0) == 0)
        def _():
            vec_ref[...] = jnp.zeros_like(vec_ref)

        def group(g, carry):
            sl = pl.ds(pl.multiple_of(g * LANES, LANES), LANES)
            h1 = h1_ref[:, sl]
            mu = jnp.mean(h1, axis=-1, keepdims=True)
            dlt = h1 - mu
            rstd = lax.rsqrt(jnp.mean(dlt * dlt, axis=-1, keepdims=True) + LN_EPS)
            nrm = dlt * rstd
            sw, dsw = _silu_and_grad(nrm * gg_ref[:, sl] + gb_ref[:, sl])
            sg, dsg = _silu_and_grad(gt_ref[:, sl])
            dh3 = dh3_ref[:, sl]
            dgate = dh3 * sw * dsg
            dproj_ref[:, sl] = dgate.astype(BF16)
            vec_ref[0:1, sl] += _colsum(dgate)
            dh2 = dh3 * sg * dsw
            vec_ref[1:2, sl] += _colsum(dh2 * nrm)
            vec_ref[2:3, sl] += _colsum(dh2)
            dn = dh2 * gg_ref[:, sl]
            dh1 = rstd * (dn - jnp.mean(dn, axis=-1, keepdims=True)
                          - nrm * jnp.mean(dn * nrm, axis=-1, keepdims=True))
            vec_ref[3:4, sl] += _colsum(dh1)
            dh1_ref[:, sl] = dh1
            return carry

        lax.fori_loop(0, d_model // LANES, group, 0, unroll=True)

    tile = pl.BlockSpec((ta, d_model), lambda i: (i, 0))
    row = pl.BlockSpec((1, d_model), lambda i: (0, 0))
    return pl.pallas_call(
        body, name="branch_a_bwd_norm", grid=(n_tok // ta,),
        in_specs=[ANY, pl.BlockSpec((ta, d_model), lambda i: (i, 2)), tile, tile, row, row],
        out_specs=[pl.BlockSpec((ta, d_model), lambda i: (i, 2)), tile,
                   pl.BlockSpec((A1_ROWS, d_model), lambda i: (0, 0))],
        out_shape=[SDS(dproj.shape, BF16), SDS((n_tok, d_model), F32), SDS((A1_ROWS, d_model), F32)],
        input_output_aliases={0: 0},
        compiler_params=_params(1),
    )(dproj, proj, h1, d_h3, gn_g, gn_b)


A2_ROWS = 8


def _branch_a_bwd_conv(dproj, proj, d_h1, conv_w_full, gp_bf, gp_f32, seq):
    n_tok = proj.shape[0]
    d_model = conv_w_full.shape[1]
    n_groups = d_model // LANES
    ta = _tile(seq, ELEMENTWISE_ROWS, HALO)
    n_tiles = n_tok // ta
    per_seq = seq // ta
    rc = _conv_rows(ta)
    last_halo = n_tok // HALO - 1
    r8 = d_model // N_DEV
    prow = _tile(r8, 32, 16)

    def body(dproj_in, av_ref, ag_ref, avh_ref, agh_ref, dh1_ref, dh1h_ref, cw_ref, gp_bf_ref, gp_f32_ref,
             dproj_ref, vec_ref, dcw_ref, opa_ref, opb_ref, opo_ref,
             ext_h0, ext_d, rbuf, own, send_sems, recv_sems, local_sems):
        del dproj_in
        i = pl.program_id(0)
        x, y, c = _mesh_pos()
        me = _block_of((x, y, c))
        peers = _peers()

        def sends():
            return [pltpu.make_async_remote_copy(
                src_ref=gp_bf_ref.at[a, pl.ds(pl.multiple_of(blk * r8, 16), r8), :], dst_ref=rbuf.at[k, a],
                send_sem=send_sems.at[a, k], recv_sem=recv_sems.at[a, k], device_id=pos, device_id_type=MESH)
                for k, (pos, blk) in enumerate(peers) for a in range(3)]

        def own_rows():
            return [pltpu.make_async_copy(gp_f32_ref.at[a, pl.ds(pl.multiple_of(me * r8, 8), r8), :],
                                          own.at[a], local_sems.at[a]) for a in range(3)]

        @pl.when(i == 0)
        def _():
            vec_ref[...] = jnp.zeros_like(vec_ref)
            dcw_ref[...] = jnp.zeros_like(dcw_ref)
            for cp in sends() + own_rows():
                cp.start()

        keep_past = jnp.where(i % per_seq == 0, 0.0, 1.0)
        keep_next = jnp.where(i % per_seq == per_seq - 1, 0.0, 1.0)

        def group(g, carry):
            sl = pl.ds(pl.multiple_of(g * LANES, LANES), LANES)
            av = av_ref[:, sl]
            sig = _sigmoid(ag_ref[:, sl])
            ext_h0[0:HALO, :] = avh_ref[:, sl] * _sigmoid(agh_ref[:, sl]) * keep_past
            ext_h0[HALO:HALO + ta, :] = av * sig
            ext_d[0:ta, :] = dh1_ref[:, sl]
            ext_d[ta:ta + HALO, :] = dh1h_ref[:, sl] * keep_next
            for r0 in range(0, ta, rc):
                dh1 = ext_d[pl.ds(r0, rc), :]
                acc = jnp.zeros((rc, LANES), F32)
                for k in range(CONV_K):
                    acc = acc + ext_d[pl.ds(r0 + CONV_K - 1 - k, rc), :] * cw_ref[k:k + 1, sl]
                    prod = dh1 * ext_h0[pl.ds(r0 + HALO - (CONV_K - 1) + k, rc), :]
                    dcw_ref[g, k] += jnp.sum(prod.reshape(rc // SUBLANES, SUBLANES, LANES), axis=0)
                rows = pl.ds(r0, rc)
                sig_r = sig[r0:r0 + rc]
                dav = acc * sig_r
                dag = dav * av[r0:r0 + rc] * (1.0 - sig_r)
                dproj_ref[rows, sl] = dav.astype(BF16)
                dproj_ref[rows, pl.ds(pl.multiple_of(d_model + g * LANES, LANES), LANES)] = dag.astype(BF16)
                vec_ref[0:1, sl] += _colsum(dav)
                vec_ref[1:2, sl] += _colsum(dag)
            return carry

        lax.fori_loop(0, n_groups, group, 0, unroll=True)

        @pl.when(i == n_tiles - 1)
        def _():
            for cp in own_rows():
                cp.wait()
            for cp in sends():
                cp.wait_recv()
            for a, o in enumerate([opa_ref, opb_ref, opo_ref]):
                for q in range(r8 // prow):
                    r = pl.ds(q * prow, prow)
                    tot = own[a, r, :]
                    for k in range(N_DEV - 1):
                        tot = tot + rbuf[k, a, r, :].astype(F32)
                    o[r, :] = tot
            for cp in sends():
                cp.wait_send()

    blk = lambda j: pl.BlockSpec((ta, d_model), lambda i: (i, j))
    halo = lambda j: pl.BlockSpec((HALO, d_model), lambda i: (jnp.maximum(i * (ta // HALO) - 1, 0), j))
    shard = pl.BlockSpec((r8, d_model), lambda i: (0, 0))
    return pl.pallas_call(
        body, name="branch_a_bwd_conv", grid=(n_tiles,),
        in_specs=[ANY, blk(0), blk(1), halo(0), halo(1), pl.BlockSpec((ta, d_model), lambda i: (i, 0)),
                  pl.BlockSpec((HALO, d_model), lambda i: (jnp.minimum((i + 1) * (ta // HALO), last_halo), 0)),
                  pl.BlockSpec((HALO, d_model), lambda i: (0, 0)), ANY, ANY],
        out_specs=[pl.BlockSpec((ta, 2 * d_model), lambda i: (i, 0)),
                   pl.BlockSpec((A2_ROWS, d_model), lambda i: (0, 0)),
                   pl.BlockSpec((n_groups, HALO, SUBLANES, LANES), lambda i: (0, 0, 0, 0)),
                   shard, shard, shard],
        out_shape=[SDS(dproj.shape, BF16), SDS((A2_ROWS, d_model), F32),
                   SDS((n_groups, HALO, SUBLANES, LANES), F32)] + [SDS((r8, d_model), F32)] * 3,
        scratch_shapes=[pltpu.VMEM((HALO + ta, LANES), F32), pltpu.VMEM((ta + HALO, LANES), F32),
                        pltpu.VMEM((N_DEV - 1, 3, r8, d_model), BF16), pltpu.VMEM((3, r8, d_model), F32),
                        pltpu.SemaphoreType.DMA((3, 7)), pltpu.SemaphoreType.DMA((3, 7)),
                        pltpu.SemaphoreType.DMA((3,))],
        input_output_aliases={0: 0},
        compiler_params=_params(1),
    )(dproj, proj, proj, proj, proj, d_h1, d_h1, conv_w_full, gp_bf, gp_f32)


def _weight_grads(lhs3, rhs3):
    n_mat, n_tok, d_model = lhs3.shape
    tk = _tile(n_tok, 2048, 16)
    n_k = n_tok // tk

    def body(a_ref, g_ref, o_ref, ob_ref):
        part = _dot_ta(a_ref[...], g_ref[...])

        @pl.when(pl.program_id(1) == 0)
        def _():
            o_ref[...] = part

        @pl.when(pl.program_id(1) != 0)
        def _():
            o_ref[...] += part

        @pl.when(pl.program_id(1) == n_k - 1)
        def _():
            ob_ref[...] = o_ref[...].astype(BF16)

    tile = pl.BlockSpec((None, tk, d_model), lambda a, i: (a, i, 0))
    out = pl.BlockSpec((None, d_model, d_model), lambda a, i: (a, 0, 0))
    return pl.pallas_call(
        body, name="grad_w_pa_pb_o", grid=(n_mat, n_k), in_specs=[tile, tile], out_specs=[out, out],
        out_shape=[SDS((n_mat, d_model, d_model), F32), SDS((n_mat, d_model, d_model), BF16)],
        compiler_params=_params(2),
    )(lhs3, rhs3)


def _packed_rows(pieces, plan, d_model):
    rows = {"row": lambda a: d_model // LANES, "block": lambda a: a.shape[0],
            "tiles": lambda a: a.shape[0] * a.shape[1]}
    return sum(rows[kind](pieces[p]) for kind, p, _ in plan)


def _grad_w_in_reduce_scatter(xt_bf, dproj, blk_order, pieces, plan):
    n_tok = dproj.shape[0]
    d_model = xt_bf.shape[0]
    dh = d_model // 2
    n_units = 2 * N_DEV
    n_pc = len(pieces)
    n_packed = _packed_rows(pieces, plan, d_model)
    n_part = n_packed + -n_packed % (N_DEV * SUBLANES)
    rsl = n_part // N_DEV

    def body(ord_ref, a_ref, g_ref, *rest):
        pc_refs, rest = rest[:n_pc], rest[n_pc:]
        (o_ref, small_ref, acc, fb, sbuf, gbuf, tbuf, rfin, rbuf_s, red, p_ref,
         send_f, send_s, recv_g, recv_t, recv_f, out_sems, send1, recv1, send2, recv2) = rest
        del ord_ref
        u = pl.program_id(0)
        s = u // 2
        hf = u % 2
        rnd = s // 2
        x, y, c = _mesh_pos()
        sibling = (x, y, 1 - c)
        me = _block_of((x, y, c))
        peers = _peers()

        def rows_of(blk):
            return pl.ds(pl.multiple_of(blk * rsl, SUBLANES), rsl)

        def scatter():
            return [pltpu.make_async_remote_copy(
                src_ref=p_ref.at[rows_of(blk), :], dst_ref=rbuf_s.at[k], send_sem=send1.at[k],
                recv_sem=recv1.at[k], device_id=pos, device_id_type=MESH) for k, (pos, blk) in enumerate(peers)]

        def gather(dst_block=None):
            return [pltpu.make_async_remote_copy(
                src_ref=red, dst_ref=small_ref.at[rows_of(me if dst_block is None else blk), :],
                send_sem=send2.at[k], recv_sem=recv2.at[k], device_id=pos, device_id_type=MESH)
                for k, (pos, blk) in enumerate(peers)]

        def own_slice():
            return pltpu.make_async_copy(red, small_ref.at[rows_of(me), :], out_sems.at[2])

        def pack():
            at = 0
            for kind, p, r in plan:
                ref = pc_refs[p]
                if kind == "row":
                    for q in range(d_model // LANES):
                        p_ref[at + q:at + q + 1, :] = ref[r:r + 1, q * LANES:(q + 1) * LANES]
                    at += d_model // LANES
                elif kind == "block":
                    p_ref[at:at + ref.shape[0], :] = ref[...]
                    at += ref.shape[0]
                else:
                    for tile in range(ref.shape[0] * ref.shape[1]):
                        p_ref[at + tile:at + tile + 1, :] = _colsum(ref[tile // ref.shape[1], tile % ref.shape[1]])
                    at += ref.shape[0] * ref.shape[1]
            if at < n_part:
                p_ref[at:n_part, :] = jnp.zeros((n_part - at, LANES), F32)

        @pl.when(u == 0)
        def _():
            pack()
            for cp in scatter():
                cp.start()

        @pl.when(u == 8)
        def _():
            for cp in scatter():
                cp.wait_recv()
            tot = p_ref[rows_of(me), :]
            for k in range(N_DEV - 1):
                tot = tot + rbuf_s[k]
            red[...] = tot
            own_slice().start()
            for cp in gather():
                cp.start()

        n1 = (jnp.bitwise_xor(x, c), jnp.bitwise_xor(y, 1 - c), c)
        n2 = (jnp.bitwise_xor(x, 1 - c), jnp.bitwise_xor(y, c), c)

        def feed(r, half):
            return pltpu.make_async_remote_copy(
                src_ref=fb.at[half], dst_ref=gbuf.at[r, half], send_sem=send_f.at[r, half],
                recv_sem=recv_g.at[r, half], device_id=sibling, device_id_type=MESH)

        def feed_sibling(half):
            return pltpu.make_async_remote_copy(
                src_ref=fb.at[half], dst_ref=rfin.at[0, half], send_sem=send_f.at[3, half],
                recv_sem=recv_f.at[0, half], device_id=sibling, device_id_type=MESH)

        def chip_sum(r, half):
            dst = [tbuf.at[half], rfin.at[2, half], rfin.at[1, half]][r]
            sem = [recv_t.at[half], recv_f.at[2, half], recv_f.at[1, half]][r]
            return pltpu.make_async_remote_copy(
                src_ref=sbuf.at[r, half], dst_ref=dst, send_sem=send_s.at[r, half], recv_sem=sem,
                device_id=[n2, n2, n1][r], device_id_type=MESH)

        def out_copy(half):
            return pltpu.make_async_copy(acc.at[half], o_ref.at[:, pl.ds(half * dh, dh)], out_sems.at[half])

        def partial_sum():
            return _dot(a_ref[:, 0:n_tok], g_ref[...])

        for half in range(2):
            for r in range(3):
                @pl.when(u == 4 * r + 4 + half)
                def _(r=r, half=half):
                    feed(r, half).wait_send()

                @pl.when(u == 4 * r + 2 + half)
                def _(r=r, half=half):
                    feed(r, half).wait_recv()
                    if r == 2:
                        chip_sum(0, half).wait_recv()

            @pl.when(u == 14 + half)
            def _(half=half):
                feed_sibling(half).wait_recv()
                chip_sum(2, half).wait_recv()
                chip_sum(1, half).wait_recv()

        @pl.when(jnp.logical_and(s % 2 == 0, s < 7))
        def _():
            fb[hf] = partial_sum().astype(BF16)

        @pl.when(jnp.logical_or(s == 1, s == 3))
        def _():
            sbuf[rnd, hf] = (partial_sum() + gbuf[rnd, hf].astype(F32)).astype(BF16)

        @pl.when(s == 5)
        def _():
            sbuf[2, hf] = (partial_sum() + gbuf[2, hf].astype(F32) + tbuf[hf].astype(F32)).astype(BF16)

        @pl.when(s == 7)
        def _():
            acc[hf] = (partial_sum() + rfin[0, hf].astype(F32) + rfin[1, hf].astype(F32)
                       + rfin[2, hf].astype(F32))

        for half in range(2):
            for r in range(3):
                @pl.when(u == 4 * r + half)
                def _(r=r, half=half):
                    feed(r, half).start()

                @pl.when(u == 4 * r + 2 + half)
                def _(r=r, half=half):
                    chip_sum(r, half).start()

            @pl.when(u == 12 + half)
            def _(half=half):
                feed_sibling(half).start()

        @pl.when(u == 14)
        def _():
            out_copy(0).start()

        @pl.when(u == 15)
        def _():
            out_copy(1).start()
            for half in range(2):
                feed_sibling(half).wait_send()
                for r in range(3):
                    chip_sum(r, half).wait_send()
                out_copy(half).wait()
            for cp in gather("theirs"):
                cp.wait_recv()
            for cp in scatter() + gather():
                cp.wait_send()
            own_slice().wait()

    grid_spec = pltpu.PrefetchScalarGridSpec(
        num_scalar_prefetch=1, grid=(n_units,),
        in_specs=[VMEM, pl.BlockSpec((n_tok, dh), lambda u, o: (0, 2 * o[u // 2] + u % 2))] + [VMEM] * n_pc,
        out_specs=[ANY, ANY],
        scratch_shapes=[pltpu.VMEM((2, d_model, dh), F32), pltpu.VMEM((2, d_model, dh), BF16),
                        pltpu.VMEM((3, 2, d_model, dh), BF16), pltpu.VMEM((3, 2, d_model, dh), BF16),
                        pltpu.VMEM((2, d_model, dh), BF16), pltpu.VMEM((3, 2, d_model, dh), BF16),
                        pltpu.VMEM((N_DEV - 1, rsl, LANES), F32), pltpu.VMEM((rsl, LANES), F32),
                        pltpu.VMEM((n_part, LANES), F32),
                        pltpu.SemaphoreType.DMA((4, 2)), pltpu.SemaphoreType.DMA((3, 2)),
                        pltpu.SemaphoreType.DMA((3, 2)), pltpu.SemaphoreType.DMA((2,)),
                        pltpu.SemaphoreType.DMA((3, 2)),
                        pltpu.SemaphoreType.DMA((3,))] + [pltpu.SemaphoreType.DMA((N_DEV - 1,))] * 4)
    return pl.pallas_call(
        body, name="grad_w_in_reduce_scatter", grid_spec=grid_spec,
        out_shape=[SDS((d_model, d_model), F32), SDS((n_part, LANES), F32)], compiler_params=_params(1),
    )(blk_order, xt_bf, dproj, *pieces)


def _grad_x_adamw(dproj, w_all, dr, w, g, m, v, small, loss_rows, packed, groups):
    n_tok, d_model = dr.shape
    tm = _tile(n_tok, 256, 16)
    n_steps = n_tok // tm
    tr = w.shape[0] // n_steps
    n_pk, n_grp = len(packed), len(groups)
    n_out = 4 * (n_pk + n_grp)

    def body(dp_ref, w_hbm, dr_ref, ws_ref, g_ref, m_ref, v_ref, small_ref, *rest):
        pk_in, rest = rest[:3 * n_pk], rest[3 * n_pk:]
        grp_in, rest = rest[:4 * n_grp], rest[4 * n_grp:]
        o_ref, gp_ref, d_ref, mo_ref, vo_ref, loss_ref = rest[:6]
        outs, (w_ref, w_sems) = rest[6:6 + n_out], rest[6 + n_out:]
        i = pl.program_id(0)

        def fetch(j):
            return pltpu.make_async_copy(w_hbm.at[j], w_ref.at[j], w_sems.at[j])

        def grad_x_tile(first):
            acc = DEEPNORM_ALPHA * dr_ref[...]
            for j in range(N_DEV):
                if first:
                    fetch(j).wait()
                acc = acc + _dot_tb(dp_ref[:, j * d_model:(j + 1) * d_model], w_ref[j])
            o_ref[...] = acc

        @pl.when(i == 0)
        def _():
            for j in range(N_DEV):
                fetch(j).start()
            grad_x_tile(True)

        @pl.when(i != 0)
        def _():
            grad_x_tile(False)

        grad = g_ref[...]
        gp_ref[...] = grad
        d_ref[...], mo_ref[...], vo_ref[...] = _adamw_math(ws_ref[...], grad, m_ref[...], v_ref[...])

        @pl.when(i == n_steps - 1)
        def _():
            def update(p, pw, grad, pm, pv):
                og, od, om, ov = outs[4 * p:4 * p + 4]
                og[...] = grad
                od[...], om[...], ov[...] = _adamw_math(pw[...], grad, pm[...], pv[...])

            for p, (pw_arr, _, _, row0) in enumerate(packed):
                pw, pm, pv = pk_in[3 * p:3 * p + 3]
                update(p, pw, small_ref[row0:row0 + pw_arr.shape[0], :], pm, pv)
            for p in range(n_grp):
                pw, pg, pm, pv = grp_in[4 * p:4 * p + 4]
                update(n_pk + p, pw, pg[...], pm, pv)
            sq = small_ref[loss_rows[0]:loss_rows[0] + loss_rows[1], :]
            loss_ref[...] = jnp.sum(_colsum(sq), axis=1, keepdims=True) * (0.5 / d_model)

    tile = pl.BlockSpec((tm, d_model), lambda i: (i, 0))
    slab = pl.BlockSpec((tr, w.shape[1]), lambda i: (i, 0))
    flat = [a for pk in packed for a in pk[:3]] + [a for grp in groups for a in grp]
    shapes = [pk[0].shape for pk in packed] + [grp[0].shape for grp in groups]
    res = pl.pallas_call(
        body, name="grad_x_adamw", grid=(n_steps,),
        in_specs=[pl.BlockSpec((tm, N_DEV * d_model), lambda i: (i, 0)), ANY, tile, slab, slab, slab, slab]
        + [VMEM] * (1 + len(flat)),
        out_specs=[tile, slab, slab, slab, slab, VMEM] + [VMEM] * n_out,
        out_shape=[SDS((n_tok, d_model), F32)] + [SDS(w.shape, F32)] * 4 + [SDS((1, 1), F32)]
        + [SDS(shape, F32) for shape in shapes for _ in range(4)],
        scratch_shapes=[pltpu.VMEM(w_all.shape, BF16), pltpu.SemaphoreType.DMA((N_DEV,))],
        compiler_params=_params(1),
    )(dproj, w_all, dr, w, g, m, v, small, *flat)
    return (res[0], tuple(res[1:5]), [tuple(res[6 + 4 * p:10 + 4 * p]) for p in range(n_pk + n_grp)],
            res[5].reshape(()))


def _adamw_math(w, g, m, v):
    m = ADAM_B1 * m + (1.0 - ADAM_B1) * g
    v = ADAM_B2 * v + (1.0 - ADAM_B2) * (g * g)
    m_hat = m / (1.0 - ADAM_B1 ** ADAM_STEP)
    v_hat = v / (1.0 - ADAM_B2 ** ADAM_STEP)
    delta = -ADAM_LR * (m_hat / (jnp.sqrt(v_hat) + ADAM_EPS) + ADAM_WD * w)
    return delta, m, v


def _as_rows(a):
    return a.reshape(-1, LANES)


def kernel(x, w_in, b_in, conv_w, conv_b, gn_g, gn_b, ln_v_g, ln_v_b, w_spatial, b_spatial, w_pa, w_pb, w_o, b_o, ln_out_g, ln_out_b, loss_target, m_w_in, m_b_in, m_conv_w, m_conv_b, m_gn_g, m_gn_b, m_ln_v_g, m_ln_v_b, m_w_spatial, m_b_spatial, m_w_pa, m_w_pb, m_w_o, m_b_o, m_ln_out_g, m_ln_out_b, v_w_in, v_b_in, v_conv_w, v_conv_b, v_gn_g, v_gn_b, v_ln_v_g, v_ln_v_b, v_w_spatial, v_b_spatial, v_w_pa, v_w_pb, v_w_o, v_b_o, v_ln_out_g, v_ln_out_b):
    n_batch, seq, d_model = x.shape
    n_tok = n_batch * seq
    n_heads = d_model // LANES
    dc = conv_w.shape[1]
    me = 4 * lax.axis_index("x") + 2 * lax.axis_index("y") + lax.axis_index("c")
    row = lambda a: a.reshape(1, d_model)

    x2 = x.reshape(n_tok, d_model)
    target2 = loss_target.reshape(n_tok, d_model)
    b_spatial_t = b_spatial.T

    first = jnp.where(lax.axis_index("c") == 1, 4, 2)
    second = 6 - first
    ag_rel = jnp.stack([0 * first, 0 * first + 1, first, second + 1, second, first + 1, 0 * first + 6, 0 * first + 7])
    ag_blocks = jnp.bitwise_xor(me, ag_rel).astype(jnp.int32)
    proj, xt_bf, w_all, wp_all, cw_all = _proj_all_gather(
        x2, w_in, w_pa, w_pb, w_o, conv_w, b_in.reshape(N_DEV, 1, d_model), ag_blocks)
    wp_full = wp_all.reshape(3, d_model, d_model)
    conv_w_full = jnp.pad(cw_all.transpose(1, 0, 2).reshape(CONV_K, d_model), ((0, HALO - CONV_K), (0, 0)))

    h3, h1 = _branch_a_fwd(proj, conv_w_full, row(conv_b), row(gn_g), row(gn_b), seq)
    s = _branch_b_fwd(proj, row(ln_v_g), row(ln_v_b), w_spatial, b_spatial_t, seq)

    dproj, d_h3, d_s, dr, lhs3, rhs3, vec_mid = _mid(
        h3, s, proj, x2, target2, wp_full, row(b_o), row(ln_out_g), row(ln_out_b))

    dproj, vec_b, d_ws, d_bs_t = _branch_b_bwd(dproj, proj, d_s, row(ln_v_g), row(ln_v_b), w_spatial, b_spatial_t, seq)
    dproj, d_h1, vec_a1 = _branch_a_bwd_norm(dproj, proj, h1, d_h3, row(gn_g), row(gn_b), seq)
    gp_f32, gp_bf = _weight_grads(lhs3, rhs3)
    dproj, vec_a2, d_cw8, g_w_pa, g_w_pb, g_w_o = _branch_a_bwd_conv(
        dproj, proj, d_h1, conv_w_full, gp_bf, gp_f32, seq)

    pieces = [vec_a2, vec_a1, vec_b, vec_mid, d_bs_t.T, _as_rows(d_ws), d_cw8]
    a2, a1, vb, mid = 0, 1, 2, 3
    plan = ([("row", p, r) for p, r in [(a2, 0), (a2, 1), (a1, 0), (vb, 0), (vb, 1), (vb, 2), (mid, 3), (mid, 4)]]
            + [("row", p, r) for p, r in [(a1, 3), (a1, 1), (a1, 2), (vb, 3), (vb, 4), (mid, 2), (mid, 0), (mid, 1)]]
            + [("block", 4, None), ("block", 5, None), ("tiles", 6, None), ("row", mid, 5)])

    rs_rel = jnp.stack([0 * first + 7, 0 * first + 6, first + 1, second, second + 1, first, 0 * first + 1, 0 * first])
    rs_blocks = jnp.bitwise_xor(me, rs_rel).astype(jnp.int32)
    g_w_in, small = _grad_w_in_reduce_scatter(xt_bf, dproj, rs_blocks, pieces, plan)

    g_rows = d_model // LANES
    o0 = N_DEV * g_rows
    o1 = o0 + 8 * g_rows
    o2 = o1 + n_heads
    o3 = o2 + n_heads * LANES
    o4 = o3 + n_heads * HALO
    g_cw_full = small[o3:o4].reshape(n_heads, HALO, LANES).transpose(1, 0, 2).reshape(HALO, d_model)
    g_conv_w = lax.dynamic_slice(g_cw_full, (0, me * dc), (CONV_K, dc))

    two_d = lambda a: a.reshape(-1, a.shape[-1]) if a.ndim != 1 else (
        a.reshape(-1, LANES) if a.shape[0] % LANES == 0 else a.reshape(1, -1))
    names = ["b_in", "conv_w", "conv_b", "gn_g", "gn_b", "ln_v_g", "ln_v_b", "w_spatial", "b_spatial",
             "w_pa", "w_pb", "w_o", "b_o", "ln_out_g", "ln_out_b"]
    ws = dict(b_in=b_in, conv_w=conv_w, conv_b=conv_b, gn_g=gn_g, gn_b=gn_b, ln_v_g=ln_v_g, ln_v_b=ln_v_b,
              w_spatial=w_spatial, b_spatial=b_spatial, w_pa=w_pa, w_pb=w_pb, w_o=w_o, b_o=b_o,
              ln_out_g=ln_out_g, ln_out_b=ln_out_b)
    ms = dict(b_in=m_b_in, conv_w=m_conv_w, conv_b=m_conv_b, gn_g=m_gn_g, gn_b=m_gn_b, ln_v_g=m_ln_v_g,
              ln_v_b=m_ln_v_b, w_spatial=m_w_spatial, b_spatial=m_b_spatial, w_pa=m_w_pa, w_pb=m_w_pb,
              w_o=m_w_o, b_o=m_b_o, ln_out_g=m_ln_out_g, ln_out_b=m_ln_out_b)
    vs = dict(b_in=v_b_in, conv_w=v_conv_w, conv_b=v_conv_b, gn_g=v_gn_g, gn_b=v_gn_b, ln_v_g=v_ln_v_g,
              ln_v_b=v_ln_v_b, w_spatial=v_w_spatial, b_spatial=v_b_spatial, w_pa=v_w_pa, w_pb=v_w_pb,
              w_o=v_w_o, b_o=v_b_o, ln_out_g=v_ln_out_g, ln_out_b=v_ln_out_b)
    vec_names = ["conv_b", "gn_g", "gn_b", "ln_v_g", "ln_v_b", "b_o", "ln_out_g", "ln_out_b"]
    first_row = dict(b_in=0, b_spatial=o1, w_spatial=o2, **{n: o0 + a * g_rows for a, n in enumerate(vec_names)})
    given = dict(conv_w=g_conv_w, w_pa=g_w_pa, w_pb=g_w_pb, w_o=g_w_o)
    grad_x, upd_w_in, upd, loss = _grad_x_adamw(
        dproj, w_all, dr, w_in, g_w_in, m_w_in, v_w_in, small, (o4, g_rows),
        [(two_d(ws[n]), two_d(ms[n]), two_d(vs[n]), row0) for n, row0 in first_row.items()],
        [tuple(two_d(a) for a in (ws[n], grad, ms[n], vs[n])) for n, grad in given.items()])
    grad_x = grad_x.reshape(x.shape)
    res = {n: tuple(a.reshape(ws[n].shape) for a in u) for n, u in zip([*first_row, *given], upd)}
    res["w_in"] = upd_w_in

    order = ["w_in"] + names
    return (loss, grad_x, *[res[n][k] for k in range(4) for n in order])
```

```python
import jax
import jax.numpy as jnp
from jax import lax
from jax.experimental import pallas as pl
from jax.experimental.pallas import tpu as pltpu

F32 = jnp.float32
BF16 = jnp.bfloat16
SDS = jax.ShapeDtypeStruct

N_DEV = 8
LANES = 128
SUBLANES = 8
CONV_K = 31
HALO = 32
ELEMENTWISE_ROWS = 512
LN_EPS = 1e-5
DEEPNORM_ALPHA = 2.0 ** 0.25
ADAM_LR, ADAM_B1, ADAM_B2, ADAM_EPS, ADAM_WD, ADAM_STEP = 0.001, 0.9, 0.999, 1e-08, 0.01, 10
GELU_C = 0.7978845608028654
GELU_A = 0.044715
VMEM_LIMIT = 56 * 1024 * 1024
MESH = pl.DeviceIdType.MESH
ANY = pl.BlockSpec(memory_space=pl.ANY)
VMEM = pl.BlockSpec(memory_space=pltpu.VMEM)


def _params(n_grid=0):
    sem = ("arbitrary",) * n_grid if n_grid else None
    return pltpu.CompilerParams(dimension_semantics=sem, vmem_limit_bytes=VMEM_LIMIT)


def _tile(n, pref, mult):
    t = min(n, pref)
    while n % t or t % mult:
        t -= 1
    return t


def _colsum(v):
    return jnp.sum(v, axis=0, keepdims=True)


def _sigmoid(v):
    return jax.nn.sigmoid(v)


def _silu_and_grad(v):
    s = _sigmoid(v)
    val = v * s
    return val, s + val * (1.0 - s)


def _gelu_and_grad(v):
    v2 = v * v
    sg = _sigmoid(v * (2.0 * GELU_C + (2.0 * GELU_C * GELU_A) * v2))
    grad = sg + v * sg * (1.0 - sg) * (2.0 * GELU_C + (6.0 * GELU_C * GELU_A) * v2)
    return v * sg, grad


def _tril_mask():
    r = lax.broadcasted_iota(jnp.int32, (LANES, LANES), 0)
    c = lax.broadcasted_iota(jnp.int32, (LANES, LANES), 1)
    return c <= r


def _dot(a, b):
    return jnp.dot(a, b, preferred_element_type=F32)


def _dot_tb(a, b):
    return lax.dot_general(a, b, (((1,), (1,)), ((), ())), preferred_element_type=F32)


def _dot_ta(a, b):
    return lax.dot_general(a, b, (((0,), (0,)), ((), ())), preferred_element_type=F32)


def _mesh_pos():
    return lax.axis_index("x"), lax.axis_index("y"), lax.axis_index("c")


def _block_of(pos):
    return 4 * pos[0] + 2 * pos[1] + pos[2]


def _peers():
    x, y, c = _mesh_pos()
    out = []
    for k in range(1, N_DEV):
        pos = (1 - x if k & 4 else x, 1 - y if k & 2 else y, 1 - c if k & 1 else c)
        out.append((pos, _block_of(pos)))
    return out


def _proj_all_gather(x2, w_in, w_pa, w_pb, w_o, conv_w, b_in3, blk_order):
    n_tok, d_model = x2.shape
    r8 = w_pa.shape[0]
    kc, dc = conv_w.shape
    tm = _tile(n_tok, 1024, LANES)
    n_t = n_tok // tm
    n_arr = 3

    def body(ord_ref, x_ref, b_ref, win_ref, wpa_ref, wpb_ref, wo_ref, cw_ref,
             proj_ref, xt_ref, wall_ref, wp_ref, cwall_ref,
             wbuf, xbuf, st_p, send_sems, recv_sems, local_sems, wall_sems):
        s = pl.program_id(0)
        t = pl.program_id(1)
        x, y, c = _mesh_pos()
        me = (x, y, c)
        sibling = (x, y, 1 - c)
        n1 = (jnp.bitwise_xor(x, c), jnp.bitwise_xor(y, 1 - c))
        n2 = (jnp.bitwise_xor(x, 1 - c), jnp.bitwise_xor(y, c))
        dg = (1 - x, 1 - y)
        outs = [wbuf, wp_ref, cwall_ref]
        srcs = [None, st_p, cw_ref]
        consumed = [me, sibling, (*n1, c), (*n2, 1 - c), (*n2, c), (*n1, 1 - c), (*dg, c), (*dg, 1 - c)]
        leaves = [(me, sibling), (me, (*n1, c)), (me, (*n2, c)), ((*n1, c), (*n2, c)),
                  ((*n1, c), sibling), ((*n2, c), sibling), ((*dg, c), sibling)]
        lands = [sibling, (*n1, c), (*n2, c), (*dg, c), (*n2, 1 - c), (*n1, 1 - c), (*dg, 1 - c)]

        def slot(o, pos):
            return o.at[:, _block_of(pos)] if o is wp_ref else o.at[_block_of(pos)]

        def copy(a, k, block, to, src=None):
            o = outs[a]
            return pltpu.make_async_remote_copy(
                src_ref=slot(o, block) if src is None else src, dst_ref=slot(o, block),
                send_sem=send_sems.at[a, k], recv_sem=recv_sems.at[a, k],
                device_id=to, device_id_type=MESH)

        def send(a, k):
            block, to = leaves[k]
            return copy(a, k, block, to, src=srcs[a] if k < 3 else None)

        def recv(a, k):
            return copy(a, k, lands[k], me)

        def local_copies():
            return [pltpu.make_async_copy(srcs[a], slot(outs[a], me), local_sems.at[a]) for a in (1, 2)]

        def to_hbm(step):
            return pltpu.make_async_copy(slot(wbuf, consumed[step]), slot(wall_ref, consumed[step]),
                                         wall_sems.at[step])

        def at_step(step):
            return pl.when(jnp.logical_and(s == step, t == 0))

        @at_step(0)
        def _():
            slot(wbuf, me)[...] = win_ref[...].astype(BF16)
            st_p[0] = wpa_ref[...].astype(BF16)
            st_p[1] = wpb_ref[...].astype(BF16)
            st_p[2] = wo_ref[...].astype(BF16)
            for k in range(3):
                send(0, k).start()
            for a in range(1, n_arr):
                for k in range(3):
                    send(a, k).start()
            for cp in local_copies():
                cp.start()
            to_hbm(0).start()

        @at_step(1)
        def _():
            recv(0, 0).wait_recv()
            to_hbm(1).start()

        for rnd in range(3):
            @at_step(2 + 2 * rnd)
            def _(rnd=rnd):
                recv(0, 1 + rnd).wait_recv()
                if rnd == 0:
                    send(0, 3).start()
                send(0, 4 + rnd).start()
                to_hbm(2 + 2 * rnd).start()

            @at_step(3 + 2 * rnd)
            def _(rnd=rnd):
                for a in (1, 2):
                    recv(a, 1 + rnd).wait_recv()
                    if rnd == 0:
                        send(a, 3).start()
                    send(a, 4 + rnd).start()
                recv(0, 4 + rnd).wait_recv()
                to_hbm(3 + 2 * rnd).start()

        rows = pl.ds(pl.multiple_of(t * tm, tm), tm)

        @pl.when(s == 0)
        def _():
            xb = x_ref[...].astype(BF16)
            xbuf[rows, :] = xb
            xt_ref[...] = xb.T

        proj_ref[...] = _dot(xbuf[rows, :], wbuf[ord_ref[s]]) + b_ref[...]

        @pl.when(jnp.logical_and(s == N_DEV - 1, t == n_t - 1))
        def _():
            for a in (1, 2):
                for k in (0, 4, 5, 6):
                    recv(a, k).wait_recv()
            for a in range(n_arr):
                for k in range(7):
                    send(a, k).wait_send()
            for cp in local_copies() + [to_hbm(step) for step in range(N_DEV)]:
                cp.wait()

    grid_spec = pltpu.PrefetchScalarGridSpec(
        num_scalar_prefetch=1, grid=(N_DEV, n_t),
        in_specs=[pl.BlockSpec((tm, d_model), lambda s, t, o: (jnp.where(s == 0, t, n_t - 1), 0)),
                  pl.BlockSpec((None, 1, d_model), lambda s, t, o: (o[s], 0, 0)),
                  VMEM, VMEM, VMEM, VMEM, VMEM],
        out_specs=[pl.BlockSpec((tm, d_model), lambda s, t, o: (t, o[s])),
                   pl.BlockSpec((d_model, tm), lambda s, t, o: (0, jnp.where(s == 0, t, n_t))),
                   ANY, ANY, ANY],
        scratch_shapes=[pltpu.VMEM((N_DEV, d_model, d_model), BF16), pltpu.VMEM((n_tok, d_model), BF16),
                        pltpu.VMEM((3, r8, d_model), BF16),
                        pltpu.SemaphoreType.DMA((n_arr, 7)), pltpu.SemaphoreType.DMA((n_arr, 7)),
                        pltpu.SemaphoreType.DMA((3,)), pltpu.SemaphoreType.DMA((N_DEV,))])
    return pl.pallas_call(
        body, name="proj_all_gather", grid_spec=grid_spec,
        out_shape=[SDS((n_tok, N_DEV * d_model), F32), SDS((d_model, (n_t + 1) * tm), BF16),
                   SDS((N_DEV, d_model, d_model), BF16), SDS((3, N_DEV, r8, d_model), BF16),
                   SDS((N_DEV, kc, dc), F32)],
        compiler_params=_params(2),
    )(blk_order, x2, b_in3, w_in, w_pa, w_pb, w_o, conv_w)


def _conv_rows(ta):
    return _tile(ta, 64, SUBLANES)


def _branch_a_fwd(proj, conv_w_full, conv_b, gn_g, gn_b, seq):
    n_tok = proj.shape[0]
    d_model = conv_b.shape[1]
    ta = _tile(seq, ELEMENTWISE_ROWS, HALO)
    per_seq = seq // ta
    rc = _conv_rows(ta)

    def body(av_ref, ag_ref, gt_ref, avh_ref, agh_ref, cw_ref, cb_ref, gg_ref, gb_ref,
             h3_ref, h1_ref, ext):
        keep = jnp.where(pl.program_id(0) % per_seq == 0, 0.0, 1.0)

        def group(g, carry):
            sl = pl.ds(pl.multiple_of(g * LANES, LANES), LANES)
            ext[0:HALO, :] = avh_ref[:, sl] * _sigmoid(agh_ref[:, sl]) * keep
            ext[HALO:HALO + ta, :] = av_ref[:, sl] * _sigmoid(ag_ref[:, sl])
            for r0 in range(0, ta, rc):
                acc = jnp.broadcast_to(cb_ref[:, sl], (rc, LANES))
                for k in range(CONV_K):
                    acc = acc + ext[pl.ds(r0 + HALO - (CONV_K - 1) + k, rc), :] * cw_ref[k:k + 1, sl]
                h1_ref[pl.ds(r0, rc), sl] = acc
            h1 = h1_ref[:, sl]
            mu = jnp.mean(h1, axis=-1, keepdims=True)
            dlt = h1 - mu
            var = jnp.mean(dlt * dlt, axis=-1, keepdims=True)
            h2 = dlt * lax.rsqrt(var + LN_EPS) * gg_ref[:, sl] + gb_ref[:, sl]
            gate = gt_ref[:, sl]
            h3_ref[:, sl] = (h2 * _sigmoid(h2) * gate * _sigmoid(gate)).astype(BF16)
            return carry

        lax.fori_loop(0, d_model // LANES, group, 0, unroll=True)

    blk = lambda j: pl.BlockSpec((ta, d_model), lambda i: (i, j))
    halo = lambda j: pl.BlockSpec((HALO, d_model), lambda i: (jnp.maximum(i * (ta // HALO) - 1, 0), j))
    row = pl.BlockSpec((1, d_model), lambda i: (0, 0))
    return pl.pallas_call(
        body, name="branch_a_fwd", grid=(n_tok // ta,),
        in_specs=[blk(0), blk(1), blk(2), halo(0), halo(1),
                  pl.BlockSpec((HALO, d_model), lambda i: (0, 0)), row, row, row],
        out_specs=[pl.BlockSpec((ta, d_model), lambda i: (i, 0))] * 2,
        out_shape=[SDS((n_tok, d_model), BF16), SDS((n_tok, d_model), F32)],
        scratch_shapes=[pltpu.VMEM((HALO + ta, LANES), F32)],
        compiler_params=_params(1),
    )(proj, proj, proj, proj, proj, conv_w_full, conv_b, gn_g, gn_b)


def _branch_b_fwd(proj, ln_g, ln_b, w_spatial, b_spatial_t, seq):
    n_tok = proj.shape[0]
    d_model = ln_g.shape[1]
    n_heads = d_model // LANES
    tb = _tile(seq, ELEMENTWISE_ROWS, LANES)

    def body(u_ref, v_ref, bg_ref, lg_ref, lb_ref, ws_ref, bs_ref, s_ref, vn_buf):
        v, _ = _gelu_and_grad(v_ref[...])
        mu = jnp.mean(v, axis=-1, keepdims=True)
        dlt = v - mu
        var = jnp.mean(dlt * dlt, axis=-1, keepdims=True)
        vn_buf[...] = (dlt * lax.rsqrt(var + LN_EPS) * lg_ref[...] + lb_ref[...]).astype(BF16)
        tril = _tril_mask()
        for h in range(n_heads):
            cols = slice(h * LANES, (h + 1) * LANES)
            w_h = jnp.where(tril, ws_ref[h], 0.0).astype(BF16)
            bias = bs_ref[:, h:h + 1]
            for ch in range(tb // LANES):
                rows = slice(ch * LANES, (ch + 1) * LANES)
                mix = _dot(w_h, vn_buf[rows, cols]) + bias
                u, _ = _gelu_and_grad(u_ref[rows, cols])
                gate = bg_ref[rows, cols]
                s_ref[rows, cols] = (u * mix * gate * _sigmoid(gate)).astype(BF16)

    blk = lambda j: pl.BlockSpec((tb, d_model), lambda i: (i, j))
    row = pl.BlockSpec((1, d_model), lambda i: (0, 0))
    return pl.pallas_call(
        body, name="branch_b_fwd", grid=(n_tok // tb,),
        in_specs=[blk(3), blk(4), blk(5), row, row,
                  pl.BlockSpec((n_heads, LANES, LANES), lambda i: (0, 0, 0)),
                  pl.BlockSpec((LANES, n_heads), lambda i: (0, 0))],
        out_specs=pl.BlockSpec((tb, d_model), lambda i: (i, 0)),
        out_shape=SDS((n_tok, d_model), BF16),
        scratch_shapes=[pltpu.VMEM((tb, d_model), BF16)],
        compiler_params=_params(1),
    )(proj, proj, proj, ln_g, ln_b, w_spatial, b_spatial_t)


MID_ROWS = 8


def _mid(h3, s, proj, x2, target, wp_full, b_o, lo_g, lo_b):
    n_tok, d_model = x2.shape
    tm = _tile(n_tok, 256, 16)

    def body(h3_ref, s_ref, ma_ref, mb_ref, x_ref, t_ref, wpa_ref, wpb_ref, wo_ref, bo_ref,
             lg_ref, lb_ref, dproj_ref, dh3_ref, ds_ref, dr_ref, lhs3_ref, rhs3_ref, vec_ref):
        @pl.when(pl.program_id(0) == 0)
        def _():
            vec_ref[...] = jnp.zeros_like(vec_ref)

        h3 = h3_ref[...]
        s = s_ref[...]
        ya = _dot(h3, wpa_ref[...])
        yb = _dot(s, wpb_ref[...])
        ga = _sigmoid(ma_ref[...])
        gb = _sigmoid(mb_ref[...])
        mixed = (ga * ya + gb * yb).astype(BF16)
        lhs3_ref[0] = h3
        lhs3_ref[1] = s
        lhs3_ref[2] = mixed
        r = DEEPNORM_ALPHA * x_ref[...] + _dot(mixed, wo_ref[...]) + bo_ref[...]
        mu = jnp.mean(r, axis=-1, keepdims=True)
        dlt = r - mu
        rstd = lax.rsqrt(jnp.mean(dlt * dlt, axis=-1, keepdims=True) + LN_EPS)
        rhat = dlt * rstd
        diff = rhat * lg_ref[...] + lb_ref[...] - t_ref[...]
        dy = diff * (1.0 / d_model)
        vec_ref[0:1, :] += _colsum(dy * rhat)
        vec_ref[1:2, :] += _colsum(dy)
        vec_ref[5:6, :] += _colsum(diff * diff)
        drh = dy * lg_ref[...]
        dr = rstd * (drh - jnp.mean(drh, axis=-1, keepdims=True)
                     - rhat * jnp.mean(drh * rhat, axis=-1, keepdims=True))
        vec_ref[2:3, :] += _colsum(dr)
        dr_ref[...] = dr
        drb = dr.astype(BF16)
        rhs3_ref[2] = drb
        dmixed = _dot_tb(drb, wo_ref[...])
        dya_f = dmixed * ga
        dyb_f = dmixed * gb
        dma = dya_f * ya * (1.0 - ga)
        dmb = dyb_f * yb * (1.0 - gb)
        vec_ref[3:4, :] += _colsum(dma)
        vec_ref[4:5, :] += _colsum(dmb)
        dproj_ref[:, 0:d_model] = dma.astype(BF16)
        dproj_ref[:, d_model:2 * d_model] = dmb.astype(BF16)
        dya = dya_f.astype(BF16)
        dyb = dyb_f.astype(BF16)
        rhs3_ref[0] = dya
        rhs3_ref[1] = dyb
        dh3_ref[...] = _dot_tb(dya, wpa_ref[...])
        ds_ref[...] = _dot_tb(dyb, wpb_ref[...])

    tile = pl.BlockSpec((tm, d_model), lambda i: (i, 0))
    full = lambda a: pl.BlockSpec((None, d_model, d_model), lambda i: (a, 0, 0))
    row = pl.BlockSpec((1, d_model), lambda i: (0, 0))
    stack = pl.BlockSpec((3, tm, d_model), lambda i: (0, i, 0))
    bf3 = SDS((3, n_tok, d_model), BF16)
    f32 = SDS((n_tok, d_model), F32)
    return pl.pallas_call(
        body, name="mid", grid=(n_tok // tm,),
        in_specs=[tile, tile, pl.BlockSpec((tm, d_model), lambda i: (i, 6)),
                  pl.BlockSpec((tm, d_model), lambda i: (i, 7)), tile, tile, full(0), full(1), full(2),
                  row, row, row],
        out_specs=[pl.BlockSpec((tm, 2 * d_model), lambda i: (i, 3)), tile, tile, tile, stack, stack,
                   pl.BlockSpec((MID_ROWS, d_model), lambda i: (0, 0))],
        out_shape=[SDS((n_tok, N_DEV * d_model), BF16), f32, f32, f32, bf3, bf3,
                   SDS((MID_ROWS, d_model), F32)],
        compiler_params=_params(1),
    )(h3, s, proj, proj, x2, target, wp_full, wp_full, wp_full, b_o, lo_g, lo_b)


B_ROWS = 8


def _branch_b_bwd(dproj, proj, d_s, ln_g, ln_b, w_spatial, b_spatial_t, seq):
    n_tok = proj.shape[0]
    d_model = ln_g.shape[1]
    n_heads = d_model // LANES
    tb = _tile(seq, ELEMENTWISE_ROWS, LANES)

    def body(dproj_in, u_ref, v_ref, bg_ref, ds_ref, lg_ref, lb_ref, ws_ref, bs_ref,
             dproj_ref, vec_ref, dws_ref, dbs_ref, vn_buf, dv_buf):
        del dproj_in

        @pl.when(pl.program_id(0) == 0)
        def _():
            vec_ref[...] = jnp.zeros_like(vec_ref)
            dws_ref[...] = jnp.zeros_like(dws_ref)
            dbs_ref[...] = jnp.zeros_like(dbs_ref)

        v, dgelu_v = _gelu_and_grad(v_ref[...])
        mu = jnp.mean(v, axis=-1, keepdims=True)
        dlt = v - mu
        rstd = lax.rsqrt(jnp.mean(dlt * dlt, axis=-1, keepdims=True) + LN_EPS)
        vhat = dlt * rstd
        vn_buf[...] = (vhat * lg_ref[...] + lb_ref[...]).astype(BF16)
        tril = _tril_mask()
        for h in range(n_heads):
            cols = slice(h * LANES, (h + 1) * LANES)
            w_h = jnp.where(tril, ws_ref[h], 0.0).astype(BF16)
            bias = bs_ref[:, h:h + 1]
            for ch in range(tb // LANES):
                rows = slice(ch * LANES, (ch + 1) * LANES)
                vn = vn_buf[rows, cols]
                mix = _dot(w_h, vn) + bias
                u, dgelu_u = _gelu_and_grad(u_ref[rows, cols])
                sg, dsilu = _silu_and_grad(bg_ref[rows, cols])
                dsv = ds_ref[rows, cols]
                du = dsv * mix * sg * dgelu_u
                dbg = dsv * u * mix * dsilu
                dmix = dsv * u * sg
                dmix_bf = dmix.astype(BF16)
                dproj_ref[rows, cols] = du.astype(BF16)
                dproj_ref[rows, 2 * d_model + h * LANES:2 * d_model + (h + 1) * LANES] = dbg.astype(BF16)
                vec_ref[0:1, cols] += _colsum(du)
                vec_ref[2:3, cols] += _colsum(dbg)
                dbs_ref[:, h:h + 1] += jnp.sum(dmix, axis=1, keepdims=True)
                dws_ref[h] += jnp.where(tril, _dot_tb(dmix_bf, vn), 0.0)
                dv_buf[rows, cols] = _dot_ta(w_h, dmix_bf)
        dvn = dv_buf[...]
        vec_ref[3:4, :] += _colsum(dvn * vhat)
        vec_ref[4:5, :] += _colsum(dvn)
        dvh = dvn * lg_ref[...]
        dv = rstd * (dvh - jnp.mean(dvh, axis=-1, keepdims=True)
                     - vhat * jnp.mean(dvh * vhat, axis=-1, keepdims=True)) * dgelu_v
        vec_ref[1:2, :] += _colsum(dv)
        dproj_ref[:, d_model:2 * d_model] = dv.astype(BF16)

    blk = lambda j: pl.BlockSpec((tb, d_model), lambda i: (i, j))
    row = pl.BlockSpec((1, d_model), lambda i: (0, 0))
    return pl.pallas_call(
        body, name="branch_b_bwd", grid=(n_tok // tb,),
        in_specs=[ANY, blk(3), blk(4), blk(5), pl.BlockSpec((tb, d_model), lambda i: (i, 0)), row, row,
                  pl.BlockSpec((n_heads, LANES, LANES), lambda i: (0, 0, 0)),
                  pl.BlockSpec((LANES, n_heads), lambda i: (0, 0))],
        out_specs=[pl.BlockSpec((tb, 3 * d_model), lambda i: (i, 1)),
                   pl.BlockSpec((B_ROWS, d_model), lambda i: (0, 0)),
                   pl.BlockSpec((n_heads, LANES, LANES), lambda i: (0, 0, 0)),
                   pl.BlockSpec((LANES, n_heads), lambda i: (0, 0))],
        out_shape=[SDS(dproj.shape, BF16), SDS((B_ROWS, d_model), F32),
                   SDS((n_heads, LANES, LANES), F32), SDS((LANES, n_heads), F32)],
        scratch_shapes=[pltpu.VMEM((tb, d_model), BF16), pltpu.VMEM((tb, d_model), F32)],
        input_output_aliases={0: 0},
        compiler_params=_params(1),
    )(dproj, proj, proj, proj, d_s, ln_g, ln_b, w_spatial, b_spatial_t)


A1_ROWS = 8


def _branch_a_bwd_norm(dproj, proj, h1, d_h3, gn_g, gn_b, seq):
    n_tok = proj.shape[0]
    d_model = gn_g.shape[1]
    ta = _tile(seq, ELEMENTWISE_ROWS, 16)

    def body(dproj_in, gt_ref, h1_ref, dh3_ref, gg_ref, gb_ref, dproj_ref, dh1_ref, vec_ref):
        del dproj_in

        @pl.when(pl.program_id(0) == 0)
        def _():
            vec_ref[...] = jnp.zeros_like(vec_ref)

        def group(g, carry):
            sl = pl.ds(pl.multiple_of(g * LANES, LANES), LANES)
            h1 = h1_ref[:, sl]
            mu = jnp.mean(h1, axis=-1, keepdims=True)
            dlt = h1 - mu
            rstd = lax.rsqrt(jnp.mean(dlt * dlt, axis=-1, keepdims=True) + LN_EPS)
            nrm = dlt * rstd
            sw, dsw = _silu_and_grad(nrm * gg_ref[:, sl] + gb_ref[:, sl])
            sg, dsg = _silu_and_grad(gt_ref[:, sl])
            dh3 = dh3_ref[:, sl]
            dgate = dh3 * sw * dsg
            dproj_ref[:, sl] = dgate.astype(BF16)
            vec_ref[0:1, sl] += _colsum(dgate)
            dh2 = dh3 * sg * dsw
            vec_ref[1:2, sl] += _colsum(dh2 * nrm)
            vec_ref[2:3, sl] += _colsum(dh2)
            dn = dh2 * gg_ref[:, sl]
            dh1 = rstd * (dn - jnp.mean(dn, axis=-1, keepdims=True)
                          - nrm * jnp.mean(dn * nrm, axis=-1, keepdims=True))
            vec_ref[3:4, sl] += _colsum(dh1)
            dh1_ref[:, sl] = dh1
            return carry

        lax.fori_loop(0, d_model // LANES, group, 0, unroll=True)

    tile = pl.BlockSpec((ta, d_model), lambda i: (i, 0))
    row = pl.BlockSpec((1, d_model), lambda i: (0, 0))
    return pl.pallas_call(
        body, name="branch_a_bwd_norm", grid=(n_tok // ta,),
        in_specs=[ANY, pl.BlockSpec((ta, d_model), lambda i: (i, 2)), tile, tile, row, row],
        out_specs=[pl.BlockSpec((ta, d_model), lambda i: (i, 2)), tile,
                   pl.BlockSpec((A1_ROWS, d_model), lambda i: (0, 0))],
        out_shape=[SDS(dproj.shape, BF16), SDS((n_tok, d_model), F32), SDS((A1_ROWS, d_model), F32)],
        input_output_aliases={0: 0},
        compiler_params=_params(1),
    )(dproj, proj, h1, d_h3, gn_g, gn_b)


A2_ROWS = 8


def _branch_a_bwd_conv(dproj, proj, d_h1, conv_w_full, gp_bf, gp_f32, seq):
    n_tok = proj.shape[0]
    d_model = conv_w_full.shape[1]
    n_groups = d_model // LANES
    ta = _tile(seq, ELEMENTWISE_ROWS, HALO)
    n_tiles = n_tok // ta
    per_seq = seq // ta
    rc = _conv_rows(ta)
    last_halo = n_tok // HALO - 1
    r8 = d_model // N_DEV
    prow = _tile(r8, 32, 16)

    def body(dproj_in, av_ref, ag_ref, avh_ref, agh_ref, dh1_ref, dh1h_ref, cw_ref, gp_bf_ref, gp_f32_ref,
             dproj_ref, vec_ref, dcw_ref, opa_ref, opb_ref, opo_ref,
             ext_h0, ext_d, rbuf, own, send_sems, recv_sems, local_sems):
        del dproj_in
        i = pl.program_id(0)
        x, y, c = _mesh_pos()
        me = _block_of((x, y, c))
        peers = _peers()

        def sends():
            return [pltpu.make_async_remote_copy(
                src_ref=gp_bf_ref.at[a, pl.ds(pl.multiple_of(blk * r8, 16), r8), :], dst_ref=rbuf.at[k, a],
                send_sem=send_sems.at[a, k], recv_sem=recv_sems.at[a, k], device_id=pos, device_id_type=MESH)
                for k, (pos, blk) in enumerate(peers) for a in range(3)]

        def own_rows():
            return [pltpu.make_async_copy(gp_f32_ref.at[a, pl.ds(pl.multiple_of(me * r8, 8), r8), :],
                                          own.at[a], local_sems.at[a]) for a in range(3)]

        @pl.when(i == 0)
        def _():
            vec_ref[...] = jnp.zeros_like(vec_ref)
            dcw_ref[...] = jnp.zeros_like(dcw_ref)
            for cp in sends() + own_rows():
                cp.start()

        keep_past = jnp.where(i % per_seq == 0, 0.0, 1.0)
        keep_next = jnp.where(i % per_seq == per_seq - 1, 0.0, 1.0)

        def group(g, carry):
            sl = pl.ds(pl.multiple_of(g * LANES, LANES), LANES)
            av = av_ref[:, sl]
            sig = _sigmoid(ag_ref[:, sl])
            ext_h0[0:HALO, :] = avh_ref[:, sl] * _sigmoid(agh_ref[:, sl]) * keep_past
            ext_h0[HALO:HALO + ta, :] = av * sig
            ext_d[0:ta, :] = dh1_ref[:, sl]
            ext_d[ta:ta + HALO, :] = dh1h_ref[:, sl] * keep_next
            for r0 in range(0, ta, rc):
                dh1 = ext_d[pl.ds(r0, rc), :]
                acc = jnp.zeros((rc, LANES), F32)
                for k in range(CONV_K):
                    acc = acc + ext_d[pl.ds(r0 + CONV_K - 1 - k, rc), :] * cw_ref[k:k + 1, sl]
                    prod = dh1 * ext_h0[pl.ds(r0 + HALO - (CONV_K - 1) + k, rc), :]
                    dcw_ref[g, k] += jnp.sum(prod.reshape(rc // SUBLANES, SUBLANES, LANES), axis=0)
                rows = pl.ds(r0, rc)
                sig_r = sig[r0:r0 + rc]
                dav = acc * sig_r
                dag = dav * av[r0:r0 + rc] * (1.0 - sig_r)
                dproj_ref[rows, sl] = dav.astype(BF16)
                dproj_ref[rows, pl.ds(pl.multiple_of(d_model + g * LANES, LANES), LANES)] = dag.astype(BF16)
                vec_ref[0:1, sl] += _colsum(dav)
                vec_ref[1:2, sl] += _colsum(dag)
            return carry

        lax.fori_loop(0, n_groups, group, 0, unroll=True)

        @pl.when(i == n_tiles - 1)
        def _():
            for cp in own_rows():
                cp.wait()
            for cp in sends():
                cp.wait_recv()
            for a, o in enumerate([opa_ref, opb_ref, opo_ref]):
                for q in range(r8 // prow):
                    r = pl.ds(q * prow, prow)
                    tot = own[a, r, :]
                    for k in range(N_DEV - 1):
                        tot = tot + rbuf[k, a, r, :].astype(F32)
                    o[r, :] = tot
            for cp in sends():
                cp.wait_send()

    blk = lambda j: pl.BlockSpec((ta, d_model), lambda i: (i, j))
    halo = lambda j: pl.BlockSpec((HALO, d_model), lambda i: (jnp.maximum(i * (ta // HALO) - 1, 0), j))
    shard = pl.BlockSpec((r8, d_model), lambda i: (0, 0))
    return pl.pallas_call(
        body, name="branch_a_bwd_conv", grid=(n_tiles,),
        in_specs=[ANY, blk(0), blk(1), halo(0), halo(1), pl.BlockSpec((ta, d_model), lambda i: (i, 0)),
                  pl.BlockSpec((HALO, d_model), lambda i: (jnp.minimum((i + 1) * (ta // HALO), last_halo), 0)),
                  pl.BlockSpec((HALO, d_model), lambda i: (0, 0)), ANY, ANY],
        out_specs=[pl.BlockSpec((ta, 2 * d_model), lambda i: (i, 0)),
                   pl.BlockSpec((A2_ROWS, d_model), lambda i: (0, 0)),
                   pl.BlockSpec((n_groups, HALO, SUBLANES, LANES), lambda i: (0, 0, 0, 0)),
                   shard, shard, shard],
        out_shape=[SDS(dproj.shape, BF16), SDS((A2_ROWS, d_model), F32),
                   SDS((n_groups, HALO, SUBLANES, LANES), F32)] + [SDS((r8, d_model), F32)] * 3,
        scratch_shapes=[pltpu.VMEM((HALO + ta, LANES), F32), pltpu.VMEM((ta + HALO, LANES), F32),
                        pltpu.VMEM((N_DEV - 1, 3, r8, d_model), BF16), pltpu.VMEM((3, r8, d_model), F32),
                        pltpu.SemaphoreType.DMA((3, 7)), pltpu.SemaphoreType.DMA((3, 7)),
                        pltpu.SemaphoreType.DMA((3,))],
        input_output_aliases={0: 0},
        compiler_params=_params(1),
    )(dproj, proj, proj, proj, proj, d_h1, d_h1, conv_w_full, gp_bf, gp_f32)


def _weight_grads(lhs3, rhs3):
    n_mat, n_tok, d_model = lhs3.shape
    tk = _tile(n_tok, 2048, 16)
    n_k = n_tok // tk

    def body(a_ref, g_ref, o_ref, ob_ref):
        part = _dot_ta(a_ref[...], g_ref[...])

        @pl.when(pl.program_id(1) == 0)
        def _():
            o_ref[...] = part

        @pl.when(pl.program_id(1) != 0)
        def _():
            o_ref[...] += part

        @pl.when(pl.program_id(1) == n_k - 1)
        def _():
            ob_ref[...] = o_ref[...].astype(BF16)

    tile = pl.BlockSpec((None, tk, d_model), lambda a, i: (a, i, 0))
    out = pl.BlockSpec((None, d_model, d_model), lambda a, i: (a, 0, 0))
    return pl.pallas_call(
        body, name="grad_w_pa_pb_o", grid=(n_mat, n_k), in_specs=[tile, tile], out_specs=[out, out],
        out_shape=[SDS((n_mat, d_model, d_model), F32), SDS((n_mat, d_model, d_model), BF16)],
        compiler_params=_params(2),
    )(lhs3, rhs3)


def _packed_rows(pieces, plan, d_model):
    rows = {"row": lambda a: d_model // LANES, "block": lambda a: a.shape[0],
            "tiles": lambda a: a.shape[0] * a.shape[1]}
    return sum(rows[kind](pieces[p]) for kind, p, _ in plan)


def _grad_w_in_reduce_scatter(xt_bf, dproj, blk_order, pieces, plan):
    n_tok = dproj.shape[0]
    d_model = xt_bf.shape[0]
    dh = d_model // 2
    n_units = 2 * N_DEV
    n_pc = len(pieces)
    n_packed = _packed_rows(pieces, plan, d_model)
    n_part = n_packed + -n_packed % (N_DEV * SUBLANES)
    rsl = n_part // N_DEV

    def body(ord_ref, a_ref, g_ref, *rest):
        pc_refs, rest = rest[:n_pc], rest[n_pc:]
        (o_ref, small_ref, acc, fb, sbuf, gbuf, tbuf, rfin, rbuf_s, red, p_ref,
         send_f, send_s, recv_g, recv_t, recv_f, out_sems, send1, recv1, send2, recv2) = rest
        del ord_ref
        u = pl.program_id(0)
        s = u // 2
        hf = u % 2
        rnd = s // 2
        x, y, c = _mesh_pos()
        sibling = (x, y, 1 - c)
        me = _block_of((x, y, c))
        peers = _peers()

        def rows_of(blk):
            return pl.ds(pl.multiple_of(blk * rsl, SUBLANES), rsl)

        def scatter():
            return [pltpu.make_async_remote_copy(
                src_ref=p_ref.at[rows_of(blk), :], dst_ref=rbuf_s.at[k], send_sem=send1.at[k],
                recv_sem=recv1.at[k], device_id=pos, device_id_type=MESH) for k, (pos, blk) in enumerate(peers)]

        def gather(dst_block=None):
            return [pltpu.make_async_remote_copy(
                src_ref=red, dst_ref=small_ref.at[rows_of(me if dst_block is None else blk), :],
                send_sem=send2.at[k], recv_sem=recv2.at[k], device_id=pos, device_id_type=MESH)
                for k, (pos, blk) in enumerate(peers)]

        def own_slice():
            return pltpu.make_async_copy(red, small_ref.at[rows_of(me), :], out_sems.at[2])

        def pack():
            at = 0
            for kind, p, r in plan:
                ref = pc_refs[p]
                if kind == "row":
                    for q in range(d_model // LANES):
                        p_ref[at + q:at + q + 1, :] = ref[r:r + 1, q * LANES:(q + 1) * LANES]
                    at += d_model // LANES
                elif kind == "block":
                    p_ref[at:at + ref.shape[0], :] = ref[...]
                    at += ref.shape[0]
                else:
                    for tile in range(ref.shape[0] * ref.shape[1]):
                        p_ref[at + tile:at + tile + 1, :] = _colsum(ref[tile // ref.shape[1], tile % ref.shape[1]])
                    at += ref.shape[0] * ref.shape[1]
            if at < n_part:
                p_ref[at:n_part, :] = jnp.zeros((n_part - at, LANES), F32)

        @pl.when(u == 0)
        def _():
            pack()
            for cp in scatter():
                cp.start()

        @pl.when(u == 8)
        def _():
            for cp in scatter():
                cp.wait_recv()
            tot = p_ref[rows_of(me), :]
            for k in range(N_DEV - 1):
                tot = tot + rbuf_s[k]
            red[...] = tot
            own_slice().start()
            for cp in gather():
                cp.start()

        n1 = (jnp.bitwise_xor(x, c), jnp.bitwise_xor(y, 1 - c), c)
        n2 = (jnp.bitwise_xor(x, 1 - c), jnp.bitwise_xor(y, c), c)

        def feed(r, half):
            return pltpu.make_async_remote_copy(
                src_ref=fb.at[half], dst_ref=gbuf.at[r, half], send_sem=send_f.at[r, half],
                recv_sem=recv_g.at[r, half], device_id=sibling, device_id_type=MESH)

        def feed_sibling(half):
            return pltpu.make_async_remote_copy(
                src_ref=fb.at[half], dst_ref=rfin.at[0, half], send_sem=send_f.at[3, half],
                recv_sem=recv_f.at[0, half], device_id=sibling, device_id_type=MESH)

        def chip_sum(r, half):
            dst = [tbuf.at[half], rfin.at[2, half], rfin.at[1, half]][r]
            sem = [recv_t.at[half], recv_f.at[2, half], recv_f.at[1, half]][r]
            return pltpu.make_async_remote_copy(
                src_ref=sbuf.at[r, half], dst_ref=dst, send_sem=send_s.at[r, half], recv_sem=sem,
                device_id=[n2, n2, n1][r], device_id_type=MESH)

        def out_copy(half):
            return pltpu.make_async_copy(acc.at[half], o_ref.at[:, pl.ds(half * dh, dh)], out_sems.at[half])

        def partial_sum():
            return _dot(a_ref[:, 0:n_tok], g_ref[...])

        for half in range(2):
            for r in range(3):
                @pl.when(u == 4 * r + 4 + half)
                def _(r=r, half=half):
                    feed(r, half).wait_send()

                @pl.when(u == 4 * r + 2 + half)
                def _(r=r, half=half):
                    feed(r, half).wait_recv()
                    if r == 2:
                        chip_sum(0, half).wait_recv()

            @pl.when(u == 14 + half)
            def _(half=half):
                feed_sibling(half).wait_recv()
                chip_sum(2, half).wait_recv()
                chip_sum(1, half).wait_recv()

        @pl.when(jnp.logical_and(s % 2 == 0, s < 7))
        def _():
            fb[hf] = partial_sum().astype(BF16)

        @pl.when(jnp.logical_or(s == 1, s == 3))
        def _():
            sbuf[rnd, hf] = (partial_sum() + gbuf[rnd, hf].astype(F32)).astype(BF16)

        @pl.when(s == 5)
        def _():
            sbuf[2, hf] = (partial_sum() + gbuf[2, hf].astype(F32) + tbuf[hf].astype(F32)).astype(BF16)

        @pl.when(s == 7)
        def _():
            acc[hf] = (partial_sum() + rfin[0, hf].astype(F32) + rfin[1, hf].astype(F32)
                       + rfin[2, hf].astype(F32))

        for half in range(2):
            for r in range(3):
                @pl.when(u == 4 * r + half)
                def _(r=r, half=half):
                    feed(r, half).start()

                @pl.when(u == 4 * r + 2 + half)
                def _(r=r, half=half):
                    chip_sum(r, half).start()

            @pl.when(u == 12 + half)
            def _(half=half):
                feed_sibling(half).start()

        @pl.when(u == 14)
        def _():
            out_copy(0).start()

        @pl.when(u == 15)
        def _():
            out_copy(1).start()
            for half in range(2):
                feed_sibling(half).wait_send()
                for r in range(3):
                    chip_sum(r, half).wait_send()
                out_copy(half).wait()
            for cp in gather("theirs"):
                cp.wait_recv()
            for cp in scatter() + gather():
                cp.wait_send()
            own_slice().wait()

    grid_spec = pltpu.PrefetchScalarGridSpec(
        num_scalar_prefetch=1, grid=(n_units,),
        in_specs=[VMEM, pl.BlockSpec((n_tok, dh), lambda u, o: (0, 2 * o[u // 2] + u % 2))] + [VMEM] * n_pc,
        out_specs=[ANY, ANY],
        scratch_shapes=[pltpu.VMEM((2, d_model, dh), F32), pltpu.VMEM((2, d_model, dh), BF16),
                        pltpu.VMEM((3, 2, d_model, dh), BF16), pltpu.VMEM((3, 2, d_model, dh), BF16),
                        pltpu.VMEM((2, d_model, dh), BF16), pltpu.VMEM((3, 2, d_model, dh), BF16),
                        pltpu.VMEM((N_DEV - 1, rsl, LANES), F32), pltpu.VMEM((rsl, LANES), F32),
                        pltpu.VMEM((n_part, LANES), F32),
                        pltpu.SemaphoreType.DMA((4, 2)), pltpu.SemaphoreType.DMA((3, 2)),
                        pltpu.SemaphoreType.DMA((3, 2)), pltpu.SemaphoreType.DMA((2,)),
                        pltpu.SemaphoreType.DMA((3, 2)),
                        pltpu.SemaphoreType.DMA((3,))] + [pltpu.SemaphoreType.DMA((N_DEV - 1,))] * 4)
    return pl.pallas_call(
        body, name="grad_w_in_reduce_scatter", grid_spec=grid_spec,
        out_shape=[SDS((d_model, d_model), F32), SDS((n_part, LANES), F32)], compiler_params=_params(1),
    )(blk_order, xt_bf, dproj, *pieces)


def _grad_x_adamw(dproj, w_all, dr, w, g, m, v, small, loss_rows, packed, groups):
    n_tok, d_model = dr.shape
    tm = _tile(n_tok, 256, 16)
    n_steps = n_tok // tm
    tr = w.shape[0] // n_steps
    n_pk, n_grp = len(packed), len(groups)
    n_out = 4 * (n_pk + n_grp)

    def body(dp_ref, w_hbm, dr_ref, ws_ref, g_ref, m_ref, v_ref, small_ref, *rest):
        pk_in, rest = rest[:3 * n_pk], rest[3 * n_pk:]
        grp_in, rest = rest[:4 * n_grp], rest[4 * n_grp:]
        o_ref, gp_ref, d_ref, mo_ref, vo_ref, loss_ref = rest[:6]
        outs, (w_ref, w_sems) = rest[6:6 + n_out], rest[6 + n_out:]
        i = pl.program_id(0)

        def fetch(j):
            return pltpu.make_async_copy(w_hbm.at[j], w_ref.at[j], w_sems.at[j])

        def grad_x_tile(first):
            acc = DEEPNORM_ALPHA * dr_ref[...]
            for j in range(N_DEV):
                if first:
                    fetch(j).wait()
                acc = acc + _dot_tb(dp_ref[:, j * d_model:(j + 1) * d_model], w_ref[j])
            o_ref[...] = acc

        @pl.when(i == 0)
        def _():
            for j in range(N_DEV):
                fetch(j).start()
            grad_x_tile(True)

        @pl.when(i != 0)
        def _():
            grad_x_tile(False)

        grad = g_ref[...]
        gp_ref[...] = grad
        d_ref[...], mo_ref[...], vo_ref[...] = _adamw_math(ws_ref[...], grad, m_ref[...], v_ref[...])

        @pl.when(i == n_steps - 1)
        def _():
            def update(p, pw, grad, pm, pv):
                og, od, om, ov = outs[4 * p:4 * p + 4]
                og[...] = grad
                od[...], om[...], ov[...] = _adamw_math(pw[...], grad, pm[...], pv[...])

            for p, (pw_arr, _, _, row0) in enumerate(packed):
                pw, pm, pv = pk_in[3 * p:3 * p + 3]
                update(p, pw, small_ref[row0:row0 + pw_arr.shape[0], :], pm, pv)
            for p in range(n_grp):
                pw, pg, pm, pv = grp_in[4 * p:4 * p + 4]
                update(n_pk + p, pw, pg[...], pm, pv)
            sq = small_ref[loss_rows[0]:loss_rows[0] + loss_rows[1], :]
            loss_ref[...] = jnp.sum(_colsum(sq), axis=1, keepdims=True) * (0.5 / d_model)

    tile = pl.BlockSpec((tm, d_model), lambda i: (i, 0))
    slab = pl.BlockSpec((tr, w.shape[1]), lambda i: (i, 0))
    flat = [a for pk in packed for a in pk[:3]] + [a for grp in groups for a in grp]
    shapes = [pk[0].shape for pk in packed] + [grp[0].shape for grp in groups]
    res = pl.pallas_call(
        body, name="grad_x_adamw", grid=(n_steps,),
        in_specs=[pl.BlockSpec((tm, N_DEV * d_model), lambda i: (i, 0)), ANY, tile, slab, slab, slab, slab]
        + [VMEM] * (1 + len(flat)),
        out_specs=[tile, slab, slab, slab, slab, VMEM] + [VMEM] * n_out,
        out_shape=[SDS((n_tok, d_model), F32)] + [SDS(w.shape, F32)] * 4 + [SDS((1, 1), F32)]
        + [SDS(shape, F32) for shape in shapes for _ in range(4)],
        scratch_shapes=[pltpu.VMEM(w_all.shape, BF16), pltpu.SemaphoreType.DMA((N_DEV,))],
        compiler_params=_params(1),
    )(dproj, w_all, dr, w, g, m, v, small, *flat)
    return (res[0], tuple(res[1:5]), [tuple(res[6 + 4 * p:10 + 4 * p]) for p in range(n_pk + n_grp)],
            res[5].reshape(()))


def _adamw_math(w, g, m, v):
    m = ADAM_B1 * m + (1.0 - ADAM_B1) * g
    v = ADAM_B2 * v + (1.0 - ADAM_B2) * (g * g)
    m_hat = m / (1.0 - ADAM_B1 ** ADAM_STEP)
    v_hat = v / (1.0 - ADAM_B2 ** ADAM_STEP)
    delta = -ADAM_LR * (m_hat / (jnp.sqrt(v_hat) + ADAM_EPS) + ADAM_WD * w)
    return delta, m, v


def _as_rows(a):
    return a.reshape(-1, LANES)


def kernel(x, w_in, b_in, conv_w, conv_b, gn_g, gn_b, ln_v_g, ln_v_b, w_spatial, b_spatial, w_pa, w_pb, w_o, b_o, ln_out_g, ln_out_b, loss_target, m_w_in, m_b_in, m_conv_w, m_conv_b, m_gn_g, m_gn_b, m_ln_v_g, m_ln_v_b, m_w_spatial, m_b_spatial, m_w_pa, m_w_pb, m_w_o, m_b_o, m_ln_out_g, m_ln_out_b, v_w_in, v_b_in, v_conv_w, v_conv_b, v_gn_g, v_gn_b, v_ln_v_g, v_ln_v_b, v_w_spatial, v_b_spatial, v_w_pa, v_w_pb, v_w_o, v_b_o, v_ln_out_g, v_ln_out_b):
    n_batch, seq, d_model = x.shape
    n_tok = n_batch * seq
    n_heads = d_model // LANES
    dc = conv_w.shape[1]
    me = 4 * lax.axis_index("x") + 2 * lax.axis_index("y") + lax.axis_index("c")
    row = lambda a: a.reshape(1, d_model)

    x2 = x.reshape(n_tok, d_model)
    target2 = loss_target.reshape(n_tok, d_model)
    b_spatial_t = b_spatial.T

    first = jnp.where(lax.axis_index("c") == 1, 4, 2)
    second = 6 - first
    ag_rel = jnp.stack([0 * first, 0 * first + 1, first, second + 1, second, first + 1, 0 * first + 6, 0 * first + 7])
    ag_blocks = jnp.bitwise_xor(me, ag_rel).astype(jnp.int32)
    proj, xt_bf, w_all, wp_all, cw_all = _proj_all_gather(
        x2, w_in, w_pa, w_pb, w_o, conv_w, b_in.reshape(N_DEV, 1, d_model), ag_blocks)
    wp_full = wp_all.reshape(3, d_model, d_model)
    conv_w_full = jnp.pad(cw_all.transpose(1, 0, 2).reshape(CONV_K, d_model), ((0, HALO - CONV_K), (0, 0)))

    h3, h1 = _branch_a_fwd(proj, conv_w_full, row(conv_b), row(gn_g), row(gn_b), seq)
    s = _branch_b_fwd(proj, row(ln_v_g), row(ln_v_b), w_spatial, b_spatial_t, seq)

    dproj, d_h3, d_s, dr, lhs3, rhs3, vec_mid = _mid(
        h3, s, proj, x2, target2, wp_full, row(b_o), row(ln_out_g), row(ln_out_b))

    dproj, vec_b, d_ws, d_bs_t = _branch_b_bwd(dproj, proj, d_s, row(ln_v_g), row(ln_v_b), w_spatial, b_spatial_t, seq)
    dproj, d_h1, vec_a1 = _branch_a_bwd_norm(dproj, proj, h1, d_h3, row(gn_g), row(gn_b), seq)
    gp_f32, gp_bf = _weight_grads(lhs3, rhs3)
    dproj, vec_a2, d_cw8, g_w_pa, g_w_pb, g_w_o = _branch_a_bwd_conv(
        dproj, proj, d_h1, conv_w_full, gp_bf, gp_f32, seq)

    pieces = [vec_a2, vec_a1, vec_b, vec_mid, d_bs_t.T, _as_rows(d_ws), d_cw8]
    a2, a1, vb, mid = 0, 1, 2, 3
    plan = ([("row", p, r) for p, r in [(a2, 0), (a2, 1), (a1, 0), (vb, 0), (vb, 1), (vb, 2), (mid, 3), (mid, 4)]]
            + [("row", p, r) for p, r in [(a1, 3), (a1, 1), (a1, 2), (vb, 3), (vb, 4), (mid, 2), (mid, 0), (mid, 1)]]
            + [("block", 4, None), ("block", 5, None), ("tiles", 6, None), ("row", mid, 5)])

    rs_rel = jnp.stack([0 * first + 7, 0 * first + 6, first + 1, second, second + 1, first, 0 * first + 1, 0 * first])
    rs_blocks = jnp.bitwise_xor(me, rs_rel).astype(jnp.int32)
    g_w_in, small = _grad_w_in_reduce_scatter(xt_bf, dproj, rs_blocks, pieces, plan)

    g_rows = d_model // LANES
    o0 = N_DEV * g_rows
    o1 = o0 + 8 * g_rows
    o2 = o1 + n_heads
    o3 = o2 + n_heads * LANES
    o4 = o3 + n_heads * HALO
    g_cw_full = small[o3:o4].reshape(n_heads, HALO, LANES).transpose(1, 0, 2).reshape(HALO, d_model)
    g_conv_w = lax.dynamic_slice(g_cw_full, (0, me * dc), (CONV_K, dc))

    two_d = lambda a: a.reshape(-1, a.shape[-1]) if a.ndim != 1 else (
        a.reshape(-1, LANES) if a.shape[0] % LANES == 0 else a.reshape(1, -1))
    names = ["b_in", "conv_w", "conv_b", "gn_g", "gn_b", "ln_v_g", "ln_v_b", "w_spatial", "b_spatial",
             "w_pa", "w_pb", "w_o", "b_o", "ln_out_g", "ln_out_b"]
    ws = dict(b_in=b_in, conv_w=conv_w, conv_b=conv_b, gn_g=gn_g, gn_b=gn_b, ln_v_g=ln_v_g, ln_v_b=ln_v_b,
              w_spatial=w_spatial, b_spatial=b_spatial, w_pa=w_pa, w_pb=w_pb, w_o=w_o, b_o=b_o,
              ln_out_g=ln_out_g, ln_out_b=ln_out_b)
    ms = dict(b_in=m_b_in, conv_w=m_conv_w, conv_b=m_conv_b, gn_g=m_gn_g, gn_b=m_gn_b, ln_v_g=m_ln_v_g,
              ln_v_b=m_ln_v_b, w_spatial=m_w_spatial, b_spatial=m_b_spatial, w_pa=m_w_pa, w_pb=m_w_pb,
              w_o=m_w_o, b_o=m_b_o, ln_out_g=m_ln_out_g, ln_out_b=m_ln_out_b)
    vs = dict(b_in=v_b_in, conv_w=v_conv_w, conv_b=v_conv_b, gn_g=v_gn_g, gn_b=v_gn_b, ln_v_g=v_ln_v_g,
              ln_v_b=v_ln_v_b, w_spatial=v_w_spatial, b_spatial=v_b_spatial, w_pa=v_w_pa, w_pb=v_w_pb,
              w_o=v_w_o, b_o=v_b_o, ln_out_g=v_ln_out_g, ln_out_b=v_ln_out_b)
    vec_names = ["conv_b", "gn_g", "gn_b", "ln_v_g", "ln_v_b", "b_o", "ln_out_g", "ln_out_b"]
    first_row = dict(b_in=0, b_spatial=o1, w_spatial=o2, **{n: o0 + a * g_rows for a, n in enumerate(vec_names)})
    given = dict(conv_w=g_conv_w, w_pa=g_w_pa, w_pb=g_w_pb, w_o=g_w_o)
    grad_x, upd_w_in, upd, loss = _grad_x_adamw(
        dproj, w_all, dr, w_in, g_w_in, m_w_in, v_w_in, small, (o4, g_rows),
        [(two_d(ws[n]), two_d(ms[n]), two_d(vs[n]), row0) for n, row0 in first_row.items()],
        [tuple(two_d(a) for a in (ws[n], grad, ms[n], vs[n])) for n, grad in given.items()])
    grad_x = grad_x.reshape(x.shape)
    res = {n: tuple(a.reshape(ws[n].shape) for a in u) for n, u in zip([*first_row, *given], upd)}
    res["w_in"] = upd_w_in

    order = ["w_in"] + names
    return (loss, grad_x, *[res[n][k] for k in range(4) for n in order])
```

```python
import jax
import jax.numpy as jnp
from jax import lax
from jax.experimental import pallas as pl
from jax.experimental.pallas import tpu as pltpu

F32 = jnp.float32
BF16 = jnp.bfloat16
SDS = jax.ShapeDtypeStruct

N_DEV = 8
LANES = 128
SUBLANES = 8
CONV_K = 31
HALO = 32
ELEMENTWISE_ROWS = 512
XT_CHUNKS = 4
LN_EPS = 1e-5
DEEPNORM_ALPHA = 2.0 ** 0.25
ADAM_LR, ADAM_B1, ADAM_B2, ADAM_EPS, ADAM_WD, ADAM_STEP = 0.001, 0.9, 0.999, 1e-08, 0.01, 10
GELU_C = 0.7978845608028654
GELU_A = 0.044715
VMEM_LIMIT = 56 * 1024 * 1024
MESH = pl.DeviceIdType.MESH
ANY = pl.BlockSpec(memory_space=pl.ANY)
VMEM = pl.BlockSpec(memory_space=pltpu.VMEM)


def _params(n_grid=0):
    sem = ("arbitrary",) * n_grid if n_grid else None
    return pltpu.CompilerParams(dimension_semantics=sem, vmem_limit_bytes=VMEM_LIMIT)


def _tile(n, pref, mult):
    t = min(n, pref)
    while n % t or t % mult:
        t -= 1
    return t


def _colsum(v):
    return jnp.sum(v, axis=0, keepdims=True)


def _sigmoid(v):
    return jax.nn.sigmoid(v)


def _silu_and_grad(v):
    s = _sigmoid(v)
    val = v * s
    return val, s + val * (1.0 - s)


def _gelu_and_grad(v):
    v2 = v * v
    sg = _sigmoid(v * (2.0 * GELU_C + (2.0 * GELU_C * GELU_A) * v2))
    grad = sg + v * sg * (1.0 - sg) * (2.0 * GELU_C + (6.0 * GELU_C * GELU_A) * v2)
    return v * sg, grad


def _tril_mask():
    r = lax.broadcasted_iota(jnp.int32, (LANES, LANES), 0)
    c = lax.broadcasted_iota(jnp.int32, (LANES, LANES), 1)
    return c <= r


def _dot(a, b):
    return jnp.dot(a, b, preferred_element_type=F32)


def _dot_tb(a, b):
    return lax.dot_general(a, b, (((1,), (1,)), ((), ())), preferred_element_type=F32)


def _dot_ta(a, b):
    return lax.dot_general(a, b, (((0,), (0,)), ((), ())), preferred_element_type=F32)


def _mesh_pos():
    return lax.axis_index("x"), lax.axis_index("y"), lax.axis_index("c")


def _block_of(pos):
    return 4 * pos[0] + 2 * pos[1] + pos[2]


def _peers():
    x, y, c = _mesh_pos()
    out = []
    for k in range(1, N_DEV):
        pos = (1 - x if k & 4 else x, 1 - y if k & 2 else y, 1 - c if k & 1 else c)
        out.append((pos, _block_of(pos)))
    return out


def _proj_all_gather(x2, w_in, w_pa, w_pb, w_o, conv_w, b_in3, blk_order):
    n_tok, d_model = x2.shape
    r8 = w_pa.shape[0]
    kc, dc = conv_w.shape
    tm = _tile(n_tok, 1024, LANES)
    n_t = n_tok // tm
    n_arr = 3

    def body(ord_ref, x_ref, b_ref, win_ref, wpa_ref, wpb_ref, wo_ref, cw_ref,
             proj_ref, xt_ref, wall_ref, wp_ref, cwall_ref,
             wbuf, xbuf, st_p, send_sems, recv_sems, local_sems, wall_sems):
        s = pl.program_id(0)
        t = pl.program_id(1)
        x, y, c = _mesh_pos()
        me = (x, y, c)
        sibling = (x, y, 1 - c)
        n1 = (jnp.bitwise_xor(x, c), jnp.bitwise_xor(y, 1 - c))
        n2 = (jnp.bitwise_xor(x, 1 - c), jnp.bitwise_xor(y, c))
        dg = (1 - x, 1 - y)
        outs = [wbuf, wp_ref, cwall_ref]
        srcs = [None, st_p, cw_ref]
        consumed = [me, sibling, (*n1, c), (*n2, 1 - c), (*n2, c), (*n1, 1 - c), (*dg, c), (*dg, 1 - c)]
        leaves = [(me, sibling), (me, (*n1, c)), (me, (*n2, c)), ((*n1, c), (*n2, c)),
                  ((*n1, c), sibling), ((*n2, c), sibling), ((*dg, c), sibling)]
        lands = [sibling, (*n1, c), (*n2, c), (*dg, c), (*n2, 1 - c), (*n1, 1 - c), (*dg, 1 - c)]

        def slot(o, pos):
            return o.at[:, _block_of(pos)] if o is wp_ref else o.at[_block_of(pos)]

        def copy(a, k, block, to, src=None):
            o = outs[a]
            return pltpu.make_async_remote_copy(
                src_ref=slot(o, block) if src is None else src, dst_ref=slot(o, block),
                send_sem=send_sems.at[a, k], recv_sem=recv_sems.at[a, k],
                device_id=to, device_id_type=MESH)

        def send(a, k):
            block, to = leaves[k]
            return copy(a, k, block, to, src=srcs[a] if k < 3 else None)

        def recv(a, k):
            return copy(a, k, lands[k], me)

        def local_copies():
            return [pltpu.make_async_copy(srcs[a], slot(outs[a], me), local_sems.at[a]) for a in (1, 2)]

        def to_hbm(step):
            return pltpu.make_async_copy(slot(wbuf, consumed[step]), slot(wall_ref, consumed[step]),
                                         wall_sems.at[step])

        def at_step(step):
            return pl.when(jnp.logical_and(s == step, t == 0))

        @at_step(0)
        def _():
            slot(wbuf, me)[...] = win_ref[...].astype(BF16)
            st_p[0] = wpa_ref[...].astype(BF16)
            st_p[1] = wpb_ref[...].astype(BF16)
            st_p[2] = wo_ref[...].astype(BF16)
            for a in range(n_arr):
                send(a, 0).start()
                send(a, 1).start()
            for cp in local_copies():
                cp.start()
            to_hbm(0).start()

        @at_step(1)
        def _():
            recv(0, 0).wait_recv()
            to_hbm(1).start()

        for rnd in range(3):
            @at_step(2 + 2 * rnd)
            def _(rnd=rnd):
                if rnd == 0:
                    for a in range(n_arr):
                        send(a, 2).start()
                recv(0, 1 + rnd).wait_recv()
                if rnd == 0:
                    send(0, 3).start()
                send(0, 4 + rnd).start()
                to_hbm(2 + 2 * rnd).start()

            @at_step(3 + 2 * rnd)
            def _(rnd=rnd):
                for a in (1, 2):
                    recv(a, 1 + rnd).wait_recv()
                    if rnd == 0:
                        send(a, 3).start()
                    send(a, 4 + rnd).start()
                recv(0, 4 + rnd).wait_recv()
                to_hbm(3 + 2 * rnd).start()

        rows = pl.ds(pl.multiple_of(t * tm, tm), tm)

        @pl.when(s == 0)
        def _():
            xb = x_ref[...].astype(BF16)
            xbuf[rows, :] = xb
            xt_ref[...] = xb.T

        proj_ref[...] = _dot(xbuf[rows, :], wbuf[ord_ref[s]]) + b_ref[...]

        @pl.when(jnp.logical_and(s == N_DEV - 1, t == n_t - 1))
        def _():
            for a in (1, 2):
                for k in (0, 4, 5, 6):
                    recv(a, k).wait_recv()
            for a in range(n_arr):
                for k in range(7):
                    send(a, k).wait_send()
            for cp in local_copies() + [to_hbm(step) for step in range(N_DEV)]:
                cp.wait()

    grid_spec = pltpu.PrefetchScalarGridSpec(
        num_scalar_prefetch=1, grid=(N_DEV, n_t),
        in_specs=[pl.BlockSpec((tm, d_model), lambda s, t, o: (jnp.where(s == 0, t, n_t - 1), 0)),
                  pl.BlockSpec((None, 1, d_model), lambda s, t, o: (o[s], 0, 0)),
                  VMEM, VMEM, VMEM, VMEM, VMEM],
        out_specs=[pl.BlockSpec((tm, d_model), lambda s, t, o: (t, o[s])),
                   pl.BlockSpec((d_model, tm), lambda s, t, o: (0, jnp.where(s == 0, t, n_t))),
                   ANY, ANY, ANY],
        scratch_shapes=[pltpu.VMEM((N_DEV, d_model, d_model), BF16), pltpu.VMEM((n_tok, d_model), BF16),
                        pltpu.VMEM((3, r8, d_model), BF16),
                        pltpu.SemaphoreType.DMA((n_arr, 7)), pltpu.SemaphoreType.DMA((n_arr, 7)),
                        pltpu.SemaphoreType.DMA((3,)), pltpu.SemaphoreType.DMA((N_DEV,))])
    return pl.pallas_call(
        body, name="proj_all_gather", grid_spec=grid_spec,
        out_shape=[SDS((n_tok, N_DEV * d_model), F32), SDS((d_model, (n_t + 1) * tm), BF16),
                   SDS((N_DEV, d_model, d_model), BF16), SDS((3, N_DEV, r8, d_model), BF16),
                   SDS((N_DEV, kc, dc), F32)],
        compiler_params=_params(2),
    )(blk_order, x2, b_in3, w_in, w_pa, w_pb, w_o, conv_w)


def _conv_rows(ta):
    return _tile(ta, 64, SUBLANES)


def _branch_a_fwd(proj, conv_w_full, conv_b, gn_g, gn_b, seq):
    n_tok = proj.shape[0]
    d_model = conv_b.shape[1]
    ta = _tile(seq, ELEMENTWISE_ROWS, HALO)
    per_seq = seq // ta
    rc = _conv_rows(ta)

    def body(av_ref, ag_ref, gt_ref, avh_ref, agh_ref, cw_ref, cb_ref, gg_ref, gb_ref,
             h3_ref, h1_ref, ext):
        keep = jnp.where(pl.program_id(0) % per_seq == 0, 0.0, 1.0)

        def group(g, carry):
            sl = pl.ds(pl.multiple_of(g * LANES, LANES), LANES)
            ext[0:HALO, :] = avh_ref[:, sl] * _sigmoid(agh_ref[:, sl]) * keep
            ext[HALO:HALO + ta, :] = av_ref[:, sl] * _sigmoid(ag_ref[:, sl])
            for r0 in range(0, ta, rc):
                acc = jnp.broadcast_to(cb_ref[:, sl], (rc, LANES))
                for k in range(CONV_K):
                    acc = acc + ext[pl.ds(r0 + HALO - (CONV_K - 1) + k, rc), :] * cw_ref[k:k + 1, sl]
                h1_ref[pl.ds(r0, rc), sl] = acc
            h1 = h1_ref[:, sl]
            mu = jnp.mean(h1, axis=-1, keepdims=True)
            dlt = h1 - mu
            var = jnp.mean(dlt * dlt, axis=-1, keepdims=True)
            h2 = dlt * lax.rsqrt(var + LN_EPS) * gg_ref[:, sl] + gb_ref[:, sl]
            gate = gt_ref[:, sl]
            h3_ref[:, sl] = (h2 * _sigmoid(h2) * gate * _sigmoid(gate)).astype(BF16)
            return carry

        lax.fori_loop(0, d_model // LANES, group, 0, unroll=True)

    blk = lambda j: pl.BlockSpec((ta, d_model), lambda i: (i, j))
    halo = lambda j: pl.BlockSpec((HALO, d_model), lambda i: (jnp.maximum(i * (ta // HALO) - 1, 0), j))
    row = pl.BlockSpec((1, d_model), lambda i: (0, 0))
    return pl.pallas_call(
        body, name="branch_a_fwd", grid=(n_tok // ta,),
        in_specs=[blk(0), blk(1), blk(2), halo(0), halo(1),
                  pl.BlockSpec((HALO, d_model), lambda i: (0, 0)), row, row, row],
        out_specs=[pl.BlockSpec((ta, d_model), lambda i: (i, 0))] * 2,
        out_shape=[SDS((n_tok, d_model), BF16), SDS((n_tok, d_model), F32)],
        scratch_shapes=[pltpu.VMEM((HALO + ta, LANES), F32)],
        compiler_params=_params(1),
    )(proj, proj, proj, proj, proj, conv_w_full, conv_b, gn_g, gn_b)


def _branch_b_fwd(proj, ln_g, ln_b, w_spatial, b_spatial_t, seq):
    n_tok = proj.shape[0]
    d_model = ln_g.shape[1]
    n_heads = d_model // LANES
    tb = _tile(seq, ELEMENTWISE_ROWS, LANES)

    def body(u_ref, v_ref, bg_ref, lg_ref, lb_ref, ws_ref, bs_ref, s_ref, vn_buf):
        v, _ = _gelu_and_grad(v_ref[...])
        mu = jnp.mean(v, axis=-1, keepdims=True)
        dlt = v - mu
        var = jnp.mean(dlt * dlt, axis=-1, keepdims=True)
        vn_buf[...] = (dlt * lax.rsqrt(var + LN_EPS) * lg_ref[...] + lb_ref[...]).astype(BF16)
        tril = _tril_mask()
        for h in range(n_heads):
            cols = slice(h * LANES, (h + 1) * LANES)
            w_h = jnp.where(tril, ws_ref[h], 0.0).astype(BF16)
            bias = bs_ref[:, h:h + 1]
            for ch in range(tb // LANES):
                rows = slice(ch * LANES, (ch + 1) * LANES)
                mix = _dot(w_h, vn_buf[rows, cols]) + bias
                u, _ = _gelu_and_grad(u_ref[rows, cols])
                gate = bg_ref[rows, cols]
                s_ref[rows, cols] = (u * mix * gate * _sigmoid(gate)).astype(BF16)

    blk = lambda j: pl.BlockSpec((tb, d_model), lambda i: (i, j))
    row = pl.BlockSpec((1, d_model), lambda i: (0, 0))
    return pl.pallas_call(
        body, name="branch_b_fwd", grid=(n_tok // tb,),
        in_specs=[blk(3), blk(4), blk(5), row, row,
                  pl.BlockSpec((n_heads, LANES, LANES), lambda i: (0, 0, 0)),
                  pl.BlockSpec((LANES, n_heads), lambda i: (0, 0))],
        out_specs=pl.BlockSpec((tb, d_model), lambda i: (i, 0)),
        out_shape=SDS((n_tok, d_model), BF16),
        scratch_shapes=[pltpu.VMEM((tb, d_model), BF16)],
        compiler_params=_params(1),
    )(proj, proj, proj, ln_g, ln_b, w_spatial, b_spatial_t)


MID_ROWS = 8


def _mid(h3, s, proj, x2, target, wp_full, b_o, lo_g, lo_b):
    n_tok, d_model = x2.shape
    tm = _tile(n_tok, 256, 16)

    def body(h3_ref, s_ref, ma_ref, mb_ref, x_ref, t_ref, wpa_ref, wpb_ref, wo_ref, bo_ref,
             lg_ref, lb_ref, dproj_ref, dh3_ref, ds_ref, dr_ref, lhs3_ref, rhs3_ref, vec_ref):
        @pl.when(pl.program_id(0) == 0)
        def _():
            vec_ref[...] = jnp.zeros_like(vec_ref)

        h3 = h3_ref[...]
        s = s_ref[...]
        ya = _dot(h3, wpa_ref[...])
        yb = _dot(s, wpb_ref[...])
        ga = _sigmoid(ma_ref[...])
        gb = _sigmoid(mb_ref[...])
        mixed = (ga * ya + gb * yb).astype(BF16)
        lhs3_ref[0] = h3
        lhs3_ref[1] = s
        lhs3_ref[2] = mixed
        r = DEEPNORM_ALPHA * x_ref[...] + _dot(mixed, wo_ref[...]) + bo_ref[...]
        mu = jnp.mean(r, axis=-1, keepdims=True)
        dlt = r - mu
        rstd = lax.rsqrt(jnp.mean(dlt * dlt, axis=-1, keepdims=True) + LN_EPS)
        rhat = dlt * rstd
        diff = rhat * lg_ref[...] + lb_ref[...] - t_ref[...]
        dy = diff * (1.0 / d_model)
        vec_ref[0:1, :] += _colsum(dy * rhat)
        vec_ref[1:2, :] += _colsum(dy)
        vec_ref[5:6, :] += _colsum(diff * diff)
        drh = dy * lg_ref[...]
        dr = rstd * (drh - jnp.mean(drh, axis=-1, keepdims=True)
                     - rhat * jnp.mean(drh * rhat, axis=-1, keepdims=True))
        vec_ref[2:3, :] += _colsum(dr)
        dr_ref[...] = dr
        drb = dr.astype(BF16)
        rhs3_ref[2] = drb
        dmixed = _dot_tb(drb, wo_ref[...])
        dya_f = dmixed * ga
        dyb_f = dmixed * gb
        dma = dya_f * ya * (1.0 - ga)
        dmb = dyb_f * yb * (1.0 - gb)
        vec_ref[3:4, :] += _colsum(dma)
        vec_ref[4:5, :] += _colsum(dmb)
        dproj_ref[:, 0:d_model] = dma.astype(BF16)
        dproj_ref[:, d_model:2 * d_model] = dmb.astype(BF16)
        dya = dya_f.astype(BF16)
        dyb = dyb_f.astype(BF16)
        rhs3_ref[0] = dya
        rhs3_ref[1] = dyb
        dh3_ref[...] = _dot_tb(dya, wpa_ref[...])
        ds_ref[...] = _dot_tb(dyb, wpb_ref[...])

    tile = pl.BlockSpec((tm, d_model), lambda i: (i, 0))
    full = lambda a: pl.BlockSpec((None, d_model, d_model), lambda i: (a, 0, 0))
    row = pl.BlockSpec((1, d_model), lambda i: (0, 0))
    stack = pl.BlockSpec((3, tm, d_model), lambda i: (0, i, 0))
    bf3 = SDS((3, n_tok, d_model), BF16)
    f32 = SDS((n_tok, d_model), F32)
    return pl.pallas_call(
        body, name="mid", grid=(n_tok // tm,),
        in_specs=[tile, tile, pl.BlockSpec((tm, d_model), lambda i: (i, 6)),
                  pl.BlockSpec((tm, d_model), lambda i: (i, 7)), tile, tile, full(0), full(1), full(2),
                  row, row, row],
        out_specs=[pl.BlockSpec((tm, 2 * d_model), lambda i: (i, 3)), tile, tile, tile, stack, stack,
                   pl.BlockSpec((MID_ROWS, d_model), lambda i: (0, 0))],
        out_shape=[SDS((n_tok, N_DEV * d_model), BF16), f32, f32, f32, bf3, bf3,
                   SDS((MID_ROWS, d_model), F32)],
        compiler_params=_params(1),
    )(h3, s, proj, proj, x2, target, wp_full, wp_full, wp_full, b_o, lo_g, lo_b)


B_ROWS = 8


def _branch_b_bwd(dproj, proj, d_s, ln_g, ln_b, w_spatial, b_spatial_t, seq):
    n_tok = proj.shape[0]
    d_model = ln_g.shape[1]
    n_heads = d_model // LANES
    tb = _tile(seq, ELEMENTWISE_ROWS, LANES)

    def body(dproj_in, u_ref, v_ref, bg_ref, ds_ref, lg_ref, lb_ref, ws_ref, bs_ref,
             dproj_ref, vec_ref, dws_ref, dbs_ref, vn_buf, dv_buf):
        del dproj_in

        @pl.when(pl.program_id(0) == 0)
        def _():
            vec_ref[...] = jnp.zeros_like(vec_ref)
            dws_ref[...] = jnp.zeros_like(dws_ref)
            dbs_ref[...] = jnp.zeros_like(dbs_ref)

        v, dgelu_v = _gelu_and_grad(v_ref[...])
        mu = jnp.mean(v, axis=-1, keepdims=True)
        dlt = v - mu
        rstd = lax.rsqrt(jnp.mean(dlt * dlt, axis=-1, keepdims=True) + LN_EPS)
        vhat = dlt * rstd
        vn_buf[...] = (vhat * lg_ref[...] + lb_ref[...]).astype(BF16)
        tril = _tril_mask()
        for h in range(n_heads):
            cols = slice(h * LANES, (h + 1) * LANES)
            w_h = jnp.where(tril, ws_ref[h], 0.0).astype(BF16)
            bias = bs_ref[:, h:h + 1]
            for ch in range(tb // LANES):
                rows = slice(ch * LANES, (ch + 1) * LANES)
                vn = vn_buf[rows, cols]
                mix = _dot(w_h, vn) + bias
                u, dgelu_u = _gelu_and_grad(u_ref[rows, cols])
                sg, dsilu = _silu_and_grad(bg_ref[rows, cols])
                dsv = ds_ref[rows, cols]
                du = dsv * mix * sg * dgelu_u
                dbg = dsv * u * mix * dsilu
                dmix = dsv * u * sg
                dmix_bf = dmix.astype(BF16)
                dproj_ref[rows, cols] = du.astype(BF16)
                dproj_ref[rows, 2 * d_model + h * LANES:2 * d_model + (h + 1) * LANES] = dbg.astype(BF16)
                vec_ref[0:1, cols] += _colsum(du)
                vec_ref[2:3, cols] += _colsum(dbg)
                dbs_ref[:, h:h + 1] += jnp.sum(dmix, axis=1, keepdims=True)
                dws_ref[h] += jnp.where(tril, _dot_tb(dmix_bf, vn), 0.0)
                dv_buf[rows, cols] = _dot_ta(w_h, dmix_bf)
        dvn = dv_buf[...]
        vec_ref[3:4, :] += _colsum(dvn * vhat)
        vec_ref[4:5, :] += _colsum(dvn)
        dvh = dvn * lg_ref[...]
        dv = rstd * (dvh - jnp.mean(dvh, axis=-1, keepdims=True)
                     - vhat * jnp.mean(dvh * vhat, axis=-1, keepdims=True)) * dgelu_v
        vec_ref[1:2, :] += _colsum(dv)
        dproj_ref[:, d_model:2 * d_model] = dv.astype(BF16)

    blk = lambda j: pl.BlockSpec((tb, d_model), lambda i: (i, j))
    row = pl.BlockSpec((1, d_model), lambda i: (0, 0))
    return pl.pallas_call(
        body, name="branch_b_bwd", grid=(n_tok // tb,),
        in_specs=[ANY, blk(3), blk(4), blk(5), pl.BlockSpec((tb, d_model), lambda i: (i, 0)), row, row,
                  pl.BlockSpec((n_heads, LANES, LANES), lambda i: (0, 0, 0)),
                  pl.BlockSpec((LANES, n_heads), lambda i: (0, 0))],
        out_specs=[pl.BlockSpec((tb, 3 * d_model), lambda i: (i, 1)),
                   pl.BlockSpec((B_ROWS, d_model), lambda i: (0, 0)),
                   pl.BlockSpec((n_heads, LANES, LANES), lambda i: (0, 0, 0)),
                   pl.BlockSpec((LANES, n_heads), lambda i: (0, 0))],
        out_shape=[SDS(dproj.shape, BF16), SDS((B_ROWS, d_model), F32),
                   SDS((n_heads, LANES, LANES), F32), SDS((LANES, n_heads), F32)],
        scratch_shapes=[pltpu.VMEM((tb, d_model), BF16), pltpu.VMEM((tb, d_model), F32)],
        input_output_aliases={0: 0},
        compiler_params=_params(1),
    )(dproj, proj, proj, proj, d_s, ln_g, ln_b, w_spatial, b_spatial_t)


A1_ROWS = 8


def _branch_a_bwd_norm(dproj, proj, h1, d_h3, gn_g, gn_b, seq):
    n_tok = proj.shape[0]
    d_model = gn_g.shape[1]
    ta = _tile(seq, ELEMENTWISE_ROWS, 16)

    def body(dproj_in, gt_ref, h1_ref, dh3_ref, gg_ref, gb_ref, dproj_ref, dh1_ref, vec_ref):
        del dproj_in

        @pl.when(pl.program_id(0) == 0)
        def _():
            vec_ref[...] = jnp.zeros_like(vec_ref)

        def group(g, carry):
            sl = pl.ds(pl.multiple_of(g * LANES, LANES), LANES)
            h1 = h1_ref[:, sl]
            mu = jnp.mean(h1, axis=-1, keepdims=True)
            dlt = h1 - mu
            rstd = lax.rsqrt(jnp.mean(dlt * dlt, axis=-1, keepdims=True) + LN_EPS)
            nrm = dlt * rstd
            sw, dsw = _silu_and_grad(nrm * gg_ref[:, sl] + gb_ref[:, sl])
            sg, dsg = _silu_and_grad(gt_ref[:, sl])
            dh3 = dh3_ref[:, sl]
            dgate = dh3 * sw * dsg
            dproj_ref[:, sl] = dgate.astype(BF16)
            vec_ref[0:1, sl] += _colsum(dgate)
            dh2 = dh3 * sg * dsw
            vec_ref[1:2, sl] += _colsum(dh2 * nrm)
            vec_ref[2:3, sl] += _colsum(dh2)
            dn = dh2 * gg_ref[:, sl]
            dh1 = rstd * (dn - jnp.mean(dn, axis=-1, keepdims=True)
                          - nrm * jnp.mean(dn * nrm, axis=-1, keepdims=True))
            vec_ref[3:4, sl] += _colsum(dh1)
            dh1_ref[:, sl] = dh1
            return carry

        lax.fori_loop(0, d_model // LANES, group, 0, unroll=True)

    tile = pl.BlockSpec((ta, d_model), lambda i: (i, 0))
    row = pl.BlockSpec((1, d_model), lambda i: (0, 0))
    return pl.pallas_call(
        body, name="branch_a_bwd_norm", grid=(n_tok // ta,),
        in_specs=[ANY, pl.BlockSpec((ta, d_model), lambda i: (i, 2)), tile, tile, row, row],
        out_specs=[pl.BlockSpec((ta, d_model), lambda i: (i, 2)), tile,
                   pl.BlockSpec((A1_ROWS, d_model), lambda i: (0, 0))],
        out_shape=[SDS(dproj.shape, BF16), SDS((n_tok, d_model), F32), SDS((A1_ROWS, d_model), F32)],
        input_output_aliases={0: 0},
        compiler_params=_params(1),
    )(dproj, proj, h1, d_h3, gn_g, gn_b)


A2_ROWS = 8


def _branch_a_bwd_conv(dproj, proj, d_h1, conv_w_full, gp_bf, gp_f32, seq):
    n_tok = proj.shape[0]
    d_model = conv_w_full.shape[1]
    n_groups = d_model // LANES
    ta = _tile(seq, ELEMENTWISE_ROWS, HALO)
    n_tiles = n_tok // ta
    per_seq = seq // ta
    rc = _conv_rows(ta)
    last_halo = n_tok // HALO - 1
    r8 = d_model // N_DEV
    prow = _tile(r8, 32, 16)

    def body(dproj_in, av_ref, ag_ref, avh_ref, agh_ref, dh1_ref, dh1h_ref, cw_ref, gp_bf_ref, gp_f32_ref,
             dproj_ref, vec_ref, dcw_ref, opa_ref, opb_ref, opo_ref,
             ext_h0, ext_d, rbuf, own, send_sems, recv_sems, local_sems):
        del dproj_in
        i = pl.program_id(0)
        x, y, c = _mesh_pos()
        me = _block_of((x, y, c))
        peers = _peers()

        def sends():
            return [pltpu.make_async_remote_copy(
                src_ref=gp_bf_ref.at[a, pl.ds(pl.multiple_of(blk * r8, 16), r8), :], dst_ref=rbuf.at[k, a],
                send_sem=send_sems.at[a, k], recv_sem=recv_sems.at[a, k], device_id=pos, device_id_type=MESH)
                for k, (pos, blk) in enumerate(peers) for a in range(3)]

        def own_rows():
            return [pltpu.make_async_copy(gp_f32_ref.at[a, pl.ds(pl.multiple_of(me * r8, 8), r8), :],
                                          own.at[a], local_sems.at[a]) for a in range(3)]

        @pl.when(i == 0)
        def _():
            vec_ref[...] = jnp.zeros_like(vec_ref)
            dcw_ref[...] = jnp.zeros_like(dcw_ref)
            for cp in sends() + own_rows():
                cp.start()

        keep_past = jnp.where(i % per_seq == 0, 0.0, 1.0)
        keep_next = jnp.where(i % per_seq == per_seq - 1, 0.0, 1.0)

        def group(g, carry):
            sl = pl.ds(pl.multiple_of(g * LANES, LANES), LANES)
            av = av_ref[:, sl]
            sig = _sigmoid(ag_ref[:, sl])
            ext_h0[0:HALO, :] = avh_ref[:, sl] * _sigmoid(agh_ref[:, sl]) * keep_past
            ext_h0[HALO:HALO + ta, :] = av * sig
            ext_d[0:ta, :] = dh1_ref[:, sl]
            ext_d[ta:ta + HALO, :] = dh1h_ref[:, sl] * keep_next
            for r0 in range(0, ta, rc):
                dh1 = ext_d[pl.ds(r0, rc), :]
                acc = jnp.zeros((rc, LANES), F32)
                for k in range(CONV_K):
                    acc = acc + ext_d[pl.ds(r0 + CONV_K - 1 - k, rc), :] * cw_ref[k:k + 1, sl]
                    prod = dh1 * ext_h0[pl.ds(r0 + HALO - (CONV_K - 1) + k, rc), :]
                    dcw_ref[g, k] += jnp.sum(prod.reshape(rc // SUBLANES, SUBLANES, LANES), axis=0)
                rows = pl.ds(r0, rc)
                sig_r = sig[r0:r0 + rc]
                dav = acc * sig_r
                dag = dav * av[r0:r0 + rc] * (1.0 - sig_r)
                dproj_ref[rows, sl] = dav.astype(BF16)
                dproj_ref[rows, pl.ds(pl.multiple_of(d_model + g * LANES, LANES), LANES)] = dag.astype(BF16)
                vec_ref[0:1, sl] += _colsum(dav)
                vec_ref[1:2, sl] += _colsum(dag)
            return carry

        lax.fori_loop(0, n_groups, group, 0, unroll=True)

        @pl.when(i == n_tiles - 1)
        def _():
            for cp in own_rows():
                cp.wait()
            for cp in sends():
                cp.wait_recv()
            for a, o in enumerate([opa_ref, opb_ref, opo_ref]):
                for q in range(r8 // prow):
                    r = pl.ds(q * prow, prow)
                    tot = own[a, r, :]
                    for k in range(N_DEV - 1):
                        tot = tot + rbuf[k, a, r, :].astype(F32)
                    o[r, :] = tot
            for cp in sends():
                cp.wait_send()

    blk = lambda j: pl.BlockSpec((ta, d_model), lambda i: (i, j))
    halo = lambda j: pl.BlockSpec((HALO, d_model), lambda i: (jnp.maximum(i * (ta // HALO) - 1, 0), j))
    shard = pl.BlockSpec((r8, d_model), lambda i: (0, 0))
    return pl.pallas_call(
        body, name="branch_a_bwd_conv", grid=(n_tiles,),
        in_specs=[ANY, blk(0), blk(1), halo(0), halo(1), pl.BlockSpec((ta, d_model), lambda i: (i, 0)),
                  pl.BlockSpec((HALO, d_model), lambda i: (jnp.minimum((i + 1) * (ta // HALO), last_halo), 0)),
                  pl.BlockSpec((HALO, d_model), lambda i: (0, 0)), ANY, ANY],
        out_specs=[pl.BlockSpec((ta, 2 * d_model), lambda i: (i, 0)),
                   pl.BlockSpec((A2_ROWS, d_model), lambda i: (0, 0)),
                   pl.BlockSpec((n_groups, HALO, SUBLANES, LANES), lambda i: (0, 0, 0, 0)),
                   shard, shard, shard],
        out_shape=[SDS(dproj.shape, BF16), SDS((A2_ROWS, d_model), F32),
                   SDS((n_groups, HALO, SUBLANES, LANES), F32)] + [SDS((r8, d_model), F32)] * 3,
        scratch_shapes=[pltpu.VMEM((HALO + ta, LANES), F32), pltpu.VMEM((ta + HALO, LANES), F32),
                        pltpu.VMEM((N_DEV - 1, 3, r8, d_model), BF16), pltpu.VMEM((3, r8, d_model), F32),
                        pltpu.SemaphoreType.DMA((3, 7)), pltpu.SemaphoreType.DMA((3, 7)),
                        pltpu.SemaphoreType.DMA((3,))],
        input_output_aliases={0: 0},
        compiler_params=_params(1),
    )(dproj, proj, proj, proj, proj, d_h1, d_h1, conv_w_full, gp_bf, gp_f32)


def _weight_grads(lhs3, rhs3):
    n_mat, n_tok, d_model = lhs3.shape
    tk = _tile(n_tok, 2048, 16)
    n_k = n_tok // tk

    def body(a_ref, g_ref, o_ref, ob_ref):
        part = _dot_ta(a_ref[...], g_ref[...])

        @pl.when(pl.program_id(1) == 0)
        def _():
            o_ref[...] = part

        @pl.when(pl.program_id(1) != 0)
        def _():
            o_ref[...] += part

        @pl.when(pl.program_id(1) == n_k - 1)
        def _():
            ob_ref[...] = o_ref[...].astype(BF16)

    tile = pl.BlockSpec((None, tk, d_model), lambda a, i: (a, i, 0))
    out = pl.BlockSpec((None, d_model, d_model), lambda a, i: (a, 0, 0))
    return pl.pallas_call(
        body, name="grad_w_pa_pb_o", grid=(n_mat, n_k), in_specs=[tile, tile], out_specs=[out, out],
        out_shape=[SDS((n_mat, d_model, d_model), F32), SDS((n_mat, d_model, d_model), BF16)],
        compiler_params=_params(2),
    )(lhs3, rhs3)


def _packed_rows(pieces, plan, d_model):
    rows = {"row": lambda a: d_model // LANES, "block": lambda a: a.shape[0],
            "tiles": lambda a: a.shape[0] * a.shape[1]}
    return sum(rows[kind](pieces[p]) for kind, p, _ in plan)


def _grad_w_in_reduce_scatter(xt_bf, dproj, blk_order, pieces, plan):
    n_tok = dproj.shape[0]
    d_model = xt_bf.shape[0]
    dh = d_model // 2
    n_units = 2 * N_DEV
    n_pc = len(pieces)
    n_packed = _packed_rows(pieces, plan, d_model)
    n_part = n_packed + -n_packed % (N_DEV * SUBLANES)
    rsl = n_part // N_DEV

    xt_cols = n_tok // XT_CHUNKS

    def body(ord_ref, a_hbm, g_ref, *rest):
        pc_refs, rest = rest[:n_pc], rest[n_pc:]
        (o_ref, small_ref, acc, fb, sbuf, gbuf, tbuf, rfin, rbuf_s, red, p_ref, a_ref,
         send_f, send_s, recv_g, recv_t, recv_f, out_sems, send1, recv1, send2, recv2, xt_sems) = rest
        del ord_ref
        u = pl.program_id(0)
        s = u // 2
        hf = u % 2
        rnd = s // 2
        x, y, c = _mesh_pos()
        sibling = (x, y, 1 - c)
        me = _block_of((x, y, c))
        peers = _peers()

        def fetch_xt(q):
            cols = pl.ds(q * xt_cols, xt_cols)
            return pltpu.make_async_copy(a_hbm.at[:, cols], a_ref.at[:, cols], xt_sems.at[q])

        def rows_of(blk):
            return pl.ds(pl.multiple_of(blk * rsl, SUBLANES), rsl)

        def scatter():
            return [pltpu.make_async_remote_copy(
                src_ref=p_ref.at[rows_of(blk), :], dst_ref=rbuf_s.at[k], send_sem=send1.at[k],
                recv_sem=recv1.at[k], device_id=pos, device_id_type=MESH) for k, (pos, blk) in enumerate(peers)]

        def gather(dst_block=None):
            return [pltpu.make_async_remote_copy(
                src_ref=red, dst_ref=small_ref.at[rows_of(me if dst_block is None else blk), :],
                send_sem=send2.at[k], recv_sem=recv2.at[k], device_id=pos, device_id_type=MESH)
                for k, (pos, blk) in enumerate(peers)]

        def own_slice():
            return pltpu.make_async_copy(red, small_ref.at[rows_of(me), :], out_sems.at[2])

        def pack():
            at = 0
            for kind, p, r in plan:
                ref = pc_refs[p]
                if kind == "row":
                    for q in range(d_model // LANES):
                        p_ref[at + q:at + q + 1, :] = ref[r:r + 1, q * LANES:(q + 1) * LANES]
                    at += d_model // LANES
                elif kind == "block":
                    p_ref[at:at + ref.shape[0], :] = ref[...]
                    at += ref.shape[0]
                else:
                    for tile in range(ref.shape[0] * ref.shape[1]):
                        p_ref[at + tile:at + tile + 1, :] = _colsum(ref[tile // ref.shape[1], tile % ref.shape[1]])
                    at += ref.shape[0] * ref.shape[1]
            if at < n_part:
                p_ref[at:n_part, :] = jnp.zeros((n_part - at, LANES), F32)

        @pl.when(u == 0)
        def _():
            for q in range(XT_CHUNKS):
                fetch_xt(q).start()
            pack()
            for cp in scatter():
                cp.start()

        @pl.when(u == 8)
        def _():
            for cp in scatter():
                cp.wait_recv()
            tot = p_ref[rows_of(me), :]
            for k in range(N_DEV - 1):
                tot = tot + rbuf_s[k]
            red[...] = tot
            own_slice().start()
            for cp in gather():
                cp.start()

        n1 = (jnp.bitwise_xor(x, c), jnp.bitwise_xor(y, 1 - c), c)
        n2 = (jnp.bitwise_xor(x, 1 - c), jnp.bitwise_xor(y, c), c)

        def feed(r, half):
            return pltpu.make_async_remote_copy(
                src_ref=fb.at[half], dst_ref=gbuf.at[r, half], send_sem=send_f.at[r, half],
                recv_sem=recv_g.at[r, half], device_id=sibling, device_id_type=MESH)

        def feed_sibling(half):
            return pltpu.make_async_remote_copy(
                src_ref=fb.at[half], dst_ref=rfin.at[0, half], send_sem=send_f.at[3, half],
                recv_sem=recv_f.at[0, half], device_id=sibling, device_id_type=MESH)

        def chip_sum(r, half):
            dst = [tbuf.at[half], rfin.at[2, half], rfin.at[1, half]][r]
            sem = [recv_t.at[half], recv_f.at[2, half], recv_f.at[1, half]][r]
            return pltpu.make_async_remote_copy(
                src_ref=sbuf.at[r, half], dst_ref=dst, send_sem=send_s.at[r, half], recv_sem=sem,
                device_id=[n2, n2, n1][r], device_id_type=MESH)

        def out_copy(half):
            return pltpu.make_async_copy(acc.at[half], o_ref.at[:, pl.ds(half * dh, dh)], out_sems.at[half])

        def partial_sum(first=False):
            if not first:
                return _dot(a_ref[...], g_ref[...])
            tot = None
            for q in range(XT_CHUNKS):
                fetch_xt(q).wait()
                part = _dot(a_ref[:, q * xt_cols:(q + 1) * xt_cols], g_ref[q * xt_cols:(q + 1) * xt_cols, :])
                tot = part if tot is None else tot + part
            return tot

        for half in range(2):
            for r in range(3):
                @pl.when(u == 4 * r + 4 + half)
                def _(r=r, half=half):
                    feed(r, half).wait_send()

                @pl.when(u == 4 * r + 2 + half)
                def _(r=r, half=half):
                    feed(r, half).wait_recv()
                    if r == 2:
                        chip_sum(0, half).wait_recv()

            @pl.when(u == 14 + half)
            def _(half=half):
                feed_sibling(half).wait_recv()
                chip_sum(2, half).wait_recv()
                chip_sum(1, half).wait_recv()

        @pl.when(u == 0)
        def _():
            fb[0] = partial_sum(first=True).astype(BF16)

        @pl.when(jnp.logical_and(jnp.logical_and(s % 2 == 0, s < 7), u > 0))
        def _():
            fb[hf] = partial_sum().astype(BF16)

        @pl.when(jnp.logical_or(s == 1, s == 3))
        def _():
            sbuf[rnd, hf] = (partial_sum() + gbuf[rnd, hf].astype(F32)).astype(BF16)

        @pl.when(s == 5)
        def _():
            sbuf[2, hf] = (partial_sum() + gbuf[2, hf].astype(F32) + tbuf[hf].astype(F32)).astype(BF16)

        @pl.when(s == 7)
        def _():
            acc[hf] = (partial_sum() + rfin[0, hf].astype(F32) + rfin[1, hf].astype(F32)
                       + rfin[2, hf].astype(F32))

        for half in range(2):
            for r in range(3):
                @pl.when(u == 4 * r + half)
                def _(r=r, half=half):
                    feed(r, half).start()

                @pl.when(u == 4 * r + 2 + half)
                def _(r=r, half=half):
                    chip_sum(r, half).start()

            @pl.when(u == 12 + half)
            def _(half=half):
                feed_sibling(half).start()

        @pl.when(u == 14)
        def _():
            out_copy(0).start()

        @pl.when(u == 15)
        def _():
            out_copy(1).start()
            for half in range(2):
                feed_sibling(half).wait_send()
                for r in range(3):
                    chip_sum(r, half).wait_send()
                out_copy(half).wait()
            for cp in gather("theirs"):
                cp.wait_recv()
            for cp in scatter() + gather():
                cp.wait_send()
            own_slice().wait()

    grid_spec = pltpu.PrefetchScalarGridSpec(
        num_scalar_prefetch=1, grid=(n_units,),
        in_specs=[ANY, pl.BlockSpec((n_tok, dh), lambda u, o: (0, 2 * o[u // 2] + u % 2))] + [VMEM] * n_pc,
        out_specs=[ANY, ANY],
        scratch_shapes=[pltpu.VMEM((2, d_model, dh), F32), pltpu.VMEM((2, d_model, dh), BF16),
                        pltpu.VMEM((3, 2, d_model, dh), BF16), pltpu.VMEM((3, 2, d_model, dh), BF16),
                        pltpu.VMEM((2, d_model, dh), BF16), pltpu.VMEM((3, 2, d_model, dh), BF16),
                        pltpu.VMEM((N_DEV - 1, rsl, LANES), F32), pltpu.VMEM((rsl, LANES), F32),
                        pltpu.VMEM((n_part, LANES), F32), pltpu.VMEM((d_model, n_tok), BF16),
                        pltpu.SemaphoreType.DMA((4, 2)), pltpu.SemaphoreType.DMA((3, 2)),
                        pltpu.SemaphoreType.DMA((3, 2)), pltpu.SemaphoreType.DMA((2,)),
                        pltpu.SemaphoreType.DMA((3, 2)),
                        pltpu.SemaphoreType.DMA((3,))] + [pltpu.SemaphoreType.DMA((N_DEV - 1,))] * 4
        + [pltpu.SemaphoreType.DMA((XT_CHUNKS,))])
    return pl.pallas_call(
        body, name="grad_w_in_reduce_scatter", grid_spec=grid_spec,
        out_shape=[SDS((d_model, d_model), F32), SDS((n_part, LANES), F32)], compiler_params=_params(1),
    )(blk_order, xt_bf, dproj, *pieces)


def _grad_x_adamw(dproj, w_all, dr, w, g, m, v, small, loss_rows, packed, groups):
    n_tok, d_model = dr.shape
    tm = _tile(n_tok, 256, 16)
    n_steps = n_tok // tm
    tr = w.shape[0] // n_steps
    n_pk, n_grp = len(packed), len(groups)
    n_out = 4 * (n_pk + n_grp)

    def body(dp_ref, w_hbm, dr_ref, ws_ref, g_ref, m_ref, v_ref, small_ref, *rest):
        pk_in, rest = rest[:3 * n_pk], rest[3 * n_pk:]
        grp_in, rest = rest[:4 * n_grp], rest[4 * n_grp:]
        o_ref, gp_ref, d_ref, mo_ref, vo_ref, loss_ref = rest[:6]
        outs, (w_ref, w_sems) = rest[6:6 + n_out], rest[6 + n_out:]
        i = pl.program_id(0)

        def fetch(j):
            return pltpu.make_async_copy(w_hbm.at[j], w_ref.at[j], w_sems.at[j])

        def grad_x_tile(first):
            acc = DEEPNORM_ALPHA * dr_ref[...]
            for j in range(N_DEV):
                if first:
                    fetch(j).wait()
                acc = acc + _dot_tb(dp_ref[:, j * d_model:(j + 1) * d_model], w_ref[j])
            o_ref[...] = acc

        def small_updates():
            def update(p, pw, grad, pm, pv):
                og, od, om, ov = outs[4 * p:4 * p + 4]
                og[...] = grad
                od[...], om[...], ov[...] = _adamw_math(pw[...], grad, pm[...], pv[...])

            for p, (pw_arr, _, _, row0) in enumerate(packed):
                pw, pm, pv = pk_in[3 * p:3 * p + 3]
                update(p, pw, small_ref[row0:row0 + pw_arr.shape[0], :], pm, pv)
            for p in range(n_grp):
                pw, pg, pm, pv = grp_in[4 * p:4 * p + 4]
                update(n_pk + p, pw, pg[...], pm, pv)
            sq = small_ref[loss_rows[0]:loss_rows[0] + loss_rows[1], :]
            loss_ref[...] = jnp.sum(_colsum(sq), axis=1, keepdims=True) * (0.5 / d_model)

        @pl.when(i == 0)
        def _():
            for j in range(N_DEV):
                fetch(j).start()
            small_updates()
            grad_x_tile(True)

        @pl.when(i != 0)
        def _():
            grad_x_tile(False)

        grad = g_ref[...]
        gp_ref[...] = grad
        d_ref[...], mo_ref[...], vo_ref[...] = _adamw_math(ws_ref[...], grad, m_ref[...], v_ref[...])

    tile = pl.BlockSpec((tm, d_model), lambda i: (i, 0))
    slab = pl.BlockSpec((tr, w.shape[1]), lambda i: (i, 0))
    flat = [a for pk in packed for a in pk[:3]] + [a for grp in groups for a in grp]
    shapes = [pk[0].shape for pk in packed] + [grp[0].shape for grp in groups]
    res = pl.pallas_call(
        body, name="grad_x_adamw", grid=(n_steps,),
        in_specs=[pl.BlockSpec((tm, N_DEV * d_model), lambda i: (i, 0)), ANY, tile, slab, slab, slab, slab]
        + [VMEM] * (1 + len(flat)),
        out_specs=[tile, slab, slab, slab, slab, VMEM] + [VMEM] * n_out,
        out_shape=[SDS((n_tok, d_model), F32)] + [SDS(w.shape, F32)] * 4 + [SDS((1, 1), F32)]
        + [SDS(shape, F32) for shape in shapes for _ in range(4)],
        scratch_shapes=[pltpu.VMEM(w_all.shape, BF16), pltpu.SemaphoreType.DMA((N_DEV,))],
        compiler_params=_params(1),
    )(dproj, w_all, dr, w, g, m, v, small, *flat)
    return (res[0], tuple(res[1:5]), [tuple(res[6 + 4 * p:10 + 4 * p]) for p in range(n_pk + n_grp)],
            res[5].reshape(()))


def _adamw_math(w, g, m, v):
    m = ADAM_B1 * m + (1.0 - ADAM_B1) * g
    v = ADAM_B2 * v + (1.0 - ADAM_B2) * (g * g)
    m_hat = m / (1.0 - ADAM_B1 ** ADAM_STEP)
    v_hat = v / (1.0 - ADAM_B2 ** ADAM_STEP)
    delta = -ADAM_LR * (m_hat / (jnp.sqrt(v_hat) + ADAM_EPS) + ADAM_WD * w)
    return delta, m, v


def _as_rows(a):
    return a.reshape(-1, LANES)


def kernel(x, w_in, b_in, conv_w, conv_b, gn_g, gn_b, ln_v_g, ln_v_b, w_spatial, b_spatial, w_pa, w_pb, w_o, b_o, ln_out_g, ln_out_b, loss_target, m_w_in, m_b_in, m_conv_w, m_conv_b, m_gn_g, m_gn_b, m_ln_v_g, m_ln_v_b, m_w_spatial, m_b_spatial, m_w_pa, m_w_pb, m_w_o, m_b_o, m_ln_out_g, m_ln_out_b, v_w_in, v_b_in, v_conv_w, v_conv_b, v_gn_g, v_gn_b, v_ln_v_g, v_ln_v_b, v_w_spatial, v_b_spatial, v_w_pa, v_w_pb, v_w_o, v_b_o, v_ln_out_g, v_ln_out_b):
    n_batch, seq, d_model = x.shape
    n_tok = n_batch * seq
    n_heads = d_model // LANES
    dc = conv_w.shape[1]
    me = 4 * lax.axis_index("x") + 2 * lax.axis_index("y") + lax.axis_index("c")
    row = lambda a: a.reshape(1, d_model)

    x2 = x.reshape(n_tok, d_model)
    target2 = loss_target.reshape(n_tok, d_model)
    b_spatial_t = b_spatial.T

    first = jnp.where(lax.axis_index("c") == 1, 4, 2)
    second = 6 - first
    ag_rel = jnp.stack([0 * first, 0 * first + 1, first, second + 1, second, first + 1, 0 * first + 6, 0 * first + 7])
    ag_blocks = jnp.bitwise_xor(me, ag_rel).astype(jnp.int32)
    proj, xt_bf, w_all, wp_all, cw_all = _proj_all_gather(
        x2, w_in, w_pa, w_pb, w_o, conv_w, b_in.reshape(N_DEV, 1, d_model), ag_blocks)
    wp_full = wp_all.reshape(3, d_model, d_model)
    conv_w_full = jnp.pad(cw_all.transpose(1, 0, 2).reshape(CONV_K, d_model), ((0, HALO - CONV_K), (0, 0)))

    h3, h1 = _branch_a_fwd(proj, conv_w_full, row(conv_b), row(gn_g), row(gn_b), seq)
    s = _branch_b_fwd(proj, row(ln_v_g), row(ln_v_b), w_spatial, b_spatial_t, seq)

    dproj, d_h3, d_s, dr, lhs3, rhs3, vec_mid = _mid(
        h3, s, proj, x2, target2, wp_full, row(b_o), row(ln_out_g), row(ln_out_b))

    dproj, vec_b, d_ws, d_bs_t = _branch_b_bwd(dproj, proj, d_s, row(ln_v_g), row(ln_v_b), w_spatial, b_spatial_t, seq)
    dproj, d_h1, vec_a1 = _branch_a_bwd_norm(dproj, proj, h1, d_h3, row(gn_g), row(gn_b), seq)
    gp_f32, gp_bf = _weight_grads(lhs3, rhs3)
    dproj, vec_a2, d_cw8, g_w_pa, g_w_pb, g_w_o = _branch_a_bwd_conv(
        dproj, proj, d_h1, conv_w_full, gp_bf, gp_f32, seq)

    pieces = [vec_a2, vec_a1, vec_b, vec_mid, d_bs_t.T, _as_rows(d_ws), d_cw8]
    a2, a1, vb, mid = 0, 1, 2, 3
    plan = ([("row", p, r) for p, r in [(a2, 0), (a2, 1), (a1, 0), (vb, 0), (vb, 1), (vb, 2), (mid, 3), (mid, 4)]]
            + [("row", p, r) for p, r in [(a1, 3), (a1, 1), (a1, 2), (vb, 3), (vb, 4), (mid, 2), (mid, 0), (mid, 1)]]
            + [("block", 4, None), ("block", 5, None), ("tiles", 6, None), ("row", mid, 5)])

    rs_rel = jnp.stack([0 * first + 7, 0 * first + 6, first + 1, second, second + 1, first, 0 * first + 1, 0 * first])
    rs_blocks = jnp.bitwise_xor(me, rs_rel).astype(jnp.int32)
    g_w_in, small = _grad_w_in_reduce_scatter(xt_bf, dproj, rs_blocks, pieces, plan)

    g_rows = d_model // LANES
    o0 = N_DEV * g_rows
    o1 = o0 + 8 * g_rows
    o2 = o1 + n_heads
    o3 = o2 + n_heads * LANES
    o4 = o3 + n_heads * HALO
    g_cw_full = small[o3:o4].reshape(n_heads, HALO, LANES).transpose(1, 0, 2).reshape(HALO, d_model)
    g_conv_w = lax.dynamic_slice(g_cw_full, (0, me * dc), (CONV_K, dc))

    two_d = lambda a: a.reshape(-1, a.shape[-1]) if a.ndim != 1 else (
        a.reshape(-1, LANES) if a.shape[0] % LANES == 0 else a.reshape(1, -1))
    names = ["b_in", "conv_w", "conv_b", "gn_g", "gn_b", "ln_v_g", "ln_v_b", "w_spatial", "b_spatial",
             "w_pa", "w_pb", "w_o", "b_o", "ln_out_g", "ln_out_b"]
    ws = dict(b_in=b_in, conv_w=conv_w, conv_b=conv_b, gn_g=gn_g, gn_b=gn_b, ln_v_g=ln_v_g, ln_v_b=ln_v_b,
              w_spatial=w_spatial, b_spatial=b_spatial, w_pa=w_pa, w_pb=w_pb, w_o=w_o, b_o=b_o,
              ln_out_g=ln_out_g, ln_out_b=ln_out_b)
    ms = dict(b_in=m_b_in, conv_w=m_conv_w, conv_b=m_conv_b, gn_g=m_gn_g, gn_b=m_gn_b, ln_v_g=m_ln_v_g,
              ln_v_b=m_ln_v_b, w_spatial=m_w_spatial, b_spatial=m_b_spatial, w_pa=m_w_pa, w_pb=m_w_pb,
              w_o=m_w_o, b_o=m_b_o, ln_out_g=m_ln_out_g, ln_out_b=m_ln_out_b)
    vs = dict(b_in=v_b_in, conv_w=v_conv_w, conv_b=v_conv_b, gn_g=v_gn_g, gn_b=v_gn_b, ln_v_g=v_ln_v_g,
              ln_v_b=v_ln_v_b, w_spatial=v_w_spatial, b_spatial=v_b_spatial, w_pa=v_w_pa, w_pb=v_w_pb,
              w_o=v_w_o, b_o=v_b_o, ln_out_g=v_ln_out_g, ln_out_b=v_ln_out_b)
    vec_names = ["conv_b", "gn_g", "gn_b", "ln_v_g", "ln_v_b", "b_o", "ln_out_g", "ln_out_b"]
    first_row = dict(b_in=0, b_spatial=o1, w_spatial=o2, **{n: o0 + a * g_rows for a, n in enumerate(vec_names)})
    given = dict(conv_w=g_conv_w, w_pa=g_w_pa, w_pb=g_w_pb, w_o=g_w_o)
    grad_x, upd_w_in, upd, loss = _grad_x_adamw(
        dproj, w_all, dr, w_in, g_w_in, m_w_in, v_w_in, small, (o4, g_rows),
        [(two_d(ws[n]), two_d(ms[n]), two_d(vs[n]), row0) for n, row0 in first_row.items()],
        [tuple(two_d(a) for a in (ws[n], grad, ms[n], vs[n])) for n, grad in given.items()])
    grad_x = grad_x.reshape(x.shape)
    res = {n: tuple(a.reshape(ws[n].shape) for a in u) for n, u in zip([*first_row, *given], upd)}
    res["w_in"] = upd_w_in

    order = ["w_in"] + names
    return (loss, grad_x, *[res[n][k] for k in range(4) for n in order])
```

```python
import jax
import jax.numpy as jnp
from jax import lax
from jax.experimental import pallas as pl
from jax.experimental.pallas import tpu as pltpu

F32 = jnp.float32
BF16 = jnp.bfloat16
SDS = jax.ShapeDtypeStruct

N_DEV = 8
LANES = 128
SUBLANES = 8
CONV_K = 31
HALO = 32
ELEMENTWISE_ROWS = 512
XT_CHUNKS = 4
LN_EPS = 1e-5
DEEPNORM_ALPHA = 2.0 ** 0.25
ADAM_LR, ADAM_B1, ADAM_B2, ADAM_EPS, ADAM_WD, ADAM_STEP = 0.001, 0.9, 0.999, 1e-08, 0.01, 10
GELU_C = 0.7978845608028654
GELU_A = 0.044715
VMEM_LIMIT = 56 * 1024 * 1024
MESH = pl.DeviceIdType.MESH
ANY = pl.BlockSpec(memory_space=pl.ANY)
VMEM = pl.BlockSpec(memory_space=pltpu.VMEM)


def _params(n_grid=0):
    sem = ("arbitrary",) * n_grid if n_grid else None
    return pltpu.CompilerParams(dimension_semantics=sem, vmem_limit_bytes=VMEM_LIMIT)


def _tile(n, pref, mult):
    t = min(n, pref)
    while n % t or t % mult:
        t -= 1
    return t


def _colsum(v):
    return jnp.sum(v, axis=0, keepdims=True)


def _sigmoid(v):
    return jax.nn.sigmoid(v)


def _silu_and_grad(v):
    s = _sigmoid(v)
    val = v * s
    return val, s + val * (1.0 - s)


def _gelu_and_grad(v):
    v2 = v * v
    sg = _sigmoid(v * (2.0 * GELU_C + (2.0 * GELU_C * GELU_A) * v2))
    grad = sg + v * sg * (1.0 - sg) * (2.0 * GELU_C + (6.0 * GELU_C * GELU_A) * v2)
    return v * sg, grad


def _tril_mask():
    r = lax.broadcasted_iota(jnp.int32, (LANES, LANES), 0)
    c = lax.broadcasted_iota(jnp.int32, (LANES, LANES), 1)
    return c <= r


def _dot(a, b):
    return jnp.dot(a, b, preferred_element_type=F32)


def _dot_tb(a, b):
    return lax.dot_general(a, b, (((1,), (1,)), ((), ())), preferred_element_type=F32)


def _dot_ta(a, b):
    return lax.dot_general(a, b, (((0,), (0,)), ((), ())), preferred_element_type=F32)


def _mesh_pos():
    return lax.axis_index("x"), lax.axis_index("y"), lax.axis_index("c")


def _block_of(pos):
    return 4 * pos[0] + 2 * pos[1] + pos[2]


def _peers():
    x, y, c = _mesh_pos()
    out = []
    for k in range(1, N_DEV):
        pos = (1 - x if k & 4 else x, 1 - y if k & 2 else y, 1 - c if k & 1 else c)
        out.append((pos, _block_of(pos)))
    return out


def _proj_all_gather(x2, w_in, w_pa, w_pb, w_o, conv_w, b_in3, blk_order):
    n_tok, d_model = x2.shape
    r8 = w_pa.shape[0]
    kc, dc = conv_w.shape
    tm = _tile(n_tok, 1024, LANES)
    n_t = n_tok // tm
    n_arr = 3

    def body(ord_ref, x_ref, b_ref, win_ref, wpa_ref, wpb_ref, wo_ref, cw_ref,
             proj_ref, xt_ref, wall_ref, wp_ref, cwall_ref,
             wbuf, xbuf, st_p, send_sems, recv_sems, local_sems, wall_sems):
        s = pl.program_id(0)
        t = pl.program_id(1)
        x, y, c = _mesh_pos()
        me = (x, y, c)
        sibling = (x, y, 1 - c)
        n1 = (jnp.bitwise_xor(x, c), jnp.bitwise_xor(y, 1 - c))
        n2 = (jnp.bitwise_xor(x, 1 - c), jnp.bitwise_xor(y, c))
        dg = (1 - x, 1 - y)
        outs = [wbuf, wp_ref, cwall_ref]
        srcs = [None, st_p, cw_ref]
        consumed = [me, sibling, (*n1, c), (*n2, 1 - c), (*n2, c), (*n1, 1 - c), (*dg, c), (*dg, 1 - c)]
        leaves = [(me, sibling), (me, (*n1, c)), (me, (*n2, c)), ((*n1, c), (*n2, c)),
                  ((*n1, c), sibling), ((*n2, c), sibling), ((*dg, c), sibling)]
        lands = [sibling, (*n1, c), (*n2, c), (*dg, c), (*n2, 1 - c), (*n1, 1 - c), (*dg, 1 - c)]

        def slot(o, pos):
            return o.at[:, _block_of(pos)] if o is wp_ref else o.at[_block_of(pos)]

        def copy(a, k, block, to, src=None):
            o = outs[a]
            return pltpu.make_async_remote_copy(
                src_ref=slot(o, block) if src is None else src, dst_ref=slot(o, block),
                send_sem=send_sems.at[a, k], recv_sem=recv_sems.at[a, k],
                device_id=to, device_id_type=MESH)

        def send(a, k):
            block, to = leaves[k]
            return copy(a, k, block, to, src=srcs[a] if k < 3 else None)

        def recv(a, k):
            return copy(a, k, lands[k], me)

        def local_copies():
            return [pltpu.make_async_copy(srcs[a], slot(outs[a], me), local_sems.at[a]) for a in (1, 2)]

        def to_hbm(step):
            return pltpu.make_async_copy(slot(wbuf, consumed[step]), slot(wall_ref, consumed[step]),
                                         wall_sems.at[step])

        def at_step(step):
            return pl.when(jnp.logical_and(s == step, t == 0))

        @at_step(0)
        def _():
            slot(wbuf, me)[...] = win_ref[...].astype(BF16)
            st_p[0] = wpa_ref[...].astype(BF16)
            st_p[1] = wpb_ref[...].astype(BF16)
            st_p[2] = wo_ref[...].astype(BF16)
            for a in range(n_arr):
                send(a, 0).start()
                send(a, 1).start()
            for cp in local_copies():
                cp.start()
            to_hbm(0).start()

        @at_step(1)
        def _():
            recv(0, 0).wait_recv()
            to_hbm(1).start()

        for rnd in range(3):
            @at_step(2 + 2 * rnd)
            def _(rnd=rnd):
                if rnd == 0:
                    for a in range(n_arr):
                        send(a, 2).start()
                recv(0, 1 + rnd).wait_recv()
                if rnd == 0:
                    send(0, 3).start()
                send(0, 4 + rnd).start()
                to_hbm(2 + 2 * rnd).start()

            @at_step(3 + 2 * rnd)
            def _(rnd=rnd):
                for a in (1, 2):
                    recv(a, 1 + rnd).wait_recv()
                    if rnd == 0:
                        send(a, 3).start()
                    send(a, 4 + rnd).start()
                recv(0, 4 + rnd).wait_recv()
                to_hbm(3 + 2 * rnd).start()

        rows = pl.ds(pl.multiple_of(t * tm, tm), tm)

        @pl.when(s == 0)
        def _():
            xb = x_ref[...].astype(BF16)
            xbuf[rows, :] = xb
            xt_ref[...] = xb.T

        proj_ref[...] = _dot(xbuf[rows, :], wbuf[ord_ref[s]]) + b_ref[...]

        @pl.when(jnp.logical_and(s == N_DEV - 1, t == n_t - 1))
        def _():
            for a in (1, 2):
                for k in (0, 4, 5, 6):
                    recv(a, k).wait_recv()
            for a in range(n_arr):
                for k in range(7):
                    send(a, k).wait_send()
            for cp in local_copies() + [to_hbm(step) for step in range(N_DEV)]:
                cp.wait()

    grid_spec = pltpu.PrefetchScalarGridSpec(
        num_scalar_prefetch=1, grid=(N_DEV, n_t),
        in_specs=[pl.BlockSpec((tm, d_model), lambda s, t, o: (jnp.where(s == 0, t, n_t - 1), 0)),
                  pl.BlockSpec((None, 1, d_model), lambda s, t, o: (o[s], 0, 0)),
                  VMEM, VMEM, VMEM, VMEM, VMEM],
        out_specs=[pl.BlockSpec((tm, d_model), lambda s, t, o: (t, o[s])),
                   pl.BlockSpec((d_model, tm), lambda s, t, o: (0, jnp.where(s == 0, t, n_t))),
                   ANY, ANY, ANY],
        scratch_shapes=[pltpu.VMEM((N_DEV, d_model, d_model), BF16), pltpu.VMEM((n_tok, d_model), BF16),
                        pltpu.VMEM((3, r8, d_model), BF16),
                        pltpu.SemaphoreType.DMA((n_arr, 7)), pltpu.SemaphoreType.DMA((n_arr, 7)),
                        pltpu.SemaphoreType.DMA((3,)), pltpu.SemaphoreType.DMA((N_DEV,))])
    return pl.pallas_call(
        body, name="proj_all_gather", grid_spec=grid_spec,
        out_shape=[SDS((n_tok, N_DEV * d_model), F32), SDS((d_model, (n_t + 1) * tm), BF16),
                   SDS((N_DEV, d_model, d_model), BF16), SDS((3, N_DEV, r8, d_model), BF16),
                   SDS((N_DEV, kc, dc), F32)],
        compiler_params=_params(2),
    )(blk_order, x2, b_in3, w_in, w_pa, w_pb, w_o, conv_w)


def _conv_rows(ta):
    return _tile(ta, 64, SUBLANES)


def _branch_a_fwd(proj, conv_w_full, conv_b, gn_g, gn_b, seq):
    n_tok = proj.shape[0]
    d_model = conv_b.shape[1]
    ta = _tile(seq, ELEMENTWISE_ROWS, HALO)
    per_seq = seq // ta
    rc = _conv_rows(ta)

    def body(av_ref, ag_ref, gt_ref, avh_ref, agh_ref, cw_ref, cb_ref, gg_ref, gb_ref,
             h3_ref, h1_ref, ext):
        keep = jnp.where(pl.program_id(0) % per_seq == 0, 0.0, 1.0)

        def group(g, carry):
            sl = pl.ds(pl.multiple_of(g * LANES, LANES), LANES)
            ext[0:HALO, :] = avh_ref[:, sl] * _sigmoid(agh_ref[:, sl]) * keep
            ext[HALO:HALO + ta, :] = av_ref[:, sl] * _sigmoid(ag_ref[:, sl])
            for r0 in range(0, ta, rc):
                acc = jnp.broadcast_to(cb_ref[:, sl], (rc, LANES))
                for k in range(CONV_K):
                    acc = acc + ext[pl.ds(r0 + HALO - (CONV_K - 1) + k, rc), :] * cw_ref[k:k + 1, sl]
                h1_ref[pl.ds(r0, rc), sl] = acc
            h1 = h1_ref[:, sl]
            mu = jnp.mean(h1, axis=-1, keepdims=True)
            dlt = h1 - mu
            var = jnp.mean(dlt * dlt, axis=-1, keepdims=True)
            h2 = dlt * lax.rsqrt(var + LN_EPS) * gg_ref[:, sl] + gb_ref[:, sl]
            gate = gt_ref[:, sl]
            h3_ref[:, sl] = (h2 * _sigmoid(h2) * gate * _sigmoid(gate)).astype(BF16)
            return carry

        lax.fori_loop(0, d_model // LANES, group, 0, unroll=True)

    blk = lambda j: pl.BlockSpec((ta, d_model), lambda i: (i, j))
    halo = lambda j: pl.BlockSpec((HALO, d_model), lambda i: (jnp.maximum(i * (ta // HALO) - 1, 0), j))
    row = pl.BlockSpec((1, d_model), lambda i: (0, 0))
    return pl.pallas_call(
        body, name="branch_a_fwd", grid=(n_tok // ta,),
        in_specs=[blk(0), blk(1), blk(2), halo(0), halo(1),
                  pl.BlockSpec((HALO, d_model), lambda i: (0, 0)), row, row, row],
        out_specs=[pl.BlockSpec((ta, d_model), lambda i: (i, 0))] * 2,
        out_shape=[SDS((n_tok, d_model), BF16), SDS((n_tok, d_model), F32)],
        scratch_shapes=[pltpu.VMEM((HALO + ta, LANES), F32)],
        compiler_params=_params(1),
    )(proj, proj, proj, proj, proj, conv_w_full, conv_b, gn_g, gn_b)


def _branch_b_fwd(proj, ln_g, ln_b, w_spatial, b_spatial_t, seq):
    n_tok = proj.shape[0]
    d_model = ln_g.shape[1]
    n_heads = d_model // LANES
    tb = _tile(seq, ELEMENTWISE_ROWS, LANES)

    def body(u_ref, v_ref, bg_ref, lg_ref, lb_ref, ws_ref, bs_ref, s_ref, vn_buf):
        v, _ = _gelu_and_grad(v_ref[...])
        mu = jnp.mean(v, axis=-1, keepdims=True)
        dlt = v - mu
        var = jnp.mean(dlt * dlt, axis=-1, keepdims=True)
        vn_buf[...] = (dlt * lax.rsqrt(var + LN_EPS) * lg_ref[...] + lb_ref[...]).astype(BF16)
        tril = _tril_mask()
        for h in range(n_heads):
            cols = slice(h * LANES, (h + 1) * LANES)
            w_h = jnp.where(tril, ws_ref[h], 0.0).astype(BF16)
            bias = bs_ref[:, h:h + 1]
            for ch in range(tb // LANES):
                rows = slice(ch * LANES, (ch + 1) * LANES)
                mix = _dot(w_h, vn_buf[rows, cols]) + bias
                u, _ = _gelu_and_grad(u_ref[rows, cols])
                gate = bg_ref[rows, cols]
                s_ref[rows, cols] = (u * mix * gate * _sigmoid(gate)).astype(BF16)

    blk = lambda j: pl.BlockSpec((tb, d_model), lambda i: (i, j))
    row = pl.BlockSpec((1, d_model), lambda i: (0, 0))
    return pl.pallas_call(
        body, name="branch_b_fwd", grid=(n_tok // tb,),
        in_specs=[blk(3), blk(4), blk(5), row, row,
                  pl.BlockSpec((n_heads, LANES, LANES), lambda i: (0, 0, 0)),
                  pl.BlockSpec((LANES, n_heads), lambda i: (0, 0))],
        out_specs=pl.BlockSpec((tb, d_model), lambda i: (i, 0)),
        out_shape=SDS((n_tok, d_model), BF16),
        scratch_shapes=[pltpu.VMEM((tb, d_model), BF16)],
        compiler_params=_params(1),
    )(proj, proj, proj, ln_g, ln_b, w_spatial, b_spatial_t)


MID_ROWS = 8


def _mid(h3, s, proj, x2, target, wp_full, b_o, lo_g, lo_b):
    n_tok, d_model = x2.shape
    tm = _tile(n_tok, 256, 16)

    def body(h3_ref, s_ref, ma_ref, mb_ref, x_ref, t_ref, wp_hbm, bo_ref,
             lg_ref, lb_ref, dproj_ref, dh3_ref, ds_ref, dr_ref, lhs3_ref, rhs3_ref, vec_ref, w_buf, w_sems):
        first = pl.program_id(0) == 0

        def fetch(k):
            return pltpu.make_async_copy(wp_hbm.at[k], w_buf.at[k], w_sems.at[k])

        def weight(k):
            @pl.when(first)
            def _():
                fetch(k).wait()
            return w_buf[k]

        @pl.when(first)
        def _():
            for k in range(3):
                fetch(k).start()
            vec_ref[...] = jnp.zeros_like(vec_ref)

        h3 = h3_ref[...]
        s = s_ref[...]
        ya = _dot(h3, weight(0))
        yb = _dot(s, weight(1))
        ga = _sigmoid(ma_ref[...])
        gb = _sigmoid(mb_ref[...])
        mixed = (ga * ya + gb * yb).astype(BF16)
        lhs3_ref[0] = h3
        lhs3_ref[1] = s
        lhs3_ref[2] = mixed
        r = DEEPNORM_ALPHA * x_ref[...] + _dot(mixed, weight(2)) + bo_ref[...]
        mu = jnp.mean(r, axis=-1, keepdims=True)
        dlt = r - mu
        rstd = lax.rsqrt(jnp.mean(dlt * dlt, axis=-1, keepdims=True) + LN_EPS)
        rhat = dlt * rstd
        diff = rhat * lg_ref[...] + lb_ref[...] - t_ref[...]
        dy = diff * (1.0 / d_model)
        vec_ref[0:1, :] += _colsum(dy * rhat)
        vec_ref[1:2, :] += _colsum(dy)
        vec_ref[5:6, :] += _colsum(diff * diff)
        drh = dy * lg_ref[...]
        dr = rstd * (drh - jnp.mean(drh, axis=-1, keepdims=True)
                     - rhat * jnp.mean(drh * rhat, axis=-1, keepdims=True))
        vec_ref[2:3, :] += _colsum(dr)
        dr_ref[...] = dr
        drb = dr.astype(BF16)
        rhs3_ref[2] = drb
        dmixed = _dot_tb(drb, w_buf[2])
        dya_f = dmixed * ga
        dyb_f = dmixed * gb
        dma = dya_f * ya * (1.0 - ga)
        dmb = dyb_f * yb * (1.0 - gb)
        vec_ref[3:4, :] += _colsum(dma)
        vec_ref[4:5, :] += _colsum(dmb)
        dproj_ref[:, 0:d_model] = dma.astype(BF16)
        dproj_ref[:, d_model:2 * d_model] = dmb.astype(BF16)
        dya = dya_f.astype(BF16)
        dyb = dyb_f.astype(BF16)
        rhs3_ref[0] = dya
        rhs3_ref[1] = dyb
        dh3_ref[...] = _dot_tb(dya, w_buf[0])
        ds_ref[...] = _dot_tb(dyb, w_buf[1])

    tile = pl.BlockSpec((tm, d_model), lambda i: (i, 0))
    row = pl.BlockSpec((1, d_model), lambda i: (0, 0))
    stack = pl.BlockSpec((3, tm, d_model), lambda i: (0, i, 0))
    bf3 = SDS((3, n_tok, d_model), BF16)
    f32 = SDS((n_tok, d_model), F32)
    return pl.pallas_call(
        body, name="mid", grid=(n_tok // tm,),
        in_specs=[tile, tile, pl.BlockSpec((tm, d_model), lambda i: (i, 6)),
                  pl.BlockSpec((tm, d_model), lambda i: (i, 7)), tile, tile, ANY, row, row, row],
        out_specs=[pl.BlockSpec((tm, 2 * d_model), lambda i: (i, 3)), tile, tile, tile, stack, stack,
                   pl.BlockSpec((MID_ROWS, d_model), lambda i: (0, 0))],
        out_shape=[SDS((n_tok, N_DEV * d_model), BF16), f32, f32, f32, bf3, bf3,
                   SDS((MID_ROWS, d_model), F32)],
        scratch_shapes=[pltpu.VMEM(wp_full.shape, BF16), pltpu.SemaphoreType.DMA((3,))],
        compiler_params=_params(1),
    )(h3, s, proj, proj, x2, target, wp_full, b_o, lo_g, lo_b)


B_ROWS = 8


def _branch_b_bwd(dproj, proj, d_s, ln_g, ln_b, w_spatial, b_spatial_t, seq):
    n_tok = proj.shape[0]
    d_model = ln_g.shape[1]
    n_heads = d_model // LANES
    tb = _tile(seq, ELEMENTWISE_ROWS, LANES)

    def body(dproj_in, u_ref, v_ref, bg_ref, ds_ref, lg_ref, lb_ref, ws_ref, bs_ref,
             dproj_ref, vec_ref, dws_ref, dbs_ref, vn_buf, dv_buf):
        del dproj_in

        @pl.when(pl.program_id(0) == 0)
        def _():
            vec_ref[...] = jnp.zeros_like(vec_ref)
            dws_ref[...] = jnp.zeros_like(dws_ref)
            dbs_ref[...] = jnp.zeros_like(dbs_ref)

        v, dgelu_v = _gelu_and_grad(v_ref[...])
        mu = jnp.mean(v, axis=-1, keepdims=True)
        dlt = v - mu
        rstd = lax.rsqrt(jnp.mean(dlt * dlt, axis=-1, keepdims=True) + LN_EPS)
        vhat = dlt * rstd
        vn_buf[...] = (vhat * lg_ref[...] + lb_ref[...]).astype(BF16)
        tril = _tril_mask()
        for h in range(n_heads):
            cols = slice(h * LANES, (h + 1) * LANES)
            w_h = jnp.where(tril, ws_ref[h], 0.0).astype(BF16)
            bias = bs_ref[:, h:h + 1]
            for ch in range(tb // LANES):
                rows = slice(ch * LANES, (ch + 1) * LANES)
                vn = vn_buf[rows, cols]
                mix = _dot(w_h, vn) + bias
                u, dgelu_u = _gelu_and_grad(u_ref[rows, cols])
                sg, dsilu = _silu_and_grad(bg_ref[rows, cols])
                dsv = ds_ref[rows, cols]
                du = dsv * mix * sg * dgelu_u
                dbg = dsv * u * mix * dsilu
                dmix = dsv * u * sg
                dmix_bf = dmix.astype(BF16)
                dproj_ref[rows, cols] = du.astype(BF16)
                dproj_ref[rows, 2 * d_model + h * LANES:2 * d_model + (h + 1) * LANES] = dbg.astype(BF16)
                vec_ref[0:1, cols] += _colsum(du)
                vec_ref[2:3, cols] += _colsum(dbg)
                dbs_ref[:, h:h + 1] += jnp.sum(dmix, axis=1, keepdims=True)
                dws_ref[h] += jnp.where(tril, _dot_tb(dmix_bf, vn), 0.0)
                dv_buf[rows, cols] = _dot_ta(w_h, dmix_bf)
        dvn = dv_buf[...]
        vec_ref[3:4, :] += _colsum(dvn * vhat)
        vec_ref[4:5, :] += _colsum(dvn)
        dvh = dvn * lg_ref[...]
        dv = rstd * (dvh - jnp.mean(dvh, axis=-1, keepdims=True)
                     - vhat * jnp.mean(dvh * vhat, axis=-1, keepdims=True)) * dgelu_v
        vec_ref[1:2, :] += _colsum(dv)
        dproj_ref[:, d_model:2 * d_model] = dv.astype(BF16)

    blk = lambda j: pl.BlockSpec((tb, d_model), lambda i: (i, j))
    row = pl.BlockSpec((1, d_model), lambda i: (0, 0))
    return pl.pallas_call(
        body, name="branch_b_bwd", grid=(n_tok // tb,),
        in_specs=[ANY, blk(3), blk(4), blk(5), pl.BlockSpec((tb, d_model), lambda i: (i, 0)), row, row,
                  pl.BlockSpec((n_heads, LANES, LANES), lambda i: (0, 0, 0)),
                  pl.BlockSpec((LANES, n_heads), lambda i: (0, 0))],
        out_specs=[pl.BlockSpec((tb, 3 * d_model), lambda i: (i, 1)),
                   pl.BlockSpec((B_ROWS, d_model), lambda i: (0, 0)),
                   pl.BlockSpec((n_heads, LANES, LANES), lambda i: (0, 0, 0)),
                   pl.BlockSpec((LANES, n_heads), lambda i: (0, 0))],
        out_shape=[SDS(dproj.shape, BF16), SDS((B_ROWS, d_model), F32),
                   SDS((n_heads, LANES, LANES), F32), SDS((LANES, n_heads), F32)],
        scratch_shapes=[pltpu.VMEM((tb, d_model), BF16), pltpu.VMEM((tb, d_model), F32)],
        input_output_aliases={0: 0},
        compiler_params=_params(1),
    )(dproj, proj, proj, proj, d_s, ln_g, ln_b, w_spatial, b_spatial_t)


A1_ROWS = 8


def _branch_a_bwd_norm(dproj, proj, h1, d_h3, gn_g, gn_b, seq):
    n_tok = proj.shape[0]
    d_model = gn_g.shape[1]
    ta = _tile(seq, ELEMENTWISE_ROWS, 16)

    def body(dproj_in, gt_ref, h1_ref, dh3_ref, gg_ref, gb_ref, dproj_ref, dh1_ref, vec_ref):
        del dproj_in

        @pl.when(pl.program_id(0) == 0)
        def _():
            vec_ref[...] = jnp.zeros_like(vec_ref)

        def group(g, carry):
            sl = pl.ds(pl.multiple_of(g * LANES, LANES), LANES)
            h1 = h1_ref[:, sl]
            mu = jnp.mean(h1, axis=-1, keepdims=True)
            dlt = h1 - mu
            rstd = lax.rsqrt(jnp.mean(dlt * dlt, axis=-1, keepdims=True) + LN_EPS)
            nrm = dlt * rstd
            sw, dsw = _silu_and_grad(nrm * gg_ref[:, sl] + gb_ref[:, sl])
            sg, dsg = _silu_and_grad(gt_ref[:, sl])
            dh3 = dh3_ref[:, sl]
            dgate = dh3 * sw * dsg
            dproj_ref[:, sl] = dgate.astype(BF16)
            vec_ref[0:1, sl] += _colsum(dgate)
            dh2 = dh3 * sg * dsw
            vec_ref[1:2, sl] += _colsum(dh2 * nrm)
            vec_ref[2:3, sl] += _colsum(dh2)
            dn = dh2 * gg_ref[:, sl]
            dh1 = rstd * (dn - jnp.mean(dn, axis=-1, keepdims=True)
                          - nrm * jnp.mean(dn * nrm, axis=-1, keepdims=True))
            vec_ref[3:4, sl] += _colsum(dh1)
            dh1_ref[:, sl] = dh1
            return carry

        lax.fori_loop(0, d_model // LANES, group, 0, unroll=True)

    tile = pl.BlockSpec((ta, d_model), lambda i: (i, 0))
    row = pl.BlockSpec((1, d_model), lambda i: (0, 0))
    return pl.pallas_call(
        body, name="branch_a_bwd_norm", grid=(n_tok // ta,),
        in_specs=[ANY, pl.BlockSpec((ta, d_model), lambda i: (i, 2)), tile, tile, row, row],
        out_specs=[pl.BlockSpec((ta, d_model), lambda i: (i, 2)), tile,
                   pl.BlockSpec((A1_ROWS, d_model), lambda i: (0, 0))],
        out_shape=[SDS(dproj.shape, BF16), SDS((n_tok, d_model), F32), SDS((A1_ROWS, d_model), F32)],
        input_output_aliases={0: 0},
        compiler_params=_params(1),
    )(dproj, proj, h1, d_h3, gn_g, gn_b)


A2_ROWS = 8


def _branch_a_bwd_conv(dproj, proj, d_h1, conv_w_full, gp_bf, gp_f32, seq):
    n_tok = proj.shape[0]
    d_model = conv_w_full.shape[1]
    n_groups = d_model // LANES
    ta = _tile(seq, ELEMENTWISE_ROWS, HALO)
    n_tiles = n_tok // ta
    per_seq = seq // ta
    rc = _conv_rows(ta)
    last_halo = n_tok // HALO - 1
    r8 = d_model // N_DEV
    prow = _tile(r8, 32, 16)

    def body(dproj_in, av_ref, ag_ref, avh_ref, agh_ref, dh1_ref, dh1h_ref, cw_ref, gp_bf_ref, gp_f32_ref,
             dproj_ref, vec_ref, dcw_ref, opa_ref, opb_ref, opo_ref,
             ext_h0, ext_d, rbuf, own, send_sems, recv_sems, local_sems):
        del dproj_in
        i = pl.program_id(0)
        x, y, c = _mesh_pos()
        me = _block_of((x, y, c))
        peers = _peers()

        def sends():
            return [pltpu.make_async_remote_copy(
                src_ref=gp_bf_ref.at[a, pl.ds(pl.multiple_of(blk * r8, 16), r8), :], dst_ref=rbuf.at[k, a],
                send_sem=send_sems.at[a, k], recv_sem=recv_sems.at[a, k], device_id=pos, device_id_type=MESH)
                for k, (pos, blk) in enumerate(peers) for a in range(3)]

        def own_rows():
            return [pltpu.make_async_copy(gp_f32_ref.at[a, pl.ds(pl.multiple_of(me * r8, 8), r8), :],
                                          own.at[a], local_sems.at[a]) for a in range(3)]

        @pl.when(i == 0)
        def _():
            vec_ref[...] = jnp.zeros_like(vec_ref)
            dcw_ref[...] = jnp.zeros_like(dcw_ref)
            for cp in sends() + own_rows():
                cp.start()

        keep_past = jnp.where(i % per_seq == 0, 0.0, 1.0)
        keep_next = jnp.where(i % per_seq == per_seq - 1, 0.0, 1.0)

        def group(g, carry):
            sl = pl.ds(pl.multiple_of(g * LANES, LANES), LANES)
            av = av_ref[:, sl]
            sig = _sigmoid(ag_ref[:, sl])
            ext_h0[0:HALO, :] = avh_ref[:, sl] * _sigmoid(agh_ref[:, sl]) * keep_past
            ext_h0[HALO:HALO + ta, :] = av * sig
            ext_d[0:ta, :] = dh1_ref[:, sl]
            ext_d[ta:ta + HALO, :] = dh1h_ref[:, sl] * keep_next
            for r0 in range(0, ta, rc):
                dh1 = ext_d[pl.ds(r0, rc), :]
                acc = jnp.zeros((rc, LANES), F32)
                for k in range(CONV_K):
                    acc = acc + ext_d[pl.ds(r0 + CONV_K - 1 - k, rc), :] * cw_ref[k:k + 1, sl]
                    prod = dh1 * ext_h0[pl.ds(r0 + HALO - (CONV_K - 1) + k, rc), :]
                    dcw_ref[g, k] += jnp.sum(prod.reshape(rc // SUBLANES, SUBLANES, LANES), axis=0)
                rows = pl.ds(r0, rc)
                sig_r = sig[r0:r0 + rc]
                dav = acc * sig_r
                dag = dav * av[r0:r0 + rc] * (1.0 - sig_r)
                dproj_ref[rows, sl] = dav.astype(BF16)
                dproj_ref[rows, pl.ds(pl.multiple_of(d_model + g * LANES, LANES), LANES)] = dag.astype(BF16)
                vec_ref[0:1, sl] += _colsum(dav)
                vec_ref[1:2, sl] += _colsum(dag)
            return carry

        lax.fori_loop(0, n_groups, group, 0, unroll=True)

        @pl.when(i == n_tiles - 1)
        def _():
            for cp in own_rows():
                cp.wait()
            for cp in sends():
                cp.wait_recv()
            for a, o in enumerate([opa_ref, opb_ref, opo_ref]):
                for q in range(r8 // prow):
                    r = pl.ds(q * prow, prow)
                    tot = own[a, r, :]
                    for k in range(N_DEV - 1):
                        tot = tot + rbuf[k, a, r, :].astype(F32)
                    o[r, :] = tot
            for cp in sends():
                cp.wait_send()

    blk = lambda j: pl.BlockSpec((ta, d_model), lambda i: (i, j))
    halo = lambda j: pl.BlockSpec((HALO, d_model), lambda i: (jnp.maximum(i * (ta // HALO) - 1, 0), j))
    shard = pl.BlockSpec((r8, d_model), lambda i: (0, 0))
    return pl.pallas_call(
        body, name="branch_a_bwd_conv", grid=(n_tiles,),
        in_specs=[ANY, blk(0), blk(1), halo(0), halo(1), pl.BlockSpec((ta, d_model), lambda i: (i, 0)),
                  pl.BlockSpec((HALO, d_model), lambda i: (jnp.minimum((i + 1) * (ta // HALO), last_halo), 0)),
                  pl.BlockSpec((HALO, d_model), lambda i: (0, 0)), ANY, ANY],
        out_specs=[pl.BlockSpec((ta, 2 * d_model), lambda i: (i, 0)),
                   pl.BlockSpec((A2_ROWS, d_model), lambda i: (0, 0)),
                   pl.BlockSpec((n_groups, HALO, SUBLANES, LANES), lambda i: (0, 0, 0, 0)),
                   shard, shard, shard],
        out_shape=[SDS(dproj.shape, BF16), SDS((A2_ROWS, d_model), F32),
                   SDS((n_groups, HALO, SUBLANES, LANES), F32)] + [SDS((r8, d_model), F32)] * 3,
        scratch_shapes=[pltpu.VMEM((HALO + ta, LANES), F32), pltpu.VMEM((ta + HALO, LANES), F32),
                        pltpu.VMEM((N_DEV - 1, 3, r8, d_model), BF16), pltpu.VMEM((3, r8, d_model), F32),
                        pltpu.SemaphoreType.DMA((3, 7)), pltpu.SemaphoreType.DMA((3, 7)),
                        pltpu.SemaphoreType.DMA((3,))],
        input_output_aliases={0: 0},
        compiler_params=_params(1),
    )(dproj, proj, proj, proj, proj, d_h1, d_h1, conv_w_full, gp_bf, gp_f32)


def _weight_grads(lhs3, rhs3):
    n_mat, n_tok, d_model = lhs3.shape
    tk = _tile(n_tok, 2048, 16)
    n_k = n_tok // tk

    def body(a_ref, g_ref, o_ref, ob_ref):
        part = _dot_ta(a_ref[...], g_ref[...])

        @pl.when(pl.program_id(1) == 0)
        def _():
            o_ref[...] = part

        @pl.when(pl.program_id(1) != 0)
        def _():
            o_ref[...] += part

        @pl.when(pl.program_id(1) == n_k - 1)
        def _():
            ob_ref[...] = o_ref[...].astype(BF16)

    tile = pl.BlockSpec((None, tk, d_model), lambda a, i: (a, i, 0))
    out = pl.BlockSpec((None, d_model, d_model), lambda a, i: (a, 0, 0))
    return pl.pallas_call(
        body, name="grad_w_pa_pb_o", grid=(n_mat, n_k), in_specs=[tile, tile], out_specs=[out, out],
        out_shape=[SDS((n_mat, d_model, d_model), F32), SDS((n_mat, d_model, d_model), BF16)],
        compiler_params=_params(2),
    )(lhs3, rhs3)


def _packed_rows(pieces, plan, d_model):
    rows = {"row": lambda a: d_model // LANES, "block": lambda a: a.shape[0],
            "tiles": lambda a: a.shape[0] * a.shape[1]}
    return sum(rows[kind](pieces[p]) for kind, p, _ in plan)


def _grad_w_in_reduce_scatter(xt_bf, dproj, blk_order, pieces, plan):
    n_tok = dproj.shape[0]
    d_model = xt_bf.shape[0]
    dh = d_model // 2
    n_units = 2 * N_DEV
    n_pc = len(pieces)
    n_packed = _packed_rows(pieces, plan, d_model)
    n_part = n_packed + -n_packed % (N_DEV * SUBLANES)
    rsl = n_part // N_DEV

    xt_cols = n_tok // XT_CHUNKS

    def body(ord_ref, a_hbm, g_ref, *rest):
        pc_refs, rest = rest[:n_pc], rest[n_pc:]
        (o_ref, small_ref, acc, fb, sbuf, gbuf, tbuf, rfin, rbuf_s, red, p_ref, a_ref,
         send_f, send_s, recv_g, recv_t, recv_f, out_sems, send1, recv1, send2, recv2, xt_sems) = rest
        del ord_ref
        u = pl.program_id(0)
        s = u // 2
        hf = u % 2
        rnd = s // 2
        x, y, c = _mesh_pos()
        sibling = (x, y, 1 - c)
        me = _block_of((x, y, c))
        peers = _peers()

        def fetch_xt(q):
            cols = pl.ds(q * xt_cols, xt_cols)
            return pltpu.make_async_copy(a_hbm.at[:, cols], a_ref.at[:, cols], xt_sems.at[q])

        def rows_of(blk):
            return pl.ds(pl.multiple_of(blk * rsl, SUBLANES), rsl)

        def scatter():
            return [pltpu.make_async_remote_copy(
                src_ref=p_ref.at[rows_of(blk), :], dst_ref=rbuf_s.at[k], send_sem=send1.at[k],
                recv_sem=recv1.at[k], device_id=pos, device_id_type=MESH) for k, (pos, blk) in enumerate(peers)]

        def gather(dst_block=None):
            return [pltpu.make_async_remote_copy(
                src_ref=red, dst_ref=small_ref.at[rows_of(me if dst_block is None else blk), :],
                send_sem=send2.at[k], recv_sem=recv2.at[k], device_id=pos, device_id_type=MESH)
                for k, (pos, blk) in enumerate(peers)]

        def own_slice():
            return pltpu.make_async_copy(red, small_ref.at[rows_of(me), :], out_sems.at[2])

        def pack():
            at = 0
            for kind, p, r in plan:
                ref = pc_refs[p]
                if kind == "row":
                    for q in range(d_model // LANES):
                        p_ref[at + q:at + q + 1, :] = ref[r:r + 1, q * LANES:(q + 1) * LANES]
                    at += d_model // LANES
                elif kind == "block":
                    p_ref[at:at + ref.shape[0], :] = ref[...]
                    at += ref.shape[0]
                else:
                    for tile in range(ref.shape[0] * ref.shape[1]):
                        p_ref[at + tile:at + tile + 1, :] = _colsum(ref[tile // ref.shape[1], tile % ref.shape[1]])
                    at += ref.shape[0] * ref.shape[1]
            if at < n_part:
                p_ref[at:n_part, :] = jnp.zeros((n_part - at, LANES), F32)

        @pl.when(u == 0)
        def _():
            for q in range(XT_CHUNKS):
                fetch_xt(q).start()
            pack()
            for cp in scatter():
                cp.start()

        @pl.when(u == 8)
        def _():
            for cp in scatter():
                cp.wait_recv()
            tot = p_ref[rows_of(me), :]
            for k in range(N_DEV - 1):
                tot = tot + rbuf_s[k]
            red[...] = tot
            own_slice().start()
            for cp in gather():
                cp.start()

        n1 = (jnp.bitwise_xor(x, c), jnp.bitwise_xor(y, 1 - c), c)
        n2 = (jnp.bitwise_xor(x, 1 - c), jnp.bitwise_xor(y, c), c)

        def feed(r, half):
            return pltpu.make_async_remote_copy(
                src_ref=fb.at[half], dst_ref=gbuf.at[r, half], send_sem=send_f.at[r, half],
                recv_sem=recv_g.at[r, half], device_id=sibling, device_id_type=MESH)

        def feed_sibling(half):
            return pltpu.make_async_remote_copy(
                src_ref=fb.at[half], dst_ref=rfin.at[0, half], send_sem=send_f.at[3, half],
                recv_sem=recv_f.at[0, half], device_id=sibling, device_id_type=MESH)

        def chip_sum(r, half):
            dst = [tbuf.at[half], rfin.at[2, half], rfin.at[1, half]][r]
            sem = [recv_t.at[half], recv_f.at[2, half], recv_f.at[1, half]][r]
            return pltpu.make_async_remote_copy(
                src_ref=sbuf.at[r, half], dst_ref=dst, send_sem=send_s.at[r, half], recv_sem=sem,
                device_id=[n2, n2, n1][r], device_id_type=MESH)

        def out_copy(half):
            return pltpu.make_async_copy(acc.at[half], o_ref.at[:, pl.ds(half * dh, dh)], out_sems.at[half])

        def partial_sum(first=False):
            if not first:
                return _dot(a_ref[...], g_ref[...])
            tot = None
            for q in range(XT_CHUNKS):
                fetch_xt(q).wait()
                part = _dot(a_ref[:, q * xt_cols:(q + 1) * xt_cols], g_ref[q * xt_cols:(q + 1) * xt_cols, :])
                tot = part if tot is None else tot + part
            return tot

        for half in range(2):
            for r in range(3):
                @pl.when(u == 4 * r + 4 + half)
                def _(r=r, half=half):
                    feed(r, half).wait_send()

                @pl.when(u == 4 * r + 2 + half)
                def _(r=r, half=half):
                    feed(r, half).wait_recv()
                    if r == 2:
                        chip_sum(0, half).wait_recv()

            @pl.when(u == 14 + half)
            def _(half=half):
                feed_sibling(half).wait_recv()
                chip_sum(2, half).wait_recv()
                chip_sum(1, half).wait_recv()

        @pl.when(u == 0)
        def _():
            fb[0] = partial_sum(first=True).astype(BF16)

        @pl.when(jnp.logical_and(jnp.logical_and(s % 2 == 0, s < 7), u > 0))
        def _():
            fb[hf] = partial_sum().astype(BF16)

        @pl.when(jnp.logical_or(s == 1, s == 3))
        def _():
            sbuf[rnd, hf] = (partial_sum() + gbuf[rnd, hf].astype(F32)).astype(BF16)

        @pl.when(s == 5)
        def _():
            sbuf[2, hf] = (partial_sum() + gbuf[2, hf].astype(F32) + tbuf[hf].astype(F32)).astype(BF16)

        @pl.when(s == 7)
        def _():
            acc[hf] = (partial_sum() + rfin[0, hf].astype(F32) + rfin[1, hf].astype(F32)
                       + rfin[2, hf].astype(F32))

        for half in range(2):
            for r in range(3):
                @pl.when(u == 4 * r + half)
                def _(r=r, half=half):
                    feed(r, half).start()

                @pl.when(u == 4 * r + 2 + half)
                def _(r=r, half=half):
                    chip_sum(r, half).start()

            @pl.when(u == 12 + half)
            def _(half=half):
                feed_sibling(half).start()

        @pl.when(u == 14)
        def _():
            out_copy(0).start()

        @pl.when(u == 15)
        def _():
            out_copy(1).start()
            for half in range(2):
                feed_sibling(half).wait_send()
                for r in range(3):
                    chip_sum(r, half).wait_send()
                out_copy(half).wait()
            for cp in gather("theirs"):
                cp.wait_recv()
            for cp in scatter() + gather():
                cp.wait_send()
            own_slice().wait()

    grid_spec = pltpu.PrefetchScalarGridSpec(
        num_scalar_prefetch=1, grid=(n_units,),
        in_specs=[ANY, pl.BlockSpec((n_tok, dh), lambda u, o: (0, 2 * o[u // 2] + u % 2))] + [VMEM] * n_pc,
        out_specs=[ANY, ANY],
        scratch_shapes=[pltpu.VMEM((2, d_model, dh), F32), pltpu.VMEM((2, d_model, dh), BF16),
                        pltpu.VMEM((3, 2, d_model, dh), BF16), pltpu.VMEM((3, 2, d_model, dh), BF16),
                        pltpu.VMEM((2, d_model, dh), BF16), pltpu.VMEM((3, 2, d_model, dh), BF16),
                        pltpu.VMEM((N_DEV - 1, rsl, LANES), F32), pltpu.VMEM((rsl, LANES), F32),
                        pltpu.VMEM((n_part, LANES), F32), pltpu.VMEM((d_model, n_tok), BF16),
                        pltpu.SemaphoreType.DMA((4, 2)), pltpu.SemaphoreType.DMA((3, 2)),
                        pltpu.SemaphoreType.DMA((3, 2)), pltpu.SemaphoreType.DMA((2,)),
                        pltpu.SemaphoreType.DMA((3, 2)),
                        pltpu.SemaphoreType.DMA((3,))] + [pltpu.SemaphoreType.DMA((N_DEV - 1,))] * 4
        + [pltpu.SemaphoreType.DMA((XT_CHUNKS,))])
    return pl.pallas_call(
        body, name="grad_w_in_reduce_scatter", grid_spec=grid_spec,
        out_shape=[SDS((d_model, d_model), F32), SDS((n_part, LANES), F32)], compiler_params=_params(1),
    )(blk_order, xt_bf, dproj, *pieces)


def _grad_x_adamw(dproj, w_all, dr, w, g, m, v, small, loss_rows, packed, groups):
    n_tok, d_model = dr.shape
    tm = _tile(n_tok, 256, 16)
    n_steps = n_tok // tm
    tr = w.shape[0] // n_steps
    n_pk, n_grp = len(packed), len(groups)
    n_out = 4 * (n_pk + n_grp)

    def body(dp_ref, w_hbm, dr_ref, ws_ref, g_ref, m_ref, v_ref, small_ref, *rest):
        pk_in, rest = rest[:3 * n_pk], rest[3 * n_pk:]
        grp_in, rest = rest[:4 * n_grp], rest[4 * n_grp:]
        o_ref, gp_ref, d_ref, mo_ref, vo_ref, loss_ref = rest[:6]
        outs, (w_ref, w_sems) = rest[6:6 + n_out], rest[6 + n_out:]
        i = pl.program_id(0)

        def fetch(j):
            return pltpu.make_async_copy(w_hbm.at[j], w_ref.at[j], w_sems.at[j])

        def grad_x_tile(first):
            acc = DEEPNORM_ALPHA * dr_ref[...]
            for j in range(N_DEV):
                if first:
                    fetch(j).wait()
                acc = acc + _dot_tb(dp_ref[:, j * d_model:(j + 1) * d_model], w_ref[j])
            o_ref[...] = acc

        def small_updates():
            def update(p, pw, grad, pm, pv):
                og, od, om, ov = outs[4 * p:4 * p + 4]
                og[...] = grad
                od[...], om[...], ov[...] = _adamw_math(pw[...], grad, pm[...], pv[...])

            for p, (pw_arr, _, _, row0) in enumerate(packed):
                pw, pm, pv = pk_in[3 * p:3 * p + 3]
                update(p, pw, small_ref[row0:row0 + pw_arr.shape[0], :], pm, pv)
            for p in range(n_grp):
                pw, pg, pm, pv = grp_in[4 * p:4 * p + 4]
                update(n_pk + p, pw, pg[...], pm, pv)
            sq = small_ref[loss_rows[0]:loss_rows[0] + loss_rows[1], :]
            loss_ref[...] = jnp.sum(_colsum(sq), axis=1, keepdims=True) * (0.5 / d_model)

        @pl.when(i == 0)
        def _():
            for j in range(N_DEV):
                fetch(j).start()
            small_updates()
            grad_x_tile(True)

        @pl.when(i != 0)
        def _():
            grad_x_tile(False)

        grad = g_ref[...]
        gp_ref[...] = grad
        d_ref[...], mo_ref[...], vo_ref[...] = _adamw_math(ws_ref[...], grad, m_ref[...], v_ref[...])

    tile = pl.BlockSpec((tm, d_model), lambda i: (i, 0))
    slab = pl.BlockSpec((tr, w.shape[1]), lambda i: (i, 0))
    flat = [a for pk in packed for a in pk[:3]] + [a for grp in groups for a in grp]
    shapes = [pk[0].shape for pk in packed] + [grp[0].shape for grp in groups]
    res = pl.pallas_call(
        body, name="grad_x_adamw", grid=(n_steps,),
        in_specs=[pl.BlockSpec((tm, N_DEV * d_model), lambda i: (i, 0)), ANY, tile, slab, slab, slab, slab]
        + [VMEM] * (1 + len(flat)),
        out_specs=[tile, slab, slab, slab, slab, VMEM] + [VMEM] * n_out,
        out_shape=[SDS((n_tok, d_model), F32)] + [SDS(w.shape, F32)] * 4 + [SDS((1, 1), F32)]
        + [SDS(shape, F32) for shape in shapes for _ in range(4)],
        scratch_shapes=[pltpu.VMEM(w_all.shape, BF16), pltpu.SemaphoreType.DMA((N_DEV,))],
        compiler_params=_params(1),
    )(dproj, w_all, dr, w, g, m, v, small, *flat)
    return (res[0], tuple(res[1:5]), [tuple(res[6 + 4 * p:10 + 4 * p]) for p in range(n_pk + n_grp)],
            res[5].reshape(()))


def _adamw_math(w, g, m, v):
    m = ADAM_B1 * m + (1.0 - ADAM_B1) * g
    v = ADAM_B2 * v + (1.0 - ADAM_B2) * (g * g)
    m_hat = m / (1.0 - ADAM_B1 ** ADAM_STEP)
    v_hat = v / (1.0 - ADAM_B2 ** ADAM_STEP)
    delta = -ADAM_LR * (m_hat / (jnp.sqrt(v_hat) + ADAM_EPS) + ADAM_WD * w)
    return delta, m, v


def _as_rows(a):
    return a.reshape(-1, LANES)


def kernel(x, w_in, b_in, conv_w, conv_b, gn_g, gn_b, ln_v_g, ln_v_b, w_spatial, b_spatial, w_pa, w_pb, w_o, b_o, ln_out_g, ln_out_b, loss_target, m_w_in, m_b_in, m_conv_w, m_conv_b, m_gn_g, m_gn_b, m_ln_v_g, m_ln_v_b, m_w_spatial, m_b_spatial, m_w_pa, m_w_pb, m_w_o, m_b_o, m_ln_out_g, m_ln_out_b, v_w_in, v_b_in, v_conv_w, v_conv_b, v_gn_g, v_gn_b, v_ln_v_g, v_ln_v_b, v_w_spatial, v_b_spatial, v_w_pa, v_w_pb, v_w_o, v_b_o, v_ln_out_g, v_ln_out_b):
    n_batch, seq, d_model = x.shape
    n_tok = n_batch * seq
    n_heads = d_model // LANES
    dc = conv_w.shape[1]
    me = 4 * lax.axis_index("x") + 2 * lax.axis_index("y") + lax.axis_index("c")
    row = lambda a: a.reshape(1, d_model)

    x2 = x.reshape(n_tok, d_model)
    target2 = loss_target.reshape(n_tok, d_model)
    b_spatial_t = b_spatial.T

    first = jnp.where(lax.axis_index("c") == 1, 4, 2)
    second = 6 - first
    ag_rel = jnp.stack([0 * first, 0 * first + 1, first, second + 1, second, first + 1, 0 * first + 6, 0 * first + 7])
    ag_blocks = jnp.bitwise_xor(me, ag_rel).astype(jnp.int32)
    proj, xt_bf, w_all, wp_all, cw_all = _proj_all_gather(
        x2, w_in, w_pa, w_pb, w_o, conv_w, b_in.reshape(N_DEV, 1, d_model), ag_blocks)
    wp_full = wp_all.reshape(3, d_model, d_model)
    conv_w_full = jnp.pad(cw_all.transpose(1, 0, 2).reshape(CONV_K, d_model), ((0, HALO - CONV_K), (0, 0)))

    h3, h1 = _branch_a_fwd(proj, conv_w_full, row(conv_b), row(gn_g), row(gn_b), seq)
    s = _branch_b_fwd(proj, row(ln_v_g), row(ln_v_b), w_spatial, b_spatial_t, seq)

    dproj, d_h3, d_s, dr, lhs3, rhs3, vec_mid = _mid(
        h3, s, proj, x2, target2, wp_full, row(b_o), row(ln_out_g), row(ln_out_b))

    dproj, vec_b, d_ws, d_bs_t = _branch_b_bwd(dproj, proj, d_s, row(ln_v_g), row(ln_v_b), w_spatial, b_spatial_t, seq)
    dproj, d_h1, vec_a1 = _branch_a_bwd_norm(dproj, proj, h1, d_h3, row(gn_g), row(gn_b), seq)
    gp_f32, gp_bf = _weight_grads(lhs3, rhs3)
    dproj, vec_a2, d_cw8, g_w_pa, g_w_pb, g_w_o = _branch_a_bwd_conv(
        dproj, proj, d_h1, conv_w_full, gp_bf, gp_f32, seq)

    pieces = [vec_a2, vec_a1, vec_b, vec_mid, d_bs_t.T, _as_rows(d_ws), d_cw8]
    a2, a1, vb, mid = 0, 1, 2, 3
    plan = ([("row", p, r) for p, r in [(a2, 0), (a2, 1), (a1, 0), (vb, 0), (vb, 1), (vb, 2), (mid, 3), (mid, 4)]]
            + [("row", p, r) for p, r in [(a1, 3), (a1, 1), (a1, 2), (vb, 3), (vb, 4), (mid, 2), (mid, 0), (mid, 1)]]
            + [("block", 4, None), ("block", 5, None), ("tiles", 6, None), ("row", mid, 5)])

    rs_rel = jnp.stack([0 * first + 7, 0 * first + 6, first + 1, second, second + 1, first, 0 * first + 1, 0 * first])
    rs_blocks = jnp.bitwise_xor(me, rs_rel).astype(jnp.int32)
    g_w_in, small = _grad_w_in_reduce_scatter(xt_bf, dproj, rs_blocks, pieces, plan)

    g_rows = d_model // LANES
    o0 = N_DEV * g_rows
    o1 = o0 + 8 * g_rows
    o2 = o1 + n_heads
    o3 = o2 + n_heads * LANES
    o4 = o3 + n_heads * HALO
    g_cw_full = small[o3:o4].reshape(n_heads, HALO, LANES).transpose(1, 0, 2).reshape(HALO, d_model)
    g_conv_w = lax.dynamic_slice(g_cw_full, (0, me * dc), (CONV_K, dc))

    two_d = lambda a: a.reshape(-1, a.shape[-1]) if a.ndim != 1 else (
        a.reshape(-1, LANES) if a.shape[0] % LANES == 0 else a.reshape(1, -1))
    names = ["b_in", "conv_w", "conv_b", "gn_g", "gn_b", "ln_v_g", "ln_v_b", "w_spatial", "b_spatial",
             "w_pa", "w_pb", "w_o", "b_o", "ln_out_g", "ln_out_b"]
    ws = dict(b_in=b_in, conv_w=conv_w, conv_b=conv_b, gn_g=gn_g, gn_b=gn_b, ln_v_g=ln_v_g, ln_v_b=ln_v_b,
              w_spatial=w_spatial, b_spatial=b_spatial, w_pa=w_pa, w_pb=w_pb, w_o=w_o, b_o=b_o,
              ln_out_g=ln_out_g, ln_out_b=ln_out_b)
    ms = dict(b_in=m_b_in, conv_w=m_conv_w, conv_b=m_conv_b, gn_g=m_gn_g, gn_b=m_gn_b, ln_v_g=m_ln_v_g,
              ln_v_b=m_ln_v_b, w_spatial=m_w_spatial, b_spatial=m_b_spatial, w_pa=m_w_pa, w_pb=m_w_pb,
              w_o=m_w_o, b_o=m_b_o, ln_out_g=m_ln_out_g, ln_out_b=m_ln_out_b)
    vs = dict(b_in=v_b_in, conv_w=v_conv_w, conv_b=v_conv_b, gn_g=v_gn_g, gn_b=v_gn_b, ln_v_g=v_ln_v_g,
              ln_v_b=v_ln_v_b, w_spatial=v_w_spatial, b_spatial=v_b_spatial, w_pa=v_w_pa, w_pb=v_w_pb,
              w_o=v_w_o, b_o=v_b_o, ln_out_g=v_ln_out_g, ln_out_b=v_ln_out_b)
    vec_names = ["conv_b", "gn_g", "gn_b", "ln_v_g", "ln_v_b", "b_o", "ln_out_g", "ln_out_b"]
    first_row = dict(b_in=0, b_spatial=o1, w_spatial=o2, **{n: o0 + a * g_rows for a, n in enumerate(vec_names)})
    given = dict(conv_w=g_conv_w, w_pa=g_w_pa, w_pb=g_w_pb, w_o=g_w_o)
    grad_x, upd_w_in, upd, loss = _grad_x_adamw(
        dproj, w_all, dr, w_in, g_w_in, m_w_in, v_w_in, small, (o4, g_rows),
        [(two_d(ws[n]), two_d(ms[n]), two_d(vs[n]), row0) for n, row0 in first_row.items()],
        [tuple(two_d(a) for a in (ws[n], grad, ms[n], vs[n])) for n, grad in given.items()])
    grad_x = grad_x.reshape(x.shape)
    res = {n: tuple(a.reshape(ws[n].shape) for a in u) for n, u in zip([*first_row, *given], upd)}
    res["w_in"] = upd_w_in

    order = ["w_in"] + names
    return (loss, grad_x, *[res[n][k] for k in range(4) for n in order])
```

```python
import jax
import jax.numpy as jnp
from jax import lax
from jax.experimental import pallas as pl
from jax.experimental.pallas import tpu as pltpu

F32 = jnp.float32
BF16 = jnp.bfloat16
SDS = jax.ShapeDtypeStruct

N_DEV = 8
LANES = 128
SUBLANES = 8
CONV_K = 31
HALO = 32
ELEMENTWISE_ROWS = 512
DP_SLOTS = 3
XT_CHUNKS = 4
LN_EPS = 1e-5
DEEPNORM_ALPHA = 2.0 ** 0.25
ADAM_LR, ADAM_B1, ADAM_B2, ADAM_EPS, ADAM_WD, ADAM_STEP = 0.001, 0.9, 0.999, 1e-08, 0.01, 10
GELU_C = 0.7978845608028654
GELU_A = 0.044715
VMEM_LIMIT = 56 * 1024 * 1024
MESH = pl.DeviceIdType.MESH
ANY = pl.BlockSpec(memory_space=pl.ANY)
VMEM = pl.BlockSpec(memory_space=pltpu.VMEM)


def _params(n_grid=0):
    sem = ("arbitrary",) * n_grid if n_grid else None
    return pltpu.CompilerParams(dimension_semantics=sem, vmem_limit_bytes=VMEM_LIMIT)


def _tile(n, pref, mult):
    t = min(n, pref)
    while n % t or t % mult:
        t -= 1
    return t


def _colsum(v):
    return jnp.sum(v, axis=0, keepdims=True)


def _sigmoid(v):
    return jax.nn.sigmoid(v)


def _silu_and_grad(v):
    s = _sigmoid(v)
    val = v * s
    return val, s + val * (1.0 - s)


def _gelu_and_grad(v):
    v2 = v * v
    sg = _sigmoid(v * (2.0 * GELU_C + (2.0 * GELU_C * GELU_A) * v2))
    grad = sg + v * sg * (1.0 - sg) * (2.0 * GELU_C + (6.0 * GELU_C * GELU_A) * v2)
    return v * sg, grad


def _tril_mask():
    r = lax.broadcasted_iota(jnp.int32, (LANES, LANES), 0)
    c = lax.broadcasted_iota(jnp.int32, (LANES, LANES), 1)
    return c <= r


def _dot(a, b):
    return jnp.dot(a, b, preferred_element_type=F32)


def _dot_tb(a, b):
    return lax.dot_general(a, b, (((1,), (1,)), ((), ())), preferred_element_type=F32)


def _dot_ta(a, b):
    return lax.dot_general(a, b, (((0,), (0,)), ((), ())), preferred_element_type=F32)


def _mesh_pos():
    return lax.axis_index("x"), lax.axis_index("y"), lax.axis_index("c")


def _block_of(pos):
    return 4 * pos[0] + 2 * pos[1] + pos[2]


def _peers():
    x, y, c = _mesh_pos()
    out = []
    for k in range(1, N_DEV):
        pos = (1 - x if k & 4 else x, 1 - y if k & 2 else y, 1 - c if k & 1 else c)
        out.append((pos, _block_of(pos)))
    return out


def _proj_all_gather(x2, w_in, w_pa, w_pb, w_o, conv_w, b_in3, blk_order):
    n_tok, d_model = x2.shape
    r8 = w_pa.shape[0]
    kc, dc = conv_w.shape
    tm = _tile(n_tok, 1024, LANES)
    n_t = n_tok // tm
    n_arr = 3

    def body(ord_ref, x_ref, b_ref, win_ref, wpa_ref, wpb_ref, wo_ref, cw_ref,
             proj_ref, xt_ref, wall_ref, wp_ref, cwall_ref,
             wbuf, xbuf, st_p, send_sems, recv_sems, local_sems, wall_sems):
        s = pl.program_id(0)
        t = pl.program_id(1)
        x, y, c = _mesh_pos()
        me = (x, y, c)
        sibling = (x, y, 1 - c)
        n1 = (jnp.bitwise_xor(x, c), jnp.bitwise_xor(y, 1 - c))
        n2 = (jnp.bitwise_xor(x, 1 - c), jnp.bitwise_xor(y, c))
        dg = (1 - x, 1 - y)
        outs = [wbuf, wp_ref, cwall_ref]
        srcs = [None, st_p, cw_ref]
        consumed = [me, sibling, (*n1, c), (*n2, 1 - c), (*n2, c), (*n1, 1 - c), (*dg, c), (*dg, 1 - c)]
        leaves = [(me, sibling), (me, (*n1, c)), (me, (*n2, c)), ((*n1, c), (*n2, c)),
                  ((*n1, c), sibling), ((*n2, c), sibling), ((*dg, c), sibling)]
        lands = [sibling, (*n1, c), (*n2, c), (*dg, c), (*n2, 1 - c), (*n1, 1 - c), (*dg, 1 - c)]

        def slot(o, pos):
            return o.at[:, _block_of(pos)] if o is wp_ref else o.at[_block_of(pos)]

        def copy(a, k, block, to, src=None):
            o = outs[a]
            return pltpu.make_async_remote_copy(
                src_ref=slot(o, block) if src is None else src, dst_ref=slot(o, block),
                send_sem=send_sems.at[a, k], recv_sem=recv_sems.at[a, k],
                device_id=to, device_id_type=MESH)

        def send(a, k):
            block, to = leaves[k]
            return copy(a, k, block, to, src=srcs[a] if k < 3 else None)

        def recv(a, k):
            return copy(a, k, lands[k], me)

        def local_copies():
            return [pltpu.make_async_copy(srcs[a], slot(outs[a], me), local_sems.at[a]) for a in (1, 2)]

        def to_hbm(step):
            return pltpu.make_async_copy(slot(wbuf, consumed[step]), slot(wall_ref, consumed[step]),
                                         wall_sems.at[step])

        def at_step(step):
            return pl.when(jnp.logical_and(s == step, t == 0))

        @at_step(0)
        def _():
            slot(wbuf, me)[...] = win_ref[...].astype(BF16)
            st_p[0] = wpa_ref[...].astype(BF16)
            st_p[1] = wpb_ref[...].astype(BF16)
            st_p[2] = wo_ref[...].astype(BF16)
            for a in range(n_arr):
                send(a, 0).start()
                send(a, 1).start()
            for cp in local_copies():
                cp.start()
            to_hbm(0).start()

        @at_step(1)
        def _():
            recv(0, 0).wait_recv()
            to_hbm(1).start()

        for rnd in range(3):
            @at_step(2 + 2 * rnd)
            def _(rnd=rnd):
                if rnd == 0:
                    for a in range(n_arr):
                        send(a, 2).start()
                recv(0, 1 + rnd).wait_recv()
                if rnd == 0:
                    send(0, 3).start()
                send(0, 4 + rnd).start()
                to_hbm(2 + 2 * rnd).start()

            @at_step(3 + 2 * rnd)
            def _(rnd=rnd):
                for a in (1, 2):
                    recv(a, 1 + rnd).wait_recv()
                    if rnd == 0:
                        send(a, 3).start()
                    send(a, 4 + rnd).start()
                recv(0, 4 + rnd).wait_recv()
                to_hbm(3 + 2 * rnd).start()

        rows = pl.ds(pl.multiple_of(t * tm, tm), tm)

        @pl.when(s == 0)
        def _():
            xb = x_ref[...].astype(BF16)
            xbuf[rows, :] = xb
            xt_ref[...] = xb.T

        proj_ref[...] = _dot(xbuf[rows, :], wbuf[ord_ref[s]]) + b_ref[...]

        @pl.when(jnp.logical_and(s == N_DEV - 1, t == n_t - 1))
        def _():
            for a in (1, 2):
                for k in (0, 4, 5, 6):
                    recv(a, k).wait_recv()
            for a in range(n_arr):
                for k in range(7):
                    send(a, k).wait_send()
            for cp in local_copies() + [to_hbm(step) for step in range(N_DEV)]:
                cp.wait()

    grid_spec = pltpu.PrefetchScalarGridSpec(
        num_scalar_prefetch=1, grid=(N_DEV, n_t),
        in_specs=[pl.BlockSpec((tm, d_model), lambda s, t, o: (jnp.where(s == 0, t, n_t - 1), 0)),
                  pl.BlockSpec((None, 1, d_model), lambda s, t, o: (o[s], 0, 0)),
                  VMEM, VMEM, VMEM, VMEM, VMEM],
        out_specs=[pl.BlockSpec((tm, d_model), lambda s, t, o: (t, o[s])),
                   pl.BlockSpec((d_model, tm), lambda s, t, o: (0, jnp.where(s == 0, t, n_t))),
                   ANY, ANY, ANY],
        scratch_shapes=[pltpu.VMEM((N_DEV, d_model, d_model), BF16), pltpu.VMEM((n_tok, d_model), BF16),
                        pltpu.VMEM((3, r8, d_model), BF16),
                        pltpu.SemaphoreType.DMA((n_arr, 7)), pltpu.SemaphoreType.DMA((n_arr, 7)),
                        pltpu.SemaphoreType.DMA((3,)), pltpu.SemaphoreType.DMA((N_DEV,))])
    return pl.pallas_call(
        body, name="proj_all_gather", grid_spec=grid_spec,
        out_shape=[SDS((n_tok, N_DEV * d_model), F32), SDS((d_model, (n_t + 1) * tm), BF16),
                   SDS((N_DEV, d_model, d_model), BF16), SDS((3, N_DEV, r8, d_model), BF16),
                   SDS((N_DEV, kc, dc), F32)],
        compiler_params=_params(2),
    )(blk_order, x2, b_in3, w_in, w_pa, w_pb, w_o, conv_w)


def _conv_rows(ta):
    return _tile(ta, 64, SUBLANES)


def _branch_a_fwd(proj, conv_w_full, conv_b, gn_g, gn_b, seq):
    n_tok = proj.shape[0]
    d_model = conv_b.shape[1]
    ta = _tile(seq, ELEMENTWISE_ROWS, HALO)
    per_seq = seq // ta
    rc = _conv_rows(ta)

    def body(av_ref, ag_ref, gt_ref, avh_ref, agh_ref, cw_ref, cb_ref, gg_ref, gb_ref,
             h3_ref, h1_ref, ext):
        keep = jnp.where(pl.program_id(0) % per_seq == 0, 0.0, 1.0)

        def group(g, carry):
            sl = pl.ds(pl.multiple_of(g * LANES, LANES), LANES)
            ext[0:HALO, :] = avh_ref[:, sl] * _sigmoid(agh_ref[:, sl]) * keep
            ext[HALO:HALO + ta, :] = av_ref[:, sl] * _sigmoid(ag_ref[:, sl])
            for r0 in range(0, ta, rc):
                acc = jnp.broadcast_to(cb_ref[:, sl], (rc, LANES))
                for k in range(CONV_K):
                    acc = acc + ext[pl.ds(r0 + HALO - (CONV_K - 1) + k, rc), :] * cw_ref[k:k + 1, sl]
                h1_ref[pl.ds(r0, rc), sl] = acc
            h1 = h1_ref[:, sl]
            mu = jnp.mean(h1, axis=-1, keepdims=True)
            dlt = h1 - mu
            var = jnp.mean(dlt * dlt, axis=-1, keepdims=True)
            h2 = dlt * lax.rsqrt(var + LN_EPS) * gg_ref[:, sl] + gb_ref[:, sl]
            gate = gt_ref[:, sl]
            h3_ref[:, sl] = (h2 * _sigmoid(h2) * gate * _sigmoid(gate)).astype(BF16)
            return carry

        lax.fori_loop(0, d_model // LANES, group, 0, unroll=True)

    blk = lambda j: pl.BlockSpec((ta, d_model), lambda i: (i, j))
    halo = lambda j: pl.BlockSpec((HALO, d_model), lambda i: (jnp.maximum(i * (ta // HALO) - 1, 0), j))
    row = pl.BlockSpec((1, d_model), lambda i: (0, 0))
    return pl.pallas_call(
        body, name="branch_a_fwd", grid=(n_tok // ta,),
        in_specs=[blk(0), blk(1), blk(2), halo(0), halo(1),
                  pl.BlockSpec((HALO, d_model), lambda i: (0, 0)), row, row, row],
        out_specs=[pl.BlockSpec((ta, d_model), lambda i: (i, 0))] * 2,
        out_shape=[SDS((n_tok, d_model), BF16), SDS((n_tok, d_model), F32)],
        scratch_shapes=[pltpu.VMEM((HALO + ta, LANES), F32)],
        compiler_params=_params(1),
    )(proj, proj, proj, proj, proj, conv_w_full, conv_b, gn_g, gn_b)


def _branch_b_fwd(proj, ln_g, ln_b, w_spatial, b_spatial_t, seq):
    n_tok = proj.shape[0]
    d_model = ln_g.shape[1]
    n_heads = d_model // LANES
    tb = _tile(seq, ELEMENTWISE_ROWS, LANES)

    def body(u_ref, v_ref, bg_ref, lg_ref, lb_ref, ws_ref, bs_ref, s_ref, vn_buf):
        v, _ = _gelu_and_grad(v_ref[...])
        mu = jnp.mean(v, axis=-1, keepdims=True)
        dlt = v - mu
        var = jnp.mean(dlt * dlt, axis=-1, keepdims=True)
        vn_buf[...] = (dlt * lax.rsqrt(var + LN_EPS) * lg_ref[...] + lb_ref[...]).astype(BF16)
        tril = _tril_mask()
        for h in range(n_heads):
            cols = slice(h * LANES, (h + 1) * LANES)
            w_h = jnp.where(tril, ws_ref[h], 0.0).astype(BF16)
            bias = bs_ref[:, h:h + 1]
            for ch in range(tb // LANES):
                rows = slice(ch * LANES, (ch + 1) * LANES)
                mix = _dot(w_h, vn_buf[rows, cols]) + bias
                u, _ = _gelu_and_grad(u_ref[rows, cols])
                gate = bg_ref[rows, cols]
                s_ref[rows, cols] = (u * mix * gate * _sigmoid(gate)).astype(BF16)

    blk = lambda j: pl.BlockSpec((tb, d_model), lambda i: (i, j))
    row = pl.BlockSpec((1, d_model), lambda i: (0, 0))
    return pl.pallas_call(
        body, name="branch_b_fwd", grid=(n_tok // tb,),
        in_specs=[blk(3), blk(4), blk(5), row, row,
                  pl.BlockSpec((n_heads, LANES, LANES), lambda i: (0, 0, 0)),
                  pl.BlockSpec((LANES, n_heads), lambda i: (0, 0))],
        out_specs=pl.BlockSpec((tb, d_model), lambda i: (i, 0)),
        out_shape=SDS((n_tok, d_model), BF16),
        scratch_shapes=[pltpu.VMEM((tb, d_model), BF16)],
        compiler_params=_params(1),
    )(proj, proj, proj, ln_g, ln_b, w_spatial, b_spatial_t)


MID_ROWS = 8


def _mid(h3, s, proj, x2, target, wp_full, b_o, lo_g, lo_b):
    n_tok, d_model = x2.shape
    tm = _tile(n_tok, 256, 16)

    def body(h3_ref, s_ref, ma_ref, mb_ref, x_ref, t_ref, wpa_ref, wpb_ref, wo_ref, bo_ref,
             lg_ref, lb_ref, dproj_ref, dh3_ref, ds_ref, dr_ref, lhs3_ref, rhs3_ref, vec_ref):
        @pl.when(pl.program_id(0) == 0)
        def _():
            vec_ref[...] = jnp.zeros_like(vec_ref)

        h3 = h3_ref[...]
        s = s_ref[...]
        ya = _dot(h3, wpa_ref[...])
        yb = _dot(s, wpb_ref[...])
        ga = _sigmoid(ma_ref[...])
        gb = _sigmoid(mb_ref[...])
        mixed = (ga * ya + gb * yb).astype(BF16)
        lhs3_ref[0] = h3
        lhs3_ref[1] = s
        lhs3_ref[2] = mixed
        r = DEEPNORM_ALPHA * x_ref[...] + _dot(mixed, wo_ref[...]) + bo_ref[...]
        mu = jnp.mean(r, axis=-1, keepdims=True)
        dlt = r - mu
        rstd = lax.rsqrt(jnp.mean(dlt * dlt, axis=-1, keepdims=True) + LN_EPS)
        rhat = dlt * rstd
        diff = rhat * lg_ref[...] + lb_ref[...] - t_ref[...]
        dy = diff * (1.0 / d_model)
        vec_ref[0:1, :] += _colsum(dy * rhat)
        vec_ref[1:2, :] += _colsum(dy)
        vec_ref[5:6, :] += _colsum(diff * diff)
        drh = dy * lg_ref[...]
        dr = rstd * (drh - jnp.mean(drh, axis=-1, keepdims=True)
                     - rhat * jnp.mean(drh * rhat, axis=-1, keepdims=True))
        vec_ref[2:3, :] += _colsum(dr)
        dr_ref[...] = dr
        drb = dr.astype(BF16)
        rhs3_ref[2] = drb
        dmixed = _dot_tb(drb, wo_ref[...])
        dya_f = dmixed * ga
        dyb_f = dmixed * gb
        dma = dya_f * ya * (1.0 - ga)
        dmb = dyb_f * yb * (1.0 - gb)
        vec_ref[3:4, :] += _colsum(dma)
        vec_ref[4:5, :] += _colsum(dmb)
        dproj_ref[:, 0:d_model] = dma.astype(BF16)
        dproj_ref[:, d_model:2 * d_model] = dmb.astype(BF16)
        dya = dya_f.astype(BF16)
        dyb = dyb_f.astype(BF16)
        rhs3_ref[0] = dya
        rhs3_ref[1] = dyb
        dh3_ref[...] = _dot_tb(dya, wpa_ref[...])
        ds_ref[...] = _dot_tb(dyb, wpb_ref[...])

    tile = pl.BlockSpec((tm, d_model), lambda i: (i, 0))
    full = lambda a: pl.BlockSpec((None, d_model, d_model), lambda i: (a, 0, 0))
    row = pl.BlockSpec((1, d_model), lambda i: (0, 0))
    stack = pl.BlockSpec((3, tm, d_model), lambda i: (0, i, 0))
    bf3 = SDS((3, n_tok, d_model), BF16)
    f32 = SDS((n_tok, d_model), F32)
    return pl.pallas_call(
        body, name="mid", grid=(n_tok // tm,),
        in_specs=[tile, tile, pl.BlockSpec((tm, d_model), lambda i: (i, 6)),
                  pl.BlockSpec((tm, d_model), lambda i: (i, 7)), tile, tile, full(0), full(1), full(2),
                  row, row, row],
        out_specs=[pl.BlockSpec((tm, 2 * d_model), lambda i: (i, 3)), tile, tile, tile, stack, stack,
                   pl.BlockSpec((MID_ROWS, d_model), lambda i: (0, 0))],
        out_shape=[SDS((n_tok, N_DEV * d_model), BF16), f32, f32, f32, bf3, bf3,
                   SDS((MID_ROWS, d_model), F32)],
        compiler_params=_params(1),
    )(h3, s, proj, proj, x2, target, wp_full, wp_full, wp_full, b_o, lo_g, lo_b)


B_ROWS = 8


def _branch_b_bwd(dproj, proj, d_s, ln_g, ln_b, w_spatial, b_spatial_t, seq):
    n_tok = proj.shape[0]
    d_model = ln_g.shape[1]
    n_heads = d_model // LANES
    tb = _tile(seq, ELEMENTWISE_ROWS, LANES)

    def body(dproj_in, u_ref, v_ref, bg_ref, ds_ref, lg_ref, lb_ref, ws_ref, bs_ref,
             dproj_ref, vec_ref, dws_ref, dbs_ref, vn_buf, dv_buf):
        del dproj_in

        @pl.when(pl.program_id(0) == 0)
        def _():
            vec_ref[...] = jnp.zeros_like(vec_ref)
            dws_ref[...] = jnp.zeros_like(dws_ref)
            dbs_ref[...] = jnp.zeros_like(dbs_ref)

        v, dgelu_v = _gelu_and_grad(v_ref[...])
        mu = jnp.mean(v, axis=-1, keepdims=True)
        dlt = v - mu
        rstd = lax.rsqrt(jnp.mean(dlt * dlt, axis=-1, keepdims=True) + LN_EPS)
        vhat = dlt * rstd
        vn_buf[...] = (vhat * lg_ref[...] + lb_ref[...]).astype(BF16)
        tril = _tril_mask()
        for h in range(n_heads):
            cols = slice(h * LANES, (h + 1) * LANES)
            w_h = jnp.where(tril, ws_ref[h], 0.0).astype(BF16)
            bias = bs_ref[:, h:h + 1]
            for ch in range(tb // LANES):
                rows = slice(ch * LANES, (ch + 1) * LANES)
                vn = vn_buf[rows, cols]
                mix = _dot(w_h, vn) + bias
                u, dgelu_u = _gelu_and_grad(u_ref[rows, cols])
                sg, dsilu = _silu_and_grad(bg_ref[rows, cols])
                dsv = ds_ref[rows, cols]
                du = dsv * mix * sg * dgelu_u
                dbg = dsv * u * mix * dsilu
                dmix = dsv * u * sg
                dmix_bf = dmix.astype(BF16)
                dproj_ref[rows, cols] = du.astype(BF16)
                dproj_ref[rows, 2 * d_model + h * LANES:2 * d_model + (h + 1) * LANES] = dbg.astype(BF16)
                vec_ref[0:1, cols] += _colsum(du)
                vec_ref[2:3, cols] += _colsum(dbg)
                dbs_ref[:, h:h + 1] += jnp.sum(dmix, axis=1, keepdims=True)
                dws_ref[h] += jnp.where(tril, _dot_tb(dmix_bf, vn), 0.0)
                dv_buf[rows, cols] = _dot_ta(w_h, dmix_bf)
        dvn = dv_buf[...]
        vec_ref[3:4, :] += _colsum(dvn * vhat)
        vec_ref[4:5, :] += _colsum(dvn)
        dvh = dvn * lg_ref[...]
        dv = rstd * (dvh - jnp.mean(dvh, axis=-1, keepdims=True)
                     - vhat * jnp.mean(dvh * vhat, axis=-1, keepdims=True)) * dgelu_v
        vec_ref[1:2, :] += _colsum(dv)
        dproj_ref[:, d_model:2 * d_model] = dv.astype(BF16)

    blk = lambda j: pl.BlockSpec((tb, d_model), lambda i: (i, j))
    row = pl.BlockSpec((1, d_model), lambda i: (0, 0))
    return pl.pallas_call(
        body, name="branch_b_bwd", grid=(n_tok // tb,),
        in_specs=[ANY, blk(3), blk(4), blk(5), pl.BlockSpec((tb, d_model), lambda i: (i, 0)), row, row,
                  pl.BlockSpec((n_heads, LANES, LANES), lambda i: (0, 0, 0)),
                  pl.BlockSpec((LANES, n_heads), lambda i: (0, 0))],
        out_specs=[pl.BlockSpec((tb, 3 * d_model), lambda i: (i, 1)),
                   pl.BlockSpec((B_ROWS, d_model), lambda i: (0, 0)),
                   pl.BlockSpec((n_heads, LANES, LANES), lambda i: (0, 0, 0)),
                   pl.BlockSpec((LANES, n_heads), lambda i: (0, 0))],
        out_shape=[SDS(dproj.shape, BF16), SDS((B_ROWS, d_model), F32),
                   SDS((n_heads, LANES, LANES), F32), SDS((LANES, n_heads), F32)],
        scratch_shapes=[pltpu.VMEM((tb, d_model), BF16), pltpu.VMEM((tb, d_model), F32)],
        input_output_aliases={0: 0},
        compiler_params=_params(1),
    )(dproj, proj, proj, proj, d_s, ln_g, ln_b, w_spatial, b_spatial_t)


A1_ROWS = 8


def _branch_a_bwd_norm(dproj, proj, h1, d_h3, gn_g, gn_b, seq):
    n_tok = proj.shape[0]
    d_model = gn_g.shape[1]
    ta = _tile(seq, ELEMENTWISE_ROWS, 16)

    def body(dproj_in, gt_ref, h1_ref, dh3_ref, gg_ref, gb_ref, dproj_ref, dh1_ref, vec_ref):
        del dproj_in

        @pl.when(pl.program_id(0) == 0)
        def _():
            vec_ref[...] = jnp.zeros_like(vec_ref)

        def group(g, carry):
            sl = pl.ds(pl.multiple_of(g * LANES, LANES), LANES)
            h1 = h1_ref[:, sl]
            mu = jnp.mean(h1, axis=-1, keepdims=True)
            dlt = h1 - mu
            rstd = lax.rsqrt(jnp.mean(dlt * dlt, axis=-1, keepdims=True) + LN_EPS)
            nrm = dlt * rstd
            sw, dsw = _silu_and_grad(nrm * gg_ref[:, sl] + gb_ref[:, sl])
            sg, dsg = _silu_and_grad(gt_ref[:, sl])
            dh3 = dh3_ref[:, sl]
            dgate = dh3 * sw * dsg
            dproj_ref[:, sl] = dgate.astype(BF16)
            vec_ref[0:1, sl] += _colsum(dgate)
            dh2 = dh3 * sg * dsw
            vec_ref[1:2, sl] += _colsum(dh2 * nrm)
            vec_ref[2:3, sl] += _colsum(dh2)
            dn = dh2 * gg_ref[:, sl]
            dh1 = rstd * (dn - jnp.mean(dn, axis=-1, keepdims=True)
                          - nrm * jnp.mean(dn * nrm, axis=-1, keepdims=True))
            vec_ref[3:4, sl] += _colsum(dh1)
            dh1_ref[:, sl] = dh1
            return carry

        lax.fori_loop(0, d_model // LANES, group, 0, unroll=True)

    tile = pl.BlockSpec((ta, d_model), lambda i: (i, 0))
    row = pl.BlockSpec((1, d_model), lambda i: (0, 0))
    return pl.pallas_call(
        body, name="branch_a_bwd_norm", grid=(n_tok // ta,),
        in_specs=[ANY, pl.BlockSpec((ta, d_model), lambda i: (i, 2)), tile, tile, row, row],
        out_specs=[pl.BlockSpec((ta, d_model), lambda i: (i, 2)), tile,
                   pl.BlockSpec((A1_ROWS, d_model), lambda i: (0, 0))],
        out_shape=[SDS(dproj.shape, BF16), SDS((n_tok, d_model), F32), SDS((A1_ROWS, d_model), F32)],
        input_output_aliases={0: 0},
        compiler_params=_params(1),
    )(dproj, proj, h1, d_h3, gn_g, gn_b)


A2_ROWS = 8


def _branch_a_bwd_conv(dproj, proj, d_h1, conv_w_full, gp_bf, gp_f32, seq):
    n_tok = proj.shape[0]
    d_model = conv_w_full.shape[1]
    n_groups = d_model // LANES
    ta = _tile(seq, ELEMENTWISE_ROWS, HALO)
    n_tiles = n_tok // ta
    per_seq = seq // ta
    rc = _conv_rows(ta)
    last_halo = n_tok // HALO - 1
    r8 = d_model // N_DEV
    prow = _tile(r8, 32, 16)

    def body(dproj_in, av_ref, ag_ref, avh_ref, agh_ref, dh1_ref, dh1h_ref, cw_ref, gp_bf_ref, gp_f32_ref,
             dproj_ref, vec_ref, dcw_ref, opa_ref, opb_ref, opo_ref,
             ext_h0, ext_d, rbuf, own, send_sems, recv_sems, local_sems):
        del dproj_in
        i = pl.program_id(0)
        x, y, c = _mesh_pos()
        me = _block_of((x, y, c))
        peers = _peers()

        def sends():
            return [pltpu.make_async_remote_copy(
                src_ref=gp_bf_ref.at[a, pl.ds(pl.multiple_of(blk * r8, 16), r8), :], dst_ref=rbuf.at[k, a],
                send_sem=send_sems.at[a, k], recv_sem=recv_sems.at[a, k], device_id=pos, device_id_type=MESH)
                for k, (pos, blk) in enumerate(peers) for a in range(3)]

        def own_rows():
            return [pltpu.make_async_copy(gp_f32_ref.at[a, pl.ds(pl.multiple_of(me * r8, 8), r8), :],
                                          own.at[a], local_sems.at[a]) for a in range(3)]

        @pl.when(i == 0)
        def _():
            vec_ref[...] = jnp.zeros_like(vec_ref)
            dcw_ref[...] = jnp.zeros_like(dcw_ref)
            for cp in sends() + own_rows():
                cp.start()

        keep_past = jnp.where(i % per_seq == 0, 0.0, 1.0)
        keep_next = jnp.where(i % per_seq == per_seq - 1, 0.0, 1.0)

        def group(g, carry):
            sl = pl.ds(pl.multiple_of(g * LANES, LANES), LANES)
            av = av_ref[:, sl]
            sig = _sigmoid(ag_ref[:, sl])
            ext_h0[0:HALO, :] = avh_ref[:, sl] * _sigmoid(agh_ref[:, sl]) * keep_past
            ext_h0[HALO:HALO + ta, :] = av * sig
            ext_d[0:ta, :] = dh1_ref[:, sl]
            ext_d[ta:ta + HALO, :] = dh1h_ref[:, sl] * keep_next
            for r0 in range(0, ta, rc):
                dh1 = ext_d[pl.ds(r0, rc), :]
                acc = jnp.zeros((rc, LANES), F32)
                for k in range(CONV_K):
                    acc = acc + ext_d[pl.ds(r0 + CONV_K - 1 - k, rc), :] * cw_ref[k:k + 1, sl]
                    prod = dh1 * ext_h0[pl.ds(r0 + HALO - (CONV_K - 1) + k, rc), :]
                    dcw_ref[g, k] += jnp.sum(prod.reshape(rc // SUBLANES, SUBLANES, LANES), axis=0)
                rows = pl.ds(r0, rc)
                sig_r = sig[r0:r0 + rc]
                dav = acc * sig_r
                dag = dav * av[r0:r0 + rc] * (1.0 - sig_r)
                dproj_ref[rows, sl] = dav.astype(BF16)
                dproj_ref[rows, pl.ds(pl.multiple_of(d_model + g * LANES, LANES), LANES)] = dag.astype(BF16)
                vec_ref[0:1, sl] += _colsum(dav)
                vec_ref[1:2, sl] += _colsum(dag)
            return carry

        lax.fori_loop(0, n_groups, group, 0, unroll=True)

        @pl.when(i == n_tiles - 1)
        def _():
            for cp in own_rows():
                cp.wait()
            for cp in sends():
                cp.wait_recv()
            for a, o in enumerate([opa_ref, opb_ref, opo_ref]):
                for q in range(r8 // prow):
                    r = pl.ds(q * prow, prow)
                    tot = own[a, r, :]
                    for k in range(N_DEV - 1):
                        tot = tot + rbuf[k, a, r, :].astype(F32)
                    o[r, :] = tot
            for cp in sends():
                cp.wait_send()

    blk = lambda j: pl.BlockSpec((ta, d_model), lambda i: (i, j))
    halo = lambda j: pl.BlockSpec((HALO, d_model), lambda i: (jnp.maximum(i * (ta // HALO) - 1, 0), j))
    shard = pl.BlockSpec((r8, d_model), lambda i: (0, 0))
    return pl.pallas_call(
        body, name="branch_a_bwd_conv", grid=(n_tiles,),
        in_specs=[ANY, blk(0), blk(1), halo(0), halo(1), pl.BlockSpec((ta, d_model), lambda i: (i, 0)),
                  pl.BlockSpec((HALO, d_model), lambda i: (jnp.minimum((i + 1) * (ta // HALO), last_halo), 0)),
                  pl.BlockSpec((HALO, d_model), lambda i: (0, 0)), ANY, ANY],
        out_specs=[pl.BlockSpec((ta, 2 * d_model), lambda i: (i, 0)),
                   pl.BlockSpec((A2_ROWS, d_model), lambda i: (0, 0)),
                   pl.BlockSpec((n_groups, HALO, SUBLANES, LANES), lambda i: (0, 0, 0, 0)),
                   shard, shard, shard],
        out_shape=[SDS(dproj.shape, BF16), SDS((A2_ROWS, d_model), F32),
                   SDS((n_groups, HALO, SUBLANES, LANES), F32)] + [SDS((r8, d_model), F32)] * 3,
        scratch_shapes=[pltpu.VMEM((HALO + ta, LANES), F32), pltpu.VMEM((ta + HALO, LANES), F32),
                        pltpu.VMEM((N_DEV - 1, 3, r8, d_model), BF16), pltpu.VMEM((3, r8, d_model), F32),
                        pltpu.SemaphoreType.DMA((3, 7)), pltpu.SemaphoreType.DMA((3, 7)),
                        pltpu.SemaphoreType.DMA((3,))],
        input_output_aliases={0: 0},
        compiler_params=_params(1),
    )(dproj, proj, proj, proj, proj, d_h1, d_h1, conv_w_full, gp_bf, gp_f32)


def _weight_grads(lhs3, rhs3):
    n_mat, n_tok, d_model = lhs3.shape
    tk = _tile(n_tok, 2048, 16)
    n_k = n_tok // tk

    def body(a_ref, g_ref, o_ref, ob_ref):
        part = _dot_ta(a_ref[...], g_ref[...])

        @pl.when(pl.program_id(1) == 0)
        def _():
            o_ref[...] = part

        @pl.when(pl.program_id(1) != 0)
        def _():
            o_ref[...] += part

        @pl.when(pl.program_id(1) == n_k - 1)
        def _():
            ob_ref[...] = o_ref[...].astype(BF16)

    tile = pl.BlockSpec((None, tk, d_model), lambda a, i: (a, i, 0))
    out = pl.BlockSpec((None, d_model, d_model), lambda a, i: (a, 0, 0))
    return pl.pallas_call(
        body, name="grad_w_pa_pb_o", grid=(n_mat, n_k), in_specs=[tile, tile], out_specs=[out, out],
        out_shape=[SDS((n_mat, d_model, d_model), F32), SDS((n_mat, d_model, d_model), BF16)],
        compiler_params=_params(2),
    )(lhs3, rhs3)


def _packed_rows(pieces, plan, d_model):
    rows = {"row": lambda a: d_model // LANES, "block": lambda a: a.shape[0],
            "tiles": lambda a: a.shape[0] * a.shape[1]}
    return sum(rows[kind](pieces[p]) for kind, p, _ in plan)


def _grad_w_in_reduce_scatter(xt_bf, dproj, blk_order, pieces, plan):
    n_tok = dproj.shape[0]
    d_model = xt_bf.shape[0]
    dh = d_model // 2
    n_units = 2 * N_DEV
    n_pc = len(pieces)
    n_packed = _packed_rows(pieces, plan, d_model)
    n_part = n_packed + -n_packed % (N_DEV * SUBLANES)
    rsl = n_part // N_DEV

    xt_cols = n_tok // XT_CHUNKS

    def body(ord_ref, a_hbm, g_ref, *rest):
        pc_refs, rest = rest[:n_pc], rest[n_pc:]
        (o_ref, small_ref, acc, fb, sbuf, gbuf, tbuf, rfin, rbuf_s, red, p_ref, a_ref,
         send_f, send_s, recv_g, recv_t, recv_f, out_sems, send1, recv1, send2, recv2, xt_sems) = rest
        del ord_ref
        u = pl.program_id(0)
        s = u // 2
        hf = u % 2
        rnd = s // 2
        x, y, c = _mesh_pos()
        sibling = (x, y, 1 - c)
        me = _block_of((x, y, c))
        peers = _peers()

        def fetch_xt(q):
            cols = pl.ds(q * xt_cols, xt_cols)
            return pltpu.make_async_copy(a_hbm.at[:, cols], a_ref.at[:, cols], xt_sems.at[q])

        def rows_of(blk):
            return pl.ds(pl.multiple_of(blk * rsl, SUBLANES), rsl)

        def scatter():
            return [pltpu.make_async_remote_copy(
                src_ref=p_ref.at[rows_of(blk), :], dst_ref=rbuf_s.at[k], send_sem=send1.at[k],
                recv_sem=recv1.at[k], device_id=pos, device_id_type=MESH) for k, (pos, blk) in enumerate(peers)]

        def gather(dst_block=None):
            return [pltpu.make_async_remote_copy(
                src_ref=red, dst_ref=small_ref.at[rows_of(me if dst_block is None else blk), :],
                send_sem=send2.at[k], recv_sem=recv2.at[k], device_id=pos, device_id_type=MESH)
                for k, (pos, blk) in enumerate(peers)]

        def own_slice():
            return pltpu.make_async_copy(red, small_ref.at[rows_of(me), :], out_sems.at[2])

        def pack():
            at = 0
            for kind, p, r in plan:
                ref = pc_refs[p]
                if kind == "row":
                    for q in range(d_model // LANES):
                        p_ref[at + q:at + q + 1, :] = ref[r:r + 1, q * LANES:(q + 1) * LANES]
                    at += d_model // LANES
                elif kind == "block":
                    p_ref[at:at + ref.shape[0], :] = ref[...]
                    at += ref.shape[0]
                else:
                    for tile in range(ref.shape[0] * ref.shape[1]):
                        p_ref[at + tile:at + tile + 1, :] = _colsum(ref[tile // ref.shape[1], tile % ref.shape[1]])
                    at += ref.shape[0] * ref.shape[1]
            if at < n_part:
                p_ref[at:n_part, :] = jnp.zeros((n_part - at, LANES), F32)

        @pl.when(u == 0)
        def _():
            for q in range(XT_CHUNKS):
                fetch_xt(q).start()
            pack()
            for cp in scatter():
                cp.start()

        @pl.when(u == 8)
        def _():
            for cp in scatter():
                cp.wait_recv()
            tot = p_ref[rows_of(me), :]
            for k in range(N_DEV - 1):
                tot = tot + rbuf_s[k]
            red[...] = tot
            own_slice().start()
            for cp in gather():
                cp.start()

        n1 = (jnp.bitwise_xor(x, c), jnp.bitwise_xor(y, 1 - c), c)
        n2 = (jnp.bitwise_xor(x, 1 - c), jnp.bitwise_xor(y, c), c)

        def feed(r, half):
            return pltpu.make_async_remote_copy(
                src_ref=fb.at[half], dst_ref=gbuf.at[r, half], send_sem=send_f.at[r, half],
                recv_sem=recv_g.at[r, half], device_id=sibling, device_id_type=MESH)

        def feed_sibling(half):
            return pltpu.make_async_remote_copy(
                src_ref=fb.at[half], dst_ref=rfin.at[0, half], send_sem=send_f.at[3, half],
                recv_sem=recv_f.at[0, half], device_id=sibling, device_id_type=MESH)

        def chip_sum(r, half):
            dst = [tbuf.at[half], rfin.at[2, half], rfin.at[1, half]][r]
            sem = [recv_t.at[half], recv_f.at[2, half], recv_f.at[1, half]][r]
            return pltpu.make_async_remote_copy(
                src_ref=sbuf.at[r, half], dst_ref=dst, send_sem=send_s.at[r, half], recv_sem=sem,
                device_id=[n2, n2, n1][r], device_id_type=MESH)

        def out_copy(half):
            return pltpu.make_async_copy(acc.at[half], o_ref.at[:, pl.ds(half * dh, dh)], out_sems.at[half])

        def partial_sum(first=False):
            if not first:
                return _dot(a_ref[...], g_ref[...])
            tot = None
            for q in range(XT_CHUNKS):
                fetch_xt(q).wait()
                part = _dot(a_ref[:, q * xt_cols:(q + 1) * xt_cols], g_ref[q * xt_cols:(q + 1) * xt_cols, :])
                tot = part if tot is None else tot + part
            return tot

        for half in range(2):
            for r in range(3):
                @pl.when(u == 4 * r + 4 + half)
                def _(r=r, half=half):
                    feed(r, half).wait_send()

                @pl.when(u == 4 * r + 2 + half)
                def _(r=r, half=half):
                    feed(r, half).wait_recv()
                    if r == 2:
                        chip_sum(0, half).wait_recv()

            @pl.when(u == 14 + half)
            def _(half=half):
                feed_sibling(half).wait_recv()
                chip_sum(2, half).wait_recv()
                chip_sum(1, half).wait_recv()

        @pl.when(u == 0)
        def _():
            fb[0] = partial_sum(first=True).astype(BF16)

        @pl.when(jnp.logical_and(jnp.logical_and(s % 2 == 0, s < 7), u > 0))
        def _():
            fb[hf] = partial_sum().astype(BF16)

        @pl.when(jnp.logical_or(s == 1, s == 3))
        def _():
            sbuf[rnd, hf] = (partial_sum() + gbuf[rnd, hf].astype(F32)).astype(BF16)

        @pl.when(s == 5)
        def _():
            sbuf[2, hf] = (partial_sum() + gbuf[2, hf].astype(F32) + tbuf[hf].astype(F32)).astype(BF16)

        @pl.when(s == 7)
        def _():
            acc[hf] = (partial_sum() + rfin[0, hf].astype(F32) + rfin[1, hf].astype(F32)
                       + rfin[2, hf].astype(F32))

        for half in range(2):
            for r in range(3):
                @pl.when(u == 4 * r + half)
                def _(r=r, half=half):
                    feed(r, half).start()

                @pl.when(u == 4 * r + 2 + half)
                def _(r=r, half=half):
                    chip_sum(r, half).start()

            @pl.when(u == 12 + half)
            def _(half=half):
                feed_sibling(half).start()

        @pl.when(u == 14)
        def _():
            out_copy(0).start()

        @pl.when(u == 15)
        def _():
            out_copy(1).start()
            for half in range(2):
                feed_sibling(half).wait_send()
                for r in range(3):
                    chip_sum(r, half).wait_send()
                out_copy(half).wait()
            for cp in gather("theirs"):
                cp.wait_recv()
            for cp in scatter() + gather():
                cp.wait_send()
            own_slice().wait()

    grid_spec = pltpu.PrefetchScalarGridSpec(
        num_scalar_prefetch=1, grid=(n_units,),
        in_specs=[ANY, pl.BlockSpec((n_tok, dh), lambda u, o: (0, 2 * o[u // 2] + u % 2))] + [VMEM] * n_pc,
        out_specs=[ANY, ANY],
        scratch_shapes=[pltpu.VMEM((2, d_model, dh), F32), pltpu.VMEM((2, d_model, dh), BF16),
                        pltpu.VMEM((3, 2, d_model, dh), BF16), pltpu.VMEM((3, 2, d_model, dh), BF16),
                        pltpu.VMEM((2, d_model, dh), BF16), pltpu.VMEM((3, 2, d_model, dh), BF16),
                        pltpu.VMEM((N_DEV - 1, rsl, LANES), F32), pltpu.VMEM((rsl, LANES), F32),
                        pltpu.VMEM((n_part, LANES), F32), pltpu.VMEM((d_model, n_tok), BF16),
                        pltpu.SemaphoreType.DMA((4, 2)), pltpu.SemaphoreType.DMA((3, 2)),
                        pltpu.SemaphoreType.DMA((3, 2)), pltpu.SemaphoreType.DMA((2,)),
                        pltpu.SemaphoreType.DMA((3, 2)),
                        pltpu.SemaphoreType.DMA((3,))] + [pltpu.SemaphoreType.DMA((N_DEV - 1,))] * 4
        + [pltpu.SemaphoreType.DMA((XT_CHUNKS,))])
    return pl.pallas_call(
        body, name="grad_w_in_reduce_scatter", grid_spec=grid_spec,
        out_shape=[SDS((d_model, d_model), F32), SDS((n_part, LANES), F32)], compiler_params=_params(1),
    )(blk_order, xt_bf, dproj, *pieces)


def _grad_x_adamw(dproj, w_all, dr, w, g, m, v, small, loss_rows, packed, groups):
    n_tok, d_model = dr.shape
    tm = _tile(n_tok, 256, 16)
    n_steps = n_tok // tm
    tr = w.shape[0] // n_steps
    n_pk, n_grp = len(packed), len(groups)
    n_out = 4 * (n_pk + n_grp)

    def body(dp_hbm, w_hbm, dr_ref, ws_ref, g_ref, m_ref, v_ref, small_ref, *rest):
        pk_in, rest = rest[:3 * n_pk], rest[3 * n_pk:]
        grp_in, rest = rest[:4 * n_grp], rest[4 * n_grp:]
        o_ref, gp_ref, d_ref, mo_ref, vo_ref, loss_ref = rest[:6]
        outs, (w_ref, w_sems, dp_buf, dp_sems) = rest[6:6 + n_out], rest[6 + n_out:]
        i = pl.program_id(0)
        slot = i % DP_SLOTS

        def fetch(j):
            return pltpu.make_async_copy(w_hbm.at[j], w_ref.at[j], w_sems.at[j])

        def dp_copy(step, to):
            rows = pl.ds(pl.multiple_of(step * tm, tm), tm)
            return pltpu.make_async_copy(dp_hbm.at[rows, :], dp_buf.at[to], dp_sems.at[to])

        @pl.when(i == 0)
        def _():
            for ahead in range(DP_SLOTS - 1):
                dp_copy(ahead, ahead).start()

        @pl.when(i + DP_SLOTS - 1 < n_steps)
        def _():
            dp_copy(i + DP_SLOTS - 1, (i + DP_SLOTS - 1) % DP_SLOTS).start()

        dp_copy(i, slot).wait()

        def grad_x_tile(first):
            acc = DEEPNORM_ALPHA * dr_ref[...]
            for j in range(N_DEV):
                if first:
                    fetch(j).wait()
                acc = acc + _dot_tb(dp_buf[slot, :, j * d_model:(j + 1) * d_model], w_ref[j])
            o_ref[...] = acc

        def small_updates():
            def update(p, pw, grad, pm, pv):
                og, od, om, ov = outs[4 * p:4 * p + 4]
                og[...] = grad
                od[...], om[...], ov[...] = _adamw_math(pw[...], grad, pm[...], pv[...])

            for p, (pw_arr, _, _, row0) in enumerate(packed):
                pw, pm, pv = pk_in[3 * p:3 * p + 3]
                update(p, pw, small_ref[row0:row0 + pw_arr.shape[0], :], pm, pv)
            for p in range(n_grp):
                pw, pg, pm, pv = grp_in[4 * p:4 * p + 4]
                update(n_pk + p, pw, pg[...], pm, pv)
            sq = small_ref[loss_rows[0]:loss_rows[0] + loss_rows[1], :]
            loss_ref[...] = jnp.sum(_colsum(sq), axis=1, keepdims=True) * (0.5 / d_model)

        @pl.when(i == 0)
        def _():
            for j in range(N_DEV):
                fetch(j).start()
            small_updates()
            grad_x_tile(True)

        @pl.when(i != 0)
        def _():
            grad_x_tile(False)

        grad = g_ref[...]
        gp_ref[...] = grad
        d_ref[...], mo_ref[...], vo_ref[...] = _adamw_math(ws_ref[...], grad, m_ref[...], v_ref[...])

    tile = pl.BlockSpec((tm, d_model), lambda i: (i, 0))
    slab = pl.BlockSpec((tr, w.shape[1]), lambda i: (i, 0))
    flat = [a for pk in packed for a in pk[:3]] + [a for grp in groups for a in grp]
    shapes = [pk[0].shape for pk in packed] + [grp[0].shape for grp in groups]
    res = pl.pallas_call(
        body, name="grad_x_adamw", grid=(n_steps,),
        in_specs=[ANY, ANY, tile, slab, slab, slab, slab] + [VMEM] * (1 + len(flat)),
        out_specs=[tile, slab, slab, slab, slab, VMEM] + [VMEM] * n_out,
        out_shape=[SDS((n_tok, d_model), F32)] + [SDS(w.shape, F32)] * 4 + [SDS((1, 1), F32)]
        + [SDS(shape, F32) for shape in shapes for _ in range(4)],
        scratch_shapes=[pltpu.VMEM(w_all.shape, BF16), pltpu.SemaphoreType.DMA((N_DEV,)),
                        pltpu.VMEM((DP_SLOTS, tm, N_DEV * d_model), BF16), pltpu.SemaphoreType.DMA((DP_SLOTS,))],
        compiler_params=_params(1),
    )(dproj, w_all, dr, w, g, m, v, small, *flat)
    return (res[0], tuple(res[1:5]), [tuple(res[6 + 4 * p:10 + 4 * p]) for p in range(n_pk + n_grp)],
            res[5].reshape(()))


def _adamw_math(w, g, m, v):
    m = ADAM_B1 * m + (1.0 - ADAM_B1) * g
    v = ADAM_B2 * v + (1.0 - ADAM_B2) * (g * g)
    m_hat = m / (1.0 - ADAM_B1 ** ADAM_STEP)
    v_hat = v / (1.0 - ADAM_B2 ** ADAM_STEP)
    delta = -ADAM_LR * (m_hat / (jnp.sqrt(v_hat) + ADAM_EPS) + ADAM_WD * w)
    return delta, m, v


def _as_rows(a):
    return a.reshape(-1, LANES)


def kernel(x, w_in, b_in, conv_w, conv_b, gn_g, gn_b, ln_v_g, ln_v_b, w_spatial, b_spatial, w_pa, w_pb, w_o, b_o, ln_out_g, ln_out_b, loss_target, m_w_in, m_b_in, m_conv_w, m_conv_b, m_gn_g, m_gn_b, m_ln_v_g, m_ln_v_b, m_w_spatial, m_b_spatial, m_w_pa, m_w_pb, m_w_o, m_b_o, m_ln_out_g, m_ln_out_b, v_w_in, v_b_in, v_conv_w, v_conv_b, v_gn_g, v_gn_b, v_ln_v_g, v_ln_v_b, v_w_spatial, v_b_spatial, v_w_pa, v_w_pb, v_w_o, v_b_o, v_ln_out_g, v_ln_out_b):
    n_batch, seq, d_model = x.shape
    n_tok = n_batch * seq
    n_heads = d_model // LANES
    dc = conv_w.shape[1]
    me = 4 * lax.axis_index("x") + 2 * lax.axis_index("y") + lax.axis_index("c")
    row = lambda a: a.reshape(1, d_model)

    x2 = x.reshape(n_tok, d_model)
    target2 = loss_target.reshape(n_tok, d_model)
    b_spatial_t = b_spatial.T

    first = jnp.where(lax.axis_index("c") == 1, 4, 2)
    second = 6 - first
    ag_rel = jnp.stack([0 * first, 0 * first + 1, first, second + 1, second, first + 1, 0 * first + 6, 0 * first + 7])
    ag_blocks = jnp.bitwise_xor(me, ag_rel).astype(jnp.int32)
    proj, xt_bf, w_all, wp_all, cw_all = _proj_all_gather(
        x2, w_in, w_pa, w_pb, w_o, conv_w, b_in.reshape(N_DEV, 1, d_model), ag_blocks)
    wp_full = wp_all.reshape(3, d_model, d_model)
    conv_w_full = jnp.pad(cw_all.transpose(1, 0, 2).reshape(CONV_K, d_model), ((0, HALO - CONV_K), (0, 0)))

    h3, h1 = _branch_a_fwd(proj, conv_w_full, row(conv_b), row(gn_g), row(gn_b), seq)
    s = _branch_b_fwd(proj, row(ln_v_g), row(ln_v_b), w_spatial, b_spatial_t, seq)

    dproj, d_h3, d_s, dr, lhs3, rhs3, vec_mid = _mid(
        h3, s, proj, x2, target2, wp_full, row(b_o), row(ln_out_g), row(ln_out_b))

    dproj, vec_b, d_ws, d_bs_t = _branch_b_bwd(dproj, proj, d_s, row(ln_v_g), row(ln_v_b), w_spatial, b_spatial_t, seq)
    dproj, d_h1, vec_a1 = _branch_a_bwd_norm(dproj, proj, h1, d_h3, row(gn_g), row(gn_b), seq)
    gp_f32, gp_bf = _weight_grads(lhs3, rhs3)
    dproj, vec_a2, d_cw8, g_w_pa, g_w_pb, g_w_o = _branch_a_bwd_conv(
        dproj, proj, d_h1, conv_w_full, gp_bf, gp_f32, seq)

    pieces = [vec_a2, vec_a1, vec_b, vec_mid, d_bs_t.T, _as_rows(d_ws), d_cw8]
    a2, a1, vb, mid = 0, 1, 2, 3
    plan = ([("row", p, r) for p, r in [(a2, 0), (a2, 1), (a1, 0), (vb, 0), (vb, 1), (vb, 2), (mid, 3), (mid, 4)]]
            + [("row", p, r) for p, r in [(a1, 3), (a1, 1), (a1, 2), (vb, 3), (vb, 4), (mid, 2), (mid, 0), (mid, 1)]]
            + [("block", 4, None), ("block", 5, None), ("tiles", 6, None), ("row", mid, 5)])

    rs_rel = jnp.stack([0 * first + 7, 0 * first + 6, first + 1, second, second + 1, first, 0 * first + 1, 0 * first])
    rs_blocks = jnp.bitwise_xor(me, rs_rel).astype(jnp.int32)
    g_w_in, small = _grad_w_in_reduce_scatter(xt_bf, dproj, rs_blocks, pieces, plan)

    g_rows = d_model // LANES
    o0 = N_DEV * g_rows
    o1 = o0 + 8 * g_rows
    o2 = o1 + n_heads
    o3 = o2 + n_heads * LANES
    o4 = o3 + n_heads * HALO
    g_cw_full = small[o3:o4].reshape(n_heads, HALO, LANES).transpose(1, 0, 2).reshape(HALO, d_model)
    g_conv_w = lax.dynamic_slice(g_cw_full, (0, me * dc), (CONV_K, dc))

    two_d = lambda a: a.reshape(-1, a.shape[-1]) if a.ndim != 1 else (
        a.reshape(-1, LANES) if a.shape[0] % LANES == 0 else a.reshape(1, -1))
    names = ["b_in", "conv_w", "conv_b", "gn_g", "gn_b", "ln_v_g", "ln_v_b", "w_spatial", "b_spatial",
             "w_pa", "w_pb", "w_o", "b_o", "ln_out_g", "ln_out_b"]
    ws = dict(b_in=b_in, conv_w=conv_w, conv_b=conv_b, gn_g=gn_g, gn_b=gn_b, ln_v_g=ln_v_g, ln_v_b=ln_v_b,
              w_spatial=w_spatial, b_spatial=b_spatial, w_pa=w_pa, w_pb=w_pb, w_o=w_o, b_o=b_o,
              ln_out_g=ln_out_g, ln_out_b=ln_out_b)
    ms = dict(b_in=m_b_in, conv_w=m_conv_w, conv_b=m_conv_b, gn_g=m_gn_g, gn_b=m_gn_b, ln_v_g=m_ln_v_g,
              ln_v_b=m_ln_v_b, w_spatial=m_w_spatial, b_spatial=m_b_spatial, w_pa=m_w_pa, w_pb=m_w_pb,
              w_o=m_w_o, b_o=m_b_o, ln_out_g=m_ln_out_g, ln_out_b=m_ln_out_b)
    vs = dict(b_in=v_b_in, conv_w=v_conv_w, conv_b=v_conv_b, gn_g=v_gn_g, gn_b=v_gn_b, ln_v_g=v_ln_v_g,
              ln_v_b=v_ln_v_b, w_spatial=v_w_spatial, b_spatial=v_b_spatial, w_pa=v_w_pa, w_pb=v_w_pb,
              w_o=v_w_o, b_o=v_b_o, ln_out_g=v_ln_out_g, ln_out_b=v_ln_out_b)
    vec_names = ["conv_b", "gn_g", "gn_b", "ln_v_g", "ln_v_b", "b_o", "ln_out_g", "ln_out_b"]
    first_row = dict(b_in=0, b_spatial=o1, w_spatial=o2, **{n: o0 + a * g_rows for a, n in enumerate(vec_names)})
    given = dict(conv_w=g_conv_w, w_pa=g_w_pa, w_pb=g_w_pb, w_o=g_w_o)
    grad_x, upd_w_in, upd, loss = _grad_x_adamw(
        dproj, w_all, dr, w_in, g_w_in, m_w_in, v_w_in, small, (o4, g_rows),
        [(two_d(ws[n]), two_d(ms[n]), two_d(vs[n]), row0) for n, row0 in first_row.items()],
        [tuple(two_d(a) for a in (ws[n], grad, ms[n], vs[n])) for n, grad in given.items()])
    grad_x = grad_x.reshape(x.shape)
    res = {n: tuple(a.reshape(ws[n].shape) for a in u) for n, u in zip([*first_row, *given], upd)}
    res["w_in"] = upd_w_in

    order = ["w_in"] + names
    return (loss, grad_x, *[res[n][k] for k in range(4) for n in order])
```

```python
import jax
import jax.numpy as jnp
from jax import lax
from jax.experimental import pallas as pl
from jax.experimental.pallas import tpu as pltpu

F32 = jnp.float32
BF16 = jnp.bfloat16
SDS = jax.ShapeDtypeStruct

N_DEV = 8
LANES = 128
SUBLANES = 8
CONV_K = 31
HALO = 32
ELEMENTWISE_ROWS = 512
XT_CHUNKS = 4
LN_EPS = 1e-5
DEEPNORM_ALPHA = 2.0 ** 0.25
ADAM_LR, ADAM_B1, ADAM_B2, ADAM_EPS, ADAM_WD, ADAM_STEP = 0.001, 0.9, 0.999, 1e-08, 0.01, 10
GELU_C = 0.7978845608028654
GELU_A = 0.044715
VMEM_LIMIT = 56 * 1024 * 1024
MESH = pl.DeviceIdType.MESH
ANY = pl.BlockSpec(memory_space=pl.ANY)
VMEM = pl.BlockSpec(memory_space=pltpu.VMEM)


def _params(n_grid=0):
    sem = ("arbitrary",) * n_grid if n_grid else None
    return pltpu.CompilerParams(dimension_semantics=sem, vmem_limit_bytes=VMEM_LIMIT)


def _tile(n, pref, mult):
    t = min(n, pref)
    while n % t or t % mult:
        t -= 1
    return t


def _colsum(v):
    return jnp.sum(v, axis=0, keepdims=True)


def _sigmoid(v):
    return jax.nn.sigmoid(v)


def _silu_and_grad(v):
    s = _sigmoid(v)
    val = v * s
    return val, s + val * (1.0 - s)


def _gelu_and_grad(v):
    v2 = v * v
    sg = _sigmoid(v * (2.0 * GELU_C + (2.0 * GELU_C * GELU_A) * v2))
    grad = sg + v * sg * (1.0 - sg) * (2.0 * GELU_C + (6.0 * GELU_C * GELU_A) * v2)
    return v * sg, grad


def _tril_mask():
    r = lax.broadcasted_iota(jnp.int32, (LANES, LANES), 0)
    c = lax.broadcasted_iota(jnp.int32, (LANES, LANES), 1)
    return c <= r


def _dot(a, b):
    return jnp.dot(a, b, preferred_element_type=F32)


def _dot_tb(a, b):
    return lax.dot_general(a, b, (((1,), (1,)), ((), ())), preferred_element_type=F32)


def _dot_ta(a, b):
    return lax.dot_general(a, b, (((0,), (0,)), ((), ())), preferred_element_type=F32)


def _mesh_pos():
    return lax.axis_index("x"), lax.axis_index("y"), lax.axis_index("c")


def _block_of(pos):
    return 4 * pos[0] + 2 * pos[1] + pos[2]


def _peers():
    x, y, c = _mesh_pos()
    out = []
    for k in range(1, N_DEV):
        pos = (1 - x if k & 4 else x, 1 - y if k & 2 else y, 1 - c if k & 1 else c)
        out.append((pos, _block_of(pos)))
    return out


def _proj_all_gather(x2, w_in, w_pa, w_pb, w_o, conv_w, b_in3, blk_order):
    n_tok, d_model = x2.shape
    r8 = w_pa.shape[0]
    kc, dc = conv_w.shape
    tm = _tile(n_tok, 1024, LANES)
    n_t = n_tok // tm
    n_arr = 3

    def body(ord_ref, x_ref, b_ref, win_ref, wpa_ref, wpb_ref, wo_ref, cw_ref,
             proj_ref, xt_ref, wall_ref, wp_ref, cwall_ref,
             wbuf, xbuf, st_p, send_sems, recv_sems, local_sems, wall_sems):
        s = pl.program_id(0)
        t = pl.program_id(1)
        x, y, c = _mesh_pos()
        me = (x, y, c)
        sibling = (x, y, 1 - c)
        n1 = (jnp.bitwise_xor(x, c), jnp.bitwise_xor(y, 1 - c))
        n2 = (jnp.bitwise_xor(x, 1 - c), jnp.bitwise_xor(y, c))
        dg = (1 - x, 1 - y)
        outs = [wbuf, wp_ref, cwall_ref]
        srcs = [None, st_p, cw_ref]
        consumed = [me, sibling, (*n1, c), (*n2, 1 - c), (*n2, c), (*n1, 1 - c), (*dg, c), (*dg, 1 - c)]
        leaves = [(me, sibling), (me, (*n1, c)), (me, (*n2, c)), ((*n1, c), (*n2, c)),
                  ((*n1, c), sibling), ((*n2, c), sibling), ((*dg, c), sibling)]
        lands = [sibling, (*n1, c), (*n2, c), (*dg, c), (*n2, 1 - c), (*n1, 1 - c), (*dg, 1 - c)]

        def slot(o, pos):
            return o.at[:, _block_of(pos)] if o is wp_ref else o.at[_block_of(pos)]

        def copy(a, k, block, to, src=None):
            o = outs[a]
            return pltpu.make_async_remote_copy(
                src_ref=slot(o, block) if src is None else src, dst_ref=slot(o, block),
                send_sem=send_sems.at[a, k], recv_sem=recv_sems.at[a, k],
                device_id=to, device_id_type=MESH)

        def send(a, k):
            block, to = leaves[k]
            return copy(a, k, block, to, src=srcs[a] if k < 3 else None)

        def recv(a, k):
            return copy(a, k, lands[k], me)

        def local_copies():
            return [pltpu.make_async_copy(srcs[a], slot(outs[a], me), local_sems.at[a]) for a in (1, 2)]

        def to_hbm(step):
            return pltpu.make_async_copy(slot(wbuf, consumed[step]), slot(wall_ref, consumed[step]),
                                         wall_sems.at[step])

        def at_step(step):
            return pl.when(jnp.logical_and(s == step, t == 0))

        @at_step(0)
        def _():
            slot(wbuf, me)[...] = win_ref[...].astype(BF16)
            st_p[0] = wpa_ref[...].astype(BF16)
            st_p[1] = wpb_ref[...].astype(BF16)
            st_p[2] = wo_ref[...].astype(BF16)
            for a in range(n_arr):
                send(a, 0).start()
                send(a, 1).start()
            for cp in local_copies():
                cp.start()
            to_hbm(0).start()

        @at_step(1)
        def _():
            recv(0, 0).wait_recv()
            to_hbm(1).start()

        for rnd in range(3):
            @at_step(2 + 2 * rnd)
            def _(rnd=rnd):
                if rnd == 0:
                    for a in range(n_arr):
                        send(a, 2).start()
                recv(0, 1 + rnd).wait_recv()
                if rnd == 0:
                    send(0, 3).start()
                send(0, 4 + rnd).start()
                to_hbm(2 + 2 * rnd).start()

            @at_step(3 + 2 * rnd)
            def _(rnd=rnd):
                for a in (1, 2):
                    recv(a, 1 + rnd).wait_recv()
                    if rnd == 0:
                        send(a, 3).start()
                    send(a, 4 + rnd).start()
                recv(0, 4 + rnd).wait_recv()
                to_hbm(3 + 2 * rnd).start()

        rows = pl.ds(pl.multiple_of(t * tm, tm), tm)

        @pl.when(s == 0)
        def _():
            xb = x_ref[...].astype(BF16)
            xbuf[rows, :] = xb
            xt_ref[...] = xb.T

        proj_ref[...] = _dot(xbuf[rows, :], wbuf[ord_ref[s]]) + b_ref[...]

        @pl.when(jnp.logical_and(s == N_DEV - 1, t == n_t - 1))
        def _():
            for a in (1, 2):
                for k in (0, 4, 5, 6):
                    recv(a, k).wait_recv()
            for a in range(n_arr):
                for k in range(7):
                    send(a, k).wait_send()
            for cp in local_copies() + [to_hbm(step) for step in range(N_DEV)]:
                cp.wait()

    grid_spec = pltpu.PrefetchScalarGridSpec(
        num_scalar_prefetch=1, grid=(N_DEV, n_t),
        in_specs=[pl.BlockSpec((tm, d_model), lambda s, t, o: (jnp.where(s == 0, t, n_t - 1), 0)),
                  pl.BlockSpec((None, 1, d_model), lambda s, t, o: (o[s], 0, 0)),
                  VMEM, VMEM, VMEM, VMEM, VMEM],
        out_specs=[pl.BlockSpec((tm, d_model), lambda s, t, o: (t, o[s])),
                   pl.BlockSpec((d_model, tm), lambda s, t, o: (0, jnp.where(s == 0, t, n_t))),
                   ANY, ANY, ANY],
        scratch_shapes=[pltpu.VMEM((N_DEV, d_model, d_model), BF16), pltpu.VMEM((n_tok, d_model), BF16),
                        pltpu.VMEM((3, r8, d_model), BF16),
                        pltpu.SemaphoreType.DMA((n_arr, 7)), pltpu.SemaphoreType.DMA((n_arr, 7)),
                        pltpu.SemaphoreType.DMA((3,)), pltpu.SemaphoreType.DMA((N_DEV,))])
    return pl.pallas_call(
        body, name="proj_all_gather", grid_spec=grid_spec,
        out_shape=[SDS((n_tok, N_DEV * d_model), F32), SDS((d_model, (n_t + 1) * tm), BF16),
                   SDS((N_DEV, d_model, d_model), BF16), SDS((3, N_DEV, r8, d_model), BF16),
                   SDS((N_DEV, kc, dc), F32)],
        compiler_params=_params(2),
    )(blk_order, x2, b_in3, w_in, w_pa, w_pb, w_o, conv_w)


def _conv_rows(ta):
    return _tile(ta, 64, SUBLANES)


def _branch_a_fwd(proj, conv_w_full, conv_b, gn_g, gn_b, seq):
    n_tok = proj.shape[0]
    d_model = conv_b.shape[1]
    ta = _tile(seq, ELEMENTWISE_ROWS, HALO)
    per_seq = seq // ta
    rc = _conv_rows(ta)

    def body(av_ref, ag_ref, gt_ref, avh_ref, agh_ref, cw_ref, cb_ref, gg_ref, gb_ref,
             h3_ref, h1_ref, ext):
        keep = jnp.where(pl.program_id(0) % per_seq == 0, 0.0, 1.0)

        def group(g, carry):
            sl = pl.ds(pl.multiple_of(g * LANES, LANES), LANES)
            ext[0:HALO, :] = avh_ref[:, sl] * _sigmoid(agh_ref[:, sl]) * keep
            ext[HALO:HALO + ta, :] = av_ref[:, sl] * _sigmoid(ag_ref[:, sl])
            for r0 in range(0, ta, rc):
                acc = jnp.broadcast_to(cb_ref[:, sl], (rc, LANES))
                for k in range(CONV_K):
                    acc = acc + ext[pl.ds(r0 + HALO - (CONV_K - 1) + k, rc), :] * cw_ref[k:k + 1, sl]
                h1_ref[pl.ds(r0, rc), sl] = acc
            h1 = h1_ref[:, sl]
            mu = jnp.mean(h1, axis=-1, keepdims=True)
            dlt = h1 - mu
            var = jnp.mean(dlt * dlt, axis=-1, keepdims=True)
            h2 = dlt * lax.rsqrt(var + LN_EPS) * gg_ref[:, sl] + gb_ref[:, sl]
            gate = gt_ref[:, sl]
            h3_ref[:, sl] = (h2 * _sigmoid(h2) * gate * _sigmoid(gate)).astype(BF16)
            return carry

        lax.fori_loop(0, d_model // LANES, group, 0, unroll=True)

    blk = lambda j: pl.BlockSpec((ta, d_model), lambda i: (i, j))
    halo = lambda j: pl.BlockSpec((HALO, d_model), lambda i: (jnp.maximum(i * (ta // HALO) - 1, 0), j))
    row = pl.BlockSpec((1, d_model), lambda i: (0, 0))
    return pl.pallas_call(
        body, name="branch_a_fwd", grid=(n_tok // ta,),
        in_specs=[blk(0), blk(1), blk(2), halo(0), halo(1),
                  pl.BlockSpec((HALO, d_model), lambda i: (0, 0)), row, row, row],
        out_specs=[pl.BlockSpec((ta, d_model), lambda i: (i, 0))] * 2,
        out_shape=[SDS((n_tok, d_model), BF16), SDS((n_tok, d_model), F32)],
        scratch_shapes=[pltpu.VMEM((HALO + ta, LANES), F32)],
        compiler_params=_params(1),
    )(proj, proj, proj, proj, proj, conv_w_full, conv_b, gn_g, gn_b)


def _branch_b_fwd(proj, ln_g, ln_b, w_spatial, b_spatial_t, seq):
    n_tok = proj.shape[0]
    d_model = ln_g.shape[1]
    n_heads = d_model // LANES
    tb = _tile(seq, ELEMENTWISE_ROWS, LANES)

    def body(u_ref, v_ref, bg_ref, lg_ref, lb_ref, ws_ref, bs_ref, s_ref, vn_buf):
        v, _ = _gelu_and_grad(v_ref[...])
        mu = jnp.mean(v, axis=-1, keepdims=True)
        dlt = v - mu
        var = jnp.mean(dlt * dlt, axis=-1, keepdims=True)
        vn_buf[...] = (dlt * lax.rsqrt(var + LN_EPS) * lg_ref[...] + lb_ref[...]).astype(BF16)
        tril = _tril_mask()
        for h in range(n_heads):
            cols = slice(h * LANES, (h + 1) * LANES)
            w_h = jnp.where(tril, ws_ref[h], 0.0).astype(BF16)
            bias = bs_ref[:, h:h + 1]
            for ch in range(tb // LANES):
                rows = slice(ch * LANES, (ch + 1) * LANES)
                mix = _dot(w_h, vn_buf[rows, cols]) + bias
                u, _ = _gelu_and_grad(u_ref[rows, cols])
                gate = bg_ref[rows, cols]
                s_ref[rows, cols] = (u * mix * gate * _sigmoid(gate)).astype(BF16)

    blk = lambda j: pl.BlockSpec((tb, d_model), lambda i: (i, j))
    row = pl.BlockSpec((1, d_model), lambda i: (0, 0))
    return pl.pallas_call(
        body, name="branch_b_fwd", grid=(n_tok // tb,),
        in_specs=[blk(3), blk(4), blk(5), row, row,
                  pl.BlockSpec((n_heads, LANES, LANES), lambda i: (0, 0, 0)),
                  pl.BlockSpec((LANES, n_heads), lambda i: (0, 0))],
        out_specs=pl.BlockSpec((tb, d_model), lambda i: (i, 0)),
        out_shape=SDS((n_tok, d_model), BF16),
        scratch_shapes=[pltpu.VMEM((tb, d_model), BF16)],
        compiler_params=_params(1),
    )(proj, proj, proj, ln_g, ln_b, w_spatial, b_spatial_t)


MID_ROWS = 8


def _mid(h3, s, proj, x2, target, wp_full, b_o, lo_g, lo_b):
    n_tok, d_model = x2.shape
    tm = _tile(n_tok, 256, 16)

    def body(h3_ref, s_ref, ma_ref, mb_ref, x_ref, t_ref, wpa_ref, wpb_ref, wo_ref, bo_ref,
             lg_ref, lb_ref, dproj_ref, dh3_ref, ds_ref, dr_ref, lhs3_ref, rhs3_ref, vec_ref):
        @pl.when(pl.program_id(0) == 0)
        def _():
            vec_ref[...] = jnp.zeros_like(vec_ref)

        h3 = h3_ref[...]
        s = s_ref[...]
        ya = _dot(h3, wpa_ref[...])
        yb = _dot(s, wpb_ref[...])
        ga = _sigmoid(ma_ref[...])
        gb = _sigmoid(mb_ref[...])
        mixed = (ga * ya + gb * yb).astype(BF16)
        lhs3_ref[0] = h3
        lhs3_ref[1] = s
        lhs3_ref[2] = mixed
        r = DEEPNORM_ALPHA * x_ref[...] + _dot(mixed, wo_ref[...]) + bo_ref[...]
        mu = jnp.mean(r, axis=-1, keepdims=True)
        dlt = r - mu
        rstd = lax.rsqrt(jnp.mean(dlt * dlt, axis=-1, keepdims=True) + LN_EPS)
        rhat = dlt * rstd
        diff = rhat * lg_ref[...] + lb_ref[...] - t_ref[...]
        dy = diff * (1.0 / d_model)
        vec_ref[0:1, :] += _colsum(dy * rhat)
        vec_ref[1:2, :] += _colsum(dy)
        vec_ref[5:6, :] += _colsum(diff * diff)
        drh = dy * lg_ref[...]
        dr = rstd * (drh - jnp.mean(drh, axis=-1, keepdims=True)
                     - rhat * jnp.mean(drh * rhat, axis=-1, keepdims=True))
        vec_ref[2:3, :] += _colsum(dr)
        dr_ref[...] = dr
        drb = dr.astype(BF16)
        rhs3_ref[2] = drb
        dmixed = _dot_tb(drb, wo_ref[...])
        dya_f = dmixed * ga
        dyb_f = dmixed * gb
        dma = dya_f * ya * (1.0 - ga)
        dmb = dyb_f * yb * (1.0 - gb)
        vec_ref[3:4, :] += _colsum(dma)
        vec_ref[4:5, :] += _colsum(dmb)
        dproj_ref[:, 0:d_model] = dma.astype(BF16)
        dproj_ref[:, d_model:2 * d_model] = dmb.astype(BF16)
        dya = dya_f.astype(BF16)
        dyb = dyb_f.astype(BF16)
        rhs3_ref[0] = dya
        rhs3_ref[1] = dyb
        dh3_ref[...] = _dot_tb(dya, wpa_ref[...])
        ds_ref[...] = _dot_tb(dyb, wpb_ref[...])

    tile = pl.BlockSpec((tm, d_model), lambda i: (i, 0))
    full = lambda a: pl.BlockSpec((None, d_model, d_model), lambda i: (a, 0, 0))
    row = pl.BlockSpec((1, d_model), lambda i: (0, 0))
    stack = pl.BlockSpec((3, tm, d_model), lambda i: (0, i, 0))
    bf3 = SDS((3, n_tok, d_model), BF16)
    f32 = SDS((n_tok, d_model), F32)
    return pl.pallas_call(
        body, name="mid", grid=(n_tok // tm,),
        in_specs=[tile, tile, pl.BlockSpec((tm, d_model), lambda i: (i, 6)),
                  pl.BlockSpec((tm, d_model), lambda i: (i, 7)), tile, tile, full(0), full(1), full(2),
                  row, row, row],
        out_specs=[pl.BlockSpec((tm, 2 * d_model), lambda i: (i, 3)), tile, tile, tile, stack, stack,
                   pl.BlockSpec((MID_ROWS, d_model), lambda i: (0, 0))],
        out_shape=[SDS((n_tok, N_DEV * d_model), BF16), f32, f32, f32, bf3, bf3,
                   SDS((MID_ROWS, d_model), F32)],
        compiler_params=_params(1),
    )(h3, s, proj, proj, x2, target, wp_full, wp_full, wp_full, b_o, lo_g, lo_b)


B_ROWS = 8


def _branch_b_bwd(dproj, proj, d_s, ln_g, ln_b, w_spatial, b_spatial_t, seq):
    n_tok = proj.shape[0]
    d_model = ln_g.shape[1]
    n_heads = d_model // LANES
    tb = _tile(seq, ELEMENTWISE_ROWS, LANES)

    def body(dproj_in, u_ref, v_ref, bg_ref, ds_ref, lg_ref, lb_ref, ws_ref, bs_ref,
             dproj_ref, vec_ref, dws_ref, dbs_ref, vn_buf, dv_buf):
        del dproj_in

        @pl.when(pl.program_id(0) == 0)
        def _():
            vec_ref[...] = jnp.zeros_like(vec_ref)
            dws_ref[...] = jnp.zeros_like(dws_ref)
            dbs_ref[...] = jnp.zeros_like(dbs_ref)

        v, dgelu_v = _gelu_and_grad(v_ref[...])
        mu = jnp.mean(v, axis=-1, keepdims=True)
        dlt = v - mu
        rstd = lax.rsqrt(jnp.mean(dlt * dlt, axis=-1, keepdims=True) + LN_EPS)
        vhat = dlt * rstd
        vn_buf[...] = (vhat * lg_ref[...] + lb_ref[...]).astype(BF16)
        tril = _tril_mask()
        for h in range(n_heads):
            cols = slice(h * LANES, (h + 1) * LANES)
            w_h = jnp.where(tril, ws_ref[h], 0.0).astype(BF16)
            bias = bs_ref[:, h:h + 1]
            for ch in range(tb // LANES):
                rows = slice(ch * LANES, (ch + 1) * LANES)
                vn = vn_buf[rows, cols]
                mix = _dot(w_h, vn) + bias
                u, dgelu_u = _gelu_and_grad(u_ref[rows, cols])
                sg, dsilu = _silu_and_grad(bg_ref[rows, cols])
                dsv = ds_ref[rows, cols]
                du = dsv * mix * sg * dgelu_u
                dbg = dsv * u * mix * dsilu
                dmix = dsv * u * sg
                dmix_bf = dmix.astype(BF16)
                dproj_ref[rows, cols] = du.astype(BF16)
                dproj_ref[rows, 2 * d_model + h * LANES:2 * d_model + (h + 1) * LANES] = dbg.astype(BF16)
                vec_ref[0:1, cols] += _colsum(du)
                vec_ref[2:3, cols] += _colsum(dbg)
                dbs_ref[:, h:h + 1] += jnp.sum(dmix, axis=1, keepdims=True)
                dws_ref[h] += jnp.where(tril, _dot_tb(dmix_bf, vn), 0.0)
                dv_buf[rows, cols] = _dot_ta(w_h, dmix_bf)
        dvn = dv_buf[...]
        vec_ref[3:4, :] += _colsum(dvn * vhat)
        vec_ref[4:5, :] += _colsum(dvn)
        dvh = dvn * lg_ref[...]
        dv = rstd * (dvh - jnp.mean(dvh, axis=-1, keepdims=True)
                     - vhat * jnp.mean(dvh * vhat, axis=-1, keepdims=True)) * dgelu_v
        vec_ref[1:2, :] += _colsum(dv)
        dproj_ref[:, d_model:2 * d_model] = dv.astype(BF16)

    blk = lambda j: pl.BlockSpec((tb, d_model), lambda i: (i, j))
    row = pl.BlockSpec((1, d_model), lambda i: (0, 0))
    return pl.pallas_call(
        body, name="branch_b_bwd", grid=(n_tok // tb,),
        in_specs=[ANY, blk(3), blk(4), blk(5), pl.BlockSpec((tb, d_model), lambda i: (i, 0)), row, row,
                  pl.BlockSpec((n_heads, LANES, LANES), lambda i: (0, 0, 0)),
                  pl.BlockSpec((LANES, n_heads), lambda i: (0, 0))],
        out_specs=[pl.BlockSpec((tb, 3 * d_model), lambda i: (i, 1)),
                   pl.BlockSpec((B_ROWS, d_model), lambda i: (0, 0)),
                   pl.BlockSpec((n_heads, LANES, LANES), lambda i: (0, 0, 0)),
                   pl.BlockSpec((LANES, n_heads), lambda i: (0, 0))],
        out_shape=[SDS(dproj.shape, BF16), SDS((B_ROWS, d_model), F32),
                   SDS((n_heads, LANES, LANES), F32), SDS((LANES, n_heads), F32)],
        scratch_shapes=[pltpu.VMEM((tb, d_model), BF16), pltpu.VMEM((tb, d_model), F32)],
        input_output_aliases={0: 0},
        compiler_params=_params(1),
    )(dproj, proj, proj, proj, d_s, ln_g, ln_b, w_spatial, b_spatial_t)


A1_ROWS = 8


def _branch_a_bwd_norm(dproj, proj, h1, d_h3, gn_g, gn_b, seq):
    n_tok = proj.shape[0]
    d_model = gn_g.shape[1]
    ta = _tile(seq, ELEMENTWISE_ROWS, 16)

    def body(dproj_in, gt_ref, h1_ref, dh3_ref, gg_ref, gb_ref, dproj_ref, dh1_ref, vec_ref):
        del dproj_in

        @pl.when(pl.program_id(0) == 0)
        def _():
            vec_ref[...] = jnp.zeros_like(vec_ref)

        def group(g, carry):
            sl = pl.ds(pl.multiple_of(g * LANES, LANES), LANES)
            h1 = h1_ref[:, sl]
            mu = jnp.mean(h1, axis=-1, keepdims=True)
            dlt = h1 - mu
            rstd = lax.rsqrt(jnp.mean(dlt * dlt, axis=-1, keepdims=True) + LN_EPS)
            nrm = dlt * rstd
            sw, dsw = _silu_and_grad(nrm * gg_ref[:, sl] + gb_ref[:, sl])
            sg, dsg = _silu_and_grad(gt_ref[:, sl])
            dh3 = dh3_ref[:, sl]
            dgate = dh3 * sw * dsg
            dproj_ref[:, sl] = dgate.astype(BF16)
            vec_ref[0:1, sl] += _colsum(dgate)
            dh2 = dh3 * sg * dsw
            vec_ref[1:2, sl] += _colsum(dh2 * nrm)
            vec_ref[2:3, sl] += _colsum(dh2)
            dn = dh2 * gg_ref[:, sl]
            dh1 = rstd * (dn - jnp.mean(dn, axis=-1, keepdims=True)
                          - nrm * jnp.mean(dn * nrm, axis=-1, keepdims=True))
            vec_ref[3:4, sl] += _colsum(dh1)
            dh1_ref[:, sl] = dh1
            return carry

        lax.fori_loop(0, d_model // LANES, group, 0, unroll=True)

    tile = pl.BlockSpec((ta, d_model), lambda i: (i, 0))
    row = pl.BlockSpec((1, d_model), lambda i: (0, 0))
    return pl.pallas_call(
        body, name="branch_a_bwd_norm", grid=(n_tok // ta,),
        in_specs=[ANY, pl.BlockSpec((ta, d_model), lambda i: (i, 2)), tile, tile, row, row],
        out_specs=[pl.BlockSpec((ta, d_model), lambda i: (i, 2)), tile,
                   pl.BlockSpec((A1_ROWS, d_model), lambda i: (0, 0))],
        out_shape=[SDS(dproj.shape, BF16), SDS((n_tok, d_model), F32), SDS((A1_ROWS, d_model), F32)],
        input_output_aliases={0: 0},
        compiler_params=_params(1),
    )(dproj, proj, h1, d_h3, gn_g, gn_b)


A2_ROWS = 8


def _branch_a_bwd_conv(dproj, proj, d_h1, conv_w_full, gp_bf, gp_f32, seq):
    n_tok = proj.shape[0]
    d_model = conv_w_full.shape[1]
    n_groups = d_model // LANES
    ta = _tile(seq, ELEMENTWISE_ROWS, HALO)
    n_tiles = n_tok // ta
    per_seq = seq // ta
    rc = _conv_rows(ta)
    last_halo = n_tok // HALO - 1
    r8 = d_model // N_DEV
    prow = _tile(r8, 32, 16)

    def body(dproj_in, av_ref, ag_ref, avh_ref, agh_ref, dh1_ref, dh1h_ref, cw_ref, gp_bf_ref, gp_f32_ref,
             dproj_ref, vec_ref, dcw_ref, opa_ref, opb_ref, opo_ref,
             ext_h0, ext_d, rbuf, own, send_sems, recv_sems, local_sems):
        del dproj_in
        i = pl.program_id(0)
        x, y, c = _mesh_pos()
        me = _block_of((x, y, c))
        peers = _peers()

        def sends():
            return [pltpu.make_async_remote_copy(
                src_ref=gp_bf_ref.at[a, pl.ds(pl.multiple_of(blk * r8, 16), r8), :], dst_ref=rbuf.at[k, a],
                send_sem=send_sems.at[a, k], recv_sem=recv_sems.at[a, k], device_id=pos, device_id_type=MESH)
                for k, (pos, blk) in enumerate(peers) for a in range(3)]

        def own_rows():
            return [pltpu.make_async_copy(gp_f32_ref.at[a, pl.ds(pl.multiple_of(me * r8, 8), r8), :],
                                          own.at[a], local_sems.at[a]) for a in range(3)]

        @pl.when(i == 0)
        def _():
            vec_ref[...] = jnp.zeros_like(vec_ref)
            dcw_ref[...] = jnp.zeros_like(dcw_ref)
            for cp in sends() + own_rows():
                cp.start()

        keep_past = jnp.where(i % per_seq == 0, 0.0, 1.0)
        keep_next = jnp.where(i % per_seq == per_seq - 1, 0.0, 1.0)

        def group(g, carry):
            sl = pl.ds(pl.multiple_of(g * LANES, LANES), LANES)
            av = av_ref[:, sl]
            sig = _sigmoid(ag_ref[:, sl])
            ext_h0[0:HALO, :] = avh_ref[:, sl] * _sigmoid(agh_ref[:, sl]) * keep_past
            ext_h0[HALO:HALO + ta, :] = av * sig
            ext_d[0:ta, :] = dh1_ref[:, sl]
            ext_d[ta:ta + HALO, :] = dh1h_ref[:, sl] * keep_next
            for r0 in range(0, ta, rc):
                dh1 = ext_d[pl.ds(r0, rc), :]
                acc = jnp.zeros((rc, LANES), F32)
                for k in range(CONV_K):
                    acc = acc + ext_d[pl.ds(r0 + CONV_K - 1 - k, rc), :] * cw_ref[k:k + 1, sl]
                    prod = dh1 * ext_h0[pl.ds(r0 + HALO - (CONV_K - 1) + k, rc), :]
                    dcw_ref[g, k] += jnp.sum(prod.reshape(rc // SUBLANES, SUBLANES, LANES), axis=0)
                rows = pl.ds(r0, rc)
                sig_r = sig[r0:r0 + rc]
                dav = acc * sig_r
                dag = dav * av[r0:r0 + rc] * (1.0 - sig_r)
                dproj_ref[rows, sl] = dav.astype(BF16)
                dproj_ref[rows, pl.ds(pl.multiple_of(d_model + g * LANES, LANES), LANES)] = dag.astype(BF16)
                vec_ref[0:1, sl] += _colsum(dav)
                vec_ref[1:2, sl] += _colsum(dag)
            return carry

        lax.fori_loop(0, n_groups, group, 0, unroll=True)

        @pl.when(i == n_tiles - 1)
        def _():
            for cp in own_rows():
                cp.wait()
            for cp in sends():
                cp.wait_recv()
            for a, o in enumerate([opa_ref, opb_ref, opo_ref]):
                for q in range(r8 // prow):
                    r = pl.ds(q * prow, prow)
                    tot = own[a, r, :]
                    for k in range(N_DEV - 1):
                        tot = tot + rbuf[k, a, r, :].astype(F32)
                    o[r, :] = tot
            for cp in sends():
                cp.wait_send()

    blk = lambda j: pl.BlockSpec((ta, d_model), lambda i: (i, j))
    halo = lambda j: pl.BlockSpec((HALO, d_model), lambda i: (jnp.maximum(i * (ta // HALO) - 1, 0), j))
    shard = pl.BlockSpec((r8, d_model), lambda i: (0, 0))
    return pl.pallas_call(
        body, name="branch_a_bwd_conv", grid=(n_tiles,),
        in_specs=[ANY, blk(0), blk(1), halo(0), halo(1), pl.BlockSpec((ta, d_model), lambda i: (i, 0)),
                  pl.BlockSpec((HALO, d_model), lambda i: (jnp.minimum((i + 1) * (ta // HALO), last_halo), 0)),
                  pl.BlockSpec((HALO, d_model), lambda i: (0, 0)), ANY, ANY],
        out_specs=[pl.BlockSpec((ta, 2 * d_model), lambda i: (i, 0)),
                   pl.BlockSpec((A2_ROWS, d_model), lambda i: (0, 0)),
                   pl.BlockSpec((n_groups, HALO, SUBLANES, LANES), lambda i: (0, 0, 0, 0)),
                   shard, shard, shard],
        out_shape=[SDS(dproj.shape, BF16), SDS((A2_ROWS, d_model), F32),
                   SDS((n_groups, HALO, SUBLANES, LANES), F32)] + [SDS((r8, d_model), F32)] * 3,
        scratch_shapes=[pltpu.VMEM((HALO + ta, LANES), F32), pltpu.VMEM((ta + HALO, LANES), F32),
                        pltpu.VMEM((N_DEV - 1, 3, r8, d_model), BF16), pltpu.VMEM((3, r8, d_model), F32),
                        pltpu.SemaphoreType.DMA((3, 7)), pltpu.SemaphoreType.DMA((3, 7)),
                        pltpu.SemaphoreType.DMA((3,))],
        input_output_aliases={0: 0},
        compiler_params=_params(1),
    )(dproj, proj, proj, proj, proj, d_h1, d_h1, conv_w_full, gp_bf, gp_f32)


def _weight_grads(lhs3, rhs3):
    n_mat, n_tok, d_model = lhs3.shape
    tk = _tile(n_tok, 2048, 16)
    n_k = n_tok // tk

    def body(a_ref, g_ref, o_ref, ob_ref):
        part = _dot_ta(a_ref[...], g_ref[...])

        @pl.when(pl.program_id(1) == 0)
        def _():
            o_ref[...] = part

        @pl.when(pl.program_id(1) != 0)
        def _():
            o_ref[...] += part

        @pl.when(pl.program_id(1) == n_k - 1)
        def _():
            ob_ref[...] = o_ref[...].astype(BF16)

    tile = pl.BlockSpec((None, tk, d_model), lambda a, i: (a, i, 0))
    out = pl.BlockSpec((None, d_model, d_model), lambda a, i: (a, 0, 0))
    return pl.pallas_call(
        body, name="grad_w_pa_pb_o", grid=(n_mat, n_k), in_specs=[tile, tile], out_specs=[out, out],
        out_shape=[SDS((n_mat, d_model, d_model), F32), SDS((n_mat, d_model, d_model), BF16)],
        compiler_params=_params(2),
    )(lhs3, rhs3)


def _packed_rows(pieces, plan, d_model):
    rows = {"row": lambda a: d_model // LANES, "block": lambda a: a.shape[0],
            "tiles": lambda a: a.shape[0] * a.shape[1]}
    return sum(rows[kind](pieces[p]) for kind, p, _ in plan)


def _grad_w_in_reduce_scatter(xt_bf, dproj, blk_order, pieces, plan):
    n_tok = dproj.shape[0]
    d_model = xt_bf.shape[0]
    dh = d_model // 2
    n_units = 2 * N_DEV
    n_pc = len(pieces)
    n_packed = _packed_rows(pieces, plan, d_model)
    n_part = n_packed + -n_packed % (N_DEV * SUBLANES)
    rsl = n_part // N_DEV

    xt_cols = n_tok // XT_CHUNKS

    def body(ord_ref, a_hbm, g_ref, *rest):
        pc_refs, rest = rest[:n_pc], rest[n_pc:]
        (o_ref, small_ref, acc, fb, sbuf, gbuf, tbuf, rfin, rbuf_s, red, p_ref, a_ref,
         send_f, send_s, recv_g, recv_t, recv_f, out_sems, send1, recv1, send2, recv2, xt_sems) = rest
        del ord_ref
        u = pl.program_id(0)
        s = u // 2
        hf = u % 2
        rnd = s // 2
        x, y, c = _mesh_pos()
        sibling = (x, y, 1 - c)
        me = _block_of((x, y, c))
        peers = _peers()

        def fetch_xt(q):
            cols = pl.ds(q * xt_cols, xt_cols)
            return pltpu.make_async_copy(a_hbm.at[:, cols], a_ref.at[:, cols], xt_sems.at[q])

        def rows_of(blk):
            return pl.ds(pl.multiple_of(blk * rsl, SUBLANES), rsl)

        def scatter():
            return [pltpu.make_async_remote_copy(
                src_ref=p_ref.at[rows_of(blk), :], dst_ref=rbuf_s.at[k], send_sem=send1.at[k],
                recv_sem=recv1.at[k], device_id=pos, device_id_type=MESH) for k, (pos, blk) in enumerate(peers)]

        def gather(dst_block=None):
            return [pltpu.make_async_remote_copy(
                src_ref=red, dst_ref=small_ref.at[rows_of(me if dst_block is None else blk), :],
                send_sem=send2.at[k], recv_sem=recv2.at[k], device_id=pos, device_id_type=MESH)
                for k, (pos, blk) in enumerate(peers)]

        def own_slice():
            return pltpu.make_async_copy(red, small_ref.at[rows_of(me), :], out_sems.at[2])

        def pack():
            at = 0
            for kind, p, r in plan:
                ref = pc_refs[p]
                if kind == "row":
                    for q in range(d_model // LANES):
                        p_ref[at + q:at + q + 1, :] = ref[r:r + 1, q * LANES:(q + 1) * LANES]
                    at += d_model // LANES
                elif kind == "block":
                    p_ref[at:at + ref.shape[0], :] = ref[...]
                    at += ref.shape[0]
                else:
                    for tile in range(ref.shape[0] * ref.shape[1]):
                        p_ref[at + tile:at + tile + 1, :] = _colsum(ref[tile // ref.shape[1], tile % ref.shape[1]])
                    at += ref.shape[0] * ref.shape[1]
            if at < n_part:
                p_ref[at:n_part, :] = jnp.zeros((n_part - at, LANES), F32)

        @pl.when(u == 0)
        def _():
            for q in range(XT_CHUNKS):
                fetch_xt(q).start(priority=q % 2)
            pack()
            for cp in scatter():
                cp.start()

        @pl.when(u == 8)
        def _():
            for cp in scatter():
                cp.wait_recv()
            tot = p_ref[rows_of(me), :]
            for k in range(N_DEV - 1):
                tot = tot + rbuf_s[k]
            red[...] = tot
            own_slice().start()
            for cp in gather():
                cp.start()

        n1 = (jnp.bitwise_xor(x, c), jnp.bitwise_xor(y, 1 - c), c)
        n2 = (jnp.bitwise_xor(x, 1 - c), jnp.bitwise_xor(y, c), c)

        def feed(r, half):
            return pltpu.make_async_remote_copy(
                src_ref=fb.at[half], dst_ref=gbuf.at[r, half], send_sem=send_f.at[r, half],
                recv_sem=recv_g.at[r, half], device_id=sibling, device_id_type=MESH)

        def feed_sibling(half):
            return pltpu.make_async_remote_copy(
                src_ref=fb.at[half], dst_ref=rfin.at[0, half], send_sem=send_f.at[3, half],
                recv_sem=recv_f.at[0, half], device_id=sibling, device_id_type=MESH)

        def chip_sum(r, half):
            dst = [tbuf.at[half], rfin.at[2, half], rfin.at[1, half]][r]
            sem = [recv_t.at[half], recv_f.at[2, half], recv_f.at[1, half]][r]
            return pltpu.make_async_remote_copy(
                src_ref=sbuf.at[r, half], dst_ref=dst, send_sem=send_s.at[r, half], recv_sem=sem,
                device_id=[n2, n2, n1][r], device_id_type=MESH)

        def out_copy(half):
            return pltpu.make_async_copy(acc.at[half], o_ref.at[:, pl.ds(half * dh, dh)], out_sems.at[half])

        def partial_sum(first=False):
            if not first:
                return _dot(a_ref[...], g_ref[...])
            tot = None
            for q in range(XT_CHUNKS):
                fetch_xt(q).wait()
                part = _dot(a_ref[:, q * xt_cols:(q + 1) * xt_cols], g_ref[q * xt_cols:(q + 1) * xt_cols, :])
                tot = part if tot is None else tot + part
            return tot

        for half in range(2):
            for r in range(3):
                @pl.when(u == 4 * r + 4 + half)
                def _(r=r, half=half):
                    feed(r, half).wait_send()

                @pl.when(u == 4 * r + 2 + half)
                def _(r=r, half=half):
                    feed(r, half).wait_recv()
                    if r == 2:
                        chip_sum(0, half).wait_recv()

            @pl.when(u == 14 + half)
            def _(half=half):
                feed_sibling(half).wait_recv()
                chip_sum(2, half).wait_recv()
                chip_sum(1, half).wait_recv()

        @pl.when(u == 0)
        def _():
            fb[0] = partial_sum(first=True).astype(BF16)

        @pl.when(jnp.logical_and(jnp.logical_and(s % 2 == 0, s < 7), u > 0))
        def _():
            fb[hf] = partial_sum().astype(BF16)

        @pl.when(jnp.logical_or(s == 1, s == 3))
        def _():
            sbuf[rnd, hf] = (partial_sum() + gbuf[rnd, hf].astype(F32)).astype(BF16)

        @pl.when(s == 5)
        def _():
            sbuf[2, hf] = (partial_sum() + gbuf[2, hf].astype(F32) + tbuf[hf].astype(F32)).astype(BF16)

        @pl.when(s == 7)
        def _():
            acc[hf] = (partial_sum() + rfin[0, hf].astype(F32) + rfin[1, hf].astype(F32)
                       + rfin[2, hf].astype(F32))

        for half in range(2):
            for r in range(3):
                @pl.when(u == 4 * r + half)
                def _(r=r, half=half):
                    feed(r, half).start()

                @pl.when(u == 4 * r + 2 + half)
                def _(r=r, half=half):
                    chip_sum(r, half).start()

            @pl.when(u == 12 + half)
            def _(half=half):
                feed_sibling(half).start()

        @pl.when(u == 14)
        def _():
            out_copy(0).start()

        @pl.when(u == 15)
        def _():
            out_copy(1).start()
            for half in range(2):
                feed_sibling(half).wait_send()
                for r in range(3):
                    chip_sum(r, half).wait_send()
                out_copy(half).wait()
            for cp in gather("theirs"):
                cp.wait_recv()
            for cp in scatter() + gather():
                cp.wait_send()
            own_slice().wait()

    grid_spec = pltpu.PrefetchScalarGridSpec(
        num_scalar_prefetch=1, grid=(n_units,),
        in_specs=[ANY, pl.BlockSpec((n_tok, dh), lambda u, o: (0, 2 * o[u // 2] + u % 2))] + [VMEM] * n_pc,
        out_specs=[ANY, ANY],
        scratch_shapes=[pltpu.VMEM((2, d_model, dh), F32), pltpu.VMEM((2, d_model, dh), BF16),
                        pltpu.VMEM((3, 2, d_model, dh), BF16), pltpu.VMEM((3, 2, d_model, dh), BF16),
                        pltpu.VMEM((2, d_model, dh), BF16), pltpu.VMEM((3, 2, d_model, dh), BF16),
                        pltpu.VMEM((N_DEV - 1, rsl, LANES), F32), pltpu.VMEM((rsl, LANES), F32),
                        pltpu.VMEM((n_part, LANES), F32), pltpu.VMEM((d_model, n_tok), BF16),
                        pltpu.SemaphoreType.DMA((4, 2)), pltpu.SemaphoreType.DMA((3, 2)),
                        pltpu.SemaphoreType.DMA((3, 2)), pltpu.SemaphoreType.DMA((2,)),
                        pltpu.SemaphoreType.DMA((3, 2)),
                        pltpu.SemaphoreType.DMA((3,))] + [pltpu.SemaphoreType.DMA((N_DEV - 1,))] * 4
        + [pltpu.SemaphoreType.DMA((XT_CHUNKS,))])
    return pl.pallas_call(
        body, name="grad_w_in_reduce_scatter", grid_spec=grid_spec,
        out_shape=[SDS((d_model, d_model), F32), SDS((n_part, LANES), F32)], compiler_params=_params(1),
    )(blk_order, xt_bf, dproj, *pieces)


def _grad_x_adamw(dproj, w_all, dr, w, g, m, v, small, loss_rows, packed, groups):
    n_tok, d_model = dr.shape
    tm = _tile(n_tok, 256, 16)
    n_steps = n_tok // tm
    tr = w.shape[0] // n_steps
    n_pk, n_grp = len(packed), len(groups)
    n_out = 4 * (n_pk + n_grp)

    def body(dp_ref, w_hbm, dr_ref, ws_ref, g_ref, m_ref, v_ref, small_ref, *rest):
        pk_in, rest = rest[:3 * n_pk], rest[3 * n_pk:]
        grp_in, rest = rest[:4 * n_grp], rest[4 * n_grp:]
        o_ref, gp_ref, d_ref, mo_ref, vo_ref, loss_ref = rest[:6]
        outs, (w_ref, w_sems) = rest[6:6 + n_out], rest[6 + n_out:]
        i = pl.program_id(0)

        def fetch(j):
            return pltpu.make_async_copy(w_hbm.at[j], w_ref.at[j], w_sems.at[j])

        def grad_x_tile(first):
            acc = DEEPNORM_ALPHA * dr_ref[...]
            for j in range(N_DEV):
                if first:
                    fetch(j).wait()
                acc = acc + _dot_tb(dp_ref[:, j * d_model:(j + 1) * d_model], w_ref[j])
            o_ref[...] = acc

        def small_updates():
            def update(p, pw, grad, pm, pv):
                og, od, om, ov = outs[4 * p:4 * p + 4]
                og[...] = grad
                od[...], om[...], ov[...] = _adamw_math(pw[...], grad, pm[...], pv[...])

            for p, (pw_arr, _, _, row0) in enumerate(packed):
                pw, pm, pv = pk_in[3 * p:3 * p + 3]
                update(p, pw, small_ref[row0:row0 + pw_arr.shape[0], :], pm, pv)
            for p in range(n_grp):
                pw, pg, pm, pv = grp_in[4 * p:4 * p + 4]
                update(n_pk + p, pw, pg[...], pm, pv)
            sq = small_ref[loss_rows[0]:loss_rows[0] + loss_rows[1], :]
            loss_ref[...] = jnp.sum(_colsum(sq), axis=1, keepdims=True) * (0.5 / d_model)

        @pl.when(i == 0)
        def _():
            for j in range(N_DEV):
                fetch(j).start(priority=j % 2)
            small_updates()
            grad_x_tile(True)

        @pl.when(i != 0)
        def _():
            grad_x_tile(False)

        grad = g_ref[...]
        gp_ref[...] = grad
        d_ref[...], mo_ref[...], vo_ref[...] = _adamw_math(ws_ref[...], grad, m_ref[...], v_ref[...])

    tile = pl.BlockSpec((tm, d_model), lambda i: (i, 0))
    slab = pl.BlockSpec((tr, w.shape[1]), lambda i: (i, 0))
    flat = [a for pk in packed for a in pk[:3]] + [a for grp in groups for a in grp]
    shapes = [pk[0].shape for pk in packed] + [grp[0].shape for grp in groups]
    res = pl.pallas_call(
        body, name="grad_x_adamw", grid=(n_steps,),
        in_specs=[pl.BlockSpec((tm, N_DEV * d_model), lambda i: (i, 0)), ANY, tile, slab, slab, slab, slab]
        + [VMEM] * (1 + len(flat)),
        out_specs=[tile, slab, slab, slab, slab, VMEM] + [VMEM] * n_out,
        out_shape=[SDS((n_tok, d_model), F32)] + [SDS(w.shape, F32)] * 4 + [SDS((1, 1), F32)]
        + [SDS(shape, F32) for shape in shapes for _ in range(4)],
        scratch_shapes=[pltpu.VMEM(w_all.shape, BF16), pltpu.SemaphoreType.DMA((N_DEV,))],
        compiler_params=_params(1),
    )(dproj, w_all, dr, w, g, m, v, small, *flat)
    return (res[0], tuple(res[1:5]), [tuple(res[6 + 4 * p:10 + 4 * p]) for p in range(n_pk + n_grp)],
            res[5].reshape(()))


def _adamw_math(w, g, m, v):
    m = ADAM_B1 * m + (1.0 - ADAM_B1) * g
    v = ADAM_B2 * v + (1.0 - ADAM_B2) * (g * g)
    m_hat = m / (1.0 - ADAM_B1 ** ADAM_STEP)
    v_hat = v / (1.0 - ADAM_B2 ** ADAM_STEP)
    delta = -ADAM_LR * (m_hat / (jnp.sqrt(v_hat) + ADAM_EPS) + ADAM_WD * w)
    return delta, m, v


def _as_rows(a):
    return a.reshape(-1, LANES)


def kernel(x, w_in, b_in, conv_w, conv_b, gn_g, gn_b, ln_v_g, ln_v_b, w_spatial, b_spatial, w_pa, w_pb, w_o, b_o, ln_out_g, ln_out_b, loss_target, m_w_in, m_b_in, m_conv_w, m_conv_b, m_gn_g, m_gn_b, m_ln_v_g, m_ln_v_b, m_w_spatial, m_b_spatial, m_w_pa, m_w_pb, m_w_o, m_b_o, m_ln_out_g, m_ln_out_b, v_w_in, v_b_in, v_conv_w, v_conv_b, v_gn_g, v_gn_b, v_ln_v_g, v_ln_v_b, v_w_spatial, v_b_spatial, v_w_pa, v_w_pb, v_w_o, v_b_o, v_ln_out_g, v_ln_out_b):
    n_batch, seq, d_model = x.shape
    n_tok = n_batch * seq
    n_heads = d_model // LANES
    dc = conv_w.shape[1]
    me = 4 * lax.axis_index("x") + 2 * lax.axis_index("y") + lax.axis_index("c")
    row = lambda a: a.reshape(1, d_model)

    x2 = x.reshape(n_tok, d_model)
    target2 = loss_target.reshape(n_tok, d_model)
    b_spatial_t = b_spatial.T

    first = jnp.where(lax.axis_index("c") == 1, 4, 2)
    second = 6 - first
    ag_rel = jnp.stack([0 * first, 0 * first + 1, first, second + 1, second, first + 1, 0 * first + 6, 0 * first + 7])
    ag_blocks = jnp.bitwise_xor(me, ag_rel).astype(jnp.int32)
    proj, xt_bf, w_all, wp_all, cw_all = _proj_all_gather(
        x2, w_in, w_pa, w_pb, w_o, conv_w, b_in.reshape(N_DEV, 1, d_model), ag_blocks)
    wp_full = wp_all.reshape(3, d_model, d_model)
    conv_w_full = jnp.pad(cw_all.transpose(1, 0, 2).reshape(CONV_K, d_model), ((0, HALO - CONV_K), (0, 0)))

    h3, h1 = _branch_a_fwd(proj, conv_w_full, row(conv_b), row(gn_g), row(gn_b), seq)
    s = _branch_b_fwd(proj, row(ln_v_g), row(ln_v_b), w_spatial, b_spatial_t, seq)

    dproj, d_h3, d_s, dr, lhs3, rhs3, vec_mid = _mid(
        h3, s, proj, x2, target2, wp_full, row(b_o), row(ln_out_g), row(ln_out_b))

    dproj, vec_b, d_ws, d_bs_t = _branch_b_bwd(dproj, proj, d_s, row(ln_v_g), row(ln_v_b), w_spatial, b_spatial_t, seq)
    dproj, d_h1, vec_a1 = _branch_a_bwd_norm(dproj, proj, h1, d_h3, row(gn_g), row(gn_b), seq)
    gp_f32, gp_bf = _weight_grads(lhs3, rhs3)
    dproj, vec_a2, d_cw8, g_w_pa, g_w_pb, g_w_o = _branch_a_bwd_conv(
        dproj, proj, d_h1, conv_w_full, gp_bf, gp_f32, seq)

    pieces = [vec_a2, vec_a1, vec_b, vec_mid, d_bs_t.T, _as_rows(d_ws), d_cw8]
    a2, a1, vb, mid = 0, 1, 2, 3
    plan = ([("row", p, r) for p, r in [(a2, 0), (a2, 1), (a1, 0), (vb, 0), (vb, 1), (vb, 2), (mid, 3), (mid, 4)]]
            + [("row", p, r) for p, r in [(a1, 3), (a1, 1), (a1, 2), (vb, 3), (vb, 4), (mid, 2), (mid, 0), (mid, 1)]]
            + [("block", 4, None), ("block", 5, None), ("tiles", 6, None), ("row", mid, 5)])

    rs_rel = jnp.stack([0 * first + 7, 0 * first + 6, first + 1, second, second + 1, first, 0 * first + 1, 0 * first])
    rs_blocks = jnp.bitwise_xor(me, rs_rel).astype(jnp.int32)
    g_w_in, small = _grad_w_in_reduce_scatter(xt_bf, dproj, rs_blocks, pieces, plan)

    g_rows = d_model // LANES
    o0 = N_DEV * g_rows
    o1 = o0 + 8 * g_rows
    o2 = o1 + n_heads
    o3 = o2 + n_heads * LANES
    o4 = o3 + n_heads * HALO
    g_cw_full = small[o3:o4].reshape(n_heads, HALO, LANES).transpose(1, 0, 2).reshape(HALO, d_model)
    g_conv_w = lax.dynamic_slice(g_cw_full, (0, me * dc), (CONV_K, dc))

    two_d = lambda a: a.reshape(-1, a.shape[-1]) if a.ndim != 1 else (
        a.reshape(-1, LANES) if a.shape[0] % LANES == 0 else a.reshape(1, -1))
    names = ["b_in", "conv_w", "conv_b", "gn_g", "gn_b", "ln_v_g", "ln_v_b", "w_spatial", "b_spatial",
             "w_pa", "w_pb", "w_o", "b_o", "ln_out_g", "ln_out_b"]
    ws = dict(b_in=b_in, conv_w=conv_w, conv_b=conv_b, gn_g=gn_g, gn_b=gn_b, ln_v_g=ln_v_g, ln_v_b=ln_v_b,
              w_spatial=w_spatial, b_spatial=b_spatial, w_pa=w_pa, w_pb=w_pb, w_o=w_o, b_o=b_o,
              ln_out_g=ln_out_g, ln_out_b=ln_out_b)
    ms = dict(b_in=m_b_in, conv_w=m_conv_w, conv_b=m_conv_b, gn_g=m_gn_g, gn_b=m_gn_b, ln_v_g=m_ln_v_g,
              ln_v_b=m_ln_v_b, w_spatial=m_w_spatial, b_spatial=m_b_spatial, w_pa=m_w_pa, w_pb=m_w_pb,
              w_o=m_w_o, b_o=m_b_o, ln_out_g=m_ln_out_g, ln_out_b=m_ln_out_b)
    vs = dict(b_in=v_b_in, conv_w=v_conv_w, conv_b=v_conv_b, gn_g=v_gn_g, gn_b=v_gn_b, ln_v_g=v_ln_v_g,
              ln_v_b=v_ln_v_b, w_spatial=v_w_spatial, b_spatial=v_b_spatial, w_pa=v_w_pa, w_pb=v_w_pb,
              w_o=v_w_o, b_o=v_b_o, ln_out_g=v_ln_out_g, ln_out_b=v_ln_out_b)
    vec_names = ["conv_b", "gn_g", "gn_b", "ln_v_g", "ln_v_b", "b_o", "ln_out_g", "ln_out_b"]
    first_row = dict(b_in=0, b_spatial=o1, w_spatial=o2, **{n: o0 + a * g_rows for a, n in enumerate(vec_names)})
    given = dict(conv_w=g_conv_w, w_pa=g_w_pa, w_pb=g_w_pb, w_o=g_w_o)
    grad_x, upd_w_in, upd, loss = _grad_x_adamw(
        dproj, w_all, dr, w_in, g_w_in, m_w_in, v_w_in, small, (o4, g_rows),
        [(two_d(ws[n]), two_d(ms[n]), two_d(vs[n]), row0) for n, row0 in first_row.items()],
        [tuple(two_d(a) for a in (ws[n], grad, ms[n], vs[n])) for n, grad in given.items()])
    grad_x = grad_x.reshape(x.shape)
    res = {n: tuple(a.reshape(ws[n].shape) for a in u) for n, u in zip([*first_row, *given], upd)}
    res["w_in"] = upd_w_in

    order = ["w_in"] + names
    return (loss, grad_x, *[res[n][k] for k in range(4) for n in order])
```
